```python
import math
import jax, jax.numpy as jnp
from jax import lax
import numpy as np

D_MODEL = 1024
BATCH = 8
SEQ = 16384
DEPTH = 4

N_MIXERS = 3
N_FOX = (DEPTH + 2) // 3
N_MLA = (DEPTH + 1) // 3
N_RET = DEPTH // 3

N_META = 16
Q_BLOCK = 128
CHUNK = 128

FOX_HEAD_DIM = 128
FOX_HEADS = D_MODEL // FOX_HEAD_DIM
FOX_WIDTH = FOX_HEADS * FOX_HEAD_DIM
FOX_IN = 4 * FOX_WIDTH + FOX_HEADS
FORGET_BIAS_OFFSET = 2.0

MLA_NOPE = 128
MLA_ROPE = 64
MLA_V = 128
MLA_HEADS = D_MODEL // 128
MLA_Q_LORA = 384
MLA_KV_LORA = 256
MLA_WIDTH = MLA_HEADS * MLA_V
MLA_IN = MLA_Q_LORA + MLA_KV_LORA + MLA_ROPE + MLA_WIDTH
ROPE_BASE = 10000.0

RET_QK_DIM = 256
RET_V_DIM = 512
RET_HEADS = D_MODEL // RET_QK_DIM
RET_QK_WIDTH = RET_HEADS * RET_QK_DIM
RET_WIDTH = RET_HEADS * RET_V_DIM
RET_IN = 2 * RET_QK_WIDTH + 2 * RET_WIDTH

DEEPNORM_ALPHA = (2 * DEPTH) ** 0.25
DEEPNORM_BETA = (8 * DEPTH) ** -0.25
NORM_EPS = 1e-5
NEG_INF = -1e30

kernel_name = "hybrid_fox_mla_retention_deepnorm"


def layer_norm(x, g, b):
    xf = x.astype(jnp.float32)
    mu = jnp.mean(xf, axis=-1, keepdims=True)
    var = jnp.mean(jnp.square(xf - mu), axis=-1, keepdims=True)
    return ((xf - mu) * lax.rsqrt(var + NORM_EPS) * g + b).astype(x.dtype)


def rms_norm(x, g):
    xf = x.astype(jnp.float32)
    ms = jnp.mean(jnp.square(xf), axis=-1, keepdims=True)
    return (xf * lax.rsqrt(ms + NORM_EPS) * g).astype(x.dtype)


def rotary(t, pos, inv_freq):
    ang = pos.astype(jnp.float32)[:, None] * inv_freq[None, :]
    cos = jnp.cos(ang)[None, :, None, :]
    sin = jnp.sin(ang)[None, :, None, :]
    t1, t2 = jnp.split(t.astype(jnp.float32), 2, axis=-1)
    return jnp.concatenate([t1 * cos - t2 * sin, t2 * cos + t1 * sin], axis=-1).astype(t.dtype)


def causal_block_attention(q, k, v, scale, cum_log_f=None):
    B, L, H, dk = q.shape
    dv = v.shape[-1]
    n_real = L - N_META
    n_blocks = n_real // Q_BLOCK
    has_decay = cum_log_f is not None

    def attend(qb, q_pos, cq, kk, vv, ck):
        s = jnp.einsum('bqhd,bkhd->bhqk', qb, kk).astype(jnp.float32) * scale
        if has_decay:
            s = s + (jnp.moveaxis(cq, 1, 2)[..., :, None] - jnp.moveaxis(ck, 1, 2)[..., None, :])
        mask = jnp.arange(kk.shape[1])[None, :] <= q_pos[:, None]
        s = jnp.where(mask, s, NEG_INF)
        p = jax.nn.softmax(s, axis=-1).astype(vv.dtype)
        return jnp.einsum('bhqk,bkhd->bqhd', p, vv)

    c_meta = cum_log_f[:, :N_META] if has_decay else None
    out_meta = attend(q[:, :N_META], jnp.arange(N_META), c_meta,
                      k[:, :N_META], v[:, :N_META], c_meta)

    q_blocks = jnp.moveaxis(q[:, N_META:].reshape(B, n_blocks, Q_BLOCK, H, dk), 1, 0)
    q_pos_blocks = N_META + jnp.arange(n_blocks)[:, None] * Q_BLOCK + jnp.arange(Q_BLOCK)[None, :]
    if has_decay:
        c_blocks = jnp.moveaxis(cum_log_f[:, N_META:].reshape(B, n_blocks, Q_BLOCK, H), 1, 0)
        xs = (q_blocks, q_pos_blocks, c_blocks)
    else:
        xs = (q_blocks, q_pos_blocks)

    def block_fn(args):
        cq = args[2] if has_decay else None
        return attend(args[0], args[1], cq, k, v, cum_log_f)

    out_real = lax.map(block_fn, xs)
    out_real = jnp.moveaxis(out_real, 0, 1).reshape(B, n_real, H, dv)
    return jnp.concatenate([out_meta, out_real], axis=1)


def fox_mixer(h, w_in, b_f, w_out):
    B, L, _ = h.shape
    proj = h @ w_in
    q, k, v, z, f_logit = jnp.split(
        proj, [FOX_WIDTH, 2 * FOX_WIDTH, 3 * FOX_WIDTH, 4 * FOX_WIDTH], axis=-1)
    q = q.reshape(B, L, FOX_HEADS, FOX_HEAD_DIM)
    k = k.reshape(B, L, FOX_HEADS, FOX_HEAD_DIM)
    v = v.reshape(B, L, FOX_HEADS, FOX_HEAD_DIM)
    log_f = jax.nn.log_sigmoid((f_logit + b_f).astype(jnp.float32))
    cum_log_f = jnp.cumsum(log_f, axis=1)
    o = causal_block_attention(q, k, v, FOX_HEAD_DIM ** -0.5, cum_log_f)
    y = o.reshape(B, L, FOX_WIDTH) * jax.nn.silu(z)
    return y @ w_out


def mla_mixer(h, pos, w_in, q_norm_g, kv_norm_g, w_uq, w_ukv, w_out):
    B, L, _ = h.shape
    proj = h @ w_in
    c_q, c_kv, k_rope, z = jnp.split(
        proj, [MLA_Q_LORA, MLA_Q_LORA + MLA_KV_LORA, MLA_Q_LORA + MLA_KV_LORA + MLA_ROPE], axis=-1)
    q = (rms_norm(c_q, q_norm_g) @ w_uq).reshape(B, L, MLA_HEADS, MLA_NOPE + MLA_ROPE)
    kv = (rms_norm(c_kv, kv_norm_g) @ w_ukv).reshape(B, L, MLA_HEADS, MLA_NOPE + MLA_V)
    q_nope, q_rope = q[..., :MLA_NOPE], q[..., MLA_NOPE:]
    k_nope, v = kv[..., :MLA_NOPE], kv[..., MLA_NOPE:]
    inv_freq = ROPE_BASE ** (-jnp.arange(0, MLA_ROPE, 2, dtype=jnp.float32) / MLA_ROPE)
    q_rope = rotary(q_rope, pos, inv_freq)
    k_rope = rotary(k_rope[:, :, None, :], pos, inv_freq)
    q_full = jnp.concatenate([q_nope, q_rope], axis=-1)
    k_full = jnp.concatenate(
        [k_nope, jnp.broadcast_to(k_rope, (B, L, MLA_HEADS, MLA_ROPE))], axis=-1)
    o = causal_block_attention(q_full, k_full, v, (MLA_NOPE + MLA_ROPE) ** -0.5)
    y = o.reshape(B, L, MLA_WIDTH) * jax.nn.silu(z)
    return y @ w_out


def retention_chunk(qc, kc, vc, state, log_gamma):
    C = qc.shape[1]
    i = jnp.arange(C, dtype=jnp.float32)
    rel = i[:, None] - i[None, :]
    intra_decay = jnp.where(rel[None] >= 0, jnp.exp(rel[None] * log_gamma[:, None, None]), 0.0)
    qf, kf, vf = qc.astype(jnp.float32), kc.astype(jnp.float32), vc.astype(jnp.float32)
    s = jnp.einsum('bihd,bjhd->bhij', qf, kf) * intra_decay
    intra = jnp.einsum('bhij,bjhe->bihe', s, vf)
    q_decay = jnp.exp((i[:, None] + 1.0) * log_gamma[None, :])
    cross = jnp.einsum('bihd,bhde->bihe', qf, state) * q_decay[None, :, :, None]
    k_decay = jnp.exp((C - 1.0 - i)[:, None] * log_gamma[None, :])
    new_state = (jnp.exp(C * log_gamma)[None, :, None, None] * state
                 + jnp.einsum('bjhd,bjhe->bhde', kf * k_decay[None, :, :, None], vf))
    return intra + cross, new_state


def retention_mixer(h, pos, w_in, gn_g, w_out):
    B, L, _ = h.shape
    n_chunks = (L - N_META) // CHUNK
    proj = h @ w_in
    q, k, v, z = jnp.split(
        proj, [RET_QK_WIDTH, 2 * RET_QK_WIDTH, 2 * RET_QK_WIDTH + RET_WIDTH], axis=-1)
    q = q.reshape(B, L, RET_HEADS, RET_QK_DIM)
    k = k.reshape(B, L, RET_HEADS, RET_QK_DIM)
    v = v.reshape(B, L, RET_HEADS, RET_V_DIM)
    inv_freq = 1.0 / (ROPE_BASE ** jnp.linspace(0.0, 1.0, RET_QK_DIM // 2, dtype=jnp.float32))
    q = rotary(q, pos, inv_freq)
    k = rotary(k, pos, inv_freq) * RET_QK_DIM ** -0.5
    log_gamma = jnp.log1p(-jnp.exp2(-5.0 - jnp.arange(RET_HEADS, dtype=jnp.float32)))

    state0 = jnp.zeros((B, RET_HEADS, RET_QK_DIM, RET_V_DIM), jnp.float32)
    o_meta, state = retention_chunk(q[:, :N_META], k[:, :N_META], v[:, :N_META], state0, log_gamma)

    def to_chunks(t):
        return jnp.moveaxis(t[:, N_META:].reshape(B, n_chunks, CHUNK, RET_HEADS, t.shape[-1]), 1, 0)

    def step(carry, xs):
        o_c, new_carry = retention_chunk(xs[0], xs[1], xs[2], carry, log_gamma)
        return new_carry, o_c

    _, o_real = lax.scan(step, state, (to_chunks(q), to_chunks(k), to_chunks(v)))
    o_real = jnp.moveaxis(o_real, 0, 1).reshape(B, L - N_META, RET_HEADS, RET_V_DIM)
    o = jnp.concatenate([o_meta, o_real], axis=1)
    mu = jnp.mean(o, axis=-1, keepdims=True)
    var = jnp.mean(jnp.square(o - mu), axis=-1, keepdims=True)
    o = ((o - mu) * lax.rsqrt(var + NORM_EPS)).reshape(B, L, RET_WIDTH) * gn_g
    y = o.astype(h.dtype) * jax.nn.silu(z)
    return y @ w_out


def _fwd_setup_inputs(seed: int = 0) -> dict:
    key = jax.random.key(seed)
    ks = jax.random.split(key, 20)
    f32 = jnp.float32

    def w(k, shape, fan_in, scale=1.0):
        return jax.random.normal(k, shape, f32) * (fan_in ** -0.5) * scale

    x = jax.random.normal(ks[0], (BATCH, SEQ, D_MODEL), f32)
    meta = jax.random.normal(ks[1], (N_META, D_MODEL), f32)

    fox_w_in = w(ks[2], (N_FOX, D_MODEL, FOX_IN), D_MODEL)
    fox_b_f = FORGET_BIAS_OFFSET + 0.1 * jax.random.normal(ks[3], (N_FOX, FOX_HEADS), f32)
    fox_w_out = w(ks[4], (N_FOX, FOX_WIDTH, D_MODEL), FOX_WIDTH, DEEPNORM_BETA)

    mla_w_in = w(ks[5], (N_MLA, D_MODEL, MLA_IN), D_MODEL)
    mla_q_norm = 1.0 + 0.05 * jax.random.normal(ks[6], (N_MLA, MLA_Q_LORA), f32)
    mla_kv_norm = 1.0 + 0.05 * jax.random.normal(ks[7], (N_MLA, MLA_KV_LORA), f32)
    mla_w_uq = w(ks[8], (N_MLA, MLA_Q_LORA, MLA_HEADS * (MLA_NOPE + MLA_ROPE)), MLA_Q_LORA)
    mla_w_ukv = w(ks[9], (N_MLA, MLA_KV_LORA, MLA_HEADS * (MLA_NOPE + MLA_V)), MLA_KV_LORA)
    mla_w_out = w(ks[10], (N_MLA, MLA_WIDTH, D_MODEL), MLA_WIDTH, DEEPNORM_BETA)

    ret_w_in = w(ks[11], (N_RET, D_MODEL, RET_IN), D_MODEL)
    ret_gn_g = 1.0 + 0.05 * jax.random.normal(ks[12], (N_RET, RET_WIDTH), f32)
    ret_w_out = w(ks[13], (N_RET, RET_WIDTH, D_MODEL), RET_WIDTH, DEEPNORM_BETA)

    ln_g = 1.0 + 0.05 * jax.random.normal(ks[14], (DEPTH, D_MODEL), f32)
    ln_b = 0.02 * jax.random.normal(ks[15], (DEPTH, D_MODEL), f32)

    return {"x": x, "meta": meta,
            "fox_w_in": fox_w_in, "fox_b_f": fox_b_f, "fox_w_out": fox_w_out,
            "mla_w_in": mla_w_in, "mla_q_norm": mla_q_norm, "mla_kv_norm": mla_kv_norm,
            "mla_w_uq": mla_w_uq, "mla_w_ukv": mla_w_ukv, "mla_w_out": mla_w_out,
            "ret_w_in": ret_w_in, "ret_gn_g": ret_gn_g, "ret_w_out": ret_w_out,
            "ln_g": ln_g, "ln_b": ln_b}


def _fwd_reference(x, meta, fox_w_in, fox_b_f, fox_w_out, mla_w_in, mla_q_norm, mla_kv_norm,
              mla_w_uq, mla_w_ukv, mla_w_out, ret_w_in, ret_gn_g, ret_w_out, ln_g, ln_b):
    B = x.shape[0]
    meta_b = jnp.broadcast_to(meta[None].astype(x.dtype), (B, N_META, D_MODEL))
    h = jnp.concatenate([meta_b, x], axis=1)
    pos = jnp.arange(h.shape[1])
    for i in range(DEPTH):
        kind, j = i % N_MIXERS, i // N_MIXERS
        if kind == 0:
            y = fox_mixer(h, fox_w_in[j], fox_b_f[j], fox_w_out[j])
        elif kind == 1:
            y = mla_mixer(h, pos, mla_w_in[j], mla_q_norm[j], mla_kv_norm[j],
                          mla_w_uq[j], mla_w_ukv[j], mla_w_out[j])
        else:
            y = retention_mixer(h, pos, ret_w_in[j], ret_gn_g[j], ret_w_out[j])
        h = layer_norm(DEEPNORM_ALPHA * h + y, ln_g[i], ln_b[i])
    return h[:, N_META:]


import jax as _jax
import jax.numpy as _jnp

TWIN_FORMAT = 'train_step'
FWD_PARAMS = ['x', 'meta', 'fox_w_in', 'fox_b_f', 'fox_w_out', 'mla_w_in', 'mla_q_norm', 'mla_kv_norm', 'mla_w_uq', 'mla_w_ukv', 'mla_w_out', 'ret_w_in', 'ret_gn_g', 'ret_w_out', 'ln_g', 'ln_b']
TWIN_WEIGHTS = ['meta', 'fox_w_in', 'fox_b_f', 'fox_w_out', 'mla_w_in', 'mla_q_norm', 'mla_kv_norm', 'mla_w_uq', 'mla_w_ukv', 'mla_w_out', 'ret_w_in', 'ret_gn_g', 'ret_w_out', 'ln_g', 'ln_b']
TWIN_DIFF_INPUT = 'x'
TWIN_INPUTS = ['x', 'meta', 'fox_w_in', 'fox_b_f', 'fox_w_out', 'mla_w_in', 'mla_q_norm', 'mla_kv_norm', 'mla_w_uq', 'mla_w_ukv', 'mla_w_out', 'ret_w_in', 'ret_gn_g', 'ret_w_out', 'ln_g', 'ln_b', 'loss_target', 'm_meta', 'm_fox_w_in', 'm_fox_b_f', 'm_fox_w_out', 'm_mla_w_in', 'm_mla_q_norm', 'm_mla_kv_norm', 'm_mla_w_uq', 'm_mla_w_ukv', 'm_mla_w_out', 'm_ret_w_in', 'm_ret_gn_g', 'm_ret_w_out', 'm_ln_g', 'm_ln_b', 'v_meta', 'v_fox_w_in', 'v_fox_b_f', 'v_fox_w_out', 'v_mla_w_in', 'v_mla_q_norm', 'v_mla_kv_norm', 'v_mla_w_uq', 'v_mla_w_ukv', 'v_mla_w_out', 'v_ret_w_in', 'v_ret_gn_g', 'v_ret_w_out', 'v_ln_g', 'v_ln_b']
TWIN_OUTPUTS = ['loss', 'grad_x', 'grad_meta', 'grad_fox_w_in', 'grad_fox_b_f', 'grad_fox_w_out', 'grad_mla_w_in', 'grad_mla_q_norm', 'grad_mla_kv_norm', 'grad_mla_w_uq', 'grad_mla_w_ukv', 'grad_mla_w_out', 'grad_ret_w_in', 'grad_ret_gn_g', 'grad_ret_w_out', 'grad_ln_g', 'grad_ln_b', 'delta_meta', 'delta_fox_w_in', 'delta_fox_b_f', 'delta_fox_w_out', 'delta_mla_w_in', 'delta_mla_q_norm', 'delta_mla_kv_norm', 'delta_mla_w_uq', 'delta_mla_w_ukv', 'delta_mla_w_out', 'delta_ret_w_in', 'delta_ret_gn_g', 'delta_ret_w_out', 'delta_ln_g', 'delta_ln_b', 'new_m_meta', 'new_m_fox_w_in', 'new_m_fox_b_f', 'new_m_fox_w_out', 'new_m_mla_w_in', 'new_m_mla_q_norm', 'new_m_mla_kv_norm', 'new_m_mla_w_uq', 'new_m_mla_w_ukv', 'new_m_mla_w_out', 'new_m_ret_w_in', 'new_m_ret_gn_g', 'new_m_ret_w_out', 'new_m_ln_g', 'new_m_ln_b', 'new_v_meta', 'new_v_fox_w_in', 'new_v_fox_b_f', 'new_v_fox_w_out', 'new_v_mla_w_in', 'new_v_mla_q_norm', 'new_v_mla_kv_norm', 'new_v_mla_w_uq', 'new_v_mla_w_ukv', 'new_v_mla_w_out', 'new_v_ret_w_in', 'new_v_ret_gn_g', 'new_v_ret_w_out', 'new_v_ln_g', 'new_v_ln_b']
TWIN_LEAF_KINDS = {'loss': 'loss', 'grad_x': 'grad_x', 'grad_meta': 'grad_w', 'grad_fox_w_in': 'grad_w', 'grad_fox_b_f': 'grad_w', 'grad_fox_w_out': 'grad_w', 'grad_mla_w_in': 'grad_w', 'grad_mla_q_norm': 'grad_w', 'grad_mla_kv_norm': 'grad_w', 'grad_mla_w_uq': 'grad_w', 'grad_mla_w_ukv': 'grad_w', 'grad_mla_w_out': 'grad_w', 'grad_ret_w_in': 'grad_w', 'grad_ret_gn_g': 'grad_w', 'grad_ret_w_out': 'grad_w', 'grad_ln_g': 'grad_w', 'grad_ln_b': 'grad_w', 'delta_meta': 'delta_w', 'delta_fox_w_in': 'delta_w', 'delta_fox_b_f': 'delta_w', 'delta_fox_w_out': 'delta_w', 'delta_mla_w_in': 'delta_w', 'delta_mla_q_norm': 'delta_w', 'delta_mla_kv_norm': 'delta_w', 'delta_mla_w_uq': 'delta_w', 'delta_mla_w_ukv': 'delta_w', 'delta_mla_w_out': 'delta_w', 'delta_ret_w_in': 'delta_w', 'delta_ret_gn_g': 'delta_w', 'delta_ret_w_out': 'delta_w', 'delta_ln_g': 'delta_w', 'delta_ln_b': 'delta_w', 'new_m_meta': 'new_m', 'new_m_fox_w_in': 'new_m', 'new_m_fox_b_f': 'new_m', 'new_m_fox_w_out': 'new_m', 'new_m_mla_w_in': 'new_m', 'new_m_mla_q_norm': 'new_m', 'new_m_mla_kv_norm': 'new_m', 'new_m_mla_w_uq': 'new_m', 'new_m_mla_w_ukv': 'new_m', 'new_m_mla_w_out': 'new_m', 'new_m_ret_w_in': 'new_m', 'new_m_ret_gn_g': 'new_m', 'new_m_ret_w_out': 'new_m', 'new_m_ln_g': 'new_m', 'new_m_ln_b': 'new_m', 'new_v_meta': 'new_v', 'new_v_fox_w_in': 'new_v', 'new_v_fox_b_f': 'new_v', 'new_v_fox_w_out': 'new_v', 'new_v_mla_w_in': 'new_v', 'new_v_mla_q_norm': 'new_v', 'new_v_mla_kv_norm': 'new_v', 'new_v_mla_w_uq': 'new_v', 'new_v_mla_w_ukv': 'new_v', 'new_v_mla_w_out': 'new_v', 'new_v_ret_w_in': 'new_v', 'new_v_ret_gn_g': 'new_v', 'new_v_ret_w_out': 'new_v', 'new_v_ln_g': 'new_v', 'new_v_ln_b': 'new_v'}


def _forward(args):
    return _fwd_reference(*[args[k] for k in FWD_PARAMS])


def _output_shape():
    def fwd():
        inp = _fwd_setup_inputs(0)
        return _fwd_reference(*[inp[k] for k in FWD_PARAMS])
    out = _jax.eval_shape(fwd)
    return out.shape, out.dtype

N_MICROBATCH = 1
ADAM_LR = 0.001
ADAM_B1 = 0.9
ADAM_B2 = 0.999
ADAM_EPS = 1e-08
ADAM_WD = 0.01
ADAM_STEP = 10
PER_EXAMPLE_BATCH_AXIS = {'x': 0, 'loss_target': 0}
SHARED_INPUTS = []
_WEIGHT_DTYPES = {'meta': _jnp.float32, 'fox_w_in': _jnp.float32, 'fox_b_f': _jnp.float32, 'fox_w_out': _jnp.float32, 'mla_w_in': _jnp.float32, 'mla_q_norm': _jnp.float32, 'mla_kv_norm': _jnp.float32, 'mla_w_uq': _jnp.float32, 'mla_w_ukv': _jnp.float32, 'mla_w_out': _jnp.float32, 'ret_w_in': _jnp.float32, 'ret_gn_g': _jnp.float32, 'ret_w_out': _jnp.float32, 'ln_g': _jnp.float32, 'ln_b': _jnp.float32}
MOMENT_SCALE = {'meta': 5.948763e-03, 'fox_w_in': 2.002462e-02, 'fox_b_f': 1.678260e-01, 'fox_w_out': 5.239866e-02, 'mla_w_in': 1.437469e-02, 'mla_q_norm': 1.378173e-02, 'mla_kv_norm': 2.776606e-02, 'mla_w_uq': 7.292574e-03, 'mla_w_ukv': 8.766963e-03, 'mla_w_out': 2.349262e-02, 'ret_w_in': 4.388807e-02, 'ret_gn_g': 3.733391e-02, 'ret_w_out': 1.261803e-01, 'ln_g': 6.603059e+01, 'ln_b': 2.438298e+00}


def _to_microbatches(a, axis):
    t = _jnp.moveaxis(a, axis, 0)
    t = t.reshape((N_MICROBATCH, t.shape[0] // N_MICROBATCH) + t.shape[1:])
    return _jnp.moveaxis(t, 1, axis + 1)


def setup_inputs(seed: int = 0) -> dict:
    inp = _fwd_setup_inputs(seed)
    key = _jax.random.fold_in(_jax.random.key(seed), 7919)
    shape, _ = _output_shape()
    out = dict(inp)
    out["loss_target"] = _jax.random.normal(_jax.random.fold_in(key, 0), shape, _jnp.float32)
    for i, name in enumerate(TWIN_WEIGHTS):
        w = inp[name].astype(_jnp.float32)
        if MOMENT_SCALE is None:
            s = _jnp.sqrt(_jnp.mean(_jnp.square(w)) + 1e-30)
        else:
            s = MOMENT_SCALE[name]
        km, kv = _jax.random.split(_jax.random.fold_in(key, i + 1))
        out[name] = w
        out["m_" + name] = s * _jax.random.normal(km, w.shape, _jnp.float32)
        out["v_" + name] = (s * s) * _jax.random.uniform(kv, w.shape, _jnp.float32, 0.5, 1.5)
    if N_MICROBATCH > 1:
        for name, axis in PER_EXAMPLE_BATCH_AXIS.items():
            out[name] = _to_microbatches(out[name], axis)
    return {'x': out['x'], 'meta': out['meta'], 'fox_w_in': out['fox_w_in'], 'fox_b_f': out['fox_b_f'], 'fox_w_out': out['fox_w_out'], 'mla_w_in': out['mla_w_in'], 'mla_q_norm': out['mla_q_norm'], 'mla_kv_norm': out['mla_kv_norm'], 'mla_w_uq': out['mla_w_uq'], 'mla_w_ukv': out['mla_w_ukv'], 'mla_w_out': out['mla_w_out'], 'ret_w_in': out['ret_w_in'], 'ret_gn_g': out['ret_gn_g'], 'ret_w_out': out['ret_w_out'], 'ln_g': out['ln_g'], 'ln_b': out['ln_b'], 'loss_target': out['loss_target'], 'm_meta': out['m_meta'], 'm_fox_w_in': out['m_fox_w_in'], 'm_fox_b_f': out['m_fox_b_f'], 'm_fox_w_out': out['m_fox_w_out'], 'm_mla_w_in': out['m_mla_w_in'], 'm_mla_q_norm': out['m_mla_q_norm'], 'm_mla_kv_norm': out['m_mla_kv_norm'], 'm_mla_w_uq': out['m_mla_w_uq'], 'm_mla_w_ukv': out['m_mla_w_ukv'], 'm_mla_w_out': out['m_mla_w_out'], 'm_ret_w_in': out['m_ret_w_in'], 'm_ret_gn_g': out['m_ret_gn_g'], 'm_ret_w_out': out['m_ret_w_out'], 'm_ln_g': out['m_ln_g'], 'm_ln_b': out['m_ln_b'], 'v_meta': out['v_meta'], 'v_fox_w_in': out['v_fox_w_in'], 'v_fox_b_f': out['v_fox_b_f'], 'v_fox_w_out': out['v_fox_w_out'], 'v_mla_w_in': out['v_mla_w_in'], 'v_mla_q_norm': out['v_mla_q_norm'], 'v_mla_kv_norm': out['v_mla_kv_norm'], 'v_mla_w_uq': out['v_mla_w_uq'], 'v_mla_w_ukv': out['v_mla_w_ukv'], 'v_mla_w_out': out['v_mla_w_out'], 'v_ret_w_in': out['v_ret_w_in'], 'v_ret_gn_g': out['v_ret_gn_g'], 'v_ret_w_out': out['v_ret_w_out'], 'v_ln_g': out['v_ln_g'], 'v_ln_b': out['v_ln_b']}


def _loss(weights, diff, rest, loss_target):
    with _jax.named_scope("forward"):
        args = {**rest, TWIN_DIFF_INPUT: diff, **{k: w.astype(_WEIGHT_DTYPES[k]) for k, w in weights.items()}}
        y = _forward(args)
    with _jax.named_scope("loss_head"):
        err = _jnp.square(y.astype(_jnp.float32) - loss_target)
        return 0.5 * _jnp.sum(_jnp.mean(err, axis=-1)) if err.ndim else 0.5 * err


def _adamw(w, g, m, v):
    m = ADAM_B1 * m + (1.0 - ADAM_B1) * g
    v = ADAM_B2 * v + (1.0 - ADAM_B2) * _jnp.square(g)
    m_hat = m / (1.0 - ADAM_B1 ** ADAM_STEP)
    v_hat = v / (1.0 - ADAM_B2 ** ADAM_STEP)
    delta = -ADAM_LR * (m_hat / (_jnp.sqrt(v_hat) + ADAM_EPS) + ADAM_WD * w)
    return delta, m, v


def reference(x, meta, fox_w_in, fox_b_f, fox_w_out, mla_w_in, mla_q_norm, mla_kv_norm, mla_w_uq, mla_w_ukv, mla_w_out, ret_w_in, ret_gn_g, ret_w_out, ln_g, ln_b, loss_target, m_meta, m_fox_w_in, m_fox_b_f, m_fox_w_out, m_mla_w_in, m_mla_q_norm, m_mla_kv_norm, m_mla_w_uq, m_mla_w_ukv, m_mla_w_out, m_ret_w_in, m_ret_gn_g, m_ret_w_out, m_ln_g, m_ln_b, v_meta, v_fox_w_in, v_fox_b_f, v_fox_w_out, v_mla_w_in, v_mla_q_norm, v_mla_kv_norm, v_mla_w_uq, v_mla_w_ukv, v_mla_w_out, v_ret_w_in, v_ret_gn_g, v_ret_w_out, v_ln_g, v_ln_b):
    given = dict(x=x, meta=meta, fox_w_in=fox_w_in, fox_b_f=fox_b_f, fox_w_out=fox_w_out, mla_w_in=mla_w_in, mla_q_norm=mla_q_norm, mla_kv_norm=mla_kv_norm, mla_w_uq=mla_w_uq, mla_w_ukv=mla_w_ukv, mla_w_out=mla_w_out, ret_w_in=ret_w_in, ret_gn_g=ret_gn_g, ret_w_out=ret_w_out, ln_g=ln_g, ln_b=ln_b, loss_target=loss_target, m_meta=m_meta, m_fox_w_in=m_fox_w_in, m_fox_b_f=m_fox_b_f, m_fox_w_out=m_fox_w_out, m_mla_w_in=m_mla_w_in, m_mla_q_norm=m_mla_q_norm, m_mla_kv_norm=m_mla_kv_norm, m_mla_w_uq=m_mla_w_uq, m_mla_w_ukv=m_mla_w_ukv, m_mla_w_out=m_mla_w_out, m_ret_w_in=m_ret_w_in, m_ret_gn_g=m_ret_gn_g, m_ret_w_out=m_ret_w_out, m_ln_g=m_ln_g, m_ln_b=m_ln_b, v_meta=v_meta, v_fox_w_in=v_fox_w_in, v_fox_b_f=v_fox_b_f, v_fox_w_out=v_fox_w_out, v_mla_w_in=v_mla_w_in, v_mla_q_norm=v_mla_q_norm, v_mla_kv_norm=v_mla_kv_norm, v_mla_w_uq=v_mla_w_uq, v_mla_w_ukv=v_mla_w_ukv, v_mla_w_out=v_mla_w_out, v_ret_w_in=v_ret_w_in, v_ret_gn_g=v_ret_gn_g, v_ret_w_out=v_ret_w_out, v_ln_g=v_ln_g, v_ln_b=v_ln_b)
    weights = {n: given[n] for n in TWIN_WEIGHTS}
    shared = {n: given[n] for n in SHARED_INPUTS}
    per_example = {n: given[n] for n in ['x']}
    grad_fn = _jax.value_and_grad(_loss, argnums=(0, 1))

    def one_microbatch(ex, loss_target):
        ex = dict(ex)
        diff = ex.pop(TWIN_DIFF_INPUT)
        return grad_fn(weights, diff, {**shared, **ex}, loss_target)

    if N_MICROBATCH == 1:
        loss, (grad_w, grad_x) = one_microbatch(per_example, given["loss_target"])
    else:
        def body(carry, xs):
            loss_sum, grad_sum = carry
            l_k, (gw_k, gx_k) = one_microbatch(xs[0], xs[1])
            with _jax.named_scope("update"):
                return (loss_sum + l_k, _jax.tree.map(_jnp.add, grad_sum, gw_k)), gx_k

        init = (_jnp.zeros((), _jnp.float32), _jax.tree.map(_jnp.zeros_like, weights))
        (loss, grad_w), grad_x = _jax.lax.scan(body, init, (per_example, given["loss_target"]))
    with _jax.named_scope("update"):
        delta_w, new_m, new_v = {}, {}, {}
        for n in TWIN_WEIGHTS:
            delta_w[n], new_m[n], new_v[n] = _adamw(weights[n], grad_w[n], given["m_" + n], given["v_" + n])
    return (loss, grad_x, *[grad_w[n] for n in TWIN_WEIGHTS], *[delta_w[n] for n in TWIN_WEIGHTS],
            *[new_m[n] for n in TWIN_WEIGHTS], *[new_v[n] for n in TWIN_WEIGHTS])
```

```python
import functools
import math

import numpy as np
import jax
import jax.numpy as jnp
from jax import lax
from jax.experimental import pallas as pl
from jax.experimental.pallas import tpu as pltpu

F32 = jnp.float32
_MXU = jnp.bfloat16

N_DEV = 8
N_META = 16
D_MODEL = 1024
ROW_TILE = 512
LANES = 128
CHUNK = 128
NEG = -1e30
VMEM_LIMIT = 60 * 1024 * 1024

FOX_HEADS = 8
MLA_HEADS = 8
MLA_NOPE, MLA_ROPE, MLA_V = 128, 64, 128
MLA_Q_LORA, MLA_KV_LORA = 384, 256
MLA_DK = 256
RET_HEADS, RET_QK, RET_V = 4, 256, 512
ROPE_BASE = 10000.0
ALPHA = (2 * 4) ** 0.25
NORM_EPS = 1e-5

ADAM_LR, ADAM_B1, ADAM_B2, ADAM_EPS, ADAM_WD, ADAM_STEP = 0.001, 0.9, 0.999, 1e-08, 0.01, 10


def _params(sem, vmem=VMEM_LIMIT):
    return pltpu.CompilerParams(dimension_semantics=sem, vmem_limit_bytes=vmem)


def _tile(n, pref):
    if n <= pref:
        return n
    t = (pref // LANES) * LANES
    while n % t:
        t -= LANES
    return t


def _matmul(a, b, mode, out_dtype, name, tm=512, tn=512, tk=1024):
    if mode == "nn":
        (M, K), (K2, N) = a.shape, b.shape
    elif mode == "nt":
        (M, K), (N, K2) = a.shape, b.shape
    else:
        (K, M), (K2, N) = a.shape, b.shape
    assert K == K2, (a.shape, b.shape, mode)
    tm, tn, tk = _tile(M, tm), _tile(N, tn), _tile(K, tk)
    nk = K // tk
    if mode == "nn":
        a_spec = pl.BlockSpec((tm, tk), lambda i, j, k: (i, k))
        b_spec = pl.BlockSpec((tk, tn), lambda i, j, k: (k, j))
        dn = (((1,), (0,)), ((), ()))
    elif mode == "nt":
        a_spec = pl.BlockSpec((tm, tk), lambda i, j, k: (i, k))
        b_spec = pl.BlockSpec((tn, tk), lambda i, j, k: (j, k))
        dn = (((1,), (1,)), ((), ()))
    else:
        a_spec = pl.BlockSpec((tk, tm), lambda i, j, k: (k, i))
        b_spec = pl.BlockSpec((tk, tn), lambda i, j, k: (k, j))
        dn = (((0,), (0,)), ((), ()))

    def body(a_ref, b_ref, o_ref, acc_ref):
        k = pl.program_id(2)
        part = lax.dot_general(a_ref[...].astype(_MXU), b_ref[...].astype(_MXU), dn,
                               preferred_element_type=F32)

        @pl.when(k == 0)
        def _():
            acc_ref[...] = part

        @pl.when(k > 0)
        def _():
            acc_ref[...] += part

        @pl.when(k == nk - 1)
        def _():
            o_ref[...] = acc_ref[...].astype(out_dtype)

    return pl.pallas_call(
        body, name=name,
        out_shape=jax.ShapeDtypeStruct((M, N), out_dtype),
        grid=(M // tm, N // tn, nk),
        in_specs=[a_spec, b_spec],
        out_specs=pl.BlockSpec((tm, tn), lambda i, j, k: (i, j)),
        scratch_shapes=[pltpu.VMEM((tm, tn), F32)],
        compiler_params=_params(("parallel", "parallel", "arbitrary")),
    )(a, b)


def _rows(w, off=0):
    return pl.BlockSpec((ROW_TILE, w), lambda i: (i + off, 0))


def _whole(shape):
    return pl.BlockSpec(shape, lambda i: (0,) * len(shape))


def _silu(z):
    return z * jax.nn.sigmoid(z)


def _dsilu(z):
    s = jax.nn.sigmoid(z)
    return s * (1.0 + z * (1.0 - s))


def _ln_fwd(h, y, g, b, name):
    L, D = h.shape

    def body(h_ref, y_ref, g_ref, b_ref, z_ref, o_ref):
        z = ALPHA * h_ref[...] + y_ref[...]
        mu = jnp.mean(z, axis=-1, keepdims=True)
        zc = z - mu
        var = jnp.mean(zc * zc, axis=-1, keepdims=True)
        z_ref[...] = z
        o_ref[...] = zc * lax.rsqrt(var + NORM_EPS) * g_ref[...] + b_ref[...]

    return pl.pallas_call(
        body, name=name,
        out_shape=(jax.ShapeDtypeStruct((L, D), F32), jax.ShapeDtypeStruct((L, D), F32)),
        grid=(L // ROW_TILE,),
        in_specs=[_rows(D), _rows(D), _whole((1, D)), _whole((1, D))],
        out_specs=(_rows(D), _rows(D)),
        compiler_params=_params(("parallel",)),
    )(h, y, g, b)


def _ln_bwd(zl, g, ga, gb, name):
    L, D = zl.shape
    two = gb is not None

    def body(*refs):
        if two:
            z_ref, g_ref, ga_ref, gb_ref, dz_ref, dg_ref, db_ref = refs
            gout = ALPHA * ga_ref[...] + gb_ref[...]
        else:
            z_ref, g_ref, ga_ref, dz_ref, dg_ref, db_ref = refs
            gout = ga_ref[...]
        z = z_ref[...]
        mu = jnp.mean(z, axis=-1, keepdims=True)
        zc = z - mu
        var = jnp.mean(zc * zc, axis=-1, keepdims=True)
        rstd = lax.rsqrt(var + NORM_EPS)
        xhat = zc * rstd
        dxh = gout * g_ref[...]
        m1 = jnp.mean(dxh, axis=-1, keepdims=True)
        m2 = jnp.mean(dxh * xhat, axis=-1, keepdims=True)
        dz_ref[...] = rstd * (dxh - m1 - xhat * m2)

        @pl.when(pl.program_id(0) == 0)
        def _():
            dg_ref[...] = jnp.zeros_like(dg_ref)
            db_ref[...] = jnp.zeros_like(db_ref)

        dg_ref[...] += jnp.sum(gout * xhat, axis=0, keepdims=True)
        db_ref[...] += jnp.sum(gout, axis=0, keepdims=True)

    ins = [zl, g, ga] + ([gb] if two else [])
    return pl.pallas_call(
        body, name=name,
        out_shape=(jax.ShapeDtypeStruct((L, D), F32), jax.ShapeDtypeStruct((1, D), F32),
                   jax.ShapeDtypeStruct((1, D), F32)),
        grid=(L // ROW_TILE,),
        in_specs=[_rows(D), _whole((1, D)), _rows(D)] + ([_rows(D)] if two else []),
        out_specs=(_rows(D), _whole((1, D)), _whole((1, D))),
        compiler_params=_params(("arbitrary",)),
    )(*ins)


def _loss_head(h, target, first):
    L, D = h.shape

    def body(h_ref, t_ref, dh_ref, loss_ref):
        i = pl.program_id(0)

        @pl.when(i == 0)
        def _():
            loss_ref[...] = jnp.zeros_like(loss_ref)

        @pl.when(i < first)
        def _():
            dh_ref[...] = jnp.zeros_like(dh_ref)

        @pl.when(i >= first)
        def _():
            err = h_ref[...] - t_ref[...]
            dh_ref[...] = err * (1.0 / D)
            part = jnp.sum(jnp.sum(err * err, axis=-1, keepdims=True) * (1.0 / D), axis=0, keepdims=True)
            loss_ref[...] += 0.5 * part

    return pl.pallas_call(
        body, name="loss_head",
        out_shape=(jax.ShapeDtypeStruct((L, D), F32), jax.ShapeDtypeStruct((1, 1), F32)),
        grid=(L // ROW_TILE,),
        in_specs=[_rows(D), pl.BlockSpec((ROW_TILE, D), lambda i: (jnp.maximum(i - first, 0), 0))],
        out_specs=(_rows(D), _whole((1, 1))),
        compiler_params=_params(("arbitrary",)),
    )(h, target)


def _input_grads(dzl, dmix, first, n_real):
    L, D = dzl.shape

    def body(a_ref, b_ref, gx_ref, gm_ref):
        i = pl.program_id(0)
        val = ALPHA * a_ref[...] + b_ref[...]

        @pl.when(i == first - 1)
        def _():
            gm_ref[...] = val[ROW_TILE - N_META:, :]

        @pl.when(i >= first)
        def _():
            gx_ref[...] = val

    return pl.pallas_call(
        body, name="input_grads",
        out_shape=(jax.ShapeDtypeStruct((n_real, D), F32), jax.ShapeDtypeStruct((N_META, D), F32)),
        grid=(L // ROW_TILE,),
        in_specs=[_rows(D), _rows(D)],
        out_specs=(pl.BlockSpec((ROW_TILE, D), lambda i: (jnp.maximum(i - first, 0), 0)),
                   _whole((N_META, D))),
        compiler_params=_params(("arbitrary",)),
    )(dzl, dmix)


def _gate_fwd(o, zsrc, name):
    L, W = o.shape
    Wz = zsrc.shape[1]

    def body(o_ref, z_ref, y_ref):
        y_ref[...] = (o_ref[...] * _silu(z_ref[:, :W])).astype(_MXU)

    return pl.pallas_call(
        body, name=name,
        out_shape=jax.ShapeDtypeStruct((L, W), _MXU),
        grid=(L // ROW_TILE,),
        in_specs=[_rows(W), _rows(Wz)],
        out_specs=_rows(W),
        compiler_params=_params(("parallel",)),
    )(o, zsrc)


def _gate_bwd(dy, o, zsrc, heads, name):
    L, W = o.shape
    Wz = zsrc.shape[1]
    hd = W // heads

    def body(dy_ref, o_ref, z_ref, do_ref, dl_ref, dz_ref):
        z = z_ref[:, :W]
        dy = dy_ref[...]
        o = o_ref[...]
        do = dy * _silu(z)
        do_ref[...] = do.astype(_MXU)
        dz_ref[...] = (dy * o * _dsilu(z)).astype(_MXU)
        prod = do * o
        for h in range(heads):
            dl_ref[h] = jnp.sum(prod[:, h * hd:(h + 1) * hd], axis=-1, keepdims=True)

    return pl.pallas_call(
        body, name=name,
        out_shape=(jax.ShapeDtypeStruct((L, W), _MXU), jax.ShapeDtypeStruct((heads, L, 1), F32),
                   jax.ShapeDtypeStruct((L, W), _MXU)),
        grid=(L // ROW_TILE,),
        in_specs=[_rows(W), _rows(W), _rows(Wz)],
        out_specs=(_rows(W), pl.BlockSpec((heads, ROW_TILE, 1), lambda i: (0, i, 0)), _rows(W)),
        compiler_params=_params(("parallel",)),
    )(dy, o, zsrc)


def _split3(x):
    hi = x.astype(_MXU)
    r1 = x - hi.astype(F32)
    mid = r1.astype(_MXU)
    lo = (r1 - mid.astype(F32)).astype(_MXU)
    return hi, mid, lo


def _tri_cumsum(x, reverse):
    T = x.shape[0]
    r = lax.broadcasted_iota(jnp.int32, (T, T), 0)
    c = lax.broadcasted_iota(jnp.int32, (T, T), 1)
    tri = jnp.where((c >= r) if reverse else (c <= r), 1.0, 0.0).astype(_MXU)
    out = jnp.zeros(x.shape, F32)
    for part in _split3(x):
        out = out + jnp.dot(tri, part, preferred_element_type=F32)
    return out


def _fox_cum(zf, bias, pad, name):
    L, Wz = zf.shape
    off = Wz // LANES - 1

    def body(f_ref, b_ref, c_ref, carry_ref):
        i = pl.program_id(0)

        @pl.when(i == 0)
        def _():
            carry_ref[...] = jnp.zeros_like(carry_ref)

        x = f_ref[...] + b_ref[...]
        logf = jnp.minimum(x, 0.0) - jnp.log(1.0 + jnp.exp(-jnp.abs(x)))
        row = i * ROW_TILE + lax.broadcasted_iota(jnp.int32, logf.shape, 0)
        logf = jnp.where(row >= pad, logf, 0.0)
        cum = _tri_cumsum(logf, False) + carry_ref[...]
        c_ref[...] = cum
        carry_ref[...] = cum[ROW_TILE - 1:, :]

    return pl.pallas_call(
        body, name=name,
        out_shape=jax.ShapeDtypeStruct((L, LANES), F32),
        grid=(L // ROW_TILE,),
        in_specs=[pl.BlockSpec((ROW_TILE, LANES), lambda i: (i, off)), _whole((1, LANES))],
        out_specs=_rows(LANES),
        scratch_shapes=[pltpu.VMEM((1, LANES), F32)],
        compiler_params=_params(("arbitrary",)),
    )(zf, bias)


def _fox_cum_bwd(dcum, zf, bias, pad, name):
    L, Wz = zf.shape
    off = Wz // LANES - 1
    n = L // ROW_TILE

    def body(d_ref, f_ref, b_ref, df_ref, db_ref, carry_ref):
        i = pl.program_id(0)

        @pl.when(i == 0)
        def _():
            carry_ref[...] = jnp.zeros_like(carry_ref)
            db_ref[...] = jnp.zeros_like(db_ref)

        rc = _tri_cumsum(d_ref[...], True) + carry_ref[...]
        carry_ref[...] = rc[:1, :]
        x = f_ref[...] + b_ref[...]
        row = (n - 1 - i) * ROW_TILE + lax.broadcasted_iota(jnp.int32, x.shape, 0)
        df = jnp.where(row >= pad, rc * jax.nn.sigmoid(-x), 0.0)
        df_ref[...] = df.astype(_MXU)
        db_ref[...] += jnp.sum(df, axis=0, keepdims=True)

    return pl.pallas_call(
        body, name=name,
        out_shape=(jax.ShapeDtypeStruct((L, LANES), _MXU), jax.ShapeDtypeStruct((1, LANES), F32)),
        grid=(n,),
        in_specs=[pl.BlockSpec((ROW_TILE, LANES), lambda i: (n - 1 - i, 0)),
                  pl.BlockSpec((ROW_TILE, LANES), lambda i: (n - 1 - i, off)), _whole((1, LANES))],
        out_specs=(pl.BlockSpec((ROW_TILE, LANES), lambda i: (n - 1 - i, 0)), _whole((1, LANES))),
        scratch_shapes=[pltpu.VMEM((1, LANES), F32)],
        compiler_params=_params(("arbitrary",)),
    )(dcum, zf, bias)


def _attn_fwd(q, qoff, k, koff, v, voff, cq_col, ck_row, heads, dk, dv, scale, name):
    L = q.shape[0]
    T = ROW_TILE
    dn_qk = (((1,), (1,)), ((), ()))

    def body(q_ref, k_ref, v_ref, cq_ref, ck_ref, o_ref, lse_ref):
        i = pl.program_id(1)
        qb = q_ref[...]
        cq = cq_ref[...]

        def step(j, carry, diag):
            m, l, acc = carry
            start = pl.multiple_of(j * T, T)
            kj = k_ref[pl.ds(start, T), :]
            vj = v_ref[pl.ds(start, T), :]
            s = lax.dot_general(qb, kj, dn_qk, preferred_element_type=F32) * scale
            s = s + (cq - ck_ref[j])
            if diag:
                r = lax.broadcasted_iota(jnp.int32, (T, T), 0)
                c = lax.broadcasted_iota(jnp.int32, (T, T), 1)
                s = jnp.where(c <= r, s, NEG)
            m_new = jnp.maximum(m, jnp.max(s, axis=1, keepdims=True))
            p = jnp.exp(s - m_new)
            a = jnp.exp(m - m_new)
            l = a * l + jnp.sum(p, axis=1, keepdims=True)
            acc = a * acc + jnp.dot(p.astype(_MXU), vj, preferred_element_type=F32)
            return m_new, l, acc

        init = (jnp.full((T, 1), -jnp.inf, F32), jnp.zeros((T, 1), F32), jnp.zeros((T, dv), F32))
        carry = lax.fori_loop(0, i, lambda j, c: step(j, c, False), init)
        m, l, acc = step(i, carry, True)
        o_ref[...] = acc / l
        lse_ref[...] = m + jnp.log(l)

    return pl.pallas_call(
        body, name=name,
        out_shape=(jax.ShapeDtypeStruct((L, heads * dv), F32), jax.ShapeDtypeStruct((heads, L, 1), F32)),
        grid=(heads, L // T),
        in_specs=[pl.BlockSpec((T, dk), lambda h, i: (i, qoff + h)),
                  pl.BlockSpec((L, dk), lambda h, i: (0, koff + h)),
                  pl.BlockSpec((L, dv), lambda h, i: (0, voff + h)),
                  pl.BlockSpec((None, T, 1), lambda h, i: (h, i, 0)),
                  pl.BlockSpec((None, L // T, 1, T), lambda h, i: (h, 0, 0, 0))],
        out_specs=(pl.BlockSpec((T, dv), lambda h, i: (i, h)),
                   pl.BlockSpec((None, T, 1), lambda h, i: (h, i, 0))),
        compiler_params=_params(("parallel", "arbitrary")),
    )(q, k, v, cq_col, ck_row.reshape(heads, L // T, 1, T))


def _attn_bwd(q, qoff, k, koff, v, voff, do, cq_row, ck_col, lse_row, delta_row, heads, dk, dv, scale, name):
    L = q.shape[0]
    T = ROW_TILE
    n = L // T
    dn_nt = (((1,), (1,)), ((), ()))
    dn_tn = (((0,), (0,)), ((), ()))

    def body(q_ref, k_ref, v_ref, do_ref, cq_ref, ck_ref, lse_ref, dl_ref,
             dq_hbm, dk_ref, dv_ref, dck_ref, dcq_ref, dq_acc, sem):
        h = pl.program_id(0)
        j = pl.program_id(1)

        @pl.when(j == 0)
        def _():
            dq_acc[...] = jnp.zeros_like(dq_acc)
            dcq_ref[...] = jnp.zeros_like(dcq_ref)

        kb = k_ref[...]
        vb = v_ref[...]
        ck = ck_ref[...]

        def step(i, carry, diag):
            dk_a, dv_a, dck_a = carry
            start = pl.multiple_of(i * T, T)
            qi = q_ref[pl.ds(start, T), :]
            doi = do_ref[pl.ds(start, T), :]
            st = lax.dot_general(kb, qi, dn_nt, preferred_element_type=F32) * scale
            st = st + (cq_ref[i] - ck)
            if diag:
                r = lax.broadcasted_iota(jnp.int32, (T, T), 0)
                c = lax.broadcasted_iota(jnp.int32, (T, T), 1)
                st = jnp.where(r <= c, st, NEG)
            pt = jnp.exp(st - lse_ref[i])
            dv_a = dv_a + jnp.dot(pt.astype(_MXU), doi, preferred_element_type=F32)
            dpt = lax.dot_general(vb, doi, dn_nt, preferred_element_type=F32)
            dst = pt * (dpt - dl_ref[i])
            dck_a = dck_a - jnp.sum(dst, axis=1, keepdims=True)
            dcq_ref[i] += jnp.sum(dst, axis=0, keepdims=True)
            dsb = (dst * scale).astype(_MXU)
            dk_a = dk_a + jnp.dot(dsb, qi, preferred_element_type=F32)
            dq_acc[pl.ds(start, T), :] += lax.dot_general(dsb, kb, dn_tn, preferred_element_type=F32)
            return dk_a, dv_a, dck_a

        init = (jnp.zeros((T, dk), F32), jnp.zeros((T, dv), F32), jnp.zeros((T, 1), F32))
        carry = step(j, init, True)
        dk_a, dv_a, dck_a = lax.fori_loop(j + 1, n, lambda i, c: step(i, c, False), carry)
        dk_ref[...] = dk_a
        dv_ref[...] = dv_a
        dck_ref[...] = dck_a

        @pl.when(j == n - 1)
        def _():
            cp = pltpu.make_async_copy(dq_acc, dq_hbm.at[:, pl.ds(pl.multiple_of(h * dk, dk), dk)], sem)
            cp.start()
            cp.wait()

    return pl.pallas_call(
        body, name=name,
        out_shape=(jax.ShapeDtypeStruct((L, heads * dk), F32), jax.ShapeDtypeStruct((L, heads * dk), F32),
                   jax.ShapeDtypeStruct((L, heads * dv), F32), jax.ShapeDtypeStruct((heads, L, 1), F32),
                   jax.ShapeDtypeStruct((heads, n, 1, T), F32)),
        grid=(heads, n),
        in_specs=[pl.BlockSpec((L, dk), lambda h, j: (0, qoff + h)),
                  pl.BlockSpec((T, dk), lambda h, j: (j, koff + h)),
                  pl.BlockSpec((T, dv), lambda h, j: (j, voff + h)),
                  pl.BlockSpec((L, dv), lambda h, j: (0, h)),
                  pl.BlockSpec((None, n, 1, T), lambda h, j: (h, 0, 0, 0)),
                  pl.BlockSpec((None, T, 1), lambda h, j: (h, j, 0)),
                  pl.BlockSpec((None, n, 1, T), lambda h, j: (h, 0, 0, 0)),
                  pl.BlockSpec((None, n, 1, T), lambda h, j: (h, 0, 0, 0))],
        out_specs=(pl.BlockSpec(memory_space=pl.ANY),
                   pl.BlockSpec((T, dk), lambda h, j: (j, h)),
                   pl.BlockSpec((T, dv), lambda h, j: (j, h)),
                   pl.BlockSpec((None, T, 1), lambda h, j: (h, j, 0)),
                   pl.BlockSpec((None, n, 1, T), lambda h, j: (h, 0, 0, 0))),
        scratch_shapes=[pltpu.VMEM((L, dk), F32), pltpu.SemaphoreType.DMA],
        compiler_params=_params(("arbitrary", "arbitrary")),
    )(q, k, v, do, cq_row.reshape(heads, n, 1, T), ck_col, lse_row.reshape(heads, n, 1, T),
      delta_row.reshape(heads, n, 1, T))


_CQ0, _CKV0, _KR0 = D_MODEL, D_MODEL + MLA_Q_LORA, D_MODEL + MLA_Q_LORA + MLA_KV_LORA
_MLA_PROJ = _KR0 + LANES


def _rms(x, g):
    ms = jnp.mean(x * x, axis=-1, keepdims=True)
    return x * lax.rsqrt(ms + NORM_EPS) * g


def _rms_bwd(x, g, dy):
    ms = jnp.mean(x * x, axis=-1, keepdims=True)
    r = lax.rsqrt(ms + NORM_EPS)
    xh = x * r
    dxh = dy * g
    dx = r * (dxh - xh * jnp.mean(dxh * xh, axis=-1, keepdims=True))
    return dx, jnp.sum(dy * xh, axis=0, keepdims=True)


def _mla_norm_fwd(proj, gq, gkv):
    L = proj.shape[0]

    def body(p_ref, gq_ref, gkv_ref, cq_ref, ckv_ref):
        cq_ref[...] = _rms(p_ref[:, _CQ0:_CKV0], gq_ref[...]).astype(_MXU)
        ckv_ref[...] = _rms(p_ref[:, _CKV0:_KR0], gkv_ref[...]).astype(_MXU)

    return pl.pallas_call(
        body, name="mla_norm_fwd",
        out_shape=(jax.ShapeDtypeStruct((L, MLA_Q_LORA), _MXU), jax.ShapeDtypeStruct((L, MLA_KV_LORA), _MXU)),
        grid=(L // ROW_TILE,),
        in_specs=[_rows(_MLA_PROJ), _whole((1, MLA_Q_LORA)), _whole((1, MLA_KV_LORA))],
        out_specs=(_rows(MLA_Q_LORA), _rows(MLA_KV_LORA)),
        compiler_params=_params(("parallel",)),
    )(proj, gq, gkv)


def _mla_norm_bwd(proj, gq, gkv, dcqn, dckvn):
    L = proj.shape[0]

    def body(p_ref, gq_ref, gkv_ref, dq_ref, dkv_ref, dcq_ref, dckv_ref, dgq_ref, dgkv_ref):
        @pl.when(pl.program_id(0) == 0)
        def _():
            dgq_ref[...] = jnp.zeros_like(dgq_ref)
            dgkv_ref[...] = jnp.zeros_like(dgkv_ref)

        dx, dg = _rms_bwd(p_ref[:, _CQ0:_CKV0], gq_ref[...], dq_ref[...])
        dcq_ref[...] = dx.astype(_MXU)
        dgq_ref[...] += dg
        dx, dg = _rms_bwd(p_ref[:, _CKV0:_KR0], gkv_ref[...], dkv_ref[...])
        dckv_ref[...] = dx.astype(_MXU)
        dgkv_ref[...] += dg

    return pl.pallas_call(
        body, name="mla_norm_bwd",
        out_shape=(jax.ShapeDtypeStruct((L, MLA_Q_LORA), _MXU), jax.ShapeDtypeStruct((L, MLA_KV_LORA), _MXU),
                   jax.ShapeDtypeStruct((1, MLA_Q_LORA), F32), jax.ShapeDtypeStruct((1, MLA_KV_LORA), F32)),
        grid=(L // ROW_TILE,),
        in_specs=[_rows(_MLA_PROJ), _whole((1, MLA_Q_LORA)), _whole((1, MLA_KV_LORA)),
                  _rows(MLA_Q_LORA), _rows(MLA_KV_LORA)],
        out_specs=(_rows(MLA_Q_LORA), _rows(MLA_KV_LORA), _whole((1, MLA_Q_LORA)), _whole((1, MLA_KV_LORA))),
        compiler_params=_params(("arbitrary",)),
    )(proj, gq, gkv, dcqn, dckvn)


def _rot_tile(t, cos, sa, sb):
    half = MLA_ROPE // 2
    return t * cos + pltpu.roll(t, LANES - half, 1) * sa + pltpu.roll(t, half, 1) * sb


def _mla_rope_fwd(qf, kv, proj, cos, sa, sb):
    L = qf.shape[0]
    H = MLA_HEADS

    def body(q_ref, kv_ref, p_ref, c_ref, sa_ref, sb_ref, qo_ref, ko_ref):
        cos, sa, sb = c_ref[...], sa_ref[...], sb_ref[...]
        kr = _rot_tile(p_ref[:, _KR0:_KR0 + LANES], cos, sa, sb).astype(_MXU)
        for h in range(H):
            b = h * MLA_DK
            qo_ref[:, b:b + LANES] = q_ref[:, b:b + LANES].astype(_MXU)
            qo_ref[:, b + LANES:b + 2 * LANES] = _rot_tile(q_ref[:, b + LANES:b + 2 * LANES], cos, sa, sb).astype(_MXU)
            ko_ref[:, b:b + LANES] = kv_ref[:, h * LANES:(h + 1) * LANES]
            ko_ref[:, b + LANES:b + 2 * LANES] = kr

    W = H * MLA_DK
    return pl.pallas_call(
        body, name="mla_rope_fwd",
        out_shape=(jax.ShapeDtypeStruct((L, W), _MXU), jax.ShapeDtypeStruct((L, W), _MXU)),
        grid=(L // ROW_TILE,),
        in_specs=[_rows(W), _rows(kv.shape[1]), _rows(_MLA_PROJ), _rows(LANES), _rows(LANES), _rows(LANES)],
        out_specs=(_rows(W), _rows(W)),
        compiler_params=_params(("parallel",)),
    )(qf, kv, proj, cos, sa, sb)


def _mla_rope_bwd(dq_full, dk_full, cos, sa, sb):
    L = dq_full.shape[0]
    H = MLA_HEADS
    W = H * MLA_DK

    def body(dq_ref, dk_ref, c_ref, sa_ref, sb_ref, dqf_ref, dkn_ref, dkr_ref):
        cos, sa, sb = c_ref[...], sa_ref[...], sb_ref[...]
        lane = lax.broadcasted_iota(jnp.int32, (ROW_TILE, LANES), 1)
        live = lane < MLA_ROPE
        krs = jnp.zeros((ROW_TILE, LANES), F32)
        for h in range(H):
            b = h * MLA_DK
            dqf_ref[:, b:b + LANES] = dq_ref[:, b:b + LANES].astype(_MXU)
            dqr = _rot_tile(dq_ref[:, b + LANES:b + 2 * LANES], cos, sa, sb)
            dqf_ref[:, b + LANES:b + 2 * LANES] = jnp.where(live, dqr, 0.0).astype(_MXU)
            dkn_ref[:, h * LANES:(h + 1) * LANES] = dk_ref[:, b:b + LANES].astype(_MXU)
            krs = krs + dk_ref[:, b + LANES:b + 2 * LANES]
        dkr_ref[...] = jnp.where(live, _rot_tile(krs, cos, sa, sb), 0.0).astype(_MXU)

    return pl.pallas_call(
        body, name="mla_rope_bwd",
        out_shape=(jax.ShapeDtypeStruct((L, W), _MXU), jax.ShapeDtypeStruct((L, H * LANES), _MXU),
                   jax.ShapeDtypeStruct((L, LANES), _MXU)),
        grid=(L // ROW_TILE,),
        in_specs=[_rows(W), _rows(W), _rows(LANES), _rows(LANES), _rows(LANES)],
        out_specs=(_rows(W), _rows(H * LANES), _rows(LANES)),
        compiler_params=_params(("parallel",)),
    )(dq_full, dk_full, cos, sa, sb)


_RET_QKW = RET_HEADS * RET_QK


def _ret_rope_fwd(qkz, cos, sin, pad):
    L = qkz.shape[0]
    hh = RET_QK // 2
    kscale = RET_QK ** -0.5

    def body(p_ref, c_ref, s_ref, q_ref, k_ref):
        cos, sin = c_ref[...], s_ref[...]
        row = pl.program_id(0) * ROW_TILE + lax.broadcasted_iota(jnp.int32, (ROW_TILE, hh), 0)
        keep = row >= pad
        for h in range(RET_HEADS):
            b = h * RET_QK
            t1, t2 = p_ref[:, b:b + hh], p_ref[:, b + hh:b + 2 * hh]
            q_ref[:, b:b + hh] = (t1 * cos - t2 * sin).astype(_MXU)
            q_ref[:, b + hh:b + 2 * hh] = (t2 * cos + t1 * sin).astype(_MXU)
            t1, t2 = p_ref[:, _RET_QKW + b:_RET_QKW + b + hh], p_ref[:, _RET_QKW + b + hh:_RET_QKW + b + 2 * hh]
            k_ref[:, b:b + hh] = jnp.where(keep, (t1 * cos - t2 * sin) * kscale, 0.0)
            k_ref[:, b + hh:b + 2 * hh] = jnp.where(keep, (t2 * cos + t1 * sin) * kscale, 0.0)

    return pl.pallas_call(
        body, name="ret_rope_fwd",
        out_shape=(jax.ShapeDtypeStruct((L, _RET_QKW), _MXU), jax.ShapeDtypeStruct((L, _RET_QKW), F32)),
        grid=(L // ROW_TILE,),
        in_specs=[_rows(qkz.shape[1]), _rows(hh), _rows(hh)],
        out_specs=(_rows(_RET_QKW), _rows(_RET_QKW)),
        compiler_params=_params(("parallel",)),
    )(qkz, cos, sin)


def _ret_rope_bwd(dqr, dkr, cos, sin, pad):
    L = dqr.shape[0]
    hh = RET_QK // 2
    kscale = RET_QK ** -0.5

    def body(dq_ref, dk_ref, c_ref, s_ref, oq_ref, ok_ref):
        cos, sin = c_ref[...], s_ref[...]
        row = pl.program_id(0) * ROW_TILE + lax.broadcasted_iota(jnp.int32, (ROW_TILE, hh), 0)
        keep = row >= pad
        for h in range(RET_HEADS):
            b = h * RET_QK
            d1, d2 = dq_ref[:, b:b + hh], dq_ref[:, b + hh:b + 2 * hh]
            oq_ref[:, b:b + hh] = (d1 * cos + d2 * sin).astype(_MXU)
            oq_ref[:, b + hh:b + 2 * hh] = (d2 * cos - d1 * sin).astype(_MXU)
            d1, d2 = dk_ref[:, b:b + hh], dk_ref[:, b + hh:b + 2 * hh]
            ok_ref[:, b:b + hh] = jnp.where(keep, (d1 * cos + d2 * sin) * kscale, 0.0).astype(_MXU)
            ok_ref[:, b + hh:b + 2 * hh] = jnp.where(keep, (d2 * cos - d1 * sin) * kscale, 0.0).astype(_MXU)

    return pl.pallas_call(
        body, name="ret_rope_bwd",
        out_shape=(jax.ShapeDtypeStruct((L, _RET_QKW), _MXU), jax.ShapeDtypeStruct((L, _RET_QKW), _MXU)),
        grid=(L // ROW_TILE,),
        in_specs=[_rows(_RET_QKW), _rows(_RET_QKW), _rows(hh), _rows(hh)],
        out_specs=(_rows(_RET_QKW), _rows(_RET_QKW)),
        compiler_params=_params(("parallel",)),
    )(dqr, dkr, cos, sin)


def _ret_decays():
    lg = np.log1p(-np.exp2(-5.0 - np.arange(RET_HEADS, dtype=np.float32))).astype(np.float32)
    i = np.arange(CHUNK, dtype=np.float32)
    rel = i[:, None] - i[None, :]
    dm = np.where(rel[None] >= 0, np.exp(rel[None] * lg[:, None, None]), 0.0).astype(np.float32)
    qd = np.exp((i[None, :] + 1.0) * lg[:, None]).astype(np.float32)[:, :, None]
    kd = np.exp((CHUNK - 1.0 - i)[None, :] * lg[:, None]).astype(np.float32)[:, :, None]
    cd = np.broadcast_to(np.exp(CHUNK * lg).astype(np.float32)[:, None, None], (RET_HEADS, 1, RET_V))
    return jnp.asarray(dm), jnp.asarray(qd), jnp.asarray(kd), jnp.asarray(np.ascontiguousarray(cd))


def _ret_fwd(qr, kr, v):
    L = qr.shape[0]
    nc = L // CHUNK
    dm, qd, kd, cd = _ret_decays()
    dn_nt = (((1,), (1,)), ((), ()))
    dn_tn = (((0,), (0,)), ((), ()))

    def body(q_ref, k_ref, v_ref, dm_ref, qd_ref, kd_ref, cd_ref, o_ref, st_ref, s_ref):
        @pl.when(pl.program_id(1) == 0)
        def _():
            s_ref[...] = jnp.zeros_like(s_ref)

        q, k, vv = q_ref[...], k_ref[...], v_ref[...]
        sb = s_ref[...].astype(_MXU)
        st_ref[...] = sb
        s = lax.dot_general(q, k.astype(_MXU), dn_nt, preferred_element_type=F32) * dm_ref[...]
        intra = jnp.dot(s.astype(_MXU), vv, preferred_element_type=F32)
        cross = jnp.dot(q, sb, preferred_element_type=F32) * qd_ref[...]
        o_ref[...] = intra + cross
        kdm = (k * kd_ref[...]).astype(_MXU)
        s_ref[...] = cd_ref[...] * s_ref[...] + lax.dot_general(kdm, vv, dn_tn, preferred_element_type=F32)

    return pl.pallas_call(
        body, name="ret_fwd",
        out_shape=(jax.ShapeDtypeStruct((L, RET_HEADS * RET_V), F32),
                   jax.ShapeDtypeStruct((RET_HEADS, nc, RET_QK, RET_V), _MXU)),
        grid=(RET_HEADS, nc),
        in_specs=[pl.BlockSpec((CHUNK, RET_QK), lambda h, c: (c, h)),
                  pl.BlockSpec((CHUNK, RET_QK), lambda h, c: (c, h)),
                  pl.BlockSpec((CHUNK, RET_V), lambda h, c: (c, h)),
                  pl.BlockSpec((None, CHUNK, CHUNK), lambda h, c: (h, 0, 0)),
                  pl.BlockSpec((None, CHUNK, 1), lambda h, c: (h, 0, 0)),
                  pl.BlockSpec((None, CHUNK, 1), lambda h, c: (h, 0, 0)),
                  pl.BlockSpec((None, 1, RET_V), lambda h, c: (h, 0, 0))],
        out_specs=(pl.BlockSpec((CHUNK, RET_V), lambda h, c: (c, h)),
                   pl.BlockSpec((None, None, RET_QK, RET_V), lambda h, c: (h, c, 0, 0))),
        scratch_shapes=[pltpu.VMEM((RET_QK, RET_V), F32)],
        compiler_params=_params(("parallel", "arbitrary")),
    )(qr, kr, v, dm, qd, kd, cd)


def _ret_bwd(qr, kr, v, do, states):
    L = qr.shape[0]
    nc = L // CHUNK
    dm, qd, kd, cd = _ret_decays()
    dn_nt = (((1,), (1,)), ((), ()))
    dn_tn = (((0,), (0,)), ((), ()))

    def body(q_ref, k_ref, v_ref, do_ref, st_ref, dm_ref, qd_ref, kd_ref, cd_ref,
             dq_ref, dk_ref, dv_ref, ds_ref):
        @pl.when(pl.program_id(1) == 0)
        def _():
            ds_ref[...] = jnp.zeros_like(ds_ref)

        q, k, vv, do = q_ref[...], k_ref[...], v_ref[...], do_ref[...]
        kb = k.astype(_MXU)
        dob = do.astype(_MXU)
        doq = (do * qd_ref[...]).astype(_MXU)
        dsb = ds_ref[...].astype(_MXU)
        dmat = dm_ref[...]
        s = (lax.dot_general(q, kb, dn_nt, preferred_element_type=F32) * dmat).astype(_MXU)
        draw = (lax.dot_general(dob, vv, dn_nt, preferred_element_type=F32) * dmat).astype(_MXU)
        kdm = (k * kd_ref[...]).astype(_MXU)
        dv = (lax.dot_general(s, dob, dn_tn, preferred_element_type=F32)
              + jnp.dot(kdm, dsb, preferred_element_type=F32))
        dq = (jnp.dot(draw, kb, preferred_element_type=F32)
              + lax.dot_general(doq, st_ref[...], dn_nt, preferred_element_type=F32))
        dk = (lax.dot_general(draw, q, dn_tn, preferred_element_type=F32)
              + lax.dot_general(vv, dsb, dn_nt, preferred_element_type=F32) * kd_ref[...])
        dq_ref[...] = dq
        dk_ref[...] = dk
        dv_ref[...] = dv.astype(_MXU)
        ds_ref[...] = cd_ref[...] * ds_ref[...] + lax.dot_general(q, doq, dn_tn, preferred_element_type=F32)

    rev = lambda h, c: (nc - 1 - c, h)
    return pl.pallas_call(
        body, name="ret_bwd",
        out_shape=(jax.ShapeDtypeStruct((L, _RET_QKW), F32), jax.ShapeDtypeStruct((L, _RET_QKW), F32),
                   jax.ShapeDtypeStruct((L, RET_HEADS * RET_V), _MXU)),
        grid=(RET_HEADS, nc),
        in_specs=[pl.BlockSpec((CHUNK, RET_QK), rev),
                  pl.BlockSpec((CHUNK, RET_QK), rev),
                  pl.BlockSpec((CHUNK, RET_V), rev),
                  pl.BlockSpec((CHUNK, RET_V), rev),
                  pl.BlockSpec((None, None, RET_QK, RET_V), lambda h, c: (h, nc - 1 - c, 0, 0)),
                  pl.BlockSpec((None, CHUNK, CHUNK), lambda h, c: (h, 0, 0)),
                  pl.BlockSpec((None, CHUNK, 1), lambda h, c: (h, 0, 0)),
                  pl.BlockSpec((None, CHUNK, 1), lambda h, c: (h, 0, 0)),
                  pl.BlockSpec((None, 1, RET_V), lambda h, c: (h, 0, 0))],
        out_specs=(pl.BlockSpec((CHUNK, RET_QK), rev), pl.BlockSpec((CHUNK, RET_QK), rev),
                   pl.BlockSpec((CHUNK, RET_V), rev)),
        scratch_shapes=[pltpu.VMEM((RET_QK, RET_V), F32)],
        compiler_params=_params(("parallel", "arbitrary")),
    )(qr, kr, v, do, states, dm, qd, kd, cd)


def _gn(o, g):
    mu = jnp.mean(o, axis=-1, keepdims=True)
    oc = o - mu
    var = jnp.mean(oc * oc, axis=-1, keepdims=True)
    return oc * lax.rsqrt(var + NORM_EPS) * g


def _ret_gate_fwd(o, qkz, gn_g):
    L, W = o.shape
    z0 = qkz.shape[1] - W

    def body(o_ref, p_ref, g_ref, y_ref):
        for h in range(RET_HEADS):
            sl = slice(h * RET_V, (h + 1) * RET_V)
            z = p_ref[:, z0 + h * RET_V:z0 + (h + 1) * RET_V]
            y_ref[:, sl] = (_gn(o_ref[:, sl], g_ref[:, sl]) * _silu(z)).astype(_MXU)

    return pl.pallas_call(
        body, name="ret_gate_fwd",
        out_shape=jax.ShapeDtypeStruct((L, W), _MXU),
        grid=(L // ROW_TILE,),
        in_specs=[_rows(W), _rows(qkz.shape[1]), _whole((1, W))],
        out_specs=_rows(W),
        compiler_params=_params(("parallel",)),
    )(o, qkz, gn_g)


def _ret_gate_bwd(dy, o, qkz, gn_g):
    L, W = o.shape
    z0 = qkz.shape[1] - W

    def body(dy_ref, o_ref, p_ref, g_ref, do_ref, dz_ref, dg_ref):
        @pl.when(pl.program_id(0) == 0)
        def _():
            dg_ref[...] = jnp.zeros_like(dg_ref)

        for h in range(RET_HEADS):
            sl = slice(h * RET_V, (h + 1) * RET_V)
            z = p_ref[:, z0 + h * RET_V:z0 + (h + 1) * RET_V]
            o = o_ref[:, sl]
            g = g_ref[:, sl]
            dy = dy_ref[:, sl]
            mu = jnp.mean(o, axis=-1, keepdims=True)
            oc = o - mu
            var = jnp.mean(oc * oc, axis=-1, keepdims=True)
            r = lax.rsqrt(var + NORM_EPS)
            xh = oc * r
            dn = dy * _silu(z)
            dz_ref[:, sl] = (dy * (xh * g) * _dsilu(z)).astype(_MXU)
            dg_ref[:, sl] += jnp.sum(dn * xh, axis=0, keepdims=True)
            dxh = dn * g
            do_ref[:, sl] = r * (dxh - jnp.mean(dxh, axis=-1, keepdims=True)
                                 - xh * jnp.mean(dxh * xh, axis=-1, keepdims=True))

    return pl.pallas_call(
        body, name="ret_gate_bwd",
        out_shape=(jax.ShapeDtypeStruct((L, W), F32), jax.ShapeDtypeStruct((L, W), _MXU),
                   jax.ShapeDtypeStruct((1, W), F32)),
        grid=(L // ROW_TILE,),
        in_specs=[_rows(W), _rows(W), _rows(qkz.shape[1]), _whole((1, W))],
        out_specs=(_rows(W), _rows(W), _whole((1, W))),
        compiler_params=_params(("arbitrary",)),
    )(dy, o, qkz, gn_g)


def _heads_t(a):
    return a[:, :FOX_HEADS].T


def _fox_fwd(h, w, lnw, pad, tag):
    w_in, b_f, w_out = w
    L = h.shape[0]
    W = D_MODEL
    w_qkv = w_in[:, :3 * W]
    w_zf = jnp.concatenate([w_in[:, 3 * W:4 * W], jnp.pad(w_in[:, 4 * W:], ((0, 0), (0, LANES - FOX_HEADS)))], axis=1)
    bias = jnp.pad(b_f, (0, LANES - FOX_HEADS))[None, :]
    qkv = _matmul(h, w_qkv, "nn", _MXU, f"{tag}_qkv")
    zf = _matmul(h, w_zf, "nn", F32, f"{tag}_zf")
    cum = _fox_cum(zf, bias, pad, f"{tag}_cum")
    ct = _heads_t(cum)
    ckey = jnp.where(jnp.arange(L)[None, :] < pad, -NEG, ct)
    nb = W // LANES
    o, lse = _attn_fwd(qkv, 0, qkv, nb, qkv, 2 * nb, ct[:, :, None], ckey[:, None, :],
                       FOX_HEADS, LANES, LANES, LANES ** -0.5, f"{tag}_attn_fwd")
    y = _gate_fwd(o, zf, f"{tag}_gate_fwd")
    mo = _matmul(y, w_out, "nn", F32, f"{tag}_out")
    zl, hn = _ln_fwd(h, mo, lnw[0], lnw[1], f"{tag}_ln_fwd")
    return hn, dict(h=h, qkv=qkv, zf=zf, ct=ct, ckey=ckey, o=o, lse=lse, y=y, zl=zl,
                    w_all=jnp.concatenate([w_qkv, w_zf], axis=1), w_out=w_out, bias=bias)


def _fox_bwd(s, dzl, pad, tag):
    W = D_MODEL
    nb = W // LANES
    dy = _matmul(dzl, s["w_out"], "nt", F32, f"{tag}_dy")
    d_wout = _matmul(s["y"], dzl, "tn", F32, f"{tag}_dwout")
    do, delta, dzg = _gate_bwd(dy, s["o"], s["zf"], FOX_HEADS, f"{tag}_gate_bwd")
    qkv = s["qkv"]
    dq, dk, dv, dck, dcq = _attn_bwd(qkv, 0, qkv, nb, qkv, 2 * nb, do, s["ct"][:, None, :], s["ckey"][:, :, None],
                                     s["lse"].reshape(FOX_HEADS, 1, -1), delta.reshape(FOX_HEADS, 1, -1),
                                     FOX_HEADS, LANES, LANES, LANES ** -0.5, f"{tag}_attn_bwd")
    dcum = jnp.pad((dck[:, :, 0] + dcq.reshape(FOX_HEADS, -1)).T, ((0, 0), (0, LANES - FOX_HEADS)))
    df, dbias = _fox_cum_bwd(dcum, s["zf"], s["bias"], pad, f"{tag}_cum_bwd")
    dproj = jnp.concatenate([dq.astype(_MXU), dk.astype(_MXU), dv.astype(_MXU), dzg, df], axis=1)
    d_wall = _matmul(s["h"], dproj, "tn", F32, f"{tag}_dwin")
    dh = _matmul(dproj, s["w_all"], "nt", F32, f"{tag}_dh")
    d_win = jnp.concatenate([d_wall[:, :4 * W], d_wall[:, 4 * W:4 * W + FOX_HEADS]], axis=1)
    return dh, (d_win, dbias[0, :FOX_HEADS], d_wout)


def _mla_tables(L, pad):
    pos = (jnp.arange(L) - pad).astype(F32)
    inv = ROPE_BASE ** (-jnp.arange(0, MLA_ROPE, 2, dtype=F32) / MLA_ROPE)
    ang = pos[:, None] * inv[None, :]
    c, s = jnp.cos(ang), jnp.sin(ang)
    z32, z64 = jnp.zeros_like(s), jnp.zeros((L, LANES - MLA_ROPE), F32)
    cos = jnp.concatenate([c, c, jnp.ones_like(z64)], axis=1)
    sa = jnp.concatenate([-s, z32, z64], axis=1)
    sb = jnp.concatenate([z32, s, z64], axis=1)
    return cos, sa, sb


def _mla_weights(w_in, w_uq, w_ukv):
    a, b = MLA_Q_LORA, MLA_Q_LORA + MLA_KV_LORA
    w_zc = jnp.concatenate([w_in[:, b + MLA_ROPE:], w_in[:, :b + MLA_ROPE],
                            jnp.zeros((D_MODEL, LANES - MLA_ROPE), w_in.dtype)], axis=1)
    uq = w_uq.reshape(MLA_Q_LORA, MLA_HEADS, MLA_NOPE + MLA_ROPE)
    uq = jnp.pad(uq, ((0, 0), (0, 0), (0, MLA_DK - MLA_NOPE - MLA_ROPE))).reshape(MLA_Q_LORA, MLA_HEADS * MLA_DK)
    ukv = w_ukv.reshape(MLA_KV_LORA, MLA_HEADS, MLA_NOPE + MLA_V)
    ukv = jnp.concatenate([ukv[:, :, :MLA_NOPE].reshape(MLA_KV_LORA, -1), ukv[:, :, MLA_NOPE:].reshape(MLA_KV_LORA, -1)], axis=1)
    return w_zc, uq, ukv


def _mla_fwd(h, w, lnw, pad):
    w_in, gq, gkv, w_uq, w_ukv, w_out = w
    L = h.shape[0]
    w_zc, uq, ukv = _mla_weights(w_in, w_uq, w_ukv)
    cos, sa, sb = _mla_tables(L, pad)
    proj = _matmul(h, w_zc, "nn", F32, "mla_proj")
    cqn, ckvn = _mla_norm_fwd(proj, gq[None, :], gkv[None, :])
    qf = _matmul(cqn, uq, "nn", F32, "mla_uq")
    kv = _matmul(ckvn, ukv, "nn", _MXU, "mla_ukv")
    q_full, k_full = _mla_rope_fwd(qf, kv, proj, cos, sa, sb)
    zero = jnp.zeros((MLA_HEADS, L), F32)
    ckey = jnp.where(jnp.arange(L)[None, :] < pad, -NEG, zero)
    nb = MLA_HEADS * MLA_NOPE // LANES
    o, lse = _attn_fwd(q_full, 0, k_full, 0, kv, nb, zero[:, :, None], ckey[:, None, :],
                       MLA_HEADS, MLA_DK, MLA_V, (MLA_NOPE + MLA_ROPE) ** -0.5, "mla_attn_fwd")
    y = _gate_fwd(o, proj, "mla_gate_fwd")
    mo = _matmul(y, w_out, "nn", F32, "mla_out")
    zl, hn = _ln_fwd(h, mo, lnw[0], lnw[1], "mla_ln_fwd")
    return hn, dict(h=h, proj=proj, cqn=cqn, ckvn=ckvn, kv=kv, q_full=q_full, k_full=k_full, zero=zero, ckey=ckey,
                    o=o, lse=lse, y=y, zl=zl, w_zc=w_zc, uq=uq, ukv=ukv, w_out=w_out, gq=gq, gkv=gkv,
                    tabs=(cos, sa, sb))


def _mla_bwd(s, dzl):
    H = MLA_HEADS
    dy = _matmul(dzl, s["w_out"], "nt", F32, "mla_dy")
    d_wout = _matmul(s["y"], dzl, "tn", F32, "mla_dwout")
    do, delta, dzg = _gate_bwd(dy, s["o"], s["proj"], H, "mla_gate_bwd")
    nb = H * MLA_NOPE // LANES
    dq_full, dk_full, dv, _, _ = _attn_bwd(s["q_full"], 0, s["k_full"], 0, s["kv"], nb, do,
                                           s["zero"][:, None, :], s["ckey"][:, :, None],
                                           s["lse"].reshape(H, 1, -1), delta.reshape(H, 1, -1),
                                           H, MLA_DK, MLA_V, (MLA_NOPE + MLA_ROPE) ** -0.5, "mla_attn_bwd")
    cos, sa, sb = s["tabs"]
    dqf, dkn, dkr = _mla_rope_bwd(dq_full, dk_full, cos, -sa, -sb)
    d_uq = _matmul(s["cqn"], dqf, "tn", F32, "mla_duq")
    dcqn = _matmul(dqf, s["uq"], "nt", F32, "mla_dcqn")
    dkv = jnp.concatenate([dkn, dv.astype(_MXU)], axis=1)
    d_ukv = _matmul(s["ckvn"], dkv, "tn", F32, "mla_dukv")
    dckvn = _matmul(dkv, s["ukv"], "nt", F32, "mla_dckvn")
    dcq, dckv, dgq, dgkv = _mla_norm_bwd(s["proj"], s["gq"][None, :], s["gkv"][None, :], dcqn, dckvn)
    dproj = jnp.concatenate([dzg, dcq, dckv, dkr], axis=1)
    d_wzc = _matmul(s["h"], dproj, "tn", F32, "mla_dwin")
    dh = _matmul(dproj, s["w_zc"], "nt", F32, "mla_dh")
    W = D_MODEL
    d_win = jnp.concatenate([d_wzc[:, W:W + MLA_Q_LORA + MLA_KV_LORA + MLA_ROPE], d_wzc[:, :W]], axis=1)
    d_wuq = d_uq.reshape(MLA_Q_LORA, H, MLA_DK)[:, :, :MLA_NOPE + MLA_ROPE].reshape(MLA_Q_LORA, -1)
    hk = H * MLA_NOPE
    d_wukv = jnp.concatenate([d_ukv[:, :hk].reshape(MLA_KV_LORA, H, MLA_NOPE),
                              d_ukv[:, hk:].reshape(MLA_KV_LORA, H, MLA_V)], axis=2).reshape(MLA_KV_LORA, -1)
    return dh, (d_win, dgq[0], dgkv[0], d_wuq, d_wukv, d_wout)


def _ret_tables(L, pad):
    pos = (jnp.arange(L) - pad).astype(F32)
    inv = 1.0 / (ROPE_BASE ** jnp.linspace(0.0, 1.0, RET_QK // 2, dtype=F32))
    ang = pos[:, None] * inv[None, :]
    return jnp.cos(ang), jnp.sin(ang)


def _ret_layer_fwd(h, w, lnw, pad):
    w_in, gn_g, w_out = w
    L = h.shape[0]
    q1 = 2 * _RET_QKW
    v1 = q1 + RET_HEADS * RET_V
    w_qkz = jnp.concatenate([w_in[:, :q1], w_in[:, v1:]], axis=1)
    cos, sin = _ret_tables(L, pad)
    qkz = _matmul(h, w_qkz, "nn", F32, "ret_qkz")
    v = _matmul(h, w_in[:, q1:v1], "nn", _MXU, "ret_v")
    qr, kr = _ret_rope_fwd(qkz, cos, sin, pad)
    o, states = _ret_fwd(qr, kr, v)
    y = _ret_gate_fwd(o, qkz, gn_g[None, :])
    mo = _matmul(y, w_out, "nn", F32, "ret_out")
    zl, hn = _ln_fwd(h, mo, lnw[0], lnw[1], "ret_ln_fwd")
    return hn, dict(h=h, qkz=qkz, v=v, qr=qr, kr=kr, o=o, states=states, y=y, zl=zl, w_in=w_in, w_out=w_out,
                    gn_g=gn_g, tabs=(cos, sin))


def _ret_layer_bwd(s, dzl, pad):
    dy = _matmul(dzl, s["w_out"], "nt", F32, "ret_dy")
    d_wout = _matmul(s["y"], dzl, "tn", F32, "ret_dwout")
    do, dzg, dgn = _ret_gate_bwd(dy, s["o"], s["qkz"], s["gn_g"][None, :])
    dqr, dkr, dv = _ret_bwd(s["qr"], s["kr"], s["v"], do, s["states"])
    cos, sin = s["tabs"]
    dq, dk = _ret_rope_bwd(dqr, dkr, cos, sin, pad)
    dproj = jnp.concatenate([dq, dk, dv, dzg], axis=1)
    d_win = _matmul(s["h"], dproj, "tn", F32, "ret_dwin")
    dh = _matmul(dproj, s["w_in"], "nt", F32, "ret_dh")
    return dh, (d_win, dgn[0], d_wout)


def _local_step(x, target, meta, fox, mla, ret, ln_g, ln_b):
    n_real = x.shape[0]
    first = -(-N_META // ROW_TILE)
    pad = first * ROW_TILE - N_META
    h0 = jnp.concatenate([jnp.zeros((pad, D_MODEL), F32), meta, x], axis=0)
    lnw = [(ln_g[i][None, :], ln_b[i][None, :]) for i in range(4)]

    h1, s0 = _fox_fwd(h0, (fox[0][0], fox[1][0], fox[2][0]), lnw[0], pad, "fox0")
    h2, s1 = _mla_fwd(h1, tuple(a[0] for a in mla), lnw[1], pad)
    h3, s2 = _ret_layer_fwd(h2, tuple(a[0] for a in ret), lnw[2], pad)
    h4, s3 = _fox_fwd(h3, (fox[0][1], fox[1][1], fox[2][1]), lnw[3], pad, "fox1")

    dh4, loss = _loss_head(h4, target, first)
    dz3, dg3, db3 = _ln_bwd(s3["zl"], lnw[3][0], dh4, None, "ln3_bwd")
    dm3, gf1 = _fox_bwd(s3, dz3, pad, "fox1")
    dz2, dg2, db2 = _ln_bwd(s2["zl"], lnw[2][0], dz3, dm3, "ln2_bwd")
    dm2, gr = _ret_layer_bwd(s2, dz2, pad)
    dz1, dg1, db1 = _ln_bwd(s1["zl"], lnw[1][0], dz2, dm2, "ln1_bwd")
    dm1, gm = _mla_bwd(s1, dz1)
    dz0, dg0, db0 = _ln_bwd(s0["zl"], lnw[0][0], dz1, dm1, "ln0_bwd")
    dm0, gf0 = _fox_bwd(s0, dz0, pad, "fox0")
    grad_x, grad_meta = _input_grads(dz0, dm0, first, n_real)

    grads = dict(
        meta=grad_meta,
        fox_w_in=jnp.stack([gf0[0], gf1[0]]), fox_b_f=jnp.stack([gf0[1], gf1[1]]),
        fox_w_out=jnp.stack([gf0[2], gf1[2]]),
        mla_w_in=gm[0][None], mla_q_norm=gm[1][None], mla_kv_norm=gm[2][None],
        mla_w_uq=gm[3][None], mla_w_ukv=gm[4][None], mla_w_out=gm[5][None],
        ret_w_in=gr[0][None], ret_gn_g=gr[1][None], ret_w_out=gr[2][None],
        ln_g=jnp.concatenate([dg0, dg1, dg2, dg3], axis=0),
        ln_b=jnp.concatenate([db0, db1, db2, db3], axis=0),
    )
    return loss, grad_x, grads


def _exchange(src, scatter, name):
    R = src.shape[-2]

    def body(src_ref, out_ref, send_sems, recv_sems, local_sem):
        x, y, c = lax.axis_index("x"), lax.axis_index("y"), lax.axis_index("c")
        me = 4 * x + 2 * y + c
        mine = pltpu.make_async_copy(src_ref.at[me] if scatter else src_ref, out_ref.at[me], local_sem)
        mine.start()
        copies = []
        for k in range(1, N_DEV):
            px = (x + (k >> 2)) % 2
            py = (y + ((k >> 1) & 1)) % 2
            pc = (c + (k & 1)) % 2
            peer = 4 * px + 2 * py + pc
            copies.append(pltpu.make_async_remote_copy(
                src_ref=src_ref.at[peer] if scatter else src_ref,
                dst_ref=out_ref.at[me],
                send_sem=send_sems.at[k - 1], recv_sem=recv_sems.at[k - 1],
                device_id=(px, py, pc), device_id_type=pl.DeviceIdType.MESH))
        for cp in copies:
            cp.start()
        for cp in copies:
            cp.wait()
        mine.wait()

    return pl.pallas_call(
        body, name=name,
        out_shape=jax.ShapeDtypeStruct((N_DEV, R, LANES), src.dtype),
        in_specs=[pl.BlockSpec(memory_space=pl.ANY)],
        out_specs=pl.BlockSpec(memory_space=pl.ANY),
        scratch_shapes=[pltpu.SemaphoreType.DMA((N_DEV - 1,)), pltpu.SemaphoreType.DMA((N_DEV - 1,)),
                        pltpu.SemaphoreType.DMA],
    )(src)


def _adamw(parts, w, m, v):
    R = w.shape[0]
    tr = _tile(R, 1024)

    def body(p_ref, w_ref, m_ref, v_ref, g_ref, d_ref, mo_ref, vo_ref):
        g = p_ref[0]
        for d in range(1, N_DEV):
            g = g + p_ref[d]
        mn = ADAM_B1 * m_ref[...] + (1.0 - ADAM_B1) * g
        vn = ADAM_B2 * v_ref[...] + (1.0 - ADAM_B2) * (g * g)
        m_hat = mn / (1.0 - ADAM_B1 ** ADAM_STEP)
        v_hat = vn / (1.0 - ADAM_B2 ** ADAM_STEP)
        g_ref[...] = g
        d_ref[...] = -ADAM_LR * (m_hat / (jnp.sqrt(v_hat) + ADAM_EPS) + ADAM_WD * w_ref[...])
        mo_ref[...] = mn
        vo_ref[...] = vn

    blk = pl.BlockSpec((tr, LANES), lambda i: (i, 0))
    sd = jax.ShapeDtypeStruct((R, LANES), F32)
    return pl.pallas_call(
        body, name="adamw",
        out_shape=(sd, sd, sd, sd),
        grid=(R // tr,),
        in_specs=[pl.BlockSpec((N_DEV, tr, LANES), lambda i: (0, i, 0)), blk, blk, blk],
        out_specs=(blk, blk, blk, blk),
        compiler_params=_params(("parallel",)),
    )(parts, w, m, v)


_SPECS = dict(
    meta=((16, 1024), 1),
    fox_w_in=((2, 1024, 4104), 2), fox_b_f=((2, 8), None), fox_w_out=((2, 1024, 1024), 1),
    mla_w_in=((1, 1024, 1728), 2), mla_q_norm=((1, 384), None), mla_kv_norm=((1, 256), None),
    mla_w_uq=((1, 384, 1536), 2), mla_w_ukv=((1, 256, 2048), 2), mla_w_out=((1, 1024, 1024), 1),
    ret_w_in=((1, 1024, 6144), 2), ret_gn_g=((1, 2048), 1), ret_w_out=((1, 2048, 1024), 1),
    ln_g=((4, 1024), None), ln_b=((4, 1024), None),
)
_NAMES = list(_SPECS)
_MXU_WEIGHTS = ["fox_w_in", "fox_w_out", "mla_w_in", "mla_w_uq", "mla_w_ukv", "mla_w_out", "ret_w_in", "ret_w_out"]
_F32_GATHERED = ["meta", "ret_gn_g"]
_PACK_ROWS = 1024


def _shard_shape(name):
    shape, ax = _SPECS[name]
    if ax is None:
        return shape
    return tuple(s // N_DEV if i == ax else s for i, s in enumerate(shape))


def _to_shards(name, full):
    shape, ax = _SPECS[name]
    if ax is None:
        return jnp.broadcast_to(full.reshape(1, -1), (N_DEV, full.size))
    parts = full.reshape(shape[:ax] + (N_DEV, shape[ax] // N_DEV) + shape[ax + 1:])
    return jnp.moveaxis(parts, ax, 0).reshape(N_DEV, -1)


def _from_shards(name, stacked):
    shape, ax = _SPECS[name]
    parts = stacked.reshape((N_DEV,) + shape[:ax] + (shape[ax] // N_DEV,) + shape[ax + 1:])
    return jnp.moveaxis(parts, 0, ax).reshape(shape)


def _pad_rows(flat):
    n = flat.shape[-1]
    unit = _PACK_ROWS * LANES
    total = -(-n // unit) * unit
    flat = jnp.pad(flat, [(0, 0)] * (flat.ndim - 1) + [(0, total - n)])
    return flat.reshape(flat.shape[:-1] + (total // LANES, LANES))


def kernel(x, meta, fox_w_in, fox_b_f, fox_w_out, mla_w_in, mla_q_norm, mla_kv_norm, mla_w_uq, mla_w_ukv, mla_w_out, ret_w_in, ret_gn_g, ret_w_out, ln_g, ln_b, loss_target, m_meta, m_fox_w_in, m_fox_b_f, m_fox_w_out, m_mla_w_in, m_mla_q_norm, m_mla_kv_norm, m_mla_w_uq, m_mla_w_ukv, m_mla_w_out, m_ret_w_in, m_ret_gn_g, m_ret_w_out, m_ln_g, m_ln_b, v_meta, v_fox_w_in, v_fox_b_f, v_fox_w_out, v_mla_w_in, v_mla_q_norm, v_mla_kv_norm, v_mla_w_uq, v_mla_w_ukv, v_mla_w_out, v_ret_w_in, v_ret_gn_g, v_ret_w_out, v_ln_g, v_ln_b):
    w = dict(meta=meta, fox_w_in=fox_w_in, fox_b_f=fox_b_f, fox_w_out=fox_w_out, mla_w_in=mla_w_in,
             mla_q_norm=mla_q_norm, mla_kv_norm=mla_kv_norm, mla_w_uq=mla_w_uq, mla_w_ukv=mla_w_ukv,
             mla_w_out=mla_w_out, ret_w_in=ret_w_in, ret_gn_g=ret_gn_g, ret_w_out=ret_w_out, ln_g=ln_g, ln_b=ln_b)
    m = dict(meta=m_meta, fox_w_in=m_fox_w_in, fox_b_f=m_fox_b_f, fox_w_out=m_fox_w_out, mla_w_in=m_mla_w_in,
             mla_q_norm=m_mla_q_norm, mla_kv_norm=m_mla_kv_norm, mla_w_uq=m_mla_w_uq, mla_w_ukv=m_mla_w_ukv,
             mla_w_out=m_mla_w_out, ret_w_in=m_ret_w_in, ret_gn_g=m_ret_gn_g, ret_w_out=m_ret_w_out, ln_g=m_ln_g, ln_b=m_ln_b)
    v = dict(meta=v_meta, fox_w_in=v_fox_w_in, fox_b_f=v_fox_b_f, fox_w_out=v_fox_w_out, mla_w_in=v_mla_w_in,
             mla_q_norm=v_mla_q_norm, mla_kv_norm=v_mla_kv_norm, mla_w_uq=v_mla_w_uq, mla_w_ukv=v_mla_w_ukv,
             mla_w_out=v_mla_w_out, ret_w_in=v_ret_w_in, ret_gn_g=v_ret_gn_g, ret_w_out=v_ret_w_out, ln_g=v_ln_g, ln_b=v_ln_b)

    def halves(a):
        return lax.bitcast_convert_type(a, jnp.bfloat16).reshape(-1)

    segs = [w[n].astype(jnp.bfloat16).reshape(-1) for n in _MXU_WEIGHTS] + [halves(w[n]) for n in _F32_GATHERED]
    sizes = [s.shape[0] for s in segs]
    gathered = _exchange(_pad_rows(jnp.concatenate(segs)), False, "weights_all_gather").reshape(N_DEV, -1)
    full, at = {}, 0
    for n, size in zip(_MXU_WEIGHTS + _F32_GATHERED, sizes):
        seg = gathered[:, at:at + size]
        at += size
        if n in _F32_GATHERED:
            seg = lax.bitcast_convert_type(seg.reshape(N_DEV, -1, 2), F32)
        full[n] = _from_shards(n, seg)
        if n in _MXU_WEIGHTS:
            full[n] = full[n].astype(_MXU)

    loss, grad_x, grads = _local_step(
        x[0], loss_target[0], full["meta"],
        (full["fox_w_in"], fox_b_f, full["fox_w_out"]),
        (full["mla_w_in"], mla_q_norm, mla_kv_norm, full["mla_w_uq"], full["mla_w_ukv"], full["mla_w_out"]),
        (full["ret_w_in"], full["ret_gn_g"], full["ret_w_out"]), ln_g, ln_b)

    def pack(tree, tail):
        return jnp.concatenate([tree[n].reshape(-1) for n in _NAMES] + [tail])

    outgoing = jnp.concatenate([_to_shards(n, grads[n]) for n in _NAMES]
                               + [jnp.broadcast_to(loss.reshape(1, 1), (N_DEV, 1))], axis=1)
    parts = _exchange(_pad_rows(outgoing), True, "grads_all_to_all")
    zero1 = jnp.zeros((1,), F32)
    g, d, mn, vn = _adamw(parts, _pad_rows(pack(w, zero1)), _pad_rows(pack(m, zero1)), _pad_rows(pack(v, zero1)))

    def unpack(buf):
        flat, out, at = buf.reshape(-1), {}, 0
        for n in _NAMES:
            shp = _shard_shape(n)
            size = math.prod(shp)
            out[n] = flat[at:at + size].reshape(shp)
            at += size
        return out, flat[at]

    g, total_loss = unpack(g)
    d, _ = unpack(d)
    mn, _ = unpack(mn)
    vn, _ = unpack(vn)
    names = _NAMES[1:]
    return (total_loss, grad_x[None], g["meta"], *[g[n] for n in names], d["meta"], *[d[n] for n in names],
            mn["meta"], *[mn[n] for n in names], vn["meta"], *[vn[n] for n in names])
```

```python
import functools
import math

import numpy as np
import jax
import jax.numpy as jnp
from jax import lax
from jax.experimental import pallas as pl
from jax.experimental.pallas import tpu as pltpu

F32 = jnp.float32
_MXU = jnp.bfloat16

N_DEV = 8
N_META = 16
D_MODEL = 1024
ROW_TILE = 512
LANES = 128
CHUNK = 128
NEG = -1e30
VMEM_LIMIT = 60 * 1024 * 1024

FOX_HEADS = 8
MLA_HEADS = 8
MLA_NOPE, MLA_ROPE, MLA_V = 128, 64, 128
MLA_Q_LORA, MLA_KV_LORA = 384, 256
MLA_DK = 256
RET_HEADS, RET_QK, RET_V = 4, 256, 512
ROPE_BASE = 10000.0
ALPHA = (2 * 4) ** 0.25
NORM_EPS = 1e-5

ADAM_LR, ADAM_B1, ADAM_B2, ADAM_EPS, ADAM_WD, ADAM_STEP = 0.001, 0.9, 0.999, 1e-08, 0.01, 10


def _params(sem, vmem=VMEM_LIMIT):
    return pltpu.CompilerParams(dimension_semantics=sem, vmem_limit_bytes=vmem)


def _tile(n, pref):
    if n <= pref:
        return n
    t = (pref // LANES) * LANES
    while n % t:
        t -= LANES
    return t


_MM_TILES = {"nn": (512, 2048, 2048), "nt": (512, 1024, 4224), "tn": (1024, 2048, 768)}


def _matmul(a, b, mode, out_dtype, name):
    if mode == "nn":
        (M, K), (K2, N) = a.shape, b.shape
    elif mode == "nt":
        (M, K), (N, K2) = a.shape, b.shape
    else:
        (K, M), (K2, N) = a.shape, b.shape
    assert K == K2, (a.shape, b.shape, mode)
    tm, tn, tk = (_tile(d, p) for d, p in zip((M, N, K), _MM_TILES[mode]))
    nk = K // tk
    if mode == "nn":
        a_spec = pl.BlockSpec((tm, tk), lambda i, j, k: (i, k))
        b_spec = pl.BlockSpec((tk, tn), lambda i, j, k: (k, j))
        dn = (((1,), (0,)), ((), ()))
    elif mode == "nt":
        a_spec = pl.BlockSpec((tm, tk), lambda i, j, k: (i, k))
        b_spec = pl.BlockSpec((tn, tk), lambda i, j, k: (j, k))
        dn = (((1,), (1,)), ((), ()))
    else:
        a_spec = pl.BlockSpec((tk, tm), lambda i, j, k: (k, i))
        b_spec = pl.BlockSpec((tk, tn), lambda i, j, k: (k, j))
        dn = (((0,), (0,)), ((), ()))

    def body(a_ref, b_ref, o_ref, acc_ref):
        k = pl.program_id(2)
        part = lax.dot_general(a_ref[...].astype(_MXU), b_ref[...].astype(_MXU), dn,
                               preferred_element_type=F32)

        @pl.when(k == 0)
        def _():
            acc_ref[...] = part

        @pl.when(k > 0)
        def _():
            acc_ref[...] += part

        @pl.when(k == nk - 1)
        def _():
            o_ref[...] = acc_ref[...].astype(out_dtype)

    return pl.pallas_call(
        body, name=name,
        out_shape=jax.ShapeDtypeStruct((M, N), out_dtype),
        grid=(M // tm, N // tn, nk),
        in_specs=[a_spec, b_spec],
        out_specs=pl.BlockSpec((tm, tn), lambda i, j, k: (i, j)),
        scratch_shapes=[pltpu.VMEM((tm, tn), F32)],
        compiler_params=_params(("parallel", "parallel", "arbitrary")),
    )(a, b)


def _rows(w, off=0):
    return pl.BlockSpec((ROW_TILE, w), lambda i: (i + off, 0))


def _whole(shape):
    return pl.BlockSpec(shape, lambda i: (0,) * len(shape))


def _silu(z):
    return z * jax.nn.sigmoid(z)


def _dsilu(z):
    s = jax.nn.sigmoid(z)
    return s * (1.0 + z * (1.0 - s))


def _ln_fwd(h, y, g, b, name):
    L, D = h.shape

    def body(h_ref, y_ref, g_ref, b_ref, z_ref, o_ref):
        z = ALPHA * h_ref[...] + y_ref[...]
        mu = jnp.mean(z, axis=-1, keepdims=True)
        zc = z - mu
        var = jnp.mean(zc * zc, axis=-1, keepdims=True)
        z_ref[...] = z
        o_ref[...] = zc * lax.rsqrt(var + NORM_EPS) * g_ref[...] + b_ref[...]

    return pl.pallas_call(
        body, name=name,
        out_shape=(jax.ShapeDtypeStruct((L, D), F32), jax.ShapeDtypeStruct((L, D), F32)),
        grid=(L // ROW_TILE,),
        in_specs=[_rows(D), _rows(D), _whole((1, D)), _whole((1, D))],
        out_specs=(_rows(D), _rows(D)),
        compiler_params=_params(("parallel",)),
    )(h, y, g, b)


def _ln_bwd(zl, g, ga, gb, name):
    L, D = zl.shape
    two = gb is not None

    def body(*refs):
        if two:
            z_ref, g_ref, ga_ref, gb_ref, dz_ref, dg_ref, db_ref = refs
            gout = ALPHA * ga_ref[...] + gb_ref[...]
        else:
            z_ref, g_ref, ga_ref, dz_ref, dg_ref, db_ref = refs
            gout = ga_ref[...]
        z = z_ref[...]
        mu = jnp.mean(z, axis=-1, keepdims=True)
        zc = z - mu
        var = jnp.mean(zc * zc, axis=-1, keepdims=True)
        rstd = lax.rsqrt(var + NORM_EPS)
        xhat = zc * rstd
        dxh = gout * g_ref[...]
        m1 = jnp.mean(dxh, axis=-1, keepdims=True)
        m2 = jnp.mean(dxh * xhat, axis=-1, keepdims=True)
        dz_ref[...] = rstd * (dxh - m1 - xhat * m2)

        @pl.when(pl.program_id(0) == 0)
        def _():
            dg_ref[...] = jnp.zeros_like(dg_ref)
            db_ref[...] = jnp.zeros_like(db_ref)

        dg_ref[...] += jnp.sum(gout * xhat, axis=0, keepdims=True)
        db_ref[...] += jnp.sum(gout, axis=0, keepdims=True)

    ins = [zl, g, ga] + ([gb] if two else [])
    return pl.pallas_call(
        body, name=name,
        out_shape=(jax.ShapeDtypeStruct((L, D), F32), jax.ShapeDtypeStruct((1, D), F32),
                   jax.ShapeDtypeStruct((1, D), F32)),
        grid=(L // ROW_TILE,),
        in_specs=[_rows(D), _whole((1, D)), _rows(D)] + ([_rows(D)] if two else []),
        out_specs=(_rows(D), _whole((1, D)), _whole((1, D))),
        compiler_params=_params(("arbitrary",)),
    )(*ins)


def _loss_head(h, target, first):
    L, D = h.shape

    def body(h_ref, t_ref, dh_ref, loss_ref):
        i = pl.program_id(0)

        @pl.when(i == 0)
        def _():
            loss_ref[...] = jnp.zeros_like(loss_ref)

        @pl.when(i < first)
        def _():
            dh_ref[...] = jnp.zeros_like(dh_ref)

        @pl.when(i >= first)
        def _():
            err = h_ref[...] - t_ref[...]
            dh_ref[...] = err * (1.0 / D)
            part = jnp.sum(jnp.sum(err * err, axis=-1, keepdims=True) * (1.0 / D), axis=0, keepdims=True)
            loss_ref[...] += 0.5 * part

    return pl.pallas_call(
        body, name="loss_head",
        out_shape=(jax.ShapeDtypeStruct((L, D), F32), jax.ShapeDtypeStruct((1, 1), F32)),
        grid=(L // ROW_TILE,),
        in_specs=[_rows(D), pl.BlockSpec((ROW_TILE, D), lambda i: (jnp.maximum(i - first, 0), 0))],
        out_specs=(_rows(D), _whole((1, 1))),
        compiler_params=_params(("arbitrary",)),
    )(h, target)


def _input_grads(dzl, dmix, first, n_real):
    L, D = dzl.shape

    def body(a_ref, b_ref, gx_ref, gm_ref):
        i = pl.program_id(0)
        val = ALPHA * a_ref[...] + b_ref[...]

        @pl.when(i == first - 1)
        def _():
            gm_ref[...] = val[ROW_TILE - N_META:, :]

        @pl.when(i >= first)
        def _():
            gx_ref[...] = val

    return pl.pallas_call(
        body, name="input_grads",
        out_shape=(jax.ShapeDtypeStruct((n_real, D), F32), jax.ShapeDtypeStruct((N_META, D), F32)),
        grid=(L // ROW_TILE,),
        in_specs=[_rows(D), _rows(D)],
        out_specs=(pl.BlockSpec((ROW_TILE, D), lambda i: (jnp.maximum(i - first, 0), 0)),
                   _whole((N_META, D))),
        compiler_params=_params(("arbitrary",)),
    )(dzl, dmix)


def _gate_fwd(o, zsrc, name):
    L, W = o.shape
    Wz = zsrc.shape[1]

    def body(o_ref, z_ref, y_ref):
        y_ref[...] = (o_ref[...] * _silu(z_ref[:, :W])).astype(_MXU)

    return pl.pallas_call(
        body, name=name,
        out_shape=jax.ShapeDtypeStruct((L, W), _MXU),
        grid=(L // ROW_TILE,),
        in_specs=[_rows(W), _rows(Wz)],
        out_specs=_rows(W),
        compiler_params=_params(("parallel",)),
    )(o, zsrc)


def _gate_bwd(dy, o, zsrc, heads, name):
    L, W = o.shape
    Wz = zsrc.shape[1]
    hd = W // heads

    def body(dy_ref, o_ref, z_ref, do_ref, dl_ref, dz_ref):
        z = z_ref[:, :W]
        dy = dy_ref[...]
        o = o_ref[...]
        do = dy * _silu(z)
        do_ref[...] = do.astype(_MXU)
        dz_ref[...] = (dy * o * _dsilu(z)).astype(_MXU)
        prod = do * o
        for h in range(heads):
            dl_ref[h] = jnp.sum(prod[:, h * hd:(h + 1) * hd], axis=-1, keepdims=True)

    return pl.pallas_call(
        body, name=name,
        out_shape=(jax.ShapeDtypeStruct((L, W), _MXU), jax.ShapeDtypeStruct((heads, L, 1), F32),
                   jax.ShapeDtypeStruct((L, W), _MXU)),
        grid=(L // ROW_TILE,),
        in_specs=[_rows(W), _rows(W), _rows(Wz)],
        out_specs=(_rows(W), pl.BlockSpec((heads, ROW_TILE, 1), lambda i: (0, i, 0)), _rows(W)),
        compiler_params=_params(("parallel",)),
    )(dy, o, zsrc)


def _split3(x):
    hi = x.astype(_MXU)
    r1 = x - hi.astype(F32)
    mid = r1.astype(_MXU)
    lo = (r1 - mid.astype(F32)).astype(_MXU)
    return hi, mid, lo


def _tri_cumsum(x, reverse):
    T = x.shape[0]
    r = lax.broadcasted_iota(jnp.int32, (T, T), 0)
    c = lax.broadcasted_iota(jnp.int32, (T, T), 1)
    tri = jnp.where((c >= r) if reverse else (c <= r), 1.0, 0.0).astype(_MXU)
    out = jnp.zeros(x.shape, F32)
    for part in _split3(x):
        out = out + jnp.dot(tri, part, preferred_element_type=F32)
    return out


def _fox_cum(zf, bias, pad, name):
    L, Wz = zf.shape
    off = Wz // LANES - 1

    def body(f_ref, b_ref, c_ref, carry_ref):
        i = pl.program_id(0)

        @pl.when(i == 0)
        def _():
            carry_ref[...] = jnp.zeros_like(carry_ref)

        x = f_ref[...] + b_ref[...]
        logf = jnp.minimum(x, 0.0) - jnp.log(1.0 + jnp.exp(-jnp.abs(x)))
        row = i * ROW_TILE + lax.broadcasted_iota(jnp.int32, logf.shape, 0)
        logf = jnp.where(row >= pad, logf, 0.0)
        cum = _tri_cumsum(logf, False) + carry_ref[...]
        c_ref[...] = cum
        carry_ref[...] = cum[ROW_TILE - 1:, :]

    return pl.pallas_call(
        body, name=name,
        out_shape=jax.ShapeDtypeStruct((L, LANES), F32),
        grid=(L // ROW_TILE,),
        in_specs=[pl.BlockSpec((ROW_TILE, LANES), lambda i: (i, off)), _whole((1, LANES))],
        out_specs=_rows(LANES),
        scratch_shapes=[pltpu.VMEM((1, LANES), F32)],
        compiler_params=_params(("arbitrary",)),
    )(zf, bias)


def _fox_cum_bwd(dcum, zf, bias, pad, name):
    L, Wz = zf.shape
    off = Wz // LANES - 1
    n = L // ROW_TILE

    def body(d_ref, f_ref, b_ref, df_ref, db_ref, carry_ref):
        i = pl.program_id(0)

        @pl.when(i == 0)
        def _():
            carry_ref[...] = jnp.zeros_like(carry_ref)
            db_ref[...] = jnp.zeros_like(db_ref)

        rc = _tri_cumsum(d_ref[...], True) + carry_ref[...]
        carry_ref[...] = rc[:1, :]
        x = f_ref[...] + b_ref[...]
        row = (n - 1 - i) * ROW_TILE + lax.broadcasted_iota(jnp.int32, x.shape, 0)
        df = jnp.where(row >= pad, rc * jax.nn.sigmoid(-x), 0.0)
        df_ref[...] = df.astype(_MXU)
        db_ref[...] += jnp.sum(df, axis=0, keepdims=True)

    return pl.pallas_call(
        body, name=name,
        out_shape=(jax.ShapeDtypeStruct((L, LANES), _MXU), jax.ShapeDtypeStruct((1, LANES), F32)),
        grid=(n,),
        in_specs=[pl.BlockSpec((ROW_TILE, LANES), lambda i: (n - 1 - i, 0)),
                  pl.BlockSpec((ROW_TILE, LANES), lambda i: (n - 1 - i, off)), _whole((1, LANES))],
        out_specs=(pl.BlockSpec((ROW_TILE, LANES), lambda i: (n - 1 - i, 0)), _whole((1, LANES))),
        scratch_shapes=[pltpu.VMEM((1, LANES), F32)],
        compiler_params=_params(("arbitrary",)),
    )(dcum, zf, bias)


LOG2E = 1.4426950408889634
LN2 = 0.6931471805599453


def _bias_terms(ct, pad):
    H, L = ct.shape
    T = ROW_TILE
    n = L // T
    c2 = ct * LOG2E
    k2 = jnp.where(jnp.arange(L)[None, :] < pad, -NEG, c2)
    return dict(cq_col=c2[:, :, None], ck_col=k2[:, :, None], cq_row=c2.reshape(H, n, 1, T),
                ck_row=k2.reshape(H, n, 1, T), ref=c2[:, ::T].reshape(H, n, 1, 1))


def _attn_fwd(q, qoff, k, koff, v, voff, bias, heads, dk, dv, scale, name):
    L = q.shape[0]
    T = ROW_TILE
    n = L // T
    c = scale * LOG2E
    dn_qk = (((1,), (1,)), ((), ()))

    def body(q_ref, k_ref, v_ref, cq_ref, ck_ref, ref_ref, o_ref, lse_ref, s_a, s_b):
        i = pl.program_id(1)
        qb = q_ref[...]
        ref = ref_ref[...]

        def scores(j, dst):
            start = pl.multiple_of(j * T, T)
            dst[...] = lax.dot_general(qb, k_ref[pl.ds(start, T), :], dn_qk, preferred_element_type=F32)

        def soft(j, carry, cur, diag):
            m, l, acc = carry
            start = pl.multiple_of(j * T, T)
            vj = v_ref[pl.ds(start, T), :]
            t = cur[...] * c - (ck_ref[j] - ref)
            if diag:
                r = lax.broadcasted_iota(jnp.int32, (T, T), 0)
                cc = lax.broadcasted_iota(jnp.int32, (T, T), 1)
                t = jnp.where(cc <= r, t, NEG)
            m_new = jnp.maximum(m, jnp.max(t, axis=1, keepdims=True))
            p = jnp.exp2(t - m_new)
            a = jnp.exp2(m - m_new)
            l = a * l + jnp.sum(p, axis=1, keepdims=True)
            acc = a * acc + jnp.dot(p.astype(_MXU), vj, preferred_element_type=F32)
            return m_new, l, acc

        def step(j, carry, cur, nxt):
            scores(j + 1, nxt)
            return soft(j, carry, cur, False)

        def pair(jj, carry):
            return step(2 * jj + 1, step(2 * jj, carry, s_a, s_b), s_b, s_a)

        scores(0, s_a)
        init = (jnp.full((T, 1), -jnp.inf, F32), jnp.zeros((T, 1), F32), jnp.zeros((T, dv), F32))
        carry = lax.fori_loop(0, i // 2, pair, init)
        odd = (i % 2) == 1
        carry = lax.cond(odd, lambda cy: step(i - 1, cy, s_a, s_b), lambda cy: cy, carry)
        m, l, acc = lax.cond(odd, lambda cy: soft(i, cy, s_b, True), lambda cy: soft(i, cy, s_a, True), carry)
        o_ref[...] = acc / l
        lse_ref[...] = m + jnp.log2(l) + (cq_ref[...] - ref)

    return pl.pallas_call(
        body, name=name,
        out_shape=(jax.ShapeDtypeStruct((L, heads * dv), F32), jax.ShapeDtypeStruct((heads, L, 1), F32)),
        grid=(heads, n),
        in_specs=[pl.BlockSpec((T, dk), lambda h, i: (i, qoff + h)),
                  pl.BlockSpec((L, dk), lambda h, i: (0, koff + h)),
                  pl.BlockSpec((L, dv), lambda h, i: (0, voff + h)),
                  pl.BlockSpec((None, T, 1), lambda h, i: (h, i, 0)),
                  pl.BlockSpec((None, n, 1, T), lambda h, i: (h, 0, 0, 0)),
                  pl.BlockSpec((None, None, 1, 1), lambda h, i: (h, i, 0, 0))],
        out_specs=(pl.BlockSpec((T, dv), lambda h, i: (i, h)),
                   pl.BlockSpec((None, T, 1), lambda h, i: (h, i, 0))),
        scratch_shapes=[pltpu.VMEM((T, T), F32), pltpu.VMEM((T, T), F32)],
        compiler_params=_params(("parallel", "arbitrary")),
    )(q, k, v, bias["cq_col"], bias["ck_row"], bias["ref"])


def _attn_bwd(q, qoff, k, koff, v, voff, do, bias, lse_row, delta_row, heads, dk, dv, scale, name):
    L = q.shape[0]
    T = ROW_TILE
    n = L // T
    c = scale * LOG2E
    dn_nt = (((1,), (1,)), ((), ()))
    dn_tn = (((0,), (0,)), ((), ()))

    def body(q_ref, k_ref, v_ref, do_ref, cq_ref, ck_ref, ref_ref, lse_ref, dl_ref,
             dq_hbm, dk_ref, dv_ref, dck_ref, dcq_ref, dq_acc, sem, st_a, dp_a, st_b, dp_b):
        h = pl.program_id(0)
        j = pl.program_id(1)

        @pl.when(j == 0)
        def _():
            dq_acc[...] = jnp.zeros_like(dq_acc)
            dcq_ref[...] = jnp.zeros_like(dcq_ref)

        kb = k_ref[...]
        vb = v_ref[...]
        ref = ref_ref[...]
        bcol = ck_ref[...] - ref

        def front(i, st_dst, dp_dst):
            start = pl.multiple_of(jnp.minimum(i, n - 1) * T, T)
            st_dst[...] = lax.dot_general(kb, q_ref[pl.ds(start, T), :], dn_nt, preferred_element_type=F32)
            dp_dst[...] = lax.dot_general(vb, do_ref[pl.ds(start, T), :], dn_nt, preferred_element_type=F32)

        def back(i, carry, st_cur, dp_cur, diag):
            dk_a, dv_a, dck_a = carry
            start = pl.multiple_of(i * T, T)
            qi = q_ref[pl.ds(start, T), :]
            doi = do_ref[pl.ds(start, T), :]
            arow = (cq_ref[i] - ref) - lse_ref[i]
            st = st_cur[...] * c + arow - bcol
            if diag:
                r = lax.broadcasted_iota(jnp.int32, (T, T), 0)
                cc = lax.broadcasted_iota(jnp.int32, (T, T), 1)
                st = jnp.where(r <= cc, st, NEG)
            pt = jnp.exp2(st)
            dv_a = dv_a + jnp.dot(pt.astype(_MXU), doi, preferred_element_type=F32)
            dst = pt * (dp_cur[...] - dl_ref[i])
            dck_a = dck_a - jnp.sum(dst, axis=1, keepdims=True)
            dcq_ref[i] += jnp.sum(dst, axis=0, keepdims=True)
            dsb = (dst * scale).astype(_MXU)
            dk_a = dk_a + jnp.dot(dsb, qi, preferred_element_type=F32)
            dq_acc[pl.ds(start, T), :] += lax.dot_general(dsb, kb, dn_tn, preferred_element_type=F32)
            return dk_a, dv_a, dck_a

        buf_a, buf_b = (st_a, dp_a), (st_b, dp_b)

        def step(i, carry, cur, nxt, diag=False):
            front(i + 1, *nxt)
            return back(i, carry, *cur, diag)

        def pair(t, carry):
            i0 = j + 1 + 2 * t
            return step(i0 + 1, step(i0, carry, buf_b, buf_a), buf_a, buf_b)

        front(j, *buf_a)
        init = (jnp.zeros((T, dk), F32), jnp.zeros((T, dv), F32), jnp.zeros((T, 1), F32))
        carry = step(j, init, buf_a, buf_b, True)
        rest = n - 1 - j
        carry = lax.fori_loop(0, rest // 2, pair, carry)
        dk_a, dv_a, dck_a = lax.cond((rest % 2) == 1, lambda cy: step(n - 1, cy, buf_b, buf_a),
                                     lambda cy: cy, carry)
        dk_ref[...] = dk_a
        dv_ref[...] = dv_a
        dck_ref[...] = dck_a

        @pl.when(j == n - 1)
        def _():
            cp = pltpu.make_async_copy(dq_acc, dq_hbm.at[:, pl.ds(pl.multiple_of(h * dk, dk), dk)], sem)
            cp.start()
            cp.wait()

    return pl.pallas_call(
        body, name=name,
        out_shape=(jax.ShapeDtypeStruct((L, heads * dk), F32), jax.ShapeDtypeStruct((L, heads * dk), F32),
                   jax.ShapeDtypeStruct((L, heads * dv), F32), jax.ShapeDtypeStruct((heads, L, 1), F32),
                   jax.ShapeDtypeStruct((heads, n, 1, T), F32)),
        grid=(heads, n),
        in_specs=[pl.BlockSpec((L, dk), lambda h, j: (0, qoff + h)),
                  pl.BlockSpec((T, dk), lambda h, j: (j, koff + h)),
                  pl.BlockSpec((T, dv), lambda h, j: (j, voff + h)),
                  pl.BlockSpec((L, dv), lambda h, j: (0, h)),
                  pl.BlockSpec((None, n, 1, T), lambda h, j: (h, 0, 0, 0)),
                  pl.BlockSpec((None, T, 1), lambda h, j: (h, j, 0)),
                  pl.BlockSpec((None, None, 1, 1), lambda h, j: (h, j, 0, 0)),
                  pl.BlockSpec((None, n, 1, T), lambda h, j: (h, 0, 0, 0)),
                  pl.BlockSpec((None, n, 1, T), lambda h, j: (h, 0, 0, 0))],
        out_specs=(pl.BlockSpec(memory_space=pl.ANY),
                   pl.BlockSpec((T, dk), lambda h, j: (j, h)),
                   pl.BlockSpec((T, dv), lambda h, j: (j, h)),
                   pl.BlockSpec((None, T, 1), lambda h, j: (h, j, 0)),
                   pl.BlockSpec((None, n, 1, T), lambda h, j: (h, 0, 0, 0))),
        scratch_shapes=[pltpu.VMEM((L, dk), F32), pltpu.SemaphoreType.DMA] + [pltpu.VMEM((T, T), F32)] * 4,
        compiler_params=_params(("arbitrary", "arbitrary")),
    )(q, k, v, do, bias["cq_row"], bias["ck_col"], bias["ref"], lse_row, delta_row)


_CQ0, _CKV0, _KR0 = D_MODEL, D_MODEL + MLA_Q_LORA, D_MODEL + MLA_Q_LORA + MLA_KV_LORA
_MLA_PROJ = _KR0 + LANES


def _rms(x, g):
    ms = jnp.mean(x * x, axis=-1, keepdims=True)
    return x * lax.rsqrt(ms + NORM_EPS) * g


def _rms_bwd(x, g, dy):
    ms = jnp.mean(x * x, axis=-1, keepdims=True)
    r = lax.rsqrt(ms + NORM_EPS)
    xh = x * r
    dxh = dy * g
    dx = r * (dxh - xh * jnp.mean(dxh * xh, axis=-1, keepdims=True))
    return dx, jnp.sum(dy * xh, axis=0, keepdims=True)


def _mla_norm_fwd(proj, gq, gkv):
    L = proj.shape[0]

    def body(p_ref, gq_ref, gkv_ref, cq_ref, ckv_ref):
        cq_ref[...] = _rms(p_ref[:, _CQ0:_CKV0], gq_ref[...]).astype(_MXU)
        ckv_ref[...] = _rms(p_ref[:, _CKV0:_KR0], gkv_ref[...]).astype(_MXU)

    return pl.pallas_call(
        body, name="mla_norm_fwd",
        out_shape=(jax.ShapeDtypeStruct((L, MLA_Q_LORA), _MXU), jax.ShapeDtypeStruct((L, MLA_KV_LORA), _MXU)),
        grid=(L // ROW_TILE,),
        in_specs=[_rows(_MLA_PROJ), _whole((1, MLA_Q_LORA)), _whole((1, MLA_KV_LORA))],
        out_specs=(_rows(MLA_Q_LORA), _rows(MLA_KV_LORA)),
        compiler_params=_params(("parallel",)),
    )(proj, gq, gkv)


def _mla_norm_bwd(proj, gq, gkv, dcqn, dckvn):
    L = proj.shape[0]

    def body(p_ref, gq_ref, gkv_ref, dq_ref, dkv_ref, dcq_ref, dckv_ref, dgq_ref, dgkv_ref):
        @pl.when(pl.program_id(0) == 0)
        def _():
            dgq_ref[...] = jnp.zeros_like(dgq_ref)
            dgkv_ref[...] = jnp.zeros_like(dgkv_ref)

        dx, dg = _rms_bwd(p_ref[:, _CQ0:_CKV0], gq_ref[...], dq_ref[...])
        dcq_ref[...] = dx.astype(_MXU)
        dgq_ref[...] += dg
        dx, dg = _rms_bwd(p_ref[:, _CKV0:_KR0], gkv_ref[...], dkv_ref[...])
        dckv_ref[...] = dx.astype(_MXU)
        dgkv_ref[...] += dg

    return pl.pallas_call(
        body, name="mla_norm_bwd",
        out_shape=(jax.ShapeDtypeStruct((L, MLA_Q_LORA), _MXU), jax.ShapeDtypeStruct((L, MLA_KV_LORA), _MXU),
                   jax.ShapeDtypeStruct((1, MLA_Q_LORA), F32), jax.ShapeDtypeStruct((1, MLA_KV_LORA), F32)),
        grid=(L // ROW_TILE,),
        in_specs=[_rows(_MLA_PROJ), _whole((1, MLA_Q_LORA)), _whole((1, MLA_KV_LORA)),
                  _rows(MLA_Q_LORA), _rows(MLA_KV_LORA)],
        out_specs=(_rows(MLA_Q_LORA), _rows(MLA_KV_LORA), _whole((1, MLA_Q_LORA)), _whole((1, MLA_KV_LORA))),
        compiler_params=_params(("arbitrary",)),
    )(proj, gq, gkv, dcqn, dckvn)


def _rot_tile(t, cos, sa, sb):
    half = MLA_ROPE // 2
    return t * cos + pltpu.roll(t, LANES - half, 1) * sa + pltpu.roll(t, half, 1) * sb


def _mla_rope_fwd(qf, kv, proj, cos, sa, sb):
    L = qf.shape[0]
    H = MLA_HEADS

    def body(q_ref, kv_ref, p_ref, c_ref, sa_ref, sb_ref, qo_ref, ko_ref):
        cos, sa, sb = c_ref[...], sa_ref[...], sb_ref[...]
        kr = _rot_tile(p_ref[:, _KR0:_KR0 + LANES], cos, sa, sb).astype(_MXU)
        for h in range(H):
            b = h * MLA_DK
            qo_ref[:, b:b + LANES] = q_ref[:, b:b + LANES].astype(_MXU)
            qo_ref[:, b + LANES:b + 2 * LANES] = _rot_tile(q_ref[:, b + LANES:b + 2 * LANES], cos, sa, sb).astype(_MXU)
            ko_ref[:, b:b + LANES] = kv_ref[:, h * LANES:(h + 1) * LANES]
            ko_ref[:, b + LANES:b + 2 * LANES] = kr

    W = H * MLA_DK
    return pl.pallas_call(
        body, name="mla_rope_fwd",
        out_shape=(jax.ShapeDtypeStruct((L, W), _MXU), jax.ShapeDtypeStruct((L, W), _MXU)),
        grid=(L // ROW_TILE,),
        in_specs=[_rows(W), _rows(kv.shape[1]), _rows(_MLA_PROJ), _rows(LANES), _rows(LANES), _rows(LANES)],
        out_specs=(_rows(W), _rows(W)),
        compiler_params=_params(("parallel",)),
    )(qf, kv, proj, cos, sa, sb)


def _mla_rope_bwd(dq_full, dk_full, cos, sa, sb):
    L = dq_full.shape[0]
    H = MLA_HEADS
    W = H * MLA_DK

    def body(dq_ref, dk_ref, c_ref, sa_ref, sb_ref, dqf_ref, dkn_ref, dkr_ref):
        cos, sa, sb = c_ref[...], sa_ref[...], sb_ref[...]
        lane = lax.broadcasted_iota(jnp.int32, (ROW_TILE, LANES), 1)
        live = lane < MLA_ROPE
        krs = jnp.zeros((ROW_TILE, LANES), F32)
        for h in range(H):
            b = h * MLA_DK
            dqf_ref[:, b:b + LANES] = dq_ref[:, b:b + LANES].astype(_MXU)
            dqr = _rot_tile(dq_ref[:, b + LANES:b + 2 * LANES], cos, sa, sb)
            dqf_ref[:, b + LANES:b + 2 * LANES] = jnp.where(live, dqr, 0.0).astype(_MXU)
            dkn_ref[:, h * LANES:(h + 1) * LANES] = dk_ref[:, b:b + LANES].astype(_MXU)
            krs = krs + dk_ref[:, b + LANES:b + 2 * LANES]
        dkr_ref[...] = jnp.where(live, _rot_tile(krs, cos, sa, sb), 0.0).astype(_MXU)

    return pl.pallas_call(
        body, name="mla_rope_bwd",
        out_shape=(jax.ShapeDtypeStruct((L, W), _MXU), jax.ShapeDtypeStruct((L, H * LANES), _MXU),
                   jax.ShapeDtypeStruct((L, LANES), _MXU)),
        grid=(L // ROW_TILE,),
        in_specs=[_rows(W), _rows(W), _rows(LANES), _rows(LANES), _rows(LANES)],
        out_specs=(_rows(W), _rows(H * LANES), _rows(LANES)),
        compiler_params=_params(("parallel",)),
    )(dq_full, dk_full, cos, sa, sb)


_RET_QKW = RET_HEADS * RET_QK


def _ret_rope_fwd(qkz, cos, sin, pad):
    L = qkz.shape[0]
    hh = RET_QK // 2
    kscale = RET_QK ** -0.5

    def body(p_ref, c_ref, s_ref, q_ref, k_ref):
        cos, sin = c_ref[...], s_ref[...]
        row = pl.program_id(0) * ROW_TILE + lax.broadcasted_iota(jnp.int32, (ROW_TILE, hh), 0)
        keep = row >= pad
        for h in range(RET_HEADS):
            b = h * RET_QK
            t1, t2 = p_ref[:, b:b + hh], p_ref[:, b + hh:b + 2 * hh]
            q_ref[:, b:b + hh] = (t1 * cos - t2 * sin).astype(_MXU)
            q_ref[:, b + hh:b + 2 * hh] = (t2 * cos + t1 * sin).astype(_MXU)
            t1, t2 = p_ref[:, _RET_QKW + b:_RET_QKW + b + hh], p_ref[:, _RET_QKW + b + hh:_RET_QKW + b + 2 * hh]
            k_ref[:, b:b + hh] = jnp.where(keep, (t1 * cos - t2 * sin) * kscale, 0.0)
            k_ref[:, b + hh:b + 2 * hh] = jnp.where(keep, (t2 * cos + t1 * sin) * kscale, 0.0)

    return pl.pallas_call(
        body, name="ret_rope_fwd",
        out_shape=(jax.ShapeDtypeStruct((L, _RET_QKW), _MXU), jax.ShapeDtypeStruct((L, _RET_QKW), F32)),
        grid=(L // ROW_TILE,),
        in_specs=[_rows(qkz.shape[1]), _rows(hh), _rows(hh)],
        out_specs=(_rows(_RET_QKW), _rows(_RET_QKW)),
        compiler_params=_params(("parallel",)),
    )(qkz, cos, sin)


def _ret_rope_bwd(dqr, dkr, cos, sin, pad):
    L = dqr.shape[0]
    hh = RET_QK // 2
    kscale = RET_QK ** -0.5

    def body(dq_ref, dk_ref, c_ref, s_ref, oq_ref, ok_ref):
        cos, sin = c_ref[...], s_ref[...]
        row = pl.program_id(0) * ROW_TILE + lax.broadcasted_iota(jnp.int32, (ROW_TILE, hh), 0)
        keep = row >= pad
        for h in range(RET_HEADS):
            b = h * RET_QK
            d1, d2 = dq_ref[:, b:b + hh], dq_ref[:, b + hh:b + 2 * hh]
            oq_ref[:, b:b + hh] = (d1 * cos + d2 * sin).astype(_MXU)
            oq_ref[:, b + hh:b + 2 * hh] = (d2 * cos - d1 * sin).astype(_MXU)
            d1, d2 = dk_ref[:, b:b + hh], dk_ref[:, b + hh:b + 2 * hh]
            ok_ref[:, b:b + hh] = jnp.where(keep, (d1 * cos + d2 * sin) * kscale, 0.0).astype(_MXU)
            ok_ref[:, b + hh:b + 2 * hh] = jnp.where(keep, (d2 * cos - d1 * sin) * kscale, 0.0).astype(_MXU)

    return pl.pallas_call(
        body, name="ret_rope_bwd",
        out_shape=(jax.ShapeDtypeStruct((L, _RET_QKW), _MXU), jax.ShapeDtypeStruct((L, _RET_QKW), _MXU)),
        grid=(L // ROW_TILE,),
        in_specs=[_rows(_RET_QKW), _rows(_RET_QKW), _rows(hh), _rows(hh)],
        out_specs=(_rows(_RET_QKW), _rows(_RET_QKW)),
        compiler_params=_params(("parallel",)),
    )(dqr, dkr, cos, sin)


def _ret_decays():
    lg = np.log1p(-np.exp2(-5.0 - np.arange(RET_HEADS, dtype=np.float32))).astype(np.float32)
    i = np.arange(CHUNK, dtype=np.float32)
    rel = i[:, None] - i[None, :]
    dm = np.where(rel[None] >= 0, np.exp(rel[None] * lg[:, None, None]), 0.0).astype(np.float32)
    qd = np.exp((i[None, :] + 1.0) * lg[:, None]).astype(np.float32)[:, :, None]
    kd = np.exp((CHUNK - 1.0 - i)[None, :] * lg[:, None]).astype(np.float32)[:, :, None]
    cd = np.broadcast_to(np.exp(CHUNK * lg).astype(np.float32)[:, None, None], (RET_HEADS, 1, RET_V))
    return jnp.asarray(dm), jnp.asarray(qd), jnp.asarray(kd), jnp.asarray(np.ascontiguousarray(cd))


def _ret_fwd(qr, kr, v):
    L = qr.shape[0]
    nc = L // CHUNK
    dm, qd, kd, cd = _ret_decays()
    dn_nt = (((1,), (1,)), ((), ()))
    dn_tn = (((0,), (0,)), ((), ()))

    def body(q_ref, k_ref, v_ref, dm_ref, qd_ref, kd_ref, cd_ref, o_ref, st_ref, s_ref):
        @pl.when(pl.program_id(1) == 0)
        def _():
            s_ref[...] = jnp.zeros_like(s_ref)

        q, k, vv = q_ref[...], k_ref[...], v_ref[...]
        sb = s_ref[...].astype(_MXU)
        st_ref[...] = sb
        s = lax.dot_general(q, k.astype(_MXU), dn_nt, preferred_element_type=F32) * dm_ref[...]
        intra = jnp.dot(s.astype(_MXU), vv, preferred_element_type=F32)
        cross = jnp.dot(q, sb, preferred_element_type=F32) * qd_ref[...]
        o_ref[...] = intra + cross
        kdm = (k * kd_ref[...]).astype(_MXU)
        s_ref[...] = cd_ref[...] * s_ref[...] + lax.dot_general(kdm, vv, dn_tn, preferred_element_type=F32)

    return pl.pallas_call(
        body, name="ret_fwd",
        out_shape=(jax.ShapeDtypeStruct((L, RET_HEADS * RET_V), F32),
                   jax.ShapeDtypeStruct((RET_HEADS, nc, RET_QK, RET_V), _MXU)),
        grid=(RET_HEADS, nc),
        in_specs=[pl.BlockSpec((CHUNK, RET_QK), lambda h, c: (c, h)),
                  pl.BlockSpec((CHUNK, RET_QK), lambda h, c: (c, h)),
                  pl.BlockSpec((CHUNK, RET_V), lambda h, c: (c, h)),
                  pl.BlockSpec((None, CHUNK, CHUNK), lambda h, c: (h, 0, 0)),
                  pl.BlockSpec((None, CHUNK, 1), lambda h, c: (h, 0, 0)),
                  pl.BlockSpec((None, CHUNK, 1), lambda h, c: (h, 0, 0)),
                  pl.BlockSpec((None, 1, RET_V), lambda h, c: (h, 0, 0))],
        out_specs=(pl.BlockSpec((CHUNK, RET_V), lambda h, c: (c, h)),
                   pl.BlockSpec((None, None, RET_QK, RET_V), lambda h, c: (h, c, 0, 0))),
        scratch_shapes=[pltpu.VMEM((RET_QK, RET_V), F32)],
        compiler_params=_params(("parallel", "arbitrary")),
    )(qr, kr, v, dm, qd, kd, cd)


def _ret_bwd(qr, kr, v, do, states):
    L = qr.shape[0]
    nc = L // CHUNK
    dm, qd, kd, cd = _ret_decays()
    dn_nt = (((1,), (1,)), ((), ()))
    dn_tn = (((0,), (0,)), ((), ()))

    def body(q_ref, k_ref, v_ref, do_ref, st_ref, dm_ref, qd_ref, kd_ref, cd_ref,
             dq_ref, dk_ref, dv_ref, ds_ref):
        @pl.when(pl.program_id(1) == 0)
        def _():
            ds_ref[...] = jnp.zeros_like(ds_ref)

        q, k, vv, do = q_ref[...], k_ref[...], v_ref[...], do_ref[...]
        kb = k.astype(_MXU)
        dob = do.astype(_MXU)
        doq = (do * qd_ref[...]).astype(_MXU)
        dsb = ds_ref[...].astype(_MXU)
        dmat = dm_ref[...]
        s = (lax.dot_general(q, kb, dn_nt, preferred_element_type=F32) * dmat).astype(_MXU)
        draw = (lax.dot_general(dob, vv, dn_nt, preferred_element_type=F32) * dmat).astype(_MXU)
        kdm = (k * kd_ref[...]).astype(_MXU)
        dv = (lax.dot_general(s, dob, dn_tn, preferred_element_type=F32)
              + jnp.dot(kdm, dsb, preferred_element_type=F32))
        dq = (jnp.dot(draw, kb, preferred_element_type=F32)
              + lax.dot_general(doq, st_ref[...], dn_nt, preferred_element_type=F32))
        dk = (lax.dot_general(draw, q, dn_tn, preferred_element_type=F32)
              + lax.dot_general(vv, dsb, dn_nt, preferred_element_type=F32) * kd_ref[...])
        dq_ref[...] = dq
        dk_ref[...] = dk
        dv_ref[...] = dv.astype(_MXU)
        ds_ref[...] = cd_ref[...] * ds_ref[...] + lax.dot_general(q, doq, dn_tn, preferred_element_type=F32)

    rev = lambda h, c: (nc - 1 - c, h)
    return pl.pallas_call(
        body, name="ret_bwd",
        out_shape=(jax.ShapeDtypeStruct((L, _RET_QKW), F32), jax.ShapeDtypeStruct((L, _RET_QKW), F32),
                   jax.ShapeDtypeStruct((L, RET_HEADS * RET_V), _MXU)),
        grid=(RET_HEADS, nc),
        in_specs=[pl.BlockSpec((CHUNK, RET_QK), rev),
                  pl.BlockSpec((CHUNK, RET_QK), rev),
                  pl.BlockSpec((CHUNK, RET_V), rev),
                  pl.BlockSpec((CHUNK, RET_V), rev),
                  pl.BlockSpec((None, None, RET_QK, RET_V), lambda h, c: (h, nc - 1 - c, 0, 0)),
                  pl.BlockSpec((None, CHUNK, CHUNK), lambda h, c: (h, 0, 0)),
                  pl.BlockSpec((None, CHUNK, 1), lambda h, c: (h, 0, 0)),
                  pl.BlockSpec((None, CHUNK, 1), lambda h, c: (h, 0, 0)),
                  pl.BlockSpec((None, 1, RET_V), lambda h, c: (h, 0, 0))],
        out_specs=(pl.BlockSpec((CHUNK, RET_QK), rev), pl.BlockSpec((CHUNK, RET_QK), rev),
                   pl.BlockSpec((CHUNK, RET_V), rev)),
        scratch_shapes=[pltpu.VMEM((RET_QK, RET_V), F32)],
        compiler_params=_params(("parallel", "arbitrary")),
    )(qr, kr, v, do, states, dm, qd, kd, cd)


def _gn(o, g):
    mu = jnp.mean(o, axis=-1, keepdims=True)
    oc = o - mu
    var = jnp.mean(oc * oc, axis=-1, keepdims=True)
    return oc * lax.rsqrt(var + NORM_EPS) * g


def _ret_gate_fwd(o, qkz, gn_g):
    L, W = o.shape
    z0 = qkz.shape[1] - W

    def body(o_ref, p_ref, g_ref, y_ref):
        for h in range(RET_HEADS):
            sl = slice(h * RET_V, (h + 1) * RET_V)
            z = p_ref[:, z0 + h * RET_V:z0 + (h + 1) * RET_V]
            y_ref[:, sl] = (_gn(o_ref[:, sl], g_ref[:, sl]) * _silu(z)).astype(_MXU)

    return pl.pallas_call(
        body, name="ret_gate_fwd",
        out_shape=jax.ShapeDtypeStruct((L, W), _MXU),
        grid=(L // ROW_TILE,),
        in_specs=[_rows(W), _rows(qkz.shape[1]), _whole((1, W))],
        out_specs=_rows(W),
        compiler_params=_params(("parallel",)),
    )(o, qkz, gn_g)


def _ret_gate_bwd(dy, o, qkz, gn_g):
    L, W = o.shape
    z0 = qkz.shape[1] - W

    def body(dy_ref, o_ref, p_ref, g_ref, do_ref, dz_ref, dg_ref):
        @pl.when(pl.program_id(0) == 0)
        def _():
            dg_ref[...] = jnp.zeros_like(dg_ref)

        for h in range(RET_HEADS):
            sl = slice(h * RET_V, (h + 1) * RET_V)
            z = p_ref[:, z0 + h * RET_V:z0 + (h + 1) * RET_V]
            o = o_ref[:, sl]
            g = g_ref[:, sl]
            dy = dy_ref[:, sl]
            mu = jnp.mean(o, axis=-1, keepdims=True)
            oc = o - mu
            var = jnp.mean(oc * oc, axis=-1, keepdims=True)
            r = lax.rsqrt(var + NORM_EPS)
            xh = oc * r
            dn = dy * _silu(z)
            dz_ref[:, sl] = (dy * (xh * g) * _dsilu(z)).astype(_MXU)
            dg_ref[:, sl] += jnp.sum(dn * xh, axis=0, keepdims=True)
            dxh = dn * g
            do_ref[:, sl] = r * (dxh - jnp.mean(dxh, axis=-1, keepdims=True)
                                 - xh * jnp.mean(dxh * xh, axis=-1, keepdims=True))

    return pl.pallas_call(
        body, name="ret_gate_bwd",
        out_shape=(jax.ShapeDtypeStruct((L, W), F32), jax.ShapeDtypeStruct((L, W), _MXU),
                   jax.ShapeDtypeStruct((1, W), F32)),
        grid=(L // ROW_TILE,),
        in_specs=[_rows(W), _rows(W), _rows(qkz.shape[1]), _whole((1, W))],
        out_specs=(_rows(W), _rows(W), _whole((1, W))),
        compiler_params=_params(("arbitrary",)),
    )(dy, o, qkz, gn_g)


def _heads_t(a):
    return a[:, :FOX_HEADS].T


def _fox_fwd(h, w, lnw, pad, tag):
    w_in, b_f, w_out = w
    L = h.shape[0]
    W = D_MODEL
    w_qkv = w_in[:, :3 * W]
    w_zf = jnp.concatenate([w_in[:, 3 * W:4 * W], jnp.pad(w_in[:, 4 * W:], ((0, 0), (0, LANES - FOX_HEADS)))], axis=1)
    bias = jnp.pad(b_f, (0, LANES - FOX_HEADS))[None, :]
    qkv = _matmul(h, w_qkv, "nn", _MXU, f"{tag}_qkv")
    zf = _matmul(h, w_zf, "nn", F32, f"{tag}_zf")
    cum = _fox_cum(zf, bias, pad, f"{tag}_cum")
    ab = _bias_terms(_heads_t(cum), pad)
    nb = W // LANES
    o, lse = _attn_fwd(qkv, 0, qkv, nb, qkv, 2 * nb, ab, FOX_HEADS, LANES, LANES, LANES ** -0.5, f"{tag}_attn_fwd")
    y = _gate_fwd(o, zf, f"{tag}_gate_fwd")
    mo = _matmul(y, w_out, "nn", F32, f"{tag}_out")
    zl, hn = _ln_fwd(h, mo, lnw[0], lnw[1], f"{tag}_ln_fwd")
    return hn, dict(h=h, qkv=qkv, zf=zf, ab=ab, o=o, lse=lse, y=y, zl=zl,
                    w_all=jnp.concatenate([w_qkv, w_zf], axis=1), w_out=w_out, bias=bias)


def _fox_bwd(s, dzl, pad, tag):
    W = D_MODEL
    nb = W // LANES
    dy = _matmul(dzl, s["w_out"], "nt", F32, f"{tag}_dy")
    d_wout = _matmul(s["y"], dzl, "tn", F32, f"{tag}_dwout")
    do, delta, dzg = _gate_bwd(dy, s["o"], s["zf"], FOX_HEADS, f"{tag}_gate_bwd")
    qkv = s["qkv"]
    rows = s["ab"]["cq_row"].shape
    dq, dk, dv, dck, dcq = _attn_bwd(qkv, 0, qkv, nb, qkv, 2 * nb, do, s["ab"], s["lse"].reshape(rows),
                                     delta.reshape(rows), FOX_HEADS, LANES, LANES, LANES ** -0.5, f"{tag}_attn_bwd")
    dcum = jnp.pad((dck[:, :, 0] + dcq.reshape(FOX_HEADS, -1)).T, ((0, 0), (0, LANES - FOX_HEADS)))
    df, dbias = _fox_cum_bwd(dcum, s["zf"], s["bias"], pad, f"{tag}_cum_bwd")
    dproj = jnp.concatenate([dq.astype(_MXU), dk.astype(_MXU), dv.astype(_MXU), dzg, df], axis=1)
    d_wall = _matmul(s["h"], dproj, "tn", F32, f"{tag}_dwin")
    dh = _matmul(dproj, s["w_all"], "nt", F32, f"{tag}_dh")
    d_win = jnp.concatenate([d_wall[:, :4 * W], d_wall[:, 4 * W:4 * W + FOX_HEADS]], axis=1)
    return dh, (d_win, dbias[0, :FOX_HEADS], d_wout)


def _mla_tables(L, pad):
    pos = (jnp.arange(L) - pad).astype(F32)
    inv = ROPE_BASE ** (-jnp.arange(0, MLA_ROPE, 2, dtype=F32) / MLA_ROPE)
    ang = pos[:, None] * inv[None, :]
    c, s = jnp.cos(ang), jnp.sin(ang)
    z32, z64 = jnp.zeros_like(s), jnp.zeros((L, LANES - MLA_ROPE), F32)
    cos = jnp.concatenate([c, c, jnp.ones_like(z64)], axis=1)
    sa = jnp.concatenate([-s, z32, z64], axis=1)
    sb = jnp.concatenate([z32, s, z64], axis=1)
    return cos, sa, sb


def _mla_weights(w_in, w_uq, w_ukv):
    a, b = MLA_Q_LORA, MLA_Q_LORA + MLA_KV_LORA
    w_zc = jnp.concatenate([w_in[:, b + MLA_ROPE:], w_in[:, :b + MLA_ROPE],
                            jnp.zeros((D_MODEL, LANES - MLA_ROPE), w_in.dtype)], axis=1)
    uq = w_uq.reshape(MLA_Q_LORA, MLA_HEADS, MLA_NOPE + MLA_ROPE)
    uq = jnp.pad(uq, ((0, 0), (0, 0), (0, MLA_DK - MLA_NOPE - MLA_ROPE))).reshape(MLA_Q_LORA, MLA_HEADS * MLA_DK)
    ukv = w_ukv.reshape(MLA_KV_LORA, MLA_HEADS, MLA_NOPE + MLA_V)
    ukv = jnp.concatenate([ukv[:, :, :MLA_NOPE].reshape(MLA_KV_LORA, -1), ukv[:, :, MLA_NOPE:].reshape(MLA_KV_LORA, -1)], axis=1)
    return w_zc, uq, ukv


def _mla_fwd(h, w, lnw, pad):
    w_in, gq, gkv, w_uq, w_ukv, w_out = w
    L = h.shape[0]
    w_zc, uq, ukv = _mla_weights(w_in, w_uq, w_ukv)
    cos, sa, sb = _mla_tables(L, pad)
    proj = _matmul(h, w_zc, "nn", F32, "mla_proj")
    cqn, ckvn = _mla_norm_fwd(proj, gq[None, :], gkv[None, :])
    qf = _matmul(cqn, uq, "nn", F32, "mla_uq")
    kv = _matmul(ckvn, ukv, "nn", _MXU, "mla_ukv")
    q_full, k_full = _mla_rope_fwd(qf, kv, proj, cos, sa, sb)
    ab = _bias_terms(jnp.zeros((MLA_HEADS, L), F32), pad)
    nb = MLA_HEADS * MLA_NOPE // LANES
    o, lse = _attn_fwd(q_full, 0, k_full, 0, kv, nb, ab, MLA_HEADS, MLA_DK, MLA_V,
                       (MLA_NOPE + MLA_ROPE) ** -0.5, "mla_attn_fwd")
    y = _gate_fwd(o, proj, "mla_gate_fwd")
    mo = _matmul(y, w_out, "nn", F32, "mla_out")
    zl, hn = _ln_fwd(h, mo, lnw[0], lnw[1], "mla_ln_fwd")
    return hn, dict(h=h, proj=proj, cqn=cqn, ckvn=ckvn, kv=kv, q_full=q_full, k_full=k_full, ab=ab,
                    o=o, lse=lse, y=y, zl=zl, w_zc=w_zc, uq=uq, ukv=ukv, w_out=w_out, gq=gq, gkv=gkv,
                    tabs=(cos, sa, sb))


def _mla_bwd(s, dzl):
    H = MLA_HEADS
    dy = _matmul(dzl, s["w_out"], "nt", F32, "mla_dy")
    d_wout = _matmul(s["y"], dzl, "tn", F32, "mla_dwout")
    do, delta, dzg = _gate_bwd(dy, s["o"], s["proj"], H, "mla_gate_bwd")
    nb = H * MLA_NOPE // LANES
    rows = s["ab"]["cq_row"].shape
    dq_full, dk_full, dv, _, _ = _attn_bwd(s["q_full"], 0, s["k_full"], 0, s["kv"], nb, do, s["ab"],
                                           s["lse"].reshape(rows), delta.reshape(rows),
                                           H, MLA_DK, MLA_V, (MLA_NOPE + MLA_ROPE) ** -0.5, "mla_attn_bwd")
    cos, sa, sb = s["tabs"]
    dqf, dkn, dkr = _mla_rope_bwd(dq_full, dk_full, cos, -sa, -sb)
    d_uq = _matmul(s["cqn"], dqf, "tn", F32, "mla_duq")
    dcqn = _matmul(dqf, s["uq"], "nt", F32, "mla_dcqn")
    dkv = jnp.concatenate([dkn, dv.astype(_MXU)], axis=1)
    d_ukv = _matmul(s["ckvn"], dkv, "tn", F32, "mla_dukv")
    dckvn = _matmul(dkv, s["ukv"], "nt", F32, "mla_dckvn")
    dcq, dckv, dgq, dgkv = _mla_norm_bwd(s["proj"], s["gq"][None, :], s["gkv"][None, :], dcqn, dckvn)
    dproj = jnp.concatenate([dzg, dcq, dckv, dkr], axis=1)
    d_wzc = _matmul(s["h"], dproj, "tn", F32, "mla_dwin")
    dh = _matmul(dproj, s["w_zc"], "nt", F32, "mla_dh")
    W = D_MODEL
    d_win = jnp.concatenate([d_wzc[:, W:W + MLA_Q_LORA + MLA_KV_LORA + MLA_ROPE], d_wzc[:, :W]], axis=1)
    d_wuq = d_uq.reshape(MLA_Q_LORA, H, MLA_DK)[:, :, :MLA_NOPE + MLA_ROPE].reshape(MLA_Q_LORA, -1)
    hk = H * MLA_NOPE
    d_wukv = jnp.concatenate([d_ukv[:, :hk].reshape(MLA_KV_LORA, H, MLA_NOPE),
                              d_ukv[:, hk:].reshape(MLA_KV_LORA, H, MLA_V)], axis=2).reshape(MLA_KV_LORA, -1)
    return dh, (d_win, dgq[0], dgkv[0], d_wuq, d_wukv, d_wout)


def _ret_tables(L, pad):
    pos = (jnp.arange(L) - pad).astype(F32)
    inv = 1.0 / (ROPE_BASE ** jnp.linspace(0.0, 1.0, RET_QK // 2, dtype=F32))
    ang = pos[:, None] * inv[None, :]
    return jnp.cos(ang), jnp.sin(ang)


def _ret_layer_fwd(h, w, lnw, pad):
    w_in, gn_g, w_out = w
    L = h.shape[0]
    q1 = 2 * _RET_QKW
    v1 = q1 + RET_HEADS * RET_V
    w_qkz = jnp.concatenate([w_in[:, :q1], w_in[:, v1:]], axis=1)
    cos, sin = _ret_tables(L, pad)
    qkz = _matmul(h, w_qkz, "nn", F32, "ret_qkz")
    v = _matmul(h, w_in[:, q1:v1], "nn", _MXU, "ret_v")
    qr, kr = _ret_rope_fwd(qkz, cos, sin, pad)
    o, states = _ret_fwd(qr, kr, v)
    y = _ret_gate_fwd(o, qkz, gn_g[None, :])
    mo = _matmul(y, w_out, "nn", F32, "ret_out")
    zl, hn = _ln_fwd(h, mo, lnw[0], lnw[1], "ret_ln_fwd")
    return hn, dict(h=h, qkz=qkz, v=v, qr=qr, kr=kr, o=o, states=states, y=y, zl=zl, w_in=w_in, w_out=w_out,
                    gn_g=gn_g, tabs=(cos, sin))


def _ret_layer_bwd(s, dzl, pad):
    dy = _matmul(dzl, s["w_out"], "nt", F32, "ret_dy")
    d_wout = _matmul(s["y"], dzl, "tn", F32, "ret_dwout")
    do, dzg, dgn = _ret_gate_bwd(dy, s["o"], s["qkz"], s["gn_g"][None, :])
    dqr, dkr, dv = _ret_bwd(s["qr"], s["kr"], s["v"], do, s["states"])
    cos, sin = s["tabs"]
    dq, dk = _ret_rope_bwd(dqr, dkr, cos, sin, pad)
    dproj = jnp.concatenate([dq, dk, dv, dzg], axis=1)
    d_win = _matmul(s["h"], dproj, "tn", F32, "ret_dwin")
    dh = _matmul(dproj, s["w_in"], "nt", F32, "ret_dh")
    return dh, (d_win, dgn[0], d_wout)


def _local_step(x, target, meta, fox, mla, ret, ln_g, ln_b):
    n_real = x.shape[0]
    first = -(-N_META // ROW_TILE)
    pad = first * ROW_TILE - N_META
    h0 = jnp.concatenate([jnp.zeros((pad, D_MODEL), F32), meta, x], axis=0)
    lnw = [(ln_g[i][None, :], ln_b[i][None, :]) for i in range(4)]

    h1, s0 = _fox_fwd(h0, (fox[0][0], fox[1][0], fox[2][0]), lnw[0], pad, "fox0")
    h2, s1 = _mla_fwd(h1, tuple(a[0] for a in mla), lnw[1], pad)
    h3, s2 = _ret_layer_fwd(h2, tuple(a[0] for a in ret), lnw[2], pad)
    h4, s3 = _fox_fwd(h3, (fox[0][1], fox[1][1], fox[2][1]), lnw[3], pad, "fox1")

    dh4, loss = _loss_head(h4, target, first)
    dz3, dg3, db3 = _ln_bwd(s3["zl"], lnw[3][0], dh4, None, "ln3_bwd")
    dm3, gf1 = _fox_bwd(s3, dz3, pad, "fox1")
    dz2, dg2, db2 = _ln_bwd(s2["zl"], lnw[2][0], dz3, dm3, "ln2_bwd")
    dm2, gr = _ret_layer_bwd(s2, dz2, pad)
    dz1, dg1, db1 = _ln_bwd(s1["zl"], lnw[1][0], dz2, dm2, "ln1_bwd")
    dm1, gm = _mla_bwd(s1, dz1)
    dz0, dg0, db0 = _ln_bwd(s0["zl"], lnw[0][0], dz1, dm1, "ln0_bwd")
    dm0, gf0 = _fox_bwd(s0, dz0, pad, "fox0")
    grad_x, grad_meta = _input_grads(dz0, dm0, first, n_real)

    grads = dict(
        meta=grad_meta,
        fox_w_in=jnp.stack([gf0[0], gf1[0]]), fox_b_f=jnp.stack([gf0[1], gf1[1]]),
        fox_w_out=jnp.stack([gf0[2], gf1[2]]),
        mla_w_in=gm[0][None], mla_q_norm=gm[1][None], mla_kv_norm=gm[2][None],
        mla_w_uq=gm[3][None], mla_w_ukv=gm[4][None], mla_w_out=gm[5][None],
        ret_w_in=gr[0][None], ret_gn_g=gr[1][None], ret_w_out=gr[2][None],
        ln_g=jnp.concatenate([dg0, dg1, dg2, dg3], axis=0),
        ln_b=jnp.concatenate([db0, db1, db2, db3], axis=0),
    )
    return loss, grad_x, grads


def _exchange(src, scatter, name):
    R = src.shape[-2]

    def body(src_ref, out_ref, send_sems, recv_sems, local_sem):
        x, y, c = lax.axis_index("x"), lax.axis_index("y"), lax.axis_index("c")
        me = 4 * x + 2 * y + c
        mine = pltpu.make_async_copy(src_ref.at[me] if scatter else src_ref, out_ref.at[me], local_sem)
        mine.start()
        copies = []
        for k in range(1, N_DEV):
            px = (x + (k >> 2)) % 2
            py = (y + ((k >> 1) & 1)) % 2
            pc = (c + (k & 1)) % 2
            peer = 4 * px + 2 * py + pc
            copies.append(pltpu.make_async_remote_copy(
                src_ref=src_ref.at[peer] if scatter else src_ref,
                dst_ref=out_ref.at[me],
                send_sem=send_sems.at[k - 1], recv_sem=recv_sems.at[k - 1],
                device_id=(px, py, pc), device_id_type=pl.DeviceIdType.MESH))
        for cp in copies:
            cp.start()
        for cp in copies:
            cp.wait()
        mine.wait()

    return pl.pallas_call(
        body, name=name,
        out_shape=jax.ShapeDtypeStruct((N_DEV, R, LANES), src.dtype),
        in_specs=[pl.BlockSpec(memory_space=pl.ANY)],
        out_specs=pl.BlockSpec(memory_space=pl.ANY),
        scratch_shapes=[pltpu.SemaphoreType.DMA((N_DEV - 1,)), pltpu.SemaphoreType.DMA((N_DEV - 1,)),
                        pltpu.SemaphoreType.DMA],
    )(src)


def _adamw(parts, w, m, v):
    R = w.shape[0]
    tr = _tile(R, 1024)

    def body(p_ref, w_ref, m_ref, v_ref, g_ref, d_ref, mo_ref, vo_ref):
        g = p_ref[0]
        for d in range(1, N_DEV):
            g = g + p_ref[d]
        mn = ADAM_B1 * m_ref[...] + (1.0 - ADAM_B1) * g
        vn = ADAM_B2 * v_ref[...] + (1.0 - ADAM_B2) * (g * g)
        m_hat = mn / (1.0 - ADAM_B1 ** ADAM_STEP)
        v_hat = vn / (1.0 - ADAM_B2 ** ADAM_STEP)
        g_ref[...] = g
        d_ref[...] = -ADAM_LR * (m_hat / (jnp.sqrt(v_hat) + ADAM_EPS) + ADAM_WD * w_ref[...])
        mo_ref[...] = mn
        vo_ref[...] = vn

    blk = pl.BlockSpec((tr, LANES), lambda i: (i, 0))
    sd = jax.ShapeDtypeStruct((R, LANES), F32)
    return pl.pallas_call(
        body, name="adamw",
        out_shape=(sd, sd, sd, sd),
        grid=(R // tr,),
        in_specs=[pl.BlockSpec((N_DEV, tr, LANES), lambda i: (0, i, 0)), blk, blk, blk],
        out_specs=(blk, blk, blk, blk),
        compiler_params=_params(("parallel",)),
    )(parts, w, m, v)


_SPECS = dict(
    meta=((16, 1024), 1),
    fox_w_in=((2, 1024, 4104), 2), fox_b_f=((2, 8), None), fox_w_out=((2, 1024, 1024), 1),
    mla_w_in=((1, 1024, 1728), 2), mla_q_norm=((1, 384), None), mla_kv_norm=((1, 256), None),
    mla_w_uq=((1, 384, 1536), 2), mla_w_ukv=((1, 256, 2048), 2), mla_w_out=((1, 1024, 1024), 1),
    ret_w_in=((1, 1024, 6144), 2), ret_gn_g=((1, 2048), 1), ret_w_out=((1, 2048, 1024), 1),
    ln_g=((4, 1024), None), ln_b=((4, 1024), None),
)
_NAMES = list(_SPECS)
_MXU_WEIGHTS = ["fox_w_in", "fox_w_out", "mla_w_in", "mla_w_uq", "mla_w_ukv", "mla_w_out", "ret_w_in", "ret_w_out"]
_F32_GATHERED = ["meta", "ret_gn_g"]
_PACK_ROWS = 1024


def _shard_shape(name):
    shape, ax = _SPECS[name]
    if ax is None:
        return shape
    return tuple(s // N_DEV if i == ax else s for i, s in enumerate(shape))


def _to_shards(name, full):
    shape, ax = _SPECS[name]
    if ax is None:
        return jnp.broadcast_to(full.reshape(1, -1), (N_DEV, full.size))
    parts = full.reshape(shape[:ax] + (N_DEV, shape[ax] // N_DEV) + shape[ax + 1:])
    return jnp.moveaxis(parts, ax, 0).reshape(N_DEV, -1)


def _from_shards(name, stacked):
    shape, ax = _SPECS[name]
    parts = stacked.reshape((N_DEV,) + shape[:ax] + (shape[ax] // N_DEV,) + shape[ax + 1:])
    return jnp.moveaxis(parts, 0, ax).reshape(shape)


def _pad_rows(flat):
    n = flat.shape[-1]
    unit = _PACK_ROWS * LANES
    total = -(-n // unit) * unit
    flat = jnp.pad(flat, [(0, 0)] * (flat.ndim - 1) + [(0, total - n)])
    return flat.reshape(flat.shape[:-1] + (total // LANES, LANES))


def kernel(x, meta, fox_w_in, fox_b_f, fox_w_out, mla_w_in, mla_q_norm, mla_kv_norm, mla_w_uq, mla_w_ukv, mla_w_out, ret_w_in, ret_gn_g, ret_w_out, ln_g, ln_b, loss_target, m_meta, m_fox_w_in, m_fox_b_f, m_fox_w_out, m_mla_w_in, m_mla_q_norm, m_mla_kv_norm, m_mla_w_uq, m_mla_w_ukv, m_mla_w_out, m_ret_w_in, m_ret_gn_g, m_ret_w_out, m_ln_g, m_ln_b, v_meta, v_fox_w_in, v_fox_b_f, v_fox_w_out, v_mla_w_in, v_mla_q_norm, v_mla_kv_norm, v_mla_w_uq, v_mla_w_ukv, v_mla_w_out, v_ret_w_in, v_ret_gn_g, v_ret_w_out, v_ln_g, v_ln_b):
    w = dict(meta=meta, fox_w_in=fox_w_in, fox_b_f=fox_b_f, fox_w_out=fox_w_out, mla_w_in=mla_w_in,
             mla_q_norm=mla_q_norm, mla_kv_norm=mla_kv_norm, mla_w_uq=mla_w_uq, mla_w_ukv=mla_w_ukv,
             mla_w_out=mla_w_out, ret_w_in=ret_w_in, ret_gn_g=ret_gn_g, ret_w_out=ret_w_out, ln_g=ln_g, ln_b=ln_b)
    m = dict(meta=m_meta, fox_w_in=m_fox_w_in, fox_b_f=m_fox_b_f, fox_w_out=m_fox_w_out, mla_w_in=m_mla_w_in,
             mla_q_norm=m_mla_q_norm, mla_kv_norm=m_mla_kv_norm, mla_w_uq=m_mla_w_uq, mla_w_ukv=m_mla_w_ukv,
             mla_w_out=m_mla_w_out, ret_w_in=m_ret_w_in, ret_gn_g=m_ret_gn_g, ret_w_out=m_ret_w_out, ln_g=m_ln_g, ln_b=m_ln_b)
    v = dict(meta=v_meta, fox_w_in=v_fox_w_in, fox_b_f=v_fox_b_f, fox_w_out=v_fox_w_out, mla_w_in=v_mla_w_in,
             mla_q_norm=v_mla_q_norm, mla_kv_norm=v_mla_kv_norm, mla_w_uq=v_mla_w_uq, mla_w_ukv=v_mla_w_ukv,
             mla_w_out=v_mla_w_out, ret_w_in=v_ret_w_in, ret_gn_g=v_ret_gn_g, ret_w_out=v_ret_w_out, ln_g=v_ln_g, ln_b=v_ln_b)

    def halves(a):
        return lax.bitcast_convert_type(a, jnp.bfloat16).reshape(-1)

    segs = [w[n].astype(jnp.bfloat16).reshape(-1) for n in _MXU_WEIGHTS] + [halves(w[n]) for n in _F32_GATHERED]
    sizes = [s.shape[0] for s in segs]
    gathered = _exchange(_pad_rows(jnp.concatenate(segs)), False, "weights_all_gather").reshape(N_DEV, -1)
    full, at = {}, 0
    for n, size in zip(_MXU_WEIGHTS + _F32_GATHERED, sizes):
        seg = gathered[:, at:at + size]
        at += size
        if n in _F32_GATHERED:
            seg = lax.bitcast_convert_type(seg.reshape(N_DEV, -1, 2), F32)
        full[n] = _from_shards(n, seg)
        if n in _MXU_WEIGHTS:
            full[n] = full[n].astype(_MXU)

    loss, grad_x, grads = _local_step(
        x[0], loss_target[0], full["meta"],
        (full["fox_w_in"], fox_b_f, full["fox_w_out"]),
        (full["mla_w_in"], mla_q_norm, mla_kv_norm, full["mla_w_uq"], full["mla_w_ukv"], full["mla_w_out"]),
        (full["ret_w_in"], full["ret_gn_g"], full["ret_w_out"]), ln_g, ln_b)

    def pack(tree, tail):
        return jnp.concatenate([tree[n].reshape(-1) for n in _NAMES] + [tail])

    outgoing = jnp.concatenate([_to_shards(n, grads[n]) for n in _NAMES]
                               + [jnp.broadcast_to(loss.reshape(1, 1), (N_DEV, 1))], axis=1)
    parts = _exchange(_pad_rows(outgoing), True, "grads_all_to_all")
    zero1 = jnp.zeros((1,), F32)
    g, d, mn, vn = _adamw(parts, _pad_rows(pack(w, zero1)), _pad_rows(pack(m, zero1)), _pad_rows(pack(v, zero1)))

    def unpack(buf):
        flat, out, at = buf.reshape(-1), {}, 0
        for n in _NAMES:
            shp = _shard_shape(n)
            size = math.prod(shp)
            out[n] = flat[at:at + size].reshape(shp)
            at += size
        return out, flat[at]

    g, total_loss = unpack(g)
    d, _ = unpack(d)
    mn, _ = unpack(mn)
    vn, _ = unpack(vn)
    names = _NAMES[1:]
    return (total_loss, grad_x[None], g["meta"], *[g[n] for n in names], d["meta"], *[d[n] for n in names],
            mn["meta"], *[mn[n] for n in names], vn["meta"], *[vn[n] for n in names])
```

```python
import functools
import math

import numpy as np
import jax
import jax.numpy as jnp
from jax import lax
from jax.experimental import pallas as pl
from jax.experimental.pallas import tpu as pltpu

F32 = jnp.float32
_MXU = jnp.bfloat16

N_DEV = 8
N_META = 16
D_MODEL = 1024
ROW_TILE = 512
LANES = 128
CHUNK = 128
NEG = -1e30
VMEM_LIMIT = 60 * 1024 * 1024

FOX_HEADS = 8
MLA_HEADS = 8
MLA_NOPE, MLA_ROPE, MLA_V = 128, 64, 128
MLA_Q_LORA, MLA_KV_LORA = 384, 256
MLA_DK = 256
RET_HEADS, RET_QK, RET_V = 4, 256, 512
ROPE_BASE = 10000.0
ALPHA = (2 * 4) ** 0.25
NORM_EPS = 1e-5

ADAM_LR, ADAM_B1, ADAM_B2, ADAM_EPS, ADAM_WD, ADAM_STEP = 0.001, 0.9, 0.999, 1e-08, 0.01, 10


def _params(sem, vmem=VMEM_LIMIT):
    return pltpu.CompilerParams(dimension_semantics=sem, vmem_limit_bytes=vmem)


def _tile(n, pref):
    if n <= pref:
        return n
    t = (pref // LANES) * LANES
    while n % t:
        t -= LANES
    return t


_MM_TILES = {"nn": (512, 2048, 2048), "nt": (512, 1024, 4224), "tn": (1024, 2048, 768)}


def _matmul(a, b, mode, out_dtype, name):
    if mode == "nn":
        (M, K), (K2, N) = a.shape, b.shape
    elif mode == "nt":
        (M, K), (N, K2) = a.shape, b.shape
    else:
        (K, M), (K2, N) = a.shape, b.shape
    assert K == K2, (a.shape, b.shape, mode)
    tm, tn, tk = (_tile(d, p) for d, p in zip((M, N, K), _MM_TILES[mode]))
    nk = K // tk
    if mode == "nn":
        a_spec = pl.BlockSpec((tm, tk), lambda i, j, k: (i, k))
        b_spec = pl.BlockSpec((tk, tn), lambda i, j, k: (k, j))
        dn = (((1,), (0,)), ((), ()))
    elif mode == "nt":
        a_spec = pl.BlockSpec((tm, tk), lambda i, j, k: (i, k))
        b_spec = pl.BlockSpec((tn, tk), lambda i, j, k: (j, k))
        dn = (((1,), (1,)), ((), ()))
    else:
        a_spec = pl.BlockSpec((tk, tm), lambda i, j, k: (k, i))
        b_spec = pl.BlockSpec((tk, tn), lambda i, j, k: (k, j))
        dn = (((0,), (0,)), ((), ()))

    def body(a_ref, b_ref, o_ref, acc_ref):
        k = pl.program_id(2)
        part = lax.dot_general(a_ref[...].astype(_MXU), b_ref[...].astype(_MXU), dn,
                               preferred_element_type=F32)

        @pl.when(k == 0)
        def _():
            acc_ref[...] = part

        @pl.when(k > 0)
        def _():
            acc_ref[...] += part

        @pl.when(k == nk - 1)
        def _():
            o_ref[...] = acc_ref[...].astype(out_dtype)

    return pl.pallas_call(
        body, name=name,
        out_shape=jax.ShapeDtypeStruct((M, N), out_dtype),
        grid=(M // tm, N // tn, nk),
        in_specs=[a_spec, b_spec],
        out_specs=pl.BlockSpec((tm, tn), lambda i, j, k: (i, j)),
        scratch_shapes=[pltpu.VMEM((tm, tn), F32)],
        compiler_params=_params(("parallel", "parallel", "arbitrary")),
    )(a, b)


def _rows(w, off=0):
    return pl.BlockSpec((ROW_TILE, w), lambda i: (i + off, 0))


def _whole(shape):
    return pl.BlockSpec(shape, lambda i: (0,) * len(shape))


def _silu(z):
    return z * jax.nn.sigmoid(z)


def _dsilu(z):
    s = jax.nn.sigmoid(z)
    return s * (1.0 + z * (1.0 - s))


def _ln_fwd(h, y, g, b, name):
    L, D = h.shape

    def body(h_ref, y_ref, g_ref, b_ref, z_ref, o_ref):
        z = ALPHA * h_ref[...] + y_ref[...]
        mu = jnp.mean(z, axis=-1, keepdims=True)
        zc = z - mu
        var = jnp.mean(zc * zc, axis=-1, keepdims=True)
        z_ref[...] = z
        o_ref[...] = zc * lax.rsqrt(var + NORM_EPS) * g_ref[...] + b_ref[...]

    return pl.pallas_call(
        body, name=name,
        out_shape=(jax.ShapeDtypeStruct((L, D), F32), jax.ShapeDtypeStruct((L, D), F32)),
        grid=(L // ROW_TILE,),
        in_specs=[_rows(D), _rows(D), _whole((1, D)), _whole((1, D))],
        out_specs=(_rows(D), _rows(D)),
        compiler_params=_params(("parallel",)),
    )(h, y, g, b)


def _ln_bwd(zl, g, ga, gb, name):
    L, D = zl.shape
    two = gb is not None

    def body(*refs):
        if two:
            z_ref, g_ref, ga_ref, gb_ref, dz_ref, dg_ref, db_ref = refs
            gout = ALPHA * ga_ref[...] + gb_ref[...]
        else:
            z_ref, g_ref, ga_ref, dz_ref, dg_ref, db_ref = refs
            gout = ga_ref[...]
        z = z_ref[...]
        mu = jnp.mean(z, axis=-1, keepdims=True)
        zc = z - mu
        var = jnp.mean(zc * zc, axis=-1, keepdims=True)
        rstd = lax.rsqrt(var + NORM_EPS)
        xhat = zc * rstd
        dxh = gout * g_ref[...]
        m1 = jnp.mean(dxh, axis=-1, keepdims=True)
        m2 = jnp.mean(dxh * xhat, axis=-1, keepdims=True)
        dz_ref[...] = rstd * (dxh - m1 - xhat * m2)

        @pl.when(pl.program_id(0) == 0)
        def _():
            dg_ref[...] = jnp.zeros_like(dg_ref)
            db_ref[...] = jnp.zeros_like(db_ref)

        dg_ref[...] += jnp.sum(gout * xhat, axis=0, keepdims=True)
        db_ref[...] += jnp.sum(gout, axis=0, keepdims=True)

    ins = [zl, g, ga] + ([gb] if two else [])
    return pl.pallas_call(
        body, name=name,
        out_shape=(jax.ShapeDtypeStruct((L, D), F32), jax.ShapeDtypeStruct((1, D), F32),
                   jax.ShapeDtypeStruct((1, D), F32)),
        grid=(L // ROW_TILE,),
        in_specs=[_rows(D), _whole((1, D)), _rows(D)] + ([_rows(D)] if two else []),
        out_specs=(_rows(D), _whole((1, D)), _whole((1, D))),
        compiler_params=_params(("arbitrary",)),
    )(*ins)


def _loss_head(h, target, first):
    L, D = h.shape

    def body(h_ref, t_ref, dh_ref, loss_ref):
        i = pl.program_id(0)

        @pl.when(i == 0)
        def _():
            loss_ref[...] = jnp.zeros_like(loss_ref)

        @pl.when(i < first)
        def _():
            dh_ref[...] = jnp.zeros_like(dh_ref)

        @pl.when(i >= first)
        def _():
            err = h_ref[...] - t_ref[...]
            dh_ref[...] = err * (1.0 / D)
            part = jnp.sum(jnp.sum(err * err, axis=-1, keepdims=True) * (1.0 / D), axis=0, keepdims=True)
            loss_ref[...] += 0.5 * part

    return pl.pallas_call(
        body, name="loss_head",
        out_shape=(jax.ShapeDtypeStruct((L, D), F32), jax.ShapeDtypeStruct((1, 1), F32)),
        grid=(L // ROW_TILE,),
        in_specs=[_rows(D), pl.BlockSpec((ROW_TILE, D), lambda i: (jnp.maximum(i - first, 0), 0))],
        out_specs=(_rows(D), _whole((1, 1))),
        compiler_params=_params(("arbitrary",)),
    )(h, target)


def _input_grads(dzl, dmix, first, n_real):
    L, D = dzl.shape

    def body(a_ref, b_ref, gx_ref, gm_ref):
        i = pl.program_id(0)
        val = ALPHA * a_ref[...] + b_ref[...]

        @pl.when(i == first - 1)
        def _():
            gm_ref[...] = val[ROW_TILE - N_META:, :]

        @pl.when(i >= first)
        def _():
            gx_ref[...] = val

    return pl.pallas_call(
        body, name="input_grads",
        out_shape=(jax.ShapeDtypeStruct((n_real, D), F32), jax.ShapeDtypeStruct((N_META, D), F32)),
        grid=(L // ROW_TILE,),
        in_specs=[_rows(D), _rows(D)],
        out_specs=(pl.BlockSpec((ROW_TILE, D), lambda i: (jnp.maximum(i - first, 0), 0)),
                   _whole((N_META, D))),
        compiler_params=_params(("arbitrary",)),
    )(dzl, dmix)


def _gate_fwd(o, zsrc, name):
    L, W = o.shape
    Wz = zsrc.shape[1]

    def body(o_ref, z_ref, y_ref):
        y_ref[...] = (o_ref[...] * _silu(z_ref[:, :W])).astype(_MXU)

    return pl.pallas_call(
        body, name=name,
        out_shape=jax.ShapeDtypeStruct((L, W), _MXU),
        grid=(L // ROW_TILE,),
        in_specs=[_rows(W), _rows(Wz)],
        out_specs=_rows(W),
        compiler_params=_params(("parallel",)),
    )(o, zsrc)


def _gate_bwd(dy, o, zsrc, heads, name):
    L, W = o.shape
    Wz = zsrc.shape[1]
    hd = W // heads

    def body(dy_ref, o_ref, z_ref, do_ref, dl_ref, dz_ref):
        z = z_ref[:, :W]
        dy = dy_ref[...]
        o = o_ref[...]
        do = dy * _silu(z)
        do_ref[...] = do.astype(_MXU)
        dz_ref[...] = (dy * o * _dsilu(z)).astype(_MXU)
        prod = do * o
        for h in range(heads):
            dl_ref[h] = _as_row(jnp.sum(prod[:, h * hd:(h + 1) * hd], axis=-1, keepdims=True))

    return pl.pallas_call(
        body, name=name,
        out_shape=(jax.ShapeDtypeStruct((L, W), _MXU), jax.ShapeDtypeStruct((heads, L // ROW_TILE, 1, ROW_TILE), F32),
                   jax.ShapeDtypeStruct((L, W), _MXU)),
        grid=(L // ROW_TILE,),
        in_specs=[_rows(W), _rows(W), _rows(Wz)],
        out_specs=(_rows(W), pl.BlockSpec((heads, None, 1, ROW_TILE), lambda i: (0, i, 0, 0)), _rows(W)),
        compiler_params=_params(("parallel",)),
    )(dy, o, zsrc)


def _split3(x):
    hi = x.astype(_MXU)
    r1 = x - hi.astype(F32)
    mid = r1.astype(_MXU)
    lo = (r1 - mid.astype(F32)).astype(_MXU)
    return hi, mid, lo


def _tri_cumsum(x, reverse):
    T = x.shape[0]
    r = lax.broadcasted_iota(jnp.int32, (T, T), 0)
    c = lax.broadcasted_iota(jnp.int32, (T, T), 1)
    tri = jnp.where((c >= r) if reverse else (c <= r), 1.0, 0.0).astype(_MXU)
    out = jnp.zeros(x.shape, F32)
    for part in _split3(x):
        out = out + jnp.dot(tri, part, preferred_element_type=F32)
    return out


def _fox_cum(zf, bias, pad, name):
    L, Wz = zf.shape
    off = Wz // LANES - 1

    def body(f_ref, b_ref, c_ref, carry_ref):
        i = pl.program_id(0)

        @pl.when(i == 0)
        def _():
            carry_ref[...] = jnp.zeros_like(carry_ref)

        x = f_ref[...] + b_ref[...]
        logf = jnp.minimum(x, 0.0) - jnp.log(1.0 + jnp.exp(-jnp.abs(x)))
        row = i * ROW_TILE + lax.broadcasted_iota(jnp.int32, logf.shape, 0)
        logf = jnp.where(row >= pad, logf, 0.0)
        cum = _tri_cumsum(logf, False) + carry_ref[...]
        c_ref[...] = cum
        carry_ref[...] = cum[ROW_TILE - 1:, :]

    return pl.pallas_call(
        body, name=name,
        out_shape=jax.ShapeDtypeStruct((L, LANES), F32),
        grid=(L // ROW_TILE,),
        in_specs=[pl.BlockSpec((ROW_TILE, LANES), lambda i: (i, off)), _whole((1, LANES))],
        out_specs=_rows(LANES),
        scratch_shapes=[pltpu.VMEM((1, LANES), F32)],
        compiler_params=_params(("arbitrary",)),
    )(zf, bias)


def _fox_cum_bwd(dcum, zf, bias, pad, name):
    L, Wz = zf.shape
    off = Wz // LANES - 1
    n = L // ROW_TILE

    def body(d_ref, f_ref, b_ref, df_ref, db_ref, carry_ref):
        i = pl.program_id(0)

        @pl.when(i == 0)
        def _():
            carry_ref[...] = jnp.zeros_like(carry_ref)
            db_ref[...] = jnp.zeros_like(db_ref)

        rc = _tri_cumsum(d_ref[...], True) + carry_ref[...]
        carry_ref[...] = rc[:1, :]
        x = f_ref[...] + b_ref[...]
        row = (n - 1 - i) * ROW_TILE + lax.broadcasted_iota(jnp.int32, x.shape, 0)
        df = jnp.where(row >= pad, rc * jax.nn.sigmoid(-x), 0.0)
        df_ref[...] = df.astype(_MXU)
        db_ref[...] += jnp.sum(df, axis=0, keepdims=True)

    return pl.pallas_call(
        body, name=name,
        out_shape=(jax.ShapeDtypeStruct((L, LANES), _MXU), jax.ShapeDtypeStruct((1, LANES), F32)),
        grid=(n,),
        in_specs=[pl.BlockSpec((ROW_TILE, LANES), lambda i: (n - 1 - i, 0)),
                  pl.BlockSpec((ROW_TILE, LANES), lambda i: (n - 1 - i, off)), _whole((1, LANES))],
        out_specs=(pl.BlockSpec((ROW_TILE, LANES), lambda i: (n - 1 - i, 0)), _whole((1, LANES))),
        scratch_shapes=[pltpu.VMEM((1, LANES), F32)],
        compiler_params=_params(("arbitrary",)),
    )(dcum, zf, bias)


LOG2E = 1.4426950408889634
LN2 = 0.6931471805599453


def _bias_terms(ct, pad):
    H, L = ct.shape
    T = ROW_TILE
    n = L // T
    c2 = ct * LOG2E
    k2 = jnp.where(jnp.arange(L)[None, :] < pad, -NEG, c2)
    return dict(cq_row=c2.reshape(H, n, 1, T), ck_row=k2.reshape(H, n, 1, T), ref=c2[:, ::T].reshape(H, n, 1, 1))


def _as_row(col):
    return jnp.transpose(jnp.broadcast_to(col, (col.shape[0], LANES)))[0:1, :]


def _as_col(row):
    return jnp.transpose(jnp.broadcast_to(row, (LANES, row.shape[1])))[:, 0:1]


def _attn_fwd(q, qoff, k, koff, v, voff, bias, heads, dk, dv, scale, name):
    L = q.shape[0]
    T = ROW_TILE
    n = L // T
    c = scale * LOG2E
    dn_qk = (((1,), (1,)), ((), ()))

    def body(q_ref, k_ref, v_ref, ck_ref, ref_ref, o_ref, lse_ref, s_a, s_b):
        i = pl.program_id(1)
        qb = q_ref[...]
        ref = ref_ref[...]

        def scores(j, dst):
            start = pl.multiple_of(j * T, T)
            dst[...] = lax.dot_general(qb, k_ref[pl.ds(start, T), :], dn_qk, preferred_element_type=F32)

        def soft(j, carry, cur, diag):
            m, l, acc = carry
            start = pl.multiple_of(j * T, T)
            vj = v_ref[pl.ds(start, T), :]
            t = cur[...] * c - (ck_ref[j] - ref)
            if diag:
                r = lax.broadcasted_iota(jnp.int32, (T, T), 0)
                cc = lax.broadcasted_iota(jnp.int32, (T, T), 1)
                t = jnp.where(cc <= r, t, NEG)
            m_new = jnp.maximum(m, jnp.max(t, axis=1, keepdims=True))
            p = jnp.exp2(t - m_new)
            a = jnp.exp2(m - m_new)
            l = a * l + jnp.sum(p, axis=1, keepdims=True)
            acc = a * acc + jnp.dot(p.astype(_MXU), vj, preferred_element_type=F32)
            return m_new, l, acc

        def step(j, carry, cur, nxt):
            scores(j + 1, nxt)
            return soft(j, carry, cur, False)

        def pair(jj, carry):
            return step(2 * jj + 1, step(2 * jj, carry, s_a, s_b), s_b, s_a)

        scores(0, s_a)
        init = (jnp.full((T, 1), -jnp.inf, F32), jnp.zeros((T, 1), F32), jnp.zeros((T, dv), F32))
        carry = lax.fori_loop(0, i // 2, pair, init)
        odd = (i % 2) == 1
        carry = lax.cond(odd, lambda cy: step(i - 1, cy, s_a, s_b), lambda cy: cy, carry)
        m, l, acc = lax.cond(odd, lambda cy: soft(i, cy, s_b, True), lambda cy: soft(i, cy, s_a, True), carry)
        o_ref[...] = acc / l
        lse_ref[...] = _as_row(m + jnp.log2(l))

    o, lse = pl.pallas_call(
        body, name=name,
        out_shape=(jax.ShapeDtypeStruct((L, heads * dv), F32), jax.ShapeDtypeStruct((heads, n, 1, T), F32)),
        grid=(heads, n),
        in_specs=[pl.BlockSpec((T, dk), lambda h, i: (i, qoff + h)),
                  pl.BlockSpec((L, dk), lambda h, i: (0, koff + h)),
                  pl.BlockSpec((L, dv), lambda h, i: (0, voff + h)),
                  pl.BlockSpec((None, n, 1, T), lambda h, i: (h, 0, 0, 0)),
                  pl.BlockSpec((None, None, 1, 1), lambda h, i: (h, i, 0, 0))],
        out_specs=(pl.BlockSpec((T, dv), lambda h, i: (i, h)),
                   pl.BlockSpec((None, None, 1, T), lambda h, i: (h, i, 0, 0))),
        scratch_shapes=[pltpu.VMEM((T, T), F32), pltpu.VMEM((T, T), F32)],
        compiler_params=_params(("parallel", "arbitrary")),
    )(q, k, v, bias["ck_row"], bias["ref"])
    return o, lse + (bias["cq_row"] - bias["ref"])


def _attn_bwd(q, qoff, k, koff, v, voff, do, bias, lse_row, delta_row, heads, dk, dv, scale, name):
    L = q.shape[0]
    T = ROW_TILE
    n = L // T
    c = scale * LOG2E
    dn_nt = (((1,), (1,)), ((), ()))
    dn_tn = (((0,), (0,)), ((), ()))

    def body(q_ref, k_ref, v_ref, do_ref, cq_ref, ck_ref, ref_ref, lse_ref, dl_ref,
             dq_hbm, dk_ref, dv_ref, dck_ref, dcq_ref, dq_acc, sem, st_a, dp_a, st_b, dp_b):
        h = pl.program_id(0)
        j = pl.program_id(1)

        @pl.when(j == 0)
        def _():
            dq_acc[...] = jnp.zeros_like(dq_acc)
            dcq_ref[...] = jnp.zeros_like(dcq_ref)

        kb = k_ref[...]
        vb = v_ref[...]
        ref = ref_ref[...]
        bcol = _as_col(ck_ref[j] - ref)

        def front(i, st_dst, dp_dst):
            start = pl.multiple_of(jnp.minimum(i, n - 1) * T, T)
            st_dst[...] = lax.dot_general(kb, q_ref[pl.ds(start, T), :], dn_nt, preferred_element_type=F32)
            dp_dst[...] = lax.dot_general(vb, do_ref[pl.ds(start, T), :], dn_nt, preferred_element_type=F32)

        def back(i, carry, st_cur, dp_cur, diag):
            dk_a, dv_a, dck_a = carry
            start = pl.multiple_of(i * T, T)
            qi = q_ref[pl.ds(start, T), :]
            doi = do_ref[pl.ds(start, T), :]
            arow = (cq_ref[i] - ref) - lse_ref[i]
            st = st_cur[...] * c + arow - bcol
            if diag:
                r = lax.broadcasted_iota(jnp.int32, (T, T), 0)
                cc = lax.broadcasted_iota(jnp.int32, (T, T), 1)
                st = jnp.where(r <= cc, st, NEG)
            pt = jnp.exp2(st)
            dv_a = dv_a + jnp.dot(pt.astype(_MXU), doi, preferred_element_type=F32)
            dst = pt * (dp_cur[...] - dl_ref[i])
            dck_a = dck_a - jnp.sum(dst, axis=1, keepdims=True)
            dcq_ref[i] += jnp.sum(dst, axis=0, keepdims=True)
            dsb = (dst * scale).astype(_MXU)
            dk_a = dk_a + jnp.dot(dsb, qi, preferred_element_type=F32)
            dq_acc[pl.ds(start, T), :] += lax.dot_general(dsb, kb, dn_tn, preferred_element_type=F32)
            return dk_a, dv_a, dck_a

        buf_a, buf_b = (st_a, dp_a), (st_b, dp_b)

        def step(i, carry, cur, nxt, diag=False):
            front(i + 1, *nxt)
            return back(i, carry, *cur, diag)

        def pair(t, carry):
            i0 = j + 1 + 2 * t
            return step(i0 + 1, step(i0, carry, buf_b, buf_a), buf_a, buf_b)

        front(j, *buf_a)
        init = (jnp.zeros((T, dk), F32), jnp.zeros((T, dv), F32), jnp.zeros((T, 1), F32))
        carry = step(j, init, buf_a, buf_b, True)
        rest = n - 1 - j
        carry = lax.fori_loop(0, rest // 2, pair, carry)
        dk_a, dv_a, dck_a = lax.cond((rest % 2) == 1, lambda cy: step(n - 1, cy, buf_b, buf_a),
                                     lambda cy: cy, carry)
        dk_ref[...] = dk_a
        dv_ref[...] = dv_a
        dck_ref[...] = _as_row(dck_a)

        @pl.when(j == n - 1)
        def _():
            cp = pltpu.make_async_copy(dq_acc, dq_hbm.at[:, pl.ds(pl.multiple_of(h * dk, dk), dk)], sem)
            cp.start()
            cp.wait()

    return pl.pallas_call(
        body, name=name,
        out_shape=(jax.ShapeDtypeStruct((L, heads * dk), F32), jax.ShapeDtypeStruct((L, heads * dk), F32),
                   jax.ShapeDtypeStruct((L, heads * dv), F32), jax.ShapeDtypeStruct((heads, n, 1, T), F32),
                   jax.ShapeDtypeStruct((heads, n, 1, T), F32)),
        grid=(heads, n),
        in_specs=[pl.BlockSpec((L, dk), lambda h, j: (0, qoff + h)),
                  pl.BlockSpec((T, dk), lambda h, j: (j, koff + h)),
                  pl.BlockSpec((T, dv), lambda h, j: (j, voff + h)),
                  pl.BlockSpec((L, dv), lambda h, j: (0, h)),
                  pl.BlockSpec((None, n, 1, T), lambda h, j: (h, 0, 0, 0)),
                  pl.BlockSpec((None, n, 1, T), lambda h, j: (h, 0, 0, 0)),
                  pl.BlockSpec((None, None, 1, 1), lambda h, j: (h, j, 0, 0)),
                  pl.BlockSpec((None, n, 1, T), lambda h, j: (h, 0, 0, 0)),
                  pl.BlockSpec((None, n, 1, T), lambda h, j: (h, 0, 0, 0))],
        out_specs=(pl.BlockSpec(memory_space=pl.ANY),
                   pl.BlockSpec((T, dk), lambda h, j: (j, h)),
                   pl.BlockSpec((T, dv), lambda h, j: (j, h)),
                   pl.BlockSpec((None, None, 1, T), lambda h, j: (h, j, 0, 0)),
                   pl.BlockSpec((None, n, 1, T), lambda h, j: (h, 0, 0, 0))),
        scratch_shapes=[pltpu.VMEM((L, dk), F32), pltpu.SemaphoreType.DMA] + [pltpu.VMEM((T, T), F32)] * 4,
        compiler_params=_params(("arbitrary", "arbitrary")),
    )(q, k, v, do, bias["cq_row"], bias["ck_row"], bias["ref"], lse_row, delta_row)


_CQ0, _CKV0, _KR0 = D_MODEL, D_MODEL + MLA_Q_LORA, D_MODEL + MLA_Q_LORA + MLA_KV_LORA
_MLA_PROJ = _KR0 + LANES


def _rms(x, g):
    ms = jnp.mean(x * x, axis=-1, keepdims=True)
    return x * lax.rsqrt(ms + NORM_EPS) * g


def _rms_bwd(x, g, dy):
    ms = jnp.mean(x * x, axis=-1, keepdims=True)
    r = lax.rsqrt(ms + NORM_EPS)
    xh = x * r
    dxh = dy * g
    dx = r * (dxh - xh * jnp.mean(dxh * xh, axis=-1, keepdims=True))
    return dx, jnp.sum(dy * xh, axis=0, keepdims=True)


def _mla_norm_fwd(proj, gq, gkv):
    L = proj.shape[0]

    def body(p_ref, gq_ref, gkv_ref, cq_ref, ckv_ref):
        cq_ref[...] = _rms(p_ref[:, _CQ0:_CKV0], gq_ref[...]).astype(_MXU)
        ckv_ref[...] = _rms(p_ref[:, _CKV0:_KR0], gkv_ref[...]).astype(_MXU)

    return pl.pallas_call(
        body, name="mla_norm_fwd",
        out_shape=(jax.ShapeDtypeStruct((L, MLA_Q_LORA), _MXU), jax.ShapeDtypeStruct((L, MLA_KV_LORA), _MXU)),
        grid=(L // ROW_TILE,),
        in_specs=[_rows(_MLA_PROJ), _whole((1, MLA_Q_LORA)), _whole((1, MLA_KV_LORA))],
        out_specs=(_rows(MLA_Q_LORA), _rows(MLA_KV_LORA)),
        compiler_params=_params(("parallel",)),
    )(proj, gq, gkv)


def _mla_norm_bwd(proj, gq, gkv, dcqn, dckvn):
    L = proj.shape[0]

    def body(p_ref, gq_ref, gkv_ref, dq_ref, dkv_ref, dcq_ref, dckv_ref, dgq_ref, dgkv_ref):
        @pl.when(pl.program_id(0) == 0)
        def _():
            dgq_ref[...] = jnp.zeros_like(dgq_ref)
            dgkv_ref[...] = jnp.zeros_like(dgkv_ref)

        dx, dg = _rms_bwd(p_ref[:, _CQ0:_CKV0], gq_ref[...], dq_ref[...])
        dcq_ref[...] = dx.astype(_MXU)
        dgq_ref[...] += dg
        dx, dg = _rms_bwd(p_ref[:, _CKV0:_KR0], gkv_ref[...], dkv_ref[...])
        dckv_ref[...] = dx.astype(_MXU)
        dgkv_ref[...] += dg

    return pl.pallas_call(
        body, name="mla_norm_bwd",
        out_shape=(jax.ShapeDtypeStruct((L, MLA_Q_LORA), _MXU), jax.ShapeDtypeStruct((L, MLA_KV_LORA), _MXU),
                   jax.ShapeDtypeStruct((1, MLA_Q_LORA), F32), jax.ShapeDtypeStruct((1, MLA_KV_LORA), F32)),
        grid=(L // ROW_TILE,),
        in_specs=[_rows(_MLA_PROJ), _whole((1, MLA_Q_LORA)), _whole((1, MLA_KV_LORA)),
                  _rows(MLA_Q_LORA), _rows(MLA_KV_LORA)],
        out_specs=(_rows(MLA_Q_LORA), _rows(MLA_KV_LORA), _whole((1, MLA_Q_LORA)), _whole((1, MLA_KV_LORA))),
        compiler_params=_params(("arbitrary",)),
    )(proj, gq, gkv, dcqn, dckvn)


def _rot_tile(t, cos, sa, sb):
    half = MLA_ROPE // 2
    return t * cos + pltpu.roll(t, LANES - half, 1) * sa + pltpu.roll(t, half, 1) * sb


def _mla_rope_fwd(qf, kv, proj, cos, sa, sb):
    L = qf.shape[0]
    H = MLA_HEADS

    def body(q_ref, kv_ref, p_ref, c_ref, sa_ref, sb_ref, qo_ref, ko_ref):
        cos, sa, sb = c_ref[...], sa_ref[...], sb_ref[...]
        kr = _rot_tile(p_ref[:, _KR0:_KR0 + LANES], cos, sa, sb).astype(_MXU)
        for h in range(H):
            b = h * MLA_DK
            qo_ref[:, b:b + LANES] = q_ref[:, b:b + LANES].astype(_MXU)
            qo_ref[:, b + LANES:b + 2 * LANES] = _rot_tile(q_ref[:, b + LANES:b + 2 * LANES], cos, sa, sb).astype(_MXU)
            ko_ref[:, b:b + LANES] = kv_ref[:, h * LANES:(h + 1) * LANES]
            ko_ref[:, b + LANES:b + 2 * LANES] = kr

    W = H * MLA_DK
    return pl.pallas_call(
        body, name="mla_rope_fwd",
        out_shape=(jax.ShapeDtypeStruct((L, W), _MXU), jax.ShapeDtypeStruct((L, W), _MXU)),
        grid=(L // ROW_TILE,),
        in_specs=[_rows(W), _rows(kv.shape[1]), _rows(_MLA_PROJ), _rows(LANES), _rows(LANES), _rows(LANES)],
        out_specs=(_rows(W), _rows(W)),
        compiler_params=_params(("parallel",)),
    )(qf, kv, proj, cos, sa, sb)


def _mla_rope_bwd(dq_full, dk_full, cos, sa, sb):
    L = dq_full.shape[0]
    H = MLA_HEADS
    W = H * MLA_DK

    def body(dq_ref, dk_ref, c_ref, sa_ref, sb_ref, dqf_ref, dkn_ref, dkr_ref):
        cos, sa, sb = c_ref[...], sa_ref[...], sb_ref[...]
        lane = lax.broadcasted_iota(jnp.int32, (ROW_TILE, LANES), 1)
        live = lane < MLA_ROPE
        krs = jnp.zeros((ROW_TILE, LANES), F32)
        for h in range(H):
            b = h * MLA_DK
            dqf_ref[:, b:b + LANES] = dq_ref[:, b:b + LANES].astype(_MXU)
            dqr = _rot_tile(dq_ref[:, b + LANES:b + 2 * LANES], cos, sa, sb)
            dqf_ref[:, b + LANES:b + 2 * LANES] = jnp.where(live, dqr, 0.0).astype(_MXU)
            dkn_ref[:, h * LANES:(h + 1) * LANES] = dk_ref[:, b:b + LANES].astype(_MXU)
            krs = krs + dk_ref[:, b + LANES:b + 2 * LANES]
        dkr_ref[...] = jnp.where(live, _rot_tile(krs, cos, sa, sb), 0.0).astype(_MXU)

    return pl.pallas_call(
        body, name="mla_rope_bwd",
        out_shape=(jax.ShapeDtypeStruct((L, W), _MXU), jax.ShapeDtypeStruct((L, H * LANES), _MXU),
                   jax.ShapeDtypeStruct((L, LANES), _MXU)),
        grid=(L // ROW_TILE,),
        in_specs=[_rows(W), _rows(W), _rows(LANES), _rows(LANES), _rows(LANES)],
        out_specs=(_rows(W), _rows(H * LANES), _rows(LANES)),
        compiler_params=_params(("parallel",)),
    )(dq_full, dk_full, cos, sa, sb)


_RET_QKW = RET_HEADS * RET_QK


def _ret_rope_fwd(qkz, cos, sin, pad):
    L = qkz.shape[0]
    hh = RET_QK // 2
    kscale = RET_QK ** -0.5

    def body(p_ref, c_ref, s_ref, q_ref, k_ref):
        cos, sin = c_ref[...], s_ref[...]
        row = pl.program_id(0) * ROW_TILE + lax.broadcasted_iota(jnp.int32, (ROW_TILE, hh), 0)
        keep = row >= pad
        for h in range(RET_HEADS):
            b = h * RET_QK
            t1, t2 = p_ref[:, b:b + hh], p_ref[:, b + hh:b + 2 * hh]
            q_ref[:, b:b + hh] = (t1 * cos - t2 * sin).astype(_MXU)
            q_ref[:, b + hh:b + 2 * hh] = (t2 * cos + t1 * sin).astype(_MXU)
            t1, t2 = p_ref[:, _RET_QKW + b:_RET_QKW + b + hh], p_ref[:, _RET_QKW + b + hh:_RET_QKW + b + 2 * hh]
            k_ref[:, b:b + hh] = jnp.where(keep, (t1 * cos - t2 * sin) * kscale, 0.0)
            k_ref[:, b + hh:b + 2 * hh] = jnp.where(keep, (t2 * cos + t1 * sin) * kscale, 0.0)

    return pl.pallas_call(
        body, name="ret_rope_fwd",
        out_shape=(jax.ShapeDtypeStruct((L, _RET_QKW), _MXU), jax.ShapeDtypeStruct((L, _RET_QKW), F32)),
        grid=(L // ROW_TILE,),
        in_specs=[_rows(qkz.shape[1]), _rows(hh), _rows(hh)],
        out_specs=(_rows(_RET_QKW), _rows(_RET_QKW)),
        compiler_params=_params(("parallel",)),
    )(qkz, cos, sin)


def _ret_rope_bwd(dqr, dkr, cos, sin, pad):
    L = dqr.shape[0]
    hh = RET_QK // 2
    kscale = RET_QK ** -0.5

    def body(dq_ref, dk_ref, c_ref, s_ref, oq_ref, ok_ref):
        cos, sin = c_ref[...], s_ref[...]
        row = pl.program_id(0) * ROW_TILE + lax.broadcasted_iota(jnp.int32, (ROW_TILE, hh), 0)
        keep = row >= pad
        for h in range(RET_HEADS):
            b = h * RET_QK
            d1, d2 = dq_ref[:, b:b + hh], dq_ref[:, b + hh:b + 2 * hh]
            oq_ref[:, b:b + hh] = (d1 * cos + d2 * sin).astype(_MXU)
            oq_ref[:, b + hh:b + 2 * hh] = (d2 * cos - d1 * sin).astype(_MXU)
            d1, d2 = dk_ref[:, b:b + hh], dk_ref[:, b + hh:b + 2 * hh]
            ok_ref[:, b:b + hh] = jnp.where(keep, (d1 * cos + d2 * sin) * kscale, 0.0).astype(_MXU)
            ok_ref[:, b + hh:b + 2 * hh] = jnp.where(keep, (d2 * cos - d1 * sin) * kscale, 0.0).astype(_MXU)

    return pl.pallas_call(
        body, name="ret_rope_bwd",
        out_shape=(jax.ShapeDtypeStruct((L, _RET_QKW), _MXU), jax.ShapeDtypeStruct((L, _RET_QKW), _MXU)),
        grid=(L // ROW_TILE,),
        in_specs=[_rows(_RET_QKW), _rows(_RET_QKW), _rows(hh), _rows(hh)],
        out_specs=(_rows(_RET_QKW), _rows(_RET_QKW)),
        compiler_params=_params(("parallel",)),
    )(dqr, dkr, cos, sin)


def _ret_decays():
    lg = np.log1p(-np.exp2(-5.0 - np.arange(RET_HEADS, dtype=np.float32))).astype(np.float32)
    i = np.arange(CHUNK, dtype=np.float32)
    rel = i[:, None] - i[None, :]
    dm = np.where(rel[None] >= 0, np.exp(rel[None] * lg[:, None, None]), 0.0).astype(np.float32)
    qd = np.exp((i[None, :] + 1.0) * lg[:, None]).astype(np.float32)[:, :, None]
    kd = np.exp((CHUNK - 1.0 - i)[None, :] * lg[:, None]).astype(np.float32)[:, :, None]
    cd = np.broadcast_to(np.exp(CHUNK * lg).astype(np.float32)[:, None, None], (RET_HEADS, 1, RET_V))
    return jnp.asarray(dm), jnp.asarray(qd), jnp.asarray(kd), jnp.asarray(np.ascontiguousarray(cd))


def _ret_fwd(qr, kr, v):
    L = qr.shape[0]
    nc = L // CHUNK
    dm, qd, kd, cd = _ret_decays()
    dn_nt = (((1,), (1,)), ((), ()))
    dn_tn = (((0,), (0,)), ((), ()))

    def body(q_ref, k_ref, v_ref, dm_ref, qd_ref, kd_ref, cd_ref, o_ref, st_ref, s_ref):
        @pl.when(pl.program_id(1) == 0)
        def _():
            s_ref[...] = jnp.zeros_like(s_ref)

        q, k, vv = q_ref[...], k_ref[...], v_ref[...]
        sb = s_ref[...].astype(_MXU)
        st_ref[...] = sb
        s = lax.dot_general(q, k.astype(_MXU), dn_nt, preferred_element_type=F32) * dm_ref[...]
        intra = jnp.dot(s.astype(_MXU), vv, preferred_element_type=F32)
        cross = jnp.dot(q, sb, preferred_element_type=F32) * qd_ref[...]
        o_ref[...] = intra + cross
        kdm = (k * kd_ref[...]).astype(_MXU)
        s_ref[...] = cd_ref[...] * s_ref[...] + lax.dot_general(kdm, vv, dn_tn, preferred_element_type=F32)

    return pl.pallas_call(
        body, name="ret_fwd",
        out_shape=(jax.ShapeDtypeStruct((L, RET_HEADS * RET_V), F32),
                   jax.ShapeDtypeStruct((RET_HEADS, nc, RET_QK, RET_V), _MXU)),
        grid=(RET_HEADS, nc),
        in_specs=[pl.BlockSpec((CHUNK, RET_QK), lambda h, c: (c, h)),
                  pl.BlockSpec((CHUNK, RET_QK), lambda h, c: (c, h)),
                  pl.BlockSpec((CHUNK, RET_V), lambda h, c: (c, h)),
                  pl.BlockSpec((None, CHUNK, CHUNK), lambda h, c: (h, 0, 0)),
                  pl.BlockSpec((None, CHUNK, 1), lambda h, c: (h, 0, 0)),
                  pl.BlockSpec((None, CHUNK, 1), lambda h, c: (h, 0, 0)),
                  pl.BlockSpec((None, 1, RET_V), lambda h, c: (h, 0, 0))],
        out_specs=(pl.BlockSpec((CHUNK, RET_V), lambda h, c: (c, h)),
                   pl.BlockSpec((None, None, RET_QK, RET_V), lambda h, c: (h, c, 0, 0))),
        scratch_shapes=[pltpu.VMEM((RET_QK, RET_V), F32)],
        compiler_params=_params(("parallel", "arbitrary")),
    )(qr, kr, v, dm, qd, kd, cd)


def _ret_bwd(qr, kr, v, do, states):
    L = qr.shape[0]
    nc = L // CHUNK
    dm, qd, kd, cd = _ret_decays()
    dn_nt = (((1,), (1,)), ((), ()))
    dn_tn = (((0,), (0,)), ((), ()))

    def body(q_ref, k_ref, v_ref, do_ref, st_ref, dm_ref, qd_ref, kd_ref, cd_ref,
             dq_ref, dk_ref, dv_ref, ds_ref):
        @pl.when(pl.program_id(1) == 0)
        def _():
            ds_ref[...] = jnp.zeros_like(ds_ref)

        q, k, vv, do = q_ref[...], k_ref[...], v_ref[...], do_ref[...]
        kb = k.astype(_MXU)
        dob = do.astype(_MXU)
        doq = (do * qd_ref[...]).astype(_MXU)
        dsb = ds_ref[...].astype(_MXU)
        dmat = dm_ref[...]
        s = (lax.dot_general(q, kb, dn_nt, preferred_element_type=F32) * dmat).astype(_MXU)
        draw = (lax.dot_general(dob, vv, dn_nt, preferred_element_type=F32) * dmat).astype(_MXU)
        kdm = (k * kd_ref[...]).astype(_MXU)
        dv = (lax.dot_general(s, dob, dn_tn, preferred_element_type=F32)
              + jnp.dot(kdm, dsb, preferred_element_type=F32))
        dq = (jnp.dot(draw, kb, preferred_element_type=F32)
              + lax.dot_general(doq, st_ref[...], dn_nt, preferred_element_type=F32))
        dk = (lax.dot_general(draw, q, dn_tn, preferred_element_type=F32)
              + lax.dot_general(vv, dsb, dn_nt, preferred_element_type=F32) * kd_ref[...])
        dq_ref[...] = dq
        dk_ref[...] = dk
        dv_ref[...] = dv.astype(_MXU)
        ds_ref[...] = cd_ref[...] * ds_ref[...] + lax.dot_general(q, doq, dn_tn, preferred_element_type=F32)

    rev = lambda h, c: (nc - 1 - c, h)
    return pl.pallas_call(
        body, name="ret_bwd",
        out_shape=(jax.ShapeDtypeStruct((L, _RET_QKW), F32), jax.ShapeDtypeStruct((L, _RET_QKW), F32),
                   jax.ShapeDtypeStruct((L, RET_HEADS * RET_V), _MXU)),
        grid=(RET_HEADS, nc),
        in_specs=[pl.BlockSpec((CHUNK, RET_QK), rev),
                  pl.BlockSpec((CHUNK, RET_QK), rev),
                  pl.BlockSpec((CHUNK, RET_V), rev),
                  pl.BlockSpec((CHUNK, RET_V), rev),
                  pl.BlockSpec((None, None, RET_QK, RET_V), lambda h, c: (h, nc - 1 - c, 0, 0)),
                  pl.BlockSpec((None, CHUNK, CHUNK), lambda h, c: (h, 0, 0)),
                  pl.BlockSpec((None, CHUNK, 1), lambda h, c: (h, 0, 0)),
                  pl.BlockSpec((None, CHUNK, 1), lambda h, c: (h, 0, 0)),
                  pl.BlockSpec((None, 1, RET_V), lambda h, c: (h, 0, 0))],
        out_specs=(pl.BlockSpec((CHUNK, RET_QK), rev), pl.BlockSpec((CHUNK, RET_QK), rev),
                   pl.BlockSpec((CHUNK, RET_V), rev)),
        scratch_shapes=[pltpu.VMEM((RET_QK, RET_V), F32)],
        compiler_params=_params(("parallel", "arbitrary")),
    )(qr, kr, v, do, states, dm, qd, kd, cd)


def _gn(o, g):
    mu = jnp.mean(o, axis=-1, keepdims=True)
    oc = o - mu
    var = jnp.mean(oc * oc, axis=-1, keepdims=True)
    return oc * lax.rsqrt(var + NORM_EPS) * g


def _ret_gate_fwd(o, qkz, gn_g):
    L, W = o.shape
    z0 = qkz.shape[1] - W

    def body(o_ref, p_ref, g_ref, y_ref):
        for h in range(RET_HEADS):
            sl = slice(h * RET_V, (h + 1) * RET_V)
            z = p_ref[:, z0 + h * RET_V:z0 + (h + 1) * RET_V]
            y_ref[:, sl] = (_gn(o_ref[:, sl], g_ref[:, sl]) * _silu(z)).astype(_MXU)

    return pl.pallas_call(
        body, name="ret_gate_fwd",
        out_shape=jax.ShapeDtypeStruct((L, W), _MXU),
        grid=(L // ROW_TILE,),
        in_specs=[_rows(W), _rows(qkz.shape[1]), _whole((1, W))],
        out_specs=_rows(W),
        compiler_params=_params(("parallel",)),
    )(o, qkz, gn_g)


def _ret_gate_bwd(dy, o, qkz, gn_g):
    L, W = o.shape
    z0 = qkz.shape[1] - W

    def body(dy_ref, o_ref, p_ref, g_ref, do_ref, dz_ref, dg_ref):
        @pl.when(pl.program_id(0) == 0)
        def _():
            dg_ref[...] = jnp.zeros_like(dg_ref)

        for h in range(RET_HEADS):
            sl = slice(h * RET_V, (h + 1) * RET_V)
            z = p_ref[:, z0 + h * RET_V:z0 + (h + 1) * RET_V]
            o = o_ref[:, sl]
            g = g_ref[:, sl]
            dy = dy_ref[:, sl]
            mu = jnp.mean(o, axis=-1, keepdims=True)
            oc = o - mu
            var = jnp.mean(oc * oc, axis=-1, keepdims=True)
            r = lax.rsqrt(var + NORM_EPS)
            xh = oc * r
            dn = dy * _silu(z)
            dz_ref[:, sl] = (dy * (xh * g) * _dsilu(z)).astype(_MXU)
            dg_ref[:, sl] += jnp.sum(dn * xh, axis=0, keepdims=True)
            dxh = dn * g
            do_ref[:, sl] = r * (dxh - jnp.mean(dxh, axis=-1, keepdims=True)
                                 - xh * jnp.mean(dxh * xh, axis=-1, keepdims=True))

    return pl.pallas_call(
        body, name="ret_gate_bwd",
        out_shape=(jax.ShapeDtypeStruct((L, W), F32), jax.ShapeDtypeStruct((L, W), _MXU),
                   jax.ShapeDtypeStruct((1, W), F32)),
        grid=(L // ROW_TILE,),
        in_specs=[_rows(W), _rows(W), _rows(qkz.shape[1]), _whole((1, W))],
        out_specs=(_rows(W), _rows(W), _whole((1, W))),
        compiler_params=_params(("arbitrary",)),
    )(dy, o, qkz, gn_g)


def _heads_t(a):
    return a[:, :FOX_HEADS].T


def _fox_fwd(h, w, lnw, pad, tag):
    w_in, b_f, w_out = w
    L = h.shape[0]
    W = D_MODEL
    w_qkv = w_in[:, :3 * W]
    w_zf = jnp.concatenate([w_in[:, 3 * W:4 * W], jnp.pad(w_in[:, 4 * W:], ((0, 0), (0, LANES - FOX_HEADS)))], axis=1)
    bias = jnp.pad(b_f, (0, LANES - FOX_HEADS))[None, :]
    qkv = _matmul(h, w_qkv, "nn", _MXU, f"{tag}_qkv")
    zf = _matmul(h, w_zf, "nn", F32, f"{tag}_zf")
    cum = _fox_cum(zf, bias, pad, f"{tag}_cum")
    ab = _bias_terms(_heads_t(cum), pad)
    nb = W // LANES
    o, lse = _attn_fwd(qkv, 0, qkv, nb, qkv, 2 * nb, ab, FOX_HEADS, LANES, LANES, LANES ** -0.5, f"{tag}_attn_fwd")
    y = _gate_fwd(o, zf, f"{tag}_gate_fwd")
    mo = _matmul(y, w_out, "nn", F32, f"{tag}_out")
    zl, hn = _ln_fwd(h, mo, lnw[0], lnw[1], f"{tag}_ln_fwd")
    return hn, dict(h=h, qkv=qkv, zf=zf, ab=ab, o=o, lse=lse, y=y, zl=zl,
                    w_all=jnp.concatenate([w_qkv, w_zf], axis=1), w_out=w_out, bias=bias)


def _fox_bwd(s, dzl, pad, tag):
    W = D_MODEL
    nb = W // LANES
    dy = _matmul(dzl, s["w_out"], "nt", F32, f"{tag}_dy")
    d_wout = _matmul(s["y"], dzl, "tn", F32, f"{tag}_dwout")
    do, delta, dzg = _gate_bwd(dy, s["o"], s["zf"], FOX_HEADS, f"{tag}_gate_bwd")
    qkv = s["qkv"]
    dq, dk, dv, dck, dcq = _attn_bwd(qkv, 0, qkv, nb, qkv, 2 * nb, do, s["ab"], s["lse"], delta,
                                     FOX_HEADS, LANES, LANES, LANES ** -0.5, f"{tag}_attn_bwd")
    dcum = jnp.pad((dck + dcq).reshape(FOX_HEADS, -1).T, ((0, 0), (0, LANES - FOX_HEADS)))
    df, dbias = _fox_cum_bwd(dcum, s["zf"], s["bias"], pad, f"{tag}_cum_bwd")
    dproj = jnp.concatenate([dq.astype(_MXU), dk.astype(_MXU), dv.astype(_MXU), dzg, df], axis=1)
    d_wall = _matmul(s["h"], dproj, "tn", F32, f"{tag}_dwin")
    dh = _matmul(dproj, s["w_all"], "nt", F32, f"{tag}_dh")
    d_win = jnp.concatenate([d_wall[:, :4 * W], d_wall[:, 4 * W:4 * W + FOX_HEADS]], axis=1)
    return dh, (d_win, dbias[0, :FOX_HEADS], d_wout)


def _mla_tables(L, pad):
    pos = (jnp.arange(L) - pad).astype(F32)
    inv = ROPE_BASE ** (-jnp.arange(0, MLA_ROPE, 2, dtype=F32) / MLA_ROPE)
    ang = pos[:, None] * inv[None, :]
    c, s = jnp.cos(ang), jnp.sin(ang)
    z32, z64 = jnp.zeros_like(s), jnp.zeros((L, LANES - MLA_ROPE), F32)
    cos = jnp.concatenate([c, c, jnp.ones_like(z64)], axis=1)
    sa = jnp.concatenate([-s, z32, z64], axis=1)
    sb = jnp.concatenate([z32, s, z64], axis=1)
    return cos, sa, sb


def _mla_weights(w_in, w_uq, w_ukv):
    a, b = MLA_Q_LORA, MLA_Q_LORA + MLA_KV_LORA
    w_zc = jnp.concatenate([w_in[:, b + MLA_ROPE:], w_in[:, :b + MLA_ROPE],
                            jnp.zeros((D_MODEL, LANES - MLA_ROPE), w_in.dtype)], axis=1)
    uq = w_uq.reshape(MLA_Q_LORA, MLA_HEADS, MLA_NOPE + MLA_ROPE)
    uq = jnp.pad(uq, ((0, 0), (0, 0), (0, MLA_DK - MLA_NOPE - MLA_ROPE))).reshape(MLA_Q_LORA, MLA_HEADS * MLA_DK)
    ukv = w_ukv.reshape(MLA_KV_LORA, MLA_HEADS, MLA_NOPE + MLA_V)
    ukv = jnp.concatenate([ukv[:, :, :MLA_NOPE].reshape(MLA_KV_LORA, -1), ukv[:, :, MLA_NOPE:].reshape(MLA_KV_LORA, -1)], axis=1)
    return w_zc, uq, ukv


def _mla_fwd(h, w, lnw, pad):
    w_in, gq, gkv, w_uq, w_ukv, w_out = w
    L = h.shape[0]
    w_zc, uq, ukv = _mla_weights(w_in, w_uq, w_ukv)
    cos, sa, sb = _mla_tables(L, pad)
    proj = _matmul(h, w_zc, "nn", F32, "mla_proj")
    cqn, ckvn = _mla_norm_fwd(proj, gq[None, :], gkv[None, :])
    qf = _matmul(cqn, uq, "nn", F32, "mla_uq")
    kv = _matmul(ckvn, ukv, "nn", _MXU, "mla_ukv")
    q_full, k_full = _mla_rope_fwd(qf, kv, proj, cos, sa, sb)
    ab = _bias_terms(jnp.zeros((MLA_HEADS, L), F32), pad)
    nb = MLA_HEADS * MLA_NOPE // LANES
    o, lse = _attn_fwd(q_full, 0, k_full, 0, kv, nb, ab, MLA_HEADS, MLA_DK, MLA_V,
                       (MLA_NOPE + MLA_ROPE) ** -0.5, "mla_attn_fwd")
    y = _gate_fwd(o, proj, "mla_gate_fwd")
    mo = _matmul(y, w_out, "nn", F32, "mla_out")
    zl, hn = _ln_fwd(h, mo, lnw[0], lnw[1], "mla_ln_fwd")
    return hn, dict(h=h, proj=proj, cqn=cqn, ckvn=ckvn, kv=kv, q_full=q_full, k_full=k_full, ab=ab,
                    o=o, lse=lse, y=y, zl=zl, w_zc=w_zc, uq=uq, ukv=ukv, w_out=w_out, gq=gq, gkv=gkv,
                    tabs=(cos, sa, sb))


def _mla_bwd(s, dzl):
    H = MLA_HEADS
    dy = _matmul(dzl, s["w_out"], "nt", F32, "mla_dy")
    d_wout = _matmul(s["y"], dzl, "tn", F32, "mla_dwout")
    do, delta, dzg = _gate_bwd(dy, s["o"], s["proj"], H, "mla_gate_bwd")
    nb = H * MLA_NOPE // LANES
    dq_full, dk_full, dv, _, _ = _attn_bwd(s["q_full"], 0, s["k_full"], 0, s["kv"], nb, do, s["ab"],
                                           s["lse"], delta,
                                           H, MLA_DK, MLA_V, (MLA_NOPE + MLA_ROPE) ** -0.5, "mla_attn_bwd")
    cos, sa, sb = s["tabs"]
    dqf, dkn, dkr = _mla_rope_bwd(dq_full, dk_full, cos, -sa, -sb)
    d_uq = _matmul(s["cqn"], dqf, "tn", F32, "mla_duq")
    dcqn = _matmul(dqf, s["uq"], "nt", F32, "mla_dcqn")
    dkv = jnp.concatenate([dkn, dv.astype(_MXU)], axis=1)
    d_ukv = _matmul(s["ckvn"], dkv, "tn", F32, "mla_dukv")
    dckvn = _matmul(dkv, s["ukv"], "nt", F32, "mla_dckvn")
    dcq, dckv, dgq, dgkv = _mla_norm_bwd(s["proj"], s["gq"][None, :], s["gkv"][None, :], dcqn, dckvn)
    dproj = jnp.concatenate([dzg, dcq, dckv, dkr], axis=1)
    d_wzc = _matmul(s["h"], dproj, "tn", F32, "mla_dwin")
    dh = _matmul(dproj, s["w_zc"], "nt", F32, "mla_dh")
    W = D_MODEL
    d_win = jnp.concatenate([d_wzc[:, W:W + MLA_Q_LORA + MLA_KV_LORA + MLA_ROPE], d_wzc[:, :W]], axis=1)
    d_wuq = d_uq.reshape(MLA_Q_LORA, H, MLA_DK)[:, :, :MLA_NOPE + MLA_ROPE].reshape(MLA_Q_LORA, -1)
    hk = H * MLA_NOPE
    d_wukv = jnp.concatenate([d_ukv[:, :hk].reshape(MLA_KV_LORA, H, MLA_NOPE),
                              d_ukv[:, hk:].reshape(MLA_KV_LORA, H, MLA_V)], axis=2).reshape(MLA_KV_LORA, -1)
    return dh, (d_win, dgq[0], dgkv[0], d_wuq, d_wukv, d_wout)


def _ret_tables(L, pad):
    pos = (jnp.arange(L) - pad).astype(F32)
    inv = 1.0 / (ROPE_BASE ** jnp.linspace(0.0, 1.0, RET_QK // 2, dtype=F32))
    ang = pos[:, None] * inv[None, :]
    return jnp.cos(ang), jnp.sin(ang)


def _ret_layer_fwd(h, w, lnw, pad):
    w_in, gn_g, w_out = w
    L = h.shape[0]
    q1 = 2 * _RET_QKW
    v1 = q1 + RET_HEADS * RET_V
    w_qkz = jnp.concatenate([w_in[:, :q1], w_in[:, v1:]], axis=1)
    cos, sin = _ret_tables(L, pad)
    qkz = _matmul(h, w_qkz, "nn", F32, "ret_qkz")
    v = _matmul(h, w_in[:, q1:v1], "nn", _MXU, "ret_v")
    qr, kr = _ret_rope_fwd(qkz, cos, sin, pad)
    o, states = _ret_fwd(qr, kr, v)
    y = _ret_gate_fwd(o, qkz, gn_g[None, :])
    mo = _matmul(y, w_out, "nn", F32, "ret_out")
    zl, hn = _ln_fwd(h, mo, lnw[0], lnw[1], "ret_ln_fwd")
    return hn, dict(h=h, qkz=qkz, v=v, qr=qr, kr=kr, o=o, states=states, y=y, zl=zl, w_in=w_in, w_out=w_out,
                    gn_g=gn_g, tabs=(cos, sin))


def _ret_layer_bwd(s, dzl, pad):
    dy = _matmul(dzl, s["w_out"], "nt", F32, "ret_dy")
    d_wout = _matmul(s["y"], dzl, "tn", F32, "ret_dwout")
    do, dzg, dgn = _ret_gate_bwd(dy, s["o"], s["qkz"], s["gn_g"][None, :])
    dqr, dkr, dv = _ret_bwd(s["qr"], s["kr"], s["v"], do, s["states"])
    cos, sin = s["tabs"]
    dq, dk = _ret_rope_bwd(dqr, dkr, cos, sin, pad)
    dproj = jnp.concatenate([dq, dk, dv, dzg], axis=1)
    d_win = _matmul(s["h"], dproj, "tn", F32, "ret_dwin")
    dh = _matmul(dproj, s["w_in"], "nt", F32, "ret_dh")
    return dh, (d_win, dgn[0], d_wout)


def _local_step(x, target, meta, fox, mla, ret, ln_g, ln_b):
    n_real = x.shape[0]
    first = -(-N_META // ROW_TILE)
    pad = first * ROW_TILE - N_META
    h0 = jnp.concatenate([jnp.zeros((pad, D_MODEL), F32), meta, x], axis=0)
    lnw = [(ln_g[i][None, :], ln_b[i][None, :]) for i in range(4)]

    h1, s0 = _fox_fwd(h0, (fox[0][0], fox[1][0], fox[2][0]), lnw[0], pad, "fox0")
    h2, s1 = _mla_fwd(h1, tuple(a[0] for a in mla), lnw[1], pad)
    h3, s2 = _ret_layer_fwd(h2, tuple(a[0] for a in ret), lnw[2], pad)
    h4, s3 = _fox_fwd(h3, (fox[0][1], fox[1][1], fox[2][1]), lnw[3], pad, "fox1")

    dh4, loss = _loss_head(h4, target, first)
    dz3, dg3, db3 = _ln_bwd(s3["zl"], lnw[3][0], dh4, None, "ln3_bwd")
    dm3, gf1 = _fox_bwd(s3, dz3, pad, "fox1")
    dz2, dg2, db2 = _ln_bwd(s2["zl"], lnw[2][0], dz3, dm3, "ln2_bwd")
    dm2, gr = _ret_layer_bwd(s2, dz2, pad)
    dz1, dg1, db1 = _ln_bwd(s1["zl"], lnw[1][0], dz2, dm2, "ln1_bwd")
    dm1, gm = _mla_bwd(s1, dz1)
    dz0, dg0, db0 = _ln_bwd(s0["zl"], lnw[0][0], dz1, dm1, "ln0_bwd")
    dm0, gf0 = _fox_bwd(s0, dz0, pad, "fox0")
    grad_x, grad_meta = _input_grads(dz0, dm0, first, n_real)

    grads = dict(
        meta=grad_meta,
        fox_w_in=jnp.stack([gf0[0], gf1[0]]), fox_b_f=jnp.stack([gf0[1], gf1[1]]),
        fox_w_out=jnp.stack([gf0[2], gf1[2]]),
        mla_w_in=gm[0][None], mla_q_norm=gm[1][None], mla_kv_norm=gm[2][None],
        mla_w_uq=gm[3][None], mla_w_ukv=gm[4][None], mla_w_out=gm[5][None],
        ret_w_in=gr[0][None], ret_gn_g=gr[1][None], ret_w_out=gr[2][None],
        ln_g=jnp.concatenate([dg0, dg1, dg2, dg3], axis=0),
        ln_b=jnp.concatenate([db0, db1, db2, db3], axis=0),
    )
    return loss, grad_x, grads


def _exchange(tensors, scatter, name):
    nt = len(tensors)

    def body(*refs):
        srcs, outs = refs[:nt], refs[nt:2 * nt]
        send_sems, recv_sems, local_sems = refs[2 * nt:]
        x, y, c = lax.axis_index("x"), lax.axis_index("y"), lax.axis_index("c")
        me = 4 * x + 2 * y + c
        local = [pltpu.make_async_copy(srcs[t].at[me] if scatter else srcs[t], outs[t].at[me], local_sems.at[t])
                 for t in range(nt)]
        for cp in local:
            cp.start()
        copies = []
        for k in range(1, N_DEV):
            px = (x + (k >> 2)) % 2
            py = (y + ((k >> 1) & 1)) % 2
            pc = (c + (k & 1)) % 2
            peer = 4 * px + 2 * py + pc
            for t in range(nt):
                copies.append(pltpu.make_async_remote_copy(
                    src_ref=srcs[t].at[peer] if scatter else srcs[t],
                    dst_ref=outs[t].at[me],
                    send_sem=send_sems.at[k - 1, t], recv_sem=recv_sems.at[k - 1, t],
                    device_id=(px, py, pc), device_id_type=pl.DeviceIdType.MESH))
        for cp in copies:
            cp.start()
        for cp in copies:
            cp.wait()
        for cp in local:
            cp.wait()

    hbm = pl.BlockSpec(memory_space=pl.ANY)
    return pl.pallas_call(
        body, name=name,
        out_shape=tuple(jax.ShapeDtypeStruct(t.shape if scatter else (N_DEV,) + t.shape, t.dtype) for t in tensors),
        in_specs=[hbm] * nt,
        out_specs=(hbm,) * nt,
        scratch_shapes=[pltpu.SemaphoreType.DMA((N_DEV - 1, nt)), pltpu.SemaphoreType.DMA((N_DEV - 1, nt)),
                        pltpu.SemaphoreType.DMA((nt,))],
    )(*tensors)


def _adamw(parts, w, m, v, name):
    R, C = w.shape
    tr = next(t for t in range(min(R, 256), 0, -1) if R % t == 0 and (t % 8 == 0 or t == R))

    def body(p_ref, w_ref, m_ref, v_ref, g_ref, d_ref, mo_ref, vo_ref):
        g = p_ref[0]
        for d in range(1, N_DEV):
            g = g + p_ref[d]
        mn = ADAM_B1 * m_ref[...] + (1.0 - ADAM_B1) * g
        vn = ADAM_B2 * v_ref[...] + (1.0 - ADAM_B2) * (g * g)
        m_hat = mn / (1.0 - ADAM_B1 ** ADAM_STEP)
        v_hat = vn / (1.0 - ADAM_B2 ** ADAM_STEP)
        g_ref[...] = g
        d_ref[...] = -ADAM_LR * (m_hat / (jnp.sqrt(v_hat) + ADAM_EPS) + ADAM_WD * w_ref[...])
        mo_ref[...] = mn
        vo_ref[...] = vn

    blk = pl.BlockSpec((tr, C), lambda i: (i, 0))
    sd = jax.ShapeDtypeStruct((R, C), F32)
    return pl.pallas_call(
        body, name=name,
        out_shape=(sd, sd, sd, sd),
        grid=(R // tr,),
        in_specs=[pl.BlockSpec((N_DEV, tr, C), lambda i: (0, i, 0)), blk, blk, blk],
        out_specs=(blk, blk, blk, blk),
        compiler_params=_params(("parallel",)),
    )(parts, w, m, v)


_SPECS = dict(
    meta=((16, 1024), 1),
    fox_w_in=((2, 1024, 4104), 2), fox_b_f=((2, 8), None), fox_w_out=((2, 1024, 1024), 1),
    mla_w_in=((1, 1024, 1728), 2), mla_q_norm=((1, 384), None), mla_kv_norm=((1, 256), None),
    mla_w_uq=((1, 384, 1536), 2), mla_w_ukv=((1, 256, 2048), 2), mla_w_out=((1, 1024, 1024), 1),
    ret_w_in=((1, 1024, 6144), 2), ret_gn_g=((1, 2048), 1), ret_w_out=((1, 2048, 1024), 1),
    ln_g=((4, 1024), None), ln_b=((4, 1024), None),
)
_NAMES = list(_SPECS)
_BIG = ["fox_w_in", "fox_w_out", "mla_w_in", "mla_w_uq", "mla_w_ukv", "mla_w_out", "ret_w_in", "ret_w_out"]
_SMALL = [n for n in _NAMES if n not in _BIG]
_SMALL_GATHERED = ["meta", "ret_gn_g"]
_PACK_ROWS = 8


def _shard_shape(name):
    shape, ax = _SPECS[name]
    if ax is None:
        return shape
    return tuple(s // N_DEV if i == ax else s for i, s in enumerate(shape))


def _split(name, full):
    shape, ax = _SPECS[name]
    parts = full.reshape(shape[:ax] + (N_DEV, shape[ax] // N_DEV) + shape[ax + 1:])
    return jnp.moveaxis(parts, ax, 0)


def _merge(name, stacked):
    shape, ax = _SPECS[name]
    return jnp.moveaxis(stacked, 0, ax).reshape(shape)


def _pad_rows(flat):
    n = flat.shape[-1]
    unit = _PACK_ROWS * LANES
    total = -(-n // unit) * unit
    flat = jnp.pad(flat, [(0, 0)] * (flat.ndim - 1) + [(0, total - n)])
    return flat.reshape(flat.shape[:-1] + (total // LANES, LANES))


def _pack_small(tree, names, tail):
    return _pad_rows(jnp.concatenate([tree[n].reshape(-1) for n in names] + [tail]))


def _unpack_small(buf, names):
    flat, out, at = buf.reshape(-1), {}, 0
    for n in names:
        shp = _shard_shape(n)
        out[n] = flat[at:at + math.prod(shp)].reshape(shp)
        at += math.prod(shp)
    return out, flat[at:]


def kernel(x, meta, fox_w_in, fox_b_f, fox_w_out, mla_w_in, mla_q_norm, mla_kv_norm, mla_w_uq, mla_w_ukv, mla_w_out, ret_w_in, ret_gn_g, ret_w_out, ln_g, ln_b, loss_target, m_meta, m_fox_w_in, m_fox_b_f, m_fox_w_out, m_mla_w_in, m_mla_q_norm, m_mla_kv_norm, m_mla_w_uq, m_mla_w_ukv, m_mla_w_out, m_ret_w_in, m_ret_gn_g, m_ret_w_out, m_ln_g, m_ln_b, v_meta, v_fox_w_in, v_fox_b_f, v_fox_w_out, v_mla_w_in, v_mla_q_norm, v_mla_kv_norm, v_mla_w_uq, v_mla_w_ukv, v_mla_w_out, v_ret_w_in, v_ret_gn_g, v_ret_w_out, v_ln_g, v_ln_b):
    w = dict(meta=meta, fox_w_in=fox_w_in, fox_b_f=fox_b_f, fox_w_out=fox_w_out, mla_w_in=mla_w_in,
             mla_q_norm=mla_q_norm, mla_kv_norm=mla_kv_norm, mla_w_uq=mla_w_uq, mla_w_ukv=mla_w_ukv,
             mla_w_out=mla_w_out, ret_w_in=ret_w_in, ret_gn_g=ret_gn_g, ret_w_out=ret_w_out, ln_g=ln_g, ln_b=ln_b)
    m = dict(meta=m_meta, fox_w_in=m_fox_w_in, fox_b_f=m_fox_b_f, fox_w_out=m_fox_w_out, mla_w_in=m_mla_w_in,
             mla_q_norm=m_mla_q_norm, mla_kv_norm=m_mla_kv_norm, mla_w_uq=m_mla_w_uq, mla_w_ukv=m_mla_w_ukv,
             mla_w_out=m_mla_w_out, ret_w_in=m_ret_w_in, ret_gn_g=m_ret_gn_g, ret_w_out=m_ret_w_out, ln_g=m_ln_g, ln_b=m_ln_b)
    v = dict(meta=v_meta, fox_w_in=v_fox_w_in, fox_b_f=v_fox_b_f, fox_w_out=v_fox_w_out, mla_w_in=v_mla_w_in,
             mla_q_norm=v_mla_q_norm, mla_kv_norm=v_mla_kv_norm, mla_w_uq=v_mla_w_uq, mla_w_ukv=v_mla_w_ukv,
             mla_w_out=v_mla_w_out, ret_w_in=v_ret_w_in, ret_gn_g=v_ret_gn_g, ret_w_out=v_ret_w_out, ln_g=v_ln_g, ln_b=v_ln_b)

    none = jnp.zeros((0,), F32)
    gathered = _exchange([w[n].astype(jnp.bfloat16) for n in _BIG] + [_pack_small(w, _SMALL_GATHERED, none)],
                         False, "weights_all_gather")
    full = {n: _merge(n, g).astype(_MXU) for n, g in zip(_BIG, gathered)}
    for d in range(N_DEV):
        part, _ = _unpack_small(gathered[-1][d], _SMALL_GATHERED)
        for n in _SMALL_GATHERED:
            full.setdefault(n, []).append(part[n])
    for n in _SMALL_GATHERED:
        full[n] = _merge(n, jnp.stack(full[n]))

    loss, grad_x, grads = _local_step(
        x[0], loss_target[0], full["meta"],
        (full["fox_w_in"], fox_b_f, full["fox_w_out"]),
        (full["mla_w_in"], mla_q_norm, mla_kv_norm, full["mla_w_uq"], full["mla_w_ukv"], full["mla_w_out"]),
        (full["ret_w_in"], full["ret_gn_g"], full["ret_w_out"]), ln_g, ln_b)

    def small_rows(n):
        if _SPECS[n][1] is None:
            return jnp.broadcast_to(grads[n].reshape(1, -1), (N_DEV, grads[n].size))
        return _split(n, grads[n]).reshape(N_DEV, -1)

    small_out = _pad_rows(jnp.concatenate([small_rows(n) for n in _SMALL]
                                          + [jnp.broadcast_to(loss.reshape(1, 1), (N_DEV, 1))], axis=1))
    parts = _exchange([_split(n, grads[n]) for n in _BIG] + [small_out], True, "grads_all_to_all")

    def rows2d(a):
        return a.reshape(-1, a.shape[-1])

    out = {}
    for n, p in zip(_BIG, parts):
        res = _adamw(p.reshape(N_DEV, -1, p.shape[-1]), rows2d(w[n]), rows2d(m[n]), rows2d(v[n]), f"adamw_{n}")
        out[n] = [r.reshape(w[n].shape) for r in res]
    zero1 = jnp.zeros((1,), F32)
    res = _adamw(parts[-1], _pack_small(w, _SMALL, zero1), _pack_small(m, _SMALL, zero1),
                 _pack_small(v, _SMALL, zero1), "adamw_small")
    small = [_unpack_small(r, _SMALL) for r in res]
    for n in _SMALL:
        out[n] = [s[0][n] for s in small]
    total_loss = small[0][1][0]
    return (total_loss, grad_x[None], *[out[n][0] for n in _NAMES], *[out[n][1] for n in _NAMES],
            *[out[n][2] for n in _NAMES], *[out[n][3] for n in _NAMES])
```

```python
import functools
import math

import numpy as np
import jax
import jax.numpy as jnp
from jax import lax
from jax.experimental import pallas as pl
from jax.experimental.pallas import tpu as pltpu

F32 = jnp.float32
_MXU = jnp.bfloat16

N_DEV = 8
N_META = 16
D_MODEL = 1024
ROW_TILE = 512
LANES = 128
CHUNK = 128
NEG = -1e30
VMEM_LIMIT = 60 * 1024 * 1024

FOX_HEADS = 8
MLA_HEADS = 8
MLA_NOPE, MLA_ROPE, MLA_V = 128, 64, 128
MLA_Q_LORA, MLA_KV_LORA = 384, 256
MLA_DK = 256
RET_HEADS, RET_QK, RET_V = 4, 256, 512
ROPE_BASE = 10000.0
ALPHA = (2 * 4) ** 0.25
NORM_EPS = 1e-5

ADAM_LR, ADAM_B1, ADAM_B2, ADAM_EPS, ADAM_WD, ADAM_STEP = 0.001, 0.9, 0.999, 1e-08, 0.01, 10


def _params(sem, vmem=VMEM_LIMIT):
    return pltpu.CompilerParams(dimension_semantics=sem, vmem_limit_bytes=vmem)


def _tile(n, pref):
    if n <= pref:
        return n
    t = (pref // LANES) * LANES
    while n % t:
        t -= LANES
    return t


_MM_TILES = {"nn": (512, 2048, 2048), "nt": (512, 1024, 4224), "tn": (1024, 2048, 768)}


def _matmul(a, b, mode, out_dtype, name):
    if mode == "nn":
        (M, K), (K2, N) = a.shape, b.shape
    elif mode == "nt":
        (M, K), (N, K2) = a.shape, b.shape
    else:
        (K, M), (K2, N) = a.shape, b.shape
    assert K == K2, (a.shape, b.shape, mode)
    tm, tn, tk = (_tile(d, p) for d, p in zip((M, N, K), _MM_TILES[mode]))
    nk = K // tk
    if mode == "nn":
        a_spec = pl.BlockSpec((tm, tk), lambda i, j, k: (i, k))
        b_spec = pl.BlockSpec((tk, tn), lambda i, j, k: (k, j))
        dn = (((1,), (0,)), ((), ()))
    elif mode == "nt":
        a_spec = pl.BlockSpec((tm, tk), lambda i, j, k: (i, k))
        b_spec = pl.BlockSpec((tn, tk), lambda i, j, k: (j, k))
        dn = (((1,), (1,)), ((), ()))
    else:
        a_spec = pl.BlockSpec((tk, tm), lambda i, j, k: (k, i))
        b_spec = pl.BlockSpec((tk, tn), lambda i, j, k: (k, j))
        dn = (((0,), (0,)), ((), ()))

    def body(a_ref, b_ref, o_ref, acc_ref):
        k = pl.program_id(2)
        part = lax.dot_general(a_ref[...].astype(_MXU), b_ref[...].astype(_MXU), dn,
                               preferred_element_type=F32)

        @pl.when(k == 0)
        def _():
            acc_ref[...] = part

        @pl.when(k > 0)
        def _():
            acc_ref[...] += part

        @pl.when(k == nk - 1)
        def _():
            o_ref[...] = acc_ref[...].astype(out_dtype)

    return pl.pallas_call(
        body, name=name,
        out_shape=jax.ShapeDtypeStruct((M, N), out_dtype),
        grid=(M // tm, N // tn, nk),
        in_specs=[a_spec, b_spec],
        out_specs=pl.BlockSpec((tm, tn), lambda i, j, k: (i, j)),
        scratch_shapes=[pltpu.VMEM((tm, tn), F32)],
        compiler_params=_params(("parallel", "parallel", "arbitrary")),
    )(a, b)


def _rows(w, off=0):
    return pl.BlockSpec((ROW_TILE, w), lambda i: (i + off, 0))


def _whole(shape):
    return pl.BlockSpec(shape, lambda i: (0,) * len(shape))


def _silu(z):
    return z * jax.nn.sigmoid(z)


def _dsilu(z):
    s = jax.nn.sigmoid(z)
    return s * (1.0 + z * (1.0 - s))


def _ln_fwd(h, y, g, b, name):
    L, D = h.shape

    def body(h_ref, y_ref, g_ref, b_ref, z_ref, o_ref):
        z = ALPHA * h_ref[...] + y_ref[...]
        mu = jnp.mean(z, axis=-1, keepdims=True)
        zc = z - mu
        var = jnp.mean(zc * zc, axis=-1, keepdims=True)
        z_ref[...] = z
        o_ref[...] = zc * lax.rsqrt(var + NORM_EPS) * g_ref[...] + b_ref[...]

    return pl.pallas_call(
        body, name=name,
        out_shape=(jax.ShapeDtypeStruct((L, D), F32), jax.ShapeDtypeStruct((L, D), F32)),
        grid=(L // ROW_TILE,),
        in_specs=[_rows(D), _rows(D), _whole((1, D)), _whole((1, D))],
        out_specs=(_rows(D), _rows(D)),
        compiler_params=_params(("parallel",)),
    )(h, y, g, b)


def _ln_bwd(zl, g, ga, gb, name):
    L, D = zl.shape
    two = gb is not None

    def body(*refs):
        if two:
            z_ref, g_ref, ga_ref, gb_ref, dz_ref, dg_ref, db_ref = refs
            gout = ALPHA * ga_ref[...] + gb_ref[...]
        else:
            z_ref, g_ref, ga_ref, dz_ref, dg_ref, db_ref = refs
            gout = ga_ref[...]
        z = z_ref[...]
        mu = jnp.mean(z, axis=-1, keepdims=True)
        zc = z - mu
        var = jnp.mean(zc * zc, axis=-1, keepdims=True)
        rstd = lax.rsqrt(var + NORM_EPS)
        xhat = zc * rstd
        dxh = gout * g_ref[...]
        m1 = jnp.mean(dxh, axis=-1, keepdims=True)
        m2 = jnp.mean(dxh * xhat, axis=-1, keepdims=True)
        dz_ref[...] = rstd * (dxh - m1 - xhat * m2)

        @pl.when(pl.program_id(0) == 0)
        def _():
            dg_ref[...] = jnp.zeros_like(dg_ref)
            db_ref[...] = jnp.zeros_like(db_ref)

        dg_ref[...] += jnp.sum(gout * xhat, axis=0, keepdims=True)
        db_ref[...] += jnp.sum(gout, axis=0, keepdims=True)

    ins = [zl, g, ga] + ([gb] if two else [])
    return pl.pallas_call(
        body, name=name,
        out_shape=(jax.ShapeDtypeStruct((L, D), F32), jax.ShapeDtypeStruct((1, D), F32),
                   jax.ShapeDtypeStruct((1, D), F32)),
        grid=(L // ROW_TILE,),
        in_specs=[_rows(D), _whole((1, D)), _rows(D)] + ([_rows(D)] if two else []),
        out_specs=(_rows(D), _whole((1, D)), _whole((1, D))),
        compiler_params=_params(("arbitrary",)),
    )(*ins)


def _loss_head(h, target, first):
    L, D = h.shape

    def body(h_ref, t_ref, dh_ref, loss_ref):
        i = pl.program_id(0)

        @pl.when(i == 0)
        def _():
            loss_ref[...] = jnp.zeros_like(loss_ref)

        @pl.when(i < first)
        def _():
            dh_ref[...] = jnp.zeros_like(dh_ref)

        @pl.when(i >= first)
        def _():
            err = h_ref[...] - t_ref[...]
            dh_ref[...] = err * (1.0 / D)
            part = jnp.sum(jnp.sum(err * err, axis=-1, keepdims=True) * (1.0 / D), axis=0, keepdims=True)
            loss_ref[...] += 0.5 * part

    return pl.pallas_call(
        body, name="loss_head",
        out_shape=(jax.ShapeDtypeStruct((L, D), F32), jax.ShapeDtypeStruct((1, 1), F32)),
        grid=(L // ROW_TILE,),
        in_specs=[_rows(D), pl.BlockSpec((ROW_TILE, D), lambda i: (jnp.maximum(i - first, 0), 0))],
        out_specs=(_rows(D), _whole((1, 1))),
        compiler_params=_params(("arbitrary",)),
    )(h, target)


def _input_grads(dzl, dmix, first, n_real):
    L, D = dzl.shape

    def body(a_ref, b_ref, gx_ref, gm_ref):
        i = pl.program_id(0)
        val = ALPHA * a_ref[...] + b_ref[...]

        @pl.when(i == first - 1)
        def _():
            gm_ref[...] = val[ROW_TILE - N_META:, :]

        @pl.when(i >= first)
        def _():
            gx_ref[...] = val

    return pl.pallas_call(
        body, name="input_grads",
        out_shape=(jax.ShapeDtypeStruct((n_real, D), F32), jax.ShapeDtypeStruct((N_META, D), F32)),
        grid=(L // ROW_TILE,),
        in_specs=[_rows(D), _rows(D)],
        out_specs=(pl.BlockSpec((ROW_TILE, D), lambda i: (jnp.maximum(i - first, 0), 0)),
                   _whole((N_META, D))),
        compiler_params=_params(("arbitrary",)),
    )(dzl, dmix)


def _gate_fwd(o, zsrc, name):
    L, W = o.shape

    def body(o_ref, z_ref, y_ref):
        y_ref[...] = (o_ref[...] * _silu(z_ref[...])).astype(_MXU)

    return pl.pallas_call(
        body, name=name,
        out_shape=jax.ShapeDtypeStruct((L, W), _MXU),
        grid=(L // ROW_TILE,),
        in_specs=[_rows(W), _rows(W)],
        out_specs=_rows(W),
        compiler_params=_params(("parallel",)),
    )(o, zsrc)


def _gate_bwd(dy, o, zsrc, heads, name):
    L, W = o.shape
    hd = W // heads

    def body(dy_ref, o_ref, z_ref, do_ref, dl_ref, dz_ref):
        z = z_ref[...]
        dy = dy_ref[...]
        o = o_ref[...]
        do = dy * _silu(z)
        do_ref[...] = do.astype(_MXU)
        dz_ref[...] = (dy * o * _dsilu(z)).astype(_MXU)
        prod = do * o
        for h in range(heads):
            dl_ref[h] = _as_row(jnp.sum(prod[:, h * hd:(h + 1) * hd], axis=-1, keepdims=True))

    return pl.pallas_call(
        body, name=name,
        out_shape=(jax.ShapeDtypeStruct((L, W), _MXU), jax.ShapeDtypeStruct((heads, L // ROW_TILE, 1, ROW_TILE), F32),
                   jax.ShapeDtypeStruct((L, W), _MXU)),
        grid=(L // ROW_TILE,),
        in_specs=[_rows(W), _rows(W), _rows(W)],
        out_specs=(_rows(W), pl.BlockSpec((heads, None, 1, ROW_TILE), lambda i: (0, i, 0, 0)), _rows(W)),
        compiler_params=_params(("parallel",)),
    )(dy, o, zsrc)


def _split3(x):
    hi = x.astype(_MXU)
    r1 = x - hi.astype(F32)
    mid = r1.astype(_MXU)
    lo = (r1 - mid.astype(F32)).astype(_MXU)
    return hi, mid, lo


def _tri_cumsum(x, reverse):
    T = x.shape[0]
    r = lax.broadcasted_iota(jnp.int32, (T, T), 0)
    c = lax.broadcasted_iota(jnp.int32, (T, T), 1)
    tri = jnp.where((c >= r) if reverse else (c <= r), 1.0, 0.0).astype(_MXU)
    out = jnp.zeros(x.shape, F32)
    for part in _split3(x):
        out = out + jnp.dot(tri, part, preferred_element_type=F32)
    return out


def _fox_cum(zf, bias, pad, name):
    L, Wz = zf.shape
    off = Wz // LANES - 1

    def body(f_ref, b_ref, c_ref, carry_ref):
        i = pl.program_id(0)

        @pl.when(i == 0)
        def _():
            carry_ref[...] = jnp.zeros_like(carry_ref)

        x = f_ref[...] + b_ref[...]
        logf = jnp.minimum(x, 0.0) - jnp.log(1.0 + jnp.exp(-jnp.abs(x)))
        row = i * ROW_TILE + lax.broadcasted_iota(jnp.int32, logf.shape, 0)
        logf = jnp.where(row >= pad, logf, 0.0)
        cum = _tri_cumsum(logf, False) + carry_ref[...]
        c_ref[...] = cum
        carry_ref[...] = cum[ROW_TILE - 1:, :]

    return pl.pallas_call(
        body, name=name,
        out_shape=jax.ShapeDtypeStruct((L, LANES), F32),
        grid=(L // ROW_TILE,),
        in_specs=[pl.BlockSpec((ROW_TILE, LANES), lambda i: (i, off)), _whole((1, LANES))],
        out_specs=_rows(LANES),
        scratch_shapes=[pltpu.VMEM((1, LANES), F32)],
        compiler_params=_params(("arbitrary",)),
    )(zf, bias)


def _fox_cum_bwd(dcum, zf, bias, pad, name):
    L, Wz = zf.shape
    off = Wz // LANES - 1
    n = L // ROW_TILE

    def body(d_ref, f_ref, b_ref, df_ref, db_ref, carry_ref):
        i = pl.program_id(0)

        @pl.when(i == 0)
        def _():
            carry_ref[...] = jnp.zeros_like(carry_ref)
            db_ref[...] = jnp.zeros_like(db_ref)

        rc = _tri_cumsum(d_ref[...], True) + carry_ref[...]
        carry_ref[...] = rc[:1, :]
        x = f_ref[...] + b_ref[...]
        row = (n - 1 - i) * ROW_TILE + lax.broadcasted_iota(jnp.int32, x.shape, 0)
        df = jnp.where(row >= pad, rc * jax.nn.sigmoid(-x), 0.0)
        df_ref[...] = df.astype(_MXU)
        db_ref[...] += jnp.sum(df, axis=0, keepdims=True)

    return pl.pallas_call(
        body, name=name,
        out_shape=(jax.ShapeDtypeStruct((L, LANES), _MXU), jax.ShapeDtypeStruct((1, LANES), F32)),
        grid=(n,),
        in_specs=[pl.BlockSpec((ROW_TILE, LANES), lambda i: (n - 1 - i, 0)),
                  pl.BlockSpec((ROW_TILE, LANES), lambda i: (n - 1 - i, off)), _whole((1, LANES))],
        out_specs=(pl.BlockSpec((ROW_TILE, LANES), lambda i: (n - 1 - i, 0)), _whole((1, LANES))),
        scratch_shapes=[pltpu.VMEM((1, LANES), F32)],
        compiler_params=_params(("arbitrary",)),
    )(dcum, zf, bias)


LOG2E = 1.4426950408889634
LN2 = 0.6931471805599453


def _bias_terms(ct, pad):
    H, L = ct.shape
    T = ROW_TILE
    n = L // T
    c2 = ct * LOG2E
    k2 = jnp.where(jnp.arange(L)[None, :] < pad, -NEG, c2)
    return dict(cq_row=c2.reshape(H, n, 1, T), ck_row=k2.reshape(H, n, 1, T), ref=c2[:, ::T].reshape(H, n, 1, 1))


def _exchange_copies(srcs, outs, send_sems, recv_sems, local_sems, scatter):
    nt = len(srcs)
    x, y, c = lax.axis_index("x"), lax.axis_index("y"), lax.axis_index("c")
    me = 4 * x + 2 * y + c
    copies = [pltpu.make_async_copy(srcs[t].at[me] if scatter else srcs[t], outs[t].at[me], local_sems.at[t])
              for t in range(nt)]
    for k in range(1, N_DEV):
        px = (x + (k >> 2)) % 2
        py = (y + ((k >> 1) & 1)) % 2
        pc = (c + (k & 1)) % 2
        peer = 4 * px + 2 * py + pc
        for t in range(nt):
            copies.append(pltpu.make_async_remote_copy(
                src_ref=srcs[t].at[peer] if scatter else srcs[t],
                dst_ref=outs[t].at[me],
                send_sem=send_sems.at[k - 1, t], recv_sem=recv_sems.at[k - 1, t],
                device_id=(px, py, pc), device_id_type=pl.DeviceIdType.MESH))
    return copies


def _exchange_shapes(tensors, scatter):
    out_shape = tuple(jax.ShapeDtypeStruct(t.shape if scatter else (N_DEV,) + t.shape, t.dtype) for t in tensors)
    nt = len(tensors)
    sems = [pltpu.SemaphoreType.DMA((N_DEV - 1, nt)), pltpu.SemaphoreType.DMA((N_DEV - 1, nt)),
            pltpu.SemaphoreType.DMA((nt,))]
    return out_shape, sems


def _side_exchange(side, srcs, outs, sems, first, last):
    if not side:
        return

    @pl.when(first)
    def _():
        for cp in _exchange_copies(srcs, outs, *sems, side[1]):
            cp.start()

    @pl.when(last)
    def _():
        for cp in _exchange_copies(srcs, outs, *sems, side[1]):
            cp.wait()


def _as_row(col):
    return jnp.transpose(jnp.broadcast_to(col, (col.shape[0], LANES)))[0:1, :]


def _as_col(row):
    return jnp.transpose(jnp.broadcast_to(row, (LANES, row.shape[1])))[:, 0:1]


def _attn_fwd(q, qoff, k, koff, v, voff, bias, heads, dk, dv, scale, name, side=None):
    L = q.shape[0]
    T = ROW_TILE
    n = L // T
    c = scale * LOG2E
    dn_qk = (((1,), (1,)), ((), ()))

    ns = len(side[0]) if side else 0

    def body(*refs):
        q_ref, k_ref, v_ref, ck_ref, ref_ref = refs[:5]
        o_ref, lse_ref = refs[5 + ns:7 + ns]
        s_a, s_b = refs[7 + 2 * ns:9 + 2 * ns]
        i = pl.program_id(1)
        _side_exchange(side, refs[5:5 + ns], refs[7 + ns:7 + 2 * ns], refs[9 + 2 * ns:],
                       (pl.program_id(0) == 0) & (i == 0), (pl.program_id(0) == heads - 1) & (i == n - 1))
        qb = q_ref[...]
        ref = ref_ref[...]

        def scores(j, dst):
            start = pl.multiple_of(j * T, T)
            dst[...] = lax.dot_general(qb, k_ref[pl.ds(start, T), :], dn_qk, preferred_element_type=F32)

        def soft(j, carry, cur, diag):
            m, l, acc = carry
            start = pl.multiple_of(j * T, T)
            vj = v_ref[pl.ds(start, T), :]
            t = cur[...] * c - (ck_ref[j] - ref)
            if diag:
                r = lax.broadcasted_iota(jnp.int32, (T, T), 0)
                cc = lax.broadcasted_iota(jnp.int32, (T, T), 1)
                t = jnp.where(cc <= r, t, NEG)
            m_new = jnp.maximum(m, jnp.max(t, axis=1, keepdims=True))
            p = jnp.exp2(t - m_new)
            a = jnp.exp2(m - m_new)
            l = a * l + jnp.sum(p, axis=1, keepdims=True)
            acc = a * acc + jnp.dot(p.astype(_MXU), vj, preferred_element_type=F32)
            return m_new, l, acc

        def step(j, carry, cur, nxt):
            scores(j + 1, nxt)
            return soft(j, carry, cur, False)

        def pair(jj, carry):
            return step(2 * jj + 1, step(2 * jj, carry, s_a, s_b), s_b, s_a)

        scores(0, s_a)
        init = (jnp.full((T, 1), -jnp.inf, F32), jnp.zeros((T, 1), F32), jnp.zeros((T, dv), F32))
        carry = lax.fori_loop(0, i // 2, pair, init)
        odd = (i % 2) == 1
        carry = lax.cond(odd, lambda cy: step(i - 1, cy, s_a, s_b), lambda cy: cy, carry)
        m, l, acc = lax.cond(odd, lambda cy: soft(i, cy, s_b, True), lambda cy: soft(i, cy, s_a, True), carry)
        o_ref[...] = acc / l
        lse_ref[...] = _as_row(m + jnp.log2(l))

    hbm = pl.BlockSpec(memory_space=pl.ANY)
    side_shapes, side_sems = _exchange_shapes(*side) if side else ((), [])
    o, lse, *side_out = pl.pallas_call(
        body, name=name,
        out_shape=(jax.ShapeDtypeStruct((L, heads * dv), F32), jax.ShapeDtypeStruct((heads, n, 1, T), F32))
        + side_shapes,
        grid=(heads, n),
        in_specs=[pl.BlockSpec((T, dk), lambda h, i: (i, qoff + h)),
                  pl.BlockSpec((L, dk), lambda h, i: (0, koff + h)),
                  pl.BlockSpec((L, dv), lambda h, i: (0, voff + h)),
                  pl.BlockSpec((None, n, 1, T), lambda h, i: (h, 0, 0, 0)),
                  pl.BlockSpec((None, None, 1, 1), lambda h, i: (h, i, 0, 0))] + [hbm] * ns,
        out_specs=(pl.BlockSpec((T, dv), lambda h, i: (i, h)),
                   pl.BlockSpec((None, None, 1, T), lambda h, i: (h, i, 0, 0))) + (hbm,) * ns,
        scratch_shapes=[pltpu.VMEM((T, T), F32), pltpu.VMEM((T, T), F32)] + side_sems,
        compiler_params=_params(("arbitrary", "arbitrary")),
    )(q, k, v, bias["ck_row"], bias["ref"], *(side[0] if side else ()))
    return o, lse + (bias["cq_row"] - bias["ref"]), side_out


def _attn_bwd(q, qoff, k, koff, v, voff, do, bias, lse_row, delta_row, heads, dk, dv, scale, name, side=None):
    L = q.shape[0]
    T = ROW_TILE
    n = L // T
    c = scale * LOG2E
    dn_nt = (((1,), (1,)), ((), ()))
    dn_tn = (((0,), (0,)), ((), ()))
    ns = len(side[0]) if side else 0

    def body(*refs):
        q_ref, k_ref, v_ref, do_ref, cq_ref, ck_ref, ref_ref, lse_ref, dl_ref = refs[:9]
        dq_hbm, dk_ref, dv_ref, dck_ref, dcq_ref = refs[9 + ns:14 + ns]
        dq_acc, sem, st_a, dp_a, st_b, dp_b = refs[14 + 2 * ns:20 + 2 * ns]
        h = pl.program_id(0)
        j = pl.program_id(1)
        _side_exchange(side, refs[9:9 + ns], refs[14 + ns:14 + 2 * ns], refs[20 + 2 * ns:],
                       (h == 0) & (j == 0), (h == heads - 1) & (j == n - 1))

        @pl.when(j == 0)
        def _():
            dq_acc[...] = jnp.zeros_like(dq_acc)
            dcq_ref[...] = jnp.zeros_like(dcq_ref)

        kb = k_ref[...]
        vb = v_ref[...]
        ref = ref_ref[...]
        bcol = _as_col(ck_ref[j] - ref)

        def front(i, st_dst, dp_dst):
            start = pl.multiple_of(jnp.minimum(i, n - 1) * T, T)
            st_dst[...] = lax.dot_general(kb, q_ref[pl.ds(start, T), :], dn_nt, preferred_element_type=F32)
            dp_dst[...] = lax.dot_general(vb, do_ref[pl.ds(start, T), :], dn_nt, preferred_element_type=F32)

        def back(i, carry, st_cur, dp_cur, diag):
            dk_a, dv_a, dck_a = carry
            start = pl.multiple_of(i * T, T)
            qi = q_ref[pl.ds(start, T), :]
            doi = do_ref[pl.ds(start, T), :]
            arow = (cq_ref[i] - ref) - lse_ref[i]
            st = st_cur[...] * c + arow - bcol
            if diag:
                r = lax.broadcasted_iota(jnp.int32, (T, T), 0)
                cc = lax.broadcasted_iota(jnp.int32, (T, T), 1)
                st = jnp.where(r <= cc, st, NEG)
            pt = jnp.exp2(st)
            dv_a = dv_a + jnp.dot(pt.astype(_MXU), doi, preferred_element_type=F32)
            dst = pt * (dp_cur[...] - dl_ref[i])
            dck_a = dck_a - jnp.sum(dst, axis=1, keepdims=True)
            dcq_ref[i] += jnp.sum(dst, axis=0, keepdims=True)
            dsb = (dst * scale).astype(_MXU)
            dk_a = dk_a + jnp.dot(dsb, qi, preferred_element_type=F32)
            dq_acc[pl.ds(start, T), :] += lax.dot_general(dsb, kb, dn_tn, preferred_element_type=F32)
            return dk_a, dv_a, dck_a

        buf_a, buf_b = (st_a, dp_a), (st_b, dp_b)

        def step(i, carry, cur, nxt, diag=False):
            front(i + 1, *nxt)
            return back(i, carry, *cur, diag)

        def pair(t, carry):
            i0 = j + 1 + 2 * t
            return step(i0 + 1, step(i0, carry, buf_b, buf_a), buf_a, buf_b)

        front(j, *buf_a)
        init = (jnp.zeros((T, dk), F32), jnp.zeros((T, dv), F32), jnp.zeros((T, 1), F32))
        carry = step(j, init, buf_a, buf_b, True)
        rest = n - 1 - j
        carry = lax.fori_loop(0, rest // 2, pair, carry)
        dk_a, dv_a, dck_a = lax.cond((rest % 2) == 1, lambda cy: step(n - 1, cy, buf_b, buf_a),
                                     lambda cy: cy, carry)
        dk_ref[...] = dk_a
        dv_ref[...] = dv_a
        dck_ref[...] = _as_row(dck_a)

        @pl.when(j == n - 1)
        def _():
            cp = pltpu.make_async_copy(dq_acc, dq_hbm.at[:, pl.ds(pl.multiple_of(h * dk, dk), dk)], sem)
            cp.start()
            cp.wait()

    hbm = pl.BlockSpec(memory_space=pl.ANY)
    side_shapes, side_sems = _exchange_shapes(*side) if side else ((), [])
    dq, dk_, dv_, dck, dcq, *side_out = pl.pallas_call(
        body, name=name,
        out_shape=(jax.ShapeDtypeStruct((L, heads * dk), F32), jax.ShapeDtypeStruct((L, heads * dk), F32),
                   jax.ShapeDtypeStruct((L, heads * dv), F32), jax.ShapeDtypeStruct((heads, n, 1, T), F32),
                   jax.ShapeDtypeStruct((heads, n, 1, T), F32)) + side_shapes,
        grid=(heads, n),
        in_specs=[pl.BlockSpec((L, dk), lambda h, j: (0, qoff + h)),
                  pl.BlockSpec((T, dk), lambda h, j: (j, koff + h)),
                  pl.BlockSpec((T, dv), lambda h, j: (j, voff + h)),
                  pl.BlockSpec((L, dv), lambda h, j: (0, h)),
                  pl.BlockSpec((None, n, 1, T), lambda h, j: (h, 0, 0, 0)),
                  pl.BlockSpec((None, n, 1, T), lambda h, j: (h, 0, 0, 0)),
                  pl.BlockSpec((None, None, 1, 1), lambda h, j: (h, j, 0, 0)),
                  pl.BlockSpec((None, n, 1, T), lambda h, j: (h, 0, 0, 0)),
                  pl.BlockSpec((None, n, 1, T), lambda h, j: (h, 0, 0, 0))] + [hbm] * ns,
        out_specs=(hbm,
                   pl.BlockSpec((T, dk), lambda h, j: (j, h)),
                   pl.BlockSpec((T, dv), lambda h, j: (j, h)),
                   pl.BlockSpec((None, None, 1, T), lambda h, j: (h, j, 0, 0)),
                   pl.BlockSpec((None, n, 1, T), lambda h, j: (h, 0, 0, 0))) + (hbm,) * ns,
        scratch_shapes=[pltpu.VMEM((L, dk), F32), pltpu.SemaphoreType.DMA] + [pltpu.VMEM((T, T), F32)] * 4
        + side_sems,
        compiler_params=_params(("arbitrary", "arbitrary")),
    )(q, k, v, do, bias["cq_row"], bias["ck_row"], bias["ref"], lse_row, delta_row, *(side[0] if side else ()))
    return dq, dk_, dv_, dck, dcq, side_out


_CQ0, _CKV0, _KR0 = D_MODEL, D_MODEL + MLA_Q_LORA, D_MODEL + MLA_Q_LORA + MLA_KV_LORA
_MLA_PROJ = _KR0 + LANES


def _rms(x, g):
    ms = jnp.mean(x * x, axis=-1, keepdims=True)
    return x * lax.rsqrt(ms + NORM_EPS) * g


def _rms_bwd(x, g, dy):
    ms = jnp.mean(x * x, axis=-1, keepdims=True)
    r = lax.rsqrt(ms + NORM_EPS)
    xh = x * r
    dxh = dy * g
    dx = r * (dxh - xh * jnp.mean(dxh * xh, axis=-1, keepdims=True))
    return dx, jnp.sum(dy * xh, axis=0, keepdims=True)


def _mla_norm_fwd(proj, gq, gkv):
    L = proj.shape[0]

    def body(p_ref, gq_ref, gkv_ref, cq_ref, ckv_ref):
        cq_ref[...] = _rms(p_ref[:, _CQ0:_CKV0], gq_ref[...]).astype(_MXU)
        ckv_ref[...] = _rms(p_ref[:, _CKV0:_KR0], gkv_ref[...]).astype(_MXU)

    return pl.pallas_call(
        body, name="mla_norm_fwd",
        out_shape=(jax.ShapeDtypeStruct((L, MLA_Q_LORA), _MXU), jax.ShapeDtypeStruct((L, MLA_KV_LORA), _MXU)),
        grid=(L // ROW_TILE,),
        in_specs=[_rows(_MLA_PROJ), _whole((1, MLA_Q_LORA)), _whole((1, MLA_KV_LORA))],
        out_specs=(_rows(MLA_Q_LORA), _rows(MLA_KV_LORA)),
        compiler_params=_params(("parallel",)),
    )(proj, gq, gkv)


def _mla_norm_bwd(proj, gq, gkv, dcqn, dckvn):
    L = proj.shape[0]

    def body(p_ref, gq_ref, gkv_ref, dq_ref, dkv_ref, dcq_ref, dckv_ref, dgq_ref, dgkv_ref):
        @pl.when(pl.program_id(0) == 0)
        def _():
            dgq_ref[...] = jnp.zeros_like(dgq_ref)
            dgkv_ref[...] = jnp.zeros_like(dgkv_ref)

        dx, dg = _rms_bwd(p_ref[:, _CQ0:_CKV0], gq_ref[...], dq_ref[...])
        dcq_ref[...] = dx.astype(_MXU)
        dgq_ref[...] += dg
        dx, dg = _rms_bwd(p_ref[:, _CKV0:_KR0], gkv_ref[...], dkv_ref[...])
        dckv_ref[...] = dx.astype(_MXU)
        dgkv_ref[...] += dg

    return pl.pallas_call(
        body, name="mla_norm_bwd",
        out_shape=(jax.ShapeDtypeStruct((L, MLA_Q_LORA), _MXU), jax.ShapeDtypeStruct((L, MLA_KV_LORA), _MXU),
                   jax.ShapeDtypeStruct((1, MLA_Q_LORA), F32), jax.ShapeDtypeStruct((1, MLA_KV_LORA), F32)),
        grid=(L // ROW_TILE,),
        in_specs=[_rows(_MLA_PROJ), _whole((1, MLA_Q_LORA)), _whole((1, MLA_KV_LORA)),
                  _rows(MLA_Q_LORA), _rows(MLA_KV_LORA)],
        out_specs=(_rows(MLA_Q_LORA), _rows(MLA_KV_LORA), _whole((1, MLA_Q_LORA)), _whole((1, MLA_KV_LORA))),
        compiler_params=_params(("arbitrary",)),
    )(proj, gq, gkv, dcqn, dckvn)


def _rot_tile(t, cos, sa, sb):
    half = MLA_ROPE // 2
    return t * cos + pltpu.roll(t, LANES - half, 1) * sa + pltpu.roll(t, half, 1) * sb


def _mla_rope_fwd(qf, kv, proj, cos, sa, sb):
    L = qf.shape[0]
    H = MLA_HEADS

    def body(q_ref, kv_ref, p_ref, c_ref, sa_ref, sb_ref, qo_ref, ko_ref):
        cos, sa, sb = c_ref[...], sa_ref[...], sb_ref[...]
        kr = _rot_tile(p_ref[...], cos, sa, sb).astype(_MXU)
        for h in range(H):
            b = h * MLA_DK
            qo_ref[:, b:b + LANES] = q_ref[:, b:b + LANES].astype(_MXU)
            qo_ref[:, b + LANES:b + 2 * LANES] = _rot_tile(q_ref[:, b + LANES:b + 2 * LANES], cos, sa, sb).astype(_MXU)
            ko_ref[:, b:b + LANES] = kv_ref[:, h * LANES:(h + 1) * LANES]
            ko_ref[:, b + LANES:b + 2 * LANES] = kr

    W = H * MLA_DK
    return pl.pallas_call(
        body, name="mla_rope_fwd",
        out_shape=(jax.ShapeDtypeStruct((L, W), _MXU), jax.ShapeDtypeStruct((L, W), _MXU)),
        grid=(L // ROW_TILE,),
        in_specs=[_rows(W), _rows(kv.shape[1]), pl.BlockSpec((ROW_TILE, LANES), lambda i: (i, _KR0 // LANES)),
                  _rows(LANES), _rows(LANES), _rows(LANES)],
        out_specs=(_rows(W), _rows(W)),
        compiler_params=_params(("parallel",)),
    )(qf, kv, proj, cos, sa, sb)


def _mla_rope_bwd(dq_full, dk_full, cos, sa, sb):
    L = dq_full.shape[0]
    H = MLA_HEADS
    W = H * MLA_DK

    def body(dq_ref, dk_ref, c_ref, sa_ref, sb_ref, dqf_ref, dkn_ref, dkr_ref):
        cos, sa, sb = c_ref[...], sa_ref[...], sb_ref[...]
        lane = lax.broadcasted_iota(jnp.int32, (ROW_TILE, LANES), 1)
        live = lane < MLA_ROPE
        krs = jnp.zeros((ROW_TILE, LANES), F32)
        for h in range(H):
            b = h * MLA_DK
            dqf_ref[:, b:b + LANES] = dq_ref[:, b:b + LANES].astype(_MXU)
            dqr = _rot_tile(dq_ref[:, b + LANES:b + 2 * LANES], cos, sa, sb)
            dqf_ref[:, b + LANES:b + 2 * LANES] = jnp.where(live, dqr, 0.0).astype(_MXU)
            dkn_ref[:, h * LANES:(h + 1) * LANES] = dk_ref[:, b:b + LANES].astype(_MXU)
            krs = krs + dk_ref[:, b + LANES:b + 2 * LANES]
        dkr_ref[...] = jnp.where(live, _rot_tile(krs, cos, sa, sb), 0.0).astype(_MXU)

    return pl.pallas_call(
        body, name="mla_rope_bwd",
        out_shape=(jax.ShapeDtypeStruct((L, W), _MXU), jax.ShapeDtypeStruct((L, H * LANES), _MXU),
                   jax.ShapeDtypeStruct((L, LANES), _MXU)),
        grid=(L // ROW_TILE,),
        in_specs=[_rows(W), _rows(W), _rows(LANES), _rows(LANES), _rows(LANES)],
        out_specs=(_rows(W), _rows(H * LANES), _rows(LANES)),
        compiler_params=_params(("parallel",)),
    )(dq_full, dk_full, cos, sa, sb)


_RET_QKW = RET_HEADS * RET_QK


def _ret_rope_fwd(qkz, cos, sin, pad):
    L = qkz.shape[0]
    hh = RET_QK // 2
    kscale = RET_QK ** -0.5

    def body(p_ref, c_ref, s_ref, q_ref, k_ref):
        cos, sin = c_ref[...], s_ref[...]
        row = pl.program_id(0) * ROW_TILE + lax.broadcasted_iota(jnp.int32, (ROW_TILE, hh), 0)
        keep = row >= pad
        for h in range(RET_HEADS):
            b = h * RET_QK
            t1, t2 = p_ref[:, b:b + hh], p_ref[:, b + hh:b + 2 * hh]
            q_ref[:, b:b + hh] = (t1 * cos - t2 * sin).astype(_MXU)
            q_ref[:, b + hh:b + 2 * hh] = (t2 * cos + t1 * sin).astype(_MXU)
            t1, t2 = p_ref[:, _RET_QKW + b:_RET_QKW + b + hh], p_ref[:, _RET_QKW + b + hh:_RET_QKW + b + 2 * hh]
            k_ref[:, b:b + hh] = jnp.where(keep, (t1 * cos - t2 * sin) * kscale, 0.0)
            k_ref[:, b + hh:b + 2 * hh] = jnp.where(keep, (t2 * cos + t1 * sin) * kscale, 0.0)

    return pl.pallas_call(
        body, name="ret_rope_fwd",
        out_shape=(jax.ShapeDtypeStruct((L, _RET_QKW), _MXU), jax.ShapeDtypeStruct((L, _RET_QKW), F32)),
        grid=(L // ROW_TILE,),
        in_specs=[_rows(2 * _RET_QKW), _rows(hh), _rows(hh)],
        out_specs=(_rows(_RET_QKW), _rows(_RET_QKW)),
        compiler_params=_params(("parallel",)),
    )(qkz, cos, sin)


def _ret_rope_bwd(dqr, dkr, cos, sin, pad):
    L = dqr.shape[0]
    hh = RET_QK // 2
    kscale = RET_QK ** -0.5

    def body(dq_ref, dk_ref, c_ref, s_ref, oq_ref, ok_ref):
        cos, sin = c_ref[...], s_ref[...]
        row = pl.program_id(0) * ROW_TILE + lax.broadcasted_iota(jnp.int32, (ROW_TILE, hh), 0)
        keep = row >= pad
        for h in range(RET_HEADS):
            b = h * RET_QK
            d1, d2 = dq_ref[:, b:b + hh], dq_ref[:, b + hh:b + 2 * hh]
            oq_ref[:, b:b + hh] = (d1 * cos + d2 * sin).astype(_MXU)
            oq_ref[:, b + hh:b + 2 * hh] = (d2 * cos - d1 * sin).astype(_MXU)
            d1, d2 = dk_ref[:, b:b + hh], dk_ref[:, b + hh:b + 2 * hh]
            ok_ref[:, b:b + hh] = jnp.where(keep, (d1 * cos + d2 * sin) * kscale, 0.0).astype(_MXU)
            ok_ref[:, b + hh:b + 2 * hh] = jnp.where(keep, (d2 * cos - d1 * sin) * kscale, 0.0).astype(_MXU)

    return pl.pallas_call(
        body, name="ret_rope_bwd",
        out_shape=(jax.ShapeDtypeStruct((L, _RET_QKW), _MXU), jax.ShapeDtypeStruct((L, _RET_QKW), _MXU)),
        grid=(L // ROW_TILE,),
        in_specs=[_rows(_RET_QKW), _rows(_RET_QKW), _rows(hh), _rows(hh)],
        out_specs=(_rows(_RET_QKW), _rows(_RET_QKW)),
        compiler_params=_params(("parallel",)),
    )(dqr, dkr, cos, sin)


def _ret_decays():
    lg = np.log1p(-np.exp2(-5.0 - np.arange(RET_HEADS, dtype=np.float32))).astype(np.float32)
    i = np.arange(CHUNK, dtype=np.float32)
    rel = i[:, None] - i[None, :]
    dm = np.where(rel[None] >= 0, np.exp(rel[None] * lg[:, None, None]), 0.0).astype(np.float32)
    qd = np.exp((i[None, :] + 1.0) * lg[:, None]).astype(np.float32)[:, :, None]
    kd = np.exp((CHUNK - 1.0 - i)[None, :] * lg[:, None]).astype(np.float32)[:, :, None]
    cd = np.broadcast_to(np.exp(CHUNK * lg).astype(np.float32)[:, None, None], (RET_HEADS, 1, RET_V))
    return jnp.asarray(dm), jnp.asarray(qd), jnp.asarray(kd), jnp.asarray(np.ascontiguousarray(cd))


def _ret_fwd(qr, kr, v):
    L = qr.shape[0]
    nc = L // CHUNK
    G = ROW_TILE // CHUNK
    dm, qd, kd, cd = _ret_decays()
    dn_nt = (((1,), (1,)), ((), ()))
    dn_tn = (((0,), (0,)), ((), ()))

    def body(q_ref, k_ref, v_ref, dm_ref, qd_ref, kd_ref, cd_ref, o_ref, st_ref, s_ref):
        @pl.when(pl.program_id(1) == 0)
        def _():
            s_ref[...] = jnp.zeros_like(s_ref)

        state = s_ref[...]
        for g in range(G):
            rows = slice(g * CHUNK, (g + 1) * CHUNK)
            q, k, vv = q_ref[rows, :], k_ref[rows, :], v_ref[rows, :]
            sb = state.astype(_MXU)
            st_ref[g] = sb
            s = lax.dot_general(q, k.astype(_MXU), dn_nt, preferred_element_type=F32) * dm_ref[...]
            intra = jnp.dot(s.astype(_MXU), vv, preferred_element_type=F32)
            cross = jnp.dot(q, sb, preferred_element_type=F32) * qd_ref[...]
            o_ref[rows, :] = intra + cross
            kdm = (k * kd_ref[...]).astype(_MXU)
            state = cd_ref[...] * state + lax.dot_general(kdm, vv, dn_tn, preferred_element_type=F32)
        s_ref[...] = state

    return pl.pallas_call(
        body, name="ret_fwd",
        out_shape=(jax.ShapeDtypeStruct((L, RET_HEADS * RET_V), F32),
                   jax.ShapeDtypeStruct((RET_HEADS, nc, RET_QK, RET_V), _MXU)),
        grid=(RET_HEADS, nc // G),
        in_specs=[pl.BlockSpec((ROW_TILE, RET_QK), lambda h, c: (c, h)),
                  pl.BlockSpec((ROW_TILE, RET_QK), lambda h, c: (c, h)),
                  pl.BlockSpec((ROW_TILE, RET_V), lambda h, c: (c, h)),
                  pl.BlockSpec((None, CHUNK, CHUNK), lambda h, c: (h, 0, 0)),
                  pl.BlockSpec((None, CHUNK, 1), lambda h, c: (h, 0, 0)),
                  pl.BlockSpec((None, CHUNK, 1), lambda h, c: (h, 0, 0)),
                  pl.BlockSpec((None, 1, RET_V), lambda h, c: (h, 0, 0))],
        out_specs=(pl.BlockSpec((ROW_TILE, RET_V), lambda h, c: (c, h)),
                   pl.BlockSpec((None, G, RET_QK, RET_V), lambda h, c: (h, c, 0, 0))),
        scratch_shapes=[pltpu.VMEM((RET_QK, RET_V), F32)],
        compiler_params=_params(("parallel", "arbitrary")),
    )(qr, kr, v, dm, qd, kd, cd)


def _ret_bwd(qr, kr, v, do, states):
    L = qr.shape[0]
    nc = L // CHUNK
    G = ROW_TILE // CHUNK
    nt = nc // G
    dm, qd, kd, cd = _ret_decays()
    dn_nt = (((1,), (1,)), ((), ()))
    dn_tn = (((0,), (0,)), ((), ()))

    def body(q_ref, k_ref, v_ref, do_ref, st_ref, dm_ref, qd_ref, kd_ref, cd_ref,
             dq_ref, dk_ref, dv_ref, ds_ref):
        @pl.when(pl.program_id(1) == 0)
        def _():
            ds_ref[...] = jnp.zeros_like(ds_ref)

        dstate = ds_ref[...]
        dmat = dm_ref[...]
        for g in reversed(range(G)):
            rows = slice(g * CHUNK, (g + 1) * CHUNK)
            q, k, vv, do = q_ref[rows, :], k_ref[rows, :], v_ref[rows, :], do_ref[rows, :]
            kb = k.astype(_MXU)
            dob = do.astype(_MXU)
            doq = (do * qd_ref[...]).astype(_MXU)
            dsb = dstate.astype(_MXU)
            s = (lax.dot_general(q, kb, dn_nt, preferred_element_type=F32) * dmat).astype(_MXU)
            draw = (lax.dot_general(dob, vv, dn_nt, preferred_element_type=F32) * dmat).astype(_MXU)
            kdm = (k * kd_ref[...]).astype(_MXU)
            dv = (lax.dot_general(s, dob, dn_tn, preferred_element_type=F32)
                  + jnp.dot(kdm, dsb, preferred_element_type=F32))
            dq = (jnp.dot(draw, kb, preferred_element_type=F32)
                  + lax.dot_general(doq, st_ref[g], dn_nt, preferred_element_type=F32))
            dk = (lax.dot_general(draw, q, dn_tn, preferred_element_type=F32)
                  + lax.dot_general(vv, dsb, dn_nt, preferred_element_type=F32) * kd_ref[...])
            dq_ref[rows, :] = dq
            dk_ref[rows, :] = dk
            dv_ref[rows, :] = dv.astype(_MXU)
            dstate = cd_ref[...] * dstate + lax.dot_general(q, doq, dn_tn, preferred_element_type=F32)
        ds_ref[...] = dstate

    rev = lambda h, c: (nt - 1 - c, h)
    return pl.pallas_call(
        body, name="ret_bwd",
        out_shape=(jax.ShapeDtypeStruct((L, _RET_QKW), F32), jax.ShapeDtypeStruct((L, _RET_QKW), F32),
                   jax.ShapeDtypeStruct((L, RET_HEADS * RET_V), _MXU)),
        grid=(RET_HEADS, nt),
        in_specs=[pl.BlockSpec((ROW_TILE, RET_QK), rev),
                  pl.BlockSpec((ROW_TILE, RET_QK), rev),
                  pl.BlockSpec((ROW_TILE, RET_V), rev),
                  pl.BlockSpec((ROW_TILE, RET_V), rev),
                  pl.BlockSpec((None, G, RET_QK, RET_V), lambda h, c: (h, nt - 1 - c, 0, 0)),
                  pl.BlockSpec((None, CHUNK, CHUNK), lambda h, c: (h, 0, 0)),
                  pl.BlockSpec((None, CHUNK, 1), lambda h, c: (h, 0, 0)),
                  pl.BlockSpec((None, CHUNK, 1), lambda h, c: (h, 0, 0)),
                  pl.BlockSpec((None, 1, RET_V), lambda h, c: (h, 0, 0))],
        out_specs=(pl.BlockSpec((ROW_TILE, RET_QK), rev), pl.BlockSpec((ROW_TILE, RET_QK), rev),
                   pl.BlockSpec((ROW_TILE, RET_V), rev)),
        scratch_shapes=[pltpu.VMEM((RET_QK, RET_V), F32)],
        compiler_params=_params(("parallel", "arbitrary")),
    )(qr, kr, v, do, states, dm, qd, kd, cd)


def _gn(o, g):
    mu = jnp.mean(o, axis=-1, keepdims=True)
    oc = o - mu
    var = jnp.mean(oc * oc, axis=-1, keepdims=True)
    return oc * lax.rsqrt(var + NORM_EPS) * g


def _ret_gate_fwd(o, qkz, gn_g):
    L, W = o.shape
    zblk = qkz.shape[1] // W - 1

    def body(o_ref, z_ref, g_ref, y_ref):
        for h in range(RET_HEADS):
            sl = slice(h * RET_V, (h + 1) * RET_V)
            y_ref[:, sl] = (_gn(o_ref[:, sl], g_ref[:, sl]) * _silu(z_ref[:, sl])).astype(_MXU)

    return pl.pallas_call(
        body, name="ret_gate_fwd",
        out_shape=jax.ShapeDtypeStruct((L, W), _MXU),
        grid=(L // ROW_TILE,),
        in_specs=[_rows(W), pl.BlockSpec((ROW_TILE, W), lambda i: (i, zblk)), _whole((1, W))],
        out_specs=_rows(W),
        compiler_params=_params(("parallel",)),
    )(o, qkz, gn_g)


def _ret_gate_bwd(dy, o, qkz, gn_g):
    L, W = o.shape
    zblk = qkz.shape[1] // W - 1

    def body(dy_ref, o_ref, z_ref, g_ref, do_ref, dz_ref, dg_ref):
        @pl.when(pl.program_id(0) == 0)
        def _():
            dg_ref[...] = jnp.zeros_like(dg_ref)

        for h in range(RET_HEADS):
            sl = slice(h * RET_V, (h + 1) * RET_V)
            z = z_ref[:, sl]
            o = o_ref[:, sl]
            g = g_ref[:, sl]
            dy = dy_ref[:, sl]
            mu = jnp.mean(o, axis=-1, keepdims=True)
            oc = o - mu
            var = jnp.mean(oc * oc, axis=-1, keepdims=True)
            r = lax.rsqrt(var + NORM_EPS)
            xh = oc * r
            dn = dy * _silu(z)
            dz_ref[:, sl] = (dy * (xh * g) * _dsilu(z)).astype(_MXU)
            dg_ref[:, sl] += jnp.sum(dn * xh, axis=0, keepdims=True)
            dxh = dn * g
            do_ref[:, sl] = r * (dxh - jnp.mean(dxh, axis=-1, keepdims=True)
                                 - xh * jnp.mean(dxh * xh, axis=-1, keepdims=True))

    return pl.pallas_call(
        body, name="ret_gate_bwd",
        out_shape=(jax.ShapeDtypeStruct((L, W), F32), jax.ShapeDtypeStruct((L, W), _MXU),
                   jax.ShapeDtypeStruct((1, W), F32)),
        grid=(L // ROW_TILE,),
        in_specs=[_rows(W), _rows(W), pl.BlockSpec((ROW_TILE, W), lambda i: (i, zblk)), _whole((1, W))],
        out_specs=(_rows(W), _rows(W), _whole((1, W))),
        compiler_params=_params(("arbitrary",)),
    )(dy, o, qkz, gn_g)


def _heads_t(a):
    return a[:, :FOX_HEADS].T


def _fox_fwd(h, w, lnw, pad, tag, side=None):
    w_in, b_f, w_out = w
    L = h.shape[0]
    W = D_MODEL
    w_qkv = w_in[:, :3 * W]
    w_zf = jnp.concatenate([w_in[:, 3 * W:4 * W], jnp.pad(w_in[:, 4 * W:], ((0, 0), (0, LANES - FOX_HEADS)))], axis=1)
    bias = jnp.pad(b_f, (0, LANES - FOX_HEADS))[None, :]
    qkv = _matmul(h, w_qkv, "nn", _MXU, f"{tag}_qkv")
    zf = _matmul(h, w_zf, "nn", F32, f"{tag}_zf")
    cum = _fox_cum(zf, bias, pad, f"{tag}_cum")
    ab = _bias_terms(_heads_t(cum), pad)
    nb = W // LANES
    o, lse, side_out = _attn_fwd(qkv, 0, qkv, nb, qkv, 2 * nb, ab, FOX_HEADS, LANES, LANES, LANES ** -0.5,
                                 f"{tag}_attn_fwd", side)
    y = _gate_fwd(o, zf, f"{tag}_gate_fwd")
    mo = _matmul(y, w_out, "nn", F32, f"{tag}_out")
    zl, hn = _ln_fwd(h, mo, lnw[0], lnw[1], f"{tag}_ln_fwd")
    return hn, dict(h=h, qkv=qkv, zf=zf, ab=ab, o=o, lse=lse, y=y, zl=zl,
                    w_all=jnp.concatenate([w_qkv, w_zf], axis=1), w_out=w_out, bias=bias), side_out


def _fox_bwd(s, dzl, pad, tag, side=None):
    W = D_MODEL
    nb = W // LANES
    dy = _matmul(dzl, s["w_out"], "nt", F32, f"{tag}_dy")
    d_wout = _matmul(s["y"], dzl, "tn", F32, f"{tag}_dwout")
    do, delta, dzg = _gate_bwd(dy, s["o"], s["zf"], FOX_HEADS, f"{tag}_gate_bwd")
    qkv = s["qkv"]
    dq, dk, dv, dck, dcq, side_out = _attn_bwd(qkv, 0, qkv, nb, qkv, 2 * nb, do, s["ab"], s["lse"], delta,
                                               FOX_HEADS, LANES, LANES, LANES ** -0.5, f"{tag}_attn_bwd", side)
    dcum = jnp.pad((dck + dcq).reshape(FOX_HEADS, -1).T, ((0, 0), (0, LANES - FOX_HEADS)))
    df, dbias = _fox_cum_bwd(dcum, s["zf"], s["bias"], pad, f"{tag}_cum_bwd")
    dproj = jnp.concatenate([dq.astype(_MXU), dk.astype(_MXU), dv.astype(_MXU), dzg, df], axis=1)
    d_wall = _matmul(s["h"], dproj, "tn", F32, f"{tag}_dwin")
    dh = _matmul(dproj, s["w_all"], "nt", F32, f"{tag}_dh")
    d_win = jnp.concatenate([d_wall[:, :4 * W], d_wall[:, 4 * W:4 * W + FOX_HEADS]], axis=1)
    return dh, (d_win, dbias[0, :FOX_HEADS], d_wout), side_out


def _mla_tables(L, pad):
    pos = (jnp.arange(L) - pad).astype(F32)
    inv = ROPE_BASE ** (-jnp.arange(0, MLA_ROPE, 2, dtype=F32) / MLA_ROPE)
    ang = pos[:, None] * inv[None, :]
    c, s = jnp.cos(ang), jnp.sin(ang)
    z32, z64 = jnp.zeros_like(s), jnp.zeros((L, LANES - MLA_ROPE), F32)
    cos = jnp.concatenate([c, c, jnp.ones_like(z64)], axis=1)
    sa = jnp.concatenate([-s, z32, z64], axis=1)
    sb = jnp.concatenate([z32, s, z64], axis=1)
    return cos, sa, sb


def _mla_weights(w_in, w_uq, w_ukv):
    a, b = MLA_Q_LORA, MLA_Q_LORA + MLA_KV_LORA
    w_zc = jnp.concatenate([w_in[:, b + MLA_ROPE:], w_in[:, :b + MLA_ROPE],
                            jnp.zeros((D_MODEL, LANES - MLA_ROPE), w_in.dtype)], axis=1)
    uq = w_uq.reshape(MLA_Q_LORA, MLA_HEADS, MLA_NOPE + MLA_ROPE)
    uq = jnp.pad(uq, ((0, 0), (0, 0), (0, MLA_DK - MLA_NOPE - MLA_ROPE))).reshape(MLA_Q_LORA, MLA_HEADS * MLA_DK)
    ukv = w_ukv.reshape(MLA_KV_LORA, MLA_HEADS, MLA_NOPE + MLA_V)
    ukv = jnp.concatenate([ukv[:, :, :MLA_NOPE].reshape(MLA_KV_LORA, -1), ukv[:, :, MLA_NOPE:].reshape(MLA_KV_LORA, -1)], axis=1)
    return w_zc, uq, ukv


def _mla_fwd(h, w, lnw, pad):
    w_in, gq, gkv, w_uq, w_ukv, w_out = w
    L = h.shape[0]
    w_zc, uq, ukv = _mla_weights(w_in, w_uq, w_ukv)
    cos, sa, sb = _mla_tables(L, pad)
    proj = _matmul(h, w_zc, "nn", F32, "mla_proj")
    cqn, ckvn = _mla_norm_fwd(proj, gq[None, :], gkv[None, :])
    qf = _matmul(cqn, uq, "nn", F32, "mla_uq")
    kv = _matmul(ckvn, ukv, "nn", _MXU, "mla_ukv")
    q_full, k_full = _mla_rope_fwd(qf, kv, proj, cos, sa, sb)
    ab = _bias_terms(jnp.zeros((MLA_HEADS, L), F32), pad)
    nb = MLA_HEADS * MLA_NOPE // LANES
    o, lse, _ = _attn_fwd(q_full, 0, k_full, 0, kv, nb, ab, MLA_HEADS, MLA_DK, MLA_V,
                          (MLA_NOPE + MLA_ROPE) ** -0.5, "mla_attn_fwd")
    y = _gate_fwd(o, proj, "mla_gate_fwd")
    mo = _matmul(y, w_out, "nn", F32, "mla_out")
    zl, hn = _ln_fwd(h, mo, lnw[0], lnw[1], "mla_ln_fwd")
    return hn, dict(h=h, proj=proj, cqn=cqn, ckvn=ckvn, kv=kv, q_full=q_full, k_full=k_full, ab=ab,
                    o=o, lse=lse, y=y, zl=zl, w_zc=w_zc, uq=uq, ukv=ukv, w_out=w_out, gq=gq, gkv=gkv,
                    tabs=(cos, sa, sb))


def _mla_bwd(s, dzl):
    H = MLA_HEADS
    dy = _matmul(dzl, s["w_out"], "nt", F32, "mla_dy")
    d_wout = _matmul(s["y"], dzl, "tn", F32, "mla_dwout")
    do, delta, dzg = _gate_bwd(dy, s["o"], s["proj"], H, "mla_gate_bwd")
    nb = H * MLA_NOPE // LANES
    dq_full, dk_full, dv, _, _, _ = _attn_bwd(s["q_full"], 0, s["k_full"], 0, s["kv"], nb, do, s["ab"],
                                              s["lse"], delta,
                                              H, MLA_DK, MLA_V, (MLA_NOPE + MLA_ROPE) ** -0.5, "mla_attn_bwd")
    cos, sa, sb = s["tabs"]
    dqf, dkn, dkr = _mla_rope_bwd(dq_full, dk_full, cos, -sa, -sb)
    d_uq = _matmul(s["cqn"], dqf, "tn", F32, "mla_duq")
    dcqn = _matmul(dqf, s["uq"], "nt", F32, "mla_dcqn")
    dkv = jnp.concatenate([dkn, dv.astype(_MXU)], axis=1)
    d_ukv = _matmul(s["ckvn"], dkv, "tn", F32, "mla_dukv")
    dckvn = _matmul(dkv, s["ukv"], "nt", F32, "mla_dckvn")
    dcq, dckv, dgq, dgkv = _mla_norm_bwd(s["proj"], s["gq"][None, :], s["gkv"][None, :], dcqn, dckvn)
    dproj = jnp.concatenate([dzg, dcq, dckv, dkr], axis=1)
    d_wzc = _matmul(s["h"], dproj, "tn", F32, "mla_dwin")
    dh = _matmul(dproj, s["w_zc"], "nt", F32, "mla_dh")
    W = D_MODEL
    d_win = jnp.concatenate([d_wzc[:, W:W + MLA_Q_LORA + MLA_KV_LORA + MLA_ROPE], d_wzc[:, :W]], axis=1)
    d_wuq = d_uq.reshape(MLA_Q_LORA, H, MLA_DK)[:, :, :MLA_NOPE + MLA_ROPE].reshape(MLA_Q_LORA, -1)
    hk = H * MLA_NOPE
    d_wukv = jnp.concatenate([d_ukv[:, :hk].reshape(MLA_KV_LORA, H, MLA_NOPE),
                              d_ukv[:, hk:].reshape(MLA_KV_LORA, H, MLA_V)], axis=2).reshape(MLA_KV_LORA, -1)
    return dh, (d_win, dgq[0], dgkv[0], d_wuq, d_wukv, d_wout)


def _ret_tables(L, pad):
    pos = (jnp.arange(L) - pad).astype(F32)
    inv = 1.0 / (ROPE_BASE ** jnp.linspace(0.0, 1.0, RET_QK // 2, dtype=F32))
    ang = pos[:, None] * inv[None, :]
    return jnp.cos(ang), jnp.sin(ang)


def _ret_layer_fwd(h, w, lnw, pad):
    w_in, gn_g, w_out = w
    L = h.shape[0]
    q1 = 2 * _RET_QKW
    v1 = q1 + RET_HEADS * RET_V
    w_qkz = jnp.concatenate([w_in[:, :q1], w_in[:, v1:]], axis=1)
    cos, sin = _ret_tables(L, pad)
    qkz = _matmul(h, w_qkz, "nn", F32, "ret_qkz")
    v = _matmul(h, w_in[:, q1:v1], "nn", _MXU, "ret_v")
    qr, kr = _ret_rope_fwd(qkz, cos, sin, pad)
    o, states = _ret_fwd(qr, kr, v)
    y = _ret_gate_fwd(o, qkz, gn_g[None, :])
    mo = _matmul(y, w_out, "nn", F32, "ret_out")
    zl, hn = _ln_fwd(h, mo, lnw[0], lnw[1], "ret_ln_fwd")
    return hn, dict(h=h, qkz=qkz, v=v, qr=qr, kr=kr, o=o, states=states, y=y, zl=zl, w_in=w_in, w_out=w_out,
                    gn_g=gn_g, tabs=(cos, sin))


def _ret_layer_bwd(s, dzl, pad):
    dy = _matmul(dzl, s["w_out"], "nt", F32, "ret_dy")
    d_wout = _matmul(s["y"], dzl, "tn", F32, "ret_dwout")
    do, dzg, dgn = _ret_gate_bwd(dy, s["o"], s["qkz"], s["gn_g"][None, :])
    dqr, dkr, dv = _ret_bwd(s["qr"], s["kr"], s["v"], do, s["states"])
    cos, sin = s["tabs"]
    dq, dk = _ret_rope_bwd(dqr, dkr, cos, sin, pad)
    dproj = jnp.concatenate([dq, dk, dv, dzg], axis=1)
    d_win = _matmul(s["h"], dproj, "tn", F32, "ret_dwin")
    dh = _matmul(dproj, s["w_in"], "nt", F32, "ret_dh")
    return dh, (d_win, dgn[0], d_wout)


def _local_step(x, target, meta, fox0_w, late_shards, small, ln_g, ln_b):
    n_real = x.shape[0]
    first = -(-N_META // ROW_TILE)
    pad = first * ROW_TILE - N_META
    h0 = jnp.concatenate([jnp.zeros((pad, D_MODEL), F32), meta, x], axis=0)
    lnw = [(ln_g[i][None, :], ln_b[i][None, :]) for i in range(4)]
    b_f = small["fox_b_f"]

    h1, s0, gathered = _fox_fwd(h0, (fox0_w[0], b_f[0], fox0_w[1]), lnw[0], pad, "fox0", (late_shards, False))
    w = {key: _merge_layer(key[0], g).astype(_MXU) for key, g in zip(_LATE, gathered)}
    h2, s1 = _mla_fwd(h1, (w["mla_w_in", 0], small["mla_q_norm"][0], small["mla_kv_norm"][0], w["mla_w_uq", 0],
                           w["mla_w_ukv", 0], w["mla_w_out", 0]), lnw[1], pad)
    h3, s2 = _ret_layer_fwd(h2, (w["ret_w_in", 0], small["ret_gn_g"][0], w["ret_w_out", 0]), lnw[2], pad)
    h4, s3, _ = _fox_fwd(h3, (w["fox_w_in", 1], b_f[1], w["fox_w_out", 1]), lnw[3], pad, "fox1")

    dh4, loss = _loss_head(h4, target, first)
    dz3, dg3, db3 = _ln_bwd(s3["zl"], lnw[3][0], dh4, None, "ln3_bwd")
    dm3, gf1, _ = _fox_bwd(s3, dz3, pad, "fox1")
    dz2, dg2, db2 = _ln_bwd(s2["zl"], lnw[2][0], dz3, dm3, "ln2_bwd")
    dm2, gr = _ret_layer_bwd(s2, dz2, pad)
    dz1, dg1, db1 = _ln_bwd(s1["zl"], lnw[1][0], dz2, dm2, "ln1_bwd")
    dm1, gm = _mla_bwd(s1, dz1)
    dz0, dg0, db0 = _ln_bwd(s0["zl"], lnw[0][0], dz1, dm1, "ln0_bwd")
    late = {("fox_w_in", 1): gf1[0], ("fox_w_out", 1): gf1[2], ("mla_w_in", 0): gm[0], ("mla_w_uq", 0): gm[3],
            ("mla_w_ukv", 0): gm[4], ("mla_w_out", 0): gm[5], ("ret_w_in", 0): gr[0], ("ret_w_out", 0): gr[2]}
    dm0, gf0, late_parts = _fox_bwd(s0, dz0, pad, "fox0", ([_split_layer(k[0], late[k]) for k in _LATE], True))
    grad_x, grad_meta = _input_grads(dz0, dm0, first, n_real)

    small_grads = dict(
        meta=grad_meta, fox_b_f=jnp.stack([gf0[1], gf1[1]]), mla_q_norm=gm[1][None], mla_kv_norm=gm[2][None],
        ret_gn_g=gr[1][None], ln_g=jnp.concatenate([dg0, dg1, dg2, dg3], axis=0),
        ln_b=jnp.concatenate([db0, db1, db2, db3], axis=0))
    return loss, grad_x, (gf0[0], gf0[2]), late_parts, small_grads


def _exchange(tensors, scatter, name):
    nt = len(tensors)

    def body(*refs):
        copies = _exchange_copies(refs[:nt], refs[nt:2 * nt], *refs[2 * nt:], scatter)
        for cp in copies:
            cp.start()
        for cp in copies:
            cp.wait()

    hbm = pl.BlockSpec(memory_space=pl.ANY)
    out_shape, sems = _exchange_shapes(tensors, scatter)
    return pl.pallas_call(
        body, name=name, out_shape=out_shape, in_specs=[hbm] * nt, out_specs=(hbm,) * nt, scratch_shapes=sems,
    )(*tensors)


def _adamw(parts, w, m, v, name):
    R, C = w.shape
    tr = next(t for t in range(min(R, 256), 0, -1) if R % t == 0 and (t % 8 == 0 or t == R))

    def body(p_ref, w_ref, m_ref, v_ref, g_ref, d_ref, mo_ref, vo_ref):
        g = p_ref[0]
        for d in range(1, N_DEV):
            g = g + p_ref[d]
        mn = ADAM_B1 * m_ref[...] + (1.0 - ADAM_B1) * g
        vn = ADAM_B2 * v_ref[...] + (1.0 - ADAM_B2) * (g * g)
        m_hat = mn / (1.0 - ADAM_B1 ** ADAM_STEP)
        v_hat = vn / (1.0 - ADAM_B2 ** ADAM_STEP)
        g_ref[...] = g
        d_ref[...] = -ADAM_LR * (m_hat / (jnp.sqrt(v_hat) + ADAM_EPS) + ADAM_WD * w_ref[...])
        mo_ref[...] = mn
        vo_ref[...] = vn

    blk = pl.BlockSpec((tr, C), lambda i: (i, 0))
    sd = jax.ShapeDtypeStruct((R, C), F32)
    return pl.pallas_call(
        body, name=name,
        out_shape=(sd, sd, sd, sd),
        grid=(R // tr,),
        in_specs=[pl.BlockSpec((N_DEV, tr, C), lambda i: (0, i, 0)), blk, blk, blk],
        out_specs=(blk, blk, blk, blk),
        compiler_params=_params(("parallel",)),
    )(parts, w, m, v)


_SPECS = dict(
    meta=((16, 1024), 1),
    fox_w_in=((2, 1024, 4104), 2), fox_b_f=((2, 8), None), fox_w_out=((2, 1024, 1024), 1),
    mla_w_in=((1, 1024, 1728), 2), mla_q_norm=((1, 384), None), mla_kv_norm=((1, 256), None),
    mla_w_uq=((1, 384, 1536), 2), mla_w_ukv=((1, 256, 2048), 2), mla_w_out=((1, 1024, 1024), 1),
    ret_w_in=((1, 1024, 6144), 2), ret_gn_g=((1, 2048), 1), ret_w_out=((1, 2048, 1024), 1),
    ln_g=((4, 1024), None), ln_b=((4, 1024), None),
)
_NAMES = list(_SPECS)
_BIG = ["fox_w_in", "fox_w_out", "mla_w_in", "mla_w_uq", "mla_w_ukv", "mla_w_out", "ret_w_in", "ret_w_out"]
_SMALL = [n for n in _NAMES if n not in _BIG]
_SMALL_GATHERED = ["meta", "ret_gn_g"]
_PACK_ROWS = 8


def _shard_shape(name):
    shape, ax = _SPECS[name]
    if ax is None:
        return shape
    return tuple(s // N_DEV if i == ax else s for i, s in enumerate(shape))


_EARLY = [("fox_w_in", 0), ("fox_w_out", 0)]
_LATE = [("fox_w_in", 1), ("fox_w_out", 1), ("mla_w_in", 0), ("mla_w_uq", 0), ("mla_w_ukv", 0), ("mla_w_out", 0),
         ("ret_w_in", 0), ("ret_w_out", 0)]


def _split_at(full, ax):
    shape = full.shape
    parts = full.reshape(shape[:ax] + (N_DEV, shape[ax] // N_DEV) + shape[ax + 1:])
    return jnp.moveaxis(parts, ax, 0)


def _merge_at(stacked, ax):
    full = jnp.moveaxis(stacked, 0, ax)
    return full.reshape(full.shape[:ax] + (-1,) + full.shape[ax + 2:])


def _split(name, full):
    return _split_at(full, _SPECS[name][1])


def _merge(name, stacked):
    return _merge_at(stacked, _SPECS[name][1])


def _split_layer(name, full):
    return _split_at(full, _SPECS[name][1] - 1)


def _merge_layer(name, stacked):
    return _merge_at(stacked, _SPECS[name][1] - 1)


def _pad_rows(flat):
    n = flat.shape[-1]
    unit = _PACK_ROWS * LANES
    total = -(-n // unit) * unit
    flat = jnp.pad(flat, [(0, 0)] * (flat.ndim - 1) + [(0, total - n)])
    return flat.reshape(flat.shape[:-1] + (total // LANES, LANES))


def _pack_small(tree, names, tail):
    return _pad_rows(jnp.concatenate([tree[n].reshape(-1) for n in names] + [tail]))


def _unpack_small(buf, names):
    flat, out, at = buf.reshape(-1), {}, 0
    for n in names:
        shp = _shard_shape(n)
        out[n] = flat[at:at + math.prod(shp)].reshape(shp)
        at += math.prod(shp)
    return out, flat[at:]


def kernel(x, meta, fox_w_in, fox_b_f, fox_w_out, mla_w_in, mla_q_norm, mla_kv_norm, mla_w_uq, mla_w_ukv, mla_w_out, ret_w_in, ret_gn_g, ret_w_out, ln_g, ln_b, loss_target, m_meta, m_fox_w_in, m_fox_b_f, m_fox_w_out, m_mla_w_in, m_mla_q_norm, m_mla_kv_norm, m_mla_w_uq, m_mla_w_ukv, m_mla_w_out, m_ret_w_in, m_ret_gn_g, m_ret_w_out, m_ln_g, m_ln_b, v_meta, v_fox_w_in, v_fox_b_f, v_fox_w_out, v_mla_w_in, v_mla_q_norm, v_mla_kv_norm, v_mla_w_uq, v_mla_w_ukv, v_mla_w_out, v_ret_w_in, v_ret_gn_g, v_ret_w_out, v_ln_g, v_ln_b):
    w = dict(meta=meta, fox_w_in=fox_w_in, fox_b_f=fox_b_f, fox_w_out=fox_w_out, mla_w_in=mla_w_in,
             mla_q_norm=mla_q_norm, mla_kv_norm=mla_kv_norm, mla_w_uq=mla_w_uq, mla_w_ukv=mla_w_ukv,
             mla_w_out=mla_w_out, ret_w_in=ret_w_in, ret_gn_g=ret_gn_g, ret_w_out=ret_w_out, ln_g=ln_g, ln_b=ln_b)
    m = dict(meta=m_meta, fox_w_in=m_fox_w_in, fox_b_f=m_fox_b_f, fox_w_out=m_fox_w_out, mla_w_in=m_mla_w_in,
             mla_q_norm=m_mla_q_norm, mla_kv_norm=m_mla_kv_norm, mla_w_uq=m_mla_w_uq, mla_w_ukv=m_mla_w_ukv,
             mla_w_out=m_mla_w_out, ret_w_in=m_ret_w_in, ret_gn_g=m_ret_gn_g, ret_w_out=m_ret_w_out, ln_g=m_ln_g, ln_b=m_ln_b)
    v = dict(meta=v_meta, fox_w_in=v_fox_w_in, fox_b_f=v_fox_b_f, fox_w_out=v_fox_w_out, mla_w_in=v_mla_w_in,
             mla_q_norm=v_mla_q_norm, mla_kv_norm=v_mla_kv_norm, mla_w_uq=v_mla_w_uq, mla_w_ukv=v_mla_w_ukv,
             mla_w_out=v_mla_w_out, ret_w_in=v_ret_w_in, ret_gn_g=v_ret_gn_g, ret_w_out=v_ret_w_out, ln_g=v_ln_g, ln_b=v_ln_b)

    def shard16(key):
        return w[key[0]][key[1]].astype(jnp.bfloat16)

    none = jnp.zeros((0,), F32)
    gathered = _exchange([shard16(k) for k in _EARLY] + [_pack_small(w, _SMALL_GATHERED, none)],
                         False, "weights_all_gather")
    fox0_w = [_merge_layer(k[0], g).astype(_MXU) for k, g in zip(_EARLY, gathered)]
    small = {n: w[n] for n in _SMALL}
    pieces = [_unpack_small(gathered[-1][d], _SMALL_GATHERED)[0] for d in range(N_DEV)]
    for n in _SMALL_GATHERED:
        small[n] = _merge(n, jnp.stack([p[n] for p in pieces]))

    loss, grad_x, early_grads, late_parts, grads = _local_step(
        x[0], loss_target[0], small["meta"], fox0_w, [shard16(k) for k in _LATE], small, ln_g, ln_b)

    def small_rows(n):
        if _SPECS[n][1] is None:
            return jnp.broadcast_to(grads[n].reshape(1, -1), (N_DEV, grads[n].size))
        return _split(n, grads[n]).reshape(N_DEV, -1)

    small_out = _pad_rows(jnp.concatenate([small_rows(n) for n in _SMALL]
                                          + [jnp.broadcast_to(loss.reshape(1, 1), (N_DEV, 1))], axis=1))
    parts = _exchange([_split_layer(k[0], g) for k, g in zip(_EARLY, early_grads)] + [small_out],
                      True, "grads_all_to_all")
    by_key = dict(zip(_EARLY + _LATE, list(parts[:-1]) + list(late_parts)))

    def rows2d(a):
        return a.reshape(-1, a.shape[-1])

    out = {}
    for n in _BIG:
        layers = [by_key[n, i] for i in range(_SPECS[n][0][0])]
        p = layers[0] if len(layers) == 1 else jnp.stack(layers, axis=1)
        res = _adamw(p.reshape(N_DEV, -1, p.shape[-1]), rows2d(w[n]), rows2d(m[n]), rows2d(v[n]), f"adamw_{n}")
        out[n] = [r.reshape(w[n].shape) for r in res]
    zero1 = jnp.zeros((1,), F32)
    res = _adamw(parts[-1], _pack_small(w, _SMALL, zero1), _pack_small(m, _SMALL, zero1),
                 _pack_small(v, _SMALL, zero1), "adamw_small")
    small_res = [_unpack_small(r, _SMALL) for r in res]
    for n in _SMALL:
        out[n] = [s[0][n] for s in small_res]
    total_loss = small_res[0][1][0]
    return (total_loss, grad_x[None], *[out[n][0] for n in _NAMES], *[out[n][1] for n in _NAMES],
            *[out[n][2] for n in _NAMES], *[out[n][3] for n in _NAMES])
```

```python
import functools
import math

import numpy as np
import jax
import jax.numpy as jnp
from jax import lax
from jax.experimental import pallas as pl
from jax.experimental.pallas import tpu as pltpu

F32 = jnp.float32
_MXU = jnp.bfloat16

N_DEV = 8
N_META = 16
D_MODEL = 1024
ROW_TILE = 512
LANES = 128
CHUNK = 128
NEG = -1e30
VMEM_LIMIT = 60 * 1024 * 1024

FOX_HEADS = 8
MLA_HEADS = 8
MLA_NOPE, MLA_ROPE, MLA_V = 128, 64, 128
MLA_Q_LORA, MLA_KV_LORA = 384, 256
MLA_DK = 256
RET_HEADS, RET_QK, RET_V = 4, 256, 512
ROPE_BASE = 10000.0
ALPHA = (2 * 4) ** 0.25
NORM_EPS = 1e-5

ADAM_LR, ADAM_B1, ADAM_B2, ADAM_EPS, ADAM_WD, ADAM_STEP = 0.001, 0.9, 0.999, 1e-08, 0.01, 10


def _params(sem, vmem=VMEM_LIMIT):
    return pltpu.CompilerParams(dimension_semantics=sem, vmem_limit_bytes=vmem)


def _tile(n, pref):
    if n <= pref:
        return n
    t = (pref // LANES) * LANES
    while n % t:
        t -= LANES
    return t


_MM_TILES = {"nn": (512, 2048, 2048), "nt": (512, 1024, 4224), "tn": (1024, 2048, 768)}


def _matmul(a, b, mode, out_dtype, name):
    if mode == "nn":
        (M, K), (K2, N) = a.shape, b.shape
    elif mode == "nt":
        (M, K), (N, K2) = a.shape, b.shape
    else:
        (K, M), (K2, N) = a.shape, b.shape
    assert K == K2, (a.shape, b.shape, mode)
    tm, tn, tk = (_tile(d, p) for d, p in zip((M, N, K), _MM_TILES[mode]))
    nk = K // tk
    if mode == "nn":
        a_spec = pl.BlockSpec((tm, tk), lambda i, j, k: (i, k))
        b_spec = pl.BlockSpec((tk, tn), lambda i, j, k: (k, j))
        dn = (((1,), (0,)), ((), ()))
    elif mode == "nt":
        a_spec = pl.BlockSpec((tm, tk), lambda i, j, k: (i, k))
        b_spec = pl.BlockSpec((tn, tk), lambda i, j, k: (j, k))
        dn = (((1,), (1,)), ((), ()))
    else:
        a_spec = pl.BlockSpec((tk, tm), lambda i, j, k: (k, i))
        b_spec = pl.BlockSpec((tk, tn), lambda i, j, k: (k, j))
        dn = (((0,), (0,)), ((), ()))

    def body(a_ref, b_ref, o_ref, acc_ref):
        k = pl.program_id(2)
        part = lax.dot_general(a_ref[...].astype(_MXU), b_ref[...].astype(_MXU), dn,
                               preferred_element_type=F32)

        @pl.when(k == 0)
        def _():
            acc_ref[...] = part

        @pl.when(k > 0)
        def _():
            acc_ref[...] += part

        @pl.when(k == nk - 1)
        def _():
            o_ref[...] = acc_ref[...].astype(out_dtype)

    return pl.pallas_call(
        body, name=name,
        out_shape=jax.ShapeDtypeStruct((M, N), out_dtype),
        grid=(M // tm, N // tn, nk),
        in_specs=[a_spec, b_spec],
        out_specs=pl.BlockSpec((tm, tn), lambda i, j, k: (i, j)),
        scratch_shapes=[pltpu.VMEM((tm, tn), F32)],
        compiler_params=_params(("parallel", "parallel", "arbitrary")),
    )(a, b)


def _rows(w, off=0):
    return pl.BlockSpec((ROW_TILE, w), lambda i: (i + off, 0))


def _whole(shape):
    return pl.BlockSpec(shape, lambda i: (0,) * len(shape))


def _silu(z):
    return z * jax.nn.sigmoid(z)


def _dsilu(z):
    s = jax.nn.sigmoid(z)
    return s * (1.0 + z * (1.0 - s))


def _ln_fwd(h, y, g, b, name):
    L, D = h.shape

    def body(h_ref, y_ref, g_ref, b_ref, z_ref, o_ref):
        z = ALPHA * h_ref[...] + y_ref[...]
        mu = jnp.mean(z, axis=-1, keepdims=True)
        zc = z - mu
        var = jnp.mean(zc * zc, axis=-1, keepdims=True)
        z_ref[...] = z
        o_ref[...] = zc * lax.rsqrt(var + NORM_EPS) * g_ref[...] + b_ref[...]

    return pl.pallas_call(
        body, name=name,
        out_shape=(jax.ShapeDtypeStruct((L, D), F32), jax.ShapeDtypeStruct((L, D), F32)),
        grid=(L // ROW_TILE,),
        in_specs=[_rows(D), _rows(D), _whole((1, D)), _whole((1, D))],
        out_specs=(_rows(D), _rows(D)),
        compiler_params=_params(("parallel",)),
    )(h, y, g, b)


def _ln_bwd(zl, g, ga, gb, name):
    L, D = zl.shape
    two = gb is not None

    def body(*refs):
        if two:
            z_ref, g_ref, ga_ref, gb_ref, dz_ref, dg_ref, db_ref = refs
            gout = ALPHA * ga_ref[...] + gb_ref[...]
        else:
            z_ref, g_ref, ga_ref, dz_ref, dg_ref, db_ref = refs
            gout = ga_ref[...]
        z = z_ref[...]
        mu = jnp.mean(z, axis=-1, keepdims=True)
        zc = z - mu
        var = jnp.mean(zc * zc, axis=-1, keepdims=True)
        rstd = lax.rsqrt(var + NORM_EPS)
        xhat = zc * rstd
        dxh = gout * g_ref[...]
        m1 = jnp.mean(dxh, axis=-1, keepdims=True)
        m2 = jnp.mean(dxh * xhat, axis=-1, keepdims=True)
        dz_ref[...] = rstd * (dxh - m1 - xhat * m2)

        @pl.when(pl.program_id(0) == 0)
        def _():
            dg_ref[...] = jnp.zeros_like(dg_ref)
            db_ref[...] = jnp.zeros_like(db_ref)

        dg_ref[...] += jnp.sum(gout * xhat, axis=0, keepdims=True)
        db_ref[...] += jnp.sum(gout, axis=0, keepdims=True)

    ins = [zl, g, ga] + ([gb] if two else [])
    return pl.pallas_call(
        body, name=name,
        out_shape=(jax.ShapeDtypeStruct((L, D), F32), jax.ShapeDtypeStruct((1, D), F32),
                   jax.ShapeDtypeStruct((1, D), F32)),
        grid=(L // ROW_TILE,),
        in_specs=[_rows(D), _whole((1, D)), _rows(D)] + ([_rows(D)] if two else []),
        out_specs=(_rows(D), _whole((1, D)), _whole((1, D))),
        compiler_params=_params(("arbitrary",)),
    )(*ins)


def _loss_head(h, target, first):
    L, D = h.shape

    def body(h_ref, t_ref, dh_ref, loss_ref):
        i = pl.program_id(0)

        @pl.when(i == 0)
        def _():
            loss_ref[...] = jnp.zeros_like(loss_ref)

        @pl.when(i < first)
        def _():
            dh_ref[...] = jnp.zeros_like(dh_ref)

        @pl.when(i >= first)
        def _():
            err = h_ref[...] - t_ref[...]
            dh_ref[...] = err * (1.0 / D)
            part = jnp.sum(jnp.sum(err * err, axis=-1, keepdims=True) * (1.0 / D), axis=0, keepdims=True)
            loss_ref[...] += 0.5 * part

    return pl.pallas_call(
        body, name="loss_head",
        out_shape=(jax.ShapeDtypeStruct((L, D), F32), jax.ShapeDtypeStruct((1, 1), F32)),
        grid=(L // ROW_TILE,),
        in_specs=[_rows(D), pl.BlockSpec((ROW_TILE, D), lambda i: (jnp.maximum(i - first, 0), 0))],
        out_specs=(_rows(D), _whole((1, 1))),
        compiler_params=_params(("arbitrary",)),
    )(h, target)


def _input_grads(dzl, dmix, first, n_real):
    L, D = dzl.shape

    def body(a_ref, b_ref, gx_ref, gm_ref):
        i = pl.program_id(0)
        val = ALPHA * a_ref[...] + b_ref[...]

        @pl.when(i == first - 1)
        def _():
            gm_ref[...] = val[ROW_TILE - N_META:, :]

        @pl.when(i >= first)
        def _():
            gx_ref[...] = val

    return pl.pallas_call(
        body, name="input_grads",
        out_shape=(jax.ShapeDtypeStruct((n_real, D), F32), jax.ShapeDtypeStruct((N_META, D), F32)),
        grid=(L // ROW_TILE,),
        in_specs=[_rows(D), _rows(D)],
        out_specs=(pl.BlockSpec((ROW_TILE, D), lambda i: (jnp.maximum(i - first, 0), 0)),
                   _whole((N_META, D))),
        compiler_params=_params(("arbitrary",)),
    )(dzl, dmix)


def _gate_fwd(o, zsrc, name):
    L, W = o.shape

    def body(o_ref, z_ref, y_ref):
        y_ref[...] = (o_ref[...] * _silu(z_ref[...])).astype(_MXU)

    return pl.pallas_call(
        body, name=name,
        out_shape=jax.ShapeDtypeStruct((L, W), _MXU),
        grid=(L // ROW_TILE,),
        in_specs=[_rows(W), _rows(W)],
        out_specs=_rows(W),
        compiler_params=_params(("parallel",)),
    )(o, zsrc)


def _gate_bwd(dy, o, zsrc, heads, name):
    L, W = o.shape
    hd = W // heads

    def body(dy_ref, o_ref, z_ref, do_ref, dl_ref, dz_ref):
        z = z_ref[...]
        dy = dy_ref[...]
        o = o_ref[...]
        do = dy * _silu(z)
        do_ref[...] = do.astype(_MXU)
        dz_ref[...] = (dy * o * _dsilu(z)).astype(_MXU)
        prod = do * o
        for h in range(heads):
            dl_ref[h] = _as_row(jnp.sum(prod[:, h * hd:(h + 1) * hd], axis=-1, keepdims=True))

    return pl.pallas_call(
        body, name=name,
        out_shape=(jax.ShapeDtypeStruct((L, W), _MXU), jax.ShapeDtypeStruct((heads, L // ROW_TILE, 1, ROW_TILE), F32),
                   jax.ShapeDtypeStruct((L, W), _MXU)),
        grid=(L // ROW_TILE,),
        in_specs=[_rows(W), _rows(W), _rows(W)],
        out_specs=(_rows(W), pl.BlockSpec((heads, None, 1, ROW_TILE), lambda i: (0, i, 0, 0)), _rows(W)),
        compiler_params=_params(("parallel",)),
    )(dy, o, zsrc)


def _split3(x):
    hi = x.astype(_MXU)
    r1 = x - hi.astype(F32)
    mid = r1.astype(_MXU)
    lo = (r1 - mid.astype(F32)).astype(_MXU)
    return hi, mid, lo


def _tri_cumsum(x, reverse):
    T = x.shape[0]
    r = lax.broadcasted_iota(jnp.int32, (T, T), 0)
    c = lax.broadcasted_iota(jnp.int32, (T, T), 1)
    tri = jnp.where((c >= r) if reverse else (c <= r), 1.0, 0.0).astype(_MXU)
    out = jnp.zeros(x.shape, F32)
    for part in _split3(x):
        out = out + jnp.dot(tri, part, preferred_element_type=F32)
    return out


def _fox_cum(zf, bias, pad, name):
    L, Wz = zf.shape
    off = Wz // LANES - 1

    def body(f_ref, b_ref, c_ref, carry_ref):
        i = pl.program_id(0)

        @pl.when(i == 0)
        def _():
            carry_ref[...] = jnp.zeros_like(carry_ref)

        x = f_ref[...] + b_ref[...]
        logf = jnp.minimum(x, 0.0) - jnp.log(1.0 + jnp.exp(-jnp.abs(x)))
        row = i * ROW_TILE + lax.broadcasted_iota(jnp.int32, logf.shape, 0)
        logf = jnp.where(row >= pad, logf, 0.0)
        cum = _tri_cumsum(logf, False) + carry_ref[...]
        c_ref[...] = cum
        carry_ref[...] = cum[ROW_TILE - 1:, :]

    return pl.pallas_call(
        body, name=name,
        out_shape=jax.ShapeDtypeStruct((L, LANES), F32),
        grid=(L // ROW_TILE,),
        in_specs=[pl.BlockSpec((ROW_TILE, LANES), lambda i: (i, off)), _whole((1, LANES))],
        out_specs=_rows(LANES),
        scratch_shapes=[pltpu.VMEM((1, LANES), F32)],
        compiler_params=_params(("arbitrary",)),
    )(zf, bias)


def _fox_cum_bwd(dcum, zf, bias, pad, name):
    L, Wz = zf.shape
    off = Wz // LANES - 1
    n = L // ROW_TILE

    def body(d_ref, f_ref, b_ref, df_ref, db_ref, carry_ref):
        i = pl.program_id(0)

        @pl.when(i == 0)
        def _():
            carry_ref[...] = jnp.zeros_like(carry_ref)
            db_ref[...] = jnp.zeros_like(db_ref)

        rc = _tri_cumsum(d_ref[...], True) + carry_ref[...]
        carry_ref[...] = rc[:1, :]
        x = f_ref[...] + b_ref[...]
        row = (n - 1 - i) * ROW_TILE + lax.broadcasted_iota(jnp.int32, x.shape, 0)
        df = jnp.where(row >= pad, rc * jax.nn.sigmoid(-x), 0.0)
        df_ref[...] = df.astype(_MXU)
        db_ref[...] += jnp.sum(df, axis=0, keepdims=True)

    return pl.pallas_call(
        body, name=name,
        out_shape=(jax.ShapeDtypeStruct((L, LANES), _MXU), jax.ShapeDtypeStruct((1, LANES), F32)),
        grid=(n,),
        in_specs=[pl.BlockSpec((ROW_TILE, LANES), lambda i: (n - 1 - i, 0)),
                  pl.BlockSpec((ROW_TILE, LANES), lambda i: (n - 1 - i, off)), _whole((1, LANES))],
        out_specs=(pl.BlockSpec((ROW_TILE, LANES), lambda i: (n - 1 - i, 0)), _whole((1, LANES))),
        scratch_shapes=[pltpu.VMEM((1, LANES), F32)],
        compiler_params=_params(("arbitrary",)),
    )(dcum, zf, bias)


LOG2E = 1.4426950408889634
LN2 = 0.6931471805599453


def _bias_terms(ct, pad):
    H, L = ct.shape
    T = ROW_TILE
    n = L // T
    c2 = ct * LOG2E
    k2 = jnp.where(jnp.arange(L)[None, :] < pad, -NEG, c2)
    return dict(cq_row=c2.reshape(H, n, 1, T), ck_row=k2.reshape(H, n, 1, T), ref=c2[:, ::T].reshape(H, n, 1, 1))


def _exchange_copies(srcs, outs, send_sems, recv_sems, local_sems, scatter):
    nt = len(srcs)
    x, y, c = lax.axis_index("x"), lax.axis_index("y"), lax.axis_index("c")
    me = 4 * x + 2 * y + c
    copies = [pltpu.make_async_copy(srcs[t].at[me] if scatter else srcs[t], outs[t].at[me], local_sems.at[t])
              for t in range(nt)]
    for k in range(1, N_DEV):
        px = (x + (k >> 2)) % 2
        py = (y + ((k >> 1) & 1)) % 2
        pc = (c + (k & 1)) % 2
        peer = 4 * px + 2 * py + pc
        for t in range(nt):
            copies.append(pltpu.make_async_remote_copy(
                src_ref=srcs[t].at[peer] if scatter else srcs[t],
                dst_ref=outs[t].at[me],
                send_sem=send_sems.at[k - 1, t], recv_sem=recv_sems.at[k - 1, t],
                device_id=(px, py, pc), device_id_type=pl.DeviceIdType.MESH))
    return copies


def _exchange_shapes(tensors, scatter):
    out_shape = tuple(jax.ShapeDtypeStruct(t.shape if scatter else (N_DEV,) + t.shape, t.dtype) for t in tensors)
    nt = len(tensors)
    sems = [pltpu.SemaphoreType.DMA((N_DEV - 1, nt)), pltpu.SemaphoreType.DMA((N_DEV - 1, nt)),
            pltpu.SemaphoreType.DMA((nt,))]
    return out_shape, sems


def _side_exchange(side, srcs, outs, sems, first, last):
    if not side:
        return

    @pl.when(first)
    def _():
        for cp in _exchange_copies(srcs, outs, *sems, side[1]):
            cp.start()

    @pl.when(last)
    def _():
        for cp in _exchange_copies(srcs, outs, *sems, side[1]):
            cp.wait()


def _as_row(col):
    return jnp.transpose(jnp.broadcast_to(col, (col.shape[0], LANES)))[0:1, :]


def _as_col(row):
    return jnp.transpose(jnp.broadcast_to(row, (LANES, row.shape[1])))[:, 0:1]


def _attn_fwd(q, qoff, k, koff, v, voff, bias, heads, dk, dv, scale, name, dead, side=None):
    L = q.shape[0]
    T = ROW_TILE
    n = L // T
    c = scale * LOG2E
    dn_qk = (((1,), (1,)), ((), ()))

    ns = len(side[0]) if side else 0

    def body(*refs):
        q_ref, k_ref, v_ref, ck_ref, ref_ref = refs[:5]
        o_ref, lse_ref = refs[5 + ns:7 + ns]
        s_a, s_b = refs[7 + 2 * ns:9 + 2 * ns]
        i = pl.program_id(1)
        _side_exchange(side, refs[5:5 + ns], refs[7 + ns:7 + 2 * ns], refs[9 + 2 * ns:],
                       (pl.program_id(0) == 0) & (i == 0), (pl.program_id(0) == heads - 1) & (i == n - 1))
        qb = q_ref[...]
        ref = ref_ref[...]

        def scores(j, dst):
            start = pl.multiple_of(j * T, T)
            dst[...] = lax.dot_general(qb, k_ref[pl.ds(start, T), :], dn_qk, preferred_element_type=F32)

        def update(t, vj, carry):
            m, l, acc = carry
            m_new = jnp.maximum(m, jnp.max(t, axis=1, keepdims=True))
            p = jnp.exp2(t - m_new)
            a = jnp.exp2(m - m_new)
            l = a * l + jnp.sum(p, axis=1, keepdims=True)
            acc = a * acc + jnp.dot(p.astype(_MXU), vj, preferred_element_type=F32)
            return m_new, l, acc

        def soft(j, carry, cur, diag):
            start = pl.multiple_of(j * T, T)
            t = cur[...] * c - (ck_ref[j] - ref)
            if diag:
                r = lax.broadcasted_iota(jnp.int32, (T, T), 0)
                cc = lax.broadcasted_iota(jnp.int32, (T, T), 1)
                t = jnp.where(cc <= r, t, NEG)
            return update(t, v_ref[pl.ds(start, T), :], carry)

        def head_tile(carry):
            t = lax.dot_general(qb, k_ref[dead:T, :], dn_qk, preferred_element_type=F32) * c
            return update(t - (ck_ref[0][:, dead:] - ref), v_ref[dead:T, :], carry)

        def step(j, carry, cur, nxt):
            scores(j + 1, nxt)
            return soft(j, carry, cur, False)

        first = jnp.where(i > 0, 1, 0) if dead else 0

        def pair(jj, carry):
            j0 = first + 2 * jj
            return step(j0 + 1, step(j0, carry, s_a, s_b), s_b, s_a)

        scores(first, s_a)
        carry = (jnp.full((T, 1), -jnp.inf, F32), jnp.zeros((T, 1), F32), jnp.zeros((T, dv), F32))
        if dead:
            carry = lax.cond(i > 0, head_tile, lambda cy: cy, carry)
        whole = i - first
        carry = lax.fori_loop(0, whole // 2, pair, carry)
        odd = (whole % 2) == 1
        carry = lax.cond(odd, lambda cy: step(i - 1, cy, s_a, s_b), lambda cy: cy, carry)
        m, l, acc = lax.cond(odd, lambda cy: soft(i, cy, s_b, True), lambda cy: soft(i, cy, s_a, True), carry)
        o_ref[...] = acc / l
        lse_ref[...] = _as_row(m + jnp.log2(l))

    hbm = pl.BlockSpec(memory_space=pl.ANY)
    side_shapes, side_sems = _exchange_shapes(*side) if side else ((), [])
    o, lse, *side_out = pl.pallas_call(
        body, name=name,
        out_shape=(jax.ShapeDtypeStruct((L, heads * dv), F32), jax.ShapeDtypeStruct((heads, n, 1, T), F32))
        + side_shapes,
        grid=(heads, n),
        in_specs=[pl.BlockSpec((T, dk), lambda h, i: (i, qoff + h)),
                  pl.BlockSpec((L, dk), lambda h, i: (0, koff + h)),
                  pl.BlockSpec((L, dv), lambda h, i: (0, voff + h)),
                  pl.BlockSpec((None, n, 1, T), lambda h, i: (h, 0, 0, 0)),
                  pl.BlockSpec((None, None, 1, 1), lambda h, i: (h, i, 0, 0))] + [hbm] * ns,
        out_specs=(pl.BlockSpec((T, dv), lambda h, i: (i, h)),
                   pl.BlockSpec((None, None, 1, T), lambda h, i: (h, i, 0, 0))) + (hbm,) * ns,
        scratch_shapes=[pltpu.VMEM((T, T), F32), pltpu.VMEM((T, T), F32)] + side_sems,
        compiler_params=_params(("arbitrary", "arbitrary")),
    )(q, k, v, bias["ck_row"], bias["ref"], *(side[0] if side else ()))
    return o, lse + (bias["cq_row"] - bias["ref"]), side_out


def _attn_bwd(q, qoff, k, koff, v, voff, do, bias, lse_row, delta_row, heads, dk, dv, scale, name, dead,
              side=None):
    L = q.shape[0]
    T = ROW_TILE
    n = L // T
    c = scale * LOG2E
    dn_nt = (((1,), (1,)), ((), ()))
    dn_tn = (((0,), (0,)), ((), ()))
    ns = len(side[0]) if side else 0

    def body(*refs):
        q_ref, k_ref, v_ref, do_ref, cq_ref, ck_ref, ref_ref, lse_ref, dl_ref = refs[:9]
        dq_hbm, dk_ref, dv_ref, dck_ref, dcq_ref = refs[9 + ns:14 + ns]
        dq_acc, sem, st_a, dp_a, st_b, dp_b = refs[14 + 2 * ns:20 + 2 * ns]
        h = pl.program_id(0)
        j = pl.program_id(1)
        _side_exchange(side, refs[9:9 + ns], refs[14 + ns:14 + 2 * ns], refs[20 + 2 * ns:],
                       (h == 0) & (j == 0), (h == heads - 1) & (j == n - 1))

        @pl.when(j == 0)
        def _():
            dq_acc[...] = jnp.zeros_like(dq_acc)
            dcq_ref[...] = jnp.zeros_like(dcq_ref)

        ref = ref_ref[...]

        def run(k0, R):
            kb = k_ref[k0:k0 + R, :]
            vb = v_ref[k0:k0 + R, :]
            bcol = _as_col(ck_ref[j][:, k0:k0 + R] - ref)

            def front(i, st_dst, dp_dst):
                start = pl.multiple_of(jnp.minimum(i, n - 1) * T, T)
                st_dst[0:R, :] = lax.dot_general(kb, q_ref[pl.ds(start, T), :], dn_nt, preferred_element_type=F32)
                dp_dst[0:R, :] = lax.dot_general(vb, do_ref[pl.ds(start, T), :], dn_nt, preferred_element_type=F32)

            def back(i, carry, st_cur, dp_cur, diag):
                dk_a, dv_a, dck_a = carry
                start = pl.multiple_of(i * T, T)
                qi = q_ref[pl.ds(start, T), :]
                doi = do_ref[pl.ds(start, T), :]
                arow = (cq_ref[i] - ref) - lse_ref[i]
                st = st_cur[0:R, :] * c + arow - bcol
                if diag:
                    r = lax.broadcasted_iota(jnp.int32, (R, T), 0) + k0
                    cc = lax.broadcasted_iota(jnp.int32, (R, T), 1)
                    st = jnp.where(r <= cc, st, NEG)
                pt = jnp.exp2(st)
                dv_a = dv_a + jnp.dot(pt.astype(_MXU), doi, preferred_element_type=F32)
                dst = pt * (dp_cur[0:R, :] - dl_ref[i])
                dck_a = dck_a - jnp.sum(dst, axis=1, keepdims=True)
                dcq_ref[i] += jnp.sum(dst, axis=0, keepdims=True)
                dsb = (dst * scale).astype(_MXU)
                dk_a = dk_a + jnp.dot(dsb, qi, preferred_element_type=F32)
                dq_acc[pl.ds(start, T), :] += lax.dot_general(dsb, kb, dn_tn, preferred_element_type=F32)
                return dk_a, dv_a, dck_a

            buf_a, buf_b = (st_a, dp_a), (st_b, dp_b)

            def step(i, carry, cur, nxt, diag=False):
                front(i + 1, *nxt)
                return back(i, carry, *cur, diag)

            def pair(t, carry):
                i0 = j + 1 + 2 * t
                return step(i0 + 1, step(i0, carry, buf_b, buf_a), buf_a, buf_b)

            front(j, *buf_a)
            init = (jnp.zeros((R, dk), F32), jnp.zeros((R, dv), F32), jnp.zeros((R, 1), F32))
            carry = step(j, init, buf_a, buf_b, True)
            rest = n - 1 - j
            carry = lax.fori_loop(0, rest // 2, pair, carry)
            dk_a, dv_a, dck_a = lax.cond((rest % 2) == 1, lambda cy: step(n - 1, cy, buf_b, buf_a),
                                         lambda cy: cy, carry)
            if k0:
                dk_ref[0:k0, :] = jnp.zeros((k0, dk), F32)
                dv_ref[0:k0, :] = jnp.zeros((k0, dv), F32)
                dck_ref[:, 0:k0] = jnp.zeros((1, k0), F32)
            dk_ref[k0:k0 + R, :] = dk_a
            dv_ref[k0:k0 + R, :] = dv_a
            dck_ref[:, k0:k0 + R] = _as_row(dck_a)

        if dead:
            pl.when(j == 0)(lambda: run(dead, T - dead))
            pl.when(j > 0)(lambda: run(0, T))
        else:
            run(0, T)

        @pl.when(j == n - 1)
        def _():
            cp = pltpu.make_async_copy(dq_acc, dq_hbm.at[:, pl.ds(pl.multiple_of(h * dk, dk), dk)], sem)
            cp.start()
            cp.wait()

    hbm = pl.BlockSpec(memory_space=pl.ANY)
    side_shapes, side_sems = _exchange_shapes(*side) if side else ((), [])
    dq, dk_, dv_, dck, dcq, *side_out = pl.pallas_call(
        body, name=name,
        out_shape=(jax.ShapeDtypeStruct((L, heads * dk), F32), jax.ShapeDtypeStruct((L, heads * dk), F32),
                   jax.ShapeDtypeStruct((L, heads * dv), F32), jax.ShapeDtypeStruct((heads, n, 1, T), F32),
                   jax.ShapeDtypeStruct((heads, n, 1, T), F32)) + side_shapes,
        grid=(heads, n),
        in_specs=[pl.BlockSpec((L, dk), lambda h, j: (0, qoff + h)),
                  pl.BlockSpec((T, dk), lambda h, j: (j, koff + h)),
                  pl.BlockSpec((T, dv), lambda h, j: (j, voff + h)),
                  pl.BlockSpec((L, dv), lambda h, j: (0, h)),
                  pl.BlockSpec((None, n, 1, T), lambda h, j: (h, 0, 0, 0)),
                  pl.BlockSpec((None, n, 1, T), lambda h, j: (h, 0, 0, 0)),
                  pl.BlockSpec((None, None, 1, 1), lambda h, j: (h, j, 0, 0)),
                  pl.BlockSpec((None, n, 1, T), lambda h, j: (h, 0, 0, 0)),
                  pl.BlockSpec((None, n, 1, T), lambda h, j: (h, 0, 0, 0))] + [hbm] * ns,
        out_specs=(hbm,
                   pl.BlockSpec((T, dk), lambda h, j: (j, h)),
                   pl.BlockSpec((T, dv), lambda h, j: (j, h)),
                   pl.BlockSpec((None, None, 1, T), lambda h, j: (h, j, 0, 0)),
                   pl.BlockSpec((None, n, 1, T), lambda h, j: (h, 0, 0, 0))) + (hbm,) * ns,
        scratch_shapes=[pltpu.VMEM((L, dk), F32), pltpu.SemaphoreType.DMA] + [pltpu.VMEM((T, T), F32)] * 4
        + side_sems,
        compiler_params=_params(("arbitrary", "arbitrary")),
    )(q, k, v, do, bias["cq_row"], bias["ck_row"], bias["ref"], lse_row, delta_row, *(side[0] if side else ()))
    return dq, dk_, dv_, dck, dcq, side_out


_CQ0, _CKV0, _KR0 = D_MODEL, D_MODEL + MLA_Q_LORA, D_MODEL + MLA_Q_LORA + MLA_KV_LORA
_MLA_PROJ = _KR0 + LANES


def _rms(x, g):
    ms = jnp.mean(x * x, axis=-1, keepdims=True)
    return x * lax.rsqrt(ms + NORM_EPS) * g


def _rms_bwd(x, g, dy):
    ms = jnp.mean(x * x, axis=-1, keepdims=True)
    r = lax.rsqrt(ms + NORM_EPS)
    xh = x * r
    dxh = dy * g
    dx = r * (dxh - xh * jnp.mean(dxh * xh, axis=-1, keepdims=True))
    return dx, jnp.sum(dy * xh, axis=0, keepdims=True)


def _mla_norm_fwd(proj, gq, gkv):
    L = proj.shape[0]

    def body(p_ref, gq_ref, gkv_ref, cq_ref, ckv_ref):
        cq_ref[...] = _rms(p_ref[:, _CQ0:_CKV0], gq_ref[...]).astype(_MXU)
        ckv_ref[...] = _rms(p_ref[:, _CKV0:_KR0], gkv_ref[...]).astype(_MXU)

    return pl.pallas_call(
        body, name="mla_norm_fwd",
        out_shape=(jax.ShapeDtypeStruct((L, MLA_Q_LORA), _MXU), jax.ShapeDtypeStruct((L, MLA_KV_LORA), _MXU)),
        grid=(L // ROW_TILE,),
        in_specs=[_rows(_MLA_PROJ), _whole((1, MLA_Q_LORA)), _whole((1, MLA_KV_LORA))],
        out_specs=(_rows(MLA_Q_LORA), _rows(MLA_KV_LORA)),
        compiler_params=_params(("parallel",)),
    )(proj, gq, gkv)


def _mla_norm_bwd(proj, gq, gkv, dcqn, dckvn):
    L = proj.shape[0]

    def body(p_ref, gq_ref, gkv_ref, dq_ref, dkv_ref, dcq_ref, dckv_ref, dgq_ref, dgkv_ref):
        @pl.when(pl.program_id(0) == 0)
        def _():
            dgq_ref[...] = jnp.zeros_like(dgq_ref)
            dgkv_ref[...] = jnp.zeros_like(dgkv_ref)

        dx, dg = _rms_bwd(p_ref[:, _CQ0:_CKV0], gq_ref[...], dq_ref[...])
        dcq_ref[...] = dx.astype(_MXU)
        dgq_ref[...] += dg
        dx, dg = _rms_bwd(p_ref[:, _CKV0:_KR0], gkv_ref[...], dkv_ref[...])
        dckv_ref[...] = dx.astype(_MXU)
        dgkv_ref[...] += dg

    return pl.pallas_call(
        body, name="mla_norm_bwd",
        out_shape=(jax.ShapeDtypeStruct((L, MLA_Q_LORA), _MXU), jax.ShapeDtypeStruct((L, MLA_KV_LORA), _MXU),
                   jax.ShapeDtypeStruct((1, MLA_Q_LORA), F32), jax.ShapeDtypeStruct((1, MLA_KV_LORA), F32)),
        grid=(L // ROW_TILE,),
        in_specs=[_rows(_MLA_PROJ), _whole((1, MLA_Q_LORA)), _whole((1, MLA_KV_LORA)),
                  _rows(MLA_Q_LORA), _rows(MLA_KV_LORA)],
        out_specs=(_rows(MLA_Q_LORA), _rows(MLA_KV_LORA), _whole((1, MLA_Q_LORA)), _whole((1, MLA_KV_LORA))),
        compiler_params=_params(("arbitrary",)),
    )(proj, gq, gkv, dcqn, dckvn)


def _rot_tile(t, cos, sa, sb):
    half = MLA_ROPE // 2
    return t * cos + pltpu.roll(t, LANES - half, 1) * sa + pltpu.roll(t, half, 1) * sb


def _mla_rope_fwd(qf, kv, proj, cos, sa, sb):
    L = qf.shape[0]
    H = MLA_HEADS

    def body(q_ref, kv_ref, p_ref, c_ref, sa_ref, sb_ref, qo_ref, ko_ref):
        cos, sa, sb = c_ref[...], sa_ref[...], sb_ref[...]
        kr = _rot_tile(p_ref[...], cos, sa, sb).astype(_MXU)
        for h in range(H):
            b = h * MLA_DK
            qo_ref[:, b:b + LANES] = q_ref[:, b:b + LANES].astype(_MXU)
            qo_ref[:, b + LANES:b + 2 * LANES] = _rot_tile(q_ref[:, b + LANES:b + 2 * LANES], cos, sa, sb).astype(_MXU)
            ko_ref[:, b:b + LANES] = kv_ref[:, h * LANES:(h + 1) * LANES]
            ko_ref[:, b + LANES:b + 2 * LANES] = kr

    W = H * MLA_DK
    return pl.pallas_call(
        body, name="mla_rope_fwd",
        out_shape=(jax.ShapeDtypeStruct((L, W), _MXU), jax.ShapeDtypeStruct((L, W), _MXU)),
        grid=(L // ROW_TILE,),
        in_specs=[_rows(W), _rows(kv.shape[1]), pl.BlockSpec((ROW_TILE, LANES), lambda i: (i, _KR0 // LANES)),
                  _rows(LANES), _rows(LANES), _rows(LANES)],
        out_specs=(_rows(W), _rows(W)),
        compiler_params=_params(("parallel",)),
    )(qf, kv, proj, cos, sa, sb)


def _mla_rope_bwd(dq_full, dk_full, cos, sa, sb):
    L = dq_full.shape[0]
    H = MLA_HEADS
    W = H * MLA_DK

    def body(dq_ref, dk_ref, c_ref, sa_ref, sb_ref, dqf_ref, dkn_ref, dkr_ref):
        cos, sa, sb = c_ref[...], sa_ref[...], sb_ref[...]
        lane = lax.broadcasted_iota(jnp.int32, (ROW_TILE, LANES), 1)
        live = lane < MLA_ROPE
        krs = jnp.zeros((ROW_TILE, LANES), F32)
        for h in range(H):
            b = h * MLA_DK
            dqf_ref[:, b:b + LANES] = dq_ref[:, b:b + LANES].astype(_MXU)
            dqr = _rot_tile(dq_ref[:, b + LANES:b + 2 * LANES], cos, sa, sb)
            dqf_ref[:, b + LANES:b + 2 * LANES] = jnp.where(live, dqr, 0.0).astype(_MXU)
            dkn_ref[:, h * LANES:(h + 1) * LANES] = dk_ref[:, b:b + LANES].astype(_MXU)
            krs = krs + dk_ref[:, b + LANES:b + 2 * LANES]
        dkr_ref[...] = jnp.where(live, _rot_tile(krs, cos, sa, sb), 0.0).astype(_MXU)

    return pl.pallas_call(
        body, name="mla_rope_bwd",
        out_shape=(jax.ShapeDtypeStruct((L, W), _MXU), jax.ShapeDtypeStruct((L, H * LANES), _MXU),
                   jax.ShapeDtypeStruct((L, LANES), _MXU)),
        grid=(L // ROW_TILE,),
        in_specs=[_rows(W), _rows(W), _rows(LANES), _rows(LANES), _rows(LANES)],
        out_specs=(_rows(W), _rows(H * LANES), _rows(LANES)),
        compiler_params=_params(("parallel",)),
    )(dq_full, dk_full, cos, sa, sb)


_RET_QKW = RET_HEADS * RET_QK


def _ret_rope_fwd(qkz, cos, sin, pad):
    L = qkz.shape[0]
    hh = RET_QK // 2
    kscale = RET_QK ** -0.5

    def body(p_ref, c_ref, s_ref, q_ref, k_ref):
        cos, sin = c_ref[...], s_ref[...]
        row = pl.program_id(0) * ROW_TILE + lax.broadcasted_iota(jnp.int32, (ROW_TILE, hh), 0)
        keep = row >= pad
        for h in range(RET_HEADS):
            b = h * RET_QK
            t1, t2 = p_ref[:, b:b + hh], p_ref[:, b + hh:b + 2 * hh]
            q_ref[:, b:b + hh] = (t1 * cos - t2 * sin).astype(_MXU)
            q_ref[:, b + hh:b + 2 * hh] = (t2 * cos + t1 * sin).astype(_MXU)
            t1, t2 = p_ref[:, _RET_QKW + b:_RET_QKW + b + hh], p_ref[:, _RET_QKW + b + hh:_RET_QKW + b + 2 * hh]
            k_ref[:, b:b + hh] = jnp.where(keep, (t1 * cos - t2 * sin) * kscale, 0.0)
            k_ref[:, b + hh:b + 2 * hh] = jnp.where(keep, (t2 * cos + t1 * sin) * kscale, 0.0)

    return pl.pallas_call(
        body, name="ret_rope_fwd",
        out_shape=(jax.ShapeDtypeStruct((L, _RET_QKW), _MXU), jax.ShapeDtypeStruct((L, _RET_QKW), F32)),
        grid=(L // ROW_TILE,),
        in_specs=[_rows(2 * _RET_QKW), _rows(hh), _rows(hh)],
        out_specs=(_rows(_RET_QKW), _rows(_RET_QKW)),
        compiler_params=_params(("parallel",)),
    )(qkz, cos, sin)


def _ret_rope_bwd(dqr, dkr, cos, sin, pad):
    L = dqr.shape[0]
    hh = RET_QK // 2
    kscale = RET_QK ** -0.5

    def body(dq_ref, dk_ref, c_ref, s_ref, oq_ref, ok_ref):
        cos, sin = c_ref[...], s_ref[...]
        row = pl.program_id(0) * ROW_TILE + lax.broadcasted_iota(jnp.int32, (ROW_TILE, hh), 0)
        keep = row >= pad
        for h in range(RET_HEADS):
            b = h * RET_QK
            d1, d2 = dq_ref[:, b:b + hh], dq_ref[:, b + hh:b + 2 * hh]
            oq_ref[:, b:b + hh] = (d1 * cos + d2 * sin).astype(_MXU)
            oq_ref[:, b + hh:b + 2 * hh] = (d2 * cos - d1 * sin).astype(_MXU)
            d1, d2 = dk_ref[:, b:b + hh], dk_ref[:, b + hh:b + 2 * hh]
            ok_ref[:, b:b + hh] = jnp.where(keep, (d1 * cos + d2 * sin) * kscale, 0.0).astype(_MXU)
            ok_ref[:, b + hh:b + 2 * hh] = jnp.where(keep, (d2 * cos - d1 * sin) * kscale, 0.0).astype(_MXU)

    return pl.pallas_call(
        body, name="ret_rope_bwd",
        out_shape=(jax.ShapeDtypeStruct((L, _RET_QKW), _MXU), jax.ShapeDtypeStruct((L, _RET_QKW), _MXU)),
        grid=(L // ROW_TILE,),
        in_specs=[_rows(_RET_QKW), _rows(_RET_QKW), _rows(hh), _rows(hh)],
        out_specs=(_rows(_RET_QKW), _rows(_RET_QKW)),
        compiler_params=_params(("parallel",)),
    )(dqr, dkr, cos, sin)


def _ret_decays():
    lg = np.log1p(-np.exp2(-5.0 - np.arange(RET_HEADS, dtype=np.float32))).astype(np.float32)
    i = np.arange(CHUNK, dtype=np.float32)
    rel = i[:, None] - i[None, :]
    dm = np.where(rel[None] >= 0, np.exp(rel[None] * lg[:, None, None]), 0.0).astype(np.float32)
    qd = np.exp((i[None, :] + 1.0) * lg[:, None]).astype(np.float32)[:, :, None]
    kd = np.exp((CHUNK - 1.0 - i)[None, :] * lg[:, None]).astype(np.float32)[:, :, None]
    cd = np.broadcast_to(np.exp(CHUNK * lg).astype(np.float32)[:, None, None], (RET_HEADS, 1, RET_V))
    return jnp.asarray(dm), jnp.asarray(qd), jnp.asarray(kd), jnp.asarray(np.ascontiguousarray(cd))


def _ret_fwd(qr, kr, v):
    L = qr.shape[0]
    nc = L // CHUNK
    G = ROW_TILE // CHUNK
    dm, qd, kd, cd = _ret_decays()
    dn_nt = (((1,), (1,)), ((), ()))
    dn_tn = (((0,), (0,)), ((), ()))

    def body(q_ref, k_ref, v_ref, dm_ref, qd_ref, kd_ref, cd_ref, o_ref, st_ref, s_ref):
        @pl.when(pl.program_id(1) == 0)
        def _():
            s_ref[...] = jnp.zeros_like(s_ref)

        state = s_ref[...]
        for g in range(G):
            rows = slice(g * CHUNK, (g + 1) * CHUNK)
            q, k, vv = q_ref[rows, :], k_ref[rows, :], v_ref[rows, :]
            sb = state.astype(_MXU)
            st_ref[g] = sb
            s = lax.dot_general(q, k.astype(_MXU), dn_nt, preferred_element_type=F32) * dm_ref[...]
            intra = jnp.dot(s.astype(_MXU), vv, preferred_element_type=F32)
            cross = jnp.dot(q, sb, preferred_element_type=F32) * qd_ref[...]
            o_ref[rows, :] = intra + cross
            kdm = (k * kd_ref[...]).astype(_MXU)
            state = cd_ref[...] * state + lax.dot_general(kdm, vv, dn_tn, preferred_element_type=F32)
        s_ref[...] = state

    return pl.pallas_call(
        body, name="ret_fwd",
        out_shape=(jax.ShapeDtypeStruct((L, RET_HEADS * RET_V), F32),
                   jax.ShapeDtypeStruct((RET_HEADS, nc, RET_QK, RET_V), _MXU)),
        grid=(RET_HEADS, nc // G),
        in_specs=[pl.BlockSpec((ROW_TILE, RET_QK), lambda h, c: (c, h)),
                  pl.BlockSpec((ROW_TILE, RET_QK), lambda h, c: (c, h)),
                  pl.BlockSpec((ROW_TILE, RET_V), lambda h, c: (c, h)),
                  pl.BlockSpec((None, CHUNK, CHUNK), lambda h, c: (h, 0, 0)),
                  pl.BlockSpec((None, CHUNK, 1), lambda h, c: (h, 0, 0)),
                  pl.BlockSpec((None, CHUNK, 1), lambda h, c: (h, 0, 0)),
                  pl.BlockSpec((None, 1, RET_V), lambda h, c: (h, 0, 0))],
        out_specs=(pl.BlockSpec((ROW_TILE, RET_V), lambda h, c: (c, h)),
                   pl.BlockSpec((None, G, RET_QK, RET_V), lambda h, c: (h, c, 0, 0))),
        scratch_shapes=[pltpu.VMEM((RET_QK, RET_V), F32)],
        compiler_params=_params(("parallel", "arbitrary")),
    )(qr, kr, v, dm, qd, kd, cd)


def _ret_bwd(qr, kr, v, do, states):
    L = qr.shape[0]
    nc = L // CHUNK
    G = ROW_TILE // CHUNK
    nt = nc // G
    dm, qd, kd, cd = _ret_decays()
    dn_nt = (((1,), (1,)), ((), ()))
    dn_tn = (((0,), (0,)), ((), ()))

    def body(q_ref, k_ref, v_ref, do_ref, st_ref, dm_ref, qd_ref, kd_ref, cd_ref,
             dq_ref, dk_ref, dv_ref, ds_ref):
        @pl.when(pl.program_id(1) == 0)
        def _():
            ds_ref[...] = jnp.zeros_like(ds_ref)

        dstate = ds_ref[...]
        dmat = dm_ref[...]
        for g in reversed(range(G)):
            rows = slice(g * CHUNK, (g + 1) * CHUNK)
            q, k, vv, do = q_ref[rows, :], k_ref[rows, :], v_ref[rows, :], do_ref[rows, :]
            kb = k.astype(_MXU)
            dob = do.astype(_MXU)
            doq = (do * qd_ref[...]).astype(_MXU)
            dsb = dstate.astype(_MXU)
            s = (lax.dot_general(q, kb, dn_nt, preferred_element_type=F32) * dmat).astype(_MXU)
            draw = (lax.dot_general(dob, vv, dn_nt, preferred_element_type=F32) * dmat).astype(_MXU)
            kdm = (k * kd_ref[...]).astype(_MXU)
            dv = (lax.dot_general(s, dob, dn_tn, preferred_element_type=F32)
                  + jnp.dot(kdm, dsb, preferred_element_type=F32))
            dq = (jnp.dot(draw, kb, preferred_element_type=F32)
                  + lax.dot_general(doq, st_ref[g], dn_nt, preferred_element_type=F32))
            dk = (lax.dot_general(draw, q, dn_tn, preferred_element_type=F32)
                  + lax.dot_general(vv, dsb, dn_nt, preferred_element_type=F32) * kd_ref[...])
            dq_ref[rows, :] = dq
            dk_ref[rows, :] = dk
            dv_ref[rows, :] = dv.astype(_MXU)
            dstate = cd_ref[...] * dstate + lax.dot_general(q, doq, dn_tn, preferred_element_type=F32)
        ds_ref[...] = dstate

    rev = lambda h, c: (nt - 1 - c, h)
    return pl.pallas_call(
        body, name="ret_bwd",
        out_shape=(jax.ShapeDtypeStruct((L, _RET_QKW), F32), jax.ShapeDtypeStruct((L, _RET_QKW), F32),
                   jax.ShapeDtypeStruct((L, RET_HEADS * RET_V), _MXU)),
        grid=(RET_HEADS, nt),
        in_specs=[pl.BlockSpec((ROW_TILE, RET_QK), rev),
                  pl.BlockSpec((ROW_TILE, RET_QK), rev),
                  pl.BlockSpec((ROW_TILE, RET_V), rev),
                  pl.BlockSpec((ROW_TILE, RET_V), rev),
                  pl.BlockSpec((None, G, RET_QK, RET_V), lambda h, c: (h, nt - 1 - c, 0, 0)),
                  pl.BlockSpec((None, CHUNK, CHUNK), lambda h, c: (h, 0, 0)),
                  pl.BlockSpec((None, CHUNK, 1), lambda h, c: (h, 0, 0)),
                  pl.BlockSpec((None, CHUNK, 1), lambda h, c: (h, 0, 0)),
                  pl.BlockSpec((None, 1, RET_V), lambda h, c: (h, 0, 0))],
        out_specs=(pl.BlockSpec((ROW_TILE, RET_QK), rev), pl.BlockSpec((ROW_TILE, RET_QK), rev),
                   pl.BlockSpec((ROW_TILE, RET_V), rev)),
        scratch_shapes=[pltpu.VMEM((RET_QK, RET_V), F32)],
        compiler_params=_params(("parallel", "arbitrary")),
    )(qr, kr, v, do, states, dm, qd, kd, cd)


def _gn(o, g):
    mu = jnp.mean(o, axis=-1, keepdims=True)
    oc = o - mu
    var = jnp.mean(oc * oc, axis=-1, keepdims=True)
    return oc * lax.rsqrt(var + NORM_EPS) * g


def _ret_gate_fwd(o, qkz, gn_g):
    L, W = o.shape
    zblk = qkz.shape[1] // W - 1

    def body(o_ref, z_ref, g_ref, y_ref):
        for h in range(RET_HEADS):
            sl = slice(h * RET_V, (h + 1) * RET_V)
            y_ref[:, sl] = (_gn(o_ref[:, sl], g_ref[:, sl]) * _silu(z_ref[:, sl])).astype(_MXU)

    return pl.pallas_call(
        body, name="ret_gate_fwd",
        out_shape=jax.ShapeDtypeStruct((L, W), _MXU),
        grid=(L // ROW_TILE,),
        in_specs=[_rows(W), pl.BlockSpec((ROW_TILE, W), lambda i: (i, zblk)), _whole((1, W))],
        out_specs=_rows(W),
        compiler_params=_params(("parallel",)),
    )(o, qkz, gn_g)


def _ret_gate_bwd(dy, o, qkz, gn_g):
    L, W = o.shape
    zblk = qkz.shape[1] // W - 1

    def body(dy_ref, o_ref, z_ref, g_ref, do_ref, dz_ref, dg_ref):
        @pl.when(pl.program_id(0) == 0)
        def _():
            dg_ref[...] = jnp.zeros_like(dg_ref)

        for h in range(RET_HEADS):
            sl = slice(h * RET_V, (h + 1) * RET_V)
            z = z_ref[:, sl]
            o = o_ref[:, sl]
            g = g_ref[:, sl]
            dy = dy_ref[:, sl]
            mu = jnp.mean(o, axis=-1, keepdims=True)
            oc = o - mu
            var = jnp.mean(oc * oc, axis=-1, keepdims=True)
            r = lax.rsqrt(var + NORM_EPS)
            xh = oc * r
            dn = dy * _silu(z)
            dz_ref[:, sl] = (dy * (xh * g) * _dsilu(z)).astype(_MXU)
            dg_ref[:, sl] += jnp.sum(dn * xh, axis=0, keepdims=True)
            dxh = dn * g
            do_ref[:, sl] = r * (dxh - jnp.mean(dxh, axis=-1, keepdims=True)
                                 - xh * jnp.mean(dxh * xh, axis=-1, keepdims=True))

    return pl.pallas_call(
        body, name="ret_gate_bwd",
        out_shape=(jax.ShapeDtypeStruct((L, W), F32), jax.ShapeDtypeStruct((L, W), _MXU),
                   jax.ShapeDtypeStruct((1, W), F32)),
        grid=(L // ROW_TILE,),
        in_specs=[_rows(W), _rows(W), pl.BlockSpec((ROW_TILE, W), lambda i: (i, zblk)), _whole((1, W))],
        out_specs=(_rows(W), _rows(W), _whole((1, W))),
        compiler_params=_params(("arbitrary",)),
    )(dy, o, qkz, gn_g)


def _dead_rows(pad):
    return pad // LANES * LANES


def _heads_t(a):
    return a[:, :FOX_HEADS].T


def _fox_fwd(h, w, lnw, pad, tag, side=None):
    w_in, b_f, w_out = w
    L = h.shape[0]
    W = D_MODEL
    w_qkv = w_in[:, :3 * W]
    w_zf = jnp.concatenate([w_in[:, 3 * W:4 * W], jnp.pad(w_in[:, 4 * W:], ((0, 0), (0, LANES - FOX_HEADS)))], axis=1)
    bias = jnp.pad(b_f, (0, LANES - FOX_HEADS))[None, :]
    qkv = _matmul(h, w_qkv, "nn", _MXU, f"{tag}_qkv")
    zf = _matmul(h, w_zf, "nn", F32, f"{tag}_zf")
    cum = _fox_cum(zf, bias, pad, f"{tag}_cum")
    ab = _bias_terms(_heads_t(cum), pad)
    nb = W // LANES
    o, lse, side_out = _attn_fwd(qkv, 0, qkv, nb, qkv, 2 * nb, ab, FOX_HEADS, LANES, LANES, LANES ** -0.5,
                                 f"{tag}_attn_fwd", _dead_rows(pad), side)
    y = _gate_fwd(o, zf, f"{tag}_gate_fwd")
    mo = _matmul(y, w_out, "nn", F32, f"{tag}_out")
    zl, hn = _ln_fwd(h, mo, lnw[0], lnw[1], f"{tag}_ln_fwd")
    return hn, dict(h=h, qkv=qkv, zf=zf, ab=ab, o=o, lse=lse, y=y, zl=zl,
                    w_all=jnp.concatenate([w_qkv, w_zf], axis=1), w_out=w_out, bias=bias), side_out


def _fox_bwd(s, dzl, pad, tag, side=None):
    W = D_MODEL
    nb = W // LANES
    dy = _matmul(dzl, s["w_out"], "nt", F32, f"{tag}_dy")
    d_wout = _matmul(s["y"], dzl, "tn", F32, f"{tag}_dwout")
    do, delta, dzg = _gate_bwd(dy, s["o"], s["zf"], FOX_HEADS, f"{tag}_gate_bwd")
    qkv = s["qkv"]
    dq, dk, dv, dck, dcq, side_out = _attn_bwd(qkv, 0, qkv, nb, qkv, 2 * nb, do, s["ab"], s["lse"], delta,
                                               FOX_HEADS, LANES, LANES, LANES ** -0.5, f"{tag}_attn_bwd",
                                               _dead_rows(pad), side)
    dcum = jnp.pad((dck + dcq).reshape(FOX_HEADS, -1).T, ((0, 0), (0, LANES - FOX_HEADS)))
    df, dbias = _fox_cum_bwd(dcum, s["zf"], s["bias"], pad, f"{tag}_cum_bwd")
    dproj = jnp.concatenate([dq.astype(_MXU), dk.astype(_MXU), dv.astype(_MXU), dzg, df], axis=1)
    d_wall = _matmul(s["h"], dproj, "tn", F32, f"{tag}_dwin")
    dh = _matmul(dproj, s["w_all"], "nt", F32, f"{tag}_dh")
    d_win = jnp.concatenate([d_wall[:, :4 * W], d_wall[:, 4 * W:4 * W + FOX_HEADS]], axis=1)
    return dh, (d_win, dbias[0, :FOX_HEADS], d_wout), side_out


def _mla_tables(L, pad):
    pos = (jnp.arange(L) - pad).astype(F32)
    inv = ROPE_BASE ** (-jnp.arange(0, MLA_ROPE, 2, dtype=F32) / MLA_ROPE)
    ang = pos[:, None] * inv[None, :]
    c, s = jnp.cos(ang), jnp.sin(ang)
    z32, z64 = jnp.zeros_like(s), jnp.zeros((L, LANES - MLA_ROPE), F32)
    cos = jnp.concatenate([c, c, jnp.ones_like(z64)], axis=1)
    sa = jnp.concatenate([-s, z32, z64], axis=1)
    sb = jnp.concatenate([z32, s, z64], axis=1)
    return cos, sa, sb


def _mla_weights(w_in, w_uq, w_ukv):
    a, b = MLA_Q_LORA, MLA_Q_LORA + MLA_KV_LORA
    w_zc = jnp.concatenate([w_in[:, b + MLA_ROPE:], w_in[:, :b + MLA_ROPE],
                            jnp.zeros((D_MODEL, LANES - MLA_ROPE), w_in.dtype)], axis=1)
    uq = w_uq.reshape(MLA_Q_LORA, MLA_HEADS, MLA_NOPE + MLA_ROPE)
    uq = jnp.pad(uq, ((0, 0), (0, 0), (0, MLA_DK - MLA_NOPE - MLA_ROPE))).reshape(MLA_Q_LORA, MLA_HEADS * MLA_DK)
    ukv = w_ukv.reshape(MLA_KV_LORA, MLA_HEADS, MLA_NOPE + MLA_V)
    ukv = jnp.concatenate([ukv[:, :, :MLA_NOPE].reshape(MLA_KV_LORA, -1), ukv[:, :, MLA_NOPE:].reshape(MLA_KV_LORA, -1)], axis=1)
    return w_zc, uq, ukv


def _mla_fwd(h, w, lnw, pad):
    w_in, gq, gkv, w_uq, w_ukv, w_out = w
    L = h.shape[0]
    w_zc, uq, ukv = _mla_weights(w_in, w_uq, w_ukv)
    cos, sa, sb = _mla_tables(L, pad)
    proj = _matmul(h, w_zc, "nn", F32, "mla_proj")
    cqn, ckvn = _mla_norm_fwd(proj, gq[None, :], gkv[None, :])
    qf = _matmul(cqn, uq, "nn", F32, "mla_uq")
    kv = _matmul(ckvn, ukv, "nn", _MXU, "mla_ukv")
    q_full, k_full = _mla_rope_fwd(qf, kv, proj, cos, sa, sb)
    ab = _bias_terms(jnp.zeros((MLA_HEADS, L), F32), pad)
    nb = MLA_HEADS * MLA_NOPE // LANES
    o, lse, _ = _attn_fwd(q_full, 0, k_full, 0, kv, nb, ab, MLA_HEADS, MLA_DK, MLA_V,
                          (MLA_NOPE + MLA_ROPE) ** -0.5, "mla_attn_fwd", _dead_rows(pad))
    y = _gate_fwd(o, proj, "mla_gate_fwd")
    mo = _matmul(y, w_out, "nn", F32, "mla_out")
    zl, hn = _ln_fwd(h, mo, lnw[0], lnw[1], "mla_ln_fwd")
    return hn, dict(h=h, proj=proj, cqn=cqn, ckvn=ckvn, kv=kv, q_full=q_full, k_full=k_full, ab=ab,
                    dead=_dead_rows(pad),
                    o=o, lse=lse, y=y, zl=zl, w_zc=w_zc, uq=uq, ukv=ukv, w_out=w_out, gq=gq, gkv=gkv,
                    tabs=(cos, sa, sb))


def _mla_bwd(s, dzl):
    H = MLA_HEADS
    dy = _matmul(dzl, s["w_out"], "nt", F32, "mla_dy")
    d_wout = _matmul(s["y"], dzl, "tn", F32, "mla_dwout")
    do, delta, dzg = _gate_bwd(dy, s["o"], s["proj"], H, "mla_gate_bwd")
    nb = H * MLA_NOPE // LANES
    dq_full, dk_full, dv, _, _, _ = _attn_bwd(s["q_full"], 0, s["k_full"], 0, s["kv"], nb, do, s["ab"],
                                              s["lse"], delta,
                                              H, MLA_DK, MLA_V, (MLA_NOPE + MLA_ROPE) ** -0.5, "mla_attn_bwd",
                                              s["dead"])
    cos, sa, sb = s["tabs"]
    dqf, dkn, dkr = _mla_rope_bwd(dq_full, dk_full, cos, -sa, -sb)
    d_uq = _matmul(s["cqn"], dqf, "tn", F32, "mla_duq")
    dcqn = _matmul(dqf, s["uq"], "nt", F32, "mla_dcqn")
    dkv = jnp.concatenate([dkn, dv.astype(_MXU)], axis=1)
    d_ukv = _matmul(s["ckvn"], dkv, "tn", F32, "mla_dukv")
    dckvn = _matmul(dkv, s["ukv"], "nt", F32, "mla_dckvn")
    dcq, dckv, dgq, dgkv = _mla_norm_bwd(s["proj"], s["gq"][None, :], s["gkv"][None, :], dcqn, dckvn)
    dproj = jnp.concatenate([dzg, dcq, dckv, dkr], axis=1)
    d_wzc = _matmul(s["h"], dproj, "tn", F32, "mla_dwin")
    dh = _matmul(dproj, s["w_zc"], "nt", F32, "mla_dh")
    W = D_MODEL
    d_win = jnp.concatenate([d_wzc[:, W:W + MLA_Q_LORA + MLA_KV_LORA + MLA_ROPE], d_wzc[:, :W]], axis=1)
    d_wuq = d_uq.reshape(MLA_Q_LORA, H, MLA_DK)[:, :, :MLA_NOPE + MLA_ROPE].reshape(MLA_Q_LORA, -1)
    hk = H * MLA_NOPE
    d_wukv = jnp.concatenate([d_ukv[:, :hk].reshape(MLA_KV_LORA, H, MLA_NOPE),
                              d_ukv[:, hk:].reshape(MLA_KV_LORA, H, MLA_V)], axis=2).reshape(MLA_KV_LORA, -1)
    return dh, (d_win, dgq[0], dgkv[0], d_wuq, d_wukv, d_wout)


def _ret_tables(L, pad):
    pos = (jnp.arange(L) - pad).astype(F32)
    inv = 1.0 / (ROPE_BASE ** jnp.linspace(0.0, 1.0, RET_QK // 2, dtype=F32))
    ang = pos[:, None] * inv[None, :]
    return jnp.cos(ang), jnp.sin(ang)


def _ret_layer_fwd(h, w, lnw, pad):
    w_in, gn_g, w_out = w
    L = h.shape[0]
    q1 = 2 * _RET_QKW
    v1 = q1 + RET_HEADS * RET_V
    w_qkz = jnp.concatenate([w_in[:, :q1], w_in[:, v1:]], axis=1)
    cos, sin = _ret_tables(L, pad)
    qkz = _matmul(h, w_qkz, "nn", F32, "ret_qkz")
    v = _matmul(h, w_in[:, q1:v1], "nn", _MXU, "ret_v")
    qr, kr = _ret_rope_fwd(qkz, cos, sin, pad)
    o, states = _ret_fwd(qr, kr, v)
    y = _ret_gate_fwd(o, qkz, gn_g[None, :])
    mo = _matmul(y, w_out, "nn", F32, "ret_out")
    zl, hn = _ln_fwd(h, mo, lnw[0], lnw[1], "ret_ln_fwd")
    return hn, dict(h=h, qkz=qkz, v=v, qr=qr, kr=kr, o=o, states=states, y=y, zl=zl, w_in=w_in, w_out=w_out,
                    gn_g=gn_g, tabs=(cos, sin))


def _ret_layer_bwd(s, dzl, pad):
    dy = _matmul(dzl, s["w_out"], "nt", F32, "ret_dy")
    d_wout = _matmul(s["y"], dzl, "tn", F32, "ret_dwout")
    do, dzg, dgn = _ret_gate_bwd(dy, s["o"], s["qkz"], s["gn_g"][None, :])
    dqr, dkr, dv = _ret_bwd(s["qr"], s["kr"], s["v"], do, s["states"])
    cos, sin = s["tabs"]
    dq, dk = _ret_rope_bwd(dqr, dkr, cos, sin, pad)
    dproj = jnp.concatenate([dq, dk, dv, dzg], axis=1)
    d_win = _matmul(s["h"], dproj, "tn", F32, "ret_dwin")
    dh = _matmul(dproj, s["w_in"], "nt", F32, "ret_dh")
    return dh, (d_win, dgn[0], d_wout)


def _local_step(x, target, meta, fox0_w, late_shards, small, ln_g, ln_b):
    n_real = x.shape[0]
    first = -(-N_META // ROW_TILE)
    pad = first * ROW_TILE - N_META
    h0 = jnp.concatenate([jnp.zeros((pad, D_MODEL), F32), meta, x], axis=0)
    lnw = [(ln_g[i][None, :], ln_b[i][None, :]) for i in range(4)]
    b_f = small["fox_b_f"]

    h1, s0, gathered = _fox_fwd(h0, (fox0_w[0], b_f[0], fox0_w[1]), lnw[0], pad, "fox0", (late_shards, False))
    w = {key: _merge_layer(key[0], g).astype(_MXU) for key, g in zip(_LATE, gathered)}
    h2, s1 = _mla_fwd(h1, (w["mla_w_in", 0], small["mla_q_norm"][0], small["mla_kv_norm"][0], w["mla_w_uq", 0],
                           w["mla_w_ukv", 0], w["mla_w_out", 0]), lnw[1], pad)
    h3, s2 = _ret_layer_fwd(h2, (w["ret_w_in", 0], small["ret_gn_g"][0], w["ret_w_out", 0]), lnw[2], pad)
    h4, s3, _ = _fox_fwd(h3, (w["fox_w_in", 1], b_f[1], w["fox_w_out", 1]), lnw[3], pad, "fox1")

    dh4, loss = _loss_head(h4, target, first)
    dz3, dg3, db3 = _ln_bwd(s3["zl"], lnw[3][0], dh4, None, "ln3_bwd")
    dm3, gf1, _ = _fox_bwd(s3, dz3, pad, "fox1")
    dz2, dg2, db2 = _ln_bwd(s2["zl"], lnw[2][0], dz3, dm3, "ln2_bwd")
    dm2, gr = _ret_layer_bwd(s2, dz2, pad)
    dz1, dg1, db1 = _ln_bwd(s1["zl"], lnw[1][0], dz2, dm2, "ln1_bwd")
    dm1, gm = _mla_bwd(s1, dz1)
    dz0, dg0, db0 = _ln_bwd(s0["zl"], lnw[0][0], dz1, dm1, "ln0_bwd")
    late = {("fox_w_in", 1): gf1[0], ("fox_w_out", 1): gf1[2], ("mla_w_in", 0): gm[0], ("mla_w_uq", 0): gm[3],
            ("mla_w_ukv", 0): gm[4], ("mla_w_out", 0): gm[5], ("ret_w_in", 0): gr[0], ("ret_w_out", 0): gr[2]}
    dm0, gf0, late_parts = _fox_bwd(s0, dz0, pad, "fox0", ([_split_layer(k[0], late[k]) for k in _LATE], True))
    grad_x, grad_meta = _input_grads(dz0, dm0, first, n_real)

    small_grads = dict(
        meta=grad_meta, fox_b_f=jnp.stack([gf0[1], gf1[1]]), mla_q_norm=gm[1][None], mla_kv_norm=gm[2][None],
        ret_gn_g=gr[1][None], ln_g=jnp.concatenate([dg0, dg1, dg2, dg3], axis=0),
        ln_b=jnp.concatenate([db0, db1, db2, db3], axis=0))
    return loss, grad_x, (gf0[0], gf0[2]), late_parts, small_grads


def _exchange(tensors, scatter, name):
    nt = len(tensors)

    def body(*refs):
        copies = _exchange_copies(refs[:nt], refs[nt:2 * nt], *refs[2 * nt:], scatter)
        for cp in copies:
            cp.start()
        for cp in copies:
            cp.wait()

    hbm = pl.BlockSpec(memory_space=pl.ANY)
    out_shape, sems = _exchange_shapes(tensors, scatter)
    return pl.pallas_call(
        body, name=name, out_shape=out_shape, in_specs=[hbm] * nt, out_specs=(hbm,) * nt, scratch_shapes=sems,
    )(*tensors)


def _adamw(parts, w, m, v, name):
    R, C = w.shape
    tr = next(t for t in range(min(R, 256), 0, -1) if R % t == 0 and (t % 8 == 0 or t == R))

    def body(p_ref, w_ref, m_ref, v_ref, g_ref, d_ref, mo_ref, vo_ref):
        g = p_ref[0]
        for d in range(1, N_DEV):
            g = g + p_ref[d]
        mn = ADAM_B1 * m_ref[...] + (1.0 - ADAM_B1) * g
        vn = ADAM_B2 * v_ref[...] + (1.0 - ADAM_B2) * (g * g)
        m_hat = mn / (1.0 - ADAM_B1 ** ADAM_STEP)
        v_hat = vn / (1.0 - ADAM_B2 ** ADAM_STEP)
        g_ref[...] = g
        d_ref[...] = -ADAM_LR * (m_hat / (jnp.sqrt(v_hat) + ADAM_EPS) + ADAM_WD * w_ref[...])
        mo_ref[...] = mn
        vo_ref[...] = vn

    blk = pl.BlockSpec((tr, C), lambda i: (i, 0))
    sd = jax.ShapeDtypeStruct((R, C), F32)
    return pl.pallas_call(
        body, name=name,
        out_shape=(sd, sd, sd, sd),
        grid=(R // tr,),
        in_specs=[pl.BlockSpec((N_DEV, tr, C), lambda i: (0, i, 0)), blk, blk, blk],
        out_specs=(blk, blk, blk, blk),
        compiler_params=_params(("parallel",)),
    )(parts, w, m, v)


_SPECS = dict(
    meta=((16, 1024), 1),
    fox_w_in=((2, 1024, 4104), 2), fox_b_f=((2, 8), None), fox_w_out=((2, 1024, 1024), 1),
    mla_w_in=((1, 1024, 1728), 2), mla_q_norm=((1, 384), None), mla_kv_norm=((1, 256), None),
    mla_w_uq=((1, 384, 1536), 2), mla_w_ukv=((1, 256, 2048), 2), mla_w_out=((1, 1024, 1024), 1),
    ret_w_in=((1, 1024, 6144), 2), ret_gn_g=((1, 2048), 1), ret_w_out=((1, 2048, 1024), 1),
    ln_g=((4, 1024), None), ln_b=((4, 1024), None),
)
_NAMES = list(_SPECS)
_BIG = ["fox_w_in", "fox_w_out", "mla_w_in", "mla_w_uq", "mla_w_ukv", "mla_w_out", "ret_w_in", "ret_w_out"]
_SMALL = [n for n in _NAMES if n not in _BIG]
_SMALL_GATHERED = ["meta", "ret_gn_g"]
_PACK_ROWS = 8


def _shard_shape(name):
    shape, ax = _SPECS[name]
    if ax is None:
        return shape
    return tuple(s // N_DEV if i == ax else s for i, s in enumerate(shape))


_EARLY = [("fox_w_in", 0), ("fox_w_out", 0)]
_LATE = [("fox_w_in", 1), ("fox_w_out", 1), ("mla_w_in", 0), ("mla_w_uq", 0), ("mla_w_ukv", 0), ("mla_w_out", 0),
         ("ret_w_in", 0), ("ret_w_out", 0)]


def _split_at(full, ax):
    shape = full.shape
    parts = full.reshape(shape[:ax] + (N_DEV, shape[ax] // N_DEV) + shape[ax + 1:])
    return jnp.moveaxis(parts, ax, 0)


def _merge_at(stacked, ax):
    full = jnp.moveaxis(stacked, 0, ax)
    return full.reshape(full.shape[:ax] + (-1,) + full.shape[ax + 2:])


def _split(name, full):
    return _split_at(full, _SPECS[name][1])


def _merge(name, stacked):
    return _merge_at(stacked, _SPECS[name][1])


def _split_layer(name, full):
    return _split_at(full, _SPECS[name][1] - 1)


def _merge_layer(name, stacked):
    return _merge_at(stacked, _SPECS[name][1] - 1)


def _pad_rows(flat):
    n = flat.shape[-1]
    unit = _PACK_ROWS * LANES
    total = -(-n // unit) * unit
    flat = jnp.pad(flat, [(0, 0)] * (flat.ndim - 1) + [(0, total - n)])
    return flat.reshape(flat.shape[:-1] + (total // LANES, LANES))


def _pack_small(tree, names, tail):
    return _pad_rows(jnp.concatenate([tree[n].reshape(-1) for n in names] + [tail]))


def _unpack_small(buf, names):
    flat, out, at = buf.reshape(-1), {}, 0
    for n in names:
        shp = _shard_shape(n)
        out[n] = flat[at:at + math.prod(shp)].reshape(shp)
        at += math.prod(shp)
    return out, flat[at:]


def kernel(x, meta, fox_w_in, fox_b_f, fox_w_out, mla_w_in, mla_q_norm, mla_kv_norm, mla_w_uq, mla_w_ukv, mla_w_out, ret_w_in, ret_gn_g, ret_w_out, ln_g, ln_b, loss_target, m_meta, m_fox_w_in, m_fox_b_f, m_fox_w_out, m_mla_w_in, m_mla_q_norm, m_mla_kv_norm, m_mla_w_uq, m_mla_w_ukv, m_mla_w_out, m_ret_w_in, m_ret_gn_g, m_ret_w_out, m_ln_g, m_ln_b, v_meta, v_fox_w_in, v_fox_b_f, v_fox_w_out, v_mla_w_in, v_mla_q_norm, v_mla_kv_norm, v_mla_w_uq, v_mla_w_ukv, v_mla_w_out, v_ret_w_in, v_ret_gn_g, v_ret_w_out, v_ln_g, v_ln_b):
    w = dict(meta=meta, fox_w_in=fox_w_in, fox_b_f=fox_b_f, fox_w_out=fox_w_out, mla_w_in=mla_w_in,
             mla_q_norm=mla_q_norm, mla_kv_norm=mla_kv_norm, mla_w_uq=mla_w_uq, mla_w_ukv=mla_w_ukv,
             mla_w_out=mla_w_out, ret_w_in=ret_w_in, ret_gn_g=ret_gn_g, ret_w_out=ret_w_out, ln_g=ln_g, ln_b=ln_b)
    m = dict(meta=m_meta, fox_w_in=m_fox_w_in, fox_b_f=m_fox_b_f, fox_w_out=m_fox_w_out, mla_w_in=m_mla_w_in,
             mla_q_norm=m_mla_q_norm, mla_kv_norm=m_mla_kv_norm, mla_w_uq=m_mla_w_uq, mla_w_ukv=m_mla_w_ukv,
             mla_w_out=m_mla_w_out, ret_w_in=m_ret_w_in, ret_gn_g=m_ret_gn_g, ret_w_out=m_ret_w_out, ln_g=m_ln_g, ln_b=m_ln_b)
    v = dict(meta=v_meta, fox_w_in=v_fox_w_in, fox_b_f=v_fox_b_f, fox_w_out=v_fox_w_out, mla_w_in=v_mla_w_in,
             mla_q_norm=v_mla_q_norm, mla_kv_norm=v_mla_kv_norm, mla_w_uq=v_mla_w_uq, mla_w_ukv=v_mla_w_ukv,
             mla_w_out=v_mla_w_out, ret_w_in=v_ret_w_in, ret_gn_g=v_ret_gn_g, ret_w_out=v_ret_w_out, ln_g=v_ln_g, ln_b=v_ln_b)

    def shard16(key):
        return w[key[0]][key[1]].astype(jnp.bfloat16)

    none = jnp.zeros((0,), F32)
    gathered = _exchange([shard16(k) for k in _EARLY] + [_pack_small(w, _SMALL_GATHERED, none)],
                         False, "weights_all_gather")
    fox0_w = [_merge_layer(k[0], g).astype(_MXU) for k, g in zip(_EARLY, gathered)]
    small = {n: w[n] for n in _SMALL}
    pieces = [_unpack_small(gathered[-1][d], _SMALL_GATHERED)[0] for d in range(N_DEV)]
    for n in _SMALL_GATHERED:
        small[n] = _merge(n, jnp.stack([p[n] for p in pieces]))

    loss, grad_x, early_grads, late_parts, grads = _local_step(
        x[0], loss_target[0], small["meta"], fox0_w, [shard16(k) for k in _LATE], small, ln_g, ln_b)

    def small_rows(n):
        if _SPECS[n][1] is None:
            return jnp.broadcast_to(grads[n].reshape(1, -1), (N_DEV, grads[n].size))
        return _split(n, grads[n]).reshape(N_DEV, -1)

    small_out = _pad_rows(jnp.concatenate([small_rows(n) for n in _SMALL]
                                          + [jnp.broadcast_to(loss.reshape(1, 1), (N_DEV, 1))], axis=1))
    parts = _exchange([_split_layer(k[0], g) for k, g in zip(_EARLY, early_grads)] + [small_out],
                      True, "grads_all_to_all")
    by_key = dict(zip(_EARLY + _LATE, list(parts[:-1]) + list(late_parts)))

    def rows2d(a):
        return a.reshape(-1, a.shape[-1])

    out = {}
    for n in _BIG:
        layers = [by_key[n, i] for i in range(_SPECS[n][0][0])]
        p = layers[0] if len(layers) == 1 else jnp.stack(layers, axis=1)
        res = _adamw(p.reshape(N_DEV, -1, p.shape[-1]), rows2d(w[n]), rows2d(m[n]), rows2d(v[n]), f"adamw_{n}")
        out[n] = [r.reshape(w[n].shape) for r in res]
    zero1 = jnp.zeros((1,), F32)
    res = _adamw(parts[-1], _pack_small(w, _SMALL, zero1), _pack_small(m, _SMALL, zero1),
                 _pack_small(v, _SMALL, zero1), "adamw_small")
    small_res = [_unpack_small(r, _SMALL) for r in res]
    for n in _SMALL:
        out[n] = [s[0][n] for s in small_res]
    total_loss = small_res[0][1][0]
    return (total_loss, grad_x[None], *[out[n][0] for n in _NAMES], *[out[n][1] for n in _NAMES],
            *[out[n][2] for n in _NAMES], *[out[n][3] for n in _NAMES])
```

```python
import functools
import math

import numpy as np
import jax
import jax.numpy as jnp
from jax import lax
from jax.experimental import pallas as pl
from jax.experimental.pallas import tpu as pltpu

F32 = jnp.float32
_MXU = jnp.bfloat16

N_DEV = 8
N_META = 16
D_MODEL = 1024
ROW_TILE = 512
LANES = 128
CHUNK = 128
NEG = -1e30
VMEM_LIMIT = 60 * 1024 * 1024

FOX_HEADS = 8
MLA_HEADS = 8
MLA_NOPE, MLA_ROPE, MLA_V = 128, 64, 128
MLA_Q_LORA, MLA_KV_LORA = 384, 256
MLA_DK = 256
RET_HEADS, RET_QK, RET_V = 4, 256, 512
ROPE_BASE = 10000.0
ALPHA = (2 * 4) ** 0.25
NORM_EPS = 1e-5

ADAM_LR, ADAM_B1, ADAM_B2, ADAM_EPS, ADAM_WD, ADAM_STEP = 0.001, 0.9, 0.999, 1e-08, 0.01, 10


def _params(sem, vmem=VMEM_LIMIT):
    return pltpu.CompilerParams(dimension_semantics=sem, vmem_limit_bytes=vmem)


def _tile(n, pref):
    if n <= pref:
        return n
    t = (pref // LANES) * LANES
    while n % t:
        t -= LANES
    return t


_MM_TILES = {"nn": (512, 4224, 2048), "nt": (512, 1024, 4224), "tn": (1024, 2048, 1536)}


def _matmul(a, b, mode, out_dtype, name):
    if mode == "nn":
        (M, K), (K2, N) = a.shape, b.shape
    elif mode == "nt":
        (M, K), (N, K2) = a.shape, b.shape
    else:
        (K, M), (K2, N) = a.shape, b.shape
    assert K == K2, (a.shape, b.shape, mode)
    tm, tn, tk = (_tile(d, p) for d, p in zip((M, N, K), _MM_TILES[mode]))
    nk = K // tk
    if mode == "nn":
        a_spec = pl.BlockSpec((tm, tk), lambda i, j, k: (i, k))
        b_spec = pl.BlockSpec((tk, tn), lambda i, j, k: (k, j))
        dn = (((1,), (0,)), ((), ()))
    elif mode == "nt":
        a_spec = pl.BlockSpec((tm, tk), lambda i, j, k: (i, k))
        b_spec = pl.BlockSpec((tn, tk), lambda i, j, k: (j, k))
        dn = (((1,), (1,)), ((), ()))
    else:
        a_spec = pl.BlockSpec((tk, tm), lambda i, j, k: (k, i))
        b_spec = pl.BlockSpec((tk, tn), lambda i, j, k: (k, j))
        dn = (((0,), (0,)), ((), ()))

    def body(a_ref, b_ref, o_ref, *acc):
        part = lax.dot_general(a_ref[...].astype(_MXU), b_ref[...].astype(_MXU), dn,
                               preferred_element_type=F32)
        if nk == 1:
            o_ref[...] = part.astype(out_dtype)
            return
        acc_ref, = acc
        k = pl.program_id(2)

        @pl.when(k == 0)
        def _():
            acc_ref[...] = part

        @pl.when(k > 0)
        def _():
            acc_ref[...] += part

        @pl.when(k == nk - 1)
        def _():
            o_ref[...] = acc_ref[...].astype(out_dtype)

    return pl.pallas_call(
        body, name=name,
        out_shape=jax.ShapeDtypeStruct((M, N), out_dtype),
        grid=(M // tm, N // tn, nk),
        in_specs=[a_spec, b_spec],
        out_specs=pl.BlockSpec((tm, tn), lambda i, j, k: (i, j)),
        scratch_shapes=[pltpu.VMEM((tm, tn), F32)] if nk > 1 else [],
        compiler_params=_params(("parallel", "parallel", "arbitrary")),
    )(a, b)


def _rows(w, off=0):
    return pl.BlockSpec((ROW_TILE, w), lambda i: (i + off, 0))


def _whole(shape):
    return pl.BlockSpec(shape, lambda i: (0,) * len(shape))


def _silu(z):
    return z * jax.nn.sigmoid(z)


def _dsilu(z):
    s = jax.nn.sigmoid(z)
    return s * (1.0 + z * (1.0 - s))


def _ln_fwd(h, y, g, b, name):
    L, D = h.shape

    def body(h_ref, y_ref, g_ref, b_ref, z_ref, o_ref):
        z = ALPHA * h_ref[...] + y_ref[...]
        mu = jnp.mean(z, axis=-1, keepdims=True)
        zc = z - mu
        var = jnp.mean(zc * zc, axis=-1, keepdims=True)
        z_ref[...] = z
        o_ref[...] = zc * lax.rsqrt(var + NORM_EPS) * g_ref[...] + b_ref[...]

    return pl.pallas_call(
        body, name=name,
        out_shape=(jax.ShapeDtypeStruct((L, D), F32), jax.ShapeDtypeStruct((L, D), F32)),
        grid=(L // ROW_TILE,),
        in_specs=[_rows(D), _rows(D), _whole((1, D)), _whole((1, D))],
        out_specs=(_rows(D), _rows(D)),
        compiler_params=_params(("parallel",)),
    )(h, y, g, b)


def _ln_bwd(zl, g, ga, gb, name):
    L, D = zl.shape
    two = gb is not None

    def body(*refs):
        if two:
            z_ref, g_ref, ga_ref, gb_ref, dz_ref, dg_ref, db_ref = refs
            gout = ALPHA * ga_ref[...] + gb_ref[...]
        else:
            z_ref, g_ref, ga_ref, dz_ref, dg_ref, db_ref = refs
            gout = ga_ref[...]
        z = z_ref[...]
        mu = jnp.mean(z, axis=-1, keepdims=True)
        zc = z - mu
        var = jnp.mean(zc * zc, axis=-1, keepdims=True)
        rstd = lax.rsqrt(var + NORM_EPS)
        xhat = zc * rstd
        dxh = gout * g_ref[...]
        m1 = jnp.mean(dxh, axis=-1, keepdims=True)
        m2 = jnp.mean(dxh * xhat, axis=-1, keepdims=True)
        dz_ref[...] = rstd * (dxh - m1 - xhat * m2)

        @pl.when(pl.program_id(0) == 0)
        def _():
            dg_ref[...] = jnp.zeros_like(dg_ref)
            db_ref[...] = jnp.zeros_like(db_ref)

        dg_ref[...] += jnp.sum(gout * xhat, axis=0, keepdims=True)
        db_ref[...] += jnp.sum(gout, axis=0, keepdims=True)

    ins = [zl, g, ga] + ([gb] if two else [])
    return pl.pallas_call(
        body, name=name,
        out_shape=(jax.ShapeDtypeStruct((L, D), F32), jax.ShapeDtypeStruct((1, D), F32),
                   jax.ShapeDtypeStruct((1, D), F32)),
        grid=(L // ROW_TILE,),
        in_specs=[_rows(D), _whole((1, D)), _rows(D)] + ([_rows(D)] if two else []),
        out_specs=(_rows(D), _whole((1, D)), _whole((1, D))),
        compiler_params=_params(("arbitrary",)),
    )(*ins)


def _loss_head(h, target, first):
    L, D = h.shape

    def body(h_ref, t_ref, dh_ref, loss_ref):
        i = pl.program_id(0)

        @pl.when(i == 0)
        def _():
            loss_ref[...] = jnp.zeros_like(loss_ref)

        @pl.when(i < first)
        def _():
            dh_ref[...] = jnp.zeros_like(dh_ref)

        @pl.when(i >= first)
        def _():
            err = h_ref[...] - t_ref[...]
            dh_ref[...] = err * (1.0 / D)
            part = jnp.sum(jnp.sum(err * err, axis=-1, keepdims=True) * (1.0 / D), axis=0, keepdims=True)
            loss_ref[...] += 0.5 * part

    return pl.pallas_call(
        body, name="loss_head",
        out_shape=(jax.ShapeDtypeStruct((L, D), F32), jax.ShapeDtypeStruct((1, 1), F32)),
        grid=(L // ROW_TILE,),
        in_specs=[_rows(D), pl.BlockSpec((ROW_TILE, D), lambda i: (jnp.maximum(i - first, 0), 0))],
        out_specs=(_rows(D), _whole((1, 1))),
        compiler_params=_params(("arbitrary",)),
    )(h, target)


def _input_grads(dzl, dmix, first, n_real):
    L, D = dzl.shape

    def body(a_ref, b_ref, gx_ref, gm_ref):
        i = pl.program_id(0)
        val = ALPHA * a_ref[...] + b_ref[...]

        @pl.when(i == first - 1)
        def _():
            gm_ref[...] = val[ROW_TILE - N_META:, :]

        @pl.when(i >= first)
        def _():
            gx_ref[...] = val

    return pl.pallas_call(
        body, name="input_grads",
        out_shape=(jax.ShapeDtypeStruct((n_real, D), F32), jax.ShapeDtypeStruct((N_META, D), F32)),
        grid=(L // ROW_TILE,),
        in_specs=[_rows(D), _rows(D)],
        out_specs=(pl.BlockSpec((ROW_TILE, D), lambda i: (jnp.maximum(i - first, 0), 0)),
                   _whole((N_META, D))),
        compiler_params=_params(("arbitrary",)),
    )(dzl, dmix)


def _gate_fwd(o, zsrc, name):
    L, W = o.shape

    def body(o_ref, z_ref, y_ref):
        y_ref[...] = (o_ref[...] * _silu(z_ref[...])).astype(_MXU)

    return pl.pallas_call(
        body, name=name,
        out_shape=jax.ShapeDtypeStruct((L, W), _MXU),
        grid=(L // ROW_TILE,),
        in_specs=[_rows(W), _rows(W)],
        out_specs=_rows(W),
        compiler_params=_params(("parallel",)),
    )(o, zsrc)


def _gate_bwd(dy, o, zsrc, heads, name):
    L, W = o.shape
    hd = W // heads

    def body(dy_ref, o_ref, z_ref, do_ref, dl_ref, dz_ref):
        z = z_ref[...]
        dy = dy_ref[...]
        o = o_ref[...]
        do = dy * _silu(z)
        do_ref[...] = do.astype(_MXU)
        dz_ref[...] = (dy * o * _dsilu(z)).astype(_MXU)
        prod = do * o
        for h in range(heads):
            dl_ref[h] = _as_row(jnp.sum(prod[:, h * hd:(h + 1) * hd], axis=-1, keepdims=True))

    return pl.pallas_call(
        body, name=name,
        out_shape=(jax.ShapeDtypeStruct((L, W), _MXU), jax.ShapeDtypeStruct((heads, L // ROW_TILE, 1, ROW_TILE), F32),
                   jax.ShapeDtypeStruct((L, W), _MXU)),
        grid=(L // ROW_TILE,),
        in_specs=[_rows(W), _rows(W), _rows(W)],
        out_specs=(_rows(W), pl.BlockSpec((heads, None, 1, ROW_TILE), lambda i: (0, i, 0, 0)), _rows(W)),
        compiler_params=_params(("parallel",)),
    )(dy, o, zsrc)


def _split3(x):
    hi = x.astype(_MXU)
    r1 = x - hi.astype(F32)
    mid = r1.astype(_MXU)
    lo = (r1 - mid.astype(F32)).astype(_MXU)
    return hi, mid, lo


def _tri_cumsum(x, reverse):
    T = x.shape[0]
    r = lax.broadcasted_iota(jnp.int32, (T, T), 0)
    c = lax.broadcasted_iota(jnp.int32, (T, T), 1)
    tri = jnp.where((c >= r) if reverse else (c <= r), 1.0, 0.0).astype(_MXU)
    out = jnp.zeros(x.shape, F32)
    for part in _split3(x):
        out = out + jnp.dot(tri, part, preferred_element_type=F32)
    return out


def _fox_cum(zf, bias, pad, name):
    L, Wz = zf.shape
    off = Wz // LANES - 1

    def body(f_ref, b_ref, c_ref, carry_ref):
        i = pl.program_id(0)

        @pl.when(i == 0)
        def _():
            carry_ref[...] = jnp.zeros_like(carry_ref)

        x = f_ref[...] + b_ref[...]
        logf = jnp.minimum(x, 0.0) - jnp.log(1.0 + jnp.exp(-jnp.abs(x)))
        row = i * ROW_TILE + lax.broadcasted_iota(jnp.int32, logf.shape, 0)
        logf = jnp.where(row >= pad, logf, 0.0)
        cum = _tri_cumsum(logf, False) + carry_ref[...]
        c_ref[...] = cum
        carry_ref[...] = cum[ROW_TILE - 1:, :]

    return pl.pallas_call(
        body, name=name,
        out_shape=jax.ShapeDtypeStruct((L, LANES), F32),
        grid=(L // ROW_TILE,),
        in_specs=[pl.BlockSpec((ROW_TILE, LANES), lambda i: (i, off)), _whole((1, LANES))],
        out_specs=_rows(LANES),
        scratch_shapes=[pltpu.VMEM((1, LANES), F32)],
        compiler_params=_params(("arbitrary",)),
    )(zf, bias)


def _fox_cum_bwd(dcum, zf, bias, pad, name):
    L, Wz = zf.shape
    off = Wz // LANES - 1
    n = L // ROW_TILE

    def body(d_ref, f_ref, b_ref, df_ref, db_ref, carry_ref):
        i = pl.program_id(0)

        @pl.when(i == 0)
        def _():
            carry_ref[...] = jnp.zeros_like(carry_ref)
            db_ref[...] = jnp.zeros_like(db_ref)

        rc = _tri_cumsum(d_ref[...], True) + carry_ref[...]
        carry_ref[...] = rc[:1, :]
        x = f_ref[...] + b_ref[...]
        row = (n - 1 - i) * ROW_TILE + lax.broadcasted_iota(jnp.int32, x.shape, 0)
        df = jnp.where(row >= pad, rc * jax.nn.sigmoid(-x), 0.0)
        df_ref[...] = df.astype(_MXU)
        db_ref[...] += jnp.sum(df, axis=0, keepdims=True)

    return pl.pallas_call(
        body, name=name,
        out_shape=(jax.ShapeDtypeStruct((L, LANES), _MXU), jax.ShapeDtypeStruct((1, LANES), F32)),
        grid=(n,),
        in_specs=[pl.BlockSpec((ROW_TILE, LANES), lambda i: (n - 1 - i, 0)),
                  pl.BlockSpec((ROW_TILE, LANES), lambda i: (n - 1 - i, off)), _whole((1, LANES))],
        out_specs=(pl.BlockSpec((ROW_TILE, LANES), lambda i: (n - 1 - i, 0)), _whole((1, LANES))),
        scratch_shapes=[pltpu.VMEM((1, LANES), F32)],
        compiler_params=_params(("arbitrary",)),
    )(dcum, zf, bias)


LOG2E = 1.4426950408889634
BWD_UNROLL = 4


def _bias_terms(ct, pad):
    H, L = ct.shape
    T = ROW_TILE
    n = L // T
    c2 = ct * LOG2E
    k2 = jnp.where(jnp.arange(L)[None, :] < pad, -NEG, c2)
    return dict(cq_row=c2.reshape(H, n, 1, T), ck_row=k2.reshape(H, n, 1, T), ref=c2[:, ::T].reshape(H, n, 1, 1))


def _exchange_copies(srcs, outs, send_sems, recv_sems, local_sems, scatter):
    nt = len(srcs)
    x, y, c = lax.axis_index("x"), lax.axis_index("y"), lax.axis_index("c")
    me = 4 * x + 2 * y + c
    copies = [pltpu.make_async_copy(srcs[t].at[me] if scatter else srcs[t], outs[t].at[me], local_sems.at[t])
              for t in range(nt)]
    for k in range(1, N_DEV):
        px = (x + (k >> 2)) % 2
        py = (y + ((k >> 1) & 1)) % 2
        pc = (c + (k & 1)) % 2
        peer = 4 * px + 2 * py + pc
        for t in range(nt):
            copies.append(pltpu.make_async_remote_copy(
                src_ref=srcs[t].at[peer] if scatter else srcs[t],
                dst_ref=outs[t].at[me],
                send_sem=send_sems.at[k - 1, t], recv_sem=recv_sems.at[k - 1, t],
                device_id=(px, py, pc), device_id_type=pl.DeviceIdType.MESH))
    return copies


def _exchange_shapes(tensors, scatter):
    out_shape = tuple(jax.ShapeDtypeStruct(t.shape if scatter else (N_DEV,) + t.shape, t.dtype) for t in tensors)
    nt = len(tensors)
    sems = [pltpu.SemaphoreType.DMA((N_DEV - 1, nt)), pltpu.SemaphoreType.DMA((N_DEV - 1, nt)),
            pltpu.SemaphoreType.DMA((nt,))]
    return out_shape, sems


def _side_exchange(side, srcs, outs, sems, first, last):
    if not side:
        return

    @pl.when(first)
    def _():
        for cp in _exchange_copies(srcs, outs, *sems, side[1]):
            cp.start()

    @pl.when(last)
    def _():
        for cp in _exchange_copies(srcs, outs, *sems, side[1]):
            cp.wait()


def _as_row(col):
    return jnp.transpose(jnp.broadcast_to(col, (col.shape[0], LANES)))[0:1, :]


def _as_col(row):
    return jnp.transpose(jnp.broadcast_to(row, (LANES, row.shape[1])))[:, 0:1]


def _attn_fwd(q, qoff, k, koff, v, voff, bias, heads, dk, dv, scale, name, dead, side=None):
    L = q.shape[0]
    T = ROW_TILE
    n = L // T
    c = scale * LOG2E
    dn_qk = (((1,), (1,)), ((), ()))

    ns = len(side[0]) if side else 0

    def body(*refs):
        q_ref, k_ref, v_ref, ck_ref, ref_ref = refs[:5]
        o_ref, lse_ref = refs[5 + ns:7 + ns]
        s_a, s_b = refs[7 + 2 * ns:9 + 2 * ns]
        i = pl.program_id(1)
        _side_exchange(side, refs[5:5 + ns], refs[7 + ns:7 + 2 * ns], refs[9 + 2 * ns:],
                       (pl.program_id(0) == 0) & (i == 0), (pl.program_id(0) == heads - 1) & (i == n - 1))
        qb = q_ref[...]
        ref = ref_ref[...]

        def scores(j, dst):
            start = pl.multiple_of(j * T, T)
            dst[...] = lax.dot_general(qb, k_ref[pl.ds(start, T), :], dn_qk, preferred_element_type=F32)

        def update(t, vj, carry):
            m, l, acc = carry
            m_new = jnp.maximum(m, jnp.max(t, axis=1, keepdims=True))
            p = jnp.exp2(t - m_new)
            a = jnp.exp2(m - m_new)
            l = a * l + jnp.sum(p, axis=1, keepdims=True)
            acc = a * acc + jnp.dot(p.astype(_MXU), vj, preferred_element_type=F32)
            return m_new, l, acc

        def soft(j, carry, cur, diag):
            start = pl.multiple_of(j * T, T)
            t = cur[...] * c - (ck_ref[j] - ref)
            if diag:
                r = lax.broadcasted_iota(jnp.int32, (T, T), 0)
                cc = lax.broadcasted_iota(jnp.int32, (T, T), 1)
                t = jnp.where(cc <= r, t, NEG)
            return update(t, v_ref[pl.ds(start, T), :], carry)

        def head_tile(carry):
            t = lax.dot_general(qb, k_ref[dead:T, :], dn_qk, preferred_element_type=F32) * c
            return update(t - (ck_ref[0][:, dead:] - ref), v_ref[dead:T, :], carry)

        def step(j, carry, cur, nxt):
            scores(j + 1, nxt)
            return soft(j, carry, cur, False)

        first = jnp.where(i > 0, 1, 0) if dead else 0

        def pair(jj, carry):
            j0 = first + 2 * jj
            return step(j0 + 1, step(j0, carry, s_a, s_b), s_b, s_a)

        scores(first, s_a)
        carry = (jnp.full((T, 1), -jnp.inf, F32), jnp.zeros((T, 1), F32), jnp.zeros((T, dv), F32))
        if dead:
            carry = lax.cond(i > 0, head_tile, lambda cy: cy, carry)
        whole = i - first
        carry = lax.fori_loop(0, whole // 2, pair, carry)
        odd = (whole % 2) == 1
        carry = lax.cond(odd, lambda cy: step(i - 1, cy, s_a, s_b), lambda cy: cy, carry)
        m, l, acc = lax.cond(odd, lambda cy: soft(i, cy, s_b, True), lambda cy: soft(i, cy, s_a, True), carry)
        o_ref[...] = acc / l
        lse_ref[...] = _as_row(m + jnp.log2(l))

    hbm = pl.BlockSpec(memory_space=pl.ANY)
    side_shapes, side_sems = _exchange_shapes(*side) if side else ((), [])
    o, lse, *side_out = pl.pallas_call(
        body, name=name,
        out_shape=(jax.ShapeDtypeStruct((L, heads * dv), F32), jax.ShapeDtypeStruct((heads, n, 1, T), F32))
        + side_shapes,
        grid=(heads, n),
        in_specs=[pl.BlockSpec((T, dk), lambda h, i: (i, qoff + h)),
                  pl.BlockSpec((L, dk), lambda h, i: (0, koff + h)),
                  pl.BlockSpec((L, dv), lambda h, i: (0, voff + h)),
                  pl.BlockSpec((None, n, 1, T), lambda h, i: (h, 0, 0, 0)),
                  pl.BlockSpec((None, None, 1, 1), lambda h, i: (h, i, 0, 0))] + [hbm] * ns,
        out_specs=(pl.BlockSpec((T, dv), lambda h, i: (i, h)),
                   pl.BlockSpec((None, None, 1, T), lambda h, i: (h, i, 0, 0))) + (hbm,) * ns,
        scratch_shapes=[pltpu.VMEM((T, T), F32), pltpu.VMEM((T, T), F32)] + side_sems,
        compiler_params=_params(("arbitrary", "arbitrary")),
    )(q, k, v, bias["ck_row"], bias["ref"], *(side[0] if side else ()))
    return o, lse + (bias["cq_row"] - bias["ref"]), side_out


def _attn_bwd(q, qoff, k, koff, v, voff, do, bias, lse_row, delta_row, heads, dk, dv, scale, name, dead,
              side=None, bias_grads=True):
    L = q.shape[0]
    T = ROW_TILE
    n = L // T
    c = scale * LOG2E
    dn_nt = (((1,), (1,)), ((), ()))
    dn_tn = (((0,), (0,)), ((), ()))
    ns = len(side[0]) if side else 0

    def body(*refs):
        q_ref, k_ref, v_ref, do_ref, cq_ref, ck_ref, ref_ref, lse_ref, dl_ref = refs[:9]
        dq_hbm, dk_ref, dv_ref, dck_ref, dcq_ref = refs[9 + ns:14 + ns]
        dq_acc, sem, st_a, dp_a, st_b, dp_b = refs[14 + 2 * ns:20 + 2 * ns]
        h = pl.program_id(0)
        j = pl.program_id(1)
        _side_exchange(side, refs[9:9 + ns], refs[14 + ns:14 + 2 * ns], refs[20 + 2 * ns:],
                       (h == 0) & (j == 0), (h == heads - 1) & (j == n - 1))

        @pl.when(j == 0)
        def _():
            dq_acc[...] = jnp.zeros_like(dq_acc)
            dcq_ref[...] = jnp.zeros_like(dcq_ref)

        ref = ref_ref[...]

        def run(k0, R):
            kb = k_ref[k0:k0 + R, :]
            vb = v_ref[k0:k0 + R, :]
            bcol = _as_col(ck_ref[j][:, k0:k0 + R] - ref)

            def front(i, st_dst, dp_dst):
                start = pl.multiple_of(jnp.minimum(i, n - 1) * T, T)
                st_dst[0:R, :] = lax.dot_general(kb, q_ref[pl.ds(start, T), :], dn_nt, preferred_element_type=F32)
                dp_dst[0:R, :] = lax.dot_general(vb, do_ref[pl.ds(start, T), :], dn_nt, preferred_element_type=F32)

            def back(i, carry, st_cur, dp_cur, diag):
                dk_a, dv_a, dck_a = carry
                start = pl.multiple_of(i * T, T)
                qi = q_ref[pl.ds(start, T), :]
                doi = do_ref[pl.ds(start, T), :]
                arow = (cq_ref[i] - ref) - lse_ref[i]
                st = st_cur[0:R, :] * c + arow - bcol
                if diag:
                    r = lax.broadcasted_iota(jnp.int32, (R, T), 0) + k0
                    cc = lax.broadcasted_iota(jnp.int32, (R, T), 1)
                    st = jnp.where(r <= cc, st, NEG)
                pt = jnp.exp2(st)
                dv_a = dv_a + jnp.dot(pt.astype(_MXU), doi, preferred_element_type=F32)
                dst = pt * (dp_cur[0:R, :] - dl_ref[i])
                if bias_grads:
                    dck_a = dck_a - jnp.sum(dst, axis=1, keepdims=True)
                    dcq_ref[i] += jnp.sum(dst, axis=0, keepdims=True)
                dsb = (dst * scale).astype(_MXU)
                dk_a = dk_a + jnp.dot(dsb, qi, preferred_element_type=F32)
                dq_acc[pl.ds(start, T), :] += lax.dot_general(dsb, kb, dn_tn, preferred_element_type=F32)
                return dk_a, dv_a, dck_a

            buf_a, buf_b = (st_a, dp_a), (st_b, dp_b)

            def step(i, carry, cur, nxt, diag=False):
                front(i + 1, *nxt)
                return back(i, carry, *cur, diag)

            def group(t, carry):
                i0 = j + 1 + BWD_UNROLL * t
                for u in range(0, BWD_UNROLL, 2):
                    carry = step(i0 + u + 1, step(i0 + u, carry, buf_b, buf_a), buf_a, buf_b)
                return carry

            front(j, *buf_a)
            init = (jnp.zeros((R, dk), F32), jnp.zeros((R, dv), F32), jnp.zeros((R, 1), F32))
            carry = step(j, init, buf_a, buf_b, True)
            rest = n - 1 - j
            carry = lax.fori_loop(0, rest // BWD_UNROLL, group, carry)
            left = rest % BWD_UNROLL
            for u in range(BWD_UNROLL - 1):
                cur, nxt = (buf_b, buf_a) if u % 2 == 0 else (buf_a, buf_b)
                carry = lax.cond(left > u, lambda cy, u=u, cur=cur, nxt=nxt: step(n - left + u, cy, cur, nxt),
                                 lambda cy: cy, carry)
            dk_a, dv_a, dck_a = carry
            if k0:
                dk_ref[0:k0, :] = jnp.zeros((k0, dk), F32)
                dv_ref[0:k0, :] = jnp.zeros((k0, dv), F32)
                dck_ref[:, 0:k0] = jnp.zeros((1, k0), F32)
            dk_ref[k0:k0 + R, :] = dk_a
            dv_ref[k0:k0 + R, :] = dv_a
            dck_ref[:, k0:k0 + R] = _as_row(dck_a)

        if dead:
            pl.when(j == 0)(lambda: run(dead, T - dead))
            pl.when(j > 0)(lambda: run(0, T))
        else:
            run(0, T)

        @pl.when(j == n - 1)
        def _():
            cp = pltpu.make_async_copy(dq_acc, dq_hbm.at[:, pl.ds(pl.multiple_of(h * dk, dk), dk)], sem)
            cp.start()
            cp.wait()

    hbm = pl.BlockSpec(memory_space=pl.ANY)
    side_shapes, side_sems = _exchange_shapes(*side) if side else ((), [])
    dq, dk_, dv_, dck, dcq, *side_out = pl.pallas_call(
        body, name=name,
        out_shape=(jax.ShapeDtypeStruct((L, heads * dk), F32), jax.ShapeDtypeStruct((L, heads * dk), F32),
                   jax.ShapeDtypeStruct((L, heads * dv), F32), jax.ShapeDtypeStruct((heads, n, 1, T), F32),
                   jax.ShapeDtypeStruct((heads, n, 1, T), F32)) + side_shapes,
        grid=(heads, n),
        in_specs=[pl.BlockSpec((L, dk), lambda h, j: (0, qoff + h)),
                  pl.BlockSpec((T, dk), lambda h, j: (j, koff + h)),
                  pl.BlockSpec((T, dv), lambda h, j: (j, voff + h)),
                  pl.BlockSpec((L, dv), lambda h, j: (0, h)),
                  pl.BlockSpec((None, n, 1, T), lambda h, j: (h, 0, 0, 0)),
                  pl.BlockSpec((None, n, 1, T), lambda h, j: (h, 0, 0, 0)),
                  pl.BlockSpec((None, None, 1, 1), lambda h, j: (h, j, 0, 0)),
                  pl.BlockSpec((None, n, 1, T), lambda h, j: (h, 0, 0, 0)),
                  pl.BlockSpec((None, n, 1, T), lambda h, j: (h, 0, 0, 0))] + [hbm] * ns,
        out_specs=(hbm,
                   pl.BlockSpec((T, dk), lambda h, j: (j, h)),
                   pl.BlockSpec((T, dv), lambda h, j: (j, h)),
                   pl.BlockSpec((None, None, 1, T), lambda h, j: (h, j, 0, 0)),
                   pl.BlockSpec((None, n, 1, T), lambda h, j: (h, 0, 0, 0))) + (hbm,) * ns,
        scratch_shapes=[pltpu.VMEM((L, dk), F32), pltpu.SemaphoreType.DMA] + [pltpu.VMEM((T, T), F32)] * 4
        + side_sems,
        compiler_params=_params(("arbitrary", "arbitrary")),
    )(q, k, v, do, bias["cq_row"], bias["ck_row"], bias["ref"], lse_row, delta_row, *(side[0] if side else ()))
    return dq, dk_, dv_, dck, dcq, side_out


_CQ0, _CKV0, _KR0 = D_MODEL, D_MODEL + MLA_Q_LORA, D_MODEL + MLA_Q_LORA + MLA_KV_LORA
_MLA_PROJ = _KR0 + LANES


def _rms(x, g):
    ms = jnp.mean(x * x, axis=-1, keepdims=True)
    return x * lax.rsqrt(ms + NORM_EPS) * g


def _rms_bwd(x, g, dy):
    ms = jnp.mean(x * x, axis=-1, keepdims=True)
    r = lax.rsqrt(ms + NORM_EPS)
    xh = x * r
    dxh = dy * g
    dx = r * (dxh - xh * jnp.mean(dxh * xh, axis=-1, keepdims=True))
    return dx, jnp.sum(dy * xh, axis=0, keepdims=True)


def _mla_norm_fwd(proj, gq, gkv):
    L = proj.shape[0]

    def body(p_ref, gq_ref, gkv_ref, cq_ref, ckv_ref):
        cq_ref[...] = _rms(p_ref[:, _CQ0:_CKV0], gq_ref[...]).astype(_MXU)
        ckv_ref[...] = _rms(p_ref[:, _CKV0:_KR0], gkv_ref[...]).astype(_MXU)

    return pl.pallas_call(
        body, name="mla_norm_fwd",
        out_shape=(jax.ShapeDtypeStruct((L, MLA_Q_LORA), _MXU), jax.ShapeDtypeStruct((L, MLA_KV_LORA), _MXU)),
        grid=(L // ROW_TILE,),
        in_specs=[_rows(_MLA_PROJ), _whole((1, MLA_Q_LORA)), _whole((1, MLA_KV_LORA))],
        out_specs=(_rows(MLA_Q_LORA), _rows(MLA_KV_LORA)),
        compiler_params=_params(("parallel",)),
    )(proj, gq, gkv)


def _mla_norm_bwd(proj, gq, gkv, dcqn, dckvn):
    L = proj.shape[0]

    def body(p_ref, gq_ref, gkv_ref, dq_ref, dkv_ref, dcq_ref, dckv_ref, dgq_ref, dgkv_ref):
        @pl.when(pl.program_id(0) == 0)
        def _():
            dgq_ref[...] = jnp.zeros_like(dgq_ref)
            dgkv_ref[...] = jnp.zeros_like(dgkv_ref)

        dx, dg = _rms_bwd(p_ref[:, _CQ0:_CKV0], gq_ref[...], dq_ref[...])
        dcq_ref[...] = dx.astype(_MXU)
        dgq_ref[...] += dg
        dx, dg = _rms_bwd(p_ref[:, _CKV0:_KR0], gkv_ref[...], dkv_ref[...])
        dckv_ref[...] = dx.astype(_MXU)
        dgkv_ref[...] += dg

    return pl.pallas_call(
        body, name="mla_norm_bwd",
        out_shape=(jax.ShapeDtypeStruct((L, MLA_Q_LORA), _MXU), jax.ShapeDtypeStruct((L, MLA_KV_LORA), _MXU),
                   jax.ShapeDtypeStruct((1, MLA_Q_LORA), F32), jax.ShapeDtypeStruct((1, MLA_KV_LORA), F32)),
        grid=(L // ROW_TILE,),
        in_specs=[_rows(_MLA_PROJ), _whole((1, MLA_Q_LORA)), _whole((1, MLA_KV_LORA)),
                  _rows(MLA_Q_LORA), _rows(MLA_KV_LORA)],
        out_specs=(_rows(MLA_Q_LORA), _rows(MLA_KV_LORA), _whole((1, MLA_Q_LORA)), _whole((1, MLA_KV_LORA))),
        compiler_params=_params(("arbitrary",)),
    )(proj, gq, gkv, dcqn, dckvn)


def _rot_tile(t, cos, sa, sb):
    half = MLA_ROPE // 2
    return t * cos + pltpu.roll(t, LANES - half, 1) * sa + pltpu.roll(t, half, 1) * sb


def _mla_rope_fwd(qf, kv, proj, cos, sa, sb):
    L = qf.shape[0]
    H = MLA_HEADS

    def body(q_ref, kv_ref, p_ref, c_ref, sa_ref, sb_ref, qo_ref, ko_ref):
        cos, sa, sb = c_ref[...], sa_ref[...], sb_ref[...]
        kr = _rot_tile(p_ref[...], cos, sa, sb).astype(_MXU)
        for h in range(H):
            b = h * MLA_DK
            qo_ref[:, b:b + LANES] = q_ref[:, b:b + LANES].astype(_MXU)
            qo_ref[:, b + LANES:b + 2 * LANES] = _rot_tile(q_ref[:, b + LANES:b + 2 * LANES], cos, sa, sb).astype(_MXU)
            ko_ref[:, b:b + LANES] = kv_ref[:, h * LANES:(h + 1) * LANES]
            ko_ref[:, b + LANES:b + 2 * LANES] = kr

    W = H * MLA_DK
    return pl.pallas_call(
        body, name="mla_rope_fwd",
        out_shape=(jax.ShapeDtypeStruct((L, W), _MXU), jax.ShapeDtypeStruct((L, W), _MXU)),
        grid=(L // ROW_TILE,),
        in_specs=[_rows(W), _rows(kv.shape[1]), pl.BlockSpec((ROW_TILE, LANES), lambda i: (i, _KR0 // LANES)),
                  _rows(LANES), _rows(LANES), _rows(LANES)],
        out_specs=(_rows(W), _rows(W)),
        compiler_params=_params(("parallel",)),
    )(qf, kv, proj, cos, sa, sb)


def _mla_rope_bwd(dq_full, dk_full, cos, sa, sb):
    L = dq_full.shape[0]
    H = MLA_HEADS
    W = H * MLA_DK

    def body(dq_ref, dk_ref, c_ref, sa_ref, sb_ref, dqf_ref, dkn_ref, dkr_ref):
        cos, sa, sb = c_ref[...], sa_ref[...], sb_ref[...]
        lane = lax.broadcasted_iota(jnp.int32, (ROW_TILE, LANES), 1)
        live = lane < MLA_ROPE
        krs = jnp.zeros((ROW_TILE, LANES), F32)
        for h in range(H):
            b = h * MLA_DK
            dqf_ref[:, b:b + LANES] = dq_ref[:, b:b + LANES].astype(_MXU)
            dqr = _rot_tile(dq_ref[:, b + LANES:b + 2 * LANES], cos, sa, sb)
            dqf_ref[:, b + LANES:b + 2 * LANES] = jnp.where(live, dqr, 0.0).astype(_MXU)
            dkn_ref[:, h * LANES:(h + 1) * LANES] = dk_ref[:, b:b + LANES].astype(_MXU)
            krs = krs + dk_ref[:, b + LANES:b + 2 * LANES]
        dkr_ref[...] = jnp.where(live, _rot_tile(krs, cos, sa, sb), 0.0).astype(_MXU)

    return pl.pallas_call(
        body, name="mla_rope_bwd",
        out_shape=(jax.ShapeDtypeStruct((L, W), _MXU), jax.ShapeDtypeStruct((L, H * LANES), _MXU),
                   jax.ShapeDtypeStruct((L, LANES), _MXU)),
        grid=(L // ROW_TILE,),
        in_specs=[_rows(W), _rows(W), _rows(LANES), _rows(LANES), _rows(LANES)],
        out_specs=(_rows(W), _rows(H * LANES), _rows(LANES)),
        compiler_params=_params(("parallel",)),
    )(dq_full, dk_full, cos, sa, sb)


_RET_QKW = RET_HEADS * RET_QK


def _ret_rope_fwd(qkz, cos, sin, pad):
    L = qkz.shape[0]
    hh = RET_QK // 2
    kscale = RET_QK ** -0.5

    def body(p_ref, c_ref, s_ref, q_ref, k_ref):
        cos, sin = c_ref[...], s_ref[...]
        row = pl.program_id(0) * ROW_TILE + lax.broadcasted_iota(jnp.int32, (ROW_TILE, hh), 0)
        keep = row >= pad
        for h in range(RET_HEADS):
            b = h * RET_QK
            t1, t2 = p_ref[:, b:b + hh], p_ref[:, b + hh:b + 2 * hh]
            q_ref[:, b:b + hh] = (t1 * cos - t2 * sin).astype(_MXU)
            q_ref[:, b + hh:b + 2 * hh] = (t2 * cos + t1 * sin).astype(_MXU)
            t1, t2 = p_ref[:, _RET_QKW + b:_RET_QKW + b + hh], p_ref[:, _RET_QKW + b + hh:_RET_QKW + b + 2 * hh]
            k_ref[:, b:b + hh] = jnp.where(keep, (t1 * cos - t2 * sin) * kscale, 0.0)
            k_ref[:, b + hh:b + 2 * hh] = jnp.where(keep, (t2 * cos + t1 * sin) * kscale, 0.0)

    return pl.pallas_call(
        body, name="ret_rope_fwd",
        out_shape=(jax.ShapeDtypeStruct((L, _RET_QKW), _MXU), jax.ShapeDtypeStruct((L, _RET_QKW), F32)),
        grid=(L // ROW_TILE,),
        in_specs=[_rows(2 * _RET_QKW), _rows(hh), _rows(hh)],
        out_specs=(_rows(_RET_QKW), _rows(_RET_QKW)),
        compiler_params=_params(("parallel",)),
    )(qkz, cos, sin)


def _ret_rope_bwd(dqr, dkr, cos, sin, pad):
    L = dqr.shape[0]
    hh = RET_QK // 2
    kscale = RET_QK ** -0.5

    def body(dq_ref, dk_ref, c_ref, s_ref, oq_ref, ok_ref):
        cos, sin = c_ref[...], s_ref[...]
        row = pl.program_id(0) * ROW_TILE + lax.broadcasted_iota(jnp.int32, (ROW_TILE, hh), 0)
        keep = row >= pad
        for h in range(RET_HEADS):
            b = h * RET_QK
            d1, d2 = dq_ref[:, b:b + hh], dq_ref[:, b + hh:b + 2 * hh]
            oq_ref[:, b:b + hh] = (d1 * cos + d2 * sin).astype(_MXU)
            oq_ref[:, b + hh:b + 2 * hh] = (d2 * cos - d1 * sin).astype(_MXU)
            d1, d2 = dk_ref[:, b:b + hh], dk_ref[:, b + hh:b + 2 * hh]
            ok_ref[:, b:b + hh] = jnp.where(keep, (d1 * cos + d2 * sin) * kscale, 0.0).astype(_MXU)
            ok_ref[:, b + hh:b + 2 * hh] = jnp.where(keep, (d2 * cos - d1 * sin) * kscale, 0.0).astype(_MXU)

    return pl.pallas_call(
        body, name="ret_rope_bwd",
        out_shape=(jax.ShapeDtypeStruct((L, _RET_QKW), _MXU), jax.ShapeDtypeStruct((L, _RET_QKW), _MXU)),
        grid=(L // ROW_TILE,),
        in_specs=[_rows(_RET_QKW), _rows(_RET_QKW), _rows(hh), _rows(hh)],
        out_specs=(_rows(_RET_QKW), _rows(_RET_QKW)),
        compiler_params=_params(("parallel",)),
    )(dqr, dkr, cos, sin)


def _ret_decays():
    lg = np.log1p(-np.exp2(-5.0 - np.arange(RET_HEADS, dtype=np.float32))).astype(np.float32)
    i = np.arange(CHUNK, dtype=np.float32)
    rel = i[:, None] - i[None, :]
    dm = np.where(rel[None] >= 0, np.exp(rel[None] * lg[:, None, None]), 0.0).astype(np.float32)
    qd = np.exp((i[None, :] + 1.0) * lg[:, None]).astype(np.float32)[:, :, None]
    kd = np.exp((CHUNK - 1.0 - i)[None, :] * lg[:, None]).astype(np.float32)[:, :, None]
    cd = np.broadcast_to(np.exp(CHUNK * lg).astype(np.float32)[:, None, None], (RET_HEADS, 1, RET_V))
    return jnp.asarray(dm), jnp.asarray(qd), jnp.asarray(kd), jnp.asarray(np.ascontiguousarray(cd))


def _ret_fwd(qr, kr, v):
    L = qr.shape[0]
    nc = L // CHUNK
    G = ROW_TILE // CHUNK
    dm, qd, kd, cd = _ret_decays()
    dn_nt = (((1,), (1,)), ((), ()))
    dn_tn = (((0,), (0,)), ((), ()))

    def body(q_ref, k_ref, v_ref, dm_ref, qd_ref, kd_ref, cd_ref, o_ref, st_ref, s_ref):
        @pl.when(pl.program_id(1) == 0)
        def _():
            s_ref[...] = jnp.zeros_like(s_ref)

        state = s_ref[...]
        for g in range(G):
            rows = slice(g * CHUNK, (g + 1) * CHUNK)
            q, k, vv = q_ref[rows, :], k_ref[rows, :], v_ref[rows, :]
            sb = state.astype(_MXU)
            st_ref[g] = sb
            s = lax.dot_general(q, k.astype(_MXU), dn_nt, preferred_element_type=F32) * dm_ref[...]
            intra = jnp.dot(s.astype(_MXU), vv, preferred_element_type=F32)
            cross = jnp.dot(q, sb, preferred_element_type=F32) * qd_ref[...]
            o_ref[rows, :] = intra + cross
            kdm = (k * kd_ref[...]).astype(_MXU)
            state = cd_ref[...] * state + lax.dot_general(kdm, vv, dn_tn, preferred_element_type=F32)
        s_ref[...] = state

    return pl.pallas_call(
        body, name="ret_fwd",
        out_shape=(jax.ShapeDtypeStruct((L, RET_HEADS * RET_V), F32),
                   jax.ShapeDtypeStruct((RET_HEADS, nc, RET_QK, RET_V), _MXU)),
        grid=(RET_HEADS, nc // G),
        in_specs=[pl.BlockSpec((ROW_TILE, RET_QK), lambda h, c: (c, h)),
                  pl.BlockSpec((ROW_TILE, RET_QK), lambda h, c: (c, h)),
                  pl.BlockSpec((ROW_TILE, RET_V), lambda h, c: (c, h)),
                  pl.BlockSpec((None, CHUNK, CHUNK), lambda h, c: (h, 0, 0)),
                  pl.BlockSpec((None, CHUNK, 1), lambda h, c: (h, 0, 0)),
                  pl.BlockSpec((None, CHUNK, 1), lambda h, c: (h, 0, 0)),
                  pl.BlockSpec((None, 1, RET_V), lambda h, c: (h, 0, 0))],
        out_specs=(pl.BlockSpec((ROW_TILE, RET_V), lambda h, c: (c, h)),
                   pl.BlockSpec((None, G, RET_QK, RET_V), lambda h, c: (h, c, 0, 0))),
        scratch_shapes=[pltpu.VMEM((RET_QK, RET_V), F32)],
        compiler_params=_params(("parallel", "arbitrary")),
    )(qr, kr, v, dm, qd, kd, cd)


def _ret_bwd(qr, kr, v, do, states):
    L = qr.shape[0]
    nc = L // CHUNK
    G = ROW_TILE // CHUNK
    nt = nc // G
    dm, qd, kd, cd = _ret_decays()
    dn_nt = (((1,), (1,)), ((), ()))
    dn_tn = (((0,), (0,)), ((), ()))

    def body(q_ref, k_ref, v_ref, do_ref, st_ref, dm_ref, qd_ref, kd_ref, cd_ref,
             dq_ref, dk_ref, dv_ref, ds_ref):
        @pl.when(pl.program_id(1) == 0)
        def _():
            ds_ref[...] = jnp.zeros_like(ds_ref)

        dstate = ds_ref[...]
        dmat = dm_ref[...]
        for g in reversed(range(G)):
            rows = slice(g * CHUNK, (g + 1) * CHUNK)
            q, k, vv, do = q_ref[rows, :], k_ref[rows, :], v_ref[rows, :], do_ref[rows, :]
            kb = k.astype(_MXU)
            dob = do.astype(_MXU)
            doq = (do * qd_ref[...]).astype(_MXU)
            dsb = dstate.astype(_MXU)
            s = (lax.dot_general(q, kb, dn_nt, preferred_element_type=F32) * dmat).astype(_MXU)
            draw = (lax.dot_general(dob, vv, dn_nt, preferred_element_type=F32) * dmat).astype(_MXU)
            kdm = (k * kd_ref[...]).astype(_MXU)
            dv = (lax.dot_general(s, dob, dn_tn, preferred_element_type=F32)
                  + jnp.dot(kdm, dsb, preferred_element_type=F32))
            dq = (jnp.dot(draw, kb, preferred_element_type=F32)
                  + lax.dot_general(doq, st_ref[g], dn_nt, preferred_element_type=F32))
            dk = (lax.dot_general(draw, q, dn_tn, preferred_element_type=F32)
                  + lax.dot_general(vv, dsb, dn_nt, preferred_element_type=F32) * kd_ref[...])
            dq_ref[rows, :] = dq
            dk_ref[rows, :] = dk
            dv_ref[rows, :] = dv.astype(_MXU)
            dstate = cd_ref[...] * dstate + lax.dot_general(q, doq, dn_tn, preferred_element_type=F32)
        ds_ref[...] = dstate

    rev = lambda h, c: (nt - 1 - c, h)
    return pl.pallas_call(
        body, name="ret_bwd",
        out_shape=(jax.ShapeDtypeStruct((L, _RET_QKW), F32), jax.ShapeDtypeStruct((L, _RET_QKW), F32),
                   jax.ShapeDtypeStruct((L, RET_HEADS * RET_V), _MXU)),
        grid=(RET_HEADS, nt),
        in_specs=[pl.BlockSpec((ROW_TILE, RET_QK), rev),
                  pl.BlockSpec((ROW_TILE, RET_QK), rev),
                  pl.BlockSpec((ROW_TILE, RET_V), rev),
                  pl.BlockSpec((ROW_TILE, RET_V), rev),
                  pl.BlockSpec((None, G, RET_QK, RET_V), lambda h, c: (h, nt - 1 - c, 0, 0)),
                  pl.BlockSpec((None, CHUNK, CHUNK), lambda h, c: (h, 0, 0)),
                  pl.BlockSpec((None, CHUNK, 1), lambda h, c: (h, 0, 0)),
                  pl.BlockSpec((None, CHUNK, 1), lambda h, c: (h, 0, 0)),
                  pl.BlockSpec((None, 1, RET_V), lambda h, c: (h, 0, 0))],
        out_specs=(pl.BlockSpec((ROW_TILE, RET_QK), rev), pl.BlockSpec((ROW_TILE, RET_QK), rev),
                   pl.BlockSpec((ROW_TILE, RET_V), rev)),
        scratch_shapes=[pltpu.VMEM((RET_QK, RET_V), F32)],
        compiler_params=_params(("parallel", "arbitrary")),
    )(qr, kr, v, do, states, dm, qd, kd, cd)


def _gn(o, g):
    mu = jnp.mean(o, axis=-1, keepdims=True)
    oc = o - mu
    var = jnp.mean(oc * oc, axis=-1, keepdims=True)
    return oc * lax.rsqrt(var + NORM_EPS) * g


def _ret_gate_fwd(o, qkz, gn_g):
    L, W = o.shape
    zblk = qkz.shape[1] // W - 1

    def body(o_ref, z_ref, g_ref, y_ref):
        for h in range(RET_HEADS):
            sl = slice(h * RET_V, (h + 1) * RET_V)
            y_ref[:, sl] = (_gn(o_ref[:, sl], g_ref[:, sl]) * _silu(z_ref[:, sl])).astype(_MXU)

    return pl.pallas_call(
        body, name="ret_gate_fwd",
        out_shape=jax.ShapeDtypeStruct((L, W), _MXU),
        grid=(L // ROW_TILE,),
        in_specs=[_rows(W), pl.BlockSpec((ROW_TILE, W), lambda i: (i, zblk)), _whole((1, W))],
        out_specs=_rows(W),
        compiler_params=_params(("parallel",)),
    )(o, qkz, gn_g)


def _ret_gate_bwd(dy, o, qkz, gn_g):
    L, W = o.shape
    zblk = qkz.shape[1] // W - 1

    def body(dy_ref, o_ref, z_ref, g_ref, do_ref, dz_ref, dg_ref):
        @pl.when(pl.program_id(0) == 0)
        def _():
            dg_ref[...] = jnp.zeros_like(dg_ref)

        for h in range(RET_HEADS):
            sl = slice(h * RET_V, (h + 1) * RET_V)
            z = z_ref[:, sl]
            o = o_ref[:, sl]
            g = g_ref[:, sl]
            dy = dy_ref[:, sl]
            mu = jnp.mean(o, axis=-1, keepdims=True)
            oc = o - mu
            var = jnp.mean(oc * oc, axis=-1, keepdims=True)
            r = lax.rsqrt(var + NORM_EPS)
            xh = oc * r
            dn = dy * _silu(z)
            dz_ref[:, sl] = (dy * (xh * g) * _dsilu(z)).astype(_MXU)
            dg_ref[:, sl] += jnp.sum(dn * xh, axis=0, keepdims=True)
            dxh = dn * g
            do_ref[:, sl] = r * (dxh - jnp.mean(dxh, axis=-1, keepdims=True)
                                 - xh * jnp.mean(dxh * xh, axis=-1, keepdims=True))

    return pl.pallas_call(
        body, name="ret_gate_bwd",
        out_shape=(jax.ShapeDtypeStruct((L, W), F32), jax.ShapeDtypeStruct((L, W), _MXU),
                   jax.ShapeDtypeStruct((1, W), F32)),
        grid=(L // ROW_TILE,),
        in_specs=[_rows(W), _rows(W), pl.BlockSpec((ROW_TILE, W), lambda i: (i, zblk)), _whole((1, W))],
        out_specs=(_rows(W), _rows(W), _whole((1, W))),
        compiler_params=_params(("arbitrary",)),
    )(dy, o, qkz, gn_g)


def _dead_rows(pad):
    return pad // LANES * LANES


def _heads_t(a):
    return a[:, :FOX_HEADS].T


def _fox_fwd(h, w, lnw, pad, tag, side=None):
    w_in, b_f, w_out = w
    L = h.shape[0]
    W = D_MODEL
    w_qkv = w_in[:, :3 * W]
    w_zf = jnp.concatenate([w_in[:, 3 * W:4 * W], jnp.pad(w_in[:, 4 * W:], ((0, 0), (0, LANES - FOX_HEADS)))], axis=1)
    bias = jnp.pad(b_f, (0, LANES - FOX_HEADS))[None, :]
    qkv = _matmul(h, w_qkv, "nn", _MXU, f"{tag}_qkv")
    zf = _matmul(h, w_zf, "nn", F32, f"{tag}_zf")
    cum = _fox_cum(zf, bias, pad, f"{tag}_cum")
    ab = _bias_terms(_heads_t(cum), pad)
    nb = W // LANES
    o, lse, side_out = _attn_fwd(qkv, 0, qkv, nb, qkv, 2 * nb, ab, FOX_HEADS, LANES, LANES, LANES ** -0.5,
                                 f"{tag}_attn_fwd", _dead_rows(pad), side)
    y = _gate_fwd(o, zf, f"{tag}_gate_fwd")
    mo = _matmul(y, w_out, "nn", F32, f"{tag}_out")
    zl, hn = _ln_fwd(h, mo, lnw[0], lnw[1], f"{tag}_ln_fwd")
    return hn, dict(h=h, qkv=qkv, zf=zf, ab=ab, o=o, lse=lse, y=y, zl=zl,
                    w_all=jnp.concatenate([w_qkv, w_zf], axis=1), w_out=w_out, bias=bias), side_out


def _fox_bwd(s, dzl, pad, tag, side=None):
    W = D_MODEL
    nb = W // LANES
    dy = _matmul(dzl, s["w_out"], "nt", F32, f"{tag}_dy")
    d_wout = _matmul(s["y"], dzl, "tn", F32, f"{tag}_dwout")
    do, delta, dzg = _gate_bwd(dy, s["o"], s["zf"], FOX_HEADS, f"{tag}_gate_bwd")
    qkv = s["qkv"]
    dq, dk, dv, dck, dcq, side_out = _attn_bwd(qkv, 0, qkv, nb, qkv, 2 * nb, do, s["ab"], s["lse"], delta,
                                               FOX_HEADS, LANES, LANES, LANES ** -0.5, f"{tag}_attn_bwd",
                                               _dead_rows(pad), side)
    dcum = jnp.pad((dck + dcq).reshape(FOX_HEADS, -1).T, ((0, 0), (0, LANES - FOX_HEADS)))
    df, dbias = _fox_cum_bwd(dcum, s["zf"], s["bias"], pad, f"{tag}_cum_bwd")
    dproj = jnp.concatenate([dq.astype(_MXU), dk.astype(_MXU), dv.astype(_MXU), dzg, df], axis=1)
    d_wall = _matmul(s["h"], dproj, "tn", F32, f"{tag}_dwin")
    dh = _matmul(dproj, s["w_all"], "nt", F32, f"{tag}_dh")
    d_win = jnp.concatenate([d_wall[:, :4 * W], d_wall[:, 4 * W:4 * W + FOX_HEADS]], axis=1)
    return dh, (d_win, dbias[0, :FOX_HEADS], d_wout), side_out


def _mla_tables(L, pad):
    pos = (jnp.arange(L) - pad).astype(F32)
    inv = ROPE_BASE ** (-jnp.arange(0, MLA_ROPE, 2, dtype=F32) / MLA_ROPE)
    ang = pos[:, None] * inv[None, :]
    c, s = jnp.cos(ang), jnp.sin(ang)
    z32, z64 = jnp.zeros_like(s), jnp.zeros((L, LANES - MLA_ROPE), F32)
    cos = jnp.concatenate([c, c, jnp.ones_like(z64)], axis=1)
    sa = jnp.concatenate([-s, z32, z64], axis=1)
    sb = jnp.concatenate([z32, s, z64], axis=1)
    return cos, sa, sb


def _mla_weights(w_in, w_uq, w_ukv):
    a, b = MLA_Q_LORA, MLA_Q_LORA + MLA_KV_LORA
    w_zc = jnp.concatenate([w_in[:, b + MLA_ROPE:], w_in[:, :b + MLA_ROPE],
                            jnp.zeros((D_MODEL, LANES - MLA_ROPE), w_in.dtype)], axis=1)
    uq = w_uq.reshape(MLA_Q_LORA, MLA_HEADS, MLA_NOPE + MLA_ROPE)
    uq = jnp.pad(uq, ((0, 0), (0, 0), (0, MLA_DK - MLA_NOPE - MLA_ROPE))).reshape(MLA_Q_LORA, MLA_HEADS * MLA_DK)
    ukv = w_ukv.reshape(MLA_KV_LORA, MLA_HEADS, MLA_NOPE + MLA_V)
    ukv = jnp.concatenate([ukv[:, :, :MLA_NOPE].reshape(MLA_KV_LORA, -1), ukv[:, :, MLA_NOPE:].reshape(MLA_KV_LORA, -1)], axis=1)
    return w_zc, uq, ukv


def _mla_fwd(h, w, lnw, pad):
    w_in, gq, gkv, w_uq, w_ukv, w_out = w
    L = h.shape[0]
    w_zc, uq, ukv = _mla_weights(w_in, w_uq, w_ukv)
    cos, sa, sb = _mla_tables(L, pad)
    proj = _matmul(h, w_zc, "nn", F32, "mla_proj")
    cqn, ckvn = _mla_norm_fwd(proj, gq[None, :], gkv[None, :])
    qf = _matmul(cqn, uq, "nn", F32, "mla_uq")
    kv = _matmul(ckvn, ukv, "nn", _MXU, "mla_ukv")
    q_full, k_full = _mla_rope_fwd(qf, kv, proj, cos, sa, sb)
    ab = _bias_terms(jnp.zeros((MLA_HEADS, L), F32), pad)
    nb = MLA_HEADS * MLA_NOPE // LANES
    o, lse, _ = _attn_fwd(q_full, 0, k_full, 0, kv, nb, ab, MLA_HEADS, MLA_DK, MLA_V,
                          (MLA_NOPE + MLA_ROPE) ** -0.5, "mla_attn_fwd", _dead_rows(pad))
    y = _gate_fwd(o, proj, "mla_gate_fwd")
    mo = _matmul(y, w_out, "nn", F32, "mla_out")
    zl, hn = _ln_fwd(h, mo, lnw[0], lnw[1], "mla_ln_fwd")
    return hn, dict(h=h, proj=proj, cqn=cqn, ckvn=ckvn, kv=kv, q_full=q_full, k_full=k_full, ab=ab,
                    dead=_dead_rows(pad),
                    o=o, lse=lse, y=y, zl=zl, w_zc=w_zc, uq=uq, ukv=ukv, w_out=w_out, gq=gq, gkv=gkv,
                    tabs=(cos, sa, sb))


def _mla_bwd(s, dzl):
    H = MLA_HEADS
    dy = _matmul(dzl, s["w_out"], "nt", F32, "mla_dy")
    d_wout = _matmul(s["y"], dzl, "tn", F32, "mla_dwout")
    do, delta, dzg = _gate_bwd(dy, s["o"], s["proj"], H, "mla_gate_bwd")
    nb = H * MLA_NOPE // LANES
    dq_full, dk_full, dv, _, _, _ = _attn_bwd(s["q_full"], 0, s["k_full"], 0, s["kv"], nb, do, s["ab"],
                                              s["lse"], delta,
                                              H, MLA_DK, MLA_V, (MLA_NOPE + MLA_ROPE) ** -0.5, "mla_attn_bwd",
                                              s["dead"], bias_grads=False)
    cos, sa, sb = s["tabs"]
    dqf, dkn, dkr = _mla_rope_bwd(dq_full, dk_full, cos, -sa, -sb)
    d_uq = _matmul(s["cqn"], dqf, "tn", F32, "mla_duq")
    dcqn = _matmul(dqf, s["uq"], "nt", F32, "mla_dcqn")
    dkv = jnp.concatenate([dkn, dv.astype(_MXU)], axis=1)
    d_ukv = _matmul(s["ckvn"], dkv, "tn", F32, "mla_dukv")
    dckvn = _matmul(dkv, s["ukv"], "nt", F32, "mla_dckvn")
    dcq, dckv, dgq, dgkv = _mla_norm_bwd(s["proj"], s["gq"][None, :], s["gkv"][None, :], dcqn, dckvn)
    dproj = jnp.concatenate([dzg, dcq, dckv, dkr], axis=1)
    d_wzc = _matmul(s["h"], dproj, "tn", F32, "mla_dwin")
    dh = _matmul(dproj, s["w_zc"], "nt", F32, "mla_dh")
    W = D_MODEL
    d_win = jnp.concatenate([d_wzc[:, W:W + MLA_Q_LORA + MLA_KV_LORA + MLA_ROPE], d_wzc[:, :W]], axis=1)
    d_wuq = d_uq.reshape(MLA_Q_LORA, H, MLA_DK)[:, :, :MLA_NOPE + MLA_ROPE].reshape(MLA_Q_LORA, -1)
    hk = H * MLA_NOPE
    d_wukv = jnp.concatenate([d_ukv[:, :hk].reshape(MLA_KV_LORA, H, MLA_NOPE),
                              d_ukv[:, hk:].reshape(MLA_KV_LORA, H, MLA_V)], axis=2).reshape(MLA_KV_LORA, -1)
    return dh, (d_win, dgq[0], dgkv[0], d_wuq, d_wukv, d_wout)


def _ret_tables(L, pad):
    pos = (jnp.arange(L) - pad).astype(F32)
    inv = 1.0 / (ROPE_BASE ** jnp.linspace(0.0, 1.0, RET_QK // 2, dtype=F32))
    ang = pos[:, None] * inv[None, :]
    return jnp.cos(ang), jnp.sin(ang)


def _ret_layer_fwd(h, w, lnw, pad):
    w_in, gn_g, w_out = w
    L = h.shape[0]
    q1 = 2 * _RET_QKW
    v1 = q1 + RET_HEADS * RET_V
    w_qkz = jnp.concatenate([w_in[:, :q1], w_in[:, v1:]], axis=1)
    cos, sin = _ret_tables(L, pad)
    qkz = _matmul(h, w_qkz, "nn", F32, "ret_qkz")
    v = _matmul(h, w_in[:, q1:v1], "nn", _MXU, "ret_v")
    qr, kr = _ret_rope_fwd(qkz, cos, sin, pad)
    o, states = _ret_fwd(qr, kr, v)
    y = _ret_gate_fwd(o, qkz, gn_g[None, :])
    mo = _matmul(y, w_out, "nn", F32, "ret_out")
    zl, hn = _ln_fwd(h, mo, lnw[0], lnw[1], "ret_ln_fwd")
    return hn, dict(h=h, qkz=qkz, v=v, qr=qr, kr=kr, o=o, states=states, y=y, zl=zl, w_in=w_in, w_out=w_out,
                    gn_g=gn_g, tabs=(cos, sin))


def _ret_layer_bwd(s, dzl, pad):
    dy = _matmul(dzl, s["w_out"], "nt", F32, "ret_dy")
    d_wout = _matmul(s["y"], dzl, "tn", F32, "ret_dwout")
    do, dzg, dgn = _ret_gate_bwd(dy, s["o"], s["qkz"], s["gn_g"][None, :])
    dqr, dkr, dv = _ret_bwd(s["qr"], s["kr"], s["v"], do, s["states"])
    cos, sin = s["tabs"]
    dq, dk = _ret_rope_bwd(dqr, dkr, cos, sin, pad)
    dproj = jnp.concatenate([dq, dk, dv, dzg], axis=1)
    d_win = _matmul(s["h"], dproj, "tn", F32, "ret_dwin")
    dh = _matmul(dproj, s["w_in"], "nt", F32, "ret_dh")
    return dh, (d_win, dgn[0], d_wout)


def _local_step(x, target, meta, fox0_w, late_shards, small, ln_g, ln_b):
    n_real = x.shape[0]
    first = -(-N_META // ROW_TILE)
    pad = first * ROW_TILE - N_META
    h0 = jnp.concatenate([jnp.zeros((pad, D_MODEL), F32), meta, x], axis=0)
    lnw = [(ln_g[i][None, :], ln_b[i][None, :]) for i in range(4)]
    b_f = small["fox_b_f"]

    h1, s0, gathered = _fox_fwd(h0, (fox0_w[0], b_f[0], fox0_w[1]), lnw[0], pad, "fox0", (late_shards, False))
    w = {key: _merge_layer(key[0], g).astype(_MXU) for key, g in zip(_LATE, gathered)}
    h2, s1 = _mla_fwd(h1, (w["mla_w_in", 0], small["mla_q_norm"][0], small["mla_kv_norm"][0], w["mla_w_uq", 0],
                           w["mla_w_ukv", 0], w["mla_w_out", 0]), lnw[1], pad)
    h3, s2 = _ret_layer_fwd(h2, (w["ret_w_in", 0], small["ret_gn_g"][0], w["ret_w_out", 0]), lnw[2], pad)
    h4, s3, _ = _fox_fwd(h3, (w["fox_w_in", 1], b_f[1], w["fox_w_out", 1]), lnw[3], pad, "fox1")

    dh4, loss = _loss_head(h4, target, first)
    dz3, dg3, db3 = _ln_bwd(s3["zl"], lnw[3][0], dh4, None, "ln3_bwd")
    dm3, gf1, _ = _fox_bwd(s3, dz3, pad, "fox1")
    dz2, dg2, db2 = _ln_bwd(s2["zl"], lnw[2][0], dz3, dm3, "ln2_bwd")
    dm2, gr = _ret_layer_bwd(s2, dz2, pad)
    dz1, dg1, db1 = _ln_bwd(s1["zl"], lnw[1][0], dz2, dm2, "ln1_bwd")
    dm1, gm = _mla_bwd(s1, dz1)
    dz0, dg0, db0 = _ln_bwd(s0["zl"], lnw[0][0], dz1, dm1, "ln0_bwd")
    late = {("fox_w_in", 1): gf1[0], ("fox_w_out", 1): gf1[2], ("mla_w_in", 0): gm[0], ("mla_w_uq", 0): gm[3],
            ("mla_w_ukv", 0): gm[4], ("mla_w_out", 0): gm[5], ("ret_w_in", 0): gr[0], ("ret_w_out", 0): gr[2]}
    dm0, gf0, late_parts = _fox_bwd(s0, dz0, pad, "fox0", ([_split_layer(k[0], late[k]) for k in _LATE], True))
    grad_x, grad_meta = _input_grads(dz0, dm0, first, n_real)

    small_grads = dict(
        meta=grad_meta, fox_b_f=jnp.stack([gf0[1], gf1[1]]), mla_q_norm=gm[1][None], mla_kv_norm=gm[2][None],
        ret_gn_g=gr[1][None], ln_g=jnp.concatenate([dg0, dg1, dg2, dg3], axis=0),
        ln_b=jnp.concatenate([db0, db1, db2, db3], axis=0))
    return loss, grad_x, (gf0[0], gf0[2]), late_parts, small_grads


def _exchange(tensors, scatter, name):
    nt = len(tensors)

    def body(*refs):
        copies = _exchange_copies(refs[:nt], refs[nt:2 * nt], *refs[2 * nt:], scatter)
        for cp in copies:
            cp.start()
        for cp in copies:
            cp.wait()

    hbm = pl.BlockSpec(memory_space=pl.ANY)
    out_shape, sems = _exchange_shapes(tensors, scatter)
    return pl.pallas_call(
        body, name=name, out_shape=out_shape, in_specs=[hbm] * nt, out_specs=(hbm,) * nt, scratch_shapes=sems,
    )(*tensors)


def _adamw(parts, w, m, v, name):
    R, C = w.shape
    tr = next(t for t in range(min(R, 256), 0, -1) if R % t == 0 and (t % 8 == 0 or t == R))

    def body(p_ref, w_ref, m_ref, v_ref, g_ref, d_ref, mo_ref, vo_ref):
        g = p_ref[0]
        for d in range(1, N_DEV):
            g = g + p_ref[d]
        mn = ADAM_B1 * m_ref[...] + (1.0 - ADAM_B1) * g
        vn = ADAM_B2 * v_ref[...] + (1.0 - ADAM_B2) * (g * g)
        m_hat = mn / (1.0 - ADAM_B1 ** ADAM_STEP)
        v_hat = vn / (1.0 - ADAM_B2 ** ADAM_STEP)
        g_ref[...] = g
        d_ref[...] = -ADAM_LR * (m_hat / (jnp.sqrt(v_hat) + ADAM_EPS) + ADAM_WD * w_ref[...])
        mo_ref[...] = mn
        vo_ref[...] = vn

    blk = pl.BlockSpec((tr, C), lambda i: (i, 0))
    sd = jax.ShapeDtypeStruct((R, C), F32)
    return pl.pallas_call(
        body, name=name,
        out_shape=(sd, sd, sd, sd),
        grid=(R // tr,),
        in_specs=[pl.BlockSpec((N_DEV, tr, C), lambda i: (0, i, 0)), blk, blk, blk],
        out_specs=(blk, blk, blk, blk),
        compiler_params=_params(("parallel",)),
    )(parts, w, m, v)


_SPECS = dict(
    meta=((16, 1024), 1),
    fox_w_in=((2, 1024, 4104), 2), fox_b_f=((2, 8), None), fox_w_out=((2, 1024, 1024), 1),
    mla_w_in=((1, 1024, 1728), 2), mla_q_norm=((1, 384), None), mla_kv_norm=((1, 256), None),
    mla_w_uq=((1, 384, 1536), 2), mla_w_ukv=((1, 256, 2048), 2), mla_w_out=((1, 1024, 1024), 1),
    ret_w_in=((1, 1024, 6144), 2), ret_gn_g=((1, 2048), 1), ret_w_out=((1, 2048, 1024), 1),
    ln_g=((4, 1024), None), ln_b=((4, 1024), None),
)
_NAMES = list(_SPECS)
_BIG = ["fox_w_in", "fox_w_out", "mla_w_in", "mla_w_uq", "mla_w_ukv", "mla_w_out", "ret_w_in", "ret_w_out"]
_SMALL = [n for n in _NAMES if n not in _BIG]
_SMALL_GATHERED = ["meta", "ret_gn_g"]
_PACK_ROWS = 8


def _shard_shape(name):
    shape, ax = _SPECS[name]
    if ax is None:
        return shape
    return tuple(s // N_DEV if i == ax else s for i, s in enumerate(shape))


_EARLY = [("fox_w_in", 0), ("fox_w_out", 0)]
_LATE = [("fox_w_in", 1), ("fox_w_out", 1), ("mla_w_in", 0), ("mla_w_uq", 0), ("mla_w_ukv", 0), ("mla_w_out", 0),
         ("ret_w_in", 0), ("ret_w_out", 0)]


def _split_at(full, ax):
    shape = full.shape
    parts = full.reshape(shape[:ax] + (N_DEV, shape[ax] // N_DEV) + shape[ax + 1:])
    return jnp.moveaxis(parts, ax, 0)


def _merge_at(stacked, ax):
    full = jnp.moveaxis(stacked, 0, ax)
    return full.reshape(full.shape[:ax] + (-1,) + full.shape[ax + 2:])


def _split(name, full):
    return _split_at(full, _SPECS[name][1])


def _merge(name, stacked):
    return _merge_at(stacked, _SPECS[name][1])


def _split_layer(name, full):
    return _split_at(full, _SPECS[name][1] - 1)


def _merge_layer(name, stacked):
    return _merge_at(stacked, _SPECS[name][1] - 1)


def _pad_rows(flat):
    n = flat.shape[-1]
    unit = _PACK_ROWS * LANES
    total = -(-n // unit) * unit
    flat = jnp.pad(flat, [(0, 0)] * (flat.ndim - 1) + [(0, total - n)])
    return flat.reshape(flat.shape[:-1] + (total // LANES, LANES))


def _pack_small(tree, names, tail):
    return _pad_rows(jnp.concatenate([tree[n].reshape(-1) for n in names] + [tail]))


def _unpack_small(buf, names):
    flat, out, at = buf.reshape(-1), {}, 0
    for n in names:
        shp = _shard_shape(n)
        out[n] = flat[at:at + math.prod(shp)].reshape(shp)
        at += math.prod(shp)
    return out, flat[at:]


def kernel(x, meta, fox_w_in, fox_b_f, fox_w_out, mla_w_in, mla_q_norm, mla_kv_norm, mla_w_uq, mla_w_ukv, mla_w_out, ret_w_in, ret_gn_g, ret_w_out, ln_g, ln_b, loss_target, m_meta, m_fox_w_in, m_fox_b_f, m_fox_w_out, m_mla_w_in, m_mla_q_norm, m_mla_kv_norm, m_mla_w_uq, m_mla_w_ukv, m_mla_w_out, m_ret_w_in, m_ret_gn_g, m_ret_w_out, m_ln_g, m_ln_b, v_meta, v_fox_w_in, v_fox_b_f, v_fox_w_out, v_mla_w_in, v_mla_q_norm, v_mla_kv_norm, v_mla_w_uq, v_mla_w_ukv, v_mla_w_out, v_ret_w_in, v_ret_gn_g, v_ret_w_out, v_ln_g, v_ln_b):
    w = dict(meta=meta, fox_w_in=fox_w_in, fox_b_f=fox_b_f, fox_w_out=fox_w_out, mla_w_in=mla_w_in,
             mla_q_norm=mla_q_norm, mla_kv_norm=mla_kv_norm, mla_w_uq=mla_w_uq, mla_w_ukv=mla_w_ukv,
             mla_w_out=mla_w_out, ret_w_in=ret_w_in, ret_gn_g=ret_gn_g, ret_w_out=ret_w_out, ln_g=ln_g, ln_b=ln_b)
    m = dict(meta=m_meta, fox_w_in=m_fox_w_in, fox_b_f=m_fox_b_f, fox_w_out=m_fox_w_out, mla_w_in=m_mla_w_in,
             mla_q_norm=m_mla_q_norm, mla_kv_norm=m_mla_kv_norm, mla_w_uq=m_mla_w_uq, mla_w_ukv=m_mla_w_ukv,
             mla_w_out=m_mla_w_out, ret_w_in=m_ret_w_in, ret_gn_g=m_ret_gn_g, ret_w_out=m_ret_w_out, ln_g=m_ln_g, ln_b=m_ln_b)
    v = dict(meta=v_meta, fox_w_in=v_fox_w_in, fox_b_f=v_fox_b_f, fox_w_out=v_fox_w_out, mla_w_in=v_mla_w_in,
             mla_q_norm=v_mla_q_norm, mla_kv_norm=v_mla_kv_norm, mla_w_uq=v_mla_w_uq, mla_w_ukv=v_mla_w_ukv,
             mla_w_out=v_mla_w_out, ret_w_in=v_ret_w_in, ret_gn_g=v_ret_gn_g, ret_w_out=v_ret_w_out, ln_g=v_ln_g, ln_b=v_ln_b)

    def shard16(key):
        return w[key[0]][key[1]].astype(jnp.bfloat16)

    none = jnp.zeros((0,), F32)
    gathered = _exchange([shard16(k) for k in _EARLY] + [_pack_small(w, _SMALL_GATHERED, none)],
                         False, "weights_all_gather")
    fox0_w = [_merge_layer(k[0], g).astype(_MXU) for k, g in zip(_EARLY, gathered)]
    small = {n: w[n] for n in _SMALL}
    pieces = [_unpack_small(gathered[-1][d], _SMALL_GATHERED)[0] for d in range(N_DEV)]
    for n in _SMALL_GATHERED:
        small[n] = _merge(n, jnp.stack([p[n] for p in pieces]))

    loss, grad_x, early_grads, late_parts, grads = _local_step(
        x[0], loss_target[0], small["meta"], fox0_w, [shard16(k) for k in _LATE], small, ln_g, ln_b)

    def small_rows(n):
        if _SPECS[n][1] is None:
            return jnp.broadcast_to(grads[n].reshape(1, -1), (N_DEV, grads[n].size))
        return _split(n, grads[n]).reshape(N_DEV, -1)

    small_out = _pad_rows(jnp.concatenate([small_rows(n) for n in _SMALL]
                                          + [jnp.broadcast_to(loss.reshape(1, 1), (N_DEV, 1))], axis=1))
    parts = _exchange([_split_layer(k[0], g) for k, g in zip(_EARLY, early_grads)] + [small_out],
                      True, "grads_all_to_all")
    by_key = dict(zip(_EARLY + _LATE, list(parts[:-1]) + list(late_parts)))

    def rows2d(a):
        return a.reshape(-1, a.shape[-1])

    out = {}
    for n in _BIG:
        layers = [by_key[n, i] for i in range(_SPECS[n][0][0])]
        p = layers[0] if len(layers) == 1 else jnp.stack(layers, axis=1)
        res = _adamw(p.reshape(N_DEV, -1, p.shape[-1]), rows2d(w[n]), rows2d(m[n]), rows2d(v[n]), f"adamw_{n}")
        out[n] = [r.reshape(w[n].shape) for r in res]
    zero1 = jnp.zeros((1,), F32)
    res = _adamw(parts[-1], _pack_small(w, _SMALL, zero1), _pack_small(m, _SMALL, zero1),
                 _pack_small(v, _SMALL, zero1), "adamw_small")
    small_res = [_unpack_small(r, _SMALL) for r in res]
    for n in _SMALL:
        out[n] = [s[0][n] for s in small_res]
    total_loss = small_res[0][1][0]
    return (total_loss, grad_x[None], *[out[n][0] for n in _NAMES], *[out[n][1] for n in _NAMES],
            *[out[n][2] for n in _NAMES], *[out[n][3] for n in _NAMES])
```

```python
import functools
import math

import numpy as np
import jax
import jax.numpy as jnp
from jax import lax
from jax.experimental import pallas as pl
from jax.experimental.pallas import tpu as pltpu

F32 = jnp.float32
_MXU = jnp.bfloat16

N_DEV = 8
N_META = 16
D_MODEL = 1024
ROW_TILE = 512
LANES = 128
CHUNK = 128
NEG = -1e30
VMEM_LIMIT = 60 * 1024 * 1024

FOX_HEADS = 8
MLA_HEADS = 8
MLA_NOPE, MLA_ROPE, MLA_V = 128, 64, 128
MLA_Q_LORA, MLA_KV_LORA = 384, 256
MLA_DK = 256
RET_HEADS, RET_QK, RET_V = 4, 256, 512
ROPE_BASE = 10000.0
ALPHA = (2 * 4) ** 0.25
NORM_EPS = 1e-5

ADAM_LR, ADAM_B1, ADAM_B2, ADAM_EPS, ADAM_WD, ADAM_STEP = 0.001, 0.9, 0.999, 1e-08, 0.01, 10


def _params(sem, vmem=VMEM_LIMIT):
    return pltpu.CompilerParams(dimension_semantics=sem, vmem_limit_bytes=vmem)


def _tile(n, pref):
    if n <= pref:
        return n
    t = (pref // LANES) * LANES
    while n % t:
        t -= LANES
    return t


_MM_TILES = {"nn": (512, 4224, 2048), "nt": (512, 2048, 6144), "tn": (1024, 2048, 1536)}


def _matmul(a, b, mode, out_dtype, name):
    if mode == "nn":
        (M, K), (K2, N) = a.shape, b.shape
    elif mode == "nt":
        (M, K), (N, K2) = a.shape, b.shape
    else:
        (K, M), (K2, N) = a.shape, b.shape
    assert K == K2, (a.shape, b.shape, mode)
    tm, tn, tk = (_tile(d, p) for d, p in zip((M, N, K), _MM_TILES[mode]))
    nk = K // tk
    if mode == "nn":
        a_spec = pl.BlockSpec((tm, tk), lambda i, j, k: (i, k))
        b_spec = pl.BlockSpec((tk, tn), lambda i, j, k: (k, j))
        dn = (((1,), (0,)), ((), ()))
    elif mode == "nt":
        a_spec = pl.BlockSpec((tm, tk), lambda i, j, k: (i, k))
        b_spec = pl.BlockSpec((tn, tk), lambda i, j, k: (j, k))
        dn = (((1,), (1,)), ((), ()))
    else:
        a_spec = pl.BlockSpec((tk, tm), lambda i, j, k: (k, i))
        b_spec = pl.BlockSpec((tk, tn), lambda i, j, k: (k, j))
        dn = (((0,), (0,)), ((), ()))

    def body(a_ref, b_ref, o_ref, *acc):
        part = lax.dot_general(a_ref[...].astype(_MXU), b_ref[...].astype(_MXU), dn,
                               preferred_element_type=F32)
        if nk == 1:
            o_ref[...] = part.astype(out_dtype)
            return
        acc_ref, = acc
        k = pl.program_id(2)

        @pl.when(k == 0)
        def _():
            acc_ref[...] = part

        @pl.when(k > 0)
        def _():
            acc_ref[...] += part

        @pl.when(k == nk - 1)
        def _():
            o_ref[...] = acc_ref[...].astype(out_dtype)

    return pl.pallas_call(
        body, name=name,
        out_shape=jax.ShapeDtypeStruct((M, N), out_dtype),
        grid=(M // tm, N // tn, nk),
        in_specs=[a_spec, b_spec],
        out_specs=pl.BlockSpec((tm, tn), lambda i, j, k: (i, j)),
        scratch_shapes=[pltpu.VMEM((tm, tn), F32)] if nk > 1 else [],
        compiler_params=_params(("parallel", "parallel", "arbitrary")),
    )(a, b)


def _rows(w, off=0):
    return pl.BlockSpec((ROW_TILE, w), lambda i: (i + off, 0))


def _whole(shape):
    return pl.BlockSpec(shape, lambda i: (0,) * len(shape))


def _silu(z):
    return z * jax.nn.sigmoid(z)


def _dsilu(z):
    s = jax.nn.sigmoid(z)
    return s * (1.0 + z * (1.0 - s))


def _ln_fwd(h, y, g, b, name):
    L, D = h.shape

    def body(h_ref, y_ref, g_ref, b_ref, z_ref, o_ref):
        z = ALPHA * h_ref[...] + y_ref[...]
        mu = jnp.mean(z, axis=-1, keepdims=True)
        zc = z - mu
        var = jnp.mean(zc * zc, axis=-1, keepdims=True)
        z_ref[...] = z
        o_ref[...] = zc * lax.rsqrt(var + NORM_EPS) * g_ref[...] + b_ref[...]

    return pl.pallas_call(
        body, name=name,
        out_shape=(jax.ShapeDtypeStruct((L, D), F32), jax.ShapeDtypeStruct((L, D), F32)),
        grid=(L // ROW_TILE,),
        in_specs=[_rows(D), _rows(D), _whole((1, D)), _whole((1, D))],
        out_specs=(_rows(D), _rows(D)),
        compiler_params=_params(("parallel",)),
    )(h, y, g, b)


def _ln_bwd(zl, g, ga, gb, name):
    L, D = zl.shape
    two = gb is not None

    def body(*refs):
        if two:
            z_ref, g_ref, ga_ref, gb_ref, dz_ref, dg_ref, db_ref = refs
            gout = ALPHA * ga_ref[...] + gb_ref[...]
        else:
            z_ref, g_ref, ga_ref, dz_ref, dg_ref, db_ref = refs
            gout = ga_ref[...]
        z = z_ref[...]
        mu = jnp.mean(z, axis=-1, keepdims=True)
        zc = z - mu
        var = jnp.mean(zc * zc, axis=-1, keepdims=True)
        rstd = lax.rsqrt(var + NORM_EPS)
        xhat = zc * rstd
        dxh = gout * g_ref[...]
        m1 = jnp.mean(dxh, axis=-1, keepdims=True)
        m2 = jnp.mean(dxh * xhat, axis=-1, keepdims=True)
        dz_ref[...] = rstd * (dxh - m1 - xhat * m2)

        @pl.when(pl.program_id(0) == 0)
        def _():
            dg_ref[...] = jnp.zeros_like(dg_ref)
            db_ref[...] = jnp.zeros_like(db_ref)

        dg_ref[...] += jnp.sum(gout * xhat, axis=0, keepdims=True)
        db_ref[...] += jnp.sum(gout, axis=0, keepdims=True)

    ins = [zl, g, ga] + ([gb] if two else [])
    return pl.pallas_call(
        body, name=name,
        out_shape=(jax.ShapeDtypeStruct((L, D), F32), jax.ShapeDtypeStruct((1, D), F32),
                   jax.ShapeDtypeStruct((1, D), F32)),
        grid=(L // ROW_TILE,),
        in_specs=[_rows(D), _whole((1, D)), _rows(D)] + ([_rows(D)] if two else []),
        out_specs=(_rows(D), _whole((1, D)), _whole((1, D))),
        compiler_params=_params(("arbitrary",)),
    )(*ins)


def _loss_head(h, target, first):
    L, D = h.shape

    def body(h_ref, t_ref, dh_ref, loss_ref):
        i = pl.program_id(0)

        @pl.when(i == 0)
        def _():
            loss_ref[...] = jnp.zeros_like(loss_ref)

        @pl.when(i < first)
        def _():
            dh_ref[...] = jnp.zeros_like(dh_ref)

        @pl.when(i >= first)
        def _():
            err = h_ref[...] - t_ref[...]
            dh_ref[...] = err * (1.0 / D)
            part = jnp.sum(jnp.sum(err * err, axis=-1, keepdims=True) * (1.0 / D), axis=0, keepdims=True)
            loss_ref[...] += 0.5 * part

    return pl.pallas_call(
        body, name="loss_head",
        out_shape=(jax.ShapeDtypeStruct((L, D), F32), jax.ShapeDtypeStruct((1, 1), F32)),
        grid=(L // ROW_TILE,),
        in_specs=[_rows(D), pl.BlockSpec((ROW_TILE, D), lambda i: (jnp.maximum(i - first, 0), 0))],
        out_specs=(_rows(D), _whole((1, 1))),
        compiler_params=_params(("arbitrary",)),
    )(h, target)


def _input_grads(dzl, dmix, first, n_real):
    L, D = dzl.shape

    def body(a_ref, b_ref, gx_ref, gm_ref):
        i = pl.program_id(0)
        val = ALPHA * a_ref[...] + b_ref[...]

        @pl.when(i == first - 1)
        def _():
            gm_ref[...] = val[ROW_TILE - N_META:, :]

        @pl.when(i >= first)
        def _():
            gx_ref[...] = val

    return pl.pallas_call(
        body, name="input_grads",
        out_shape=(jax.ShapeDtypeStruct((n_real, D), F32), jax.ShapeDtypeStruct((N_META, D), F32)),
        grid=(L // ROW_TILE,),
        in_specs=[_rows(D), _rows(D)],
        out_specs=(pl.BlockSpec((ROW_TILE, D), lambda i: (jnp.maximum(i - first, 0), 0)),
                   _whole((N_META, D))),
        compiler_params=_params(("arbitrary",)),
    )(dzl, dmix)


def _gate_fwd(o, zsrc, name):
    L, W = o.shape

    def body(o_ref, z_ref, y_ref):
        y_ref[...] = (o_ref[...] * _silu(z_ref[...])).astype(_MXU)

    return pl.pallas_call(
        body, name=name,
        out_shape=jax.ShapeDtypeStruct((L, W), _MXU),
        grid=(L // ROW_TILE,),
        in_specs=[_rows(W), _rows(W)],
        out_specs=_rows(W),
        compiler_params=_params(("parallel",)),
    )(o, zsrc)


def _gate_bwd(dy, o, zsrc, heads, name):
    L, W = o.shape
    hd = W // heads

    def body(dy_ref, o_ref, z_ref, do_ref, dl_ref, dz_ref):
        z = z_ref[...]
        dy = dy_ref[...]
        o = o_ref[...]
        do = dy * _silu(z)
        do_ref[...] = do.astype(_MXU)
        dz_ref[...] = (dy * o * _dsilu(z)).astype(_MXU)
        prod = do * o
        for h in range(heads):
            dl_ref[h] = _as_row(jnp.sum(prod[:, h * hd:(h + 1) * hd], axis=-1, keepdims=True))

    return pl.pallas_call(
        body, name=name,
        out_shape=(jax.ShapeDtypeStruct((L, W), _MXU), jax.ShapeDtypeStruct((heads, L // ROW_TILE, 1, ROW_TILE), F32),
                   jax.ShapeDtypeStruct((L, W), _MXU)),
        grid=(L // ROW_TILE,),
        in_specs=[_rows(W), _rows(W), _rows(W)],
        out_specs=(_rows(W), pl.BlockSpec((heads, None, 1, ROW_TILE), lambda i: (0, i, 0, 0)), _rows(W)),
        compiler_params=_params(("parallel",)),
    )(dy, o, zsrc)


def _split3(x):
    hi = x.astype(_MXU)
    r1 = x - hi.astype(F32)
    mid = r1.astype(_MXU)
    lo = (r1 - mid.astype(F32)).astype(_MXU)
    return hi, mid, lo


def _tri_cumsum(x, reverse):
    T = x.shape[0]
    r = lax.broadcasted_iota(jnp.int32, (T, T), 0)
    c = lax.broadcasted_iota(jnp.int32, (T, T), 1)
    tri = jnp.where((c >= r) if reverse else (c <= r), 1.0, 0.0).astype(_MXU)
    out = jnp.zeros(x.shape, F32)
    for part in _split3(x):
        out = out + jnp.dot(tri, part, preferred_element_type=F32)
    return out


def _fox_cum(zf, bias, pad, name):
    L, Wz = zf.shape
    off = Wz // LANES - 1

    def body(f_ref, b_ref, c_ref, carry_ref):
        i = pl.program_id(0)

        @pl.when(i == 0)
        def _():
            carry_ref[...] = jnp.zeros_like(carry_ref)

        x = f_ref[...] + b_ref[...]
        logf = jnp.minimum(x, 0.0) - jnp.log(1.0 + jnp.exp(-jnp.abs(x)))
        row = i * ROW_TILE + lax.broadcasted_iota(jnp.int32, logf.shape, 0)
        logf = jnp.where(row >= pad, logf, 0.0)
        cum = _tri_cumsum(logf, False) + carry_ref[...]
        c_ref[...] = cum
        carry_ref[...] = cum[ROW_TILE - 1:, :]

    return pl.pallas_call(
        body, name=name,
        out_shape=jax.ShapeDtypeStruct((L, LANES), F32),
        grid=(L // ROW_TILE,),
        in_specs=[pl.BlockSpec((ROW_TILE, LANES), lambda i: (i, off)), _whole((1, LANES))],
        out_specs=_rows(LANES),
        scratch_shapes=[pltpu.VMEM((1, LANES), F32)],
        compiler_params=_params(("arbitrary",)),
    )(zf, bias)


def _fox_cum_bwd(dcum, zf, bias, pad, name):
    L, Wz = zf.shape
    off = Wz // LANES - 1
    n = L // ROW_TILE

    def body(d_ref, f_ref, b_ref, df_ref, db_ref, carry_ref):
        i = pl.program_id(0)

        @pl.when(i == 0)
        def _():
            carry_ref[...] = jnp.zeros_like(carry_ref)
            db_ref[...] = jnp.zeros_like(db_ref)

        rc = _tri_cumsum(d_ref[...], True) + carry_ref[...]
        carry_ref[...] = rc[:1, :]
        x = f_ref[...] + b_ref[...]
        row = (n - 1 - i) * ROW_TILE + lax.broadcasted_iota(jnp.int32, x.shape, 0)
        df = jnp.where(row >= pad, rc * jax.nn.sigmoid(-x), 0.0)
        df_ref[...] = df.astype(_MXU)
        db_ref[...] += jnp.sum(df, axis=0, keepdims=True)

    return pl.pallas_call(
        body, name=name,
        out_shape=(jax.ShapeDtypeStruct((L, LANES), _MXU), jax.ShapeDtypeStruct((1, LANES), F32)),
        grid=(n,),
        in_specs=[pl.BlockSpec((ROW_TILE, LANES), lambda i: (n - 1 - i, 0)),
                  pl.BlockSpec((ROW_TILE, LANES), lambda i: (n - 1 - i, off)), _whole((1, LANES))],
        out_specs=(pl.BlockSpec((ROW_TILE, LANES), lambda i: (n - 1 - i, 0)), _whole((1, LANES))),
        scratch_shapes=[pltpu.VMEM((1, LANES), F32)],
        compiler_params=_params(("arbitrary",)),
    )(dcum, zf, bias)


LOG2E = 1.4426950408889634
BWD_UNROLL = 8
FWD_UNROLL = 4


def _bias_terms(ct, pad):
    H, L = ct.shape
    T = ROW_TILE
    n = L // T
    c2 = ct * LOG2E
    k2 = jnp.where(jnp.arange(L)[None, :] < pad, -NEG, c2)
    return dict(cq_row=c2.reshape(H, n, 1, T), ck_row=k2.reshape(H, n, 1, T), ref=c2[:, ::T].reshape(H, n, 1, 1))


def _exchange_copies(srcs, outs, send_sems, recv_sems, local_sems, scatter):
    nt = len(srcs)
    x, y, c = lax.axis_index("x"), lax.axis_index("y"), lax.axis_index("c")
    me = 4 * x + 2 * y + c
    copies = [pltpu.make_async_copy(srcs[t].at[me] if scatter else srcs[t], outs[t].at[me], local_sems.at[t])
              for t in range(nt)]
    for k in range(1, N_DEV):
        px = (x + (k >> 2)) % 2
        py = (y + ((k >> 1) & 1)) % 2
        pc = (c + (k & 1)) % 2
        peer = 4 * px + 2 * py + pc
        for t in range(nt):
            copies.append(pltpu.make_async_remote_copy(
                src_ref=srcs[t].at[peer] if scatter else srcs[t],
                dst_ref=outs[t].at[me],
                send_sem=send_sems.at[k - 1, t], recv_sem=recv_sems.at[k - 1, t],
                device_id=(px, py, pc), device_id_type=pl.DeviceIdType.MESH))
    return copies


def _exchange_shapes(tensors, scatter):
    out_shape = tuple(jax.ShapeDtypeStruct(t.shape if scatter else (N_DEV,) + t.shape, t.dtype) for t in tensors)
    nt = len(tensors)
    sems = [pltpu.SemaphoreType.DMA((N_DEV - 1, nt)), pltpu.SemaphoreType.DMA((N_DEV - 1, nt)),
            pltpu.SemaphoreType.DMA((nt,))]
    return out_shape, sems


def _side_exchange(side, srcs, outs, sems, first, last):
    if not side:
        return

    @pl.when(first)
    def _():
        for cp in _exchange_copies(srcs, outs, *sems, side[1]):
            cp.start()

    @pl.when(last)
    def _():
        for cp in _exchange_copies(srcs, outs, *sems, side[1]):
            cp.wait()


def _as_row(col):
    return jnp.transpose(jnp.broadcast_to(col, (col.shape[0], LANES)))[0:1, :]


def _as_col(row):
    return jnp.transpose(jnp.broadcast_to(row, (LANES, row.shape[1])))[:, 0:1]


def _attn_fwd(q, qoff, k, koff, v, voff, bias, heads, dk, dv, scale, name, dead, side=None):
    L = q.shape[0]
    T = ROW_TILE
    n = L // T
    c = scale * LOG2E
    dn_qk = (((1,), (1,)), ((), ()))

    ns = len(side[0]) if side else 0

    def body(*refs):
        q_ref, k_ref, v_ref, ck_ref, ref_ref = refs[:5]
        o_ref, lse_ref = refs[5 + ns:7 + ns]
        s_a, s_b = refs[7 + 2 * ns:9 + 2 * ns]
        i = pl.program_id(1)
        _side_exchange(side, refs[5:5 + ns], refs[7 + ns:7 + 2 * ns], refs[9 + 2 * ns:],
                       (pl.program_id(0) == 0) & (i == 0), (pl.program_id(0) == heads - 1) & (i == n - 1))
        qb = q_ref[...]
        ref = ref_ref[...]

        def scores(j, dst):
            start = pl.multiple_of(j * T, T)
            dst[...] = lax.dot_general(qb, k_ref[pl.ds(start, T), :], dn_qk, preferred_element_type=F32)

        def update(t, vj, carry):
            m, l, acc = carry
            m_new = jnp.maximum(m, jnp.max(t, axis=1, keepdims=True))
            p = jnp.exp2(t - m_new)
            a = jnp.exp2(m - m_new)
            l = a * l + jnp.sum(p, axis=1, keepdims=True)
            acc = a * acc + jnp.dot(p.astype(_MXU), vj, preferred_element_type=F32)
            return m_new, l, acc

        def soft(j, carry, cur, diag):
            start = pl.multiple_of(j * T, T)
            t = cur[...] * c - (ck_ref[j] - ref)
            if diag:
                r = lax.broadcasted_iota(jnp.int32, (T, T), 0)
                cc = lax.broadcasted_iota(jnp.int32, (T, T), 1)
                t = jnp.where(cc <= r, t, NEG)
            return update(t, v_ref[pl.ds(start, T), :], carry)

        def head_tile(carry):
            t = lax.dot_general(qb, k_ref[dead:T, :], dn_qk, preferred_element_type=F32) * c
            return update(t - (ck_ref[0][:, dead:] - ref), v_ref[dead:T, :], carry)

        def step(j, carry, cur, nxt):
            scores(j + 1, nxt)
            return soft(j, carry, cur, False)

        first = jnp.where(i > 0, 1, 0) if dead else 0

        def run_of(j0, steps, carry):
            for u in range(0, steps, 2):
                carry = step(j0 + u + 1, step(j0 + u, carry, s_a, s_b), s_b, s_a)
            return carry

        scores(first, s_a)
        carry = (jnp.full((T, 1), -jnp.inf, F32), jnp.zeros((T, 1), F32), jnp.zeros((T, dv), F32))
        if dead:
            carry = lax.cond(i > 0, head_tile, lambda cy: cy, carry)
        whole = i - first
        big = whole // FWD_UNROLL
        carry = lax.fori_loop(0, big, lambda g, cy: run_of(first + FWD_UNROLL * g, FWD_UNROLL, cy), carry)
        done = first + FWD_UNROLL * big
        carry = lax.fori_loop(0, (i - done) // 2, lambda g, cy: run_of(done + 2 * g, 2, cy), carry)
        odd = ((i - done) % 2) == 1
        carry = lax.cond(odd, lambda cy: step(i - 1, cy, s_a, s_b), lambda cy: cy, carry)
        m, l, acc = lax.cond(odd, lambda cy: soft(i, cy, s_b, True), lambda cy: soft(i, cy, s_a, True), carry)
        o_ref[...] = acc / l
        lse_ref[...] = _as_row(m + jnp.log2(l))

    hbm = pl.BlockSpec(memory_space=pl.ANY)
    side_shapes, side_sems = _exchange_shapes(*side) if side else ((), [])
    o, lse, *side_out = pl.pallas_call(
        body, name=name,
        out_shape=(jax.ShapeDtypeStruct((L, heads * dv), F32), jax.ShapeDtypeStruct((heads, n, 1, T), F32))
        + side_shapes,
        grid=(heads, n),
        in_specs=[pl.BlockSpec((T, dk), lambda h, i: (i, qoff + h)),
                  pl.BlockSpec((L, dk), lambda h, i: (0, koff + h)),
                  pl.BlockSpec((L, dv), lambda h, i: (0, voff + h)),
                  pl.BlockSpec((None, n, 1, T), lambda h, i: (h, 0, 0, 0)),
                  pl.BlockSpec((None, None, 1, 1), lambda h, i: (h, i, 0, 0))] + [hbm] * ns,
        out_specs=(pl.BlockSpec((T, dv), lambda h, i: (i, h)),
                   pl.BlockSpec((None, None, 1, T), lambda h, i: (h, i, 0, 0))) + (hbm,) * ns,
        scratch_shapes=[pltpu.VMEM((T, T), F32), pltpu.VMEM((T, T), F32)] + side_sems,
        compiler_params=_params(("arbitrary", "arbitrary")),
    )(q, k, v, bias["ck_row"], bias["ref"], *(side[0] if side else ()))
    return o, lse + (bias["cq_row"] - bias["ref"]), side_out


def _attn_bwd(q, qoff, k, koff, v, voff, do, bias, lse_row, delta_row, heads, dk, dv, scale, name, dead,
              side=None, bias_grads=True):
    L = q.shape[0]
    T = ROW_TILE
    n = L // T
    c = scale * LOG2E
    dn_nt = (((1,), (1,)), ((), ()))
    dn_tn = (((0,), (0,)), ((), ()))
    ns = len(side[0]) if side else 0

    def body(*refs):
        q_ref, k_ref, v_ref, do_ref, cq_ref, ck_ref, ref_ref, lse_ref, dl_ref = refs[:9]
        dq_hbm, dk_ref, dv_ref, dck_ref, dcq_ref = refs[9 + ns:14 + ns]
        dq_acc, sem, st_a, dp_a, st_b, dp_b = refs[14 + 2 * ns:20 + 2 * ns]
        h = pl.program_id(0)
        j = pl.program_id(1)
        _side_exchange(side, refs[9:9 + ns], refs[14 + ns:14 + 2 * ns], refs[20 + 2 * ns:],
                       (h == 0) & (j == 0), (h == heads - 1) & (j == n - 1))

        @pl.when(j == 0)
        def _():
            dq_acc[...] = jnp.zeros_like(dq_acc)
            dcq_ref[...] = jnp.zeros_like(dcq_ref)

        ref = ref_ref[...]

        def run(k0, R):
            kb = k_ref[k0:k0 + R, :]
            vb = v_ref[k0:k0 + R, :]
            bcol = _as_col(ck_ref[j][:, k0:k0 + R] - ref)

            def front(i, st_dst, dp_dst):
                start = pl.multiple_of(jnp.minimum(i, n - 1) * T, T)
                st_dst[0:R, :] = lax.dot_general(kb, q_ref[pl.ds(start, T), :], dn_nt, preferred_element_type=F32)
                dp_dst[0:R, :] = lax.dot_general(vb, do_ref[pl.ds(start, T), :], dn_nt, preferred_element_type=F32)

            def back(i, carry, st_cur, dp_cur, diag):
                dk_a, dv_a, dck_a = carry
                start = pl.multiple_of(i * T, T)
                qi = q_ref[pl.ds(start, T), :]
                doi = do_ref[pl.ds(start, T), :]
                arow = (cq_ref[i] - ref) - lse_ref[i]
                st = st_cur[0:R, :] * c + arow - bcol
                if diag:
                    r = lax.broadcasted_iota(jnp.int32, (R, T), 0) + k0
                    cc = lax.broadcasted_iota(jnp.int32, (R, T), 1)
                    st = jnp.where(r <= cc, st, NEG)
                pt = jnp.exp2(st)
                dv_a = dv_a + jnp.dot(pt.astype(_MXU), doi, preferred_element_type=F32)
                dst = pt * (dp_cur[0:R, :] - dl_ref[i])
                if bias_grads:
                    dck_a = dck_a - jnp.sum(dst, axis=1, keepdims=True)
                    dcq_ref[i] += jnp.sum(dst, axis=0, keepdims=True)
                dsb = (dst * scale).astype(_MXU)
                dk_a = dk_a + jnp.dot(dsb, qi, preferred_element_type=F32)
                dq_acc[pl.ds(start, T), :] += lax.dot_general(dsb, kb, dn_tn, preferred_element_type=F32)
                return dk_a, dv_a, dck_a

            buf_a, buf_b = (st_a, dp_a), (st_b, dp_b)

            def step(i, carry, cur, nxt, diag=False):
                front(i + 1, *nxt)
                return back(i, carry, *cur, diag)

            def run_of(i0, steps, carry):
                for u in range(0, steps, 2):
                    carry = step(i0 + u + 1, step(i0 + u, carry, buf_b, buf_a), buf_a, buf_b)
                return carry

            front(j, *buf_a)
            init = (jnp.zeros((R, dk), F32), jnp.zeros((R, dv), F32), jnp.zeros((R, 1), F32))
            carry = step(j, init, buf_a, buf_b, True)
            big = (n - 1 - j) // BWD_UNROLL
            carry = lax.fori_loop(0, big, lambda g, cy: run_of(j + 1 + BWD_UNROLL * g, BWD_UNROLL, cy), carry)
            done = j + 1 + BWD_UNROLL * big
            carry = lax.fori_loop(0, (n - done) // 2, lambda g, cy: run_of(done + 2 * g, 2, cy), carry)
            dk_a, dv_a, dck_a = lax.cond(((n - done) % 2) == 1, lambda cy: step(n - 1, cy, buf_b, buf_a),
                                         lambda cy: cy, carry)
            if k0:
                dk_ref[0:k0, :] = jnp.zeros((k0, dk), F32)
                dv_ref[0:k0, :] = jnp.zeros((k0, dv), F32)
                dck_ref[:, 0:k0] = jnp.zeros((1, k0), F32)
            dk_ref[k0:k0 + R, :] = dk_a
            dv_ref[k0:k0 + R, :] = dv_a
            dck_ref[:, k0:k0 + R] = _as_row(dck_a)

        if dead:
            pl.when(j == 0)(lambda: run(dead, T - dead))
            pl.when(j > 0)(lambda: run(0, T))
        else:
            run(0, T)

        @pl.when(j == n - 1)
        def _():
            cp = pltpu.make_async_copy(dq_acc, dq_hbm.at[:, pl.ds(pl.multiple_of(h * dk, dk), dk)], sem)
            cp.start()
            cp.wait()

    hbm = pl.BlockSpec(memory_space=pl.ANY)
    side_shapes, side_sems = _exchange_shapes(*side) if side else ((), [])
    dq, dk_, dv_, dck, dcq, *side_out = pl.pallas_call(
        body, name=name,
        out_shape=(jax.ShapeDtypeStruct((L, heads * dk), F32), jax.ShapeDtypeStruct((L, heads * dk), F32),
                   jax.ShapeDtypeStruct((L, heads * dv), F32), jax.ShapeDtypeStruct((heads, n, 1, T), F32),
                   jax.ShapeDtypeStruct((heads, n, 1, T), F32)) + side_shapes,
        grid=(heads, n),
        in_specs=[pl.BlockSpec((L, dk), lambda h, j: (0, qoff + h)),
                  pl.BlockSpec((T, dk), lambda h, j: (j, koff + h)),
                  pl.BlockSpec((T, dv), lambda h, j: (j, voff + h)),
                  pl.BlockSpec((L, dv), lambda h, j: (0, h)),
                  pl.BlockSpec((None, n, 1, T), lambda h, j: (h, 0, 0, 0)),
                  pl.BlockSpec((None, n, 1, T), lambda h, j: (h, 0, 0, 0)),
                  pl.BlockSpec((None, None, 1, 1), lambda h, j: (h, j, 0, 0)),
                  pl.BlockSpec((None, n, 1, T), lambda h, j: (h, 0, 0, 0)),
                  pl.BlockSpec((None, n, 1, T), lambda h, j: (h, 0, 0, 0))] + [hbm] * ns,
        out_specs=(hbm,
                   pl.BlockSpec((T, dk), lambda h, j: (j, h)),
                   pl.BlockSpec((T, dv), lambda h, j: (j, h)),
                   pl.BlockSpec((None, None, 1, T), lambda h, j: (h, j, 0, 0)),
                   pl.BlockSpec((None, n, 1, T), lambda h, j: (h, 0, 0, 0))) + (hbm,) * ns,
        scratch_shapes=[pltpu.VMEM((L, dk), F32), pltpu.SemaphoreType.DMA] + [pltpu.VMEM((T, T), F32)] * 4
        + side_sems,
        compiler_params=_params(("arbitrary", "arbitrary")),
    )(q, k, v, do, bias["cq_row"], bias["ck_row"], bias["ref"], lse_row, delta_row, *(side[0] if side else ()))
    return dq, dk_, dv_, dck, dcq, side_out


_CQ0, _CKV0, _KR0 = D_MODEL, D_MODEL + MLA_Q_LORA, D_MODEL + MLA_Q_LORA + MLA_KV_LORA
_MLA_PROJ = _KR0 + LANES


def _rms(x, g):
    ms = jnp.mean(x * x, axis=-1, keepdims=True)
    return x * lax.rsqrt(ms + NORM_EPS) * g


def _rms_bwd(x, g, dy):
    ms = jnp.mean(x * x, axis=-1, keepdims=True)
    r = lax.rsqrt(ms + NORM_EPS)
    xh = x * r
    dxh = dy * g
    dx = r * (dxh - xh * jnp.mean(dxh * xh, axis=-1, keepdims=True))
    return dx, jnp.sum(dy * xh, axis=0, keepdims=True)


def _mla_norm_fwd(proj, gq, gkv):
    L = proj.shape[0]

    def body(p_ref, gq_ref, gkv_ref, cq_ref, ckv_ref):
        cq_ref[...] = _rms(p_ref[:, _CQ0:_CKV0], gq_ref[...]).astype(_MXU)
        ckv_ref[...] = _rms(p_ref[:, _CKV0:_KR0], gkv_ref[...]).astype(_MXU)

    return pl.pallas_call(
        body, name="mla_norm_fwd",
        out_shape=(jax.ShapeDtypeStruct((L, MLA_Q_LORA), _MXU), jax.ShapeDtypeStruct((L, MLA_KV_LORA), _MXU)),
        grid=(L // ROW_TILE,),
        in_specs=[_rows(_MLA_PROJ), _whole((1, MLA_Q_LORA)), _whole((1, MLA_KV_LORA))],
        out_specs=(_rows(MLA_Q_LORA), _rows(MLA_KV_LORA)),
        compiler_params=_params(("parallel",)),
    )(proj, gq, gkv)


def _mla_norm_bwd(proj, gq, gkv, dcqn, dckvn):
    L = proj.shape[0]

    def body(p_ref, gq_ref, gkv_ref, dq_ref, dkv_ref, dcq_ref, dckv_ref, dgq_ref, dgkv_ref):
        @pl.when(pl.program_id(0) == 0)
        def _():
            dgq_ref[...] = jnp.zeros_like(dgq_ref)
            dgkv_ref[...] = jnp.zeros_like(dgkv_ref)

        dx, dg = _rms_bwd(p_ref[:, _CQ0:_CKV0], gq_ref[...], dq_ref[...])
        dcq_ref[...] = dx.astype(_MXU)
        dgq_ref[...] += dg
        dx, dg = _rms_bwd(p_ref[:, _CKV0:_KR0], gkv_ref[...], dkv_ref[...])
        dckv_ref[...] = dx.astype(_MXU)
        dgkv_ref[...] += dg

    return pl.pallas_call(
        body, name="mla_norm_bwd",
        out_shape=(jax.ShapeDtypeStruct((L, MLA_Q_LORA), _MXU), jax.ShapeDtypeStruct((L, MLA_KV_LORA), _MXU),
                   jax.ShapeDtypeStruct((1, MLA_Q_LORA), F32), jax.ShapeDtypeStruct((1, MLA_KV_LORA), F32)),
        grid=(L // ROW_TILE,),
        in_specs=[_rows(_MLA_PROJ), _whole((1, MLA_Q_LORA)), _whole((1, MLA_KV_LORA)),
                  _rows(MLA_Q_LORA), _rows(MLA_KV_LORA)],
        out_specs=(_rows(MLA_Q_LORA), _rows(MLA_KV_LORA), _whole((1, MLA_Q_LORA)), _whole((1, MLA_KV_LORA))),
        compiler_params=_params(("arbitrary",)),
    )(proj, gq, gkv, dcqn, dckvn)


def _rot_tile(t, cos, sa, sb):
    half = MLA_ROPE // 2
    return t * cos + pltpu.roll(t, LANES - half, 1) * sa + pltpu.roll(t, half, 1) * sb


def _mla_rope_fwd(qf, kv, proj, cos, sa, sb):
    L = qf.shape[0]
    H = MLA_HEADS

    def body(q_ref, kv_ref, p_ref, c_ref, sa_ref, sb_ref, qo_ref, ko_ref):
        cos, sa, sb = c_ref[...], sa_ref[...], sb_ref[...]
        kr = _rot_tile(p_ref[...], cos, sa, sb).astype(_MXU)
        for h in range(H):
            b = h * MLA_DK
            qo_ref[:, b:b + LANES] = q_ref[:, b:b + LANES].astype(_MXU)
            qo_ref[:, b + LANES:b + 2 * LANES] = _rot_tile(q_ref[:, b + LANES:b + 2 * LANES], cos, sa, sb).astype(_MXU)
            ko_ref[:, b:b + LANES] = kv_ref[:, h * LANES:(h + 1) * LANES]
            ko_ref[:, b + LANES:b + 2 * LANES] = kr

    W = H * MLA_DK
    return pl.pallas_call(
        body, name="mla_rope_fwd",
        out_shape=(jax.ShapeDtypeStruct((L, W), _MXU), jax.ShapeDtypeStruct((L, W), _MXU)),
        grid=(L // ROW_TILE,),
        in_specs=[_rows(W), _rows(kv.shape[1]), pl.BlockSpec((ROW_TILE, LANES), lambda i: (i, _KR0 // LANES)),
                  _rows(LANES), _rows(LANES), _rows(LANES)],
        out_specs=(_rows(W), _rows(W)),
        compiler_params=_params(("parallel",)),
    )(qf, kv, proj, cos, sa, sb)


def _mla_rope_bwd(dq_full, dk_full, cos, sa, sb):
    L = dq_full.shape[0]
    H = MLA_HEADS
    W = H * MLA_DK

    def body(dq_ref, dk_ref, c_ref, sa_ref, sb_ref, dqf_ref, dkn_ref, dkr_ref):
        cos, sa, sb = c_ref[...], sa_ref[...], sb_ref[...]
        lane = lax.broadcasted_iota(jnp.int32, (ROW_TILE, LANES), 1)
        live = lane < MLA_ROPE
        krs = jnp.zeros((ROW_TILE, LANES), F32)
        for h in range(H):
            b = h * MLA_DK
            dqf_ref[:, b:b + LANES] = dq_ref[:, b:b + LANES].astype(_MXU)
            dqr = _rot_tile(dq_ref[:, b + LANES:b + 2 * LANES], cos, sa, sb)
            dqf_ref[:, b + LANES:b + 2 * LANES] = jnp.where(live, dqr, 0.0).astype(_MXU)
            dkn_ref[:, h * LANES:(h + 1) * LANES] = dk_ref[:, b:b + LANES].astype(_MXU)
            krs = krs + dk_ref[:, b + LANES:b + 2 * LANES]
        dkr_ref[...] = jnp.where(live, _rot_tile(krs, cos, sa, sb), 0.0).astype(_MXU)

    return pl.pallas_call(
        body, name="mla_rope_bwd",
        out_shape=(jax.ShapeDtypeStruct((L, W), _MXU), jax.ShapeDtypeStruct((L, H * LANES), _MXU),
                   jax.ShapeDtypeStruct((L, LANES), _MXU)),
        grid=(L // ROW_TILE,),
        in_specs=[_rows(W), _rows(W), _rows(LANES), _rows(LANES), _rows(LANES)],
        out_specs=(_rows(W), _rows(H * LANES), _rows(LANES)),
        compiler_params=_params(("parallel",)),
    )(dq_full, dk_full, cos, sa, sb)


_RET_QKW = RET_HEADS * RET_QK


def _ret_rope_fwd(qkz, cos, sin, pad):
    L = qkz.shape[0]
    hh = RET_QK // 2
    kscale = RET_QK ** -0.5

    def body(p_ref, c_ref, s_ref, q_ref, k_ref):
        cos, sin = c_ref[...], s_ref[...]
        row = pl.program_id(0) * ROW_TILE + lax.broadcasted_iota(jnp.int32, (ROW_TILE, hh), 0)
        keep = row >= pad
        for h in range(RET_HEADS):
            b = h * RET_QK
            t1, t2 = p_ref[:, b:b + hh], p_ref[:, b + hh:b + 2 * hh]
            q_ref[:, b:b + hh] = (t1 * cos - t2 * sin).astype(_MXU)
            q_ref[:, b + hh:b + 2 * hh] = (t2 * cos + t1 * sin).astype(_MXU)
            t1, t2 = p_ref[:, _RET_QKW + b:_RET_QKW + b + hh], p_ref[:, _RET_QKW + b + hh:_RET_QKW + b + 2 * hh]
            k_ref[:, b:b + hh] = jnp.where(keep, (t1 * cos - t2 * sin) * kscale, 0.0)
            k_ref[:, b + hh:b + 2 * hh] = jnp.where(keep, (t2 * cos + t1 * sin) * kscale, 0.0)

    return pl.pallas_call(
        body, name="ret_rope_fwd",
        out_shape=(jax.ShapeDtypeStruct((L, _RET_QKW), _MXU), jax.ShapeDtypeStruct((L, _RET_QKW), F32)),
        grid=(L // ROW_TILE,),
        in_specs=[_rows(2 * _RET_QKW), _rows(hh), _rows(hh)],
        out_specs=(_rows(_RET_QKW), _rows(_RET_QKW)),
        compiler_params=_params(("parallel",)),
    )(qkz, cos, sin)


def _ret_rope_bwd(dqr, dkr, cos, sin, pad):
    L = dqr.shape[0]
    hh = RET_QK // 2
    kscale = RET_QK ** -0.5

    def body(dq_ref, dk_ref, c_ref, s_ref, oq_ref, ok_ref):
        cos, sin = c_ref[...], s_ref[...]
        row = pl.program_id(0) * ROW_TILE + lax.broadcasted_iota(jnp.int32, (ROW_TILE, hh), 0)
        keep = row >= pad
        for h in range(RET_HEADS):
            b = h * RET_QK
            d1, d2 = dq_ref[:, b:b + hh], dq_ref[:, b + hh:b + 2 * hh]
            oq_ref[:, b:b + hh] = (d1 * cos + d2 * sin).astype(_MXU)
            oq_ref[:, b + hh:b + 2 * hh] = (d2 * cos - d1 * sin).astype(_MXU)
            d1, d2 = dk_ref[:, b:b + hh], dk_ref[:, b + hh:b + 2 * hh]
            ok_ref[:, b:b + hh] = jnp.where(keep, (d1 * cos + d2 * sin) * kscale, 0.0).astype(_MXU)
            ok_ref[:, b + hh:b + 2 * hh] = jnp.where(keep, (d2 * cos - d1 * sin) * kscale, 0.0).astype(_MXU)

    return pl.pallas_call(
        body, name="ret_rope_bwd",
        out_shape=(jax.ShapeDtypeStruct((L, _RET_QKW), _MXU), jax.ShapeDtypeStruct((L, _RET_QKW), _MXU)),
        grid=(L // ROW_TILE,),
        in_specs=[_rows(_RET_QKW), _rows(_RET_QKW), _rows(hh), _rows(hh)],
        out_specs=(_rows(_RET_QKW), _rows(_RET_QKW)),
        compiler_params=_params(("parallel",)),
    )(dqr, dkr, cos, sin)


def _ret_decays():
    lg = np.log1p(-np.exp2(-5.0 - np.arange(RET_HEADS, dtype=np.float32))).astype(np.float32)
    i = np.arange(CHUNK, dtype=np.float32)
    rel = i[:, None] - i[None, :]
    dm = np.where(rel[None] >= 0, np.exp(rel[None] * lg[:, None, None]), 0.0).astype(np.float32)
    qd = np.exp((i[None, :] + 1.0) * lg[:, None]).astype(np.float32)[:, :, None]
    kd = np.exp((CHUNK - 1.0 - i)[None, :] * lg[:, None]).astype(np.float32)[:, :, None]
    cd = np.broadcast_to(np.exp(CHUNK * lg).astype(np.float32)[:, None, None], (RET_HEADS, 1, RET_V))
    return jnp.asarray(dm), jnp.asarray(qd), jnp.asarray(kd), jnp.asarray(np.ascontiguousarray(cd))


def _ret_fwd(qr, kr, v):
    L = qr.shape[0]
    nc = L // CHUNK
    G = ROW_TILE // CHUNK
    dm, qd, kd, cd = _ret_decays()
    dn_nt = (((1,), (1,)), ((), ()))
    dn_tn = (((0,), (0,)), ((), ()))

    def body(q_ref, k_ref, v_ref, dm_ref, qd_ref, kd_ref, cd_ref, o_ref, st_ref, s_ref):
        @pl.when(pl.program_id(1) == 0)
        def _():
            s_ref[...] = jnp.zeros_like(s_ref)

        state = s_ref[...]
        for g in range(G):
            rows = slice(g * CHUNK, (g + 1) * CHUNK)
            q, k, vv = q_ref[rows, :], k_ref[rows, :], v_ref[rows, :]
            sb = state.astype(_MXU)
            st_ref[g] = sb
            s = lax.dot_general(q, k.astype(_MXU), dn_nt, preferred_element_type=F32) * dm_ref[...]
            intra = jnp.dot(s.astype(_MXU), vv, preferred_element_type=F32)
            cross = jnp.dot(q, sb, preferred_element_type=F32) * qd_ref[...]
            o_ref[rows, :] = intra + cross
            kdm = (k * kd_ref[...]).astype(_MXU)
            state = cd_ref[...] * state + lax.dot_general(kdm, vv, dn_tn, preferred_element_type=F32)
        s_ref[...] = state

    return pl.pallas_call(
        body, name="ret_fwd",
        out_shape=(jax.ShapeDtypeStruct((L, RET_HEADS * RET_V), F32),
                   jax.ShapeDtypeStruct((RET_HEADS, nc, RET_QK, RET_V), _MXU)),
        grid=(RET_HEADS, nc // G),
        in_specs=[pl.BlockSpec((ROW_TILE, RET_QK), lambda h, c: (c, h)),
                  pl.BlockSpec((ROW_TILE, RET_QK), lambda h, c: (c, h)),
                  pl.BlockSpec((ROW_TILE, RET_V), lambda h, c: (c, h)),
                  pl.BlockSpec((None, CHUNK, CHUNK), lambda h, c: (h, 0, 0)),
                  pl.BlockSpec((None, CHUNK, 1), lambda h, c: (h, 0, 0)),
                  pl.BlockSpec((None, CHUNK, 1), lambda h, c: (h, 0, 0)),
                  pl.BlockSpec((None, 1, RET_V), lambda h, c: (h, 0, 0))],
        out_specs=(pl.BlockSpec((ROW_TILE, RET_V), lambda h, c: (c, h)),
                   pl.BlockSpec((None, G, RET_QK, RET_V), lambda h, c: (h, c, 0, 0))),
        scratch_shapes=[pltpu.VMEM((RET_QK, RET_V), F32)],
        compiler_params=_params(("parallel", "arbitrary")),
    )(qr, kr, v, dm, qd, kd, cd)


def _ret_bwd(qr, kr, v, do, states):
    L = qr.shape[0]
    nc = L // CHUNK
    G = ROW_TILE // CHUNK
    nt = nc // G
    dm, qd, kd, cd = _ret_decays()
    dn_nt = (((1,), (1,)), ((), ()))
    dn_tn = (((0,), (0,)), ((), ()))

    def body(q_ref, k_ref, v_ref, do_ref, st_ref, dm_ref, qd_ref, kd_ref, cd_ref,
             dq_ref, dk_ref, dv_ref, ds_ref):
        @pl.when(pl.program_id(1) == 0)
        def _():
            ds_ref[...] = jnp.zeros_like(ds_ref)

        dstate = ds_ref[...]
        dmat = dm_ref[...]
        for g in reversed(range(G)):
            rows = slice(g * CHUNK, (g + 1) * CHUNK)
            q, k, vv, do = q_ref[rows, :], k_ref[rows, :], v_ref[rows, :], do_ref[rows, :]
            kb = k.astype(_MXU)
            dob = do.astype(_MXU)
            doq = (do * qd_ref[...]).astype(_MXU)
            dsb = dstate.astype(_MXU)
            s = (lax.dot_general(q, kb, dn_nt, preferred_element_type=F32) * dmat).astype(_MXU)
            draw = (lax.dot_general(dob, vv, dn_nt, preferred_element_type=F32) * dmat).astype(_MXU)
            kdm = (k * kd_ref[...]).astype(_MXU)
            dv = (lax.dot_general(s, dob, dn_tn, preferred_element_type=F32)
                  + jnp.dot(kdm, dsb, preferred_element_type=F32))
            dq = (jnp.dot(draw, kb, preferred_element_type=F32)
                  + lax.dot_general(doq, st_ref[g], dn_nt, preferred_element_type=F32))
            dk = (lax.dot_general(draw, q, dn_tn, preferred_element_type=F32)
                  + lax.dot_general(vv, dsb, dn_nt, preferred_element_type=F32) * kd_ref[...])
            dq_ref[rows, :] = dq
            dk_ref[rows, :] = dk
            dv_ref[rows, :] = dv.astype(_MXU)
            dstate = cd_ref[...] * dstate + lax.dot_general(q, doq, dn_tn, preferred_element_type=F32)
        ds_ref[...] = dstate

    rev = lambda h, c: (nt - 1 - c, h)
    return pl.pallas_call(
        body, name="ret_bwd",
        out_shape=(jax.ShapeDtypeStruct((L, _RET_QKW), F32), jax.ShapeDtypeStruct((L, _RET_QKW), F32),
                   jax.ShapeDtypeStruct((L, RET_HEADS * RET_V), _MXU)),
        grid=(RET_HEADS, nt),
        in_specs=[pl.BlockSpec((ROW_TILE, RET_QK), rev),
                  pl.BlockSpec((ROW_TILE, RET_QK), rev),
                  pl.BlockSpec((ROW_TILE, RET_V), rev),
                  pl.BlockSpec((ROW_TILE, RET_V), rev),
                  pl.BlockSpec((None, G, RET_QK, RET_V), lambda h, c: (h, nt - 1 - c, 0, 0)),
                  pl.BlockSpec((None, CHUNK, CHUNK), lambda h, c: (h, 0, 0)),
                  pl.BlockSpec((None, CHUNK, 1), lambda h, c: (h, 0, 0)),
                  pl.BlockSpec((None, CHUNK, 1), lambda h, c: (h, 0, 0)),
                  pl.BlockSpec((None, 1, RET_V), lambda h, c: (h, 0, 0))],
        out_specs=(pl.BlockSpec((ROW_TILE, RET_QK), rev), pl.BlockSpec((ROW_TILE, RET_QK), rev),
                   pl.BlockSpec((ROW_TILE, RET_V), rev)),
        scratch_shapes=[pltpu.VMEM((RET_QK, RET_V), F32)],
        compiler_params=_params(("parallel", "arbitrary")),
    )(qr, kr, v, do, states, dm, qd, kd, cd)


def _gn(o, g):
    mu = jnp.mean(o, axis=-1, keepdims=True)
    oc = o - mu
    var = jnp.mean(oc * oc, axis=-1, keepdims=True)
    return oc * lax.rsqrt(var + NORM_EPS) * g


def _ret_gate_fwd(o, qkz, gn_g):
    L, W = o.shape
    zblk = qkz.shape[1] // W - 1

    def body(o_ref, z_ref, g_ref, y_ref):
        for h in range(RET_HEADS):
            sl = slice(h * RET_V, (h + 1) * RET_V)
            y_ref[:, sl] = (_gn(o_ref[:, sl], g_ref[:, sl]) * _silu(z_ref[:, sl])).astype(_MXU)

    return pl.pallas_call(
        body, name="ret_gate_fwd",
        out_shape=jax.ShapeDtypeStruct((L, W), _MXU),
        grid=(L // ROW_TILE,),
        in_specs=[_rows(W), pl.BlockSpec((ROW_TILE, W), lambda i: (i, zblk)), _whole((1, W))],
        out_specs=_rows(W),
        compiler_params=_params(("parallel",)),
    )(o, qkz, gn_g)


def _ret_gate_bwd(dy, o, qkz, gn_g):
    L, W = o.shape
    zblk = qkz.shape[1] // W - 1

    def body(dy_ref, o_ref, z_ref, g_ref, do_ref, dz_ref, dg_ref):
        @pl.when(pl.program_id(0) == 0)
        def _():
            dg_ref[...] = jnp.zeros_like(dg_ref)

        for h in range(RET_HEADS):
            sl = slice(h * RET_V, (h + 1) * RET_V)
            z = z_ref[:, sl]
            o = o_ref[:, sl]
            g = g_ref[:, sl]
            dy = dy_ref[:, sl]
            mu = jnp.mean(o, axis=-1, keepdims=True)
            oc = o - mu
            var = jnp.mean(oc * oc, axis=-1, keepdims=True)
            r = lax.rsqrt(var + NORM_EPS)
            xh = oc * r
            dn = dy * _silu(z)
            dz_ref[:, sl] = (dy * (xh * g) * _dsilu(z)).astype(_MXU)
            dg_ref[:, sl] += jnp.sum(dn * xh, axis=0, keepdims=True)
            dxh = dn * g
            do_ref[:, sl] = r * (dxh - jnp.mean(dxh, axis=-1, keepdims=True)
                                 - xh * jnp.mean(dxh * xh, axis=-1, keepdims=True))

    return pl.pallas_call(
        body, name="ret_gate_bwd",
        out_shape=(jax.ShapeDtypeStruct((L, W), F32), jax.ShapeDtypeStruct((L, W), _MXU),
                   jax.ShapeDtypeStruct((1, W), F32)),
        grid=(L // ROW_TILE,),
        in_specs=[_rows(W), _rows(W), pl.BlockSpec((ROW_TILE, W), lambda i: (i, zblk)), _whole((1, W))],
        out_specs=(_rows(W), _rows(W), _whole((1, W))),
        compiler_params=_params(("arbitrary",)),
    )(dy, o, qkz, gn_g)


def _dead_rows(pad):
    return pad // LANES * LANES


def _heads_t(a):
    return a[:, :FOX_HEADS].T


def _fox_fwd(h, w, lnw, pad, tag, side=None):
    w_in, b_f, w_out = w
    L = h.shape[0]
    W = D_MODEL
    w_qkv = w_in[:, :3 * W]
    w_zf = jnp.concatenate([w_in[:, 3 * W:4 * W], jnp.pad(w_in[:, 4 * W:], ((0, 0), (0, LANES - FOX_HEADS)))], axis=1)
    bias = jnp.pad(b_f, (0, LANES - FOX_HEADS))[None, :]
    qkv = _matmul(h, w_qkv, "nn", _MXU, f"{tag}_qkv")
    zf = _matmul(h, w_zf, "nn", F32, f"{tag}_zf")
    cum = _fox_cum(zf, bias, pad, f"{tag}_cum")
    ab = _bias_terms(_heads_t(cum), pad)
    nb = W // LANES
    o, lse, side_out = _attn_fwd(qkv, 0, qkv, nb, qkv, 2 * nb, ab, FOX_HEADS, LANES, LANES, LANES ** -0.5,
                                 f"{tag}_attn_fwd", _dead_rows(pad), side)
    y = _gate_fwd(o, zf, f"{tag}_gate_fwd")
    mo = _matmul(y, w_out, "nn", F32, f"{tag}_out")
    zl, hn = _ln_fwd(h, mo, lnw[0], lnw[1], f"{tag}_ln_fwd")
    return hn, dict(h=h, qkv=qkv, zf=zf, ab=ab, o=o, lse=lse, y=y, zl=zl,
                    w_all=jnp.concatenate([w_qkv, w_zf], axis=1), w_out=w_out, bias=bias), side_out


def _fox_bwd(s, dzl, pad, tag, side=None):
    W = D_MODEL
    nb = W // LANES
    dy = _matmul(dzl, s["w_out"], "nt", F32, f"{tag}_dy")
    d_wout = _matmul(s["y"], dzl, "tn", F32, f"{tag}_dwout")
    do, delta, dzg = _gate_bwd(dy, s["o"], s["zf"], FOX_HEADS, f"{tag}_gate_bwd")
    qkv = s["qkv"]
    dq, dk, dv, dck, dcq, side_out = _attn_bwd(qkv, 0, qkv, nb, qkv, 2 * nb, do, s["ab"], s["lse"], delta,
                                               FOX_HEADS, LANES, LANES, LANES ** -0.5, f"{tag}_attn_bwd",
                                               _dead_rows(pad), side)
    dcum = jnp.pad((dck + dcq).reshape(FOX_HEADS, -1).T, ((0, 0), (0, LANES - FOX_HEADS)))
    df, dbias = _fox_cum_bwd(dcum, s["zf"], s["bias"], pad, f"{tag}_cum_bwd")
    dproj = jnp.concatenate([dq.astype(_MXU), dk.astype(_MXU), dv.astype(_MXU), dzg, df], axis=1)
    d_wall = _matmul(s["h"], dproj, "tn", F32, f"{tag}_dwin")
    dh = _matmul(dproj, s["w_all"], "nt", F32, f"{tag}_dh")
    d_win = jnp.concatenate([d_wall[:, :4 * W], d_wall[:, 4 * W:4 * W + FOX_HEADS]], axis=1)
    return dh, (d_win, dbias[0, :FOX_HEADS], d_wout), side_out


def _mla_tables(L, pad):
    pos = (jnp.arange(L) - pad).astype(F32)
    inv = ROPE_BASE ** (-jnp.arange(0, MLA_ROPE, 2, dtype=F32) / MLA_ROPE)
    ang = pos[:, None] * inv[None, :]
    c, s = jnp.cos(ang), jnp.sin(ang)
    z32, z64 = jnp.zeros_like(s), jnp.zeros((L, LANES - MLA_ROPE), F32)
    cos = jnp.concatenate([c, c, jnp.ones_like(z64)], axis=1)
    sa = jnp.concatenate([-s, z32, z64], axis=1)
    sb = jnp.concatenate([z32, s, z64], axis=1)
    return cos, sa, sb


def _mla_weights(w_in, w_uq, w_ukv):
    a, b = MLA_Q_LORA, MLA_Q_LORA + MLA_KV_LORA
    w_zc = jnp.concatenate([w_in[:, b + MLA_ROPE:], w_in[:, :b + MLA_ROPE],
                            jnp.zeros((D_MODEL, LANES - MLA_ROPE), w_in.dtype)], axis=1)
    uq = w_uq.reshape(MLA_Q_LORA, MLA_HEADS, MLA_NOPE + MLA_ROPE)
    uq = jnp.pad(uq, ((0, 0), (0, 0), (0, MLA_DK - MLA_NOPE - MLA_ROPE))).reshape(MLA_Q_LORA, MLA_HEADS * MLA_DK)
    ukv = w_ukv.reshape(MLA_KV_LORA, MLA_HEADS, MLA_NOPE + MLA_V)
    ukv = jnp.concatenate([ukv[:, :, :MLA_NOPE].reshape(MLA_KV_LORA, -1), ukv[:, :, MLA_NOPE:].reshape(MLA_KV_LORA, -1)], axis=1)
    return w_zc, uq, ukv


def _mla_fwd(h, w, lnw, pad):
    w_in, gq, gkv, w_uq, w_ukv, w_out = w
    L = h.shape[0]
    w_zc, uq, ukv = _mla_weights(w_in, w_uq, w_ukv)
    cos, sa, sb = _mla_tables(L, pad)
    proj = _matmul(h, w_zc, "nn", F32, "mla_proj")
    cqn, ckvn = _mla_norm_fwd(proj, gq[None, :], gkv[None, :])
    qf = _matmul(cqn, uq, "nn", F32, "mla_uq")
    kv = _matmul(ckvn, ukv, "nn", _MXU, "mla_ukv")
    q_full, k_full = _mla_rope_fwd(qf, kv, proj, cos, sa, sb)
    ab = _bias_terms(jnp.zeros((MLA_HEADS, L), F32), pad)
    nb = MLA_HEADS * MLA_NOPE // LANES
    o, lse, _ = _attn_fwd(q_full, 0, k_full, 0, kv, nb, ab, MLA_HEADS, MLA_DK, MLA_V,
                          (MLA_NOPE + MLA_ROPE) ** -0.5, "mla_attn_fwd", _dead_rows(pad))
    y = _gate_fwd(o, proj, "mla_gate_fwd")
    mo = _matmul(y, w_out, "nn", F32, "mla_out")
    zl, hn = _ln_fwd(h, mo, lnw[0], lnw[1], "mla_ln_fwd")
    return hn, dict(h=h, proj=proj, cqn=cqn, ckvn=ckvn, kv=kv, q_full=q_full, k_full=k_full, ab=ab,
                    dead=_dead_rows(pad),
                    o=o, lse=lse, y=y, zl=zl, w_zc=w_zc, uq=uq, ukv=ukv, w_out=w_out, gq=gq, gkv=gkv,
                    tabs=(cos, sa, sb))


def _mla_bwd(s, dzl):
    H = MLA_HEADS
    dy = _matmul(dzl, s["w_out"], "nt", F32, "mla_dy")
    d_wout = _matmul(s["y"], dzl, "tn", F32, "mla_dwout")
    do, delta, dzg = _gate_bwd(dy, s["o"], s["proj"], H, "mla_gate_bwd")
    nb = H * MLA_NOPE // LANES
    dq_full, dk_full, dv, _, _, _ = _attn_bwd(s["q_full"], 0, s["k_full"], 0, s["kv"], nb, do, s["ab"],
                                              s["lse"], delta,
                                              H, MLA_DK, MLA_V, (MLA_NOPE + MLA_ROPE) ** -0.5, "mla_attn_bwd",
                                              s["dead"], bias_grads=False)
    cos, sa, sb = s["tabs"]
    dqf, dkn, dkr = _mla_rope_bwd(dq_full, dk_full, cos, -sa, -sb)
    d_uq = _matmul(s["cqn"], dqf, "tn", F32, "mla_duq")
    dcqn = _matmul(dqf, s["uq"], "nt", F32, "mla_dcqn")
    dkv = jnp.concatenate([dkn, dv.astype(_MXU)], axis=1)
    d_ukv = _matmul(s["ckvn"], dkv, "tn", F32, "mla_dukv")
    dckvn = _matmul(dkv, s["ukv"], "nt", F32, "mla_dckvn")
    dcq, dckv, dgq, dgkv = _mla_norm_bwd(s["proj"], s["gq"][None, :], s["gkv"][None, :], dcqn, dckvn)
    dproj = jnp.concatenate([dzg, dcq, dckv, dkr], axis=1)
    d_wzc = _matmul(s["h"], dproj, "tn", F32, "mla_dwin")
    dh = _matmul(dproj, s["w_zc"], "nt", F32, "mla_dh")
    W = D_MODEL
    d_win = jnp.concatenate([d_wzc[:, W:W + MLA_Q_LORA + MLA_KV_LORA + MLA_ROPE], d_wzc[:, :W]], axis=1)
    d_wuq = d_uq.reshape(MLA_Q_LORA, H, MLA_DK)[:, :, :MLA_NOPE + MLA_ROPE].reshape(MLA_Q_LORA, -1)
    hk = H * MLA_NOPE
    d_wukv = jnp.concatenate([d_ukv[:, :hk].reshape(MLA_KV_LORA, H, MLA_NOPE),
                              d_ukv[:, hk:].reshape(MLA_KV_LORA, H, MLA_V)], axis=2).reshape(MLA_KV_LORA, -1)
    return dh, (d_win, dgq[0], dgkv[0], d_wuq, d_wukv, d_wout)


def _ret_tables(L, pad):
    pos = (jnp.arange(L) - pad).astype(F32)
    inv = 1.0 / (ROPE_BASE ** jnp.linspace(0.0, 1.0, RET_QK // 2, dtype=F32))
    ang = pos[:, None] * inv[None, :]
    return jnp.cos(ang), jnp.sin(ang)


def _ret_layer_fwd(h, w, lnw, pad):
    w_in, gn_g, w_out = w
    L = h.shape[0]
    q1 = 2 * _RET_QKW
    v1 = q1 + RET_HEADS * RET_V
    w_qkz = jnp.concatenate([w_in[:, :q1], w_in[:, v1:]], axis=1)
    cos, sin = _ret_tables(L, pad)
    qkz = _matmul(h, w_qkz, "nn", F32, "ret_qkz")
    v = _matmul(h, w_in[:, q1:v1], "nn", _MXU, "ret_v")
    qr, kr = _ret_rope_fwd(qkz, cos, sin, pad)
    o, states = _ret_fwd(qr, kr, v)
    y = _ret_gate_fwd(o, qkz, gn_g[None, :])
    mo = _matmul(y, w_out, "nn", F32, "ret_out")
    zl, hn = _ln_fwd(h, mo, lnw[0], lnw[1], "ret_ln_fwd")
    return hn, dict(h=h, qkz=qkz, v=v, qr=qr, kr=kr, o=o, states=states, y=y, zl=zl, w_in=w_in, w_out=w_out,
                    gn_g=gn_g, tabs=(cos, sin))


def _ret_layer_bwd(s, dzl, pad):
    dy = _matmul(dzl, s["w_out"], "nt", F32, "ret_dy")
    d_wout = _matmul(s["y"], dzl, "tn", F32, "ret_dwout")
    do, dzg, dgn = _ret_gate_bwd(dy, s["o"], s["qkz"], s["gn_g"][None, :])
    dqr, dkr, dv = _ret_bwd(s["qr"], s["kr"], s["v"], do, s["states"])
    cos, sin = s["tabs"]
    dq, dk = _ret_rope_bwd(dqr, dkr, cos, sin, pad)
    dproj = jnp.concatenate([dq, dk, dv, dzg], axis=1)
    d_win = _matmul(s["h"], dproj, "tn", F32, "ret_dwin")
    dh = _matmul(dproj, s["w_in"], "nt", F32, "ret_dh")
    return dh, (d_win, dgn[0], d_wout)


def _local_step(x, target, meta, fox0_w, late_shards, small, ln_g, ln_b):
    n_real = x.shape[0]
    first = -(-N_META // ROW_TILE)
    pad = first * ROW_TILE - N_META
    h0 = jnp.concatenate([jnp.zeros((pad, D_MODEL), F32), meta, x], axis=0)
    lnw = [(ln_g[i][None, :], ln_b[i][None, :]) for i in range(4)]
    b_f = small["fox_b_f"]

    h1, s0, gathered = _fox_fwd(h0, (fox0_w[0], b_f[0], fox0_w[1]), lnw[0], pad, "fox0", (late_shards, False))
    w = {key: _merge_layer(key[0], g).astype(_MXU) for key, g in zip(_LATE, gathered)}
    h2, s1 = _mla_fwd(h1, (w["mla_w_in", 0], small["mla_q_norm"][0], small["mla_kv_norm"][0], w["mla_w_uq", 0],
                           w["mla_w_ukv", 0], w["mla_w_out", 0]), lnw[1], pad)
    h3, s2 = _ret_layer_fwd(h2, (w["ret_w_in", 0], small["ret_gn_g"][0], w["ret_w_out", 0]), lnw[2], pad)
    h4, s3, _ = _fox_fwd(h3, (w["fox_w_in", 1], b_f[1], w["fox_w_out", 1]), lnw[3], pad, "fox1")

    dh4, loss = _loss_head(h4, target, first)
    dz3, dg3, db3 = _ln_bwd(s3["zl"], lnw[3][0], dh4, None, "ln3_bwd")
    dm3, gf1, _ = _fox_bwd(s3, dz3, pad, "fox1")
    dz2, dg2, db2 = _ln_bwd(s2["zl"], lnw[2][0], dz3, dm3, "ln2_bwd")
    dm2, gr = _ret_layer_bwd(s2, dz2, pad)
    dz1, dg1, db1 = _ln_bwd(s1["zl"], lnw[1][0], dz2, dm2, "ln1_bwd")
    dm1, gm = _mla_bwd(s1, dz1)
    dz0, dg0, db0 = _ln_bwd(s0["zl"], lnw[0][0], dz1, dm1, "ln0_bwd")
    late = {("fox_w_in", 1): gf1[0], ("fox_w_out", 1): gf1[2], ("mla_w_in", 0): gm[0], ("mla_w_uq", 0): gm[3],
            ("mla_w_ukv", 0): gm[4], ("mla_w_out", 0): gm[5], ("ret_w_in", 0): gr[0], ("ret_w_out", 0): gr[2]}
    dm0, gf0, late_parts = _fox_bwd(s0, dz0, pad, "fox0", ([_split_layer(k[0], late[k]) for k in _LATE], True))
    grad_x, grad_meta = _input_grads(dz0, dm0, first, n_real)

    small_grads = dict(
        meta=grad_meta, fox_b_f=jnp.stack([gf0[1], gf1[1]]), mla_q_norm=gm[1][None], mla_kv_norm=gm[2][None],
        ret_gn_g=gr[1][None], ln_g=jnp.concatenate([dg0, dg1, dg2, dg3], axis=0),
        ln_b=jnp.concatenate([db0, db1, db2, db3], axis=0))
    return loss, grad_x, (gf0[0], gf0[2]), late_parts, small_grads


def _exchange(tensors, scatter, name):
    nt = len(tensors)

    def body(*refs):
        copies = _exchange_copies(refs[:nt], refs[nt:2 * nt], *refs[2 * nt:], scatter)
        for cp in copies:
            cp.start()
        for cp in copies:
            cp.wait()

    hbm = pl.BlockSpec(memory_space=pl.ANY)
    out_shape, sems = _exchange_shapes(tensors, scatter)
    return pl.pallas_call(
        body, name=name, out_shape=out_shape, in_specs=[hbm] * nt, out_specs=(hbm,) * nt, scratch_shapes=sems,
    )(*tensors)


def _adamw(parts, w, m, v, name):
    R, C = w.shape
    tr = next(t for t in range(min(R, 256), 0, -1) if R % t == 0 and (t % 8 == 0 or t == R))

    def body(p_ref, w_ref, m_ref, v_ref, g_ref, d_ref, mo_ref, vo_ref):
        g = p_ref[0]
        for d in range(1, N_DEV):
            g = g + p_ref[d]
        mn = ADAM_B1 * m_ref[...] + (1.0 - ADAM_B1) * g
        vn = ADAM_B2 * v_ref[...] + (1.0 - ADAM_B2) * (g * g)
        m_hat = mn / (1.0 - ADAM_B1 ** ADAM_STEP)
        v_hat = vn / (1.0 - ADAM_B2 ** ADAM_STEP)
        g_ref[...] = g
        d_ref[...] = -ADAM_LR * (m_hat / (jnp.sqrt(v_hat) + ADAM_EPS) + ADAM_WD * w_ref[...])
        mo_ref[...] = mn
        vo_ref[...] = vn

    blk = pl.BlockSpec((tr, C), lambda i: (i, 0))
    sd = jax.ShapeDtypeStruct((R, C), F32)
    return pl.pallas_call(
        body, name=name,
        out_shape=(sd, sd, sd, sd),
        grid=(R // tr,),
        in_specs=[pl.BlockSpec((N_DEV, tr, C), lambda i: (0, i, 0)), blk, blk, blk],
        out_specs=(blk, blk, blk, blk),
        compiler_params=_params(("parallel",)),
    )(parts, w, m, v)


_SPECS = dict(
    meta=((16, 1024), 1),
    fox_w_in=((2, 1024, 4104), 2), fox_b_f=((2, 8), None), fox_w_out=((2, 1024, 1024), 1),
    mla_w_in=((1, 1024, 1728), 2), mla_q_norm=((1, 384), None), mla_kv_norm=((1, 256), None),
    mla_w_uq=((1, 384, 1536), 2), mla_w_ukv=((1, 256, 2048), 2), mla_w_out=((1, 1024, 1024), 1),
    ret_w_in=((1, 1024, 6144), 2), ret_gn_g=((1, 2048), 1), ret_w_out=((1, 2048, 1024), 1),
    ln_g=((4, 1024), None), ln_b=((4, 1024), None),
)
_NAMES = list(_SPECS)
_BIG = ["fox_w_in", "fox_w_out", "mla_w_in", "mla_w_uq", "mla_w_ukv", "mla_w_out", "ret_w_in", "ret_w_out"]
_SMALL = [n for n in _NAMES if n not in _BIG]
_SMALL_GATHERED = ["meta", "ret_gn_g"]
_PACK_ROWS = 8


def _shard_shape(name):
    shape, ax = _SPECS[name]
    if ax is None:
        return shape
    return tuple(s // N_DEV if i == ax else s for i, s in enumerate(shape))


_EARLY = [("fox_w_in", 0), ("fox_w_out", 0)]
_LATE = [("fox_w_in", 1), ("fox_w_out", 1), ("mla_w_in", 0), ("mla_w_uq", 0), ("mla_w_ukv", 0), ("mla_w_out", 0),
         ("ret_w_in", 0), ("ret_w_out", 0)]


def _split_at(full, ax):
    shape = full.shape
    parts = full.reshape(shape[:ax] + (N_DEV, shape[ax] // N_DEV) + shape[ax + 1:])
    return jnp.moveaxis(parts, ax, 0)


def _merge_at(stacked, ax):
    full = jnp.moveaxis(stacked, 0, ax)
    return full.reshape(full.shape[:ax] + (-1,) + full.shape[ax + 2:])


def _split(name, full):
    return _split_at(full, _SPECS[name][1])


def _merge(name, stacked):
    return _merge_at(stacked, _SPECS[name][1])


def _split_layer(name, full):
    return _split_at(full, _SPECS[name][1] - 1)


def _merge_layer(name, stacked):
    return _merge_at(stacked, _SPECS[name][1] - 1)


def _pad_rows(flat):
    n = flat.shape[-1]
    unit = _PACK_ROWS * LANES
    total = -(-n // unit) * unit
    flat = jnp.pad(flat, [(0, 0)] * (flat.ndim - 1) + [(0, total - n)])
    return flat.reshape(flat.shape[:-1] + (total // LANES, LANES))


def _pack_small(tree, names, tail):
    return _pad_rows(jnp.concatenate([tree[n].reshape(-1) for n in names] + [tail]))


def _unpack_small(buf, names):
    flat, out, at = buf.reshape(-1), {}, 0
    for n in names:
        shp = _shard_shape(n)
        out[n] = flat[at:at + math.prod(shp)].reshape(shp)
        at += math.prod(shp)
    return out, flat[at:]


def kernel(x, meta, fox_w_in, fox_b_f, fox_w_out, mla_w_in, mla_q_norm, mla_kv_norm, mla_w_uq, mla_w_ukv, mla_w_out, ret_w_in, ret_gn_g, ret_w_out, ln_g, ln_b, loss_target, m_meta, m_fox_w_in, m_fox_b_f, m_fox_w_out, m_mla_w_in, m_mla_q_norm, m_mla_kv_norm, m_mla_w_uq, m_mla_w_ukv, m_mla_w_out, m_ret_w_in, m_ret_gn_g, m_ret_w_out, m_ln_g, m_ln_b, v_meta, v_fox_w_in, v_fox_b_f, v_fox_w_out, v_mla_w_in, v_mla_q_norm, v_mla_kv_norm, v_mla_w_uq, v_mla_w_ukv, v_mla_w_out, v_ret_w_in, v_ret_gn_g, v_ret_w_out, v_ln_g, v_ln_b):
    w = dict(meta=meta, fox_w_in=fox_w_in, fox_b_f=fox_b_f, fox_w_out=fox_w_out, mla_w_in=mla_w_in,
             mla_q_norm=mla_q_norm, mla_kv_norm=mla_kv_norm, mla_w_uq=mla_w_uq, mla_w_ukv=mla_w_ukv,
             mla_w_out=mla_w_out, ret_w_in=ret_w_in, ret_gn_g=ret_gn_g, ret_w_out=ret_w_out, ln_g=ln_g, ln_b=ln_b)
    m = dict(meta=m_meta, fox_w_in=m_fox_w_in, fox_b_f=m_fox_b_f, fox_w_out=m_fox_w_out, mla_w_in=m_mla_w_in,
             mla_q_norm=m_mla_q_norm, mla_kv_norm=m_mla_kv_norm, mla_w_uq=m_mla_w_uq, mla_w_ukv=m_mla_w_ukv,
             mla_w_out=m_mla_w_out, ret_w_in=m_ret_w_in, ret_gn_g=m_ret_gn_g, ret_w_out=m_ret_w_out, ln_g=m_ln_g, ln_b=m_ln_b)
    v = dict(meta=v_meta, fox_w_in=v_fox_w_in, fox_b_f=v_fox_b_f, fox_w_out=v_fox_w_out, mla_w_in=v_mla_w_in,
             mla_q_norm=v_mla_q_norm, mla_kv_norm=v_mla_kv_norm, mla_w_uq=v_mla_w_uq, mla_w_ukv=v_mla_w_ukv,
             mla_w_out=v_mla_w_out, ret_w_in=v_ret_w_in, ret_gn_g=v_ret_gn_g, ret_w_out=v_ret_w_out, ln_g=v_ln_g, ln_b=v_ln_b)

    def shard16(key):
        return w[key[0]][key[1]].astype(jnp.bfloat16)

    none = jnp.zeros((0,), F32)
    gathered = _exchange([shard16(k) for k in _EARLY] + [_pack_small(w, _SMALL_GATHERED, none)],
                         False, "weights_all_gather")
    fox0_w = [_merge_layer(k[0], g).astype(_MXU) for k, g in zip(_EARLY, gathered)]
    small = {n: w[n] for n in _SMALL}
    pieces = [_unpack_small(gathered[-1][d], _SMALL_GATHERED)[0] for d in range(N_DEV)]
    for n in _SMALL_GATHERED:
        small[n] = _merge(n, jnp.stack([p[n] for p in pieces]))

    loss, grad_x, early_grads, late_parts, grads = _local_step(
        x[0], loss_target[0], small["meta"], fox0_w, [shard16(k) for k in _LATE], small, ln_g, ln_b)

    def small_rows(n):
        if _SPECS[n][1] is None:
            return jnp.broadcast_to(grads[n].reshape(1, -1), (N_DEV, grads[n].size))
        return _split(n, grads[n]).reshape(N_DEV, -1)

    small_out = _pad_rows(jnp.concatenate([small_rows(n) for n in _SMALL]
                                          + [jnp.broadcast_to(loss.reshape(1, 1), (N_DEV, 1))], axis=1))
    parts = _exchange([_split_layer(k[0], g) for k, g in zip(_EARLY, early_grads)] + [small_out],
                      True, "grads_all_to_all")
    by_key = dict(zip(_EARLY + _LATE, list(parts[:-1]) + list(late_parts)))

    def rows2d(a):
        return a.reshape(-1, a.shape[-1])

    out = {}
    for n in _BIG:
        layers = [by_key[n, i] for i in range(_SPECS[n][0][0])]
        p = layers[0] if len(layers) == 1 else jnp.stack(layers, axis=1)
        res = _adamw(p.reshape(N_DEV, -1, p.shape[-1]), rows2d(w[n]), rows2d(m[n]), rows2d(v[n]), f"adamw_{n}")
        out[n] = [r.reshape(w[n].shape) for r in res]
    zero1 = jnp.zeros((1,), F32)
    res = _adamw(parts[-1], _pack_small(w, _SMALL, zero1), _pack_small(m, _SMALL, zero1),
                 _pack_small(v, _SMALL, zero1), "adamw_small")
    small_res = [_unpack_small(r, _SMALL) for r in res]
    for n in _SMALL:
        out[n] = [s[0][n] for s in small_res]
    total_loss = small_res[0][1][0]
    return (total_loss, grad_x[None], *[out[n][0] for n in _NAMES], *[out[n][1] for n in _NAMES],
            *[out[n][2] for n in _NAMES], *[out[n][3] for n in _NAMES])
```

```python
import functools
import math

import numpy as np
import jax
import jax.numpy as jnp
from jax import lax
from jax.experimental import pallas as pl
from jax.experimental.pallas import tpu as pltpu

F32 = jnp.float32
_MXU = jnp.bfloat16

N_DEV = 8
N_META = 16
D_MODEL = 1024
ROW_TILE = 512
LANES = 128
CHUNK = 128
NEG = -1e30
VMEM_LIMIT = 60 * 1024 * 1024

FOX_HEADS = 8
MLA_HEADS = 8
MLA_NOPE, MLA_ROPE, MLA_V = 128, 64, 128
MLA_Q_LORA, MLA_KV_LORA = 384, 256
MLA_DK = 256
RET_HEADS, RET_QK, RET_V = 4, 256, 512
ROPE_BASE = 10000.0
ALPHA = (2 * 4) ** 0.25
NORM_EPS = 1e-5

ADAM_LR, ADAM_B1, ADAM_B2, ADAM_EPS, ADAM_WD, ADAM_STEP = 0.001, 0.9, 0.999, 1e-08, 0.01, 10


def _params(sem, vmem=VMEM_LIMIT):
    return pltpu.CompilerParams(dimension_semantics=sem, vmem_limit_bytes=vmem)


def _tile(n, pref):
    if n <= pref:
        return n
    t = (pref // LANES) * LANES
    while n % t:
        t -= LANES
    return t


_MM_TILES = {"nn": (512, 4224, 2048), "nt": (512, 2048, 6144), "tn": (1024, 2048, 1536)}


def _matmul(a, b, mode, out_dtype, name):
    if mode == "nn":
        (M, K), (K2, N) = a.shape, b.shape
    elif mode == "nt":
        (M, K), (N, K2) = a.shape, b.shape
    else:
        (K, M), (K2, N) = a.shape, b.shape
    assert K == K2, (a.shape, b.shape, mode)
    tm, tn, tk = (_tile(d, p) for d, p in zip((M, N, K), _MM_TILES[mode]))
    nk = K // tk
    if mode == "nn":
        a_spec = pl.BlockSpec((tm, tk), lambda i, j, k: (i, k))
        b_spec = pl.BlockSpec((tk, tn), lambda i, j, k: (k, j))
        dn = (((1,), (0,)), ((), ()))
    elif mode == "nt":
        a_spec = pl.BlockSpec((tm, tk), lambda i, j, k: (i, k))
        b_spec = pl.BlockSpec((tn, tk), lambda i, j, k: (j, k))
        dn = (((1,), (1,)), ((), ()))
    else:
        a_spec = pl.BlockSpec((tk, tm), lambda i, j, k: (k, i))
        b_spec = pl.BlockSpec((tk, tn), lambda i, j, k: (k, j))
        dn = (((0,), (0,)), ((), ()))

    def body(a_ref, b_ref, o_ref, *acc):
        part = lax.dot_general(a_ref[...].astype(_MXU), b_ref[...].astype(_MXU), dn,
                               preferred_element_type=F32)
        if nk == 1:
            o_ref[...] = part.astype(out_dtype)
            return
        acc_ref, = acc
        k = pl.program_id(2)

        @pl.when(k == 0)
        def _():
            acc_ref[...] = part

        @pl.when(k > 0)
        def _():
            acc_ref[...] += part

        @pl.when(k == nk - 1)
        def _():
            o_ref[...] = acc_ref[...].astype(out_dtype)

    return pl.pallas_call(
        body, name=name,
        out_shape=jax.ShapeDtypeStruct((M, N), out_dtype),
        grid=(M // tm, N // tn, nk),
        in_specs=[a_spec, b_spec],
        out_specs=pl.BlockSpec((tm, tn), lambda i, j, k: (i, j)),
        scratch_shapes=[pltpu.VMEM((tm, tn), F32)] if nk > 1 else [],
        compiler_params=_params(("parallel", "parallel", "arbitrary")),
    )(a, b)


def _rows(w, off=0):
    return pl.BlockSpec((ROW_TILE, w), lambda i: (i + off, 0))


def _whole(shape):
    return pl.BlockSpec(shape, lambda i: (0,) * len(shape))


def _silu(z):
    return z * jax.nn.sigmoid(z)


def _dsilu(z):
    s = jax.nn.sigmoid(z)
    return s * (1.0 + z * (1.0 - s))


def _ln_fwd(h, y, g, b, name):
    L, D = h.shape

    def body(h_ref, y_ref, g_ref, b_ref, z_ref, o_ref):
        z = ALPHA * h_ref[...] + y_ref[...]
        mu = jnp.mean(z, axis=-1, keepdims=True)
        zc = z - mu
        var = jnp.mean(zc * zc, axis=-1, keepdims=True)
        z_ref[...] = z
        o_ref[...] = zc * lax.rsqrt(var + NORM_EPS) * g_ref[...] + b_ref[...]

    return pl.pallas_call(
        body, name=name,
        out_shape=(jax.ShapeDtypeStruct((L, D), F32), jax.ShapeDtypeStruct((L, D), F32)),
        grid=(L // ROW_TILE,),
        in_specs=[_rows(D), _rows(D), _whole((1, D)), _whole((1, D))],
        out_specs=(_rows(D), _rows(D)),
        compiler_params=_params(("parallel",)),
    )(h, y, g, b)


def _ln_bwd(zl, g, ga, gb, name):
    L, D = zl.shape
    two = gb is not None

    def body(*refs):
        if two:
            z_ref, g_ref, ga_ref, gb_ref, dz_ref, dg_ref, db_ref = refs
            gout = ALPHA * ga_ref[...] + gb_ref[...]
        else:
            z_ref, g_ref, ga_ref, dz_ref, dg_ref, db_ref = refs
            gout = ga_ref[...]
        z = z_ref[...]
        mu = jnp.mean(z, axis=-1, keepdims=True)
        zc = z - mu
        var = jnp.mean(zc * zc, axis=-1, keepdims=True)
        rstd = lax.rsqrt(var + NORM_EPS)
        xhat = zc * rstd
        dxh = gout * g_ref[...]
        m1 = jnp.mean(dxh, axis=-1, keepdims=True)
        m2 = jnp.mean(dxh * xhat, axis=-1, keepdims=True)
        dz_ref[...] = rstd * (dxh - m1 - xhat * m2)

        @pl.when(pl.program_id(0) == 0)
        def _():
            dg_ref[...] = jnp.zeros_like(dg_ref)
            db_ref[...] = jnp.zeros_like(db_ref)

        dg_ref[...] += jnp.sum(gout * xhat, axis=0, keepdims=True)
        db_ref[...] += jnp.sum(gout, axis=0, keepdims=True)

    ins = [zl, g, ga] + ([gb] if two else [])
    return pl.pallas_call(
        body, name=name,
        out_shape=(jax.ShapeDtypeStruct((L, D), F32), jax.ShapeDtypeStruct((1, D), F32),
                   jax.ShapeDtypeStruct((1, D), F32)),
        grid=(L // ROW_TILE,),
        in_specs=[_rows(D), _whole((1, D)), _rows(D)] + ([_rows(D)] if two else []),
        out_specs=(_rows(D), _whole((1, D)), _whole((1, D))),
        compiler_params=_params(("arbitrary",)),
    )(*ins)


def _loss_head(h, target, first):
    L, D = h.shape

    def body(h_ref, t_ref, dh_ref, loss_ref):
        i = pl.program_id(0)

        @pl.when(i == 0)
        def _():
            loss_ref[...] = jnp.zeros_like(loss_ref)

        @pl.when(i < first)
        def _():
            dh_ref[...] = jnp.zeros_like(dh_ref)

        @pl.when(i >= first)
        def _():
            err = h_ref[...] - t_ref[...]
            dh_ref[...] = err * (1.0 / D)
            part = jnp.sum(jnp.sum(err * err, axis=-1, keepdims=True) * (1.0 / D), axis=0, keepdims=True)
            loss_ref[...] += 0.5 * part

    return pl.pallas_call(
        body, name="loss_head",
        out_shape=(jax.ShapeDtypeStruct((L, D), F32), jax.ShapeDtypeStruct((1, 1), F32)),
        grid=(L // ROW_TILE,),
        in_specs=[_rows(D), pl.BlockSpec((ROW_TILE, D), lambda i: (jnp.maximum(i - first, 0), 0))],
        out_specs=(_rows(D), _whole((1, 1))),
        compiler_params=_params(("arbitrary",)),
    )(h, target)


def _input_grads(dzl, dmix, first, n_real):
    L, D = dzl.shape

    def body(a_ref, b_ref, gx_ref, gm_ref):
        i = pl.program_id(0)
        val = ALPHA * a_ref[...] + b_ref[...]

        @pl.when(i == first - 1)
        def _():
            gm_ref[...] = val[ROW_TILE - N_META:, :]

        @pl.when(i >= first)
        def _():
            gx_ref[...] = val

    return pl.pallas_call(
        body, name="input_grads",
        out_shape=(jax.ShapeDtypeStruct((n_real, D), F32), jax.ShapeDtypeStruct((N_META, D), F32)),
        grid=(L // ROW_TILE,),
        in_specs=[_rows(D), _rows(D)],
        out_specs=(pl.BlockSpec((ROW_TILE, D), lambda i: (jnp.maximum(i - first, 0), 0)),
                   _whole((N_META, D))),
        compiler_params=_params(("arbitrary",)),
    )(dzl, dmix)


def _gate_fwd(o, zsrc, name):
    L, W = o.shape

    def body(o_ref, z_ref, y_ref):
        y_ref[...] = (o_ref[...] * _silu(z_ref[...])).astype(_MXU)

    return pl.pallas_call(
        body, name=name,
        out_shape=jax.ShapeDtypeStruct((L, W), _MXU),
        grid=(L // ROW_TILE,),
        in_specs=[_rows(W), _rows(W)],
        out_specs=_rows(W),
        compiler_params=_params(("parallel",)),
    )(o, zsrc)


def _gate_bwd(dy, o, zsrc, heads, name):
    L, W = o.shape
    hd = W // heads

    def body(dy_ref, o_ref, z_ref, do_ref, dl_ref, dz_ref):
        z = z_ref[...]
        dy = dy_ref[...]
        o = o_ref[...]
        do = dy * _silu(z)
        do_ref[...] = do.astype(_MXU)
        dz_ref[...] = (dy * o * _dsilu(z)).astype(_MXU)
        prod = do * o
        for h in range(heads):
            dl_ref[h] = _as_row(jnp.sum(prod[:, h * hd:(h + 1) * hd], axis=-1, keepdims=True))

    return pl.pallas_call(
        body, name=name,
        out_shape=(jax.ShapeDtypeStruct((L, W), _MXU), jax.ShapeDtypeStruct((heads, L // ROW_TILE, 1, ROW_TILE), F32),
                   jax.ShapeDtypeStruct((L, W), _MXU)),
        grid=(L // ROW_TILE,),
        in_specs=[_rows(W), _rows(W), _rows(W)],
        out_specs=(_rows(W), pl.BlockSpec((heads, None, 1, ROW_TILE), lambda i: (0, i, 0, 0)), _rows(W)),
        compiler_params=_params(("parallel",)),
    )(dy, o, zsrc)


def _split3(x):
    hi = x.astype(_MXU)
    r1 = x - hi.astype(F32)
    mid = r1.astype(_MXU)
    lo = (r1 - mid.astype(F32)).astype(_MXU)
    return hi, mid, lo


def _tri_cumsum(x, reverse):
    T = x.shape[0]
    r = lax.broadcasted_iota(jnp.int32, (T, T), 0)
    c = lax.broadcasted_iota(jnp.int32, (T, T), 1)
    tri = jnp.where((c >= r) if reverse else (c <= r), 1.0, 0.0).astype(_MXU)
    out = jnp.zeros(x.shape, F32)
    for part in _split3(x):
        out = out + jnp.dot(tri, part, preferred_element_type=F32)
    return out


def _fox_cum(zf, bias, pad, name):
    L, Wz = zf.shape
    off = Wz // LANES - 1

    def body(f_ref, b_ref, c_ref, carry_ref):
        i = pl.program_id(0)

        @pl.when(i == 0)
        def _():
            carry_ref[...] = jnp.zeros_like(carry_ref)

        x = f_ref[...] + b_ref[...]
        logf = jnp.minimum(x, 0.0) - jnp.log(1.0 + jnp.exp(-jnp.abs(x)))
        row = i * ROW_TILE + lax.broadcasted_iota(jnp.int32, logf.shape, 0)
        logf = jnp.where(row >= pad, logf, 0.0)
        cum = _tri_cumsum(logf, False) + carry_ref[...]
        c_ref[...] = cum
        carry_ref[...] = cum[ROW_TILE - 1:, :]

    return pl.pallas_call(
        body, name=name,
        out_shape=jax.ShapeDtypeStruct((L, LANES), F32),
        grid=(L // ROW_TILE,),
        in_specs=[pl.BlockSpec((ROW_TILE, LANES), lambda i: (i, off)), _whole((1, LANES))],
        out_specs=_rows(LANES),
        scratch_shapes=[pltpu.VMEM((1, LANES), F32)],
        compiler_params=_params(("arbitrary",)),
    )(zf, bias)


def _fox_cum_bwd(dcum, zf, bias, pad, name):
    L, Wz = zf.shape
    off = Wz // LANES - 1
    n = L // ROW_TILE

    def body(d_ref, f_ref, b_ref, df_ref, db_ref, carry_ref):
        i = pl.program_id(0)

        @pl.when(i == 0)
        def _():
            carry_ref[...] = jnp.zeros_like(carry_ref)
            db_ref[...] = jnp.zeros_like(db_ref)

        rc = _tri_cumsum(d_ref[...], True) + carry_ref[...]
        carry_ref[...] = rc[:1, :]
        x = f_ref[...] + b_ref[...]
        row = (n - 1 - i) * ROW_TILE + lax.broadcasted_iota(jnp.int32, x.shape, 0)
        df = jnp.where(row >= pad, rc * jax.nn.sigmoid(-x), 0.0)
        df_ref[...] = df.astype(_MXU)
        db_ref[...] += jnp.sum(df, axis=0, keepdims=True)

    return pl.pallas_call(
        body, name=name,
        out_shape=(jax.ShapeDtypeStruct((L, LANES), _MXU), jax.ShapeDtypeStruct((1, LANES), F32)),
        grid=(n,),
        in_specs=[pl.BlockSpec((ROW_TILE, LANES), lambda i: (n - 1 - i, 0)),
                  pl.BlockSpec((ROW_TILE, LANES), lambda i: (n - 1 - i, off)), _whole((1, LANES))],
        out_specs=(pl.BlockSpec((ROW_TILE, LANES), lambda i: (n - 1 - i, 0)), _whole((1, LANES))),
        scratch_shapes=[pltpu.VMEM((1, LANES), F32)],
        compiler_params=_params(("arbitrary",)),
    )(dcum, zf, bias)


LOG2E = 1.4426950408889634
BWD_UNROLL = 8
FWD_UNROLL = 8


def _bias_terms(ct, pad):
    H, L = ct.shape
    T = ROW_TILE
    n = L // T
    c2 = ct * LOG2E
    k2 = jnp.where(jnp.arange(L)[None, :] < pad, -NEG, c2)
    return dict(cq_row=c2.reshape(H, n, 1, T), ck_row=k2.reshape(H, n, 1, T), ref=c2[:, ::T].reshape(H, n, 1, 1))


def _exchange_copies(srcs, outs, send_sems, recv_sems, local_sems, scatter):
    nt = len(srcs)
    x, y, c = lax.axis_index("x"), lax.axis_index("y"), lax.axis_index("c")
    me = 4 * x + 2 * y + c
    copies = [pltpu.make_async_copy(srcs[t].at[me] if scatter else srcs[t], outs[t].at[me], local_sems.at[t])
              for t in range(nt)]
    for k in range(1, N_DEV):
        px = (x + (k >> 2)) % 2
        py = (y + ((k >> 1) & 1)) % 2
        pc = (c + (k & 1)) % 2
        peer = 4 * px + 2 * py + pc
        for t in range(nt):
            copies.append(pltpu.make_async_remote_copy(
                src_ref=srcs[t].at[peer] if scatter else srcs[t],
                dst_ref=outs[t].at[me],
                send_sem=send_sems.at[k - 1, t], recv_sem=recv_sems.at[k - 1, t],
                device_id=(px, py, pc), device_id_type=pl.DeviceIdType.MESH))
    return copies


def _exchange_shapes(tensors, scatter):
    out_shape = tuple(jax.ShapeDtypeStruct(t.shape if scatter else (N_DEV,) + t.shape, t.dtype) for t in tensors)
    nt = len(tensors)
    sems = [pltpu.SemaphoreType.DMA((N_DEV - 1, nt)), pltpu.SemaphoreType.DMA((N_DEV - 1, nt)),
            pltpu.SemaphoreType.DMA((nt,))]
    return out_shape, sems


def _side_exchange(side, srcs, outs, sems, first, last):
    if not side:
        return

    @pl.when(first)
    def _():
        for cp in _exchange_copies(srcs, outs, *sems, side[1]):
            cp.start()

    @pl.when(last)
    def _():
        for cp in _exchange_copies(srcs, outs, *sems, side[1]):
            cp.wait()


def _as_row(col):
    return jnp.transpose(jnp.broadcast_to(col, (col.shape[0], LANES)))[0:1, :]


def _as_col(row):
    return jnp.transpose(jnp.broadcast_to(row, (LANES, row.shape[1])))[:, 0:1]


def _attn_fwd(q, qoff, k, koff, v, voff, bias, heads, dk, dv, scale, name, dead, side=None):
    L = q.shape[0]
    T = ROW_TILE
    n = L // T
    c = scale * LOG2E
    dn_qk = (((1,), (1,)), ((), ()))

    ns = len(side[0]) if side else 0

    def body(*refs):
        q_ref, k_ref, v_ref, ck_ref, ref_ref = refs[:5]
        o_ref, lse_ref = refs[5 + ns:7 + ns]
        s_a, s_b = refs[7 + 2 * ns:9 + 2 * ns]
        i = pl.program_id(1)
        _side_exchange(side, refs[5:5 + ns], refs[7 + ns:7 + 2 * ns], refs[9 + 2 * ns:],
                       (pl.program_id(0) == 0) & (i == 0), (pl.program_id(0) == heads - 1) & (i == n - 1))
        qb = q_ref[...]
        ref = ref_ref[...]

        def scores(j, dst):
            start = pl.multiple_of(j * T, T)
            dst[...] = lax.dot_general(qb, k_ref[pl.ds(start, T), :], dn_qk, preferred_element_type=F32)

        def update(t, vj, carry):
            m, l, acc = carry
            m_new = jnp.maximum(m, jnp.max(t, axis=1, keepdims=True))
            p = jnp.exp2(t - m_new)
            a = jnp.exp2(m - m_new)
            l = a * l + jnp.sum(p, axis=1, keepdims=True)
            acc = a * acc + jnp.dot(p.astype(_MXU), vj, preferred_element_type=F32)
            return m_new, l, acc

        def soft(j, carry, cur, diag):
            start = pl.multiple_of(j * T, T)
            t = cur[...] * c - (ck_ref[j] - ref)
            if diag:
                r = lax.broadcasted_iota(jnp.int32, (T, T), 0)
                cc = lax.broadcasted_iota(jnp.int32, (T, T), 1)
                t = jnp.where(cc <= r, t, NEG)
            return update(t, v_ref[pl.ds(start, T), :], carry)

        def head_tile(carry):
            t = lax.dot_general(qb, k_ref[dead:T, :], dn_qk, preferred_element_type=F32) * c
            return update(t - (ck_ref[0][:, dead:] - ref), v_ref[dead:T, :], carry)

        def step(j, carry, cur, nxt):
            scores(j + 1, nxt)
            return soft(j, carry, cur, False)

        first = jnp.where(i > 0, 1, 0) if dead else 0

        def run_of(j0, steps, carry):
            for u in range(0, steps, 2):
                carry = step(j0 + u + 1, step(j0 + u, carry, s_a, s_b), s_b, s_a)
            return carry

        scores(first, s_a)
        carry = (jnp.full((T, 1), -jnp.inf, F32), jnp.zeros((T, 1), F32), jnp.zeros((T, dv), F32))
        if dead:
            carry = lax.cond(i > 0, head_tile, lambda cy: cy, carry)
        whole = i - first
        big = whole // FWD_UNROLL
        carry = lax.fori_loop(0, big, lambda g, cy: run_of(first + FWD_UNROLL * g, FWD_UNROLL, cy), carry)
        done = first + FWD_UNROLL * big
        carry = lax.fori_loop(0, (i - done) // 2, lambda g, cy: run_of(done + 2 * g, 2, cy), carry)
        odd = ((i - done) % 2) == 1
        carry = lax.cond(odd, lambda cy: step(i - 1, cy, s_a, s_b), lambda cy: cy, carry)
        m, l, acc = lax.cond(odd, lambda cy: soft(i, cy, s_b, True), lambda cy: soft(i, cy, s_a, True), carry)
        o_ref[...] = acc / l
        lse_ref[...] = _as_row(m + jnp.log2(l))

    hbm = pl.BlockSpec(memory_space=pl.ANY)
    side_shapes, side_sems = _exchange_shapes(*side) if side else ((), [])
    o, lse, *side_out = pl.pallas_call(
        body, name=name,
        out_shape=(jax.ShapeDtypeStruct((L, heads * dv), F32), jax.ShapeDtypeStruct((heads, n, 1, T), F32))
        + side_shapes,
        grid=(heads, n),
        in_specs=[pl.BlockSpec((T, dk), lambda h, i: (i, qoff + h)),
                  pl.BlockSpec((L, dk), lambda h, i: (0, koff + h)),
                  pl.BlockSpec((L, dv), lambda h, i: (0, voff + h)),
                  pl.BlockSpec((None, n, 1, T), lambda h, i: (h, 0, 0, 0)),
                  pl.BlockSpec((None, None, 1, 1), lambda h, i: (h, i, 0, 0))] + [hbm] * ns,
        out_specs=(pl.BlockSpec((T, dv), lambda h, i: (i, h)),
                   pl.BlockSpec((None, None, 1, T), lambda h, i: (h, i, 0, 0))) + (hbm,) * ns,
        scratch_shapes=[pltpu.VMEM((T, T), F32), pltpu.VMEM((T, T), F32)] + side_sems,
        compiler_params=_params(("arbitrary", "arbitrary")),
    )(q, k, v, bias["ck_row"], bias["ref"], *(side[0] if side else ()))
    return o, lse + (bias["cq_row"] - bias["ref"]), side_out


def _attn_bwd(q, qoff, k, koff, v, voff, do, bias, lse_row, delta_row, heads, dk, dv, scale, name, dead,
              side=None, bias_grads=True):
    L = q.shape[0]
    T = ROW_TILE
    n = L // T
    c = scale * LOG2E
    dn_nt = (((1,), (1,)), ((), ()))
    dn_tn = (((0,), (0,)), ((), ()))
    ns = len(side[0]) if side else 0

    def body(*refs):
        q_ref, k_ref, v_ref, do_ref, cq_ref, ck_ref, ref_ref, lse_ref, dl_ref = refs[:9]
        dq_hbm, dk_ref, dv_ref, dck_ref, dcq_ref = refs[9 + ns:14 + ns]
        dq_acc, sem, st_a, dp_a, st_b, dp_b = refs[14 + 2 * ns:20 + 2 * ns]
        h = pl.program_id(0)
        j = pl.program_id(1)
        _side_exchange(side, refs[9:9 + ns], refs[14 + ns:14 + 2 * ns], refs[20 + 2 * ns:],
                       (h == 0) & (j == 0), (h == heads - 1) & (j == n - 1))

        @pl.when(j == 0)
        def _():
            dq_acc[...] = jnp.zeros_like(dq_acc)
            dcq_ref[...] = jnp.zeros_like(dcq_ref)

        ref = ref_ref[...]

        def run(k0, R):
            kb = k_ref[k0:k0 + R, :]
            vb = v_ref[k0:k0 + R, :]
            bcol = _as_col(ck_ref[j][:, k0:k0 + R] - ref)

            def front(i, st_dst, dp_dst):
                start = pl.multiple_of(jnp.minimum(i, n - 1) * T, T)
                st_dst[0:R, :] = lax.dot_general(kb, q_ref[pl.ds(start, T), :], dn_nt, preferred_element_type=F32)
                dp_dst[0:R, :] = lax.dot_general(vb, do_ref[pl.ds(start, T), :], dn_nt, preferred_element_type=F32)

            def back(i, carry, st_cur, dp_cur, diag):
                dk_a, dv_a, dck_a = carry
                start = pl.multiple_of(i * T, T)
                qi = q_ref[pl.ds(start, T), :]
                doi = do_ref[pl.ds(start, T), :]
                arow = (cq_ref[i] - ref) - lse_ref[i]
                st = st_cur[0:R, :] * c + arow - bcol
                if diag:
                    r = lax.broadcasted_iota(jnp.int32, (R, T), 0) + k0
                    cc = lax.broadcasted_iota(jnp.int32, (R, T), 1)
                    st = jnp.where(r <= cc, st, NEG)
                pt = jnp.exp2(st)
                dv_a = dv_a + jnp.dot(pt.astype(_MXU), doi, preferred_element_type=F32)
                dst = pt * (dp_cur[0:R, :] - dl_ref[i])
                if bias_grads:
                    dck_a = dck_a - jnp.sum(dst, axis=1, keepdims=True)
                    dcq_ref[i] += jnp.sum(dst, axis=0, keepdims=True)
                dsb = (dst * scale).astype(_MXU)
                dk_a = dk_a + jnp.dot(dsb, qi, preferred_element_type=F32)
                dq_acc[pl.ds(start, T), :] += lax.dot_general(dsb, kb, dn_tn, preferred_element_type=F32)
                return dk_a, dv_a, dck_a

            buf_a, buf_b = (st_a, dp_a), (st_b, dp_b)

            def step(i, carry, cur, nxt, diag=False):
                front(i + 1, *nxt)
                return back(i, carry, *cur, diag)

            def run_of(i0, steps, carry):
                for u in range(0, steps, 2):
                    carry = step(i0 + u + 1, step(i0 + u, carry, buf_b, buf_a), buf_a, buf_b)
                return carry

            front(j, *buf_a)
            init = (jnp.zeros((R, dk), F32), jnp.zeros((R, dv), F32), jnp.zeros((R, 1), F32))
            carry = step(j, init, buf_a, buf_b, True)
            big = (n - 1 - j) // BWD_UNROLL
            carry = lax.fori_loop(0, big, lambda g, cy: run_of(j + 1 + BWD_UNROLL * g, BWD_UNROLL, cy), carry)
            done = j + 1 + BWD_UNROLL * big
            carry = lax.fori_loop(0, (n - done) // 2, lambda g, cy: run_of(done + 2 * g, 2, cy), carry)
            dk_a, dv_a, dck_a = lax.cond(((n - done) % 2) == 1, lambda cy: step(n - 1, cy, buf_b, buf_a),
                                         lambda cy: cy, carry)
            if k0:
                dk_ref[0:k0, :] = jnp.zeros((k0, dk), F32)
                dv_ref[0:k0, :] = jnp.zeros((k0, dv), F32)
                dck_ref[:, 0:k0] = jnp.zeros((1, k0), F32)
            dk_ref[k0:k0 + R, :] = dk_a
            dv_ref[k0:k0 + R, :] = dv_a
            dck_ref[:, k0:k0 + R] = _as_row(dck_a)

        if dead:
            pl.when(j == 0)(lambda: run(dead, T - dead))
            pl.when(j > 0)(lambda: run(0, T))
        else:
            run(0, T)

        @pl.when(j == n - 1)
        def _():
            cp = pltpu.make_async_copy(dq_acc, dq_hbm.at[:, pl.ds(pl.multiple_of(h * dk, dk), dk)], sem)
            cp.start()
            cp.wait()

    hbm = pl.BlockSpec(memory_space=pl.ANY)
    side_shapes, side_sems = _exchange_shapes(*side) if side else ((), [])
    dq, dk_, dv_, dck, dcq, *side_out = pl.pallas_call(
        body, name=name,
        out_shape=(jax.ShapeDtypeStruct((L, heads * dk), F32), jax.ShapeDtypeStruct((L, heads * dk), F32),
                   jax.ShapeDtypeStruct((L, heads * dv), F32), jax.ShapeDtypeStruct((heads, n, 1, T), F32),
                   jax.ShapeDtypeStruct((heads, n, 1, T), F32)) + side_shapes,
        grid=(heads, n),
        in_specs=[pl.BlockSpec((L, dk), lambda h, j: (0, qoff + h)),
                  pl.BlockSpec((T, dk), lambda h, j: (j, koff + h)),
                  pl.BlockSpec((T, dv), lambda h, j: (j, voff + h)),
                  pl.BlockSpec((L, dv), lambda h, j: (0, h)),
                  pl.BlockSpec((None, n, 1, T), lambda h, j: (h, 0, 0, 0)),
                  pl.BlockSpec((None, n, 1, T), lambda h, j: (h, 0, 0, 0)),
                  pl.BlockSpec((None, None, 1, 1), lambda h, j: (h, j, 0, 0)),
                  pl.BlockSpec((None, n, 1, T), lambda h, j: (h, 0, 0, 0)),
                  pl.BlockSpec((None, n, 1, T), lambda h, j: (h, 0, 0, 0))] + [hbm] * ns,
        out_specs=(hbm,
                   pl.BlockSpec((T, dk), lambda h, j: (j, h)),
                   pl.BlockSpec((T, dv), lambda h, j: (j, h)),
                   pl.BlockSpec((None, None, 1, T), lambda h, j: (h, j, 0, 0)),
                   pl.BlockSpec((None, n, 1, T), lambda h, j: (h, 0, 0, 0))) + (hbm,) * ns,
        scratch_shapes=[pltpu.VMEM((L, dk), F32), pltpu.SemaphoreType.DMA] + [pltpu.VMEM((T, T), F32)] * 4
        + side_sems,
        compiler_params=_params(("arbitrary", "arbitrary")),
    )(q, k, v, do, bias["cq_row"], bias["ck_row"], bias["ref"], lse_row, delta_row, *(side[0] if side else ()))
    return dq, dk_, dv_, dck, dcq, side_out


_CQ0, _CKV0, _KR0 = D_MODEL, D_MODEL + MLA_Q_LORA, D_MODEL + MLA_Q_LORA + MLA_KV_LORA
_MLA_PROJ = _KR0 + LANES


def _rms(x, g):
    ms = jnp.mean(x * x, axis=-1, keepdims=True)
    return x * lax.rsqrt(ms + NORM_EPS) * g


def _rms_bwd(x, g, dy):
    ms = jnp.mean(x * x, axis=-1, keepdims=True)
    r = lax.rsqrt(ms + NORM_EPS)
    xh = x * r
    dxh = dy * g
    dx = r * (dxh - xh * jnp.mean(dxh * xh, axis=-1, keepdims=True))
    return dx, jnp.sum(dy * xh, axis=0, keepdims=True)


def _mla_norm_fwd(proj, gq, gkv):
    L = proj.shape[0]

    def body(p_ref, gq_ref, gkv_ref, cq_ref, ckv_ref):
        cq_ref[...] = _rms(p_ref[:, _CQ0:_CKV0], gq_ref[...]).astype(_MXU)
        ckv_ref[...] = _rms(p_ref[:, _CKV0:_KR0], gkv_ref[...]).astype(_MXU)

    return pl.pallas_call(
        body, name="mla_norm_fwd",
        out_shape=(jax.ShapeDtypeStruct((L, MLA_Q_LORA), _MXU), jax.ShapeDtypeStruct((L, MLA_KV_LORA), _MXU)),
        grid=(L // ROW_TILE,),
        in_specs=[_rows(_MLA_PROJ), _whole((1, MLA_Q_LORA)), _whole((1, MLA_KV_LORA))],
        out_specs=(_rows(MLA_Q_LORA), _rows(MLA_KV_LORA)),
        compiler_params=_params(("parallel",)),
    )(proj, gq, gkv)


def _mla_norm_bwd(proj, gq, gkv, dcqn, dckvn):
    L = proj.shape[0]

    def body(p_ref, gq_ref, gkv_ref, dq_ref, dkv_ref, dcq_ref, dckv_ref, dgq_ref, dgkv_ref):
        @pl.when(pl.program_id(0) == 0)
        def _():
            dgq_ref[...] = jnp.zeros_like(dgq_ref)
            dgkv_ref[...] = jnp.zeros_like(dgkv_ref)

        dx, dg = _rms_bwd(p_ref[:, _CQ0:_CKV0], gq_ref[...], dq_ref[...])
        dcq_ref[...] = dx.astype(_MXU)
        dgq_ref[...] += dg
        dx, dg = _rms_bwd(p_ref[:, _CKV0:_KR0], gkv_ref[...], dkv_ref[...])
        dckv_ref[...] = dx.astype(_MXU)
        dgkv_ref[...] += dg

    return pl.pallas_call(
        body, name="mla_norm_bwd",
        out_shape=(jax.ShapeDtypeStruct((L, MLA_Q_LORA), _MXU), jax.ShapeDtypeStruct((L, MLA_KV_LORA), _MXU),
                   jax.ShapeDtypeStruct((1, MLA_Q_LORA), F32), jax.ShapeDtypeStruct((1, MLA_KV_LORA), F32)),
        grid=(L // ROW_TILE,),
        in_specs=[_rows(_MLA_PROJ), _whole((1, MLA_Q_LORA)), _whole((1, MLA_KV_LORA)),
                  _rows(MLA_Q_LORA), _rows(MLA_KV_LORA)],
        out_specs=(_rows(MLA_Q_LORA), _rows(MLA_KV_LORA), _whole((1, MLA_Q_LORA)), _whole((1, MLA_KV_LORA))),
        compiler_params=_params(("arbitrary",)),
    )(proj, gq, gkv, dcqn, dckvn)


def _rot_tile(t, cos, sa, sb):
    half = MLA_ROPE // 2
    return t * cos + pltpu.roll(t, LANES - half, 1) * sa + pltpu.roll(t, half, 1) * sb


def _mla_rope_fwd(qf, kv, proj, cos, sa, sb):
    L = qf.shape[0]
    H = MLA_HEADS

    def body(q_ref, kv_ref, p_ref, c_ref, sa_ref, sb_ref, qo_ref, ko_ref):
        cos, sa, sb = c_ref[...], sa_ref[...], sb_ref[...]
        kr = _rot_tile(p_ref[...], cos, sa, sb).astype(_MXU)
        for h in range(H):
            b = h * MLA_DK
            qo_ref[:, b:b + LANES] = q_ref[:, b:b + LANES].astype(_MXU)
            qo_ref[:, b + LANES:b + 2 * LANES] = _rot_tile(q_ref[:, b + LANES:b + 2 * LANES], cos, sa, sb).astype(_MXU)
            ko_ref[:, b:b + LANES] = kv_ref[:, h * LANES:(h + 1) * LANES]
            ko_ref[:, b + LANES:b + 2 * LANES] = kr

    W = H * MLA_DK
    return pl.pallas_call(
        body, name="mla_rope_fwd",
        out_shape=(jax.ShapeDtypeStruct((L, W), _MXU), jax.ShapeDtypeStruct((L, W), _MXU)),
        grid=(L // ROW_TILE,),
        in_specs=[_rows(W), _rows(kv.shape[1]), pl.BlockSpec((ROW_TILE, LANES), lambda i: (i, _KR0 // LANES)),
                  _rows(LANES), _rows(LANES), _rows(LANES)],
        out_specs=(_rows(W), _rows(W)),
        compiler_params=_params(("parallel",)),
    )(qf, kv, proj, cos, sa, sb)


def _mla_rope_bwd(dq_full, dk_full, cos, sa, sb):
    L = dq_full.shape[0]
    H = MLA_HEADS
    W = H * MLA_DK

    def body(dq_ref, dk_ref, c_ref, sa_ref, sb_ref, dqf_ref, dkn_ref, dkr_ref):
        cos, sa, sb = c_ref[...], sa_ref[...], sb_ref[...]
        lane = lax.broadcasted_iota(jnp.int32, (ROW_TILE, LANES), 1)
        live = lane < MLA_ROPE
        krs = jnp.zeros((ROW_TILE, LANES), F32)
        for h in range(H):
            b = h * MLA_DK
            dqf_ref[:, b:b + LANES] = dq_ref[:, b:b + LANES].astype(_MXU)
            dqr = _rot_tile(dq_ref[:, b + LANES:b + 2 * LANES], cos, sa, sb)
            dqf_ref[:, b + LANES:b + 2 * LANES] = jnp.where(live, dqr, 0.0).astype(_MXU)
            dkn_ref[:, h * LANES:(h + 1) * LANES] = dk_ref[:, b:b + LANES].astype(_MXU)
            krs = krs + dk_ref[:, b + LANES:b + 2 * LANES]
        dkr_ref[...] = jnp.where(live, _rot_tile(krs, cos, sa, sb), 0.0).astype(_MXU)

    return pl.pallas_call(
        body, name="mla_rope_bwd",
        out_shape=(jax.ShapeDtypeStruct((L, W), _MXU), jax.ShapeDtypeStruct((L, H * LANES), _MXU),
                   jax.ShapeDtypeStruct((L, LANES), _MXU)),
        grid=(L // ROW_TILE,),
        in_specs=[_rows(W), _rows(W), _rows(LANES), _rows(LANES), _rows(LANES)],
        out_specs=(_rows(W), _rows(H * LANES), _rows(LANES)),
        compiler_params=_params(("parallel",)),
    )(dq_full, dk_full, cos, sa, sb)


_RET_QKW = RET_HEADS * RET_QK


def _ret_rope_fwd(qkz, cos, sin, pad):
    L = qkz.shape[0]
    hh = RET_QK // 2
    kscale = RET_QK ** -0.5

    def body(p_ref, c_ref, s_ref, q_ref, k_ref):
        cos, sin = c_ref[...], s_ref[...]
        row = pl.program_id(0) * ROW_TILE + lax.broadcasted_iota(jnp.int32, (ROW_TILE, hh), 0)
        keep = row >= pad
        for h in range(RET_HEADS):
            b = h * RET_QK
            t1, t2 = p_ref[:, b:b + hh], p_ref[:, b + hh:b + 2 * hh]
            q_ref[:, b:b + hh] = (t1 * cos - t2 * sin).astype(_MXU)
            q_ref[:, b + hh:b + 2 * hh] = (t2 * cos + t1 * sin).astype(_MXU)
            t1, t2 = p_ref[:, _RET_QKW + b:_RET_QKW + b + hh], p_ref[:, _RET_QKW + b + hh:_RET_QKW + b + 2 * hh]
            k_ref[:, b:b + hh] = jnp.where(keep, (t1 * cos - t2 * sin) * kscale, 0.0)
            k_ref[:, b + hh:b + 2 * hh] = jnp.where(keep, (t2 * cos + t1 * sin) * kscale, 0.0)

    return pl.pallas_call(
        body, name="ret_rope_fwd",
        out_shape=(jax.ShapeDtypeStruct((L, _RET_QKW), _MXU), jax.ShapeDtypeStruct((L, _RET_QKW), F32)),
        grid=(L // ROW_TILE,),
        in_specs=[_rows(2 * _RET_QKW), _rows(hh), _rows(hh)],
        out_specs=(_rows(_RET_QKW), _rows(_RET_QKW)),
        compiler_params=_params(("parallel",)),
    )(qkz, cos, sin)


def _ret_rope_bwd(dqr, dkr, cos, sin, pad):
    L = dqr.shape[0]
    hh = RET_QK // 2
    kscale = RET_QK ** -0.5

    def body(dq_ref, dk_ref, c_ref, s_ref, oq_ref, ok_ref):
        cos, sin = c_ref[...], s_ref[...]
        row = pl.program_id(0) * ROW_TILE + lax.broadcasted_iota(jnp.int32, (ROW_TILE, hh), 0)
        keep = row >= pad
        for h in range(RET_HEADS):
            b = h * RET_QK
            d1, d2 = dq_ref[:, b:b + hh], dq_ref[:, b + hh:b + 2 * hh]
            oq_ref[:, b:b + hh] = (d1 * cos + d2 * sin).astype(_MXU)
            oq_ref[:, b + hh:b + 2 * hh] = (d2 * cos - d1 * sin).astype(_MXU)
            d1, d2 = dk_ref[:, b:b + hh], dk_ref[:, b + hh:b + 2 * hh]
            ok_ref[:, b:b + hh] = jnp.where(keep, (d1 * cos + d2 * sin) * kscale, 0.0).astype(_MXU)
            ok_ref[:, b + hh:b + 2 * hh] = jnp.where(keep, (d2 * cos - d1 * sin) * kscale, 0.0).astype(_MXU)

    return pl.pallas_call(
        body, name="ret_rope_bwd",
        out_shape=(jax.ShapeDtypeStruct((L, _RET_QKW), _MXU), jax.ShapeDtypeStruct((L, _RET_QKW), _MXU)),
        grid=(L // ROW_TILE,),
        in_specs=[_rows(_RET_QKW), _rows(_RET_QKW), _rows(hh), _rows(hh)],
        out_specs=(_rows(_RET_QKW), _rows(_RET_QKW)),
        compiler_params=_params(("parallel",)),
    )(dqr, dkr, cos, sin)


def _ret_rows(L):
    return next(r * ROW_TILE for r in (3, 2, 1) if L % (r * ROW_TILE) == 0)


def _ret_decays():
    lg = np.log1p(-np.exp2(-5.0 - np.arange(RET_HEADS, dtype=np.float32))).astype(np.float32)
    i = np.arange(CHUNK, dtype=np.float32)
    rel = i[:, None] - i[None, :]
    dm = np.where(rel[None] >= 0, np.exp(rel[None] * lg[:, None, None]), 0.0).astype(np.float32)
    qd = np.exp((i[None, :] + 1.0) * lg[:, None]).astype(np.float32)[:, :, None]
    kd = np.exp((CHUNK - 1.0 - i)[None, :] * lg[:, None]).astype(np.float32)[:, :, None]
    cd = np.broadcast_to(np.exp(CHUNK * lg).astype(np.float32)[:, None, None], (RET_HEADS, 1, RET_V))
    return jnp.asarray(dm), jnp.asarray(qd), jnp.asarray(kd), jnp.asarray(np.ascontiguousarray(cd))


def _ret_fwd(qr, kr, v):
    L = qr.shape[0]
    nc = L // CHUNK
    RB = _ret_rows(L)
    G = RB // CHUNK
    dm, qd, kd, cd = _ret_decays()
    dn_nt = (((1,), (1,)), ((), ()))
    dn_tn = (((0,), (0,)), ((), ()))

    def body(q_ref, k_ref, v_ref, dm_ref, qd_ref, kd_ref, cd_ref, o_ref, st_ref, s_ref):
        @pl.when(pl.program_id(1) == 0)
        def _():
            s_ref[...] = jnp.zeros_like(s_ref)

        state = s_ref[...]
        for g in range(G):
            rows = slice(g * CHUNK, (g + 1) * CHUNK)
            q, k, vv = q_ref[rows, :], k_ref[rows, :], v_ref[rows, :]
            sb = state.astype(_MXU)
            st_ref[g] = sb
            s = lax.dot_general(q, k.astype(_MXU), dn_nt, preferred_element_type=F32) * dm_ref[...]
            intra = jnp.dot(s.astype(_MXU), vv, preferred_element_type=F32)
            cross = jnp.dot(q, sb, preferred_element_type=F32) * qd_ref[...]
            o_ref[rows, :] = intra + cross
            kdm = (k * kd_ref[...]).astype(_MXU)
            state = cd_ref[...] * state + lax.dot_general(kdm, vv, dn_tn, preferred_element_type=F32)
        s_ref[...] = state

    return pl.pallas_call(
        body, name="ret_fwd",
        out_shape=(jax.ShapeDtypeStruct((L, RET_HEADS * RET_V), F32),
                   jax.ShapeDtypeStruct((RET_HEADS, nc, RET_QK, RET_V), _MXU)),
        grid=(RET_HEADS, nc // G),
        in_specs=[pl.BlockSpec((RB, RET_QK), lambda h, c: (c, h)),
                  pl.BlockSpec((RB, RET_QK), lambda h, c: (c, h)),
                  pl.BlockSpec((RB, RET_V), lambda h, c: (c, h)),
                  pl.BlockSpec((None, CHUNK, CHUNK), lambda h, c: (h, 0, 0)),
                  pl.BlockSpec((None, CHUNK, 1), lambda h, c: (h, 0, 0)),
                  pl.BlockSpec((None, CHUNK, 1), lambda h, c: (h, 0, 0)),
                  pl.BlockSpec((None, 1, RET_V), lambda h, c: (h, 0, 0))],
        out_specs=(pl.BlockSpec((RB, RET_V), lambda h, c: (c, h)),
                   pl.BlockSpec((None, G, RET_QK, RET_V), lambda h, c: (h, c, 0, 0))),
        scratch_shapes=[pltpu.VMEM((RET_QK, RET_V), F32)],
        compiler_params=_params(("parallel", "arbitrary")),
    )(qr, kr, v, dm, qd, kd, cd)


def _ret_bwd(qr, kr, v, do, states):
    L = qr.shape[0]
    nc = L // CHUNK
    RB = _ret_rows(L)
    G = RB // CHUNK
    nt = nc // G
    dm, qd, kd, cd = _ret_decays()
    dn_nt = (((1,), (1,)), ((), ()))
    dn_tn = (((0,), (0,)), ((), ()))

    def body(q_ref, k_ref, v_ref, do_ref, st_ref, dm_ref, qd_ref, kd_ref, cd_ref,
             dq_ref, dk_ref, dv_ref, ds_ref):
        @pl.when(pl.program_id(1) == 0)
        def _():
            ds_ref[...] = jnp.zeros_like(ds_ref)

        dstate = ds_ref[...]
        dmat = dm_ref[...]
        for g in reversed(range(G)):
            rows = slice(g * CHUNK, (g + 1) * CHUNK)
            q, k, vv, do = q_ref[rows, :], k_ref[rows, :], v_ref[rows, :], do_ref[rows, :]
            kb = k.astype(_MXU)
            dob = do.astype(_MXU)
            doq = (do * qd_ref[...]).astype(_MXU)
            dsb = dstate.astype(_MXU)
            s = (lax.dot_general(q, kb, dn_nt, preferred_element_type=F32) * dmat).astype(_MXU)
            draw = (lax.dot_general(dob, vv, dn_nt, preferred_element_type=F32) * dmat).astype(_MXU)
            kdm = (k * kd_ref[...]).astype(_MXU)
            dv = (lax.dot_general(s, dob, dn_tn, preferred_element_type=F32)
                  + jnp.dot(kdm, dsb, preferred_element_type=F32))
            dq = (jnp.dot(draw, kb, preferred_element_type=F32)
                  + lax.dot_general(doq, st_ref[g], dn_nt, preferred_element_type=F32))
            dk = (lax.dot_general(draw, q, dn_tn, preferred_element_type=F32)
                  + lax.dot_general(vv, dsb, dn_nt, preferred_element_type=F32) * kd_ref[...])
            dq_ref[rows, :] = dq
            dk_ref[rows, :] = dk
            dv_ref[rows, :] = dv.astype(_MXU)
            dstate = cd_ref[...] * dstate + lax.dot_general(q, doq, dn_tn, preferred_element_type=F32)
        ds_ref[...] = dstate

    rev = lambda h, c: (nt - 1 - c, h)
    return pl.pallas_call(
        body, name="ret_bwd",
        out_shape=(jax.ShapeDtypeStruct((L, _RET_QKW), F32), jax.ShapeDtypeStruct((L, _RET_QKW), F32),
                   jax.ShapeDtypeStruct((L, RET_HEADS * RET_V), _MXU)),
        grid=(RET_HEADS, nt),
        in_specs=[pl.BlockSpec((RB, RET_QK), rev),
                  pl.BlockSpec((RB, RET_QK), rev),
                  pl.BlockSpec((RB, RET_V), rev),
                  pl.BlockSpec((RB, RET_V), rev),
                  pl.BlockSpec((None, G, RET_QK, RET_V), lambda h, c: (h, nt - 1 - c, 0, 0)),
                  pl.BlockSpec((None, CHUNK, CHUNK), lambda h, c: (h, 0, 0)),
                  pl.BlockSpec((None, CHUNK, 1), lambda h, c: (h, 0, 0)),
                  pl.BlockSpec((None, CHUNK, 1), lambda h, c: (h, 0, 0)),
                  pl.BlockSpec((None, 1, RET_V), lambda h, c: (h, 0, 0))],
        out_specs=(pl.BlockSpec((RB, RET_QK), rev), pl.BlockSpec((RB, RET_QK), rev),
                   pl.BlockSpec((RB, RET_V), rev)),
        scratch_shapes=[pltpu.VMEM((RET_QK, RET_V), F32)],
        compiler_params=_params(("parallel", "arbitrary")),
    )(qr, kr, v, do, states, dm, qd, kd, cd)


def _gn(o, g):
    mu = jnp.mean(o, axis=-1, keepdims=True)
    oc = o - mu
    var = jnp.mean(oc * oc, axis=-1, keepdims=True)
    return oc * lax.rsqrt(var + NORM_EPS) * g


def _ret_gate_fwd(o, qkz, gn_g):
    L, W = o.shape
    zblk = qkz.shape[1] // W - 1

    def body(o_ref, z_ref, g_ref, y_ref):
        for h in range(RET_HEADS):
            sl = slice(h * RET_V, (h + 1) * RET_V)
            y_ref[:, sl] = (_gn(o_ref[:, sl], g_ref[:, sl]) * _silu(z_ref[:, sl])).astype(_MXU)

    return pl.pallas_call(
        body, name="ret_gate_fwd",
        out_shape=jax.ShapeDtypeStruct((L, W), _MXU),
        grid=(L // ROW_TILE,),
        in_specs=[_rows(W), pl.BlockSpec((ROW_TILE, W), lambda i: (i, zblk)), _whole((1, W))],
        out_specs=_rows(W),
        compiler_params=_params(("parallel",)),
    )(o, qkz, gn_g)


def _ret_gate_bwd(dy, o, qkz, gn_g):
    L, W = o.shape
    zblk = qkz.shape[1] // W - 1

    def body(dy_ref, o_ref, z_ref, g_ref, do_ref, dz_ref, dg_ref):
        @pl.when(pl.program_id(0) == 0)
        def _():
            dg_ref[...] = jnp.zeros_like(dg_ref)

        for h in range(RET_HEADS):
            sl = slice(h * RET_V, (h + 1) * RET_V)
            z = z_ref[:, sl]
            o = o_ref[:, sl]
            g = g_ref[:, sl]
            dy = dy_ref[:, sl]
            mu = jnp.mean(o, axis=-1, keepdims=True)
            oc = o - mu
            var = jnp.mean(oc * oc, axis=-1, keepdims=True)
            r = lax.rsqrt(var + NORM_EPS)
            xh = oc * r
            dn = dy * _silu(z)
            dz_ref[:, sl] = (dy * (xh * g) * _dsilu(z)).astype(_MXU)
            dg_ref[:, sl] += jnp.sum(dn * xh, axis=0, keepdims=True)
            dxh = dn * g
            do_ref[:, sl] = r * (dxh - jnp.mean(dxh, axis=-1, keepdims=True)
                                 - xh * jnp.mean(dxh * xh, axis=-1, keepdims=True))

    return pl.pallas_call(
        body, name="ret_gate_bwd",
        out_shape=(jax.ShapeDtypeStruct((L, W), F32), jax.ShapeDtypeStruct((L, W), _MXU),
                   jax.ShapeDtypeStruct((1, W), F32)),
        grid=(L // ROW_TILE,),
        in_specs=[_rows(W), _rows(W), pl.BlockSpec((ROW_TILE, W), lambda i: (i, zblk)), _whole((1, W))],
        out_specs=(_rows(W), _rows(W), _whole((1, W))),
        compiler_params=_params(("arbitrary",)),
    )(dy, o, qkz, gn_g)


def _dead_rows(pad):
    return pad // LANES * LANES


def _heads_t(a):
    return a[:, :FOX_HEADS].T


def _fox_fwd(h, w, lnw, pad, tag, side=None):
    w_in, b_f, w_out = w
    L = h.shape[0]
    W = D_MODEL
    w_qkv = w_in[:, :3 * W]
    w_zf = jnp.concatenate([w_in[:, 3 * W:4 * W], jnp.pad(w_in[:, 4 * W:], ((0, 0), (0, LANES - FOX_HEADS)))], axis=1)
    bias = jnp.pad(b_f, (0, LANES - FOX_HEADS))[None, :]
    qkv = _matmul(h, w_qkv, "nn", _MXU, f"{tag}_qkv")
    zf = _matmul(h, w_zf, "nn", F32, f"{tag}_zf")
    cum = _fox_cum(zf, bias, pad, f"{tag}_cum")
    ab = _bias_terms(_heads_t(cum), pad)
    nb = W // LANES
    o, lse, side_out = _attn_fwd(qkv, 0, qkv, nb, qkv, 2 * nb, ab, FOX_HEADS, LANES, LANES, LANES ** -0.5,
                                 f"{tag}_attn_fwd", _dead_rows(pad), side)
    y = _gate_fwd(o, zf, f"{tag}_gate_fwd")
    mo = _matmul(y, w_out, "nn", F32, f"{tag}_out")
    zl, hn = _ln_fwd(h, mo, lnw[0], lnw[1], f"{tag}_ln_fwd")
    return hn, dict(h=h, qkv=qkv, zf=zf, ab=ab, o=o, lse=lse, y=y, zl=zl,
                    w_all=jnp.concatenate([w_qkv, w_zf], axis=1), w_out=w_out, bias=bias), side_out


def _fox_bwd(s, dzl, pad, tag, side=None, send_wout=False):
    W = D_MODEL
    nb = W // LANES
    dy = _matmul(dzl, s["w_out"], "nt", F32, f"{tag}_dy")
    d_wout = _matmul(s["y"], dzl, "tn", F32, f"{tag}_dwout")
    if send_wout:
        side = (list(side[0]) + [_split_layer("fox_w_out", d_wout)], side[1])
    do, delta, dzg = _gate_bwd(dy, s["o"], s["zf"], FOX_HEADS, f"{tag}_gate_bwd")
    qkv = s["qkv"]
    dq, dk, dv, dck, dcq, side_out = _attn_bwd(qkv, 0, qkv, nb, qkv, 2 * nb, do, s["ab"], s["lse"], delta,
                                               FOX_HEADS, LANES, LANES, LANES ** -0.5, f"{tag}_attn_bwd",
                                               _dead_rows(pad), side)
    dcum = jnp.pad((dck + dcq).reshape(FOX_HEADS, -1).T, ((0, 0), (0, LANES - FOX_HEADS)))
    df, dbias = _fox_cum_bwd(dcum, s["zf"], s["bias"], pad, f"{tag}_cum_bwd")
    dproj = jnp.concatenate([dq.astype(_MXU), dk.astype(_MXU), dv.astype(_MXU), dzg, df], axis=1)
    d_wall = _matmul(s["h"], dproj, "tn", F32, f"{tag}_dwin")
    dh = _matmul(dproj, s["w_all"], "nt", F32, f"{tag}_dh")
    d_win = jnp.concatenate([d_wall[:, :4 * W], d_wall[:, 4 * W:4 * W + FOX_HEADS]], axis=1)
    return dh, (d_win, dbias[0, :FOX_HEADS], d_wout), side_out


def _mla_tables(L, pad):
    pos = (jnp.arange(L) - pad).astype(F32)
    inv = ROPE_BASE ** (-jnp.arange(0, MLA_ROPE, 2, dtype=F32) / MLA_ROPE)
    ang = pos[:, None] * inv[None, :]
    c, s = jnp.cos(ang), jnp.sin(ang)
    z32, z64 = jnp.zeros_like(s), jnp.zeros((L, LANES - MLA_ROPE), F32)
    cos = jnp.concatenate([c, c, jnp.ones_like(z64)], axis=1)
    sa = jnp.concatenate([-s, z32, z64], axis=1)
    sb = jnp.concatenate([z32, s, z64], axis=1)
    return cos, sa, sb


def _mla_weights(w_in, w_uq, w_ukv):
    a, b = MLA_Q_LORA, MLA_Q_LORA + MLA_KV_LORA
    w_zc = jnp.concatenate([w_in[:, b + MLA_ROPE:], w_in[:, :b + MLA_ROPE],
                            jnp.zeros((D_MODEL, LANES - MLA_ROPE), w_in.dtype)], axis=1)
    uq = w_uq.reshape(MLA_Q_LORA, MLA_HEADS, MLA_NOPE + MLA_ROPE)
    uq = jnp.pad(uq, ((0, 0), (0, 0), (0, MLA_DK - MLA_NOPE - MLA_ROPE))).reshape(MLA_Q_LORA, MLA_HEADS * MLA_DK)
    ukv = w_ukv.reshape(MLA_KV_LORA, MLA_HEADS, MLA_NOPE + MLA_V)
    ukv = jnp.concatenate([ukv[:, :, :MLA_NOPE].reshape(MLA_KV_LORA, -1), ukv[:, :, MLA_NOPE:].reshape(MLA_KV_LORA, -1)], axis=1)
    return w_zc, uq, ukv


def _mla_fwd(h, w, lnw, pad):
    w_in, gq, gkv, w_uq, w_ukv, w_out = w
    L = h.shape[0]
    w_zc, uq, ukv = _mla_weights(w_in, w_uq, w_ukv)
    cos, sa, sb = _mla_tables(L, pad)
    proj = _matmul(h, w_zc, "nn", F32, "mla_proj")
    cqn, ckvn = _mla_norm_fwd(proj, gq[None, :], gkv[None, :])
    qf = _matmul(cqn, uq, "nn", F32, "mla_uq")
    kv = _matmul(ckvn, ukv, "nn", _MXU, "mla_ukv")
    q_full, k_full = _mla_rope_fwd(qf, kv, proj, cos, sa, sb)
    ab = _bias_terms(jnp.zeros((MLA_HEADS, L), F32), pad)
    nb = MLA_HEADS * MLA_NOPE // LANES
    o, lse, _ = _attn_fwd(q_full, 0, k_full, 0, kv, nb, ab, MLA_HEADS, MLA_DK, MLA_V,
                          (MLA_NOPE + MLA_ROPE) ** -0.5, "mla_attn_fwd", _dead_rows(pad))
    y = _gate_fwd(o, proj, "mla_gate_fwd")
    mo = _matmul(y, w_out, "nn", F32, "mla_out")
    zl, hn = _ln_fwd(h, mo, lnw[0], lnw[1], "mla_ln_fwd")
    return hn, dict(h=h, proj=proj, cqn=cqn, ckvn=ckvn, kv=kv, q_full=q_full, k_full=k_full, ab=ab,
                    dead=_dead_rows(pad),
                    o=o, lse=lse, y=y, zl=zl, w_zc=w_zc, uq=uq, ukv=ukv, w_out=w_out, gq=gq, gkv=gkv,
                    tabs=(cos, sa, sb))


def _mla_bwd(s, dzl):
    H = MLA_HEADS
    dy = _matmul(dzl, s["w_out"], "nt", F32, "mla_dy")
    d_wout = _matmul(s["y"], dzl, "tn", F32, "mla_dwout")
    do, delta, dzg = _gate_bwd(dy, s["o"], s["proj"], H, "mla_gate_bwd")
    nb = H * MLA_NOPE // LANES
    dq_full, dk_full, dv, _, _, _ = _attn_bwd(s["q_full"], 0, s["k_full"], 0, s["kv"], nb, do, s["ab"],
                                              s["lse"], delta,
                                              H, MLA_DK, MLA_V, (MLA_NOPE + MLA_ROPE) ** -0.5, "mla_attn_bwd",
                                              s["dead"], bias_grads=False)
    cos, sa, sb = s["tabs"]
    dqf, dkn, dkr = _mla_rope_bwd(dq_full, dk_full, cos, -sa, -sb)
    d_uq = _matmul(s["cqn"], dqf, "tn", F32, "mla_duq")
    dcqn = _matmul(dqf, s["uq"], "nt", F32, "mla_dcqn")
    dkv = jnp.concatenate([dkn, dv.astype(_MXU)], axis=1)
    d_ukv = _matmul(s["ckvn"], dkv, "tn", F32, "mla_dukv")
    dckvn = _matmul(dkv, s["ukv"], "nt", F32, "mla_dckvn")
    dcq, dckv, dgq, dgkv = _mla_norm_bwd(s["proj"], s["gq"][None, :], s["gkv"][None, :], dcqn, dckvn)
    dproj = jnp.concatenate([dzg, dcq, dckv, dkr], axis=1)
    d_wzc = _matmul(s["h"], dproj, "tn", F32, "mla_dwin")
    dh = _matmul(dproj, s["w_zc"], "nt", F32, "mla_dh")
    W = D_MODEL
    d_win = jnp.concatenate([d_wzc[:, W:W + MLA_Q_LORA + MLA_KV_LORA + MLA_ROPE], d_wzc[:, :W]], axis=1)
    d_wuq = d_uq.reshape(MLA_Q_LORA, H, MLA_DK)[:, :, :MLA_NOPE + MLA_ROPE].reshape(MLA_Q_LORA, -1)
    hk = H * MLA_NOPE
    d_wukv = jnp.concatenate([d_ukv[:, :hk].reshape(MLA_KV_LORA, H, MLA_NOPE),
                              d_ukv[:, hk:].reshape(MLA_KV_LORA, H, MLA_V)], axis=2).reshape(MLA_KV_LORA, -1)
    return dh, (d_win, dgq[0], dgkv[0], d_wuq, d_wukv, d_wout)


def _ret_tables(L, pad):
    pos = (jnp.arange(L) - pad).astype(F32)
    inv = 1.0 / (ROPE_BASE ** jnp.linspace(0.0, 1.0, RET_QK // 2, dtype=F32))
    ang = pos[:, None] * inv[None, :]
    return jnp.cos(ang), jnp.sin(ang)


def _ret_layer_fwd(h, w, lnw, pad):
    w_in, gn_g, w_out = w
    L = h.shape[0]
    q1 = 2 * _RET_QKW
    v1 = q1 + RET_HEADS * RET_V
    w_qkz = jnp.concatenate([w_in[:, :q1], w_in[:, v1:]], axis=1)
    cos, sin = _ret_tables(L, pad)
    qkz = _matmul(h, w_qkz, "nn", F32, "ret_qkz")
    v = _matmul(h, w_in[:, q1:v1], "nn", _MXU, "ret_v")
    qr, kr = _ret_rope_fwd(qkz, cos, sin, pad)
    o, states = _ret_fwd(qr, kr, v)
    y = _ret_gate_fwd(o, qkz, gn_g[None, :])
    mo = _matmul(y, w_out, "nn", F32, "ret_out")
    zl, hn = _ln_fwd(h, mo, lnw[0], lnw[1], "ret_ln_fwd")
    return hn, dict(h=h, qkz=qkz, v=v, qr=qr, kr=kr, o=o, states=states, y=y, zl=zl, w_in=w_in, w_out=w_out,
                    gn_g=gn_g, tabs=(cos, sin))


def _ret_layer_bwd(s, dzl, pad):
    dy = _matmul(dzl, s["w_out"], "nt", F32, "ret_dy")
    d_wout = _matmul(s["y"], dzl, "tn", F32, "ret_dwout")
    do, dzg, dgn = _ret_gate_bwd(dy, s["o"], s["qkz"], s["gn_g"][None, :])
    dqr, dkr, dv = _ret_bwd(s["qr"], s["kr"], s["v"], do, s["states"])
    cos, sin = s["tabs"]
    dq, dk = _ret_rope_bwd(dqr, dkr, cos, sin, pad)
    dproj = jnp.concatenate([dq, dk, dv, dzg], axis=1)
    d_win = _matmul(s["h"], dproj, "tn", F32, "ret_dwin")
    dh = _matmul(dproj, s["w_in"], "nt", F32, "ret_dh")
    return dh, (d_win, dgn[0], d_wout)


def _local_step(x, target, meta, fox0_w, late_shards, small, ln_g, ln_b):
    n_real = x.shape[0]
    first = -(-N_META // ROW_TILE)
    pad = first * ROW_TILE - N_META
    h0 = jnp.concatenate([jnp.zeros((pad, D_MODEL), F32), meta, x], axis=0)
    lnw = [(ln_g[i][None, :], ln_b[i][None, :]) for i in range(4)]
    b_f = small["fox_b_f"]

    h1, s0, gathered = _fox_fwd(h0, (fox0_w[0], b_f[0], fox0_w[1]), lnw[0], pad, "fox0", (late_shards, False))
    w = {key: _merge_layer(key[0], g).astype(_MXU) for key, g in zip(_LATE, gathered)}
    h2, s1 = _mla_fwd(h1, (w["mla_w_in", 0], small["mla_q_norm"][0], small["mla_kv_norm"][0], w["mla_w_uq", 0],
                           w["mla_w_ukv", 0], w["mla_w_out", 0]), lnw[1], pad)
    h3, s2 = _ret_layer_fwd(h2, (w["ret_w_in", 0], small["ret_gn_g"][0], w["ret_w_out", 0]), lnw[2], pad)
    h4, s3, _ = _fox_fwd(h3, (w["fox_w_in", 1], b_f[1], w["fox_w_out", 1]), lnw[3], pad, "fox1")

    dh4, loss = _loss_head(h4, target, first)
    dz3, dg3, db3 = _ln_bwd(s3["zl"], lnw[3][0], dh4, None, "ln3_bwd")
    dm3, gf1, _ = _fox_bwd(s3, dz3, pad, "fox1")
    dz2, dg2, db2 = _ln_bwd(s2["zl"], lnw[2][0], dz3, dm3, "ln2_bwd")
    dm2, gr = _ret_layer_bwd(s2, dz2, pad)
    dz1, dg1, db1 = _ln_bwd(s1["zl"], lnw[1][0], dz2, dm2, "ln1_bwd")
    dm1, gm = _mla_bwd(s1, dz1)
    dz0, dg0, db0 = _ln_bwd(s0["zl"], lnw[0][0], dz1, dm1, "ln0_bwd")
    late = {("fox_w_in", 1): gf1[0], ("fox_w_out", 1): gf1[2], ("mla_w_in", 0): gm[0], ("mla_w_uq", 0): gm[3],
            ("mla_w_ukv", 0): gm[4], ("mla_w_out", 0): gm[5], ("ret_w_in", 0): gr[0], ("ret_w_out", 0): gr[2]}
    dm0, gf0, sent = _fox_bwd(s0, dz0, pad, "fox0", ([_split_layer(k[0], late[k]) for k in _LATE], True), True)
    parts = dict(zip(_LATE + [("fox_w_out", 0)], sent))
    grad_x, grad_meta = _input_grads(dz0, dm0, first, n_real)

    small_grads = dict(
        meta=grad_meta, fox_b_f=jnp.stack([gf0[1], gf1[1]]), mla_q_norm=gm[1][None], mla_kv_norm=gm[2][None],
        ret_gn_g=gr[1][None], ln_g=jnp.concatenate([dg0, dg1, dg2, dg3], axis=0),
        ln_b=jnp.concatenate([db0, db1, db2, db3], axis=0))
    return loss, grad_x, gf0[0], parts, small_grads


def _exchange(tensors, scatter, name):
    nt = len(tensors)

    def body(*refs):
        copies = _exchange_copies(refs[:nt], refs[nt:2 * nt], *refs[2 * nt:], scatter)
        for cp in copies:
            cp.start()
        for cp in copies:
            cp.wait()

    hbm = pl.BlockSpec(memory_space=pl.ANY)
    out_shape, sems = _exchange_shapes(tensors, scatter)
    return pl.pallas_call(
        body, name=name, out_shape=out_shape, in_specs=[hbm] * nt, out_specs=(hbm,) * nt, scratch_shapes=sems,
    )(*tensors)


def _adamw(parts, w, m, v, name):
    R, C = w.shape
    tr = next(t for t in range(min(R, 256), 0, -1) if R % t == 0 and (t % 8 == 0 or t == R))

    def body(p_ref, w_ref, m_ref, v_ref, g_ref, d_ref, mo_ref, vo_ref):
        g = p_ref[0]
        for d in range(1, N_DEV):
            g = g + p_ref[d]
        mn = ADAM_B1 * m_ref[...] + (1.0 - ADAM_B1) * g
        vn = ADAM_B2 * v_ref[...] + (1.0 - ADAM_B2) * (g * g)
        m_hat = mn / (1.0 - ADAM_B1 ** ADAM_STEP)
        v_hat = vn / (1.0 - ADAM_B2 ** ADAM_STEP)
        g_ref[...] = g
        d_ref[...] = -ADAM_LR * (m_hat / (jnp.sqrt(v_hat) + ADAM_EPS) + ADAM_WD * w_ref[...])
        mo_ref[...] = mn
        vo_ref[...] = vn

    blk = pl.BlockSpec((tr, C), lambda i: (i, 0))
    sd = jax.ShapeDtypeStruct((R, C), F32)
    return pl.pallas_call(
        body, name=name,
        out_shape=(sd, sd, sd, sd),
        grid=(R // tr,),
        in_specs=[pl.BlockSpec((N_DEV, tr, C), lambda i: (0, i, 0)), blk, blk, blk],
        out_specs=(blk, blk, blk, blk),
        compiler_params=_params(("parallel",)),
    )(parts, w, m, v)


_SPECS = dict(
    meta=((16, 1024), 1),
    fox_w_in=((2, 1024, 4104), 2), fox_b_f=((2, 8), None), fox_w_out=((2, 1024, 1024), 1),
    mla_w_in=((1, 1024, 1728), 2), mla_q_norm=((1, 384), None), mla_kv_norm=((1, 256), None),
    mla_w_uq=((1, 384, 1536), 2), mla_w_ukv=((1, 256, 2048), 2), mla_w_out=((1, 1024, 1024), 1),
    ret_w_in=((1, 1024, 6144), 2), ret_gn_g=((1, 2048), 1), ret_w_out=((1, 2048, 1024), 1),
    ln_g=((4, 1024), None), ln_b=((4, 1024), None),
)
_NAMES = list(_SPECS)
_BIG = ["fox_w_in", "fox_w_out", "mla_w_in", "mla_w_uq", "mla_w_ukv", "mla_w_out", "ret_w_in", "ret_w_out"]
_SMALL = [n for n in _NAMES if n not in _BIG]
_SMALL_GATHERED = ["meta", "ret_gn_g"]
_PACK_ROWS = 8


def _shard_shape(name):
    shape, ax = _SPECS[name]
    if ax is None:
        return shape
    return tuple(s // N_DEV if i == ax else s for i, s in enumerate(shape))


_EARLY = [("fox_w_in", 0), ("fox_w_out", 0)]
_LATE = [("fox_w_in", 1), ("fox_w_out", 1), ("mla_w_in", 0), ("mla_w_uq", 0), ("mla_w_ukv", 0), ("mla_w_out", 0),
         ("ret_w_in", 0), ("ret_w_out", 0)]


def _split_at(full, ax):
    shape = full.shape
    parts = full.reshape(shape[:ax] + (N_DEV, shape[ax] // N_DEV) + shape[ax + 1:])
    return jnp.moveaxis(parts, ax, 0)


def _merge_at(stacked, ax):
    full = jnp.moveaxis(stacked, 0, ax)
    return full.reshape(full.shape[:ax] + (-1,) + full.shape[ax + 2:])


def _split(name, full):
    return _split_at(full, _SPECS[name][1])


def _merge(name, stacked):
    return _merge_at(stacked, _SPECS[name][1])


def _split_layer(name, full):
    return _split_at(full, _SPECS[name][1] - 1)


def _merge_layer(name, stacked):
    return _merge_at(stacked, _SPECS[name][1] - 1)


def _pad_rows(flat):
    n = flat.shape[-1]
    unit = _PACK_ROWS * LANES
    total = -(-n // unit) * unit
    flat = jnp.pad(flat, [(0, 0)] * (flat.ndim - 1) + [(0, total - n)])
    return flat.reshape(flat.shape[:-1] + (total // LANES, LANES))


def _pack_small(tree, names, tail):
    return _pad_rows(jnp.concatenate([tree[n].reshape(-1) for n in names] + [tail]))


def _unpack_small(buf, names):
    flat, out, at = buf.reshape(-1), {}, 0
    for n in names:
        shp = _shard_shape(n)
        out[n] = flat[at:at + math.prod(shp)].reshape(shp)
        at += math.prod(shp)
    return out, flat[at:]


def kernel(x, meta, fox_w_in, fox_b_f, fox_w_out, mla_w_in, mla_q_norm, mla_kv_norm, mla_w_uq, mla_w_ukv, mla_w_out, ret_w_in, ret_gn_g, ret_w_out, ln_g, ln_b, loss_target, m_meta, m_fox_w_in, m_fox_b_f, m_fox_w_out, m_mla_w_in, m_mla_q_norm, m_mla_kv_norm, m_mla_w_uq, m_mla_w_ukv, m_mla_w_out, m_ret_w_in, m_ret_gn_g, m_ret_w_out, m_ln_g, m_ln_b, v_meta, v_fox_w_in, v_fox_b_f, v_fox_w_out, v_mla_w_in, v_mla_q_norm, v_mla_kv_norm, v_mla_w_uq, v_mla_w_ukv, v_mla_w_out, v_ret_w_in, v_ret_gn_g, v_ret_w_out, v_ln_g, v_ln_b):
    w = dict(meta=meta, fox_w_in=fox_w_in, fox_b_f=fox_b_f, fox_w_out=fox_w_out, mla_w_in=mla_w_in,
             mla_q_norm=mla_q_norm, mla_kv_norm=mla_kv_norm, mla_w_uq=mla_w_uq, mla_w_ukv=mla_w_ukv,
             mla_w_out=mla_w_out, ret_w_in=ret_w_in, ret_gn_g=ret_gn_g, ret_w_out=ret_w_out, ln_g=ln_g, ln_b=ln_b)
    m = dict(meta=m_meta, fox_w_in=m_fox_w_in, fox_b_f=m_fox_b_f, fox_w_out=m_fox_w_out, mla_w_in=m_mla_w_in,
             mla_q_norm=m_mla_q_norm, mla_kv_norm=m_mla_kv_norm, mla_w_uq=m_mla_w_uq, mla_w_ukv=m_mla_w_ukv,
             mla_w_out=m_mla_w_out, ret_w_in=m_ret_w_in, ret_gn_g=m_ret_gn_g, ret_w_out=m_ret_w_out, ln_g=m_ln_g, ln_b=m_ln_b)
    v = dict(meta=v_meta, fox_w_in=v_fox_w_in, fox_b_f=v_fox_b_f, fox_w_out=v_fox_w_out, mla_w_in=v_mla_w_in,
             mla_q_norm=v_mla_q_norm, mla_kv_norm=v_mla_kv_norm, mla_w_uq=v_mla_w_uq, mla_w_ukv=v_mla_w_ukv,
             mla_w_out=v_mla_w_out, ret_w_in=v_ret_w_in, ret_gn_g=v_ret_gn_g, ret_w_out=v_ret_w_out, ln_g=v_ln_g, ln_b=v_ln_b)

    def shard16(key):
        return w[key[0]][key[1]].astype(jnp.bfloat16)

    none = jnp.zeros((0,), F32)
    gathered = _exchange([shard16(k) for k in _EARLY] + [_pack_small(w, _SMALL_GATHERED, none)],
                         False, "weights_all_gather")
    fox0_w = [_merge_layer(k[0], g).astype(_MXU) for k, g in zip(_EARLY, gathered)]
    small = {n: w[n] for n in _SMALL}
    pieces = [_unpack_small(gathered[-1][d], _SMALL_GATHERED)[0] for d in range(N_DEV)]
    for n in _SMALL_GATHERED:
        small[n] = _merge(n, jnp.stack([p[n] for p in pieces]))

    loss, grad_x, d_win0, by_key, grads = _local_step(
        x[0], loss_target[0], small["meta"], fox0_w, [shard16(k) for k in _LATE], small, ln_g, ln_b)

    def small_rows(n):
        if _SPECS[n][1] is None:
            return jnp.broadcast_to(grads[n].reshape(1, -1), (N_DEV, grads[n].size))
        return _split(n, grads[n]).reshape(N_DEV, -1)

    small_out = _pad_rows(jnp.concatenate([small_rows(n) for n in _SMALL]
                                          + [jnp.broadcast_to(loss.reshape(1, 1), (N_DEV, 1))], axis=1))
    parts = _exchange([_split_layer("fox_w_in", d_win0), small_out], True, "grads_all_to_all")
    by_key["fox_w_in", 0] = parts[0]

    def rows2d(a):
        return a.reshape(-1, a.shape[-1])

    out = {}
    for n in _BIG:
        layers = [by_key[n, i] for i in range(_SPECS[n][0][0])]
        p = layers[0] if len(layers) == 1 else jnp.stack(layers, axis=1)
        res = _adamw(p.reshape(N_DEV, -1, p.shape[-1]), rows2d(w[n]), rows2d(m[n]), rows2d(v[n]), f"adamw_{n}")
        out[n] = [r.reshape(w[n].shape) for r in res]
    zero1 = jnp.zeros((1,), F32)
    res = _adamw(parts[-1], _pack_small(w, _SMALL, zero1), _pack_small(m, _SMALL, zero1),
                 _pack_small(v, _SMALL, zero1), "adamw_small")
    small_res = [_unpack_small(r, _SMALL) for r in res]
    for n in _SMALL:
        out[n] = [s[0][n] for s in small_res]
    total_loss = small_res[0][1][0]
    return (total_loss, grad_x[None], *[out[n][0] for n in _NAMES], *[out[n][1] for n in _NAMES],
            *[out[n][2] for n in _NAMES], *[out[n][3] for n in _NAMES])
```

```python
import math

import numpy as np
import jax
import jax.numpy as jnp
from jax import lax
from jax.experimental import pallas as pl
from jax.experimental.pallas import tpu as pltpu

F32 = jnp.float32
_MXU = jnp.bfloat16

N_DEV = 8
N_META = 16
D_MODEL = 1024
ROW_TILE = 512
LANES = 128
CHUNK = 128
NEG = -1e30
V7X_VMEM_BYTES = 64 * 1024 * 1024
VMEM_LIMIT = V7X_VMEM_BYTES - 4 * 1024 * 1024

FOX_HEADS = 8
MLA_HEADS = 8
MLA_NOPE, MLA_ROPE, MLA_V = 128, 64, 128
MLA_Q_LORA, MLA_KV_LORA = 384, 256
MLA_DK = 256
RET_HEADS, RET_QK, RET_V = 4, 256, 512
ROPE_BASE = 10000.0
ALPHA = (2 * 4) ** 0.25
NORM_EPS = 1e-5

ADAM_LR, ADAM_B1, ADAM_B2, ADAM_EPS, ADAM_WD, ADAM_STEP = 0.001, 0.9, 0.999, 1e-08, 0.01, 10


def _params(sem, vmem=VMEM_LIMIT):
    return pltpu.CompilerParams(dimension_semantics=sem, vmem_limit_bytes=vmem)


def _tile(n, pref):
    if n <= pref:
        return n
    t = (pref // LANES) * LANES
    while n % t:
        t -= LANES
    return t


_MM_TILES = {"nn": (512, 4224, 2048), "nt": (512, 2048, 6144), "tn": (1024, 2048, 1536)}


def _matmul(a, b, mode, out_dtype, name):
    if mode == "nn":
        (M, K), (K2, N) = a.shape, b.shape
    elif mode == "nt":
        (M, K), (N, K2) = a.shape, b.shape
    else:
        (K, M), (K2, N) = a.shape, b.shape
    assert K == K2, (a.shape, b.shape, mode)
    tm, tn, tk = (_tile(d, p) for d, p in zip((M, N, K), _MM_TILES[mode]))
    nk = K // tk
    if mode == "nn":
        a_spec = pl.BlockSpec((tm, tk), lambda i, j, k: (i, k))
        b_spec = pl.BlockSpec((tk, tn), lambda i, j, k: (k, j))
        dn = (((1,), (0,)), ((), ()))
    elif mode == "nt":
        a_spec = pl.BlockSpec((tm, tk), lambda i, j, k: (i, k))
        b_spec = pl.BlockSpec((tn, tk), lambda i, j, k: (j, k))
        dn = (((1,), (1,)), ((), ()))
    else:
        a_spec = pl.BlockSpec((tk, tm), lambda i, j, k: (k, i))
        b_spec = pl.BlockSpec((tk, tn), lambda i, j, k: (k, j))
        dn = (((0,), (0,)), ((), ()))

    def body(a_ref, b_ref, o_ref, *acc):
        part = lax.dot_general(a_ref[...].astype(_MXU), b_ref[...].astype(_MXU), dn,
                               preferred_element_type=F32)
        if nk == 1:
            o_ref[...] = part.astype(out_dtype)
            return
        acc_ref, = acc
        k = pl.program_id(2)

        @pl.when(k == 0)
        def _():
            acc_ref[...] = part

        @pl.when(k > 0)
        def _():
            acc_ref[...] += part

        @pl.when(k == nk - 1)
        def _():
            o_ref[...] = acc_ref[...].astype(out_dtype)

    return pl.pallas_call(
        body, name=name,
        out_shape=jax.ShapeDtypeStruct((M, N), out_dtype),
        grid=(M // tm, N // tn, nk),
        in_specs=[a_spec, b_spec],
        out_specs=pl.BlockSpec((tm, tn), lambda i, j, k: (i, j)),
        scratch_shapes=[pltpu.VMEM((tm, tn), F32)] if nk > 1 else [],
        compiler_params=_params(("parallel", "parallel", "arbitrary")),
    )(a, b)


def _rows(w, off=0):
    return pl.BlockSpec((ROW_TILE, w), lambda i: (i + off, 0))


def _whole(shape):
    return pl.BlockSpec(shape, lambda i: (0,) * len(shape))


def _silu(z):
    return z * jax.nn.sigmoid(z)


def _dsilu(z):
    s = jax.nn.sigmoid(z)
    return s * (1.0 + z * (1.0 - s))


def _ln_fwd(h, y, g, b, name):
    L, D = h.shape

    def body(h_ref, y_ref, g_ref, b_ref, z_ref, o_ref):
        z = ALPHA * h_ref[...] + y_ref[...]
        mu = jnp.mean(z, axis=-1, keepdims=True)
        zc = z - mu
        var = jnp.mean(zc * zc, axis=-1, keepdims=True)
        z_ref[...] = z
        o_ref[...] = zc * lax.rsqrt(var + NORM_EPS) * g_ref[...] + b_ref[...]

    return pl.pallas_call(
        body, name=name,
        out_shape=(jax.ShapeDtypeStruct((L, D), F32), jax.ShapeDtypeStruct((L, D), F32)),
        grid=(L // ROW_TILE,),
        in_specs=[_rows(D), _rows(D), _whole((1, D)), _whole((1, D))],
        out_specs=(_rows(D), _rows(D)),
        compiler_params=_params(("parallel",)),
    )(h, y, g, b)


def _ln_bwd(zl, g, ga, gb, name):
    L, D = zl.shape
    two = gb is not None

    def body(*refs):
        if two:
            z_ref, g_ref, ga_ref, gb_ref, dz_ref, dg_ref, db_ref = refs
            gout = ALPHA * ga_ref[...] + gb_ref[...]
        else:
            z_ref, g_ref, ga_ref, dz_ref, dg_ref, db_ref = refs
            gout = ga_ref[...]
        z = z_ref[...]
        mu = jnp.mean(z, axis=-1, keepdims=True)
        zc = z - mu
        var = jnp.mean(zc * zc, axis=-1, keepdims=True)
        rstd = lax.rsqrt(var + NORM_EPS)
        xhat = zc * rstd
        dxh = gout * g_ref[...]
        m1 = jnp.mean(dxh, axis=-1, keepdims=True)
        m2 = jnp.mean(dxh * xhat, axis=-1, keepdims=True)
        dz_ref[...] = rstd * (dxh - m1 - xhat * m2)

        @pl.when(pl.program_id(0) == 0)
        def _():
            dg_ref[...] = jnp.zeros_like(dg_ref)
            db_ref[...] = jnp.zeros_like(db_ref)

        dg_ref[...] += jnp.sum(gout * xhat, axis=0, keepdims=True)
        db_ref[...] += jnp.sum(gout, axis=0, keepdims=True)

    ins = [zl, g, ga] + ([gb] if two else [])
    return pl.pallas_call(
        body, name=name,
        out_shape=(jax.ShapeDtypeStruct((L, D), F32), jax.ShapeDtypeStruct((1, D), F32),
                   jax.ShapeDtypeStruct((1, D), F32)),
        grid=(L // ROW_TILE,),
        in_specs=[_rows(D), _whole((1, D)), _rows(D)] + ([_rows(D)] if two else []),
        out_specs=(_rows(D), _whole((1, D)), _whole((1, D))),
        compiler_params=_params(("arbitrary",)),
    )(*ins)


def _loss_head(h, target, first):
    L, D = h.shape

    def body(h_ref, t_ref, dh_ref, loss_ref):
        i = pl.program_id(0)

        @pl.when(i == 0)
        def _():
            loss_ref[...] = jnp.zeros_like(loss_ref)

        @pl.when(i < first)
        def _():
            dh_ref[...] = jnp.zeros_like(dh_ref)

        @pl.when(i >= first)
        def _():
            err = h_ref[...] - t_ref[...]
            dh_ref[...] = err * (1.0 / D)
            part = jnp.sum(jnp.sum(err * err, axis=-1, keepdims=True) * (1.0 / D), axis=0, keepdims=True)
            loss_ref[...] += 0.5 * part

    return pl.pallas_call(
        body, name="loss_head",
        out_shape=(jax.ShapeDtypeStruct((L, D), F32), jax.ShapeDtypeStruct((1, 1), F32)),
        grid=(L // ROW_TILE,),
        in_specs=[_rows(D), pl.BlockSpec((ROW_TILE, D), lambda i: (jnp.maximum(i - first, 0), 0))],
        out_specs=(_rows(D), _whole((1, 1))),
        compiler_params=_params(("arbitrary",)),
    )(h, target)


def _input_grads(dzl, dmix, first, n_real):
    L, D = dzl.shape

    def body(a_ref, b_ref, gx_ref, gm_ref):
        i = pl.program_id(0)
        val = ALPHA * a_ref[...] + b_ref[...]

        @pl.when(i == first - 1)
        def _():
            gm_ref[...] = val[ROW_TILE - N_META:, :]

        @pl.when(i >= first)
        def _():
            gx_ref[...] = val

    return pl.pallas_call(
        body, name="input_grads",
        out_shape=(jax.ShapeDtypeStruct((n_real, D), F32), jax.ShapeDtypeStruct((N_META, D), F32)),
        grid=(L // ROW_TILE,),
        in_specs=[_rows(D), _rows(D)],
        out_specs=(pl.BlockSpec((ROW_TILE, D), lambda i: (jnp.maximum(i - first, 0), 0)),
                   _whole((N_META, D))),
        compiler_params=_params(("arbitrary",)),
    )(dzl, dmix)


def _gate_bwd(dy, o, zsrc, heads, name):
    L, W = o.shape
    hd = W // heads

    def body(dy_ref, o_ref, z_ref, do_ref, dl_ref, dz_ref):
        z = z_ref[...]
        dy = dy_ref[...]
        o = o_ref[...]
        do = dy * _silu(z)
        do_ref[...] = do.astype(_MXU)
        dz_ref[...] = (dy * o * _dsilu(z)).astype(_MXU)
        prod = do * o
        for h in range(heads):
            dl_ref[h] = _as_row(jnp.sum(prod[:, h * hd:(h + 1) * hd], axis=-1, keepdims=True))

    return pl.pallas_call(
        body, name=name,
        out_shape=(jax.ShapeDtypeStruct((L, W), _MXU), jax.ShapeDtypeStruct((heads, L // ROW_TILE, 1, ROW_TILE), F32),
                   jax.ShapeDtypeStruct((L, W), _MXU)),
        grid=(L // ROW_TILE,),
        in_specs=[_rows(W), _rows(W), _rows(W)],
        out_specs=(_rows(W), pl.BlockSpec((heads, None, 1, ROW_TILE), lambda i: (0, i, 0, 0)), _rows(W)),
        compiler_params=_params(("parallel",)),
    )(dy, o, zsrc)


def _split3(x):
    hi = x.astype(_MXU)
    r1 = x - hi.astype(F32)
    mid = r1.astype(_MXU)
    lo = (r1 - mid.astype(F32)).astype(_MXU)
    return hi, mid, lo


def _tri_cumsum(x, reverse):
    T = x.shape[0]
    r = lax.broadcasted_iota(jnp.int32, (T, T), 0)
    c = lax.broadcasted_iota(jnp.int32, (T, T), 1)
    tri = jnp.where((c >= r) if reverse else (c <= r), 1.0, 0.0).astype(_MXU)
    out = jnp.zeros(x.shape, F32)
    for part in _split3(x):
        out = out + jnp.dot(tri, part, preferred_element_type=F32)
    return out


def _fox_cum(zf, bias, pad, name):
    L, Wz = zf.shape
    off = Wz // LANES - 1

    def body(f_ref, b_ref, c_ref, carry_ref):
        i = pl.program_id(0)

        @pl.when(i == 0)
        def _():
            carry_ref[...] = jnp.zeros_like(carry_ref)

        x = f_ref[...] + b_ref[...]
        logf = jnp.minimum(x, 0.0) - jnp.log(1.0 + jnp.exp(-jnp.abs(x)))
        row = i * ROW_TILE + lax.broadcasted_iota(jnp.int32, logf.shape, 0)
        logf = jnp.where(row >= pad, logf, 0.0)
        cum = _tri_cumsum(logf, False) + carry_ref[...]
        c_ref[...] = cum
        carry_ref[...] = cum[ROW_TILE - 1:, :]

    return pl.pallas_call(
        body, name=name,
        out_shape=jax.ShapeDtypeStruct((L, LANES), F32),
        grid=(L // ROW_TILE,),
        in_specs=[pl.BlockSpec((ROW_TILE, LANES), lambda i: (i, off)), _whole((1, LANES))],
        out_specs=_rows(LANES),
        scratch_shapes=[pltpu.VMEM((1, LANES), F32)],
        compiler_params=_params(("arbitrary",)),
    )(zf, bias)


def _fox_cum_bwd(dcum, zf, bias, pad, name):
    L, Wz = zf.shape
    off = Wz // LANES - 1
    n = L // ROW_TILE

    def body(d_ref, f_ref, b_ref, df_ref, db_ref, carry_ref):
        i = pl.program_id(0)

        @pl.when(i == 0)
        def _():
            carry_ref[...] = jnp.zeros_like(carry_ref)
            db_ref[...] = jnp.zeros_like(db_ref)

        rc = _tri_cumsum(d_ref[...], True) + carry_ref[...]
        carry_ref[...] = rc[:1, :]
        x = f_ref[...] + b_ref[...]
        row = (n - 1 - i) * ROW_TILE + lax.broadcasted_iota(jnp.int32, x.shape, 0)
        df = jnp.where(row >= pad, rc * jax.nn.sigmoid(-x), 0.0)
        df_ref[...] = df.astype(_MXU)
        db_ref[...] += jnp.sum(df, axis=0, keepdims=True)

    return pl.pallas_call(
        body, name=name,
        out_shape=(jax.ShapeDtypeStruct((L, LANES), _MXU), jax.ShapeDtypeStruct((1, LANES), F32)),
        grid=(n,),
        in_specs=[pl.BlockSpec((ROW_TILE, LANES), lambda i: (n - 1 - i, 0)),
                  pl.BlockSpec((ROW_TILE, LANES), lambda i: (n - 1 - i, off)), _whole((1, LANES))],
        out_specs=(pl.BlockSpec((ROW_TILE, LANES), lambda i: (n - 1 - i, 0)), _whole((1, LANES))),
        scratch_shapes=[pltpu.VMEM((1, LANES), F32)],
        compiler_params=_params(("arbitrary",)),
    )(dcum, zf, bias)


LOG2E = 1.4426950408889634
BWD_UNROLL = 8
FWD_UNROLL = 8


def _bias_terms(ct, pad):
    H, L = ct.shape
    T = ROW_TILE
    n = L // T
    c2 = ct * LOG2E
    k2 = jnp.where(jnp.arange(L)[None, :] < pad, -NEG, c2)
    return dict(cq_row=c2.reshape(H, n, 1, T), ck_row=k2.reshape(H, n, 1, T), ref=c2[:, ::T].reshape(H, n, 1, 1))


def _exchange_copies(srcs, outs, send_sems, recv_sems, local_sems, scatter):
    nt = len(srcs)
    x, y, c = lax.axis_index("x"), lax.axis_index("y"), lax.axis_index("c")
    me = 4 * x + 2 * y + c
    copies = [pltpu.make_async_copy(srcs[t].at[me] if scatter else srcs[t], outs[t].at[me], local_sems.at[t])
              for t in range(nt)]
    for k in range(1, N_DEV):
        px = (x + (k >> 2)) % 2
        py = (y + ((k >> 1) & 1)) % 2
        pc = (c + (k & 1)) % 2
        peer = 4 * px + 2 * py + pc
        for t in range(nt):
            copies.append(pltpu.make_async_remote_copy(
                src_ref=srcs[t].at[peer] if scatter else srcs[t],
                dst_ref=outs[t].at[me],
                send_sem=send_sems.at[k - 1, t], recv_sem=recv_sems.at[k - 1, t],
                device_id=(px, py, pc), device_id_type=pl.DeviceIdType.MESH))
    return copies


def _exchange_shapes(tensors, scatter):
    out_shape = tuple(jax.ShapeDtypeStruct(t.shape if scatter else (N_DEV,) + t.shape, t.dtype) for t in tensors)
    nt = len(tensors)
    sems = [pltpu.SemaphoreType.DMA((N_DEV - 1, nt)), pltpu.SemaphoreType.DMA((N_DEV - 1, nt)),
            pltpu.SemaphoreType.DMA((nt,))]
    return out_shape, sems


def _side_exchange(side, srcs, outs, sems, first, last):
    if not side:
        return

    @pl.when(first)
    def _():
        for cp in _exchange_copies(srcs, outs, *sems, side[1]):
            cp.start()

    @pl.when(last)
    def _():
        for cp in _exchange_copies(srcs, outs, *sems, side[1]):
            cp.wait()


def _as_row(col):
    return jnp.transpose(jnp.broadcast_to(col, (col.shape[0], LANES)))[0:1, :]


def _as_col(row):
    return jnp.transpose(jnp.broadcast_to(row, (LANES, row.shape[1])))[:, 0:1]


def _attn_fwd(q, qoff, k, koff, v, voff, zsrc, bias, heads, dk, dv, scale, name, dead, side=None):
    L = q.shape[0]
    T = ROW_TILE
    n = L // T
    c = scale * LOG2E
    dn_qk = (((1,), (1,)), ((), ()))

    ns = len(side[0]) if side else 0

    def body(*refs):
        q_ref, k_ref, v_ref, z_ref, ck_ref, ref_ref = refs[:6]
        o_ref, lse_ref, y_ref = refs[6 + ns:9 + ns]
        s_a, s_b = refs[9 + 2 * ns:11 + 2 * ns]
        i = pl.program_id(1)
        _side_exchange(side, refs[6:6 + ns], refs[9 + ns:9 + 2 * ns], refs[11 + 2 * ns:],
                       (pl.program_id(0) == 0) & (i == 0), (pl.program_id(0) == heads - 1) & (i == n - 1))
        qb = q_ref[...]
        ref = ref_ref[...]

        def scores(j, dst):
            start = pl.multiple_of(j * T, T)
            dst[...] = lax.dot_general(qb, k_ref[pl.ds(start, T), :], dn_qk, preferred_element_type=F32)

        def update(t, vj, carry):
            m, l, acc = carry
            m_new = jnp.maximum(m, jnp.max(t, axis=1, keepdims=True))
            p = jnp.exp2(t - m_new)
            a = jnp.exp2(m - m_new)
            l = a * l + jnp.sum(p, axis=1, keepdims=True)
            acc = a * acc + jnp.dot(p.astype(_MXU), vj, preferred_element_type=F32)
            return m_new, l, acc

        def soft(j, carry, cur, diag):
            start = pl.multiple_of(j * T, T)
            t = cur[...] * c - (ck_ref[j] - ref)
            if diag:
                r = lax.broadcasted_iota(jnp.int32, (T, T), 0)
                cc = lax.broadcasted_iota(jnp.int32, (T, T), 1)
                t = jnp.where(cc <= r, t, NEG)
            return update(t, v_ref[pl.ds(start, T), :], carry)

        def head_tile(carry):
            t = lax.dot_general(qb, k_ref[dead:T, :], dn_qk, preferred_element_type=F32) * c
            return update(t - (ck_ref[0][:, dead:] - ref), v_ref[dead:T, :], carry)

        def step(j, carry, cur, nxt):
            scores(j + 1, nxt)
            return soft(j, carry, cur, False)

        first = jnp.where(i > 0, 1, 0) if dead else 0

        def run_of(j0, steps, carry):
            for u in range(0, steps, 2):
                carry = step(j0 + u + 1, step(j0 + u, carry, s_a, s_b), s_b, s_a)
            return carry

        scores(first, s_a)
        carry = (jnp.full((T, 1), -jnp.inf, F32), jnp.zeros((T, 1), F32), jnp.zeros((T, dv), F32))
        if dead:
            carry = lax.cond(i > 0, head_tile, lambda cy: cy, carry)
        whole = i - first
        big = whole // FWD_UNROLL
        carry = lax.fori_loop(0, big, lambda g, cy: run_of(first + FWD_UNROLL * g, FWD_UNROLL, cy), carry)
        done = first + FWD_UNROLL * big
        carry = lax.fori_loop(0, (i - done) // 2, lambda g, cy: run_of(done + 2 * g, 2, cy), carry)
        odd = ((i - done) % 2) == 1
        carry = lax.cond(odd, lambda cy: step(i - 1, cy, s_a, s_b), lambda cy: cy, carry)
        m, l, acc = lax.cond(odd, lambda cy: soft(i, cy, s_b, True), lambda cy: soft(i, cy, s_a, True), carry)
        o = acc / l
        o_ref[...] = o
        y_ref[...] = (o * _silu(z_ref[...])).astype(_MXU)
        lse_ref[...] = _as_row(m + jnp.log2(l))

    hbm = pl.BlockSpec(memory_space=pl.ANY)
    side_shapes, side_sems = _exchange_shapes(*side) if side else ((), [])
    o, lse, y, *side_out = pl.pallas_call(
        body, name=name,
        out_shape=(jax.ShapeDtypeStruct((L, heads * dv), F32), jax.ShapeDtypeStruct((heads, n, 1, T), F32),
                   jax.ShapeDtypeStruct((L, heads * dv), _MXU)) + side_shapes,
        grid=(heads, n),
        in_specs=[pl.BlockSpec((T, dk), lambda h, i: (i, qoff + h)),
                  pl.BlockSpec((L, dk), lambda h, i: (0, koff + h)),
                  pl.BlockSpec((L, dv), lambda h, i: (0, voff + h)),
                  pl.BlockSpec((T, dv), lambda h, i: (i, h)),
                  pl.BlockSpec((None, n, 1, T), lambda h, i: (h, 0, 0, 0)),
                  pl.BlockSpec((None, None, 1, 1), lambda h, i: (h, i, 0, 0))] + [hbm] * ns,
        out_specs=(pl.BlockSpec((T, dv), lambda h, i: (i, h)),
                   pl.BlockSpec((None, None, 1, T), lambda h, i: (h, i, 0, 0)),
                   pl.BlockSpec((T, dv), lambda h, i: (i, h))) + (hbm,) * ns,
        scratch_shapes=[pltpu.VMEM((T, T), F32), pltpu.VMEM((T, T), F32)] + side_sems,
        compiler_params=_params(("arbitrary", "arbitrary")),
    )(q, k, v, zsrc, bias["ck_row"], bias["ref"], *(side[0] if side else ()))
    return o, lse + (bias["cq_row"] - bias["ref"]), y, side_out


def _attn_bwd(q, qoff, k, koff, v, voff, do, bias, lse_row, delta_row, heads, dk, dv, scale, name, dead,
              side=None, bias_grads=True):
    L = q.shape[0]
    T = ROW_TILE
    n = L // T
    c = scale * LOG2E
    dn_nt = (((1,), (1,)), ((), ()))
    dn_tn = (((0,), (0,)), ((), ()))
    ns = len(side[0]) if side else 0

    def body(*refs):
        q_ref, k_ref, v_ref, do_ref, cq_ref, ck_ref, ref_ref, lse_ref, dl_ref = refs[:9]
        dq_hbm, dk_ref, dv_ref, dck_ref, dcq_ref = refs[9 + ns:14 + ns]
        dq_acc, sem, st_a, dp_a, st_b, dp_b = refs[14 + 2 * ns:20 + 2 * ns]
        h = pl.program_id(0)
        j = pl.program_id(1)
        _side_exchange(side, refs[9:9 + ns], refs[14 + ns:14 + 2 * ns], refs[20 + 2 * ns:],
                       (h == 0) & (j == 0), (h == heads - 1) & (j == n - 1))

        @pl.when(j == 0)
        def _():
            dq_acc[...] = jnp.zeros_like(dq_acc)
            dcq_ref[...] = jnp.zeros_like(dcq_ref)

        ref = ref_ref[...]

        def run(k0, R):
            kb = k_ref[k0:k0 + R, :]
            vb = v_ref[k0:k0 + R, :]
            bcol = _as_col(ck_ref[j][:, k0:k0 + R] - ref)

            def front(i, st_dst, dp_dst):
                start = pl.multiple_of(jnp.minimum(i, n - 1) * T, T)
                st_dst[0:R, :] = lax.dot_general(kb, q_ref[pl.ds(start, T), :], dn_nt, preferred_element_type=F32)
                dp_dst[0:R, :] = lax.dot_general(vb, do_ref[pl.ds(start, T), :], dn_nt, preferred_element_type=F32)

            def back(i, carry, st_cur, dp_cur, diag):
                dk_a, dv_a, dck_a = carry
                start = pl.multiple_of(i * T, T)
                qi = q_ref[pl.ds(start, T), :]
                doi = do_ref[pl.ds(start, T), :]
                arow = (cq_ref[i] - ref) - lse_ref[i]
                st = st_cur[0:R, :] * c + arow - bcol
                if diag:
                    r = lax.broadcasted_iota(jnp.int32, (R, T), 0) + k0
                    cc = lax.broadcasted_iota(jnp.int32, (R, T), 1)
                    st = jnp.where(r <= cc, st, NEG)
                pt = jnp.exp2(st)
                dv_a = dv_a + jnp.dot(pt.astype(_MXU), doi, preferred_element_type=F32)
                dst = pt * (dp_cur[0:R, :] - dl_ref[i])
                if bias_grads:
                    dck_a = dck_a - jnp.sum(dst, axis=1, keepdims=True)
                    dcq_ref[i] += jnp.sum(dst, axis=0, keepdims=True)
                dsb = (dst * scale).astype(_MXU)
                dk_a = dk_a + jnp.dot(dsb, qi, preferred_element_type=F32)
                dq_acc[pl.ds(start, T), :] += lax.dot_general(dsb, kb, dn_tn, preferred_element_type=F32)
                return dk_a, dv_a, dck_a

            buf_a, buf_b = (st_a, dp_a), (st_b, dp_b)

            def step(i, carry, cur, nxt, diag=False):
                front(i + 1, *nxt)
                return back(i, carry, *cur, diag)

            def run_of(i0, steps, carry):
                for u in range(0, steps, 2):
                    carry = step(i0 + u + 1, step(i0 + u, carry, buf_b, buf_a), buf_a, buf_b)
                return carry

            front(j, *buf_a)
            init = (jnp.zeros((R, dk), F32), jnp.zeros((R, dv), F32), jnp.zeros((R, 1), F32))
            carry = step(j, init, buf_a, buf_b, True)
            big = (n - 1 - j) // BWD_UNROLL
            carry = lax.fori_loop(0, big, lambda g, cy: run_of(j + 1 + BWD_UNROLL * g, BWD_UNROLL, cy), carry)
            done = j + 1 + BWD_UNROLL * big
            carry = lax.fori_loop(0, (n - done) // 2, lambda g, cy: run_of(done + 2 * g, 2, cy), carry)
            dk_a, dv_a, dck_a = lax.cond(((n - done) % 2) == 1, lambda cy: step(n - 1, cy, buf_b, buf_a),
                                         lambda cy: cy, carry)
            if k0:
                dk_ref[0:k0, :] = jnp.zeros((k0, dk), F32)
                dv_ref[0:k0, :] = jnp.zeros((k0, dv), F32)
                dck_ref[:, 0:k0] = jnp.zeros((1, k0), F32)
            dk_ref[k0:k0 + R, :] = dk_a
            dv_ref[k0:k0 + R, :] = dv_a
            dck_ref[:, k0:k0 + R] = _as_row(dck_a)

        if dead:
            pl.when(j == 0)(lambda: run(dead, T - dead))
            pl.when(j > 0)(lambda: run(0, T))
        else:
            run(0, T)

        @pl.when(j == n - 1)
        def _():
            cp = pltpu.make_async_copy(dq_acc, dq_hbm.at[:, pl.ds(pl.multiple_of(h * dk, dk), dk)], sem)
            cp.start()
            cp.wait()

    hbm = pl.BlockSpec(memory_space=pl.ANY)
    side_shapes, side_sems = _exchange_shapes(*side) if side else ((), [])
    dq, dk_, dv_, dck, dcq, *side_out = pl.pallas_call(
        body, name=name,
        out_shape=(jax.ShapeDtypeStruct((L, heads * dk), F32), jax.ShapeDtypeStruct((L, heads * dk), F32),
                   jax.ShapeDtypeStruct((L, heads * dv), F32), jax.ShapeDtypeStruct((heads, n, 1, T), F32),
                   jax.ShapeDtypeStruct((heads, n, 1, T), F32)) + side_shapes,
        grid=(heads, n),
        in_specs=[pl.BlockSpec((L, dk), lambda h, j: (0, qoff + h)),
                  pl.BlockSpec((T, dk), lambda h, j: (j, koff + h)),
                  pl.BlockSpec((T, dv), lambda h, j: (j, voff + h)),
                  pl.BlockSpec((L, dv), lambda h, j: (0, h)),
                  pl.BlockSpec((None, n, 1, T), lambda h, j: (h, 0, 0, 0)),
                  pl.BlockSpec((None, n, 1, T), lambda h, j: (h, 0, 0, 0)),
                  pl.BlockSpec((None, None, 1, 1), lambda h, j: (h, j, 0, 0)),
                  pl.BlockSpec((None, n, 1, T), lambda h, j: (h, 0, 0, 0)),
                  pl.BlockSpec((None, n, 1, T), lambda h, j: (h, 0, 0, 0))] + [hbm] * ns,
        out_specs=(hbm,
                   pl.BlockSpec((T, dk), lambda h, j: (j, h)),
                   pl.BlockSpec((T, dv), lambda h, j: (j, h)),
                   pl.BlockSpec((None, None, 1, T), lambda h, j: (h, j, 0, 0)),
                   pl.BlockSpec((None, n, 1, T), lambda h, j: (h, 0, 0, 0))) + (hbm,) * ns,
        scratch_shapes=[pltpu.VMEM((L, dk), F32), pltpu.SemaphoreType.DMA] + [pltpu.VMEM((T, T), F32)] * 4
        + side_sems,
        compiler_params=_params(("arbitrary", "arbitrary")),
    )(q, k, v, do, bias["cq_row"], bias["ck_row"], bias["ref"], lse_row, delta_row, *(side[0] if side else ()))
    return dq, dk_, dv_, dck, dcq, side_out


_CQ0, _CKV0, _KR0 = D_MODEL, D_MODEL + MLA_Q_LORA, D_MODEL + MLA_Q_LORA + MLA_KV_LORA
_MLA_PROJ = _KR0 + LANES


def _rms(x, g):
    ms = jnp.mean(x * x, axis=-1, keepdims=True)
    return x * lax.rsqrt(ms + NORM_EPS) * g


def _rms_bwd(x, g, dy):
    ms = jnp.mean(x * x, axis=-1, keepdims=True)
    r = lax.rsqrt(ms + NORM_EPS)
    xh = x * r
    dxh = dy * g
    dx = r * (dxh - xh * jnp.mean(dxh * xh, axis=-1, keepdims=True))
    return dx, jnp.sum(dy * xh, axis=0, keepdims=True)


def _mla_norm_fwd(proj, gq, gkv):
    L = proj.shape[0]

    def body(p_ref, gq_ref, gkv_ref, cq_ref, ckv_ref):
        cq_ref[...] = _rms(p_ref[:, _CQ0:_CKV0], gq_ref[...]).astype(_MXU)
        ckv_ref[...] = _rms(p_ref[:, _CKV0:_KR0], gkv_ref[...]).astype(_MXU)

    return pl.pallas_call(
        body, name="mla_norm_fwd",
        out_shape=(jax.ShapeDtypeStruct((L, MLA_Q_LORA), _MXU), jax.ShapeDtypeStruct((L, MLA_KV_LORA), _MXU)),
        grid=(L // ROW_TILE,),
        in_specs=[_rows(_MLA_PROJ), _whole((1, MLA_Q_LORA)), _whole((1, MLA_KV_LORA))],
        out_specs=(_rows(MLA_Q_LORA), _rows(MLA_KV_LORA)),
        compiler_params=_params(("parallel",)),
    )(proj, gq, gkv)


def _mla_norm_bwd(proj, gq, gkv, dcqn, dckvn):
    L = proj.shape[0]

    def body(p_ref, gq_ref, gkv_ref, dq_ref, dkv_ref, dcq_ref, dckv_ref, dgq_ref, dgkv_ref):
        @pl.when(pl.program_id(0) == 0)
        def _():
            dgq_ref[...] = jnp.zeros_like(dgq_ref)
            dgkv_ref[...] = jnp.zeros_like(dgkv_ref)

        dx, dg = _rms_bwd(p_ref[:, _CQ0:_CKV0], gq_ref[...], dq_ref[...])
        dcq_ref[...] = dx.astype(_MXU)
        dgq_ref[...] += dg
        dx, dg = _rms_bwd(p_ref[:, _CKV0:_KR0], gkv_ref[...], dkv_ref[...])
        dckv_ref[...] = dx.astype(_MXU)
        dgkv_ref[...] += dg

    return pl.pallas_call(
        body, name="mla_norm_bwd",
        out_shape=(jax.ShapeDtypeStruct((L, MLA_Q_LORA), _MXU), jax.ShapeDtypeStruct((L, MLA_KV_LORA), _MXU),
                   jax.ShapeDtypeStruct((1, MLA_Q_LORA), F32), jax.ShapeDtypeStruct((1, MLA_KV_LORA), F32)),
        grid=(L // ROW_TILE,),
        in_specs=[_rows(_MLA_PROJ), _whole((1, MLA_Q_LORA)), _whole((1, MLA_KV_LORA)),
                  _rows(MLA_Q_LORA), _rows(MLA_KV_LORA)],
        out_specs=(_rows(MLA_Q_LORA), _rows(MLA_KV_LORA), _whole((1, MLA_Q_LORA)), _whole((1, MLA_KV_LORA))),
        compiler_params=_params(("arbitrary",)),
    )(proj, gq, gkv, dcqn, dckvn)


def _rot_tile(t, cos, sa, sb):
    half = MLA_ROPE // 2
    return t * cos + pltpu.roll(t, LANES - half, 1) * sa + pltpu.roll(t, half, 1) * sb


def _mla_rope_fwd(qf, kv, proj, cos, sa, sb):
    L = qf.shape[0]
    H = MLA_HEADS

    def body(q_ref, kv_ref, p_ref, c_ref, sa_ref, sb_ref, qo_ref, ko_ref):
        cos, sa, sb = c_ref[...], sa_ref[...], sb_ref[...]
        kr = _rot_tile(p_ref[...], cos, sa, sb).astype(_MXU)
        for h in range(H):
            b = h * MLA_DK
            qo_ref[:, b:b + LANES] = q_ref[:, b:b + LANES].astype(_MXU)
            qo_ref[:, b + LANES:b + 2 * LANES] = _rot_tile(q_ref[:, b + LANES:b + 2 * LANES], cos, sa, sb).astype(_MXU)
            ko_ref[:, b:b + LANES] = kv_ref[:, h * LANES:(h + 1) * LANES]
            ko_ref[:, b + LANES:b + 2 * LANES] = kr

    W = H * MLA_DK
    return pl.pallas_call(
        body, name="mla_rope_fwd",
        out_shape=(jax.ShapeDtypeStruct((L, W), _MXU), jax.ShapeDtypeStruct((L, W), _MXU)),
        grid=(L // ROW_TILE,),
        in_specs=[_rows(W), _rows(kv.shape[1]), pl.BlockSpec((ROW_TILE, LANES), lambda i: (i, _KR0 // LANES)),
                  _rows(LANES), _rows(LANES), _rows(LANES)],
        out_specs=(_rows(W), _rows(W)),
        compiler_params=_params(("parallel",)),
    )(qf, kv, proj, cos, sa, sb)


def _mla_rope_bwd(dq_full, dk_full, cos, sa, sb):
    L = dq_full.shape[0]
    H = MLA_HEADS
    W = H * MLA_DK

    def body(dq_ref, dk_ref, c_ref, sa_ref, sb_ref, dqf_ref, dkn_ref, dkr_ref):
        cos, sa, sb = c_ref[...], sa_ref[...], sb_ref[...]
        lane = lax.broadcasted_iota(jnp.int32, (ROW_TILE, LANES), 1)
        live = lane < MLA_ROPE
        krs = jnp.zeros((ROW_TILE, LANES), F32)
        for h in range(H):
            b = h * MLA_DK
            dqf_ref[:, b:b + LANES] = dq_ref[:, b:b + LANES].astype(_MXU)
            dqr = _rot_tile(dq_ref[:, b + LANES:b + 2 * LANES], cos, sa, sb)
            dqf_ref[:, b + LANES:b + 2 * LANES] = jnp.where(live, dqr, 0.0).astype(_MXU)
            dkn_ref[:, h * LANES:(h + 1) * LANES] = dk_ref[:, b:b + LANES].astype(_MXU)
            krs = krs + dk_ref[:, b + LANES:b + 2 * LANES]
        dkr_ref[...] = jnp.where(live, _rot_tile(krs, cos, sa, sb), 0.0).astype(_MXU)

    return pl.pallas_call(
        body, name="mla_rope_bwd",
        out_shape=(jax.ShapeDtypeStruct((L, W), _MXU), jax.ShapeDtypeStruct((L, H * LANES), _MXU),
                   jax.ShapeDtypeStruct((L, LANES), _MXU)),
        grid=(L // ROW_TILE,),
        in_specs=[_rows(W), _rows(W), _rows(LANES), _rows(LANES), _rows(LANES)],
        out_specs=(_rows(W), _rows(H * LANES), _rows(LANES)),
        compiler_params=_params(("parallel",)),
    )(dq_full, dk_full, cos, sa, sb)


_RET_QKW = RET_HEADS * RET_QK


def _ret_rope_fwd(qkz, cos, sin, pad):
    L = qkz.shape[0]
    hh = RET_QK // 2
    kscale = RET_QK ** -0.5

    def body(p_ref, c_ref, s_ref, q_ref, k_ref):
        cos, sin = c_ref[...], s_ref[...]
        row = pl.program_id(0) * ROW_TILE + lax.broadcasted_iota(jnp.int32, (ROW_TILE, hh), 0)
        keep = row >= pad
        for h in range(RET_HEADS):
            b = h * RET_QK
            t1, t2 = p_ref[:, b:b + hh], p_ref[:, b + hh:b + 2 * hh]
            q_ref[:, b:b + hh] = (t1 * cos - t2 * sin).astype(_MXU)
            q_ref[:, b + hh:b + 2 * hh] = (t2 * cos + t1 * sin).astype(_MXU)
            t1, t2 = p_ref[:, _RET_QKW + b:_RET_QKW + b + hh], p_ref[:, _RET_QKW + b + hh:_RET_QKW + b + 2 * hh]
            k_ref[:, b:b + hh] = jnp.where(keep, (t1 * cos - t2 * sin) * kscale, 0.0)
            k_ref[:, b + hh:b + 2 * hh] = jnp.where(keep, (t2 * cos + t1 * sin) * kscale, 0.0)

    return pl.pallas_call(
        body, name="ret_rope_fwd",
        out_shape=(jax.ShapeDtypeStruct((L, _RET_QKW), _MXU), jax.ShapeDtypeStruct((L, _RET_QKW), F32)),
        grid=(L // ROW_TILE,),
        in_specs=[_rows(2 * _RET_QKW), _rows(hh), _rows(hh)],
        out_specs=(_rows(_RET_QKW), _rows(_RET_QKW)),
        compiler_params=_params(("parallel",)),
    )(qkz, cos, sin)


def _ret_rope_bwd(dqr, dkr, cos, sin, pad):
    L = dqr.shape[0]
    hh = RET_QK // 2
    kscale = RET_QK ** -0.5

    def body(dq_ref, dk_ref, c_ref, s_ref, oq_ref, ok_ref):
        cos, sin = c_ref[...], s_ref[...]
        row = pl.program_id(0) * ROW_TILE + lax.broadcasted_iota(jnp.int32, (ROW_TILE, hh), 0)
        keep = row >= pad
        for h in range(RET_HEADS):
            b = h * RET_QK
            d1, d2 = dq_ref[:, b:b + hh], dq_ref[:, b + hh:b + 2 * hh]
            oq_ref[:, b:b + hh] = (d1 * cos + d2 * sin).astype(_MXU)
            oq_ref[:, b + hh:b + 2 * hh] = (d2 * cos - d1 * sin).astype(_MXU)
            d1, d2 = dk_ref[:, b:b + hh], dk_ref[:, b + hh:b + 2 * hh]
            ok_ref[:, b:b + hh] = jnp.where(keep, (d1 * cos + d2 * sin) * kscale, 0.0).astype(_MXU)
            ok_ref[:, b + hh:b + 2 * hh] = jnp.where(keep, (d2 * cos - d1 * sin) * kscale, 0.0).astype(_MXU)

    return pl.pallas_call(
        body, name="ret_rope_bwd",
        out_shape=(jax.ShapeDtypeStruct((L, _RET_QKW), _MXU), jax.ShapeDtypeStruct((L, _RET_QKW), _MXU)),
        grid=(L // ROW_TILE,),
        in_specs=[_rows(_RET_QKW), _rows(_RET_QKW), _rows(hh), _rows(hh)],
        out_specs=(_rows(_RET_QKW), _rows(_RET_QKW)),
        compiler_params=_params(("parallel",)),
    )(dqr, dkr, cos, sin)


def _ret_rows(L):
    return next(r * ROW_TILE for r in (3, 2, 1) if L % (r * ROW_TILE) == 0)


def _ret_decays():
    lg = np.log1p(-np.exp2(-5.0 - np.arange(RET_HEADS, dtype=np.float32))).astype(np.float32)
    i = np.arange(CHUNK, dtype=np.float32)
    rel = i[:, None] - i[None, :]
    dm = np.where(rel[None] >= 0, np.exp(rel[None] * lg[:, None, None]), 0.0).astype(np.float32)
    qd = np.exp((i[None, :] + 1.0) * lg[:, None]).astype(np.float32)[:, :, None]
    kd = np.exp((CHUNK - 1.0 - i)[None, :] * lg[:, None]).astype(np.float32)[:, :, None]
    cd = np.broadcast_to(np.exp(CHUNK * lg).astype(np.float32)[:, None, None], (RET_HEADS, 1, RET_V))
    return jnp.asarray(dm), jnp.asarray(qd), jnp.asarray(kd), jnp.asarray(np.ascontiguousarray(cd))


def _ret_fwd(qr, kr, v):
    L = qr.shape[0]
    nc = L // CHUNK
    RB = _ret_rows(L)
    G = RB // CHUNK
    dm, qd, kd, cd = _ret_decays()
    dn_nt = (((1,), (1,)), ((), ()))
    dn_tn = (((0,), (0,)), ((), ()))

    def body(q_ref, k_ref, v_ref, dm_ref, qd_ref, kd_ref, cd_ref, o_ref, st_ref, s_ref):
        @pl.when(pl.program_id(1) == 0)
        def _():
            s_ref[...] = jnp.zeros_like(s_ref)

        state = s_ref[...]
        for g in range(G):
            rows = slice(g * CHUNK, (g + 1) * CHUNK)
            q, k, vv = q_ref[rows, :], k_ref[rows, :], v_ref[rows, :]
            sb = state.astype(_MXU)
            st_ref[g] = sb
            s = lax.dot_general(q, k.astype(_MXU), dn_nt, preferred_element_type=F32) * dm_ref[...]
            intra = jnp.dot(s.astype(_MXU), vv, preferred_element_type=F32)
            cross = jnp.dot(q, sb, preferred_element_type=F32) * qd_ref[...]
            o_ref[rows, :] = intra + cross
            kdm = (k * kd_ref[...]).astype(_MXU)
            state = cd_ref[...] * state + lax.dot_general(kdm, vv, dn_tn, preferred_element_type=F32)
        s_ref[...] = state

    return pl.pallas_call(
        body, name="ret_fwd",
        out_shape=(jax.ShapeDtypeStruct((L, RET_HEADS * RET_V), F32),
                   jax.ShapeDtypeStruct((RET_HEADS, nc, RET_QK, RET_V), _MXU)),
        grid=(RET_HEADS, nc // G),
        in_specs=[pl.BlockSpec((RB, RET_QK), lambda h, c: (c, h)),
                  pl.BlockSpec((RB, RET_QK), lambda h, c: (c, h)),
                  pl.BlockSpec((RB, RET_V), lambda h, c: (c, h)),
                  pl.BlockSpec((None, CHUNK, CHUNK), lambda h, c: (h, 0, 0)),
                  pl.BlockSpec((None, CHUNK, 1), lambda h, c: (h, 0, 0)),
                  pl.BlockSpec((None, CHUNK, 1), lambda h, c: (h, 0, 0)),
                  pl.BlockSpec((None, 1, RET_V), lambda h, c: (h, 0, 0))],
        out_specs=(pl.BlockSpec((RB, RET_V), lambda h, c: (c, h)),
                   pl.BlockSpec((None, G, RET_QK, RET_V), lambda h, c: (h, c, 0, 0))),
        scratch_shapes=[pltpu.VMEM((RET_QK, RET_V), F32)],
        compiler_params=_params(("parallel", "arbitrary")),
    )(qr, kr, v, dm, qd, kd, cd)


def _ret_bwd(qr, kr, v, do, states):
    L = qr.shape[0]
    nc = L // CHUNK
    RB = _ret_rows(L)
    G = RB // CHUNK
    nt = nc // G
    dm, qd, kd, cd = _ret_decays()
    dn_nt = (((1,), (1,)), ((), ()))
    dn_tn = (((0,), (0,)), ((), ()))

    def body(q_ref, k_ref, v_ref, do_ref, st_ref, dm_ref, qd_ref, kd_ref, cd_ref,
             dq_ref, dk_ref, dv_ref, ds_ref):
        @pl.when(pl.program_id(1) == 0)
        def _():
            ds_ref[...] = jnp.zeros_like(ds_ref)

        dstate = ds_ref[...]
        dmat = dm_ref[...]
        for g in reversed(range(G)):
            rows = slice(g * CHUNK, (g + 1) * CHUNK)
            q, k, vv, do = q_ref[rows, :], k_ref[rows, :], v_ref[rows, :], do_ref[rows, :]
            kb = k.astype(_MXU)
            dob = do.astype(_MXU)
            doq = (do * qd_ref[...]).astype(_MXU)
            dsb = dstate.astype(_MXU)
            s = (lax.dot_general(q, kb, dn_nt, preferred_element_type=F32) * dmat).astype(_MXU)
            draw = (lax.dot_general(dob, vv, dn_nt, preferred_element_type=F32) * dmat).astype(_MXU)
            kdm = (k * kd_ref[...]).astype(_MXU)
            dv = (lax.dot_general(s, dob, dn_tn, preferred_element_type=F32)
                  + jnp.dot(kdm, dsb, preferred_element_type=F32))
            dq = (jnp.dot(draw, kb, preferred_element_type=F32)
                  + lax.dot_general(doq, st_ref[g], dn_nt, preferred_element_type=F32))
            dk = (lax.dot_general(draw, q, dn_tn, preferred_element_type=F32)
                  + lax.dot_general(vv, dsb, dn_nt, preferred_element_type=F32) * kd_ref[...])
            dq_ref[rows, :] = dq
            dk_ref[rows, :] = dk
            dv_ref[rows, :] = dv.astype(_MXU)
            dstate = cd_ref[...] * dstate + lax.dot_general(q, doq, dn_tn, preferred_element_type=F32)
        ds_ref[...] = dstate

    rev = lambda h, c: (nt - 1 - c, h)
    return pl.pallas_call(
        body, name="ret_bwd",
        out_shape=(jax.ShapeDtypeStruct((L, _RET_QKW), F32), jax.ShapeDtypeStruct((L, _RET_QKW), F32),
                   jax.ShapeDtypeStruct((L, RET_HEADS * RET_V), _MXU)),
        grid=(RET_HEADS, nt),
        in_specs=[pl.BlockSpec((RB, RET_QK), rev),
                  pl.BlockSpec((RB, RET_QK), rev),
                  pl.BlockSpec((RB, RET_V), rev),
                  pl.BlockSpec((RB, RET_V), rev),
                  pl.BlockSpec((None, G, RET_QK, RET_V), lambda h, c: (h, nt - 1 - c, 0, 0)),
                  pl.BlockSpec((None, CHUNK, CHUNK), lambda h, c: (h, 0, 0)),
                  pl.BlockSpec((None, CHUNK, 1), lambda h, c: (h, 0, 0)),
                  pl.BlockSpec((None, CHUNK, 1), lambda h, c: (h, 0, 0)),
                  pl.BlockSpec((None, 1, RET_V), lambda h, c: (h, 0, 0))],
        out_specs=(pl.BlockSpec((RB, RET_QK), rev), pl.BlockSpec((RB, RET_QK), rev),
                   pl.BlockSpec((RB, RET_V), rev)),
        scratch_shapes=[pltpu.VMEM((RET_QK, RET_V), F32)],
        compiler_params=_params(("parallel", "arbitrary")),
    )(qr, kr, v, do, states, dm, qd, kd, cd)


def _gn(o, g):
    mu = jnp.mean(o, axis=-1, keepdims=True)
    oc = o - mu
    var = jnp.mean(oc * oc, axis=-1, keepdims=True)
    return oc * lax.rsqrt(var + NORM_EPS) * g


def _ret_gate_fwd(o, qkz, gn_g):
    L, W = o.shape
    zblk = qkz.shape[1] // W - 1

    def body(o_ref, z_ref, g_ref, y_ref):
        for h in range(RET_HEADS):
            sl = slice(h * RET_V, (h + 1) * RET_V)
            y_ref[:, sl] = (_gn(o_ref[:, sl], g_ref[:, sl]) * _silu(z_ref[:, sl])).astype(_MXU)

    return pl.pallas_call(
        body, name="ret_gate_fwd",
        out_shape=jax.ShapeDtypeStruct((L, W), _MXU),
        grid=(L // ROW_TILE,),
        in_specs=[_rows(W), pl.BlockSpec((ROW_TILE, W), lambda i: (i, zblk)), _whole((1, W))],
        out_specs=_rows(W),
        compiler_params=_params(("parallel",)),
    )(o, qkz, gn_g)


def _ret_gate_bwd(dy, o, qkz, gn_g):
    L, W = o.shape
    zblk = qkz.shape[1] // W - 1

    def body(dy_ref, o_ref, z_ref, g_ref, do_ref, dz_ref, dg_ref):
        @pl.when(pl.program_id(0) == 0)
        def _():
            dg_ref[...] = jnp.zeros_like(dg_ref)

        for h in range(RET_HEADS):
            sl = slice(h * RET_V, (h + 1) * RET_V)
            z = z_ref[:, sl]
            o = o_ref[:, sl]
            g = g_ref[:, sl]
            dy = dy_ref[:, sl]
            mu = jnp.mean(o, axis=-1, keepdims=True)
            oc = o - mu
            var = jnp.mean(oc * oc, axis=-1, keepdims=True)
            r = lax.rsqrt(var + NORM_EPS)
            xh = oc * r
            dn = dy * _silu(z)
            dz_ref[:, sl] = (dy * (xh * g) * _dsilu(z)).astype(_MXU)
            dg_ref[:, sl] += jnp.sum(dn * xh, axis=0, keepdims=True)
            dxh = dn * g
            do_ref[:, sl] = r * (dxh - jnp.mean(dxh, axis=-1, keepdims=True)
                                 - xh * jnp.mean(dxh * xh, axis=-1, keepdims=True))

    return pl.pallas_call(
        body, name="ret_gate_bwd",
        out_shape=(jax.ShapeDtypeStruct((L, W), F32), jax.ShapeDtypeStruct((L, W), _MXU),
                   jax.ShapeDtypeStruct((1, W), F32)),
        grid=(L // ROW_TILE,),
        in_specs=[_rows(W), _rows(W), pl.BlockSpec((ROW_TILE, W), lambda i: (i, zblk)), _whole((1, W))],
        out_specs=(_rows(W), _rows(W), _whole((1, W))),
        compiler_params=_params(("arbitrary",)),
    )(dy, o, qkz, gn_g)


def _dead_rows(pad):
    return pad // LANES * LANES


def _heads_t(a):
    return a[:, :FOX_HEADS].T


def _fox_fwd(h, w, lnw, pad, tag, side=None):
    w_in, b_f, w_out = w
    L = h.shape[0]
    W = D_MODEL
    w_qkv = w_in[:, :3 * W]
    w_zf = jnp.concatenate([w_in[:, 3 * W:4 * W], jnp.pad(w_in[:, 4 * W:], ((0, 0), (0, LANES - FOX_HEADS)))], axis=1)
    bias = jnp.pad(b_f, (0, LANES - FOX_HEADS))[None, :]
    qkv = _matmul(h, w_qkv, "nn", _MXU, f"{tag}_qkv")
    zf = _matmul(h, w_zf, "nn", F32, f"{tag}_zf")
    cum = _fox_cum(zf, bias, pad, f"{tag}_cum")
    ab = _bias_terms(_heads_t(cum), pad)
    nb = W // LANES
    o, lse, y, side_out = _attn_fwd(qkv, 0, qkv, nb, qkv, 2 * nb, zf, ab, FOX_HEADS, LANES, LANES, LANES ** -0.5,
                                    f"{tag}_attn_fwd", _dead_rows(pad), side)
    mo = _matmul(y, w_out, "nn", F32, f"{tag}_out")
    zl, hn = _ln_fwd(h, mo, lnw[0], lnw[1], f"{tag}_ln_fwd")
    return hn, dict(h=h, qkv=qkv, zf=zf, ab=ab, o=o, lse=lse, y=y, zl=zl,
                    w_all=jnp.concatenate([w_qkv, w_zf], axis=1), w_out=w_out, bias=bias), side_out


def _fox_bwd(s, dzl, pad, tag, side=None, send_wout=False):
    W = D_MODEL
    nb = W // LANES
    dy = _matmul(dzl, s["w_out"], "nt", F32, f"{tag}_dy")
    d_wout = _matmul(s["y"], dzl, "tn", F32, f"{tag}_dwout")
    if send_wout:
        side = (list(side[0]) + [_split_layer("fox_w_out", d_wout)], side[1])
    do, delta, dzg = _gate_bwd(dy, s["o"], s["zf"], FOX_HEADS, f"{tag}_gate_bwd")
    qkv = s["qkv"]
    dq, dk, dv, dck, dcq, side_out = _attn_bwd(qkv, 0, qkv, nb, qkv, 2 * nb, do, s["ab"], s["lse"], delta,
                                               FOX_HEADS, LANES, LANES, LANES ** -0.5, f"{tag}_attn_bwd",
                                               _dead_rows(pad), side)
    dcum = jnp.pad((dck + dcq).reshape(FOX_HEADS, -1).T, ((0, 0), (0, LANES - FOX_HEADS)))
    df, dbias = _fox_cum_bwd(dcum, s["zf"], s["bias"], pad, f"{tag}_cum_bwd")
    dproj = jnp.concatenate([dq.astype(_MXU), dk.astype(_MXU), dv.astype(_MXU), dzg, df], axis=1)
    d_wall = _matmul(s["h"], dproj, "tn", F32, f"{tag}_dwin")
    dh = _matmul(dproj, s["w_all"], "nt", F32, f"{tag}_dh")
    d_win = jnp.concatenate([d_wall[:, :4 * W], d_wall[:, 4 * W:4 * W + FOX_HEADS]], axis=1)
    return dh, (d_win, dbias[0, :FOX_HEADS], d_wout), side_out


def _mla_tables(L, pad):
    pos = (jnp.arange(L) - pad).astype(F32)
    inv = ROPE_BASE ** (-jnp.arange(0, MLA_ROPE, 2, dtype=F32) / MLA_ROPE)
    ang = pos[:, None] * inv[None, :]
    c, s = jnp.cos(ang), jnp.sin(ang)
    z32, z64 = jnp.zeros_like(s), jnp.zeros((L, LANES - MLA_ROPE), F32)
    cos = jnp.concatenate([c, c, jnp.ones_like(z64)], axis=1)
    sa = jnp.concatenate([-s, z32, z64], axis=1)
    sb = jnp.concatenate([z32, s, z64], axis=1)
    return cos, sa, sb


def _mla_weights(w_in, w_uq, w_ukv):
    a, b = MLA_Q_LORA, MLA_Q_LORA + MLA_KV_LORA
    w_zc = jnp.concatenate([w_in[:, b + MLA_ROPE:], w_in[:, :b + MLA_ROPE],
                            jnp.zeros((D_MODEL, LANES - MLA_ROPE), w_in.dtype)], axis=1)
    uq = w_uq.reshape(MLA_Q_LORA, MLA_HEADS, MLA_NOPE + MLA_ROPE)
    uq = jnp.pad(uq, ((0, 0), (0, 0), (0, MLA_DK - MLA_NOPE - MLA_ROPE))).reshape(MLA_Q_LORA, MLA_HEADS * MLA_DK)
    ukv = w_ukv.reshape(MLA_KV_LORA, MLA_HEADS, MLA_NOPE + MLA_V)
    ukv = jnp.concatenate([ukv[:, :, :MLA_NOPE].reshape(MLA_KV_LORA, -1), ukv[:, :, MLA_NOPE:].reshape(MLA_KV_LORA, -1)], axis=1)
    return w_zc, uq, ukv


def _mla_fwd(h, w, lnw, pad):
    w_in, gq, gkv, w_uq, w_ukv, w_out = w
    L = h.shape[0]
    w_zc, uq, ukv = _mla_weights(w_in, w_uq, w_ukv)
    cos, sa, sb = _mla_tables(L, pad)
    proj = _matmul(h, w_zc, "nn", F32, "mla_proj")
    cqn, ckvn = _mla_norm_fwd(proj, gq[None, :], gkv[None, :])
    qf = _matmul(cqn, uq, "nn", F32, "mla_uq")
    kv = _matmul(ckvn, ukv, "nn", _MXU, "mla_ukv")
    q_full, k_full = _mla_rope_fwd(qf, kv, proj, cos, sa, sb)
    ab = _bias_terms(jnp.zeros((MLA_HEADS, L), F32), pad)
    nb = MLA_HEADS * MLA_NOPE // LANES
    o, lse, y, _ = _attn_fwd(q_full, 0, k_full, 0, kv, nb, proj, ab, MLA_HEADS, MLA_DK, MLA_V,
                             (MLA_NOPE + MLA_ROPE) ** -0.5, "mla_attn_fwd", _dead_rows(pad))
    mo = _matmul(y, w_out, "nn", F32, "mla_out")
    zl, hn = _ln_fwd(h, mo, lnw[0], lnw[1], "mla_ln_fwd")
    return hn, dict(h=h, proj=proj, cqn=cqn, ckvn=ckvn, kv=kv, q_full=q_full, k_full=k_full, ab=ab,
                    dead=_dead_rows(pad),
                    o=o, lse=lse, y=y, zl=zl, w_zc=w_zc, uq=uq, ukv=ukv, w_out=w_out, gq=gq, gkv=gkv,
                    tabs=(cos, sa, sb))


def _mla_bwd(s, dzl):
    H = MLA_HEADS
    dy = _matmul(dzl, s["w_out"], "nt", F32, "mla_dy")
    d_wout = _matmul(s["y"], dzl, "tn", F32, "mla_dwout")
    do, delta, dzg = _gate_bwd(dy, s["o"], s["proj"], H, "mla_gate_bwd")
    nb = H * MLA_NOPE // LANES
    dq_full, dk_full, dv, _, _, _ = _attn_bwd(s["q_full"], 0, s["k_full"], 0, s["kv"], nb, do, s["ab"],
                                              s["lse"], delta,
                                              H, MLA_DK, MLA_V, (MLA_NOPE + MLA_ROPE) ** -0.5, "mla_attn_bwd",
                                              s["dead"], bias_grads=False)
    cos, sa, sb = s["tabs"]
    dqf, dkn, dkr = _mla_rope_bwd(dq_full, dk_full, cos, -sa, -sb)
    d_uq = _matmul(s["cqn"], dqf, "tn", F32, "mla_duq")
    dcqn = _matmul(dqf, s["uq"], "nt", F32, "mla_dcqn")
    dkv = jnp.concatenate([dkn, dv.astype(_MXU)], axis=1)
    d_ukv = _matmul(s["ckvn"], dkv, "tn", F32, "mla_dukv")
    dckvn = _matmul(dkv, s["ukv"], "nt", F32, "mla_dckvn")
    dcq, dckv, dgq, dgkv = _mla_norm_bwd(s["proj"], s["gq"][None, :], s["gkv"][None, :], dcqn, dckvn)
    dproj = jnp.concatenate([dzg, dcq, dckv, dkr], axis=1)
    d_wzc = _matmul(s["h"], dproj, "tn", F32, "mla_dwin")
    dh = _matmul(dproj, s["w_zc"], "nt", F32, "mla_dh")
    W = D_MODEL
    d_win = jnp.concatenate([d_wzc[:, W:W + MLA_Q_LORA + MLA_KV_LORA + MLA_ROPE], d_wzc[:, :W]], axis=1)
    d_wuq = d_uq.reshape(MLA_Q_LORA, H, MLA_DK)[:, :, :MLA_NOPE + MLA_ROPE].reshape(MLA_Q_LORA, -1)
    hk = H * MLA_NOPE
    d_wukv = jnp.concatenate([d_ukv[:, :hk].reshape(MLA_KV_LORA, H, MLA_NOPE),
                              d_ukv[:, hk:].reshape(MLA_KV_LORA, H, MLA_V)], axis=2).reshape(MLA_KV_LORA, -1)
    return dh, (d_win, dgq[0], dgkv[0], d_wuq, d_wukv, d_wout)


def _ret_tables(L, pad):
    pos = (jnp.arange(L) - pad).astype(F32)
    inv = 1.0 / (ROPE_BASE ** jnp.linspace(0.0, 1.0, RET_QK // 2, dtype=F32))
    ang = pos[:, None] * inv[None, :]
    return jnp.cos(ang), jnp.sin(ang)


def _ret_layer_fwd(h, w, lnw, pad):
    w_in, gn_g, w_out = w
    L = h.shape[0]
    q1 = 2 * _RET_QKW
    v1 = q1 + RET_HEADS * RET_V
    w_qkz = jnp.concatenate([w_in[:, :q1], w_in[:, v1:]], axis=1)
    cos, sin = _ret_tables(L, pad)
    qkz = _matmul(h, w_qkz, "nn", F32, "ret_qkz")
    v = _matmul(h, w_in[:, q1:v1], "nn", _MXU, "ret_v")
    qr, kr = _ret_rope_fwd(qkz, cos, sin, pad)
    o, states = _ret_fwd(qr, kr, v)
    y = _ret_gate_fwd(o, qkz, gn_g[None, :])
    mo = _matmul(y, w_out, "nn", F32, "ret_out")
    zl, hn = _ln_fwd(h, mo, lnw[0], lnw[1], "ret_ln_fwd")
    return hn, dict(h=h, qkz=qkz, v=v, qr=qr, kr=kr, o=o, states=states, y=y, zl=zl, w_in=w_in, w_out=w_out,
                    gn_g=gn_g, tabs=(cos, sin))


def _ret_layer_bwd(s, dzl, pad):
    dy = _matmul(dzl, s["w_out"], "nt", F32, "ret_dy")
    d_wout = _matmul(s["y"], dzl, "tn", F32, "ret_dwout")
    do, dzg, dgn = _ret_gate_bwd(dy, s["o"], s["qkz"], s["gn_g"][None, :])
    dqr, dkr, dv = _ret_bwd(s["qr"], s["kr"], s["v"], do, s["states"])
    cos, sin = s["tabs"]
    dq, dk = _ret_rope_bwd(dqr, dkr, cos, sin, pad)
    dproj = jnp.concatenate([dq, dk, dv, dzg], axis=1)
    d_win = _matmul(s["h"], dproj, "tn", F32, "ret_dwin")
    dh = _matmul(dproj, s["w_in"], "nt", F32, "ret_dh")
    return dh, (d_win, dgn[0], d_wout)


def _local_step(x, target, meta, fox0_w, late_shards, small, ln_g, ln_b):
    n_real = x.shape[0]
    first = -(-N_META // ROW_TILE)
    pad = first * ROW_TILE - N_META
    h0 = jnp.concatenate([jnp.zeros((pad, D_MODEL), F32), meta, x], axis=0)
    lnw = [(ln_g[i][None, :], ln_b[i][None, :]) for i in range(4)]
    b_f = small["fox_b_f"]

    h1, s0, gathered = _fox_fwd(h0, (fox0_w[0], b_f[0], fox0_w[1]), lnw[0], pad, "fox0", (late_shards, False))
    w = {key: _merge_layer(key[0], g).astype(_MXU) for key, g in zip(_LATE, gathered)}
    h2, s1 = _mla_fwd(h1, (w["mla_w_in", 0], small["mla_q_norm"][0], small["mla_kv_norm"][0], w["mla_w_uq", 0],
                           w["mla_w_ukv", 0], w["mla_w_out", 0]), lnw[1], pad)
    h3, s2 = _ret_layer_fwd(h2, (w["ret_w_in", 0], small["ret_gn_g"][0], w["ret_w_out", 0]), lnw[2], pad)
    h4, s3, _ = _fox_fwd(h3, (w["fox_w_in", 1], b_f[1], w["fox_w_out", 1]), lnw[3], pad, "fox1")

    dh4, loss = _loss_head(h4, target, first)
    dz3, dg3, db3 = _ln_bwd(s3["zl"], lnw[3][0], dh4, None, "ln3_bwd")
    dm3, gf1, _ = _fox_bwd(s3, dz3, pad, "fox1")
    dz2, dg2, db2 = _ln_bwd(s2["zl"], lnw[2][0], dz3, dm3, "ln2_bwd")
    dm2, gr = _ret_layer_bwd(s2, dz2, pad)
    dz1, dg1, db1 = _ln_bwd(s1["zl"], lnw[1][0], dz2, dm2, "ln1_bwd")
    dm1, gm = _mla_bwd(s1, dz1)
    dz0, dg0, db0 = _ln_bwd(s0["zl"], lnw[0][0], dz1, dm1, "ln0_bwd")
    late = {("fox_w_in", 1): gf1[0], ("fox_w_out", 1): gf1[2], ("mla_w_in", 0): gm[0], ("mla_w_uq", 0): gm[3],
            ("mla_w_ukv", 0): gm[4], ("mla_w_out", 0): gm[5], ("ret_w_in", 0): gr[0], ("ret_w_out", 0): gr[2]}
    dm0, gf0, sent = _fox_bwd(s0, dz0, pad, "fox0", ([_split_layer(k[0], late[k]) for k in _LATE], True), True)
    parts = dict(zip(_LATE + [("fox_w_out", 0)], sent))
    grad_x, grad_meta = _input_grads(dz0, dm0, first, n_real)

    small_grads = dict(
        meta=grad_meta, fox_b_f=jnp.stack([gf0[1], gf1[1]]), mla_q_norm=gm[1][None], mla_kv_norm=gm[2][None],
        ret_gn_g=gr[1][None], ln_g=jnp.concatenate([dg0, dg1, dg2, dg3], axis=0),
        ln_b=jnp.concatenate([db0, db1, db2, db3], axis=0))
    return loss, grad_x, gf0[0], parts, small_grads


def _exchange(tensors, scatter, name):
    nt = len(tensors)

    def body(*refs):
        copies = _exchange_copies(refs[:nt], refs[nt:2 * nt], *refs[2 * nt:], scatter)
        for cp in copies:
            cp.start()
        for cp in copies:
            cp.wait()

    hbm = pl.BlockSpec(memory_space=pl.ANY)
    out_shape, sems = _exchange_shapes(tensors, scatter)
    return pl.pallas_call(
        body, name=name, out_shape=out_shape, in_specs=[hbm] * nt, out_specs=(hbm,) * nt, scratch_shapes=sems,
    )(*tensors)


def _adamw(parts, w, m, v, name):
    R, C = w.shape
    tr = next(t for t in range(min(R, 256), 0, -1) if R % t == 0 and (t % 8 == 0 or t == R))

    def body(p_ref, w_ref, m_ref, v_ref, g_ref, d_ref, mo_ref, vo_ref):
        g = p_ref[0]
        for d in range(1, N_DEV):
            g = g + p_ref[d]
        mn = ADAM_B1 * m_ref[...] + (1.0 - ADAM_B1) * g
        vn = ADAM_B2 * v_ref[...] + (1.0 - ADAM_B2) * (g * g)
        m_hat = mn / (1.0 - ADAM_B1 ** ADAM_STEP)
        v_hat = vn / (1.0 - ADAM_B2 ** ADAM_STEP)
        g_ref[...] = g
        d_ref[...] = -ADAM_LR * (m_hat / (jnp.sqrt(v_hat) + ADAM_EPS) + ADAM_WD * w_ref[...])
        mo_ref[...] = mn
        vo_ref[...] = vn

    blk = pl.BlockSpec((tr, C), lambda i: (i, 0))
    sd = jax.ShapeDtypeStruct((R, C), F32)
    return pl.pallas_call(
        body, name=name,
        out_shape=(sd, sd, sd, sd),
        grid=(R // tr,),
        in_specs=[pl.BlockSpec((N_DEV, tr, C), lambda i: (0, i, 0)), blk, blk, blk],
        out_specs=(blk, blk, blk, blk),
        compiler_params=_params(("parallel",)),
    )(parts, w, m, v)


_SPECS = dict(
    meta=((16, 1024), 1),
    fox_w_in=((2, 1024, 4104), 2), fox_b_f=((2, 8), None), fox_w_out=((2, 1024, 1024), 1),
    mla_w_in=((1, 1024, 1728), 2), mla_q_norm=((1, 384), None), mla_kv_norm=((1, 256), None),
    mla_w_uq=((1, 384, 1536), 2), mla_w_ukv=((1, 256, 2048), 2), mla_w_out=((1, 1024, 1024), 1),
    ret_w_in=((1, 1024, 6144), 2), ret_gn_g=((1, 2048), 1), ret_w_out=((1, 2048, 1024), 1),
    ln_g=((4, 1024), None), ln_b=((4, 1024), None),
)
_NAMES = list(_SPECS)
_BIG = ["fox_w_in", "fox_w_out", "mla_w_in", "mla_w_uq", "mla_w_ukv", "mla_w_out", "ret_w_in", "ret_w_out"]
_SMALL = [n for n in _NAMES if n not in _BIG]
_SMALL_GATHERED = ["meta", "ret_gn_g"]
_PACK_ROWS = 8


def _shard_shape(name):
    shape, ax = _SPECS[name]
    if ax is None:
        return shape
    return tuple(s // N_DEV if i == ax else s for i, s in enumerate(shape))


_EARLY = [("fox_w_in", 0), ("fox_w_out", 0)]
_LATE = [("fox_w_in", 1), ("fox_w_out", 1), ("mla_w_in", 0), ("mla_w_uq", 0), ("mla_w_ukv", 0), ("mla_w_out", 0),
         ("ret_w_in", 0), ("ret_w_out", 0)]


def _split_at(full, ax):
    shape = full.shape
    parts = full.reshape(shape[:ax] + (N_DEV, shape[ax] // N_DEV) + shape[ax + 1:])
    return jnp.moveaxis(parts, ax, 0)


def _merge_at(stacked, ax):
    full = jnp.moveaxis(stacked, 0, ax)
    return full.reshape(full.shape[:ax] + (-1,) + full.shape[ax + 2:])


def _split(name, full):
    return _split_at(full, _SPECS[name][1])


def _merge(name, stacked):
    return _merge_at(stacked, _SPECS[name][1])


def _split_layer(name, full):
    return _split_at(full, _SPECS[name][1] - 1)


def _merge_layer(name, stacked):
    return _merge_at(stacked, _SPECS[name][1] - 1)


def _pad_rows(flat):
    n = flat.shape[-1]
    unit = _PACK_ROWS * LANES
    total = -(-n // unit) * unit
    flat = jnp.pad(flat, [(0, 0)] * (flat.ndim - 1) + [(0, total - n)])
    return flat.reshape(flat.shape[:-1] + (total // LANES, LANES))


def _pack_small(tree, names, tail):
    return _pad_rows(jnp.concatenate([tree[n].reshape(-1) for n in names] + [tail]))


def _unpack_small(buf, names):
    flat, out, at = buf.reshape(-1), {}, 0
    for n in names:
        shp = _shard_shape(n)
        out[n] = flat[at:at + math.prod(shp)].reshape(shp)
        at += math.prod(shp)
    return out, flat[at:]


def kernel(x, meta, fox_w_in, fox_b_f, fox_w_out, mla_w_in, mla_q_norm, mla_kv_norm, mla_w_uq, mla_w_ukv, mla_w_out, ret_w_in, ret_gn_g, ret_w_out, ln_g, ln_b, loss_target, m_meta, m_fox_w_in, m_fox_b_f, m_fox_w_out, m_mla_w_in, m_mla_q_norm, m_mla_kv_norm, m_mla_w_uq, m_mla_w_ukv, m_mla_w_out, m_ret_w_in, m_ret_gn_g, m_ret_w_out, m_ln_g, m_ln_b, v_meta, v_fox_w_in, v_fox_b_f, v_fox_w_out, v_mla_w_in, v_mla_q_norm, v_mla_kv_norm, v_mla_w_uq, v_mla_w_ukv, v_mla_w_out, v_ret_w_in, v_ret_gn_g, v_ret_w_out, v_ln_g, v_ln_b):
    w = dict(meta=meta, fox_w_in=fox_w_in, fox_b_f=fox_b_f, fox_w_out=fox_w_out, mla_w_in=mla_w_in,
             mla_q_norm=mla_q_norm, mla_kv_norm=mla_kv_norm, mla_w_uq=mla_w_uq, mla_w_ukv=mla_w_ukv,
             mla_w_out=mla_w_out, ret_w_in=ret_w_in, ret_gn_g=ret_gn_g, ret_w_out=ret_w_out, ln_g=ln_g, ln_b=ln_b)
    m = dict(meta=m_meta, fox_w_in=m_fox_w_in, fox_b_f=m_fox_b_f, fox_w_out=m_fox_w_out, mla_w_in=m_mla_w_in,
             mla_q_norm=m_mla_q_norm, mla_kv_norm=m_mla_kv_norm, mla_w_uq=m_mla_w_uq, mla_w_ukv=m_mla_w_ukv,
             mla_w_out=m_mla_w_out, ret_w_in=m_ret_w_in, ret_gn_g=m_ret_gn_g, ret_w_out=m_ret_w_out, ln_g=m_ln_g, ln_b=m_ln_b)
    v = dict(meta=v_meta, fox_w_in=v_fox_w_in, fox_b_f=v_fox_b_f, fox_w_out=v_fox_w_out, mla_w_in=v_mla_w_in,
             mla_q_norm=v_mla_q_norm, mla_kv_norm=v_mla_kv_norm, mla_w_uq=v_mla_w_uq, mla_w_ukv=v_mla_w_ukv,
             mla_w_out=v_mla_w_out, ret_w_in=v_ret_w_in, ret_gn_g=v_ret_gn_g, ret_w_out=v_ret_w_out, ln_g=v_ln_g, ln_b=v_ln_b)

    def shard16(key):
        return w[key[0]][key[1]].astype(jnp.bfloat16)

    none = jnp.zeros((0,), F32)
    gathered = _exchange([shard16(k) for k in _EARLY] + [_pack_small(w, _SMALL_GATHERED, none)],
                         False, "weights_all_gather")
    fox0_w = [_merge_layer(k[0], g).astype(_MXU) for k, g in zip(_EARLY, gathered)]
    small = {n: w[n] for n in _SMALL}
    pieces = [_unpack_small(gathered[-1][d], _SMALL_GATHERED)[0] for d in range(N_DEV)]
    for n in _SMALL_GATHERED:
        small[n] = _merge(n, jnp.stack([p[n] for p in pieces]))

    loss, grad_x, d_win0, by_key, grads = _local_step(
        x[0], loss_target[0], small["meta"], fox0_w, [shard16(k) for k in _LATE], small, ln_g, ln_b)

    def small_rows(n):
        if _SPECS[n][1] is None:
            return jnp.broadcast_to(grads[n].reshape(1, -1), (N_DEV, grads[n].size))
        return _split(n, grads[n]).reshape(N_DEV, -1)

    small_out = _pad_rows(jnp.concatenate([small_rows(n) for n in _SMALL]
                                          + [jnp.broadcast_to(loss.reshape(1, 1), (N_DEV, 1))], axis=1))
    parts = _exchange([_split_layer("fox_w_in", d_win0), small_out], True, "grads_all_to_all")
    by_key["fox_w_in", 0] = parts[0]

    def rows2d(a):
        return a.reshape(-1, a.shape[-1])

    out = {}
    for n in _BIG:
        layers = [by_key[n, i] for i in range(_SPECS[n][0][0])]
        p = layers[0] if len(layers) == 1 else jnp.stack(layers, axis=1)
        res = _adamw(p.reshape(N_DEV, -1, p.shape[-1]), rows2d(w[n]), rows2d(m[n]), rows2d(v[n]), f"adamw_{n}")
        out[n] = [r.reshape(w[n].shape) for r in res]
    zero1 = jnp.zeros((1,), F32)
    res = _adamw(parts[-1], _pack_small(w, _SMALL, zero1), _pack_small(m, _SMALL, zero1),
                 _pack_small(v, _SMALL, zero1), "adamw_small")
    small_res = [_unpack_small(r, _SMALL) for r in res]
    for n in _SMALL:
        out[n] = [s[0][n] for s in small_res]
    total_loss = small_res[0][1][0]
    return (total_loss, grad_x[None], *[out[n][0] for n in _NAMES], *[out[n][1] for n in _NAMES],
            *[out[n][2] for n in _NAMES], *[out[n][3] for n in _NAMES])
```

```python
import math

import numpy as np
import jax
import jax.numpy as jnp
from jax import lax
from jax.experimental import pallas as pl
from jax.experimental.pallas import tpu as pltpu

F32 = jnp.float32
_MXU = jnp.bfloat16

N_DEV = 8
N_META = 16
D_MODEL = 1024
ROW_TILE = 512
LANES = 128
CHUNK = 128
NEG = -1e30
V7X_VMEM_BYTES = 64 * 1024 * 1024
VMEM_LIMIT = V7X_VMEM_BYTES - 4 * 1024 * 1024

FOX_HEADS = 8
MLA_HEADS = 8
MLA_NOPE, MLA_ROPE, MLA_V = 128, 64, 128
MLA_Q_LORA, MLA_KV_LORA = 384, 256
MLA_DK = 256
RET_HEADS, RET_QK, RET_V = 4, 256, 512
ROPE_BASE = 10000.0
ALPHA = (2 * 4) ** 0.25
NORM_EPS = 1e-5

ADAM_LR, ADAM_B1, ADAM_B2, ADAM_EPS, ADAM_WD, ADAM_STEP = 0.001, 0.9, 0.999, 1e-08, 0.01, 10


def _params(sem, vmem=VMEM_LIMIT):
    return pltpu.CompilerParams(dimension_semantics=sem, vmem_limit_bytes=vmem)


def _tile(n, pref):
    if n <= pref:
        return n
    t = (pref // LANES) * LANES
    while n % t:
        t -= LANES
    return t


_MM_TILES = {"nn": (512, 4224, 2048), "nt": (512, 2048, 6144), "tn": (1024, 2048, 1536)}


def _matmul(a, b, mode, out_dtype, name, side=None):
    if mode == "nn":
        (M, K), (K2, N) = a.shape, b.shape
    elif mode == "nt":
        (M, K), (N, K2) = a.shape, b.shape
    else:
        (K, M), (K2, N) = a.shape, b.shape
    assert K == K2, (a.shape, b.shape, mode)
    tm, tn, tk = (_tile(d, p) for d, p in zip((M, N, K), _MM_TILES[mode]))
    nk = K // tk
    if mode == "nn":
        a_spec = pl.BlockSpec((tm, tk), lambda i, j, k: (i, k))
        b_spec = pl.BlockSpec((tk, tn), lambda i, j, k: (k, j))
        dn = (((1,), (0,)), ((), ()))
    elif mode == "nt":
        a_spec = pl.BlockSpec((tm, tk), lambda i, j, k: (i, k))
        b_spec = pl.BlockSpec((tn, tk), lambda i, j, k: (j, k))
        dn = (((1,), (1,)), ((), ()))
    else:
        a_spec = pl.BlockSpec((tk, tm), lambda i, j, k: (k, i))
        b_spec = pl.BlockSpec((tk, tn), lambda i, j, k: (k, j))
        dn = (((0,), (0,)), ((), ()))

    ns = len(side[0]) if side else 0
    grid = (M // tm, N // tn, nk)
    n_acc = 1 if nk > 1 else 0

    def body(*refs):
        a_ref, b_ref = refs[:2]
        o_ref = refs[2 + ns]
        k = pl.program_id(2)
        if side:
            i, j = pl.program_id(0), pl.program_id(1)
            _side_exchange(side, refs[2:2 + ns], refs[3 + ns:3 + 2 * ns], refs[3 + 2 * ns + n_acc:],
                           (i == 0) & (j == 0) & (k == 0),
                           (i == grid[0] - 1) & (j == grid[1] - 1) & (k == nk - 1))
        part = lax.dot_general(a_ref[...].astype(_MXU), b_ref[...].astype(_MXU), dn,
                               preferred_element_type=F32)
        if nk == 1:
            o_ref[...] = part.astype(out_dtype)
            return
        acc_ref = refs[3 + 2 * ns]

        @pl.when(k == 0)
        def _():
            acc_ref[...] = part

        @pl.when(k > 0)
        def _():
            acc_ref[...] += part

        @pl.when(k == nk - 1)
        def _():
            o_ref[...] = acc_ref[...].astype(out_dtype)

    hbm = pl.BlockSpec(memory_space=pl.ANY)
    side_shapes, side_sems = _exchange_shapes(*side) if side else ((), [])
    out, *side_out = pl.pallas_call(
        body, name=name,
        out_shape=(jax.ShapeDtypeStruct((M, N), out_dtype),) + side_shapes,
        grid=grid,
        in_specs=[a_spec, b_spec] + [hbm] * ns,
        out_specs=(pl.BlockSpec((tm, tn), lambda i, j, k: (i, j)),) + (hbm,) * ns,
        scratch_shapes=[pltpu.VMEM((tm, tn), F32)] * n_acc + side_sems,
        compiler_params=_params(("arbitrary",) * 3 if side else ("parallel", "parallel", "arbitrary")),
    )(a, b, *(side[0] if side else ()))
    return (out, side_out) if side else out


def _rows(w, off=0):
    return pl.BlockSpec((ROW_TILE, w), lambda i: (i + off, 0))


def _whole(shape):
    return pl.BlockSpec(shape, lambda i: (0,) * len(shape))


def _silu(z):
    return z * jax.nn.sigmoid(z)


def _dsilu(z):
    s = jax.nn.sigmoid(z)
    return s * (1.0 + z * (1.0 - s))


def _ln_fwd(h, y, g, b, name):
    L, D = h.shape

    def body(h_ref, y_ref, g_ref, b_ref, z_ref, o_ref):
        z = ALPHA * h_ref[...] + y_ref[...]
        mu = jnp.mean(z, axis=-1, keepdims=True)
        zc = z - mu
        var = jnp.mean(zc * zc, axis=-1, keepdims=True)
        z_ref[...] = z
        o_ref[...] = zc * lax.rsqrt(var + NORM_EPS) * g_ref[...] + b_ref[...]

    return pl.pallas_call(
        body, name=name,
        out_shape=(jax.ShapeDtypeStruct((L, D), F32), jax.ShapeDtypeStruct((L, D), F32)),
        grid=(L // ROW_TILE,),
        in_specs=[_rows(D), _rows(D), _whole((1, D)), _whole((1, D))],
        out_specs=(_rows(D), _rows(D)),
        compiler_params=_params(("parallel",)),
    )(h, y, g, b)


def _ln_bwd(zl, g, ga, gb, name):
    L, D = zl.shape
    two = gb is not None

    def body(*refs):
        if two:
            z_ref, g_ref, ga_ref, gb_ref, dz_ref, dg_ref, db_ref = refs
            gout = ALPHA * ga_ref[...] + gb_ref[...]
        else:
            z_ref, g_ref, ga_ref, dz_ref, dg_ref, db_ref = refs
            gout = ga_ref[...]
        z = z_ref[...]
        mu = jnp.mean(z, axis=-1, keepdims=True)
        zc = z - mu
        var = jnp.mean(zc * zc, axis=-1, keepdims=True)
        rstd = lax.rsqrt(var + NORM_EPS)
        xhat = zc * rstd
        dxh = gout * g_ref[...]
        m1 = jnp.mean(dxh, axis=-1, keepdims=True)
        m2 = jnp.mean(dxh * xhat, axis=-1, keepdims=True)
        dz_ref[...] = rstd * (dxh - m1 - xhat * m2)

        @pl.when(pl.program_id(0) == 0)
        def _():
            dg_ref[...] = jnp.zeros_like(dg_ref)
            db_ref[...] = jnp.zeros_like(db_ref)

        dg_ref[...] += jnp.sum(gout * xhat, axis=0, keepdims=True)
        db_ref[...] += jnp.sum(gout, axis=0, keepdims=True)

    ins = [zl, g, ga] + ([gb] if two else [])
    return pl.pallas_call(
        body, name=name,
        out_shape=(jax.ShapeDtypeStruct((L, D), F32), jax.ShapeDtypeStruct((1, D), F32),
                   jax.ShapeDtypeStruct((1, D), F32)),
        grid=(L // ROW_TILE,),
        in_specs=[_rows(D), _whole((1, D)), _rows(D)] + ([_rows(D)] if two else []),
        out_specs=(_rows(D), _whole((1, D)), _whole((1, D))),
        compiler_params=_params(("arbitrary",)),
    )(*ins)


def _loss_head(h, target, first):
    L, D = h.shape

    def body(h_ref, t_ref, dh_ref, loss_ref):
        i = pl.program_id(0)

        @pl.when(i == 0)
        def _():
            loss_ref[...] = jnp.zeros_like(loss_ref)

        @pl.when(i < first)
        def _():
            dh_ref[...] = jnp.zeros_like(dh_ref)

        @pl.when(i >= first)
        def _():
            err = h_ref[...] - t_ref[...]
            dh_ref[...] = err * (1.0 / D)
            part = jnp.sum(jnp.sum(err * err, axis=-1, keepdims=True) * (1.0 / D), axis=0, keepdims=True)
            loss_ref[...] += 0.5 * part

    return pl.pallas_call(
        body, name="loss_head",
        out_shape=(jax.ShapeDtypeStruct((L, D), F32), jax.ShapeDtypeStruct((1, 1), F32)),
        grid=(L // ROW_TILE,),
        in_specs=[_rows(D), pl.BlockSpec((ROW_TILE, D), lambda i: (jnp.maximum(i - first, 0), 0))],
        out_specs=(_rows(D), _whole((1, 1))),
        compiler_params=_params(("arbitrary",)),
    )(h, target)


def _input_grads(dzl, dmix, first, n_real):
    L, D = dzl.shape

    def body(a_ref, b_ref, gx_ref, gm_ref):
        i = pl.program_id(0)
        val = ALPHA * a_ref[...] + b_ref[...]

        @pl.when(i == first - 1)
        def _():
            gm_ref[...] = val[ROW_TILE - N_META:, :]

        @pl.when(i >= first)
        def _():
            gx_ref[...] = val

    return pl.pallas_call(
        body, name="input_grads",
        out_shape=(jax.ShapeDtypeStruct((n_real, D), F32), jax.ShapeDtypeStruct((N_META, D), F32)),
        grid=(L // ROW_TILE,),
        in_specs=[_rows(D), _rows(D)],
        out_specs=(pl.BlockSpec((ROW_TILE, D), lambda i: (jnp.maximum(i - first, 0), 0)),
                   _whole((N_META, D))),
        compiler_params=_params(("arbitrary",)),
    )(dzl, dmix)


def _gate_bwd(dy, o, zsrc, heads, name):
    L, W = o.shape
    hd = W // heads

    def body(dy_ref, o_ref, z_ref, do_ref, dl_ref, dz_ref):
        z = z_ref[...]
        dy = dy_ref[...]
        o = o_ref[...]
        do = dy * _silu(z)
        do_ref[...] = do.astype(_MXU)
        dz_ref[...] = (dy * o * _dsilu(z)).astype(_MXU)
        prod = do * o
        for h in range(heads):
            dl_ref[h] = _as_row(jnp.sum(prod[:, h * hd:(h + 1) * hd], axis=-1, keepdims=True))

    return pl.pallas_call(
        body, name=name,
        out_shape=(jax.ShapeDtypeStruct((L, W), _MXU), jax.ShapeDtypeStruct((heads, L // ROW_TILE, 1, ROW_TILE), F32),
                   jax.ShapeDtypeStruct((L, W), _MXU)),
        grid=(L // ROW_TILE,),
        in_specs=[_rows(W), _rows(W), _rows(W)],
        out_specs=(_rows(W), pl.BlockSpec((heads, None, 1, ROW_TILE), lambda i: (0, i, 0, 0)), _rows(W)),
        compiler_params=_params(("parallel",)),
    )(dy, o, zsrc)


def _split3(x):
    hi = x.astype(_MXU)
    r1 = x - hi.astype(F32)
    mid = r1.astype(_MXU)
    lo = (r1 - mid.astype(F32)).astype(_MXU)
    return hi, mid, lo


def _tri_cumsum(x, reverse):
    T = x.shape[0]
    r = lax.broadcasted_iota(jnp.int32, (T, T), 0)
    c = lax.broadcasted_iota(jnp.int32, (T, T), 1)
    tri = jnp.where((c >= r) if reverse else (c <= r), 1.0, 0.0).astype(_MXU)
    out = jnp.zeros(x.shape, F32)
    for part in _split3(x):
        out = out + jnp.dot(tri, part, preferred_element_type=F32)
    return out


def _fox_cum(zf, bias, pad, name):
    L, Wz = zf.shape
    off = Wz // LANES - 1

    def body(f_ref, b_ref, c_ref, carry_ref):
        i = pl.program_id(0)

        @pl.when(i == 0)
        def _():
            carry_ref[...] = jnp.zeros_like(carry_ref)

        x = f_ref[...] + b_ref[...]
        logf = jnp.minimum(x, 0.0) - jnp.log(1.0 + jnp.exp(-jnp.abs(x)))
        row = i * ROW_TILE + lax.broadcasted_iota(jnp.int32, logf.shape, 0)
        logf = jnp.where(row >= pad, logf, 0.0)
        cum = _tri_cumsum(logf, False) + carry_ref[...]
        c_ref[...] = cum
        carry_ref[...] = cum[ROW_TILE - 1:, :]

    return pl.pallas_call(
        body, name=name,
        out_shape=jax.ShapeDtypeStruct((L, LANES), F32),
        grid=(L // ROW_TILE,),
        in_specs=[pl.BlockSpec((ROW_TILE, LANES), lambda i: (i, off)), _whole((1, LANES))],
        out_specs=_rows(LANES),
        scratch_shapes=[pltpu.VMEM((1, LANES), F32)],
        compiler_params=_params(("arbitrary",)),
    )(zf, bias)


def _fox_cum_bwd(dcum, zf, bias, pad, name):
    L, Wz = zf.shape
    off = Wz // LANES - 1
    n = L // ROW_TILE

    def body(d_ref, f_ref, b_ref, df_ref, db_ref, carry_ref):
        i = pl.program_id(0)

        @pl.when(i == 0)
        def _():
            carry_ref[...] = jnp.zeros_like(carry_ref)
            db_ref[...] = jnp.zeros_like(db_ref)

        rc = _tri_cumsum(d_ref[...], True) + carry_ref[...]
        carry_ref[...] = rc[:1, :]
        x = f_ref[...] + b_ref[...]
        row = (n - 1 - i) * ROW_TILE + lax.broadcasted_iota(jnp.int32, x.shape, 0)
        df = jnp.where(row >= pad, rc * jax.nn.sigmoid(-x), 0.0)
        df_ref[...] = df.astype(_MXU)
        db_ref[...] += jnp.sum(df, axis=0, keepdims=True)

    return pl.pallas_call(
        body, name=name,
        out_shape=(jax.ShapeDtypeStruct((L, LANES), _MXU), jax.ShapeDtypeStruct((1, LANES), F32)),
        grid=(n,),
        in_specs=[pl.BlockSpec((ROW_TILE, LANES), lambda i: (n - 1 - i, 0)),
                  pl.BlockSpec((ROW_TILE, LANES), lambda i: (n - 1 - i, off)), _whole((1, LANES))],
        out_specs=(pl.BlockSpec((ROW_TILE, LANES), lambda i: (n - 1 - i, 0)), _whole((1, LANES))),
        scratch_shapes=[pltpu.VMEM((1, LANES), F32)],
        compiler_params=_params(("arbitrary",)),
    )(dcum, zf, bias)


LOG2E = 1.4426950408889634
BWD_UNROLL = 8
FWD_UNROLL = 8


def _bias_terms(ct, pad):
    H, L = ct.shape
    T = ROW_TILE
    n = L // T
    c2 = ct * LOG2E
    k2 = jnp.where(jnp.arange(L)[None, :] < pad, -NEG, c2)
    return dict(cq_row=c2.reshape(H, n, 1, T), ck_row=k2.reshape(H, n, 1, T), ref=c2[:, ::T].reshape(H, n, 1, 1))


def _exchange_copies(srcs, outs, send_sems, recv_sems, local_sems, scatter):
    nt = len(srcs)
    x, y, c = lax.axis_index("x"), lax.axis_index("y"), lax.axis_index("c")
    me = 4 * x + 2 * y + c
    copies = [pltpu.make_async_copy(srcs[t].at[me] if scatter else srcs[t], outs[t].at[me], local_sems.at[t])
              for t in range(nt)]
    for k in range(1, N_DEV):
        px = (x + (k >> 2)) % 2
        py = (y + ((k >> 1) & 1)) % 2
        pc = (c + (k & 1)) % 2
        peer = 4 * px + 2 * py + pc
        for t in range(nt):
            copies.append(pltpu.make_async_remote_copy(
                src_ref=srcs[t].at[peer] if scatter else srcs[t],
                dst_ref=outs[t].at[me],
                send_sem=send_sems.at[k - 1, t], recv_sem=recv_sems.at[k - 1, t],
                device_id=(px, py, pc), device_id_type=pl.DeviceIdType.MESH))
    return copies


def _exchange_shapes(tensors, scatter):
    out_shape = tuple(jax.ShapeDtypeStruct(t.shape if scatter else (N_DEV,) + t.shape, t.dtype) for t in tensors)
    nt = len(tensors)
    sems = [pltpu.SemaphoreType.DMA((N_DEV - 1, nt)), pltpu.SemaphoreType.DMA((N_DEV - 1, nt)),
            pltpu.SemaphoreType.DMA((nt,))]
    return out_shape, sems


def _side_exchange(side, srcs, outs, sems, first, last):
    if not side:
        return

    @pl.when(first)
    def _():
        for cp in _exchange_copies(srcs, outs, *sems, side[1]):
            cp.start()

    @pl.when(last)
    def _():
        for cp in _exchange_copies(srcs, outs, *sems, side[1]):
            cp.wait()


def _as_row(col):
    return jnp.transpose(jnp.broadcast_to(col, (col.shape[0], LANES)))[0:1, :]


def _as_col(row):
    return jnp.transpose(jnp.broadcast_to(row, (LANES, row.shape[1])))[:, 0:1]


def _attn_fwd(q, qoff, k, koff, v, voff, zsrc, bias, heads, dk, dv, scale, name, dead, side=None):
    L = q.shape[0]
    T = ROW_TILE
    n = L // T
    c = scale * LOG2E
    dn_qk = (((1,), (1,)), ((), ()))

    ns = len(side[0]) if side else 0

    def body(*refs):
        q_ref, k_ref, v_ref, z_ref, ck_ref, ref_ref = refs[:6]
        o_ref, lse_ref, y_ref = refs[6 + ns:9 + ns]
        s_a, s_b = refs[9 + 2 * ns:11 + 2 * ns]
        i = pl.program_id(1)
        _side_exchange(side, refs[6:6 + ns], refs[9 + ns:9 + 2 * ns], refs[11 + 2 * ns:],
                       (pl.program_id(0) == 0) & (i == 0), (pl.program_id(0) == heads - 1) & (i == n - 1))
        qb = q_ref[...]
        ref = ref_ref[...]

        def scores(j, dst):
            start = pl.multiple_of(j * T, T)
            dst[...] = lax.dot_general(qb, k_ref[pl.ds(start, T), :], dn_qk, preferred_element_type=F32)

        def update(t, vj, carry):
            m, l, acc = carry
            m_new = jnp.maximum(m, jnp.max(t, axis=1, keepdims=True))
            p = jnp.exp2(t - m_new)
            a = jnp.exp2(m - m_new)
            l = a * l + jnp.sum(p, axis=1, keepdims=True)
            acc = a * acc + jnp.dot(p.astype(_MXU), vj, preferred_element_type=F32)
            return m_new, l, acc

        def soft(j, carry, cur, diag):
            start = pl.multiple_of(j * T, T)
            t = cur[...] * c - (ck_ref[j] - ref)
            if diag:
                r = lax.broadcasted_iota(jnp.int32, (T, T), 0)
                cc = lax.broadcasted_iota(jnp.int32, (T, T), 1)
                t = jnp.where(cc <= r, t, NEG)
            return update(t, v_ref[pl.ds(start, T), :], carry)

        def head_tile(carry):
            t = lax.dot_general(qb, k_ref[dead:T, :], dn_qk, preferred_element_type=F32) * c
            return update(t - (ck_ref[0][:, dead:] - ref), v_ref[dead:T, :], carry)

        def step(j, carry, cur, nxt):
            scores(j + 1, nxt)
            return soft(j, carry, cur, False)

        first = jnp.where(i > 0, 1, 0) if dead else 0

        def run_of(j0, steps, carry):
            for u in range(0, steps, 2):
                carry = step(j0 + u + 1, step(j0 + u, carry, s_a, s_b), s_b, s_a)
            return carry

        scores(first, s_a)
        carry = (jnp.full((T, 1), -jnp.inf, F32), jnp.zeros((T, 1), F32), jnp.zeros((T, dv), F32))
        if dead:
            carry = lax.cond(i > 0, head_tile, lambda cy: cy, carry)
        whole = i - first
        big = whole // FWD_UNROLL
        carry = lax.fori_loop(0, big, lambda g, cy: run_of(first + FWD_UNROLL * g, FWD_UNROLL, cy), carry)
        done = first + FWD_UNROLL * big
        carry = lax.fori_loop(0, (i - done) // 2, lambda g, cy: run_of(done + 2 * g, 2, cy), carry)
        odd = ((i - done) % 2) == 1
        carry = lax.cond(odd, lambda cy: step(i - 1, cy, s_a, s_b), lambda cy: cy, carry)
        m, l, acc = lax.cond(odd, lambda cy: soft(i, cy, s_b, True), lambda cy: soft(i, cy, s_a, True), carry)
        o = acc / l
        o_ref[...] = o
        y_ref[...] = (o * _silu(z_ref[...])).astype(_MXU)
        lse_ref[...] = _as_row(m + jnp.log2(l))

    hbm = pl.BlockSpec(memory_space=pl.ANY)
    side_shapes, side_sems = _exchange_shapes(*side) if side else ((), [])
    o, lse, y, *side_out = pl.pallas_call(
        body, name=name,
        out_shape=(jax.ShapeDtypeStruct((L, heads * dv), F32), jax.ShapeDtypeStruct((heads, n, 1, T), F32),
                   jax.ShapeDtypeStruct((L, heads * dv), _MXU)) + side_shapes,
        grid=(heads, n),
        in_specs=[pl.BlockSpec((T, dk), lambda h, i: (i, qoff + h)),
                  pl.BlockSpec((L, dk), lambda h, i: (0, koff + h)),
                  pl.BlockSpec((L, dv), lambda h, i: (0, voff + h)),
                  pl.BlockSpec((T, dv), lambda h, i: (i, h)),
                  pl.BlockSpec((None, n, 1, T), lambda h, i: (h, 0, 0, 0)),
                  pl.BlockSpec((None, None, 1, 1), lambda h, i: (h, i, 0, 0))] + [hbm] * ns,
        out_specs=(pl.BlockSpec((T, dv), lambda h, i: (i, h)),
                   pl.BlockSpec((None, None, 1, T), lambda h, i: (h, i, 0, 0)),
                   pl.BlockSpec((T, dv), lambda h, i: (i, h))) + (hbm,) * ns,
        scratch_shapes=[pltpu.VMEM((T, T), F32), pltpu.VMEM((T, T), F32)] + side_sems,
        compiler_params=_params(("arbitrary", "arbitrary")),
    )(q, k, v, zsrc, bias["ck_row"], bias["ref"], *(side[0] if side else ()))
    return o, lse + (bias["cq_row"] - bias["ref"]), y, side_out


def _attn_bwd(q, qoff, k, koff, v, voff, do, bias, lse_row, delta_row, heads, dk, dv, scale, name, dead,
              side=None, bias_grads=True):
    L = q.shape[0]
    T = ROW_TILE
    n = L // T
    c = scale * LOG2E
    dn_nt = (((1,), (1,)), ((), ()))
    dn_tn = (((0,), (0,)), ((), ()))
    ns = len(side[0]) if side else 0

    def body(*refs):
        q_ref, k_ref, v_ref, do_ref, cq_ref, ck_ref, ref_ref, lse_ref, dl_ref = refs[:9]
        dq_hbm, dk_ref, dv_ref, dck_ref, dcq_ref = refs[9 + ns:14 + ns]
        dq_acc, sem, st_a, dp_a, st_b, dp_b = refs[14 + 2 * ns:20 + 2 * ns]
        h = pl.program_id(0)
        j = pl.program_id(1)
        _side_exchange(side, refs[9:9 + ns], refs[14 + ns:14 + 2 * ns], refs[20 + 2 * ns:],
                       (h == 0) & (j == 0), (h == heads - 1) & (j == n - 1))

        @pl.when(j == 0)
        def _():
            dq_acc[...] = jnp.zeros_like(dq_acc)
            dcq_ref[...] = jnp.zeros_like(dcq_ref)

        ref = ref_ref[...]

        def run(k0, R):
            kb = k_ref[k0:k0 + R, :]
            vb = v_ref[k0:k0 + R, :]
            bcol = _as_col(ck_ref[j][:, k0:k0 + R] - ref)

            def front(i, st_dst, dp_dst):
                start = pl.multiple_of(jnp.minimum(i, n - 1) * T, T)
                st_dst[0:R, :] = lax.dot_general(kb, q_ref[pl.ds(start, T), :], dn_nt, preferred_element_type=F32)
                dp_dst[0:R, :] = lax.dot_general(vb, do_ref[pl.ds(start, T), :], dn_nt, preferred_element_type=F32)

            def back(i, carry, st_cur, dp_cur, diag):
                dk_a, dv_a, dck_a = carry
                start = pl.multiple_of(i * T, T)
                qi = q_ref[pl.ds(start, T), :]
                doi = do_ref[pl.ds(start, T), :]
                arow = (cq_ref[i] - ref) - lse_ref[i]
                st = st_cur[0:R, :] * c + arow - bcol
                if diag:
                    r = lax.broadcasted_iota(jnp.int32, (R, T), 0) + k0
                    cc = lax.broadcasted_iota(jnp.int32, (R, T), 1)
                    st = jnp.where(r <= cc, st, NEG)
                pt = jnp.exp2(st)
                dv_a = dv_a + jnp.dot(pt.astype(_MXU), doi, preferred_element_type=F32)
                dst = pt * (dp_cur[0:R, :] - dl_ref[i])
                if bias_grads:
                    dck_a = dck_a - jnp.sum(dst, axis=1, keepdims=True)
                    dcq_ref[i] += jnp.sum(dst, axis=0, keepdims=True)
                dsb = (dst * scale).astype(_MXU)
                dk_a = dk_a + jnp.dot(dsb, qi, preferred_element_type=F32)
                dq_acc[pl.ds(start, T), :] += lax.dot_general(dsb, kb, dn_tn, preferred_element_type=F32)
                return dk_a, dv_a, dck_a

            buf_a, buf_b = (st_a, dp_a), (st_b, dp_b)

            def step(i, carry, cur, nxt, diag=False):
                front(i + 1, *nxt)
                return back(i, carry, *cur, diag)

            def run_of(i0, steps, carry):
                for u in range(0, steps, 2):
                    carry = step(i0 + u + 1, step(i0 + u, carry, buf_b, buf_a), buf_a, buf_b)
                return carry

            front(j, *buf_a)
            init = (jnp.zeros((R, dk), F32), jnp.zeros((R, dv), F32), jnp.zeros((R, 1), F32))
            carry = step(j, init, buf_a, buf_b, True)
            big = (n - 1 - j) // BWD_UNROLL
            carry = lax.fori_loop(0, big, lambda g, cy: run_of(j + 1 + BWD_UNROLL * g, BWD_UNROLL, cy), carry)
            done = j + 1 + BWD_UNROLL * big
            carry = lax.fori_loop(0, (n - done) // 2, lambda g, cy: run_of(done + 2 * g, 2, cy), carry)
            dk_a, dv_a, dck_a = lax.cond(((n - done) % 2) == 1, lambda cy: step(n - 1, cy, buf_b, buf_a),
                                         lambda cy: cy, carry)
            if k0:
                dk_ref[0:k0, :] = jnp.zeros((k0, dk), F32)
                dv_ref[0:k0, :] = jnp.zeros((k0, dv), F32)
                dck_ref[:, 0:k0] = jnp.zeros((1, k0), F32)
            dk_ref[k0:k0 + R, :] = dk_a
            dv_ref[k0:k0 + R, :] = dv_a
            dck_ref[:, k0:k0 + R] = _as_row(dck_a)

        if dead:
            pl.when(j == 0)(lambda: run(dead, T - dead))
            pl.when(j > 0)(lambda: run(0, T))
        else:
            run(0, T)

        @pl.when(j == n - 1)
        def _():
            cp = pltpu.make_async_copy(dq_acc, dq_hbm.at[:, pl.ds(pl.multiple_of(h * dk, dk), dk)], sem)
            cp.start()
            cp.wait()

    hbm = pl.BlockSpec(memory_space=pl.ANY)
    side_shapes, side_sems = _exchange_shapes(*side) if side else ((), [])
    dq, dk_, dv_, dck, dcq, *side_out = pl.pallas_call(
        body, name=name,
        out_shape=(jax.ShapeDtypeStruct((L, heads * dk), F32), jax.ShapeDtypeStruct((L, heads * dk), F32),
                   jax.ShapeDtypeStruct((L, heads * dv), F32), jax.ShapeDtypeStruct((heads, n, 1, T), F32),
                   jax.ShapeDtypeStruct((heads, n, 1, T), F32)) + side_shapes,
        grid=(heads, n),
        in_specs=[pl.BlockSpec((L, dk), lambda h, j: (0, qoff + h)),
                  pl.BlockSpec((T, dk), lambda h, j: (j, koff + h)),
                  pl.BlockSpec((T, dv), lambda h, j: (j, voff + h)),
                  pl.BlockSpec((L, dv), lambda h, j: (0, h)),
                  pl.BlockSpec((None, n, 1, T), lambda h, j: (h, 0, 0, 0)),
                  pl.BlockSpec((None, n, 1, T), lambda h, j: (h, 0, 0, 0)),
                  pl.BlockSpec((None, None, 1, 1), lambda h, j: (h, j, 0, 0)),
                  pl.BlockSpec((None, n, 1, T), lambda h, j: (h, 0, 0, 0)),
                  pl.BlockSpec((None, n, 1, T), lambda h, j: (h, 0, 0, 0))] + [hbm] * ns,
        out_specs=(hbm,
                   pl.BlockSpec((T, dk), lambda h, j: (j, h)),
                   pl.BlockSpec((T, dv), lambda h, j: (j, h)),
                   pl.BlockSpec((None, None, 1, T), lambda h, j: (h, j, 0, 0)),
                   pl.BlockSpec((None, n, 1, T), lambda h, j: (h, 0, 0, 0))) + (hbm,) * ns,
        scratch_shapes=[pltpu.VMEM((L, dk), F32), pltpu.SemaphoreType.DMA] + [pltpu.VMEM((T, T), F32)] * 4
        + side_sems,
        compiler_params=_params(("arbitrary", "arbitrary")),
    )(q, k, v, do, bias["cq_row"], bias["ck_row"], bias["ref"], lse_row, delta_row, *(side[0] if side else ()))
    return dq, dk_, dv_, dck, dcq, side_out


_CQ0, _CKV0, _KR0 = D_MODEL, D_MODEL + MLA_Q_LORA, D_MODEL + MLA_Q_LORA + MLA_KV_LORA
_MLA_PROJ = _KR0 + LANES


def _rms(x, g):
    ms = jnp.mean(x * x, axis=-1, keepdims=True)
    return x * lax.rsqrt(ms + NORM_EPS) * g


def _rms_bwd(x, g, dy):
    ms = jnp.mean(x * x, axis=-1, keepdims=True)
    r = lax.rsqrt(ms + NORM_EPS)
    xh = x * r
    dxh = dy * g
    dx = r * (dxh - xh * jnp.mean(dxh * xh, axis=-1, keepdims=True))
    return dx, jnp.sum(dy * xh, axis=0, keepdims=True)


def _mla_norm_fwd(proj, gq, gkv):
    L = proj.shape[0]

    def body(p_ref, gq_ref, gkv_ref, cq_ref, ckv_ref):
        cq_ref[...] = _rms(p_ref[:, _CQ0:_CKV0], gq_ref[...]).astype(_MXU)
        ckv_ref[...] = _rms(p_ref[:, _CKV0:_KR0], gkv_ref[...]).astype(_MXU)

    return pl.pallas_call(
        body, name="mla_norm_fwd",
        out_shape=(jax.ShapeDtypeStruct((L, MLA_Q_LORA), _MXU), jax.ShapeDtypeStruct((L, MLA_KV_LORA), _MXU)),
        grid=(L // ROW_TILE,),
        in_specs=[_rows(_MLA_PROJ), _whole((1, MLA_Q_LORA)), _whole((1, MLA_KV_LORA))],
        out_specs=(_rows(MLA_Q_LORA), _rows(MLA_KV_LORA)),
        compiler_params=_params(("parallel",)),
    )(proj, gq, gkv)


def _mla_norm_bwd(proj, gq, gkv, dcqn, dckvn):
    L = proj.shape[0]

    def body(p_ref, gq_ref, gkv_ref, dq_ref, dkv_ref, dcq_ref, dckv_ref, dgq_ref, dgkv_ref):
        @pl.when(pl.program_id(0) == 0)
        def _():
            dgq_ref[...] = jnp.zeros_like(dgq_ref)
            dgkv_ref[...] = jnp.zeros_like(dgkv_ref)

        dx, dg = _rms_bwd(p_ref[:, _CQ0:_CKV0], gq_ref[...], dq_ref[...])
        dcq_ref[...] = dx.astype(_MXU)
        dgq_ref[...] += dg
        dx, dg = _rms_bwd(p_ref[:, _CKV0:_KR0], gkv_ref[...], dkv_ref[...])
        dckv_ref[...] = dx.astype(_MXU)
        dgkv_ref[...] += dg

    return pl.pallas_call(
        body, name="mla_norm_bwd",
        out_shape=(jax.ShapeDtypeStruct((L, MLA_Q_LORA), _MXU), jax.ShapeDtypeStruct((L, MLA_KV_LORA), _MXU),
                   jax.ShapeDtypeStruct((1, MLA_Q_LORA), F32), jax.ShapeDtypeStruct((1, MLA_KV_LORA), F32)),
        grid=(L // ROW_TILE,),
        in_specs=[_rows(_MLA_PROJ), _whole((1, MLA_Q_LORA)), _whole((1, MLA_KV_LORA)),
                  _rows(MLA_Q_LORA), _rows(MLA_KV_LORA)],
        out_specs=(_rows(MLA_Q_LORA), _rows(MLA_KV_LORA), _whole((1, MLA_Q_LORA)), _whole((1, MLA_KV_LORA))),
        compiler_params=_params(("arbitrary",)),
    )(proj, gq, gkv, dcqn, dckvn)


def _rot_tile(t, cos, sa, sb):
    half = MLA_ROPE // 2
    return t * cos + pltpu.roll(t, LANES - half, 1) * sa + pltpu.roll(t, half, 1) * sb


def _mla_rope_fwd(qf, kv, proj, cos, sa, sb):
    L = qf.shape[0]
    H = MLA_HEADS

    def body(q_ref, kv_ref, p_ref, c_ref, sa_ref, sb_ref, qo_ref, ko_ref):
        cos, sa, sb = c_ref[...], sa_ref[...], sb_ref[...]
        kr = _rot_tile(p_ref[...], cos, sa, sb).astype(_MXU)
        for h in range(H):
            b = h * MLA_DK
            qo_ref[:, b:b + LANES] = q_ref[:, b:b + LANES].astype(_MXU)
            qo_ref[:, b + LANES:b + 2 * LANES] = _rot_tile(q_ref[:, b + LANES:b + 2 * LANES], cos, sa, sb).astype(_MXU)
            ko_ref[:, b:b + LANES] = kv_ref[:, h * LANES:(h + 1) * LANES]
            ko_ref[:, b + LANES:b + 2 * LANES] = kr

    W = H * MLA_DK
    return pl.pallas_call(
        body, name="mla_rope_fwd",
        out_shape=(jax.ShapeDtypeStruct((L, W), _MXU), jax.ShapeDtypeStruct((L, W), _MXU)),
        grid=(L // ROW_TILE,),
        in_specs=[_rows(W), _rows(kv.shape[1]), pl.BlockSpec((ROW_TILE, LANES), lambda i: (i, _KR0 // LANES)),
                  _rows(LANES), _rows(LANES), _rows(LANES)],
        out_specs=(_rows(W), _rows(W)),
        compiler_params=_params(("parallel",)),
    )(qf, kv, proj, cos, sa, sb)


def _mla_rope_bwd(dq_full, dk_full, cos, sa, sb):
    L = dq_full.shape[0]
    H = MLA_HEADS
    W = H * MLA_DK

    def body(dq_ref, dk_ref, c_ref, sa_ref, sb_ref, dqf_ref, dkn_ref, dkr_ref):
        cos, sa, sb = c_ref[...], sa_ref[...], sb_ref[...]
        lane = lax.broadcasted_iota(jnp.int32, (ROW_TILE, LANES), 1)
        live = lane < MLA_ROPE
        krs = jnp.zeros((ROW_TILE, LANES), F32)
        for h in range(H):
            b = h * MLA_DK
            dqf_ref[:, b:b + LANES] = dq_ref[:, b:b + LANES].astype(_MXU)
            dqr = _rot_tile(dq_ref[:, b + LANES:b + 2 * LANES], cos, sa, sb)
            dqf_ref[:, b + LANES:b + 2 * LANES] = jnp.where(live, dqr, 0.0).astype(_MXU)
            dkn_ref[:, h * LANES:(h + 1) * LANES] = dk_ref[:, b:b + LANES].astype(_MXU)
            krs = krs + dk_ref[:, b + LANES:b + 2 * LANES]
        dkr_ref[...] = jnp.where(live, _rot_tile(krs, cos, sa, sb), 0.0).astype(_MXU)

    return pl.pallas_call(
        body, name="mla_rope_bwd",
        out_shape=(jax.ShapeDtypeStruct((L, W), _MXU), jax.ShapeDtypeStruct((L, H * LANES), _MXU),
                   jax.ShapeDtypeStruct((L, LANES), _MXU)),
        grid=(L // ROW_TILE,),
        in_specs=[_rows(W), _rows(W), _rows(LANES), _rows(LANES), _rows(LANES)],
        out_specs=(_rows(W), _rows(H * LANES), _rows(LANES)),
        compiler_params=_params(("parallel",)),
    )(dq_full, dk_full, cos, sa, sb)


_RET_QKW = RET_HEADS * RET_QK


def _ret_rope_fwd(qkz, cos, sin, pad):
    L = qkz.shape[0]
    hh = RET_QK // 2
    kscale = RET_QK ** -0.5

    def body(p_ref, c_ref, s_ref, q_ref, k_ref):
        cos, sin = c_ref[...], s_ref[...]
        row = pl.program_id(0) * ROW_TILE + lax.broadcasted_iota(jnp.int32, (ROW_TILE, hh), 0)
        keep = row >= pad
        for h in range(RET_HEADS):
            b = h * RET_QK
            t1, t2 = p_ref[:, b:b + hh], p_ref[:, b + hh:b + 2 * hh]
            q_ref[:, b:b + hh] = (t1 * cos - t2 * sin).astype(_MXU)
            q_ref[:, b + hh:b + 2 * hh] = (t2 * cos + t1 * sin).astype(_MXU)
            t1, t2 = p_ref[:, _RET_QKW + b:_RET_QKW + b + hh], p_ref[:, _RET_QKW + b + hh:_RET_QKW + b + 2 * hh]
            k_ref[:, b:b + hh] = jnp.where(keep, (t1 * cos - t2 * sin) * kscale, 0.0)
            k_ref[:, b + hh:b + 2 * hh] = jnp.where(keep, (t2 * cos + t1 * sin) * kscale, 0.0)

    return pl.pallas_call(
        body, name="ret_rope_fwd",
        out_shape=(jax.ShapeDtypeStruct((L, _RET_QKW), _MXU), jax.ShapeDtypeStruct((L, _RET_QKW), F32)),
        grid=(L // ROW_TILE,),
        in_specs=[_rows(2 * _RET_QKW), _rows(hh), _rows(hh)],
        out_specs=(_rows(_RET_QKW), _rows(_RET_QKW)),
        compiler_params=_params(("parallel",)),
    )(qkz, cos, sin)


def _ret_rope_bwd(dqr, dkr, cos, sin, pad):
    L = dqr.shape[0]
    hh = RET_QK // 2
    kscale = RET_QK ** -0.5

    def body(dq_ref, dk_ref, c_ref, s_ref, oq_ref, ok_ref):
        cos, sin = c_ref[...], s_ref[...]
        row = pl.program_id(0) * ROW_TILE + lax.broadcasted_iota(jnp.int32, (ROW_TILE, hh), 0)
        keep = row >= pad
        for h in range(RET_HEADS):
            b = h * RET_QK
            d1, d2 = dq_ref[:, b:b + hh], dq_ref[:, b + hh:b + 2 * hh]
            oq_ref[:, b:b + hh] = (d1 * cos + d2 * sin).astype(_MXU)
            oq_ref[:, b + hh:b + 2 * hh] = (d2 * cos - d1 * sin).astype(_MXU)
            d1, d2 = dk_ref[:, b:b + hh], dk_ref[:, b + hh:b + 2 * hh]
            ok_ref[:, b:b + hh] = jnp.where(keep, (d1 * cos + d2 * sin) * kscale, 0.0).astype(_MXU)
            ok_ref[:, b + hh:b + 2 * hh] = jnp.where(keep, (d2 * cos - d1 * sin) * kscale, 0.0).astype(_MXU)

    return pl.pallas_call(
        body, name="ret_rope_bwd",
        out_shape=(jax.ShapeDtypeStruct((L, _RET_QKW), _MXU), jax.ShapeDtypeStruct((L, _RET_QKW), _MXU)),
        grid=(L // ROW_TILE,),
        in_specs=[_rows(_RET_QKW), _rows(_RET_QKW), _rows(hh), _rows(hh)],
        out_specs=(_rows(_RET_QKW), _rows(_RET_QKW)),
        compiler_params=_params(("parallel",)),
    )(dqr, dkr, cos, sin)


def _ret_rows(L):
    return next(r * ROW_TILE for r in (3, 2, 1) if L % (r * ROW_TILE) == 0)


def _ret_decays():
    lg = np.log1p(-np.exp2(-5.0 - np.arange(RET_HEADS, dtype=np.float32))).astype(np.float32)
    i = np.arange(CHUNK, dtype=np.float32)
    rel = i[:, None] - i[None, :]
    dm = np.where(rel[None] >= 0, np.exp(rel[None] * lg[:, None, None]), 0.0).astype(np.float32)
    qd = np.exp((i[None, :] + 1.0) * lg[:, None]).astype(np.float32)[:, :, None]
    kd = np.exp((CHUNK - 1.0 - i)[None, :] * lg[:, None]).astype(np.float32)[:, :, None]
    cd = np.broadcast_to(np.exp(CHUNK * lg).astype(np.float32)[:, None, None], (RET_HEADS, 1, RET_V))
    return jnp.asarray(dm), jnp.asarray(qd), jnp.asarray(kd), jnp.asarray(np.ascontiguousarray(cd))


def _ret_fwd(qr, kr, v):
    L = qr.shape[0]
    nc = L // CHUNK
    RB = _ret_rows(L)
    G = RB // CHUNK
    dm, qd, kd, cd = _ret_decays()
    dn_nt = (((1,), (1,)), ((), ()))
    dn_tn = (((0,), (0,)), ((), ()))

    def body(q_ref, k_ref, v_ref, dm_ref, qd_ref, kd_ref, cd_ref, o_ref, st_ref, s_ref):
        @pl.when(pl.program_id(1) == 0)
        def _():
            s_ref[...] = jnp.zeros_like(s_ref)

        state = s_ref[...]
        for g in range(G):
            rows = slice(g * CHUNK, (g + 1) * CHUNK)
            q, k, vv = q_ref[rows, :], k_ref[rows, :], v_ref[rows, :]
            sb = state.astype(_MXU)
            st_ref[g] = sb
            s = lax.dot_general(q, k.astype(_MXU), dn_nt, preferred_element_type=F32) * dm_ref[...]
            intra = jnp.dot(s.astype(_MXU), vv, preferred_element_type=F32)
            cross = jnp.dot(q, sb, preferred_element_type=F32) * qd_ref[...]
            o_ref[rows, :] = intra + cross
            kdm = (k * kd_ref[...]).astype(_MXU)
            state = cd_ref[...] * state + lax.dot_general(kdm, vv, dn_tn, preferred_element_type=F32)
        s_ref[...] = state

    return pl.pallas_call(
        body, name="ret_fwd",
        out_shape=(jax.ShapeDtypeStruct((L, RET_HEADS * RET_V), F32),
                   jax.ShapeDtypeStruct((RET_HEADS, nc, RET_QK, RET_V), _MXU)),
        grid=(RET_HEADS, nc // G),
        in_specs=[pl.BlockSpec((RB, RET_QK), lambda h, c: (c, h)),
                  pl.BlockSpec((RB, RET_QK), lambda h, c: (c, h)),
                  pl.BlockSpec((RB, RET_V), lambda h, c: (c, h)),
                  pl.BlockSpec((None, CHUNK, CHUNK), lambda h, c: (h, 0, 0)),
                  pl.BlockSpec((None, CHUNK, 1), lambda h, c: (h, 0, 0)),
                  pl.BlockSpec((None, CHUNK, 1), lambda h, c: (h, 0, 0)),
                  pl.BlockSpec((None, 1, RET_V), lambda h, c: (h, 0, 0))],
        out_specs=(pl.BlockSpec((RB, RET_V), lambda h, c: (c, h)),
                   pl.BlockSpec((None, G, RET_QK, RET_V), lambda h, c: (h, c, 0, 0))),
        scratch_shapes=[pltpu.VMEM((RET_QK, RET_V), F32)],
        compiler_params=_params(("parallel", "arbitrary")),
    )(qr, kr, v, dm, qd, kd, cd)


def _ret_bwd(qr, kr, v, do, states):
    L = qr.shape[0]
    nc = L // CHUNK
    RB = _ret_rows(L)
    G = RB // CHUNK
    nt = nc // G
    dm, qd, kd, cd = _ret_decays()
    dn_nt = (((1,), (1,)), ((), ()))
    dn_tn = (((0,), (0,)), ((), ()))

    def body(q_ref, k_ref, v_ref, do_ref, st_ref, dm_ref, qd_ref, kd_ref, cd_ref,
             dq_ref, dk_ref, dv_ref, ds_ref):
        @pl.when(pl.program_id(1) == 0)
        def _():
            ds_ref[...] = jnp.zeros_like(ds_ref)

        dstate = ds_ref[...]
        dmat = dm_ref[...]
        for g in reversed(range(G)):
            rows = slice(g * CHUNK, (g + 1) * CHUNK)
            q, k, vv, do = q_ref[rows, :], k_ref[rows, :], v_ref[rows, :], do_ref[rows, :]
            kb = k.astype(_MXU)
            dob = do.astype(_MXU)
            doq = (do * qd_ref[...]).astype(_MXU)
            dsb = dstate.astype(_MXU)
            s = (lax.dot_general(q, kb, dn_nt, preferred_element_type=F32) * dmat).astype(_MXU)
            draw = (lax.dot_general(dob, vv, dn_nt, preferred_element_type=F32) * dmat).astype(_MXU)
            kdm = (k * kd_ref[...]).astype(_MXU)
            dv = (lax.dot_general(s, dob, dn_tn, preferred_element_type=F32)
                  + jnp.dot(kdm, dsb, preferred_element_type=F32))
            dq = (jnp.dot(draw, kb, preferred_element_type=F32)
                  + lax.dot_general(doq, st_ref[g], dn_nt, preferred_element_type=F32))
            dk = (lax.dot_general(draw, q, dn_tn, preferred_element_type=F32)
                  + lax.dot_general(vv, dsb, dn_nt, preferred_element_type=F32) * kd_ref[...])
            dq_ref[rows, :] = dq
            dk_ref[rows, :] = dk
            dv_ref[rows, :] = dv.astype(_MXU)
            dstate = cd_ref[...] * dstate + lax.dot_general(q, doq, dn_tn, preferred_element_type=F32)
        ds_ref[...] = dstate

    rev = lambda h, c: (nt - 1 - c, h)
    return pl.pallas_call(
        body, name="ret_bwd",
        out_shape=(jax.ShapeDtypeStruct((L, _RET_QKW), F32), jax.ShapeDtypeStruct((L, _RET_QKW), F32),
                   jax.ShapeDtypeStruct((L, RET_HEADS * RET_V), _MXU)),
        grid=(RET_HEADS, nt),
        in_specs=[pl.BlockSpec((RB, RET_QK), rev),
                  pl.BlockSpec((RB, RET_QK), rev),
                  pl.BlockSpec((RB, RET_V), rev),
                  pl.BlockSpec((RB, RET_V), rev),
                  pl.BlockSpec((None, G, RET_QK, RET_V), lambda h, c: (h, nt - 1 - c, 0, 0)),
                  pl.BlockSpec((None, CHUNK, CHUNK), lambda h, c: (h, 0, 0)),
                  pl.BlockSpec((None, CHUNK, 1), lambda h, c: (h, 0, 0)),
                  pl.BlockSpec((None, CHUNK, 1), lambda h, c: (h, 0, 0)),
                  pl.BlockSpec((None, 1, RET_V), lambda h, c: (h, 0, 0))],
        out_specs=(pl.BlockSpec((RB, RET_QK), rev), pl.BlockSpec((RB, RET_QK), rev),
                   pl.BlockSpec((RB, RET_V), rev)),
        scratch_shapes=[pltpu.VMEM((RET_QK, RET_V), F32)],
        compiler_params=_params(("parallel", "arbitrary")),
    )(qr, kr, v, do, states, dm, qd, kd, cd)


def _gn(o, g):
    mu = jnp.mean(o, axis=-1, keepdims=True)
    oc = o - mu
    var = jnp.mean(oc * oc, axis=-1, keepdims=True)
    return oc * lax.rsqrt(var + NORM_EPS) * g


def _ret_gate_fwd(o, qkz, gn_g):
    L, W = o.shape
    zblk = qkz.shape[1] // W - 1

    def body(o_ref, z_ref, g_ref, y_ref):
        for h in range(RET_HEADS):
            sl = slice(h * RET_V, (h + 1) * RET_V)
            y_ref[:, sl] = (_gn(o_ref[:, sl], g_ref[:, sl]) * _silu(z_ref[:, sl])).astype(_MXU)

    return pl.pallas_call(
        body, name="ret_gate_fwd",
        out_shape=jax.ShapeDtypeStruct((L, W), _MXU),
        grid=(L // ROW_TILE,),
        in_specs=[_rows(W), pl.BlockSpec((ROW_TILE, W), lambda i: (i, zblk)), _whole((1, W))],
        out_specs=_rows(W),
        compiler_params=_params(("parallel",)),
    )(o, qkz, gn_g)


def _ret_gate_bwd(dy, o, qkz, gn_g):
    L, W = o.shape
    zblk = qkz.shape[1] // W - 1

    def body(dy_ref, o_ref, z_ref, g_ref, do_ref, dz_ref, dg_ref):
        @pl.when(pl.program_id(0) == 0)
        def _():
            dg_ref[...] = jnp.zeros_like(dg_ref)

        for h in range(RET_HEADS):
            sl = slice(h * RET_V, (h + 1) * RET_V)
            z = z_ref[:, sl]
            o = o_ref[:, sl]
            g = g_ref[:, sl]
            dy = dy_ref[:, sl]
            mu = jnp.mean(o, axis=-1, keepdims=True)
            oc = o - mu
            var = jnp.mean(oc * oc, axis=-1, keepdims=True)
            r = lax.rsqrt(var + NORM_EPS)
            xh = oc * r
            dn = dy * _silu(z)
            dz_ref[:, sl] = (dy * (xh * g) * _dsilu(z)).astype(_MXU)
            dg_ref[:, sl] += jnp.sum(dn * xh, axis=0, keepdims=True)
            dxh = dn * g
            do_ref[:, sl] = r * (dxh - jnp.mean(dxh, axis=-1, keepdims=True)
                                 - xh * jnp.mean(dxh * xh, axis=-1, keepdims=True))

    return pl.pallas_call(
        body, name="ret_gate_bwd",
        out_shape=(jax.ShapeDtypeStruct((L, W), F32), jax.ShapeDtypeStruct((L, W), _MXU),
                   jax.ShapeDtypeStruct((1, W), F32)),
        grid=(L // ROW_TILE,),
        in_specs=[_rows(W), _rows(W), pl.BlockSpec((ROW_TILE, W), lambda i: (i, zblk)), _whole((1, W))],
        out_specs=(_rows(W), _rows(W), _whole((1, W))),
        compiler_params=_params(("arbitrary",)),
    )(dy, o, qkz, gn_g)


def _dead_rows(pad):
    return pad // LANES * LANES


def _heads_t(a):
    return a[:, :FOX_HEADS].T


def _fox_fwd(h, w, lnw, pad, tag, side=None):
    w_in, b_f, w_out = w
    L = h.shape[0]
    W = D_MODEL
    w_qkv = w_in[:, :3 * W]
    w_zf = jnp.concatenate([w_in[:, 3 * W:4 * W], jnp.pad(w_in[:, 4 * W:], ((0, 0), (0, LANES - FOX_HEADS)))], axis=1)
    bias = jnp.pad(b_f, (0, LANES - FOX_HEADS))[None, :]
    qkv = _matmul(h, w_qkv, "nn", _MXU, f"{tag}_qkv")
    zf = _matmul(h, w_zf, "nn", F32, f"{tag}_zf")
    cum = _fox_cum(zf, bias, pad, f"{tag}_cum")
    ab = _bias_terms(_heads_t(cum), pad)
    nb = W // LANES
    o, lse, y, side_out = _attn_fwd(qkv, 0, qkv, nb, qkv, 2 * nb, zf, ab, FOX_HEADS, LANES, LANES, LANES ** -0.5,
                                    f"{tag}_attn_fwd", _dead_rows(pad), side)
    mo = _matmul(y, w_out, "nn", F32, f"{tag}_out")
    zl, hn = _ln_fwd(h, mo, lnw[0], lnw[1], f"{tag}_ln_fwd")
    return hn, dict(h=h, qkv=qkv, zf=zf, ab=ab, o=o, lse=lse, y=y, zl=zl,
                    w_all=jnp.concatenate([w_qkv, w_zf], axis=1), w_out=w_out, bias=bias), side_out


def _fox_bwd(s, dzl, pad, tag, side=None, send_own=False):
    W = D_MODEL
    nb = W // LANES
    dy = _matmul(dzl, s["w_out"], "nt", F32, f"{tag}_dy")
    d_wout = _matmul(s["y"], dzl, "tn", F32, f"{tag}_dwout")
    if send_own:
        side = (list(side[0]) + [_split_layer("fox_w_out", d_wout)], True)
    do, delta, dzg = _gate_bwd(dy, s["o"], s["zf"], FOX_HEADS, f"{tag}_gate_bwd")
    qkv = s["qkv"]
    dq, dk, dv, dck, dcq, side_out = _attn_bwd(qkv, 0, qkv, nb, qkv, 2 * nb, do, s["ab"], s["lse"], delta,
                                               FOX_HEADS, LANES, LANES, LANES ** -0.5, f"{tag}_attn_bwd",
                                               _dead_rows(pad), side)
    dcum = jnp.pad((dck + dcq).reshape(FOX_HEADS, -1).T, ((0, 0), (0, LANES - FOX_HEADS)))
    df, dbias = _fox_cum_bwd(dcum, s["zf"], s["bias"], pad, f"{tag}_cum_bwd")
    dproj = jnp.concatenate([dq.astype(_MXU), dk.astype(_MXU), dv.astype(_MXU), dzg, df], axis=1)
    d_wall = _matmul(s["h"], dproj, "tn", F32, f"{tag}_dwin")
    d_win = jnp.concatenate([d_wall[:, :4 * W], d_wall[:, 4 * W:4 * W + FOX_HEADS]], axis=1)
    if send_own:
        dh, win_parts = _matmul(dproj, s["w_all"], "nt", F32, f"{tag}_dh", ([_split_layer("fox_w_in", d_win)], True))
        side_out = list(side_out) + list(win_parts)
    else:
        dh = _matmul(dproj, s["w_all"], "nt", F32, f"{tag}_dh")
    return dh, (d_win, dbias[0, :FOX_HEADS], d_wout), side_out


def _mla_tables(L, pad):
    pos = (jnp.arange(L) - pad).astype(F32)
    inv = ROPE_BASE ** (-jnp.arange(0, MLA_ROPE, 2, dtype=F32) / MLA_ROPE)
    ang = pos[:, None] * inv[None, :]
    c, s = jnp.cos(ang), jnp.sin(ang)
    z32, z64 = jnp.zeros_like(s), jnp.zeros((L, LANES - MLA_ROPE), F32)
    cos = jnp.concatenate([c, c, jnp.ones_like(z64)], axis=1)
    sa = jnp.concatenate([-s, z32, z64], axis=1)
    sb = jnp.concatenate([z32, s, z64], axis=1)
    return cos, sa, sb


def _mla_weights(w_in, w_uq, w_ukv):
    a, b = MLA_Q_LORA, MLA_Q_LORA + MLA_KV_LORA
    w_zc = jnp.concatenate([w_in[:, b + MLA_ROPE:], w_in[:, :b + MLA_ROPE],
                            jnp.zeros((D_MODEL, LANES - MLA_ROPE), w_in.dtype)], axis=1)
    uq = w_uq.reshape(MLA_Q_LORA, MLA_HEADS, MLA_NOPE + MLA_ROPE)
    uq = jnp.pad(uq, ((0, 0), (0, 0), (0, MLA_DK - MLA_NOPE - MLA_ROPE))).reshape(MLA_Q_LORA, MLA_HEADS * MLA_DK)
    ukv = w_ukv.reshape(MLA_KV_LORA, MLA_HEADS, MLA_NOPE + MLA_V)
    ukv = jnp.concatenate([ukv[:, :, :MLA_NOPE].reshape(MLA_KV_LORA, -1), ukv[:, :, MLA_NOPE:].reshape(MLA_KV_LORA, -1)], axis=1)
    return w_zc, uq, ukv


def _mla_fwd(h, w, lnw, pad):
    w_in, gq, gkv, w_uq, w_ukv, w_out = w
    L = h.shape[0]
    w_zc, uq, ukv = _mla_weights(w_in, w_uq, w_ukv)
    cos, sa, sb = _mla_tables(L, pad)
    proj = _matmul(h, w_zc, "nn", F32, "mla_proj")
    cqn, ckvn = _mla_norm_fwd(proj, gq[None, :], gkv[None, :])
    qf = _matmul(cqn, uq, "nn", F32, "mla_uq")
    kv = _matmul(ckvn, ukv, "nn", _MXU, "mla_ukv")
    q_full, k_full = _mla_rope_fwd(qf, kv, proj, cos, sa, sb)
    ab = _bias_terms(jnp.zeros((MLA_HEADS, L), F32), pad)
    nb = MLA_HEADS * MLA_NOPE // LANES
    o, lse, y, _ = _attn_fwd(q_full, 0, k_full, 0, kv, nb, proj, ab, MLA_HEADS, MLA_DK, MLA_V,
                             (MLA_NOPE + MLA_ROPE) ** -0.5, "mla_attn_fwd", _dead_rows(pad))
    mo = _matmul(y, w_out, "nn", F32, "mla_out")
    zl, hn = _ln_fwd(h, mo, lnw[0], lnw[1], "mla_ln_fwd")
    return hn, dict(h=h, proj=proj, cqn=cqn, ckvn=ckvn, kv=kv, q_full=q_full, k_full=k_full, ab=ab,
                    dead=_dead_rows(pad),
                    o=o, lse=lse, y=y, zl=zl, w_zc=w_zc, uq=uq, ukv=ukv, w_out=w_out, gq=gq, gkv=gkv,
                    tabs=(cos, sa, sb))


def _mla_bwd(s, dzl):
    H = MLA_HEADS
    dy = _matmul(dzl, s["w_out"], "nt", F32, "mla_dy")
    d_wout = _matmul(s["y"], dzl, "tn", F32, "mla_dwout")
    do, delta, dzg = _gate_bwd(dy, s["o"], s["proj"], H, "mla_gate_bwd")
    nb = H * MLA_NOPE // LANES
    dq_full, dk_full, dv, _, _, _ = _attn_bwd(s["q_full"], 0, s["k_full"], 0, s["kv"], nb, do, s["ab"],
                                              s["lse"], delta,
                                              H, MLA_DK, MLA_V, (MLA_NOPE + MLA_ROPE) ** -0.5, "mla_attn_bwd",
                                              s["dead"], bias_grads=False)
    cos, sa, sb = s["tabs"]
    dqf, dkn, dkr = _mla_rope_bwd(dq_full, dk_full, cos, -sa, -sb)
    d_uq = _matmul(s["cqn"], dqf, "tn", F32, "mla_duq")
    dcqn = _matmul(dqf, s["uq"], "nt", F32, "mla_dcqn")
    dkv = jnp.concatenate([dkn, dv.astype(_MXU)], axis=1)
    d_ukv = _matmul(s["ckvn"], dkv, "tn", F32, "mla_dukv")
    dckvn = _matmul(dkv, s["ukv"], "nt", F32, "mla_dckvn")
    dcq, dckv, dgq, dgkv = _mla_norm_bwd(s["proj"], s["gq"][None, :], s["gkv"][None, :], dcqn, dckvn)
    dproj = jnp.concatenate([dzg, dcq, dckv, dkr], axis=1)
    d_wzc = _matmul(s["h"], dproj, "tn", F32, "mla_dwin")
    dh = _matmul(dproj, s["w_zc"], "nt", F32, "mla_dh")
    W = D_MODEL
    d_win = jnp.concatenate([d_wzc[:, W:W + MLA_Q_LORA + MLA_KV_LORA + MLA_ROPE], d_wzc[:, :W]], axis=1)
    d_wuq = d_uq.reshape(MLA_Q_LORA, H, MLA_DK)[:, :, :MLA_NOPE + MLA_ROPE].reshape(MLA_Q_LORA, -1)
    hk = H * MLA_NOPE
    d_wukv = jnp.concatenate([d_ukv[:, :hk].reshape(MLA_KV_LORA, H, MLA_NOPE),
                              d_ukv[:, hk:].reshape(MLA_KV_LORA, H, MLA_V)], axis=2).reshape(MLA_KV_LORA, -1)
    return dh, (d_win, dgq[0], dgkv[0], d_wuq, d_wukv, d_wout)


def _ret_tables(L, pad):
    pos = (jnp.arange(L) - pad).astype(F32)
    inv = 1.0 / (ROPE_BASE ** jnp.linspace(0.0, 1.0, RET_QK // 2, dtype=F32))
    ang = pos[:, None] * inv[None, :]
    return jnp.cos(ang), jnp.sin(ang)


def _ret_layer_fwd(h, w, lnw, pad):
    w_in, gn_g, w_out = w
    L = h.shape[0]
    q1 = 2 * _RET_QKW
    v1 = q1 + RET_HEADS * RET_V
    w_qkz = jnp.concatenate([w_in[:, :q1], w_in[:, v1:]], axis=1)
    cos, sin = _ret_tables(L, pad)
    qkz = _matmul(h, w_qkz, "nn", F32, "ret_qkz")
    v = _matmul(h, w_in[:, q1:v1], "nn", _MXU, "ret_v")
    qr, kr = _ret_rope_fwd(qkz, cos, sin, pad)
    o, states = _ret_fwd(qr, kr, v)
    y = _ret_gate_fwd(o, qkz, gn_g[None, :])
    mo = _matmul(y, w_out, "nn", F32, "ret_out")
    zl, hn = _ln_fwd(h, mo, lnw[0], lnw[1], "ret_ln_fwd")
    return hn, dict(h=h, qkz=qkz, v=v, qr=qr, kr=kr, o=o, states=states, y=y, zl=zl, w_in=w_in, w_out=w_out,
                    gn_g=gn_g, tabs=(cos, sin))


def _ret_layer_bwd(s, dzl, pad):
    dy = _matmul(dzl, s["w_out"], "nt", F32, "ret_dy")
    d_wout = _matmul(s["y"], dzl, "tn", F32, "ret_dwout")
    do, dzg, dgn = _ret_gate_bwd(dy, s["o"], s["qkz"], s["gn_g"][None, :])
    dqr, dkr, dv = _ret_bwd(s["qr"], s["kr"], s["v"], do, s["states"])
    cos, sin = s["tabs"]
    dq, dk = _ret_rope_bwd(dqr, dkr, cos, sin, pad)
    dproj = jnp.concatenate([dq, dk, dv, dzg], axis=1)
    d_win = _matmul(s["h"], dproj, "tn", F32, "ret_dwin")
    dh = _matmul(dproj, s["w_in"], "nt", F32, "ret_dh")
    return dh, (d_win, dgn[0], d_wout)


def _local_step(x, target, meta, fox0_w, late_shards, small, ln_g, ln_b):
    n_real = x.shape[0]
    first = -(-N_META // ROW_TILE)
    pad = first * ROW_TILE - N_META
    h0 = jnp.concatenate([jnp.zeros((pad, D_MODEL), F32), meta, x], axis=0)
    lnw = [(ln_g[i][None, :], ln_b[i][None, :]) for i in range(4)]
    b_f = small["fox_b_f"]

    h1, s0, gathered = _fox_fwd(h0, (fox0_w[0], b_f[0], fox0_w[1]), lnw[0], pad, "fox0", (late_shards, False))
    w = {key: _merge_layer(key[0], g).astype(_MXU) for key, g in zip(_LATE, gathered)}
    h2, s1 = _mla_fwd(h1, (w["mla_w_in", 0], small["mla_q_norm"][0], small["mla_kv_norm"][0], w["mla_w_uq", 0],
                           w["mla_w_ukv", 0], w["mla_w_out", 0]), lnw[1], pad)
    h3, s2 = _ret_layer_fwd(h2, (w["ret_w_in", 0], small["ret_gn_g"][0], w["ret_w_out", 0]), lnw[2], pad)
    h4, s3, _ = _fox_fwd(h3, (w["fox_w_in", 1], b_f[1], w["fox_w_out", 1]), lnw[3], pad, "fox1")

    dh4, loss = _loss_head(h4, target, first)
    dz3, dg3, db3 = _ln_bwd(s3["zl"], lnw[3][0], dh4, None, "ln3_bwd")
    dm3, gf1, _ = _fox_bwd(s3, dz3, pad, "fox1")
    dz2, dg2, db2 = _ln_bwd(s2["zl"], lnw[2][0], dz3, dm3, "ln2_bwd")
    dm2, gr = _ret_layer_bwd(s2, dz2, pad)
    dz1, dg1, db1 = _ln_bwd(s1["zl"], lnw[1][0], dz2, dm2, "ln1_bwd")
    dm1, gm = _mla_bwd(s1, dz1)
    dz0, dg0, db0 = _ln_bwd(s0["zl"], lnw[0][0], dz1, dm1, "ln0_bwd")
    late = {("fox_w_in", 1): gf1[0], ("fox_w_out", 1): gf1[2], ("mla_w_in", 0): gm[0], ("mla_w_uq", 0): gm[3],
            ("mla_w_ukv", 0): gm[4], ("mla_w_out", 0): gm[5], ("ret_w_in", 0): gr[0], ("ret_w_out", 0): gr[2]}
    dm0, gf0, sent = _fox_bwd(s0, dz0, pad, "fox0", ([_split_layer(k[0], late[k]) for k in _LATE], True), True)
    parts = dict(zip(_LATE + [("fox_w_out", 0), ("fox_w_in", 0)], sent))
    grad_x, grad_meta = _input_grads(dz0, dm0, first, n_real)

    small_grads = dict(
        meta=grad_meta, fox_b_f=jnp.stack([gf0[1], gf1[1]]), mla_q_norm=gm[1][None], mla_kv_norm=gm[2][None],
        ret_gn_g=gr[1][None], ln_g=jnp.concatenate([dg0, dg1, dg2, dg3], axis=0),
        ln_b=jnp.concatenate([db0, db1, db2, db3], axis=0))
    return loss, grad_x, parts, small_grads


def _exchange(tensors, scatter, name):
    nt = len(tensors)

    def body(*refs):
        copies = _exchange_copies(refs[:nt], refs[nt:2 * nt], *refs[2 * nt:], scatter)
        for cp in copies:
            cp.start()
        for cp in copies:
            cp.wait()

    hbm = pl.BlockSpec(memory_space=pl.ANY)
    out_shape, sems = _exchange_shapes(tensors, scatter)
    return pl.pallas_call(
        body, name=name, out_shape=out_shape, in_specs=[hbm] * nt, out_specs=(hbm,) * nt, scratch_shapes=sems,
    )(*tensors)


def _adamw(parts, w, m, v, name):
    R, C = w.shape
    tr = next(t for t in range(min(R, 256), 0, -1) if R % t == 0 and (t % 8 == 0 or t == R))

    def body(p_ref, w_ref, m_ref, v_ref, g_ref, d_ref, mo_ref, vo_ref):
        g = p_ref[0]
        for d in range(1, N_DEV):
            g = g + p_ref[d]
        mn = ADAM_B1 * m_ref[...] + (1.0 - ADAM_B1) * g
        vn = ADAM_B2 * v_ref[...] + (1.0 - ADAM_B2) * (g * g)
        m_hat = mn / (1.0 - ADAM_B1 ** ADAM_STEP)
        v_hat = vn / (1.0 - ADAM_B2 ** ADAM_STEP)
        g_ref[...] = g
        d_ref[...] = -ADAM_LR * (m_hat / (jnp.sqrt(v_hat) + ADAM_EPS) + ADAM_WD * w_ref[...])
        mo_ref[...] = mn
        vo_ref[...] = vn

    blk = pl.BlockSpec((tr, C), lambda i: (i, 0))
    sd = jax.ShapeDtypeStruct((R, C), F32)
    return pl.pallas_call(
        body, name=name,
        out_shape=(sd, sd, sd, sd),
        grid=(R // tr,),
        in_specs=[pl.BlockSpec((N_DEV, tr, C), lambda i: (0, i, 0)), blk, blk, blk],
        out_specs=(blk, blk, blk, blk),
        compiler_params=_params(("parallel",)),
    )(parts, w, m, v)


_SPECS = dict(
    meta=((16, 1024), 1),
    fox_w_in=((2, 1024, 4104), 2), fox_b_f=((2, 8), None), fox_w_out=((2, 1024, 1024), 1),
    mla_w_in=((1, 1024, 1728), 2), mla_q_norm=((1, 384), None), mla_kv_norm=((1, 256), None),
    mla_w_uq=((1, 384, 1536), 2), mla_w_ukv=((1, 256, 2048), 2), mla_w_out=((1, 1024, 1024), 1),
    ret_w_in=((1, 1024, 6144), 2), ret_gn_g=((1, 2048), 1), ret_w_out=((1, 2048, 1024), 1),
    ln_g=((4, 1024), None), ln_b=((4, 1024), None),
)
_NAMES = list(_SPECS)
_BIG = ["fox_w_in", "fox_w_out", "mla_w_in", "mla_w_uq", "mla_w_ukv", "mla_w_out", "ret_w_in", "ret_w_out"]
_SMALL = [n for n in _NAMES if n not in _BIG]
_SMALL_GATHERED = ["meta", "ret_gn_g"]
_PACK_ROWS = 8


def _shard_shape(name):
    shape, ax = _SPECS[name]
    if ax is None:
        return shape
    return tuple(s // N_DEV if i == ax else s for i, s in enumerate(shape))


_EARLY = [("fox_w_in", 0), ("fox_w_out", 0)]
_LATE = [("fox_w_in", 1), ("fox_w_out", 1), ("mla_w_in", 0), ("mla_w_uq", 0), ("mla_w_ukv", 0), ("mla_w_out", 0),
         ("ret_w_in", 0), ("ret_w_out", 0)]


def _split_at(full, ax):
    shape = full.shape
    parts = full.reshape(shape[:ax] + (N_DEV, shape[ax] // N_DEV) + shape[ax + 1:])
    return jnp.moveaxis(parts, ax, 0)


def _merge_at(stacked, ax):
    full = jnp.moveaxis(stacked, 0, ax)
    return full.reshape(full.shape[:ax] + (-1,) + full.shape[ax + 2:])


def _split(name, full):
    return _split_at(full, _SPECS[name][1])


def _merge(name, stacked):
    return _merge_at(stacked, _SPECS[name][1])


def _split_layer(name, full):
    return _split_at(full, _SPECS[name][1] - 1)


def _merge_layer(name, stacked):
    return _merge_at(stacked, _SPECS[name][1] - 1)


def _pad_rows(flat):
    n = flat.shape[-1]
    unit = _PACK_ROWS * LANES
    total = -(-n // unit) * unit
    flat = jnp.pad(flat, [(0, 0)] * (flat.ndim - 1) + [(0, total - n)])
    return flat.reshape(flat.shape[:-1] + (total // LANES, LANES))


def _pack_small(tree, names, tail):
    return _pad_rows(jnp.concatenate([tree[n].reshape(-1) for n in names] + [tail]))


def _unpack_small(buf, names):
    flat, out, at = buf.reshape(-1), {}, 0
    for n in names:
        shp = _shard_shape(n)
        out[n] = flat[at:at + math.prod(shp)].reshape(shp)
        at += math.prod(shp)
    return out, flat[at:]


def kernel(x, meta, fox_w_in, fox_b_f, fox_w_out, mla_w_in, mla_q_norm, mla_kv_norm, mla_w_uq, mla_w_ukv, mla_w_out, ret_w_in, ret_gn_g, ret_w_out, ln_g, ln_b, loss_target, m_meta, m_fox_w_in, m_fox_b_f, m_fox_w_out, m_mla_w_in, m_mla_q_norm, m_mla_kv_norm, m_mla_w_uq, m_mla_w_ukv, m_mla_w_out, m_ret_w_in, m_ret_gn_g, m_ret_w_out, m_ln_g, m_ln_b, v_meta, v_fox_w_in, v_fox_b_f, v_fox_w_out, v_mla_w_in, v_mla_q_norm, v_mla_kv_norm, v_mla_w_uq, v_mla_w_ukv, v_mla_w_out, v_ret_w_in, v_ret_gn_g, v_ret_w_out, v_ln_g, v_ln_b):
    w = dict(meta=meta, fox_w_in=fox_w_in, fox_b_f=fox_b_f, fox_w_out=fox_w_out, mla_w_in=mla_w_in,
             mla_q_norm=mla_q_norm, mla_kv_norm=mla_kv_norm, mla_w_uq=mla_w_uq, mla_w_ukv=mla_w_ukv,
             mla_w_out=mla_w_out, ret_w_in=ret_w_in, ret_gn_g=ret_gn_g, ret_w_out=ret_w_out, ln_g=ln_g, ln_b=ln_b)
    m = dict(meta=m_meta, fox_w_in=m_fox_w_in, fox_b_f=m_fox_b_f, fox_w_out=m_fox_w_out, mla_w_in=m_mla_w_in,
             mla_q_norm=m_mla_q_norm, mla_kv_norm=m_mla_kv_norm, mla_w_uq=m_mla_w_uq, mla_w_ukv=m_mla_w_ukv,
             mla_w_out=m_mla_w_out, ret_w_in=m_ret_w_in, ret_gn_g=m_ret_gn_g, ret_w_out=m_ret_w_out, ln_g=m_ln_g, ln_b=m_ln_b)
    v = dict(meta=v_meta, fox_w_in=v_fox_w_in, fox_b_f=v_fox_b_f, fox_w_out=v_fox_w_out, mla_w_in=v_mla_w_in,
             mla_q_norm=v_mla_q_norm, mla_kv_norm=v_mla_kv_norm, mla_w_uq=v_mla_w_uq, mla_w_ukv=v_mla_w_ukv,
             mla_w_out=v_mla_w_out, ret_w_in=v_ret_w_in, ret_gn_g=v_ret_gn_g, ret_w_out=v_ret_w_out, ln_g=v_ln_g, ln_b=v_ln_b)

    def shard16(key):
        return w[key[0]][key[1]].astype(jnp.bfloat16)

    none = jnp.zeros((0,), F32)
    gathered = _exchange([shard16(k) for k in _EARLY] + [_pack_small(w, _SMALL_GATHERED, none)],
                         False, "weights_all_gather")
    fox0_w = [_merge_layer(k[0], g).astype(_MXU) for k, g in zip(_EARLY, gathered)]
    small = {n: w[n] for n in _SMALL}
    pieces = [_unpack_small(gathered[-1][d], _SMALL_GATHERED)[0] for d in range(N_DEV)]
    for n in _SMALL_GATHERED:
        small[n] = _merge(n, jnp.stack([p[n] for p in pieces]))

    loss, grad_x, by_key, grads = _local_step(
        x[0], loss_target[0], small["meta"], fox0_w, [shard16(k) for k in _LATE], small, ln_g, ln_b)

    def small_rows(n):
        if _SPECS[n][1] is None:
            return jnp.broadcast_to(grads[n].reshape(1, -1), (N_DEV, grads[n].size))
        return _split(n, grads[n]).reshape(N_DEV, -1)

    small_out = _pad_rows(jnp.concatenate([small_rows(n) for n in _SMALL]
                                          + [jnp.broadcast_to(loss.reshape(1, 1), (N_DEV, 1))], axis=1))
    parts = _exchange([small_out], True, "grads_all_to_all")

    def rows2d(a):
        return a.reshape(-1, a.shape[-1])

    out = {}
    for n in _BIG:
        layers = [by_key[n, i] for i in range(_SPECS[n][0][0])]
        p = layers[0] if len(layers) == 1 else jnp.stack(layers, axis=1)
        res = _adamw(p.reshape(N_DEV, -1, p.shape[-1]), rows2d(w[n]), rows2d(m[n]), rows2d(v[n]), f"adamw_{n}")
        out[n] = [r.reshape(w[n].shape) for r in res]
    zero1 = jnp.zeros((1,), F32)
    res = _adamw(parts[-1], _pack_small(w, _SMALL, zero1), _pack_small(m, _SMALL, zero1),
                 _pack_small(v, _SMALL, zero1), "adamw_small")
    small_res = [_unpack_small(r, _SMALL) for r in res]
    for n in _SMALL:
        out[n] = [s[0][n] for s in small_res]
    total_loss = small_res[0][1][0]
    return (total_loss, grad_x[None], *[out[n][0] for n in _NAMES], *[out[n][1] for n in _NAMES],
            *[out[n][2] for n in _NAMES], *[out[n][3] for n in _NAMES])
```

```python
import math

import numpy as np
import jax
import jax.numpy as jnp
from jax import lax
from jax.experimental import pallas as pl
from jax.experimental.pallas import tpu as pltpu

F32 = jnp.float32
_MXU = jnp.bfloat16

N_DEV = 8
N_META = 16
D_MODEL = 1024
ROW_TILE = 512
LANES = 128
CHUNK = 128
NEG = -1e30
V7X_VMEM_BYTES = 64 * 1024 * 1024
VMEM_LIMIT = V7X_VMEM_BYTES - 4 * 1024 * 1024

FOX_HEADS = 8
MLA_HEADS = 8
MLA_NOPE, MLA_ROPE, MLA_V = 128, 64, 128
MLA_Q_LORA, MLA_KV_LORA = 384, 256
MLA_DK = 256
RET_HEADS, RET_QK, RET_V = 4, 256, 512
ROPE_BASE = 10000.0
ALPHA = (2 * 4) ** 0.25
NORM_EPS = 1e-5

ADAM_LR, ADAM_B1, ADAM_B2, ADAM_EPS, ADAM_WD, ADAM_STEP = 0.001, 0.9, 0.999, 1e-08, 0.01, 10


def _params(sem, vmem=VMEM_LIMIT):
    return pltpu.CompilerParams(dimension_semantics=sem, vmem_limit_bytes=vmem)


def _tile(n, pref):
    if n <= pref:
        return n
    t = (pref // LANES) * LANES
    while n % t:
        t -= LANES
    return t


_MM_TILES = {"nn": (512, 4224, 2048), "nt": (512, 2048, 6144), "tn": (1024, 2048, 1536)}


def _matmul(a, b, mode, out_dtype, name, side=None):
    if mode == "nn":
        (M, K), (K2, N) = a.shape, b.shape
    elif mode == "nt":
        (M, K), (N, K2) = a.shape, b.shape
    else:
        (K, M), (K2, N) = a.shape, b.shape
    assert K == K2, (a.shape, b.shape, mode)
    tm, tn, tk = (_tile(d, p) for d, p in zip((M, N, K), _MM_TILES[mode]))
    nk = K // tk
    if mode == "nn":
        a_spec = pl.BlockSpec((tm, tk), lambda i, j, k: (i, k))
        b_spec = pl.BlockSpec((tk, tn), lambda i, j, k: (k, j))
        dn = (((1,), (0,)), ((), ()))
    elif mode == "nt":
        a_spec = pl.BlockSpec((tm, tk), lambda i, j, k: (i, k))
        b_spec = pl.BlockSpec((tn, tk), lambda i, j, k: (j, k))
        dn = (((1,), (1,)), ((), ()))
    else:
        a_spec = pl.BlockSpec((tk, tm), lambda i, j, k: (k, i))
        b_spec = pl.BlockSpec((tk, tn), lambda i, j, k: (k, j))
        dn = (((0,), (0,)), ((), ()))

    ns = len(side[0]) if side else 0
    grid = (M // tm, N // tn, nk)
    n_acc = 1 if nk > 1 else 0

    def body(*refs):
        a_ref, b_ref = refs[:2]
        o_ref = refs[2 + ns]
        k = pl.program_id(2)
        if side:
            i, j = pl.program_id(0), pl.program_id(1)
            _side_exchange(side, refs[2:2 + ns], refs[3 + ns:3 + 2 * ns], refs[3 + 2 * ns + n_acc:],
                           (i == 0) & (j == 0) & (k == 0),
                           (i == grid[0] - 1) & (j == grid[1] - 1) & (k == nk - 1))
        part = lax.dot_general(a_ref[...].astype(_MXU), b_ref[...].astype(_MXU), dn,
                               preferred_element_type=F32)
        if nk == 1:
            o_ref[...] = part.astype(out_dtype)
            return
        acc_ref = refs[3 + 2 * ns]

        @pl.when(k == 0)
        def _():
            acc_ref[...] = part

        @pl.when(k > 0)
        def _():
            acc_ref[...] += part

        @pl.when(k == nk - 1)
        def _():
            o_ref[...] = acc_ref[...].astype(out_dtype)

    hbm = pl.BlockSpec(memory_space=pl.ANY)
    side_shapes, side_sems = _exchange_shapes(*side) if side else ((), [])
    out, *side_out = pl.pallas_call(
        body, name=name,
        out_shape=(jax.ShapeDtypeStruct((M, N), out_dtype),) + side_shapes,
        grid=grid,
        in_specs=[a_spec, b_spec] + [hbm] * ns,
        out_specs=(pl.BlockSpec((tm, tn), lambda i, j, k: (i, j)),) + (hbm,) * ns,
        scratch_shapes=[pltpu.VMEM((tm, tn), F32)] * n_acc + side_sems,
        compiler_params=_params(("arbitrary",) * 3 if side else ("parallel", "parallel", "arbitrary")),
    )(a, b, *(side[0] if side else ()))
    return (out, side_out) if side else out


def _rows(w, off=0):
    return pl.BlockSpec((ROW_TILE, w), lambda i: (i + off, 0))


def _whole(shape):
    return pl.BlockSpec(shape, lambda i: (0,) * len(shape))


def _silu(z):
    return z * jax.nn.sigmoid(z)


def _dsilu(z):
    s = jax.nn.sigmoid(z)
    return s * (1.0 + z * (1.0 - s))


def _ln_fwd(h, y, g, b, name):
    L, D = h.shape

    def body(h_ref, y_ref, g_ref, b_ref, z_ref, o_ref):
        z = ALPHA * h_ref[...] + y_ref[...]
        mu = jnp.mean(z, axis=-1, keepdims=True)
        zc = z - mu
        var = jnp.mean(zc * zc, axis=-1, keepdims=True)
        z_ref[...] = z
        o_ref[...] = zc * lax.rsqrt(var + NORM_EPS) * g_ref[...] + b_ref[...]

    return pl.pallas_call(
        body, name=name,
        out_shape=(jax.ShapeDtypeStruct((L, D), F32), jax.ShapeDtypeStruct((L, D), F32)),
        grid=(L // ROW_TILE,),
        in_specs=[_rows(D), _rows(D), _whole((1, D)), _whole((1, D))],
        out_specs=(_rows(D), _rows(D)),
        compiler_params=_params(("parallel",)),
    )(h, y, g, b)


def _ln_bwd(zl, g, ga, gb, name):
    L, D = zl.shape
    two = gb is not None

    def body(*refs):
        if two:
            z_ref, g_ref, ga_ref, gb_ref, dz_ref, dg_ref, db_ref = refs
            gout = ALPHA * ga_ref[...] + gb_ref[...]
        else:
            z_ref, g_ref, ga_ref, dz_ref, dg_ref, db_ref = refs
            gout = ga_ref[...]
        z = z_ref[...]
        mu = jnp.mean(z, axis=-1, keepdims=True)
        zc = z - mu
        var = jnp.mean(zc * zc, axis=-1, keepdims=True)
        rstd = lax.rsqrt(var + NORM_EPS)
        xhat = zc * rstd
        dxh = gout * g_ref[...]
        m1 = jnp.mean(dxh, axis=-1, keepdims=True)
        m2 = jnp.mean(dxh * xhat, axis=-1, keepdims=True)
        dz_ref[...] = rstd * (dxh - m1 - xhat * m2)

        @pl.when(pl.program_id(0) == 0)
        def _():
            dg_ref[...] = jnp.zeros_like(dg_ref)
            db_ref[...] = jnp.zeros_like(db_ref)

        dg_ref[...] += jnp.sum(gout * xhat, axis=0, keepdims=True)
        db_ref[...] += jnp.sum(gout, axis=0, keepdims=True)

    ins = [zl, g, ga] + ([gb] if two else [])
    return pl.pallas_call(
        body, name=name,
        out_shape=(jax.ShapeDtypeStruct((L, D), F32), jax.ShapeDtypeStruct((1, D), F32),
                   jax.ShapeDtypeStruct((1, D), F32)),
        grid=(L // ROW_TILE,),
        in_specs=[_rows(D), _whole((1, D)), _rows(D)] + ([_rows(D)] if two else []),
        out_specs=(_rows(D), _whole((1, D)), _whole((1, D))),
        compiler_params=_params(("arbitrary",)),
    )(*ins)


def _loss_head(h, target, first):
    L, D = h.shape

    def body(h_ref, t_ref, dh_ref, loss_ref):
        i = pl.program_id(0)

        @pl.when(i == 0)
        def _():
            loss_ref[...] = jnp.zeros_like(loss_ref)

        @pl.when(i < first)
        def _():
            dh_ref[...] = jnp.zeros_like(dh_ref)

        @pl.when(i >= first)
        def _():
            err = h_ref[...] - t_ref[...]
            dh_ref[...] = err * (1.0 / D)
            part = jnp.sum(jnp.sum(err * err, axis=-1, keepdims=True) * (1.0 / D), axis=0, keepdims=True)
            loss_ref[...] += 0.5 * part

    return pl.pallas_call(
        body, name="loss_head",
        out_shape=(jax.ShapeDtypeStruct((L, D), F32), jax.ShapeDtypeStruct((1, 1), F32)),
        grid=(L // ROW_TILE,),
        in_specs=[_rows(D), pl.BlockSpec((ROW_TILE, D), lambda i: (jnp.maximum(i - first, 0), 0))],
        out_specs=(_rows(D), _whole((1, 1))),
        compiler_params=_params(("arbitrary",)),
    )(h, target)


def _input_grads(dzl, dmix, first, n_real):
    L, D = dzl.shape

    def body(a_ref, b_ref, gx_ref, gm_ref):
        i = pl.program_id(0)
        val = ALPHA * a_ref[...] + b_ref[...]

        @pl.when(i == first - 1)
        def _():
            gm_ref[...] = val[ROW_TILE - N_META:, :]

        @pl.when(i >= first)
        def _():
            gx_ref[...] = val

    return pl.pallas_call(
        body, name="input_grads",
        out_shape=(jax.ShapeDtypeStruct((n_real, D), F32), jax.ShapeDtypeStruct((N_META, D), F32)),
        grid=(L // ROW_TILE,),
        in_specs=[_rows(D), _rows(D)],
        out_specs=(pl.BlockSpec((ROW_TILE, D), lambda i: (jnp.maximum(i - first, 0), 0)),
                   _whole((N_META, D))),
        compiler_params=_params(("arbitrary",)),
    )(dzl, dmix)


def _gate_bwd(dy, o, zsrc, heads, name):
    L, W = o.shape
    hd = W // heads

    def body(dy_ref, o_ref, z_ref, do_ref, dl_ref, dz_ref):
        z = z_ref[...]
        dy = dy_ref[...]
        o = o_ref[...]
        do = dy * _silu(z)
        do_ref[...] = do.astype(_MXU)
        dz_ref[...] = (dy * o * _dsilu(z)).astype(_MXU)
        prod = do * o
        for h in range(heads):
            dl_ref[h] = _as_row(jnp.sum(prod[:, h * hd:(h + 1) * hd], axis=-1, keepdims=True))

    return pl.pallas_call(
        body, name=name,
        out_shape=(jax.ShapeDtypeStruct((L, W), _MXU), jax.ShapeDtypeStruct((heads, L // ROW_TILE, 1, ROW_TILE), F32),
                   jax.ShapeDtypeStruct((L, W), _MXU)),
        grid=(L // ROW_TILE,),
        in_specs=[_rows(W), _rows(W), _rows(W)],
        out_specs=(_rows(W), pl.BlockSpec((heads, None, 1, ROW_TILE), lambda i: (0, i, 0, 0)), _rows(W)),
        compiler_params=_params(("parallel",)),
    )(dy, o, zsrc)


def _split3(x):
    hi = x.astype(_MXU)
    r1 = x - hi.astype(F32)
    mid = r1.astype(_MXU)
    lo = (r1 - mid.astype(F32)).astype(_MXU)
    return hi, mid, lo


def _tri_cumsum(x, reverse):
    T = x.shape[0]
    r = lax.broadcasted_iota(jnp.int32, (T, T), 0)
    c = lax.broadcasted_iota(jnp.int32, (T, T), 1)
    tri = jnp.where((c >= r) if reverse else (c <= r), 1.0, 0.0).astype(_MXU)
    out = jnp.zeros(x.shape, F32)
    for part in _split3(x):
        out = out + jnp.dot(tri, part, preferred_element_type=F32)
    return out


def _fox_cum(zf, bias, pad, name):
    L, Wz = zf.shape
    off = Wz // LANES - 1

    def body(f_ref, b_ref, c_ref, carry_ref):
        i = pl.program_id(0)

        @pl.when(i == 0)
        def _():
            carry_ref[...] = jnp.zeros_like(carry_ref)

        x = f_ref[...] + b_ref[...]
        logf = jnp.minimum(x, 0.0) - jnp.log(1.0 + jnp.exp(-jnp.abs(x)))
        row = i * ROW_TILE + lax.broadcasted_iota(jnp.int32, logf.shape, 0)
        logf = jnp.where(row >= pad, logf, 0.0)
        cum = _tri_cumsum(logf, False) + carry_ref[...]
        c_ref[...] = cum
        carry_ref[...] = cum[ROW_TILE - 1:, :]

    return pl.pallas_call(
        body, name=name,
        out_shape=jax.ShapeDtypeStruct((L, LANES), F32),
        grid=(L // ROW_TILE,),
        in_specs=[pl.BlockSpec((ROW_TILE, LANES), lambda i: (i, off)), _whole((1, LANES))],
        out_specs=_rows(LANES),
        scratch_shapes=[pltpu.VMEM((1, LANES), F32)],
        compiler_params=_params(("arbitrary",)),
    )(zf, bias)


def _fox_cum_bwd(dcum, zf, bias, pad, name):
    L, Wz = zf.shape
    off = Wz // LANES - 1
    n = L // ROW_TILE

    def body(d_ref, f_ref, b_ref, df_ref, db_ref, carry_ref):
        i = pl.program_id(0)

        @pl.when(i == 0)
        def _():
            carry_ref[...] = jnp.zeros_like(carry_ref)
            db_ref[...] = jnp.zeros_like(db_ref)

        rc = _tri_cumsum(d_ref[...], True) + carry_ref[...]
        carry_ref[...] = rc[:1, :]
        x = f_ref[...] + b_ref[...]
        row = (n - 1 - i) * ROW_TILE + lax.broadcasted_iota(jnp.int32, x.shape, 0)
        df = jnp.where(row >= pad, rc * jax.nn.sigmoid(-x), 0.0)
        df_ref[...] = df.astype(_MXU)
        db_ref[...] += jnp.sum(df, axis=0, keepdims=True)

    return pl.pallas_call(
        body, name=name,
        out_shape=(jax.ShapeDtypeStruct((L, LANES), _MXU), jax.ShapeDtypeStruct((1, LANES), F32)),
        grid=(n,),
        in_specs=[pl.BlockSpec((ROW_TILE, LANES), lambda i: (n - 1 - i, 0)),
                  pl.BlockSpec((ROW_TILE, LANES), lambda i: (n - 1 - i, off)), _whole((1, LANES))],
        out_specs=(pl.BlockSpec((ROW_TILE, LANES), lambda i: (n - 1 - i, 0)), _whole((1, LANES))),
        scratch_shapes=[pltpu.VMEM((1, LANES), F32)],
        compiler_params=_params(("arbitrary",)),
    )(dcum, zf, bias)


LOG2E = 1.4426950408889634
UNROLL_SIZES = (8, 4, 2)


def _bias_terms(ct, pad):
    H, L = ct.shape
    T = ROW_TILE
    n = L // T
    c2 = ct * LOG2E
    k2 = jnp.where(jnp.arange(L)[None, :] < pad, -NEG, c2)
    return dict(cq_row=c2.reshape(H, n, 1, T), ck_row=k2.reshape(H, n, 1, T), ref=c2[:, ::T].reshape(H, n, 1, 1))


def _exchange_copies(srcs, outs, send_sems, recv_sems, local_sems, scatter):
    nt = len(srcs)
    x, y, c = lax.axis_index("x"), lax.axis_index("y"), lax.axis_index("c")
    me = 4 * x + 2 * y + c
    copies = [pltpu.make_async_copy(srcs[t].at[me] if scatter else srcs[t], outs[t].at[me], local_sems.at[t])
              for t in range(nt)]
    for k in range(1, N_DEV):
        px = (x + (k >> 2)) % 2
        py = (y + ((k >> 1) & 1)) % 2
        pc = (c + (k & 1)) % 2
        peer = 4 * px + 2 * py + pc
        for t in range(nt):
            copies.append(pltpu.make_async_remote_copy(
                src_ref=srcs[t].at[peer] if scatter else srcs[t],
                dst_ref=outs[t].at[me],
                send_sem=send_sems.at[k - 1, t], recv_sem=recv_sems.at[k - 1, t],
                device_id=(px, py, pc), device_id_type=pl.DeviceIdType.MESH))
    return copies


def _exchange_shapes(tensors, scatter):
    out_shape = tuple(jax.ShapeDtypeStruct(t.shape if scatter else (N_DEV,) + t.shape, t.dtype) for t in tensors)
    nt = len(tensors)
    sems = [pltpu.SemaphoreType.DMA((N_DEV - 1, nt)), pltpu.SemaphoreType.DMA((N_DEV - 1, nt)),
            pltpu.SemaphoreType.DMA((nt,))]
    return out_shape, sems


def _side_exchange(side, srcs, outs, sems, first, last):
    if not side:
        return

    @pl.when(first)
    def _():
        for cp in _exchange_copies(srcs, outs, *sems, side[1]):
            cp.start()

    @pl.when(last)
    def _():
        for cp in _exchange_copies(srcs, outs, *sems, side[1]):
            cp.wait()


def _as_row(col):
    return jnp.transpose(jnp.broadcast_to(col, (col.shape[0], LANES)))[0:1, :]


def _as_col(row):
    return jnp.transpose(jnp.broadcast_to(row, (LANES, row.shape[1])))[:, 0:1]


def _attn_fwd(q, qoff, k, koff, v, voff, zsrc, bias, heads, dk, dv, scale, name, dead, side=None):
    L = q.shape[0]
    T = ROW_TILE
    n = L // T
    c = scale * LOG2E
    dn_qk = (((1,), (1,)), ((), ()))

    ns = len(side[0]) if side else 0

    def body(*refs):
        q_ref, k_ref, v_ref, z_ref, ck_ref, ref_ref = refs[:6]
        o_ref, lse_ref, y_ref = refs[6 + ns:9 + ns]
        s_a, s_b = refs[9 + 2 * ns:11 + 2 * ns]
        i = pl.program_id(1)
        _side_exchange(side, refs[6:6 + ns], refs[9 + ns:9 + 2 * ns], refs[11 + 2 * ns:],
                       (pl.program_id(0) == 0) & (i == 0), (pl.program_id(0) == heads - 1) & (i == n - 1))
        qb = q_ref[...]
        ref = ref_ref[...]

        def scores(j, dst):
            start = pl.multiple_of(j * T, T)
            dst[...] = lax.dot_general(qb, k_ref[pl.ds(start, T), :], dn_qk, preferred_element_type=F32)

        def update(t, vj, carry):
            m, l, acc = carry
            m_new = jnp.maximum(m, jnp.max(t, axis=1, keepdims=True))
            p = jnp.exp2(t - m_new)
            a = jnp.exp2(m - m_new)
            l = a * l + jnp.sum(p, axis=1, keepdims=True)
            acc = a * acc + jnp.dot(p.astype(_MXU), vj, preferred_element_type=F32)
            return m_new, l, acc

        def soft(j, carry, cur, diag):
            start = pl.multiple_of(j * T, T)
            t = cur[...] * c - (ck_ref[j] - ref)
            if diag:
                r = lax.broadcasted_iota(jnp.int32, (T, T), 0)
                cc = lax.broadcasted_iota(jnp.int32, (T, T), 1)
                t = jnp.where(cc <= r, t, NEG)
            return update(t, v_ref[pl.ds(start, T), :], carry)

        def head_tile(carry):
            t = lax.dot_general(qb, k_ref[dead:T, :], dn_qk, preferred_element_type=F32) * c
            return update(t - (ck_ref[0][:, dead:] - ref), v_ref[dead:T, :], carry)

        def step(j, carry, cur, nxt):
            scores(j + 1, nxt)
            return soft(j, carry, cur, False)

        first = jnp.where(i > 0, 1, 0) if dead else 0

        def run_of(j0, steps, carry):
            for u in range(0, steps, 2):
                carry = step(j0 + u + 1, step(j0 + u, carry, s_a, s_b), s_b, s_a)
            return carry

        scores(first, s_a)
        carry = (jnp.full((T, 1), -jnp.inf, F32), jnp.zeros((T, 1), F32), jnp.zeros((T, dv), F32))
        if dead:
            carry = lax.cond(i > 0, head_tile, lambda cy: cy, carry)
        done = first
        for size in UNROLL_SIZES:
            trips = (i - done) // size
            carry = lax.fori_loop(0, trips, lambda g, cy, d=done, s=size: run_of(d + s * g, s, cy), carry)
            done = done + size * trips
        odd = (i - done) == 1
        carry = lax.cond(odd, lambda cy: step(i - 1, cy, s_a, s_b), lambda cy: cy, carry)
        m, l, acc = lax.cond(odd, lambda cy: soft(i, cy, s_b, True), lambda cy: soft(i, cy, s_a, True), carry)
        o = acc / l
        o_ref[...] = o
        y_ref[...] = (o * _silu(z_ref[...])).astype(_MXU)
        lse_ref[...] = _as_row(m + jnp.log2(l))

    hbm = pl.BlockSpec(memory_space=pl.ANY)
    side_shapes, side_sems = _exchange_shapes(*side) if side else ((), [])
    o, lse, y, *side_out = pl.pallas_call(
        body, name=name,
        out_shape=(jax.ShapeDtypeStruct((L, heads * dv), F32), jax.ShapeDtypeStruct((heads, n, 1, T), F32),
                   jax.ShapeDtypeStruct((L, heads * dv), _MXU)) + side_shapes,
        grid=(heads, n),
        in_specs=[pl.BlockSpec((T, dk), lambda h, i: (i, qoff + h)),
                  pl.BlockSpec((L, dk), lambda h, i: (0, koff + h)),
                  pl.BlockSpec((L, dv), lambda h, i: (0, voff + h)),
                  pl.BlockSpec((T, dv), lambda h, i: (i, h)),
                  pl.BlockSpec((None, n, 1, T), lambda h, i: (h, 0, 0, 0)),
                  pl.BlockSpec((None, None, 1, 1), lambda h, i: (h, i, 0, 0))] + [hbm] * ns,
        out_specs=(pl.BlockSpec((T, dv), lambda h, i: (i, h)),
                   pl.BlockSpec((None, None, 1, T), lambda h, i: (h, i, 0, 0)),
                   pl.BlockSpec((T, dv), lambda h, i: (i, h))) + (hbm,) * ns,
        scratch_shapes=[pltpu.VMEM((T, T), F32), pltpu.VMEM((T, T), F32)] + side_sems,
        compiler_params=_params(("arbitrary", "arbitrary")),
    )(q, k, v, zsrc, bias["ck_row"], bias["ref"], *(side[0] if side else ()))
    return o, lse + (bias["cq_row"] - bias["ref"]), y, side_out


def _attn_bwd(q, qoff, k, koff, v, voff, do, bias, lse_row, delta_row, heads, dk, dv, scale, name, dead,
              side=None, bias_grads=True):
    L = q.shape[0]
    T = ROW_TILE
    n = L // T
    c = scale * LOG2E
    dn_nt = (((1,), (1,)), ((), ()))
    dn_tn = (((0,), (0,)), ((), ()))
    ns = len(side[0]) if side else 0

    def body(*refs):
        q_ref, k_ref, v_ref, do_ref, cq_ref, ck_ref, ref_ref, lse_ref, dl_ref = refs[:9]
        dq_hbm, dk_ref, dv_ref, dck_ref, dcq_ref = refs[9 + ns:14 + ns]
        dq_acc, sem, st_a, dp_a, st_b, dp_b = refs[14 + 2 * ns:20 + 2 * ns]
        h = pl.program_id(0)
        j = pl.program_id(1)
        _side_exchange(side, refs[9:9 + ns], refs[14 + ns:14 + 2 * ns], refs[20 + 2 * ns:],
                       (h == 0) & (j == 0), (h == heads - 1) & (j == n - 1))

        @pl.when(j == 0)
        def _():
            dq_acc[...] = jnp.zeros_like(dq_acc)
            dcq_ref[...] = jnp.zeros_like(dcq_ref)

        ref = ref_ref[...]

        def run(k0, R):
            kb = k_ref[k0:k0 + R, :]
            vb = v_ref[k0:k0 + R, :]
            bcol = _as_col(ck_ref[j][:, k0:k0 + R] - ref)

            def front(i, st_dst, dp_dst):
                start = pl.multiple_of(jnp.minimum(i, n - 1) * T, T)
                st_dst[0:R, :] = lax.dot_general(kb, q_ref[pl.ds(start, T), :], dn_nt, preferred_element_type=F32)
                dp_dst[0:R, :] = lax.dot_general(vb, do_ref[pl.ds(start, T), :], dn_nt, preferred_element_type=F32)

            def back(i, carry, st_cur, dp_cur, diag):
                dk_a, dv_a, dck_a = carry
                start = pl.multiple_of(i * T, T)
                qi = q_ref[pl.ds(start, T), :]
                doi = do_ref[pl.ds(start, T), :]
                arow = (cq_ref[i] - ref) - lse_ref[i]
                st = st_cur[0:R, :] * c + arow - bcol
                if diag:
                    r = lax.broadcasted_iota(jnp.int32, (R, T), 0) + k0
                    cc = lax.broadcasted_iota(jnp.int32, (R, T), 1)
                    st = jnp.where(r <= cc, st, NEG)
                pt = jnp.exp2(st)
                dv_a = dv_a + jnp.dot(pt.astype(_MXU), doi, preferred_element_type=F32)
                dst = pt * (dp_cur[0:R, :] - dl_ref[i])
                if bias_grads:
                    dck_a = dck_a - jnp.sum(dst, axis=1, keepdims=True)
                    dcq_ref[i] += jnp.sum(dst, axis=0, keepdims=True)
                dsb = (dst * scale).astype(_MXU)
                dk_a = dk_a + jnp.dot(dsb, qi, preferred_element_type=F32)
                dq_acc[pl.ds(start, T), :] += lax.dot_general(dsb, kb, dn_tn, preferred_element_type=F32)
                return dk_a, dv_a, dck_a

            buf_a, buf_b = (st_a, dp_a), (st_b, dp_b)

            def step(i, carry, cur, nxt, diag=False):
                front(i + 1, *nxt)
                return back(i, carry, *cur, diag)

            def run_of(i0, steps, carry):
                for u in range(0, steps, 2):
                    carry = step(i0 + u + 1, step(i0 + u, carry, buf_b, buf_a), buf_a, buf_b)
                return carry

            front(j, *buf_a)
            init = (jnp.zeros((R, dk), F32), jnp.zeros((R, dv), F32), jnp.zeros((R, 1), F32))
            carry = step(j, init, buf_a, buf_b, True)
            done = j + 1
            for size in UNROLL_SIZES:
                trips = (n - done) // size
                carry = lax.fori_loop(0, trips, lambda g, cy, d=done, s=size: run_of(d + s * g, s, cy), carry)
                done = done + size * trips
            dk_a, dv_a, dck_a = lax.cond((n - done) == 1, lambda cy: step(n - 1, cy, buf_b, buf_a),
                                         lambda cy: cy, carry)
            if k0:
                dk_ref[0:k0, :] = jnp.zeros((k0, dk), F32)
                dv_ref[0:k0, :] = jnp.zeros((k0, dv), F32)
                dck_ref[:, 0:k0] = jnp.zeros((1, k0), F32)
            dk_ref[k0:k0 + R, :] = dk_a
            dv_ref[k0:k0 + R, :] = dv_a
            dck_ref[:, k0:k0 + R] = _as_row(dck_a)

        if dead:
            pl.when(j == 0)(lambda: run(dead, T - dead))
            pl.when(j > 0)(lambda: run(0, T))
        else:
            run(0, T)

        @pl.when(j == n - 1)
        def _():
            cp = pltpu.make_async_copy(dq_acc, dq_hbm.at[:, pl.ds(pl.multiple_of(h * dk, dk), dk)], sem)
            cp.start()
            cp.wait()

    hbm = pl.BlockSpec(memory_space=pl.ANY)
    side_shapes, side_sems = _exchange_shapes(*side) if side else ((), [])
    dq, dk_, dv_, dck, dcq, *side_out = pl.pallas_call(
        body, name=name,
        out_shape=(jax.ShapeDtypeStruct((L, heads * dk), F32), jax.ShapeDtypeStruct((L, heads * dk), F32),
                   jax.ShapeDtypeStruct((L, heads * dv), F32), jax.ShapeDtypeStruct((heads, n, 1, T), F32),
                   jax.ShapeDtypeStruct((heads, n, 1, T), F32)) + side_shapes,
        grid=(heads, n),
        in_specs=[pl.BlockSpec((L, dk), lambda h, j: (0, qoff + h)),
                  pl.BlockSpec((T, dk), lambda h, j: (j, koff + h)),
                  pl.BlockSpec((T, dv), lambda h, j: (j, voff + h)),
                  pl.BlockSpec((L, dv), lambda h, j: (0, h)),
                  pl.BlockSpec((None, n, 1, T), lambda h, j: (h, 0, 0, 0)),
                  pl.BlockSpec((None, n, 1, T), lambda h, j: (h, 0, 0, 0)),
                  pl.BlockSpec((None, None, 1, 1), lambda h, j: (h, j, 0, 0)),
                  pl.BlockSpec((None, n, 1, T), lambda h, j: (h, 0, 0, 0)),
                  pl.BlockSpec((None, n, 1, T), lambda h, j: (h, 0, 0, 0))] + [hbm] * ns,
        out_specs=(hbm,
                   pl.BlockSpec((T, dk), lambda h, j: (j, h)),
                   pl.BlockSpec((T, dv), lambda h, j: (j, h)),
                   pl.BlockSpec((None, None, 1, T), lambda h, j: (h, j, 0, 0)),
                   pl.BlockSpec((None, n, 1, T), lambda h, j: (h, 0, 0, 0))) + (hbm,) * ns,
        scratch_shapes=[pltpu.VMEM((L, dk), F32), pltpu.SemaphoreType.DMA] + [pltpu.VMEM((T, T), F32)] * 4
        + side_sems,
        compiler_params=_params(("arbitrary", "arbitrary")),
    )(q, k, v, do, bias["cq_row"], bias["ck_row"], bias["ref"], lse_row, delta_row, *(side[0] if side else ()))
    return dq, dk_, dv_, dck, dcq, side_out


_CQ0, _CKV0, _KR0 = D_MODEL, D_MODEL + MLA_Q_LORA, D_MODEL + MLA_Q_LORA + MLA_KV_LORA
_MLA_PROJ = _KR0 + LANES


def _rms(x, g):
    ms = jnp.mean(x * x, axis=-1, keepdims=True)
    return x * lax.rsqrt(ms + NORM_EPS) * g


def _rms_bwd(x, g, dy):
    ms = jnp.mean(x * x, axis=-1, keepdims=True)
    r = lax.rsqrt(ms + NORM_EPS)
    xh = x * r
    dxh = dy * g
    dx = r * (dxh - xh * jnp.mean(dxh * xh, axis=-1, keepdims=True))
    return dx, jnp.sum(dy * xh, axis=0, keepdims=True)


def _mla_norm_fwd(proj, gq, gkv):
    L = proj.shape[0]

    def body(p_ref, gq_ref, gkv_ref, cq_ref, ckv_ref):
        cq_ref[...] = _rms(p_ref[:, _CQ0:_CKV0], gq_ref[...]).astype(_MXU)
        ckv_ref[...] = _rms(p_ref[:, _CKV0:_KR0], gkv_ref[...]).astype(_MXU)

    return pl.pallas_call(
        body, name="mla_norm_fwd",
        out_shape=(jax.ShapeDtypeStruct((L, MLA_Q_LORA), _MXU), jax.ShapeDtypeStruct((L, MLA_KV_LORA), _MXU)),
        grid=(L // ROW_TILE,),
        in_specs=[_rows(_MLA_PROJ), _whole((1, MLA_Q_LORA)), _whole((1, MLA_KV_LORA))],
        out_specs=(_rows(MLA_Q_LORA), _rows(MLA_KV_LORA)),
        compiler_params=_params(("parallel",)),
    )(proj, gq, gkv)


def _mla_norm_bwd(proj, gq, gkv, dcqn, dckvn):
    L = proj.shape[0]

    def body(p_ref, gq_ref, gkv_ref, dq_ref, dkv_ref, dcq_ref, dckv_ref, dgq_ref, dgkv_ref):
        @pl.when(pl.program_id(0) == 0)
        def _():
            dgq_ref[...] = jnp.zeros_like(dgq_ref)
            dgkv_ref[...] = jnp.zeros_like(dgkv_ref)

        dx, dg = _rms_bwd(p_ref[:, _CQ0:_CKV0], gq_ref[...], dq_ref[...])
        dcq_ref[...] = dx.astype(_MXU)
        dgq_ref[...] += dg
        dx, dg = _rms_bwd(p_ref[:, _CKV0:_KR0], gkv_ref[...], dkv_ref[...])
        dckv_ref[...] = dx.astype(_MXU)
        dgkv_ref[...] += dg

    return pl.pallas_call(
        body, name="mla_norm_bwd",
        out_shape=(jax.ShapeDtypeStruct((L, MLA_Q_LORA), _MXU), jax.ShapeDtypeStruct((L, MLA_KV_LORA), _MXU),
                   jax.ShapeDtypeStruct((1, MLA_Q_LORA), F32), jax.ShapeDtypeStruct((1, MLA_KV_LORA), F32)),
        grid=(L // ROW_TILE,),
        in_specs=[_rows(_MLA_PROJ), _whole((1, MLA_Q_LORA)), _whole((1, MLA_KV_LORA)),
                  _rows(MLA_Q_LORA), _rows(MLA_KV_LORA)],
        out_specs=(_rows(MLA_Q_LORA), _rows(MLA_KV_LORA), _whole((1, MLA_Q_LORA)), _whole((1, MLA_KV_LORA))),
        compiler_params=_params(("arbitrary",)),
    )(proj, gq, gkv, dcqn, dckvn)


def _rot_tile(t, cos, sa, sb):
    half = MLA_ROPE // 2
    return t * cos + pltpu.roll(t, LANES - half, 1) * sa + pltpu.roll(t, half, 1) * sb


def _mla_rope_fwd(qf, kv, proj, cos, sa, sb):
    L = qf.shape[0]
    H = MLA_HEADS

    def body(q_ref, kv_ref, p_ref, c_ref, sa_ref, sb_ref, qo_ref, ko_ref):
        cos, sa, sb = c_ref[...], sa_ref[...], sb_ref[...]
        kr = _rot_tile(p_ref[...], cos, sa, sb).astype(_MXU)
        for h in range(H):
            b = h * MLA_DK
            qo_ref[:, b:b + LANES] = q_ref[:, b:b + LANES].astype(_MXU)
            qo_ref[:, b + LANES:b + 2 * LANES] = _rot_tile(q_ref[:, b + LANES:b + 2 * LANES], cos, sa, sb).astype(_MXU)
            ko_ref[:, b:b + LANES] = kv_ref[:, h * LANES:(h + 1) * LANES]
            ko_ref[:, b + LANES:b + 2 * LANES] = kr

    W = H * MLA_DK
    return pl.pallas_call(
        body, name="mla_rope_fwd",
        out_shape=(jax.ShapeDtypeStruct((L, W), _MXU), jax.ShapeDtypeStruct((L, W), _MXU)),
        grid=(L // ROW_TILE,),
        in_specs=[_rows(W), _rows(kv.shape[1]), pl.BlockSpec((ROW_TILE, LANES), lambda i: (i, _KR0 // LANES)),
                  _rows(LANES), _rows(LANES), _rows(LANES)],
        out_specs=(_rows(W), _rows(W)),
        compiler_params=_params(("parallel",)),
    )(qf, kv, proj, cos, sa, sb)


def _mla_rope_bwd(dq_full, dk_full, cos, sa, sb):
    L = dq_full.shape[0]
    H = MLA_HEADS
    W = H * MLA_DK

    def body(dq_ref, dk_ref, c_ref, sa_ref, sb_ref, dqf_ref, dkn_ref, dkr_ref):
        cos, sa, sb = c_ref[...], sa_ref[...], sb_ref[...]
        lane = lax.broadcasted_iota(jnp.int32, (ROW_TILE, LANES), 1)
        live = lane < MLA_ROPE
        krs = jnp.zeros((ROW_TILE, LANES), F32)
        for h in range(H):
            b = h * MLA_DK
            dqf_ref[:, b:b + LANES] = dq_ref[:, b:b + LANES].astype(_MXU)
            dqr = _rot_tile(dq_ref[:, b + LANES:b + 2 * LANES], cos, sa, sb)
            dqf_ref[:, b + LANES:b + 2 * LANES] = jnp.where(live, dqr, 0.0).astype(_MXU)
            dkn_ref[:, h * LANES:(h + 1) * LANES] = dk_ref[:, b:b + LANES].astype(_MXU)
            krs = krs + dk_ref[:, b + LANES:b + 2 * LANES]
        dkr_ref[...] = jnp.where(live, _rot_tile(krs, cos, sa, sb), 0.0).astype(_MXU)

    return pl.pallas_call(
        body, name="mla_rope_bwd",
        out_shape=(jax.ShapeDtypeStruct((L, W), _MXU), jax.ShapeDtypeStruct((L, H * LANES), _MXU),
                   jax.ShapeDtypeStruct((L, LANES), _MXU)),
        grid=(L // ROW_TILE,),
        in_specs=[_rows(W), _rows(W), _rows(LANES), _rows(LANES), _rows(LANES)],
        out_specs=(_rows(W), _rows(H * LANES), _rows(LANES)),
        compiler_params=_params(("parallel",)),
    )(dq_full, dk_full, cos, sa, sb)


_RET_QKW = RET_HEADS * RET_QK


def _ret_rope_fwd(qkz, cos, sin, pad):
    L = qkz.shape[0]
    hh = RET_QK // 2
    kscale = RET_QK ** -0.5

    def body(p_ref, c_ref, s_ref, q_ref, k_ref):
        cos, sin = c_ref[...], s_ref[...]
        row = pl.program_id(0) * ROW_TILE + lax.broadcasted_iota(jnp.int32, (ROW_TILE, hh), 0)
        keep = row >= pad
        for h in range(RET_HEADS):
            b = h * RET_QK
            t1, t2 = p_ref[:, b:b + hh], p_ref[:, b + hh:b + 2 * hh]
            q_ref[:, b:b + hh] = (t1 * cos - t2 * sin).astype(_MXU)
            q_ref[:, b + hh:b + 2 * hh] = (t2 * cos + t1 * sin).astype(_MXU)
            t1, t2 = p_ref[:, _RET_QKW + b:_RET_QKW + b + hh], p_ref[:, _RET_QKW + b + hh:_RET_QKW + b + 2 * hh]
            k_ref[:, b:b + hh] = jnp.where(keep, (t1 * cos - t2 * sin) * kscale, 0.0)
            k_ref[:, b + hh:b + 2 * hh] = jnp.where(keep, (t2 * cos + t1 * sin) * kscale, 0.0)

    return pl.pallas_call(
        body, name="ret_rope_fwd",
        out_shape=(jax.ShapeDtypeStruct((L, _RET_QKW), _MXU), jax.ShapeDtypeStruct((L, _RET_QKW), F32)),
        grid=(L // ROW_TILE,),
        in_specs=[_rows(2 * _RET_QKW), _rows(hh), _rows(hh)],
        out_specs=(_rows(_RET_QKW), _rows(_RET_QKW)),
        compiler_params=_params(("parallel",)),
    )(qkz, cos, sin)


def _ret_rope_bwd(dqr, dkr, cos, sin, pad):
    L = dqr.shape[0]
    hh = RET_QK // 2
    kscale = RET_QK ** -0.5

    def body(dq_ref, dk_ref, c_ref, s_ref, oq_ref, ok_ref):
        cos, sin = c_ref[...], s_ref[...]
        row = pl.program_id(0) * ROW_TILE + lax.broadcasted_iota(jnp.int32, (ROW_TILE, hh), 0)
        keep = row >= pad
        for h in range(RET_HEADS):
            b = h * RET_QK
            d1, d2 = dq_ref[:, b:b + hh], dq_ref[:, b + hh:b + 2 * hh]
            oq_ref[:, b:b + hh] = (d1 * cos + d2 * sin).astype(_MXU)
            oq_ref[:, b + hh:b + 2 * hh] = (d2 * cos - d1 * sin).astype(_MXU)
            d1, d2 = dk_ref[:, b:b + hh], dk_ref[:, b + hh:b + 2 * hh]
            ok_ref[:, b:b + hh] = jnp.where(keep, (d1 * cos + d2 * sin) * kscale, 0.0).astype(_MXU)
            ok_ref[:, b + hh:b + 2 * hh] = jnp.where(keep, (d2 * cos - d1 * sin) * kscale, 0.0).astype(_MXU)

    return pl.pallas_call(
        body, name="ret_rope_bwd",
        out_shape=(jax.ShapeDtypeStruct((L, _RET_QKW), _MXU), jax.ShapeDtypeStruct((L, _RET_QKW), _MXU)),
        grid=(L // ROW_TILE,),
        in_specs=[_rows(_RET_QKW), _rows(_RET_QKW), _rows(hh), _rows(hh)],
        out_specs=(_rows(_RET_QKW), _rows(_RET_QKW)),
        compiler_params=_params(("parallel",)),
    )(dqr, dkr, cos, sin)


def _ret_rows(L):
    return next(r * ROW_TILE for r in (3, 2, 1) if L % (r * ROW_TILE) == 0)


def _ret_decays():
    lg = np.log1p(-np.exp2(-5.0 - np.arange(RET_HEADS, dtype=np.float32))).astype(np.float32)
    i = np.arange(CHUNK, dtype=np.float32)
    rel = i[:, None] - i[None, :]
    dm = np.where(rel[None] >= 0, np.exp(rel[None] * lg[:, None, None]), 0.0).astype(np.float32)
    qd = np.exp((i[None, :] + 1.0) * lg[:, None]).astype(np.float32)[:, :, None]
    kd = np.exp((CHUNK - 1.0 - i)[None, :] * lg[:, None]).astype(np.float32)[:, :, None]
    cd = np.broadcast_to(np.exp(CHUNK * lg).astype(np.float32)[:, None, None], (RET_HEADS, 1, RET_V))
    return jnp.asarray(dm), jnp.asarray(qd), jnp.asarray(kd), jnp.asarray(np.ascontiguousarray(cd))


def _ret_fwd(qr, kr, v):
    L = qr.shape[0]
    nc = L // CHUNK
    RB = _ret_rows(L)
    G = RB // CHUNK
    dm, qd, kd, cd = _ret_decays()
    dn_nt = (((1,), (1,)), ((), ()))
    dn_tn = (((0,), (0,)), ((), ()))

    def body(q_ref, k_ref, v_ref, dm_ref, qd_ref, kd_ref, cd_ref, o_ref, st_ref, s_ref):
        @pl.when(pl.program_id(1) == 0)
        def _():
            s_ref[...] = jnp.zeros_like(s_ref)

        state = s_ref[...]
        for g in range(G):
            rows = slice(g * CHUNK, (g + 1) * CHUNK)
            q, k, vv = q_ref[rows, :], k_ref[rows, :], v_ref[rows, :]
            sb = state.astype(_MXU)
            st_ref[g] = sb
            s = lax.dot_general(q, k.astype(_MXU), dn_nt, preferred_element_type=F32) * dm_ref[...]
            intra = jnp.dot(s.astype(_MXU), vv, preferred_element_type=F32)
            cross = jnp.dot(q, sb, preferred_element_type=F32) * qd_ref[...]
            o_ref[rows, :] = intra + cross
            kdm = (k * kd_ref[...]).astype(_MXU)
            state = cd_ref[...] * state + lax.dot_general(kdm, vv, dn_tn, preferred_element_type=F32)
        s_ref[...] = state

    return pl.pallas_call(
        body, name="ret_fwd",
        out_shape=(jax.ShapeDtypeStruct((L, RET_HEADS * RET_V), F32),
                   jax.ShapeDtypeStruct((RET_HEADS, nc, RET_QK, RET_V), _MXU)),
        grid=(RET_HEADS, nc // G),
        in_specs=[pl.BlockSpec((RB, RET_QK), lambda h, c: (c, h)),
                  pl.BlockSpec((RB, RET_QK), lambda h, c: (c, h)),
                  pl.BlockSpec((RB, RET_V), lambda h, c: (c, h)),
                  pl.BlockSpec((None, CHUNK, CHUNK), lambda h, c: (h, 0, 0)),
                  pl.BlockSpec((None, CHUNK, 1), lambda h, c: (h, 0, 0)),
                  pl.BlockSpec((None, CHUNK, 1), lambda h, c: (h, 0, 0)),
                  pl.BlockSpec((None, 1, RET_V), lambda h, c: (h, 0, 0))],
        out_specs=(pl.BlockSpec((RB, RET_V), lambda h, c: (c, h)),
                   pl.BlockSpec((None, G, RET_QK, RET_V), lambda h, c: (h, c, 0, 0))),
        scratch_shapes=[pltpu.VMEM((RET_QK, RET_V), F32)],
        compiler_params=_params(("parallel", "arbitrary")),
    )(qr, kr, v, dm, qd, kd, cd)


def _ret_bwd(qr, kr, v, do, states):
    L = qr.shape[0]
    nc = L // CHUNK
    RB = _ret_rows(L)
    G = RB // CHUNK
    nt = nc // G
    dm, qd, kd, cd = _ret_decays()
    dn_nt = (((1,), (1,)), ((), ()))
    dn_tn = (((0,), (0,)), ((), ()))

    def body(q_ref, k_ref, v_ref, do_ref, st_ref, dm_ref, qd_ref, kd_ref, cd_ref,
             dq_ref, dk_ref, dv_ref, ds_ref):
        @pl.when(pl.program_id(1) == 0)
        def _():
            ds_ref[...] = jnp.zeros_like(ds_ref)

        dstate = ds_ref[...]
        dmat = dm_ref[...]
        for g in reversed(range(G)):
            rows = slice(g * CHUNK, (g + 1) * CHUNK)
            q, k, vv, do = q_ref[rows, :], k_ref[rows, :], v_ref[rows, :], do_ref[rows, :]
            kb = k.astype(_MXU)
            dob = do.astype(_MXU)
            doq = (do * qd_ref[...]).astype(_MXU)
            dsb = dstate.astype(_MXU)
            s = (lax.dot_general(q, kb, dn_nt, preferred_element_type=F32) * dmat).astype(_MXU)
            draw = (lax.dot_general(dob, vv, dn_nt, preferred_element_type=F32) * dmat).astype(_MXU)
            kdm = (k * kd_ref[...]).astype(_MXU)
            dv = (lax.dot_general(s, dob, dn_tn, preferred_element_type=F32)
                  + jnp.dot(kdm, dsb, preferred_element_type=F32))
            dq = (jnp.dot(draw, kb, preferred_element_type=F32)
                  + lax.dot_general(doq, st_ref[g], dn_nt, preferred_element_type=F32))
            dk = (lax.dot_general(draw, q, dn_tn, preferred_element_type=F32)
                  + lax.dot_general(vv, dsb, dn_nt, preferred_element_type=F32) * kd_ref[...])
            dq_ref[rows, :] = dq
            dk_ref[rows, :] = dk
            dv_ref[rows, :] = dv.astype(_MXU)
            dstate = cd_ref[...] * dstate + lax.dot_general(q, doq, dn_tn, preferred_element_type=F32)
        ds_ref[...] = dstate

    rev = lambda h, c: (nt - 1 - c, h)
    return pl.pallas_call(
        body, name="ret_bwd",
        out_shape=(jax.ShapeDtypeStruct((L, _RET_QKW), F32), jax.ShapeDtypeStruct((L, _RET_QKW), F32),
                   jax.ShapeDtypeStruct((L, RET_HEADS * RET_V), _MXU)),
        grid=(RET_HEADS, nt),
        in_specs=[pl.BlockSpec((RB, RET_QK), rev),
                  pl.BlockSpec((RB, RET_QK), rev),
                  pl.BlockSpec((RB, RET_V), rev),
                  pl.BlockSpec((RB, RET_V), rev),
                  pl.BlockSpec((None, G, RET_QK, RET_V), lambda h, c: (h, nt - 1 - c, 0, 0)),
                  pl.BlockSpec((None, CHUNK, CHUNK), lambda h, c: (h, 0, 0)),
                  pl.BlockSpec((None, CHUNK, 1), lambda h, c: (h, 0, 0)),
                  pl.BlockSpec((None, CHUNK, 1), lambda h, c: (h, 0, 0)),
                  pl.BlockSpec((None, 1, RET_V), lambda h, c: (h, 0, 0))],
        out_specs=(pl.BlockSpec((RB, RET_QK), rev), pl.BlockSpec((RB, RET_QK), rev),
                   pl.BlockSpec((RB, RET_V), rev)),
        scratch_shapes=[pltpu.VMEM((RET_QK, RET_V), F32)],
        compiler_params=_params(("parallel", "arbitrary")),
    )(qr, kr, v, do, states, dm, qd, kd, cd)


def _gn(o, g):
    mu = jnp.mean(o, axis=-1, keepdims=True)
    oc = o - mu
    var = jnp.mean(oc * oc, axis=-1, keepdims=True)
    return oc * lax.rsqrt(var + NORM_EPS) * g


def _ret_gate_fwd(o, qkz, gn_g):
    L, W = o.shape
    zblk = qkz.shape[1] // W - 1

    def body(o_ref, z_ref, g_ref, y_ref):
        for h in range(RET_HEADS):
            sl = slice(h * RET_V, (h + 1) * RET_V)
            y_ref[:, sl] = (_gn(o_ref[:, sl], g_ref[:, sl]) * _silu(z_ref[:, sl])).astype(_MXU)

    return pl.pallas_call(
        body, name="ret_gate_fwd",
        out_shape=jax.ShapeDtypeStruct((L, W), _MXU),
        grid=(L // ROW_TILE,),
        in_specs=[_rows(W), pl.BlockSpec((ROW_TILE, W), lambda i: (i, zblk)), _whole((1, W))],
        out_specs=_rows(W),
        compiler_params=_params(("parallel",)),
    )(o, qkz, gn_g)


def _ret_gate_bwd(dy, o, qkz, gn_g):
    L, W = o.shape
    zblk = qkz.shape[1] // W - 1

    def body(dy_ref, o_ref, z_ref, g_ref, do_ref, dz_ref, dg_ref):
        @pl.when(pl.program_id(0) == 0)
        def _():
            dg_ref[...] = jnp.zeros_like(dg_ref)

        for h in range(RET_HEADS):
            sl = slice(h * RET_V, (h + 1) * RET_V)
            z = z_ref[:, sl]
            o = o_ref[:, sl]
            g = g_ref[:, sl]
            dy = dy_ref[:, sl]
            mu = jnp.mean(o, axis=-1, keepdims=True)
            oc = o - mu
            var = jnp.mean(oc * oc, axis=-1, keepdims=True)
            r = lax.rsqrt(var + NORM_EPS)
            xh = oc * r
            dn = dy * _silu(z)
            dz_ref[:, sl] = (dy * (xh * g) * _dsilu(z)).astype(_MXU)
            dg_ref[:, sl] += jnp.sum(dn * xh, axis=0, keepdims=True)
            dxh = dn * g
            do_ref[:, sl] = r * (dxh - jnp.mean(dxh, axis=-1, keepdims=True)
                                 - xh * jnp.mean(dxh * xh, axis=-1, keepdims=True))

    return pl.pallas_call(
        body, name="ret_gate_bwd",
        out_shape=(jax.ShapeDtypeStruct((L, W), F32), jax.ShapeDtypeStruct((L, W), _MXU),
                   jax.ShapeDtypeStruct((1, W), F32)),
        grid=(L // ROW_TILE,),
        in_specs=[_rows(W), _rows(W), pl.BlockSpec((ROW_TILE, W), lambda i: (i, zblk)), _whole((1, W))],
        out_specs=(_rows(W), _rows(W), _whole((1, W))),
        compiler_params=_params(("arbitrary",)),
    )(dy, o, qkz, gn_g)


def _dead_rows(pad):
    return pad // LANES * LANES


def _heads_t(a):
    return a[:, :FOX_HEADS].T


def _fox_fwd(h, w, lnw, pad, tag, side=None):
    w_in, b_f, w_out = w
    L = h.shape[0]
    W = D_MODEL
    w_qkv = w_in[:, :3 * W]
    w_zf = jnp.concatenate([w_in[:, 3 * W:4 * W], jnp.pad(w_in[:, 4 * W:], ((0, 0), (0, LANES - FOX_HEADS)))], axis=1)
    bias = jnp.pad(b_f, (0, LANES - FOX_HEADS))[None, :]
    qkv = _matmul(h, w_qkv, "nn", _MXU, f"{tag}_qkv")
    zf = _matmul(h, w_zf, "nn", F32, f"{tag}_zf")
    cum = _fox_cum(zf, bias, pad, f"{tag}_cum")
    ab = _bias_terms(_heads_t(cum), pad)
    nb = W // LANES
    o, lse, y, side_out = _attn_fwd(qkv, 0, qkv, nb, qkv, 2 * nb, zf, ab, FOX_HEADS, LANES, LANES, LANES ** -0.5,
                                    f"{tag}_attn_fwd", _dead_rows(pad), side)
    mo = _matmul(y, w_out, "nn", F32, f"{tag}_out")
    zl, hn = _ln_fwd(h, mo, lnw[0], lnw[1], f"{tag}_ln_fwd")
    return hn, dict(h=h, qkv=qkv, zf=zf, ab=ab, o=o, lse=lse, y=y, zl=zl,
                    w_all=jnp.concatenate([w_qkv, w_zf], axis=1), w_out=w_out, bias=bias), side_out


def _fox_bwd(s, dzl, pad, tag, side=None, send_own=False):
    W = D_MODEL
    nb = W // LANES
    dy = _matmul(dzl, s["w_out"], "nt", F32, f"{tag}_dy")
    d_wout = _matmul(s["y"], dzl, "tn", F32, f"{tag}_dwout")
    if send_own:
        side = (list(side[0]) + [_split_layer("fox_w_out", d_wout)], True)
    do, delta, dzg = _gate_bwd(dy, s["o"], s["zf"], FOX_HEADS, f"{tag}_gate_bwd")
    qkv = s["qkv"]
    dq, dk, dv, dck, dcq, side_out = _attn_bwd(qkv, 0, qkv, nb, qkv, 2 * nb, do, s["ab"], s["lse"], delta,
                                               FOX_HEADS, LANES, LANES, LANES ** -0.5, f"{tag}_attn_bwd",
                                               _dead_rows(pad), side)
    dcum = jnp.pad((dck + dcq).reshape(FOX_HEADS, -1).T, ((0, 0), (0, LANES - FOX_HEADS)))
    df, dbias = _fox_cum_bwd(dcum, s["zf"], s["bias"], pad, f"{tag}_cum_bwd")
    dproj = jnp.concatenate([dq.astype(_MXU), dk.astype(_MXU), dv.astype(_MXU), dzg, df], axis=1)
    d_wall = _matmul(s["h"], dproj, "tn", F32, f"{tag}_dwin")
    d_win = jnp.concatenate([d_wall[:, :4 * W], d_wall[:, 4 * W:4 * W + FOX_HEADS]], axis=1)
    if send_own:
        dh, win_parts = _matmul(dproj, s["w_all"], "nt", F32, f"{tag}_dh", ([_split_layer("fox_w_in", d_win)], True))
        side_out = list(side_out) + list(win_parts)
    else:
        dh = _matmul(dproj, s["w_all"], "nt", F32, f"{tag}_dh")
    return dh, (d_win, dbias[0, :FOX_HEADS], d_wout), side_out


def _mla_tables(L, pad):
    pos = (jnp.arange(L) - pad).astype(F32)
    inv = ROPE_BASE ** (-jnp.arange(0, MLA_ROPE, 2, dtype=F32) / MLA_ROPE)
    ang = pos[:, None] * inv[None, :]
    c, s = jnp.cos(ang), jnp.sin(ang)
    z32, z64 = jnp.zeros_like(s), jnp.zeros((L, LANES - MLA_ROPE), F32)
    cos = jnp.concatenate([c, c, jnp.ones_like(z64)], axis=1)
    sa = jnp.concatenate([-s, z32, z64], axis=1)
    sb = jnp.concatenate([z32, s, z64], axis=1)
    return cos, sa, sb


def _mla_weights(w_in, w_uq, w_ukv):
    a, b = MLA_Q_LORA, MLA_Q_LORA + MLA_KV_LORA
    w_zc = jnp.concatenate([w_in[:, b + MLA_ROPE:], w_in[:, :b + MLA_ROPE],
                            jnp.zeros((D_MODEL, LANES - MLA_ROPE), w_in.dtype)], axis=1)
    uq = w_uq.reshape(MLA_Q_LORA, MLA_HEADS, MLA_NOPE + MLA_ROPE)
    uq = jnp.pad(uq, ((0, 0), (0, 0), (0, MLA_DK - MLA_NOPE - MLA_ROPE))).reshape(MLA_Q_LORA, MLA_HEADS * MLA_DK)
    ukv = w_ukv.reshape(MLA_KV_LORA, MLA_HEADS, MLA_NOPE + MLA_V)
    ukv = jnp.concatenate([ukv[:, :, :MLA_NOPE].reshape(MLA_KV_LORA, -1), ukv[:, :, MLA_NOPE:].reshape(MLA_KV_LORA, -1)], axis=1)
    return w_zc, uq, ukv


def _mla_fwd(h, w, lnw, pad):
    w_in, gq, gkv, w_uq, w_ukv, w_out = w
    L = h.shape[0]
    w_zc, uq, ukv = _mla_weights(w_in, w_uq, w_ukv)
    cos, sa, sb = _mla_tables(L, pad)
    proj = _matmul(h, w_zc, "nn", F32, "mla_proj")
    cqn, ckvn = _mla_norm_fwd(proj, gq[None, :], gkv[None, :])
    qf = _matmul(cqn, uq, "nn", F32, "mla_uq")
    kv = _matmul(ckvn, ukv, "nn", _MXU, "mla_ukv")
    q_full, k_full = _mla_rope_fwd(qf, kv, proj, cos, sa, sb)
    ab = _bias_terms(jnp.zeros((MLA_HEADS, L), F32), pad)
    nb = MLA_HEADS * MLA_NOPE // LANES
    o, lse, y, _ = _attn_fwd(q_full, 0, k_full, 0, kv, nb, proj, ab, MLA_HEADS, MLA_DK, MLA_V,
                             (MLA_NOPE + MLA_ROPE) ** -0.5, "mla_attn_fwd", _dead_rows(pad))
    mo = _matmul(y, w_out, "nn", F32, "mla_out")
    zl, hn = _ln_fwd(h, mo, lnw[0], lnw[1], "mla_ln_fwd")
    return hn, dict(h=h, proj=proj, cqn=cqn, ckvn=ckvn, kv=kv, q_full=q_full, k_full=k_full, ab=ab,
                    dead=_dead_rows(pad),
                    o=o, lse=lse, y=y, zl=zl, w_zc=w_zc, uq=uq, ukv=ukv, w_out=w_out, gq=gq, gkv=gkv,
                    tabs=(cos, sa, sb))


def _mla_bwd(s, dzl):
    H = MLA_HEADS
    dy = _matmul(dzl, s["w_out"], "nt", F32, "mla_dy")
    d_wout = _matmul(s["y"], dzl, "tn", F32, "mla_dwout")
    do, delta, dzg = _gate_bwd(dy, s["o"], s["proj"], H, "mla_gate_bwd")
    nb = H * MLA_NOPE // LANES
    dq_full, dk_full, dv, _, _, _ = _attn_bwd(s["q_full"], 0, s["k_full"], 0, s["kv"], nb, do, s["ab"],
                                              s["lse"], delta,
                                              H, MLA_DK, MLA_V, (MLA_NOPE + MLA_ROPE) ** -0.5, "mla_attn_bwd",
                                              s["dead"], bias_grads=False)
    cos, sa, sb = s["tabs"]
    dqf, dkn, dkr = _mla_rope_bwd(dq_full, dk_full, cos, -sa, -sb)
    d_uq = _matmul(s["cqn"], dqf, "tn", F32, "mla_duq")
    dcqn = _matmul(dqf, s["uq"], "nt", F32, "mla_dcqn")
    dkv = jnp.concatenate([dkn, dv.astype(_MXU)], axis=1)
    d_ukv = _matmul(s["ckvn"], dkv, "tn", F32, "mla_dukv")
    dckvn = _matmul(dkv, s["ukv"], "nt", F32, "mla_dckvn")
    dcq, dckv, dgq, dgkv = _mla_norm_bwd(s["proj"], s["gq"][None, :], s["gkv"][None, :], dcqn, dckvn)
    dproj = jnp.concatenate([dzg, dcq, dckv, dkr], axis=1)
    d_wzc = _matmul(s["h"], dproj, "tn", F32, "mla_dwin")
    dh = _matmul(dproj, s["w_zc"], "nt", F32, "mla_dh")
    W = D_MODEL
    d_win = jnp.concatenate([d_wzc[:, W:W + MLA_Q_LORA + MLA_KV_LORA + MLA_ROPE], d_wzc[:, :W]], axis=1)
    d_wuq = d_uq.reshape(MLA_Q_LORA, H, MLA_DK)[:, :, :MLA_NOPE + MLA_ROPE].reshape(MLA_Q_LORA, -1)
    hk = H * MLA_NOPE
    d_wukv = jnp.concatenate([d_ukv[:, :hk].reshape(MLA_KV_LORA, H, MLA_NOPE),
                              d_ukv[:, hk:].reshape(MLA_KV_LORA, H, MLA_V)], axis=2).reshape(MLA_KV_LORA, -1)
    return dh, (d_win, dgq[0], dgkv[0], d_wuq, d_wukv, d_wout)


def _ret_tables(L, pad):
    pos = (jnp.arange(L) - pad).astype(F32)
    inv = 1.0 / (ROPE_BASE ** jnp.linspace(0.0, 1.0, RET_QK // 2, dtype=F32))
    ang = pos[:, None] * inv[None, :]
    return jnp.cos(ang), jnp.sin(ang)


def _ret_layer_fwd(h, w, lnw, pad):
    w_in, gn_g, w_out = w
    L = h.shape[0]
    q1 = 2 * _RET_QKW
    v1 = q1 + RET_HEADS * RET_V
    w_qkz = jnp.concatenate([w_in[:, :q1], w_in[:, v1:]], axis=1)
    cos, sin = _ret_tables(L, pad)
    qkz = _matmul(h, w_qkz, "nn", F32, "ret_qkz")
    v = _matmul(h, w_in[:, q1:v1], "nn", _MXU, "ret_v")
    qr, kr = _ret_rope_fwd(qkz, cos, sin, pad)
    o, states = _ret_fwd(qr, kr, v)
    y = _ret_gate_fwd(o, qkz, gn_g[None, :])
    mo = _matmul(y, w_out, "nn", F32, "ret_out")
    zl, hn = _ln_fwd(h, mo, lnw[0], lnw[1], "ret_ln_fwd")
    return hn, dict(h=h, qkz=qkz, v=v, qr=qr, kr=kr, o=o, states=states, y=y, zl=zl, w_in=w_in, w_out=w_out,
                    gn_g=gn_g, tabs=(cos, sin))


def _ret_layer_bwd(s, dzl, pad):
    dy = _matmul(dzl, s["w_out"], "nt", F32, "ret_dy")
    d_wout = _matmul(s["y"], dzl, "tn", F32, "ret_dwout")
    do, dzg, dgn = _ret_gate_bwd(dy, s["o"], s["qkz"], s["gn_g"][None, :])
    dqr, dkr, dv = _ret_bwd(s["qr"], s["kr"], s["v"], do, s["states"])
    cos, sin = s["tabs"]
    dq, dk = _ret_rope_bwd(dqr, dkr, cos, sin, pad)
    dproj = jnp.concatenate([dq, dk, dv, dzg], axis=1)
    d_win = _matmul(s["h"], dproj, "tn", F32, "ret_dwin")
    dh = _matmul(dproj, s["w_in"], "nt", F32, "ret_dh")
    return dh, (d_win, dgn[0], d_wout)


def _local_step(x, target, meta, fox0_w, late_shards, small, ln_g, ln_b):
    n_real = x.shape[0]
    first = -(-N_META // ROW_TILE)
    pad = first * ROW_TILE - N_META
    h0 = jnp.concatenate([jnp.zeros((pad, D_MODEL), F32), meta, x], axis=0)
    lnw = [(ln_g[i][None, :], ln_b[i][None, :]) for i in range(4)]
    b_f = small["fox_b_f"]

    h1, s0, gathered = _fox_fwd(h0, (fox0_w[0], b_f[0], fox0_w[1]), lnw[0], pad, "fox0", (late_shards, False))
    w = {key: _merge_layer(key[0], g).astype(_MXU) for key, g in zip(_LATE, gathered)}
    h2, s1 = _mla_fwd(h1, (w["mla_w_in", 0], small["mla_q_norm"][0], small["mla_kv_norm"][0], w["mla_w_uq", 0],
                           w["mla_w_ukv", 0], w["mla_w_out", 0]), lnw[1], pad)
    h3, s2 = _ret_layer_fwd(h2, (w["ret_w_in", 0], small["ret_gn_g"][0], w["ret_w_out", 0]), lnw[2], pad)
    h4, s3, _ = _fox_fwd(h3, (w["fox_w_in", 1], b_f[1], w["fox_w_out", 1]), lnw[3], pad, "fox1")

    dh4, loss = _loss_head(h4, target, first)
    dz3, dg3, db3 = _ln_bwd(s3["zl"], lnw[3][0], dh4, None, "ln3_bwd")
    dm3, gf1, _ = _fox_bwd(s3, dz3, pad, "fox1")
    dz2, dg2, db2 = _ln_bwd(s2["zl"], lnw[2][0], dz3, dm3, "ln2_bwd")
    dm2, gr = _ret_layer_bwd(s2, dz2, pad)
    dz1, dg1, db1 = _ln_bwd(s1["zl"], lnw[1][0], dz2, dm2, "ln1_bwd")
    dm1, gm = _mla_bwd(s1, dz1)
    dz0, dg0, db0 = _ln_bwd(s0["zl"], lnw[0][0], dz1, dm1, "ln0_bwd")
    late = {("fox_w_in", 1): gf1[0], ("fox_w_out", 1): gf1[2], ("mla_w_in", 0): gm[0], ("mla_w_uq", 0): gm[3],
            ("mla_w_ukv", 0): gm[4], ("mla_w_out", 0): gm[5], ("ret_w_in", 0): gr[0], ("ret_w_out", 0): gr[2]}
    dm0, gf0, sent = _fox_bwd(s0, dz0, pad, "fox0", ([_split_layer(k[0], late[k]) for k in _LATE], True), True)
    parts = dict(zip(_LATE + [("fox_w_out", 0), ("fox_w_in", 0)], sent))
    grad_x, grad_meta = _input_grads(dz0, dm0, first, n_real)

    small_grads = dict(
        meta=grad_meta, fox_b_f=jnp.stack([gf0[1], gf1[1]]), mla_q_norm=gm[1][None], mla_kv_norm=gm[2][None],
        ret_gn_g=gr[1][None], ln_g=jnp.concatenate([dg0, dg1, dg2, dg3], axis=0),
        ln_b=jnp.concatenate([db0, db1, db2, db3], axis=0))
    return loss, grad_x, parts, small_grads


def _exchange(tensors, scatter, name):
    nt = len(tensors)

    def body(*refs):
        copies = _exchange_copies(refs[:nt], refs[nt:2 * nt], *refs[2 * nt:], scatter)
        for cp in copies:
            cp.start()
        for cp in copies:
            cp.wait()

    hbm = pl.BlockSpec(memory_space=pl.ANY)
    out_shape, sems = _exchange_shapes(tensors, scatter)
    return pl.pallas_call(
        body, name=name, out_shape=out_shape, in_specs=[hbm] * nt, out_specs=(hbm,) * nt, scratch_shapes=sems,
    )(*tensors)


def _adamw(parts, w, m, v, name):
    R, C = w.shape
    tr = next(t for t in range(min(R, 256), 0, -1) if R % t == 0 and (t % 8 == 0 or t == R))

    def body(p_ref, w_ref, m_ref, v_ref, g_ref, d_ref, mo_ref, vo_ref):
        g = p_ref[0]
        for d in range(1, N_DEV):
            g = g + p_ref[d]
        mn = ADAM_B1 * m_ref[...] + (1.0 - ADAM_B1) * g
        vn = ADAM_B2 * v_ref[...] + (1.0 - ADAM_B2) * (g * g)
        m_hat = mn / (1.0 - ADAM_B1 ** ADAM_STEP)
        v_hat = vn / (1.0 - ADAM_B2 ** ADAM_STEP)
        g_ref[...] = g
        d_ref[...] = -ADAM_LR * (m_hat / (jnp.sqrt(v_hat) + ADAM_EPS) + ADAM_WD * w_ref[...])
        mo_ref[...] = mn
        vo_ref[...] = vn

    blk = pl.BlockSpec((tr, C), lambda i: (i, 0))
    sd = jax.ShapeDtypeStruct((R, C), F32)
    return pl.pallas_call(
        body, name=name,
        out_shape=(sd, sd, sd, sd),
        grid=(R // tr,),
        in_specs=[pl.BlockSpec((N_DEV, tr, C), lambda i: (0, i, 0)), blk, blk, blk],
        out_specs=(blk, blk, blk, blk),
        compiler_params=_params(("parallel",)),
    )(parts, w, m, v)


_SPECS = dict(
    meta=((16, 1024), 1),
    fox_w_in=((2, 1024, 4104), 2), fox_b_f=((2, 8), None), fox_w_out=((2, 1024, 1024), 1),
    mla_w_in=((1, 1024, 1728), 2), mla_q_norm=((1, 384), None), mla_kv_norm=((1, 256), None),
    mla_w_uq=((1, 384, 1536), 2), mla_w_ukv=((1, 256, 2048), 2), mla_w_out=((1, 1024, 1024), 1),
    ret_w_in=((1, 1024, 6144), 2), ret_gn_g=((1, 2048), 1), ret_w_out=((1, 2048, 1024), 1),
    ln_g=((4, 1024), None), ln_b=((4, 1024), None),
)
_NAMES = list(_SPECS)
_BIG = ["fox_w_in", "fox_w_out", "mla_w_in", "mla_w_uq", "mla_w_ukv", "mla_w_out", "ret_w_in", "ret_w_out"]
_SMALL = [n for n in _NAMES if n not in _BIG]
_SMALL_GATHERED = ["meta", "ret_gn_g"]
_PACK_ROWS = 8


def _shard_shape(name):
    shape, ax = _SPECS[name]
    if ax is None:
        return shape
    return tuple(s // N_DEV if i == ax else s for i, s in enumerate(shape))


_EARLY = [("fox_w_in", 0), ("fox_w_out", 0)]
_LATE = [("fox_w_in", 1), ("fox_w_out", 1), ("mla_w_in", 0), ("mla_w_uq", 0), ("mla_w_ukv", 0), ("mla_w_out", 0),
         ("ret_w_in", 0), ("ret_w_out", 0)]


def _split_at(full, ax):
    shape = full.shape
    parts = full.reshape(shape[:ax] + (N_DEV, shape[ax] // N_DEV) + shape[ax + 1:])
    return jnp.moveaxis(parts, ax, 0)


def _merge_at(stacked, ax):
    full = jnp.moveaxis(stacked, 0, ax)
    return full.reshape(full.shape[:ax] + (-1,) + full.shape[ax + 2:])


def _split(name, full):
    return _split_at(full, _SPECS[name][1])


def _merge(name, stacked):
    return _merge_at(stacked, _SPECS[name][1])


def _split_layer(name, full):
    return _split_at(full, _SPECS[name][1] - 1)


def _merge_layer(name, stacked):
    return _merge_at(stacked, _SPECS[name][1] - 1)


def _pad_rows(flat):
    n = flat.shape[-1]
    unit = _PACK_ROWS * LANES
    total = -(-n // unit) * unit
    flat = jnp.pad(flat, [(0, 0)] * (flat.ndim - 1) + [(0, total - n)])
    return flat.reshape(flat.shape[:-1] + (total // LANES, LANES))


def _pack_small(tree, names, tail):
    return _pad_rows(jnp.concatenate([tree[n].reshape(-1) for n in names] + [tail]))


def _unpack_small(buf, names):
    flat, out, at = buf.reshape(-1), {}, 0
    for n in names:
        shp = _shard_shape(n)
        out[n] = flat[at:at + math.prod(shp)].reshape(shp)
        at += math.prod(shp)
    return out, flat[at:]


def kernel(x, meta, fox_w_in, fox_b_f, fox_w_out, mla_w_in, mla_q_norm, mla_kv_norm, mla_w_uq, mla_w_ukv, mla_w_out, ret_w_in, ret_gn_g, ret_w_out, ln_g, ln_b, loss_target, m_meta, m_fox_w_in, m_fox_b_f, m_fox_w_out, m_mla_w_in, m_mla_q_norm, m_mla_kv_norm, m_mla_w_uq, m_mla_w_ukv, m_mla_w_out, m_ret_w_in, m_ret_gn_g, m_ret_w_out, m_ln_g, m_ln_b, v_meta, v_fox_w_in, v_fox_b_f, v_fox_w_out, v_mla_w_in, v_mla_q_norm, v_mla_kv_norm, v_mla_w_uq, v_mla_w_ukv, v_mla_w_out, v_ret_w_in, v_ret_gn_g, v_ret_w_out, v_ln_g, v_ln_b):
    w = dict(meta=meta, fox_w_in=fox_w_in, fox_b_f=fox_b_f, fox_w_out=fox_w_out, mla_w_in=mla_w_in,
             mla_q_norm=mla_q_norm, mla_kv_norm=mla_kv_norm, mla_w_uq=mla_w_uq, mla_w_ukv=mla_w_ukv,
             mla_w_out=mla_w_out, ret_w_in=ret_w_in, ret_gn_g=ret_gn_g, ret_w_out=ret_w_out, ln_g=ln_g, ln_b=ln_b)
    m = dict(meta=m_meta, fox_w_in=m_fox_w_in, fox_b_f=m_fox_b_f, fox_w_out=m_fox_w_out, mla_w_in=m_mla_w_in,
             mla_q_norm=m_mla_q_norm, mla_kv_norm=m_mla_kv_norm, mla_w_uq=m_mla_w_uq, mla_w_ukv=m_mla_w_ukv,
             mla_w_out=m_mla_w_out, ret_w_in=m_ret_w_in, ret_gn_g=m_ret_gn_g, ret_w_out=m_ret_w_out, ln_g=m_ln_g, ln_b=m_ln_b)
    v = dict(meta=v_meta, fox_w_in=v_fox_w_in, fox_b_f=v_fox_b_f, fox_w_out=v_fox_w_out, mla_w_in=v_mla_w_in,
             mla_q_norm=v_mla_q_norm, mla_kv_norm=v_mla_kv_norm, mla_w_uq=v_mla_w_uq, mla_w_ukv=v_mla_w_ukv,
             mla_w_out=v_mla_w_out, ret_w_in=v_ret_w_in, ret_gn_g=v_ret_gn_g, ret_w_out=v_ret_w_out, ln_g=v_ln_g, ln_b=v_ln_b)

    def shard16(key):
        return w[key[0]][key[1]].astype(jnp.bfloat16)

    none = jnp.zeros((0,), F32)
    gathered = _exchange([shard16(k) for k in _EARLY] + [_pack_small(w, _SMALL_GATHERED, none)],
                         False, "weights_all_gather")
    fox0_w = [_merge_layer(k[0], g).astype(_MXU) for k, g in zip(_EARLY, gathered)]
    small = {n: w[n] for n in _SMALL}
    pieces = [_unpack_small(gathered[-1][d], _SMALL_GATHERED)[0] for d in range(N_DEV)]
    for n in _SMALL_GATHERED:
        small[n] = _merge(n, jnp.stack([p[n] for p in pieces]))

    loss, grad_x, by_key, grads = _local_step(
        x[0], loss_target[0], small["meta"], fox0_w, [shard16(k) for k in _LATE], small, ln_g, ln_b)

    def small_rows(n):
        if _SPECS[n][1] is None:
            return jnp.broadcast_to(grads[n].reshape(1, -1), (N_DEV, grads[n].size))
        return _split(n, grads[n]).reshape(N_DEV, -1)

    small_out = _pad_rows(jnp.concatenate([small_rows(n) for n in _SMALL]
                                          + [jnp.broadcast_to(loss.reshape(1, 1), (N_DEV, 1))], axis=1))
    parts = _exchange([small_out], True, "grads_all_to_all")

    def rows2d(a):
        return a.reshape(-1, a.shape[-1])

    out = {}
    for n in _BIG:
        layers = [by_key[n, i] for i in range(_SPECS[n][0][0])]
        p = layers[0] if len(layers) == 1 else jnp.stack(layers, axis=1)
        res = _adamw(p.reshape(N_DEV, -1, p.shape[-1]), rows2d(w[n]), rows2d(m[n]), rows2d(v[n]), f"adamw_{n}")
        out[n] = [r.reshape(w[n].shape) for r in res]
    zero1 = jnp.zeros((1,), F32)
    res = _adamw(parts[-1], _pack_small(w, _SMALL, zero1), _pack_small(m, _SMALL, zero1),
                 _pack_small(v, _SMALL, zero1), "adamw_small")
    small_res = [_unpack_small(r, _SMALL) for r in res]
    for n in _SMALL:
        out[n] = [s[0][n] for s in small_res]
    total_loss = small_res[0][1][0]
    return (total_loss, grad_x[None], *[out[n][0] for n in _NAMES], *[out[n][1] for n in _NAMES],
            *[out[n][2] for n in _NAMES], *[out[n][3] for n in _NAMES])
```

```python
import math

import numpy as np
import jax
import jax.numpy as jnp
from jax import lax
from jax.experimental import pallas as pl
from jax.experimental.pallas import tpu as pltpu

F32 = jnp.float32
_MXU = jnp.bfloat16

N_DEV = 8
N_META = 16
D_MODEL = 1024
ROW_TILE = 512
LANES = 128
CHUNK = 128
NEG = -1e30
V7X_VMEM_BYTES = 64 * 1024 * 1024
VMEM_LIMIT = V7X_VMEM_BYTES - 4 * 1024 * 1024

FOX_HEADS = 8
MLA_HEADS = 8
MLA_NOPE, MLA_ROPE, MLA_V = 128, 64, 128
MLA_Q_LORA, MLA_KV_LORA = 384, 256
MLA_DK = 256
RET_HEADS, RET_QK, RET_V = 4, 256, 512
ROPE_BASE = 10000.0
ALPHA = (2 * 4) ** 0.25
NORM_EPS = 1e-5

ADAM_LR, ADAM_B1, ADAM_B2, ADAM_EPS, ADAM_WD, ADAM_STEP = 0.001, 0.9, 0.999, 1e-08, 0.01, 10


def _params(sem, vmem=VMEM_LIMIT):
    return pltpu.CompilerParams(dimension_semantics=sem, vmem_limit_bytes=vmem)


def _tile(n, pref):
    if n <= pref:
        return n
    t = (pref // LANES) * LANES
    while n % t:
        t -= LANES
    return t


_MM_TILES = {"nn": (512, 4224, 2048), "nt": (512, 2048, 6144), "tn": (1024, 2048, 1536)}


def _matmul(a, b, mode, out_dtype, name, side=None):
    if mode == "nn":
        (M, K), (K2, N) = a.shape, b.shape
    elif mode == "nt":
        (M, K), (N, K2) = a.shape, b.shape
    else:
        (K, M), (K2, N) = a.shape, b.shape
    assert K == K2, (a.shape, b.shape, mode)
    tm, tn, tk = (_tile(d, p) for d, p in zip((M, N, K), _MM_TILES[mode]))
    nk = K // tk
    if mode == "nn":
        a_spec = pl.BlockSpec((tm, tk), lambda i, j, k: (i, k))
        b_spec = pl.BlockSpec((tk, tn), lambda i, j, k: (k, j))
        dn = (((1,), (0,)), ((), ()))
    elif mode == "nt":
        a_spec = pl.BlockSpec((tm, tk), lambda i, j, k: (i, k))
        b_spec = pl.BlockSpec((tn, tk), lambda i, j, k: (j, k))
        dn = (((1,), (1,)), ((), ()))
    else:
        a_spec = pl.BlockSpec((tk, tm), lambda i, j, k: (k, i))
        b_spec = pl.BlockSpec((tk, tn), lambda i, j, k: (k, j))
        dn = (((0,), (0,)), ((), ()))

    ns = len(side[0]) if side else 0
    grid = (M // tm, N // tn, nk)
    n_acc = 1 if nk > 1 else 0

    def body(*refs):
        a_ref, b_ref = refs[:2]
        o_ref = refs[2 + ns]
        k = pl.program_id(2)
        if side:
            i, j = pl.program_id(0), pl.program_id(1)
            _side_exchange(side, refs[2:2 + ns], refs[3 + ns:3 + 2 * ns], refs[3 + 2 * ns + n_acc:],
                           (i == 0) & (j == 0) & (k == 0),
                           (i == grid[0] - 1) & (j == grid[1] - 1) & (k == nk - 1))
        part = lax.dot_general(a_ref[...].astype(_MXU), b_ref[...].astype(_MXU), dn,
                               preferred_element_type=F32)
        if nk == 1:
            o_ref[...] = part.astype(out_dtype)
            return
        acc_ref = refs[3 + 2 * ns]

        @pl.when(k == 0)
        def _():
            acc_ref[...] = part

        @pl.when(k > 0)
        def _():
            acc_ref[...] += part

        @pl.when(k == nk - 1)
        def _():
            o_ref[...] = acc_ref[...].astype(out_dtype)

    hbm = pl.BlockSpec(memory_space=pl.ANY)
    side_shapes, side_sems = _exchange_shapes(*side) if side else ((), [])
    out, *side_out = pl.pallas_call(
        body, name=name,
        out_shape=(jax.ShapeDtypeStruct((M, N), out_dtype),) + side_shapes,
        grid=grid,
        in_specs=[a_spec, b_spec] + [hbm] * ns,
        out_specs=(pl.BlockSpec((tm, tn), lambda i, j, k: (i, j)),) + (hbm,) * ns,
        scratch_shapes=[pltpu.VMEM((tm, tn), F32)] * n_acc + side_sems,
        compiler_params=_params(("arbitrary",) * 3 if side else ("parallel", "parallel", "arbitrary")),
    )(a, b, *(side[0] if side else ()))
    return (out, side_out) if side else out


def _rows(w, off=0):
    return pl.BlockSpec((ROW_TILE, w), lambda i: (i + off, 0))


def _whole(shape):
    return pl.BlockSpec(shape, lambda i: (0,) * len(shape))


def _silu(z):
    return z * jax.nn.sigmoid(z)


def _dsilu(z):
    s = jax.nn.sigmoid(z)
    return s * (1.0 + z * (1.0 - s))


def _out_ln(y, w_out, h, g, b, name):
    L, D = h.shape
    K = y.shape[1]

    def body(y_ref, w_ref, h_ref, g_ref, b_ref, z_ref, o_ref):
        mo = jnp.dot(y_ref[...].astype(_MXU), w_ref[...].astype(_MXU), preferred_element_type=F32)
        z = ALPHA * h_ref[...] + mo
        mu = jnp.mean(z, axis=-1, keepdims=True)
        zc = z - mu
        var = jnp.mean(zc * zc, axis=-1, keepdims=True)
        z_ref[...] = z
        o_ref[...] = zc * lax.rsqrt(var + NORM_EPS) * g_ref[...] + b_ref[...]

    return pl.pallas_call(
        body, name=name,
        out_shape=(jax.ShapeDtypeStruct((L, D), F32), jax.ShapeDtypeStruct((L, D), F32)),
        grid=(L // ROW_TILE,),
        in_specs=[_rows(K), _whole((K, D)), _rows(D), _whole((1, D)), _whole((1, D))],
        out_specs=(_rows(D), _rows(D)),
        compiler_params=_params(("parallel",)),
    )(y, w_out, h, g, b)


def _ln_bwd(zl, g, ga, gb, name):
    L, D = zl.shape
    two = gb is not None

    def body(*refs):
        if two:
            z_ref, g_ref, ga_ref, gb_ref, dz_ref, dg_ref, db_ref = refs
            gout = ALPHA * ga_ref[...] + gb_ref[...]
        else:
            z_ref, g_ref, ga_ref, dz_ref, dg_ref, db_ref = refs
            gout = ga_ref[...]
        z = z_ref[...]
        mu = jnp.mean(z, axis=-1, keepdims=True)
        zc = z - mu
        var = jnp.mean(zc * zc, axis=-1, keepdims=True)
        rstd = lax.rsqrt(var + NORM_EPS)
        xhat = zc * rstd
        dxh = gout * g_ref[...]
        m1 = jnp.mean(dxh, axis=-1, keepdims=True)
        m2 = jnp.mean(dxh * xhat, axis=-1, keepdims=True)
        dz_ref[...] = rstd * (dxh - m1 - xhat * m2)

        @pl.when(pl.program_id(0) == 0)
        def _():
            dg_ref[...] = jnp.zeros_like(dg_ref)
            db_ref[...] = jnp.zeros_like(db_ref)

        dg_ref[...] += jnp.sum(gout * xhat, axis=0, keepdims=True)
        db_ref[...] += jnp.sum(gout, axis=0, keepdims=True)

    ins = [zl, g, ga] + ([gb] if two else [])
    return pl.pallas_call(
        body, name=name,
        out_shape=(jax.ShapeDtypeStruct((L, D), F32), jax.ShapeDtypeStruct((1, D), F32),
                   jax.ShapeDtypeStruct((1, D), F32)),
        grid=(L // ROW_TILE,),
        in_specs=[_rows(D), _whole((1, D)), _rows(D)] + ([_rows(D)] if two else []),
        out_specs=(_rows(D), _whole((1, D)), _whole((1, D))),
        compiler_params=_params(("arbitrary",)),
    )(*ins)


def _loss_head(h, target, first):
    L, D = h.shape

    def body(h_ref, t_ref, dh_ref, loss_ref):
        i = pl.program_id(0)

        @pl.when(i == 0)
        def _():
            loss_ref[...] = jnp.zeros_like(loss_ref)

        @pl.when(i < first)
        def _():
            dh_ref[...] = jnp.zeros_like(dh_ref)

        @pl.when(i >= first)
        def _():
            err = h_ref[...] - t_ref[...]
            dh_ref[...] = err * (1.0 / D)
            part = jnp.sum(jnp.sum(err * err, axis=-1, keepdims=True) * (1.0 / D), axis=0, keepdims=True)
            loss_ref[...] += 0.5 * part

    return pl.pallas_call(
        body, name="loss_head",
        out_shape=(jax.ShapeDtypeStruct((L, D), F32), jax.ShapeDtypeStruct((1, 1), F32)),
        grid=(L // ROW_TILE,),
        in_specs=[_rows(D), pl.BlockSpec((ROW_TILE, D), lambda i: (jnp.maximum(i - first, 0), 0))],
        out_specs=(_rows(D), _whole((1, 1))),
        compiler_params=_params(("arbitrary",)),
    )(h, target)


def _input_grads(dzl, dmix, first, n_real):
    L, D = dzl.shape

    def body(a_ref, b_ref, gx_ref, gm_ref):
        i = pl.program_id(0)
        val = ALPHA * a_ref[...] + b_ref[...]

        @pl.when(i == first - 1)
        def _():
            gm_ref[...] = val[ROW_TILE - N_META:, :]

        @pl.when(i >= first)
        def _():
            gx_ref[...] = val

    return pl.pallas_call(
        body, name="input_grads",
        out_shape=(jax.ShapeDtypeStruct((n_real, D), F32), jax.ShapeDtypeStruct((N_META, D), F32)),
        grid=(L // ROW_TILE,),
        in_specs=[_rows(D), _rows(D)],
        out_specs=(pl.BlockSpec((ROW_TILE, D), lambda i: (jnp.maximum(i - first, 0), 0)),
                   _whole((N_META, D))),
        compiler_params=_params(("arbitrary",)),
    )(dzl, dmix)


def _gate_bwd(dzl, w_out, o, zsrc, heads, name):
    L, W = o.shape
    D = dzl.shape[1]
    hd = W // heads

    def body(g_ref, w_ref, o_ref, z_ref, do_ref, dl_ref, dz_ref):
        z = z_ref[...]
        dy = lax.dot_general(g_ref[...].astype(_MXU), w_ref[...].astype(_MXU), (((1,), (1,)), ((), ())),
                             preferred_element_type=F32)
        o = o_ref[...]
        do = dy * _silu(z)
        do_ref[...] = do.astype(_MXU)
        dz_ref[...] = (dy * o * _dsilu(z)).astype(_MXU)
        prod = do * o
        for h in range(heads):
            dl_ref[h] = _as_row(jnp.sum(prod[:, h * hd:(h + 1) * hd], axis=-1, keepdims=True))

    return pl.pallas_call(
        body, name=name,
        out_shape=(jax.ShapeDtypeStruct((L, W), _MXU), jax.ShapeDtypeStruct((heads, L // ROW_TILE, 1, ROW_TILE), F32),
                   jax.ShapeDtypeStruct((L, W), _MXU)),
        grid=(L // ROW_TILE,),
        in_specs=[_rows(D), _whole((W, D)), _rows(W), _rows(W)],
        out_specs=(_rows(W), pl.BlockSpec((heads, None, 1, ROW_TILE), lambda i: (0, i, 0, 0)), _rows(W)),
        compiler_params=_params(("parallel",)),
    )(dzl, w_out, o, zsrc)


def _split3(x):
    hi = x.astype(_MXU)
    r1 = x - hi.astype(F32)
    mid = r1.astype(_MXU)
    lo = (r1 - mid.astype(F32)).astype(_MXU)
    return hi, mid, lo


def _tri_cumsum(x, reverse):
    T = x.shape[0]
    r = lax.broadcasted_iota(jnp.int32, (T, T), 0)
    c = lax.broadcasted_iota(jnp.int32, (T, T), 1)
    tri = jnp.where((c >= r) if reverse else (c <= r), 1.0, 0.0).astype(_MXU)
    out = jnp.zeros(x.shape, F32)
    for part in _split3(x):
        out = out + jnp.dot(tri, part, preferred_element_type=F32)
    return out


def _fox_cum(zf, bias, pad, name):
    L, Wz = zf.shape
    off = Wz // LANES - 1

    def body(f_ref, b_ref, c_ref, carry_ref):
        i = pl.program_id(0)

        @pl.when(i == 0)
        def _():
            carry_ref[...] = jnp.zeros_like(carry_ref)

        x = f_ref[...] + b_ref[...]
        logf = jnp.minimum(x, 0.0) - jnp.log(1.0 + jnp.exp(-jnp.abs(x)))
        row = i * ROW_TILE + lax.broadcasted_iota(jnp.int32, logf.shape, 0)
        logf = jnp.where(row >= pad, logf, 0.0)
        cum = _tri_cumsum(logf, False) + carry_ref[...]
        c_ref[...] = cum
        carry_ref[...] = cum[ROW_TILE - 1:, :]

    return pl.pallas_call(
        body, name=name,
        out_shape=jax.ShapeDtypeStruct((L, LANES), F32),
        grid=(L // ROW_TILE,),
        in_specs=[pl.BlockSpec((ROW_TILE, LANES), lambda i: (i, off)), _whole((1, LANES))],
        out_specs=_rows(LANES),
        scratch_shapes=[pltpu.VMEM((1, LANES), F32)],
        compiler_params=_params(("arbitrary",)),
    )(zf, bias)


def _fox_cum_bwd(dcum, zf, bias, pad, name):
    L, Wz = zf.shape
    off = Wz // LANES - 1
    n = L // ROW_TILE

    def body(d_ref, f_ref, b_ref, df_ref, db_ref, carry_ref):
        i = pl.program_id(0)

        @pl.when(i == 0)
        def _():
            carry_ref[...] = jnp.zeros_like(carry_ref)
            db_ref[...] = jnp.zeros_like(db_ref)

        rc = _tri_cumsum(d_ref[...], True) + carry_ref[...]
        carry_ref[...] = rc[:1, :]
        x = f_ref[...] + b_ref[...]
        row = (n - 1 - i) * ROW_TILE + lax.broadcasted_iota(jnp.int32, x.shape, 0)
        df = jnp.where(row >= pad, rc * jax.nn.sigmoid(-x), 0.0)
        df_ref[...] = df.astype(_MXU)
        db_ref[...] += jnp.sum(df, axis=0, keepdims=True)

    return pl.pallas_call(
        body, name=name,
        out_shape=(jax.ShapeDtypeStruct((L, LANES), _MXU), jax.ShapeDtypeStruct((1, LANES), F32)),
        grid=(n,),
        in_specs=[pl.BlockSpec((ROW_TILE, LANES), lambda i: (n - 1 - i, 0)),
                  pl.BlockSpec((ROW_TILE, LANES), lambda i: (n - 1 - i, off)), _whole((1, LANES))],
        out_specs=(pl.BlockSpec((ROW_TILE, LANES), lambda i: (n - 1 - i, 0)), _whole((1, LANES))),
        scratch_shapes=[pltpu.VMEM((1, LANES), F32)],
        compiler_params=_params(("arbitrary",)),
    )(dcum, zf, bias)


LOG2E = 1.4426950408889634
UNROLL_SIZES = (8, 4, 2)


def _bias_terms(ct, pad):
    H, L = ct.shape
    T = ROW_TILE
    n = L // T
    c2 = ct * LOG2E
    k2 = jnp.where(jnp.arange(L)[None, :] < pad, -NEG, c2)
    return dict(cq_row=c2.reshape(H, n, 1, T), ck_row=k2.reshape(H, n, 1, T), ref=c2[:, ::T].reshape(H, n, 1, 1))


def _exchange_copies(srcs, outs, send_sems, recv_sems, local_sems, scatter):
    nt = len(srcs)
    x, y, c = lax.axis_index("x"), lax.axis_index("y"), lax.axis_index("c")
    me = 4 * x + 2 * y + c
    copies = [pltpu.make_async_copy(srcs[t].at[me] if scatter else srcs[t], outs[t].at[me], local_sems.at[t])
              for t in range(nt)]
    for k in range(1, N_DEV):
        px = (x + (k >> 2)) % 2
        py = (y + ((k >> 1) & 1)) % 2
        pc = (c + (k & 1)) % 2
        peer = 4 * px + 2 * py + pc
        for t in range(nt):
            copies.append(pltpu.make_async_remote_copy(
                src_ref=srcs[t].at[peer] if scatter else srcs[t],
                dst_ref=outs[t].at[me],
                send_sem=send_sems.at[k - 1, t], recv_sem=recv_sems.at[k - 1, t],
                device_id=(px, py, pc), device_id_type=pl.DeviceIdType.MESH))
    return copies


def _exchange_shapes(tensors, scatter):
    out_shape = tuple(jax.ShapeDtypeStruct(t.shape if scatter else (N_DEV,) + t.shape, t.dtype) for t in tensors)
    nt = len(tensors)
    sems = [pltpu.SemaphoreType.DMA((N_DEV - 1, nt)), pltpu.SemaphoreType.DMA((N_DEV - 1, nt)),
            pltpu.SemaphoreType.DMA((nt,))]
    return out_shape, sems


def _side_exchange(side, srcs, outs, sems, first, last):
    if not side:
        return

    @pl.when(first)
    def _():
        for cp in _exchange_copies(srcs, outs, *sems, side[1]):
            cp.start()

    @pl.when(last)
    def _():
        for cp in _exchange_copies(srcs, outs, *sems, side[1]):
            cp.wait()


def _as_row(col):
    return jnp.transpose(jnp.broadcast_to(col, (col.shape[0], LANES)))[0:1, :]


def _as_col(row):
    return jnp.transpose(jnp.broadcast_to(row, (LANES, row.shape[1])))[:, 0:1]


def _attn_fwd(q, qoff, k, koff, v, voff, zsrc, bias, heads, dk, dv, scale, name, dead, side=None):
    L = q.shape[0]
    T = ROW_TILE
    n = L // T
    c = scale * LOG2E
    dn_qk = (((1,), (1,)), ((), ()))

    ns = len(side[0]) if side else 0

    def body(*refs):
        q_ref, k_ref, v_ref, z_ref, ck_ref, ref_ref = refs[:6]
        o_ref, lse_ref, y_ref = refs[6 + ns:9 + ns]
        s_a, s_b = refs[9 + 2 * ns:11 + 2 * ns]
        i = pl.program_id(1)
        _side_exchange(side, refs[6:6 + ns], refs[9 + ns:9 + 2 * ns], refs[11 + 2 * ns:],
                       (pl.program_id(0) == 0) & (i == 0), (pl.program_id(0) == heads - 1) & (i == n - 1))
        qb = q_ref[...]
        ref = ref_ref[...]

        def scores(j, dst):
            start = pl.multiple_of(j * T, T)
            dst[...] = lax.dot_general(qb, k_ref[pl.ds(start, T), :], dn_qk, preferred_element_type=F32)

        def update(t, vj, carry):
            m, l, acc = carry
            m_new = jnp.maximum(m, jnp.max(t, axis=1, keepdims=True))
            p = jnp.exp2(t - m_new)
            a = jnp.exp2(m - m_new)
            l = a * l + jnp.sum(p, axis=1, keepdims=True)
            acc = a * acc + jnp.dot(p.astype(_MXU), vj, preferred_element_type=F32)
            return m_new, l, acc

        def soft(j, carry, cur, diag):
            start = pl.multiple_of(j * T, T)
            t = cur[...] * c - (ck_ref[j] - ref)
            if diag:
                r = lax.broadcasted_iota(jnp.int32, (T, T), 0)
                cc = lax.broadcasted_iota(jnp.int32, (T, T), 1)
                t = jnp.where(cc <= r, t, NEG)
            return update(t, v_ref[pl.ds(start, T), :], carry)

        def head_tile(carry):
            t = lax.dot_general(qb, k_ref[dead:T, :], dn_qk, preferred_element_type=F32) * c
            return update(t - (ck_ref[0][:, dead:] - ref), v_ref[dead:T, :], carry)

        def step(j, carry, cur, nxt):
            scores(j + 1, nxt)
            return soft(j, carry, cur, False)

        first = jnp.where(i > 0, 1, 0) if dead else 0

        def run_of(j0, steps, carry):
            for u in range(0, steps, 2):
                carry = step(j0 + u + 1, step(j0 + u, carry, s_a, s_b), s_b, s_a)
            return carry

        scores(first, s_a)
        carry = (jnp.full((T, 1), -jnp.inf, F32), jnp.zeros((T, 1), F32), jnp.zeros((T, dv), F32))
        if dead:
            carry = lax.cond(i > 0, head_tile, lambda cy: cy, carry)
        done = first
        for size in UNROLL_SIZES:
            trips = (i - done) // size
            carry = lax.fori_loop(0, trips, lambda g, cy, d=done, s=size: run_of(d + s * g, s, cy), carry)
            done = done + size * trips
        odd = (i - done) == 1
        carry = lax.cond(odd, lambda cy: step(i - 1, cy, s_a, s_b), lambda cy: cy, carry)
        m, l, acc = lax.cond(odd, lambda cy: soft(i, cy, s_b, True), lambda cy: soft(i, cy, s_a, True), carry)
        o = acc / l
        o_ref[...] = o
        y_ref[...] = (o * _silu(z_ref[...])).astype(_MXU)
        lse_ref[...] = _as_row(m + jnp.log2(l))

    hbm = pl.BlockSpec(memory_space=pl.ANY)
    side_shapes, side_sems = _exchange_shapes(*side) if side else ((), [])
    o, lse, y, *side_out = pl.pallas_call(
        body, name=name,
        out_shape=(jax.ShapeDtypeStruct((L, heads * dv), F32), jax.ShapeDtypeStruct((heads, n, 1, T), F32),
                   jax.ShapeDtypeStruct((L, heads * dv), _MXU)) + side_shapes,
        grid=(heads, n),
        in_specs=[pl.BlockSpec((T, dk), lambda h, i: (i, qoff + h)),
                  pl.BlockSpec((L, dk), lambda h, i: (0, koff + h)),
                  pl.BlockSpec((L, dv), lambda h, i: (0, voff + h)),
                  pl.BlockSpec((T, dv), lambda h, i: (i, h)),
                  pl.BlockSpec((None, n, 1, T), lambda h, i: (h, 0, 0, 0)),
                  pl.BlockSpec((None, None, 1, 1), lambda h, i: (h, i, 0, 0))] + [hbm] * ns,
        out_specs=(pl.BlockSpec((T, dv), lambda h, i: (i, h)),
                   pl.BlockSpec((None, None, 1, T), lambda h, i: (h, i, 0, 0)),
                   pl.BlockSpec((T, dv), lambda h, i: (i, h))) + (hbm,) * ns,
        scratch_shapes=[pltpu.VMEM((T, T), F32), pltpu.VMEM((T, T), F32)] + side_sems,
        compiler_params=_params(("arbitrary", "arbitrary")),
    )(q, k, v, zsrc, bias["ck_row"], bias["ref"], *(side[0] if side else ()))
    return o, lse + (bias["cq_row"] - bias["ref"]), y, side_out


def _attn_bwd(q, qoff, k, koff, v, voff, do, bias, lse_row, delta_row, heads, dk, dv, scale, name, dead,
              side=None, bias_grads=True):
    L = q.shape[0]
    T = ROW_TILE
    n = L // T
    c = scale * LOG2E
    dn_nt = (((1,), (1,)), ((), ()))
    dn_tn = (((0,), (0,)), ((), ()))
    ns = len(side[0]) if side else 0

    def body(*refs):
        q_ref, k_ref, v_ref, do_ref, cq_ref, ck_ref, ref_ref, lse_ref, dl_ref = refs[:9]
        dq_hbm, dk_ref, dv_ref, dck_ref, dcq_ref = refs[9 + ns:14 + ns]
        dq_acc, sem, st_a, dp_a, st_b, dp_b = refs[14 + 2 * ns:20 + 2 * ns]
        h = pl.program_id(0)
        j = pl.program_id(1)
        _side_exchange(side, refs[9:9 + ns], refs[14 + ns:14 + 2 * ns], refs[20 + 2 * ns:],
                       (h == 0) & (j == 0), (h == heads - 1) & (j == n - 1))

        @pl.when(j == 0)
        def _():
            dq_acc[...] = jnp.zeros_like(dq_acc)
            dcq_ref[...] = jnp.zeros_like(dcq_ref)

        ref = ref_ref[...]

        def run(k0, R):
            kb = k_ref[k0:k0 + R, :]
            vb = v_ref[k0:k0 + R, :]
            bcol = _as_col(ck_ref[j][:, k0:k0 + R] - ref)

            def front(i, st_dst, dp_dst):
                start = pl.multiple_of(jnp.minimum(i, n - 1) * T, T)
                st_dst[0:R, :] = lax.dot_general(kb, q_ref[pl.ds(start, T), :], dn_nt, preferred_element_type=F32)
                dp_dst[0:R, :] = lax.dot_general(vb, do_ref[pl.ds(start, T), :], dn_nt, preferred_element_type=F32)

            def back(i, carry, st_cur, dp_cur, diag):
                dk_a, dv_a, dck_a = carry
                start = pl.multiple_of(i * T, T)
                qi = q_ref[pl.ds(start, T), :]
                doi = do_ref[pl.ds(start, T), :]
                arow = (cq_ref[i] - ref) - lse_ref[i]
                st = st_cur[0:R, :] * c + arow - bcol
                if diag:
                    r = lax.broadcasted_iota(jnp.int32, (R, T), 0) + k0
                    cc = lax.broadcasted_iota(jnp.int32, (R, T), 1)
                    st = jnp.where(r <= cc, st, NEG)
                pt = jnp.exp2(st)
                dv_a = dv_a + jnp.dot(pt.astype(_MXU), doi, preferred_element_type=F32)
                dst = pt * (dp_cur[0:R, :] - dl_ref[i])
                if bias_grads:
                    dck_a = dck_a - jnp.sum(dst, axis=1, keepdims=True)
                    dcq_ref[i] += jnp.sum(dst, axis=0, keepdims=True)
                dsb = (dst * scale).astype(_MXU)
                dk_a = dk_a + jnp.dot(dsb, qi, preferred_element_type=F32)
                dq_acc[pl.ds(start, T), :] += lax.dot_general(dsb, kb, dn_tn, preferred_element_type=F32)
                return dk_a, dv_a, dck_a

            buf_a, buf_b = (st_a, dp_a), (st_b, dp_b)

            def step(i, carry, cur, nxt, diag=False):
                front(i + 1, *nxt)
                return back(i, carry, *cur, diag)

            def run_of(i0, steps, carry):
                for u in range(0, steps, 2):
                    carry = step(i0 + u + 1, step(i0 + u, carry, buf_b, buf_a), buf_a, buf_b)
                return carry

            front(j, *buf_a)
            init = (jnp.zeros((R, dk), F32), jnp.zeros((R, dv), F32), jnp.zeros((R, 1), F32))
            carry = step(j, init, buf_a, buf_b, True)
            done = j + 1
            for size in UNROLL_SIZES:
                trips = (n - done) // size
                carry = lax.fori_loop(0, trips, lambda g, cy, d=done, s=size: run_of(d + s * g, s, cy), carry)
                done = done + size * trips
            dk_a, dv_a, dck_a = lax.cond((n - done) == 1, lambda cy: step(n - 1, cy, buf_b, buf_a),
                                         lambda cy: cy, carry)
            if k0:
                dk_ref[0:k0, :] = jnp.zeros((k0, dk), F32)
                dv_ref[0:k0, :] = jnp.zeros((k0, dv), F32)
                dck_ref[:, 0:k0] = jnp.zeros((1, k0), F32)
            dk_ref[k0:k0 + R, :] = dk_a
            dv_ref[k0:k0 + R, :] = dv_a
            dck_ref[:, k0:k0 + R] = _as_row(dck_a)

        if dead:
            pl.when(j == 0)(lambda: run(dead, T - dead))
            pl.when(j > 0)(lambda: run(0, T))
        else:
            run(0, T)

        @pl.when(j == n - 1)
        def _():
            cp = pltpu.make_async_copy(dq_acc, dq_hbm.at[:, pl.ds(pl.multiple_of(h * dk, dk), dk)], sem)
            cp.start()
            cp.wait()

    hbm = pl.BlockSpec(memory_space=pl.ANY)
    side_shapes, side_sems = _exchange_shapes(*side) if side else ((), [])
    dq, dk_, dv_, dck, dcq, *side_out = pl.pallas_call(
        body, name=name,
        out_shape=(jax.ShapeDtypeStruct((L, heads * dk), F32), jax.ShapeDtypeStruct((L, heads * dk), F32),
                   jax.ShapeDtypeStruct((L, heads * dv), F32), jax.ShapeDtypeStruct((heads, n, 1, T), F32),
                   jax.ShapeDtypeStruct((heads, n, 1, T), F32)) + side_shapes,
        grid=(heads, n),
        in_specs=[pl.BlockSpec((L, dk), lambda h, j: (0, qoff + h)),
                  pl.BlockSpec((T, dk), lambda h, j: (j, koff + h)),
                  pl.BlockSpec((T, dv), lambda h, j: (j, voff + h)),
                  pl.BlockSpec((L, dv), lambda h, j: (0, h)),
                  pl.BlockSpec((None, n, 1, T), lambda h, j: (h, 0, 0, 0)),
                  pl.BlockSpec((None, n, 1, T), lambda h, j: (h, 0, 0, 0)),
                  pl.BlockSpec((None, None, 1, 1), lambda h, j: (h, j, 0, 0)),
                  pl.BlockSpec((None, n, 1, T), lambda h, j: (h, 0, 0, 0)),
                  pl.BlockSpec((None, n, 1, T), lambda h, j: (h, 0, 0, 0))] + [hbm] * ns,
        out_specs=(hbm,
                   pl.BlockSpec((T, dk), lambda h, j: (j, h)),
                   pl.BlockSpec((T, dv), lambda h, j: (j, h)),
                   pl.BlockSpec((None, None, 1, T), lambda h, j: (h, j, 0, 0)),
                   pl.BlockSpec((None, n, 1, T), lambda h, j: (h, 0, 0, 0))) + (hbm,) * ns,
        scratch_shapes=[pltpu.VMEM((L, dk), F32), pltpu.SemaphoreType.DMA] + [pltpu.VMEM((T, T), F32)] * 4
        + side_sems,
        compiler_params=_params(("arbitrary", "arbitrary")),
    )(q, k, v, do, bias["cq_row"], bias["ck_row"], bias["ref"], lse_row, delta_row, *(side[0] if side else ()))
    return dq, dk_, dv_, dck, dcq, side_out


_CQ0, _CKV0, _KR0 = D_MODEL, D_MODEL + MLA_Q_LORA, D_MODEL + MLA_Q_LORA + MLA_KV_LORA
_MLA_PROJ = _KR0 + LANES


def _rms(x, g):
    ms = jnp.mean(x * x, axis=-1, keepdims=True)
    return x * lax.rsqrt(ms + NORM_EPS) * g


def _rms_bwd(x, g, dy):
    ms = jnp.mean(x * x, axis=-1, keepdims=True)
    r = lax.rsqrt(ms + NORM_EPS)
    xh = x * r
    dxh = dy * g
    dx = r * (dxh - xh * jnp.mean(dxh * xh, axis=-1, keepdims=True))
    return dx, jnp.sum(dy * xh, axis=0, keepdims=True)


def _mla_norm_fwd(proj, gq, gkv):
    L = proj.shape[0]

    def body(p_ref, gq_ref, gkv_ref, cq_ref, ckv_ref):
        cq_ref[...] = _rms(p_ref[:, _CQ0:_CKV0], gq_ref[...]).astype(_MXU)
        ckv_ref[...] = _rms(p_ref[:, _CKV0:_KR0], gkv_ref[...]).astype(_MXU)

    return pl.pallas_call(
        body, name="mla_norm_fwd",
        out_shape=(jax.ShapeDtypeStruct((L, MLA_Q_LORA), _MXU), jax.ShapeDtypeStruct((L, MLA_KV_LORA), _MXU)),
        grid=(L // ROW_TILE,),
        in_specs=[_rows(_MLA_PROJ), _whole((1, MLA_Q_LORA)), _whole((1, MLA_KV_LORA))],
        out_specs=(_rows(MLA_Q_LORA), _rows(MLA_KV_LORA)),
        compiler_params=_params(("parallel",)),
    )(proj, gq, gkv)


def _mla_norm_bwd(proj, gq, gkv, dcqn, dckvn):
    L = proj.shape[0]

    def body(p_ref, gq_ref, gkv_ref, dq_ref, dkv_ref, dcq_ref, dckv_ref, dgq_ref, dgkv_ref):
        @pl.when(pl.program_id(0) == 0)
        def _():
            dgq_ref[...] = jnp.zeros_like(dgq_ref)
            dgkv_ref[...] = jnp.zeros_like(dgkv_ref)

        dx, dg = _rms_bwd(p_ref[:, _CQ0:_CKV0], gq_ref[...], dq_ref[...])
        dcq_ref[...] = dx.astype(_MXU)
        dgq_ref[...] += dg
        dx, dg = _rms_bwd(p_ref[:, _CKV0:_KR0], gkv_ref[...], dkv_ref[...])
        dckv_ref[...] = dx.astype(_MXU)
        dgkv_ref[...] += dg

    return pl.pallas_call(
        body, name="mla_norm_bwd",
        out_shape=(jax.ShapeDtypeStruct((L, MLA_Q_LORA), _MXU), jax.ShapeDtypeStruct((L, MLA_KV_LORA), _MXU),
                   jax.ShapeDtypeStruct((1, MLA_Q_LORA), F32), jax.ShapeDtypeStruct((1, MLA_KV_LORA), F32)),
        grid=(L // ROW_TILE,),
        in_specs=[_rows(_MLA_PROJ), _whole((1, MLA_Q_LORA)), _whole((1, MLA_KV_LORA)),
                  _rows(MLA_Q_LORA), _rows(MLA_KV_LORA)],
        out_specs=(_rows(MLA_Q_LORA), _rows(MLA_KV_LORA), _whole((1, MLA_Q_LORA)), _whole((1, MLA_KV_LORA))),
        compiler_params=_params(("arbitrary",)),
    )(proj, gq, gkv, dcqn, dckvn)


def _rot_tile(t, cos, sa, sb):
    half = MLA_ROPE // 2
    return t * cos + pltpu.roll(t, LANES - half, 1) * sa + pltpu.roll(t, half, 1) * sb


def _mla_rope_fwd(qf, kv, proj, cos, sa, sb):
    L = qf.shape[0]
    H = MLA_HEADS

    def body(q_ref, kv_ref, p_ref, c_ref, sa_ref, sb_ref, qo_ref, ko_ref):
        cos, sa, sb = c_ref[...], sa_ref[...], sb_ref[...]
        kr = _rot_tile(p_ref[...], cos, sa, sb).astype(_MXU)
        for h in range(H):
            b = h * MLA_DK
            qo_ref[:, b:b + LANES] = q_ref[:, b:b + LANES].astype(_MXU)
            qo_ref[:, b + LANES:b + 2 * LANES] = _rot_tile(q_ref[:, b + LANES:b + 2 * LANES], cos, sa, sb).astype(_MXU)
            ko_ref[:, b:b + LANES] = kv_ref[:, h * LANES:(h + 1) * LANES]
            ko_ref[:, b + LANES:b + 2 * LANES] = kr

    W = H * MLA_DK
    return pl.pallas_call(
        body, name="mla_rope_fwd",
        out_shape=(jax.ShapeDtypeStruct((L, W), _MXU), jax.ShapeDtypeStruct((L, W), _MXU)),
        grid=(L // ROW_TILE,),
        in_specs=[_rows(W), _rows(kv.shape[1]), pl.BlockSpec((ROW_TILE, LANES), lambda i: (i, _KR0 // LANES)),
                  _rows(LANES), _rows(LANES), _rows(LANES)],
        out_specs=(_rows(W), _rows(W)),
        compiler_params=_params(("parallel",)),
    )(qf, kv, proj, cos, sa, sb)


def _mla_rope_bwd(dq_full, dk_full, cos, sa, sb):
    L = dq_full.shape[0]
    H = MLA_HEADS
    W = H * MLA_DK

    def body(dq_ref, dk_ref, c_ref, sa_ref, sb_ref, dqf_ref, dkn_ref, dkr_ref):
        cos, sa, sb = c_ref[...], sa_ref[...], sb_ref[...]
        lane = lax.broadcasted_iota(jnp.int32, (ROW_TILE, LANES), 1)
        live = lane < MLA_ROPE
        krs = jnp.zeros((ROW_TILE, LANES), F32)
        for h in range(H):
            b = h * MLA_DK
            dqf_ref[:, b:b + LANES] = dq_ref[:, b:b + LANES].astype(_MXU)
            dqr = _rot_tile(dq_ref[:, b + LANES:b + 2 * LANES], cos, sa, sb)
            dqf_ref[:, b + LANES:b + 2 * LANES] = jnp.where(live, dqr, 0.0).astype(_MXU)
            dkn_ref[:, h * LANES:(h + 1) * LANES] = dk_ref[:, b:b + LANES].astype(_MXU)
            krs = krs + dk_ref[:, b + LANES:b + 2 * LANES]
        dkr_ref[...] = jnp.where(live, _rot_tile(krs, cos, sa, sb), 0.0).astype(_MXU)

    return pl.pallas_call(
        body, name="mla_rope_bwd",
        out_shape=(jax.ShapeDtypeStruct((L, W), _MXU), jax.ShapeDtypeStruct((L, H * LANES), _MXU),
                   jax.ShapeDtypeStruct((L, LANES), _MXU)),
        grid=(L // ROW_TILE,),
        in_specs=[_rows(W), _rows(W), _rows(LANES), _rows(LANES), _rows(LANES)],
        out_specs=(_rows(W), _rows(H * LANES), _rows(LANES)),
        compiler_params=_params(("parallel",)),
    )(dq_full, dk_full, cos, sa, sb)


_RET_QKW = RET_HEADS * RET_QK


def _ret_rope_fwd(qkz, cos, sin, pad):
    L = qkz.shape[0]
    hh = RET_QK // 2
    kscale = RET_QK ** -0.5

    def body(p_ref, c_ref, s_ref, q_ref, k_ref):
        cos, sin = c_ref[...], s_ref[...]
        row = pl.program_id(0) * ROW_TILE + lax.broadcasted_iota(jnp.int32, (ROW_TILE, hh), 0)
        keep = row >= pad
        for h in range(RET_HEADS):
            b = h * RET_QK
            t1, t2 = p_ref[:, b:b + hh], p_ref[:, b + hh:b + 2 * hh]
            q_ref[:, b:b + hh] = (t1 * cos - t2 * sin).astype(_MXU)
            q_ref[:, b + hh:b + 2 * hh] = (t2 * cos + t1 * sin).astype(_MXU)
            t1, t2 = p_ref[:, _RET_QKW + b:_RET_QKW + b + hh], p_ref[:, _RET_QKW + b + hh:_RET_QKW + b + 2 * hh]
            k_ref[:, b:b + hh] = jnp.where(keep, (t1 * cos - t2 * sin) * kscale, 0.0)
            k_ref[:, b + hh:b + 2 * hh] = jnp.where(keep, (t2 * cos + t1 * sin) * kscale, 0.0)

    return pl.pallas_call(
        body, name="ret_rope_fwd",
        out_shape=(jax.ShapeDtypeStruct((L, _RET_QKW), _MXU), jax.ShapeDtypeStruct((L, _RET_QKW), F32)),
        grid=(L // ROW_TILE,),
        in_specs=[_rows(2 * _RET_QKW), _rows(hh), _rows(hh)],
        out_specs=(_rows(_RET_QKW), _rows(_RET_QKW)),
        compiler_params=_params(("parallel",)),
    )(qkz, cos, sin)


def _ret_rope_bwd(dqr, dkr, cos, sin, pad):
    L = dqr.shape[0]
    hh = RET_QK // 2
    kscale = RET_QK ** -0.5

    def body(dq_ref, dk_ref, c_ref, s_ref, oq_ref, ok_ref):
        cos, sin = c_ref[...], s_ref[...]
        row = pl.program_id(0) * ROW_TILE + lax.broadcasted_iota(jnp.int32, (ROW_TILE, hh), 0)
        keep = row >= pad
        for h in range(RET_HEADS):
            b = h * RET_QK
            d1, d2 = dq_ref[:, b:b + hh], dq_ref[:, b + hh:b + 2 * hh]
            oq_ref[:, b:b + hh] = (d1 * cos + d2 * sin).astype(_MXU)
            oq_ref[:, b + hh:b + 2 * hh] = (d2 * cos - d1 * sin).astype(_MXU)
            d1, d2 = dk_ref[:, b:b + hh], dk_ref[:, b + hh:b + 2 * hh]
            ok_ref[:, b:b + hh] = jnp.where(keep, (d1 * cos + d2 * sin) * kscale, 0.0).astype(_MXU)
            ok_ref[:, b + hh:b + 2 * hh] = jnp.where(keep, (d2 * cos - d1 * sin) * kscale, 0.0).astype(_MXU)

    return pl.pallas_call(
        body, name="ret_rope_bwd",
        out_shape=(jax.ShapeDtypeStruct((L, _RET_QKW), _MXU), jax.ShapeDtypeStruct((L, _RET_QKW), _MXU)),
        grid=(L // ROW_TILE,),
        in_specs=[_rows(_RET_QKW), _rows(_RET_QKW), _rows(hh), _rows(hh)],
        out_specs=(_rows(_RET_QKW), _rows(_RET_QKW)),
        compiler_params=_params(("parallel",)),
    )(dqr, dkr, cos, sin)


def _ret_rows(L):
    return next(r * ROW_TILE for r in (3, 2, 1) if L % (r * ROW_TILE) == 0)


def _ret_decays():
    lg = np.log1p(-np.exp2(-5.0 - np.arange(RET_HEADS, dtype=np.float32))).astype(np.float32)
    i = np.arange(CHUNK, dtype=np.float32)
    rel = i[:, None] - i[None, :]
    dm = np.where(rel[None] >= 0, np.exp(rel[None] * lg[:, None, None]), 0.0).astype(np.float32)
    qd = np.exp((i[None, :] + 1.0) * lg[:, None]).astype(np.float32)[:, :, None]
    kd = np.exp((CHUNK - 1.0 - i)[None, :] * lg[:, None]).astype(np.float32)[:, :, None]
    cd = np.broadcast_to(np.exp(CHUNK * lg).astype(np.float32)[:, None, None], (RET_HEADS, 1, RET_V))
    return jnp.asarray(dm), jnp.asarray(qd), jnp.asarray(kd), jnp.asarray(np.ascontiguousarray(cd))


def _ret_fwd(qr, kr, v):
    L = qr.shape[0]
    nc = L // CHUNK
    RB = _ret_rows(L)
    G = RB // CHUNK
    dm, qd, kd, cd = _ret_decays()
    dn_nt = (((1,), (1,)), ((), ()))
    dn_tn = (((0,), (0,)), ((), ()))

    def body(q_ref, k_ref, v_ref, dm_ref, qd_ref, kd_ref, cd_ref, o_ref, st_ref, s_ref):
        @pl.when(pl.program_id(1) == 0)
        def _():
            s_ref[...] = jnp.zeros_like(s_ref)

        state = s_ref[...]
        for g in range(G):
            rows = slice(g * CHUNK, (g + 1) * CHUNK)
            q, k, vv = q_ref[rows, :], k_ref[rows, :], v_ref[rows, :]
            sb = state.astype(_MXU)
            st_ref[g] = sb
            s = lax.dot_general(q, k.astype(_MXU), dn_nt, preferred_element_type=F32) * dm_ref[...]
            intra = jnp.dot(s.astype(_MXU), vv, preferred_element_type=F32)
            cross = jnp.dot(q, sb, preferred_element_type=F32) * qd_ref[...]
            o_ref[rows, :] = intra + cross
            kdm = (k * kd_ref[...]).astype(_MXU)
            state = cd_ref[...] * state + lax.dot_general(kdm, vv, dn_tn, preferred_element_type=F32)
        s_ref[...] = state

    return pl.pallas_call(
        body, name="ret_fwd",
        out_shape=(jax.ShapeDtypeStruct((L, RET_HEADS * RET_V), F32),
                   jax.ShapeDtypeStruct((RET_HEADS, nc, RET_QK, RET_V), _MXU)),
        grid=(RET_HEADS, nc // G),
        in_specs=[pl.BlockSpec((RB, RET_QK), lambda h, c: (c, h)),
                  pl.BlockSpec((RB, RET_QK), lambda h, c: (c, h)),
                  pl.BlockSpec((RB, RET_V), lambda h, c: (c, h)),
                  pl.BlockSpec((None, CHUNK, CHUNK), lambda h, c: (h, 0, 0)),
                  pl.BlockSpec((None, CHUNK, 1), lambda h, c: (h, 0, 0)),
                  pl.BlockSpec((None, CHUNK, 1), lambda h, c: (h, 0, 0)),
                  pl.BlockSpec((None, 1, RET_V), lambda h, c: (h, 0, 0))],
        out_specs=(pl.BlockSpec((RB, RET_V), lambda h, c: (c, h)),
                   pl.BlockSpec((None, G, RET_QK, RET_V), lambda h, c: (h, c, 0, 0))),
        scratch_shapes=[pltpu.VMEM((RET_QK, RET_V), F32)],
        compiler_params=_params(("parallel", "arbitrary")),
    )(qr, kr, v, dm, qd, kd, cd)


def _ret_bwd(qr, kr, v, do, states):
    L = qr.shape[0]
    nc = L // CHUNK
    RB = _ret_rows(L)
    G = RB // CHUNK
    nt = nc // G
    dm, qd, kd, cd = _ret_decays()
    dn_nt = (((1,), (1,)), ((), ()))
    dn_tn = (((0,), (0,)), ((), ()))

    def body(q_ref, k_ref, v_ref, do_ref, st_ref, dm_ref, qd_ref, kd_ref, cd_ref,
             dq_ref, dk_ref, dv_ref, ds_ref):
        @pl.when(pl.program_id(1) == 0)
        def _():
            ds_ref[...] = jnp.zeros_like(ds_ref)

        dstate = ds_ref[...]
        dmat = dm_ref[...]
        for g in reversed(range(G)):
            rows = slice(g * CHUNK, (g + 1) * CHUNK)
            q, k, vv, do = q_ref[rows, :], k_ref[rows, :], v_ref[rows, :], do_ref[rows, :]
            kb = k.astype(_MXU)
            dob = do.astype(_MXU)
            doq = (do * qd_ref[...]).astype(_MXU)
            dsb = dstate.astype(_MXU)
            s = (lax.dot_general(q, kb, dn_nt, preferred_element_type=F32) * dmat).astype(_MXU)
            draw = (lax.dot_general(dob, vv, dn_nt, preferred_element_type=F32) * dmat).astype(_MXU)
            kdm = (k * kd_ref[...]).astype(_MXU)
            dv = (lax.dot_general(s, dob, dn_tn, preferred_element_type=F32)
                  + jnp.dot(kdm, dsb, preferred_element_type=F32))
            dq = (jnp.dot(draw, kb, preferred_element_type=F32)
                  + lax.dot_general(doq, st_ref[g], dn_nt, preferred_element_type=F32))
            dk = (lax.dot_general(draw, q, dn_tn, preferred_element_type=F32)
                  + lax.dot_general(vv, dsb, dn_nt, preferred_element_type=F32) * kd_ref[...])
            dq_ref[rows, :] = dq
            dk_ref[rows, :] = dk
            dv_ref[rows, :] = dv.astype(_MXU)
            dstate = cd_ref[...] * dstate + lax.dot_general(q, doq, dn_tn, preferred_element_type=F32)
        ds_ref[...] = dstate

    rev = lambda h, c: (nt - 1 - c, h)
    return pl.pallas_call(
        body, name="ret_bwd",
        out_shape=(jax.ShapeDtypeStruct((L, _RET_QKW), F32), jax.ShapeDtypeStruct((L, _RET_QKW), F32),
                   jax.ShapeDtypeStruct((L, RET_HEADS * RET_V), _MXU)),
        grid=(RET_HEADS, nt),
        in_specs=[pl.BlockSpec((RB, RET_QK), rev),
                  pl.BlockSpec((RB, RET_QK), rev),
                  pl.BlockSpec((RB, RET_V), rev),
                  pl.BlockSpec((RB, RET_V), rev),
                  pl.BlockSpec((None, G, RET_QK, RET_V), lambda h, c: (h, nt - 1 - c, 0, 0)),
                  pl.BlockSpec((None, CHUNK, CHUNK), lambda h, c: (h, 0, 0)),
                  pl.BlockSpec((None, CHUNK, 1), lambda h, c: (h, 0, 0)),
                  pl.BlockSpec((None, CHUNK, 1), lambda h, c: (h, 0, 0)),
                  pl.BlockSpec((None, 1, RET_V), lambda h, c: (h, 0, 0))],
        out_specs=(pl.BlockSpec((RB, RET_QK), rev), pl.BlockSpec((RB, RET_QK), rev),
                   pl.BlockSpec((RB, RET_V), rev)),
        scratch_shapes=[pltpu.VMEM((RET_QK, RET_V), F32)],
        compiler_params=_params(("parallel", "arbitrary")),
    )(qr, kr, v, do, states, dm, qd, kd, cd)


def _gn(o, g):
    mu = jnp.mean(o, axis=-1, keepdims=True)
    oc = o - mu
    var = jnp.mean(oc * oc, axis=-1, keepdims=True)
    return oc * lax.rsqrt(var + NORM_EPS) * g


def _ret_gate_fwd(o, qkz, gn_g):
    L, W = o.shape
    zblk = qkz.shape[1] // W - 1

    def body(o_ref, z_ref, g_ref, y_ref):
        for h in range(RET_HEADS):
            sl = slice(h * RET_V, (h + 1) * RET_V)
            y_ref[:, sl] = (_gn(o_ref[:, sl], g_ref[:, sl]) * _silu(z_ref[:, sl])).astype(_MXU)

    return pl.pallas_call(
        body, name="ret_gate_fwd",
        out_shape=jax.ShapeDtypeStruct((L, W), _MXU),
        grid=(L // ROW_TILE,),
        in_specs=[_rows(W), pl.BlockSpec((ROW_TILE, W), lambda i: (i, zblk)), _whole((1, W))],
        out_specs=_rows(W),
        compiler_params=_params(("parallel",)),
    )(o, qkz, gn_g)


def _ret_gate_bwd(dy, o, qkz, gn_g):
    L, W = o.shape
    zblk = qkz.shape[1] // W - 1

    def body(dy_ref, o_ref, z_ref, g_ref, do_ref, dz_ref, dg_ref):
        @pl.when(pl.program_id(0) == 0)
        def _():
            dg_ref[...] = jnp.zeros_like(dg_ref)

        for h in range(RET_HEADS):
            sl = slice(h * RET_V, (h + 1) * RET_V)
            z = z_ref[:, sl]
            o = o_ref[:, sl]
            g = g_ref[:, sl]
            dy = dy_ref[:, sl]
            mu = jnp.mean(o, axis=-1, keepdims=True)
            oc = o - mu
            var = jnp.mean(oc * oc, axis=-1, keepdims=True)
            r = lax.rsqrt(var + NORM_EPS)
            xh = oc * r
            dn = dy * _silu(z)
            dz_ref[:, sl] = (dy * (xh * g) * _dsilu(z)).astype(_MXU)
            dg_ref[:, sl] += jnp.sum(dn * xh, axis=0, keepdims=True)
            dxh = dn * g
            do_ref[:, sl] = r * (dxh - jnp.mean(dxh, axis=-1, keepdims=True)
                                 - xh * jnp.mean(dxh * xh, axis=-1, keepdims=True))

    return pl.pallas_call(
        body, name="ret_gate_bwd",
        out_shape=(jax.ShapeDtypeStruct((L, W), F32), jax.ShapeDtypeStruct((L, W), _MXU),
                   jax.ShapeDtypeStruct((1, W), F32)),
        grid=(L // ROW_TILE,),
        in_specs=[_rows(W), _rows(W), pl.BlockSpec((ROW_TILE, W), lambda i: (i, zblk)), _whole((1, W))],
        out_specs=(_rows(W), _rows(W), _whole((1, W))),
        compiler_params=_params(("arbitrary",)),
    )(dy, o, qkz, gn_g)


def _dead_rows(pad):
    return pad // LANES * LANES


def _heads_t(a):
    return a[:, :FOX_HEADS].T


def _fox_fwd(h, w, lnw, pad, tag, side=None):
    w_in, b_f, w_out = w
    L = h.shape[0]
    W = D_MODEL
    w_qkv = w_in[:, :3 * W]
    w_zf = jnp.concatenate([w_in[:, 3 * W:4 * W], jnp.pad(w_in[:, 4 * W:], ((0, 0), (0, LANES - FOX_HEADS)))], axis=1)
    bias = jnp.pad(b_f, (0, LANES - FOX_HEADS))[None, :]
    qkv = _matmul(h, w_qkv, "nn", _MXU, f"{tag}_qkv")
    zf = _matmul(h, w_zf, "nn", F32, f"{tag}_zf")
    cum = _fox_cum(zf, bias, pad, f"{tag}_cum")
    ab = _bias_terms(_heads_t(cum), pad)
    nb = W // LANES
    o, lse, y, side_out = _attn_fwd(qkv, 0, qkv, nb, qkv, 2 * nb, zf, ab, FOX_HEADS, LANES, LANES, LANES ** -0.5,
                                    f"{tag}_attn_fwd", _dead_rows(pad), side)
    zl, hn = _out_ln(y, w_out, h, lnw[0], lnw[1], f"{tag}_out_ln")
    return hn, dict(h=h, qkv=qkv, zf=zf, ab=ab, o=o, lse=lse, y=y, zl=zl,
                    w_all=jnp.concatenate([w_qkv, w_zf], axis=1), w_out=w_out, bias=bias), side_out


def _fox_bwd(s, dzl, pad, tag, side=None, send_own=False):
    W = D_MODEL
    nb = W // LANES
    d_wout = _matmul(s["y"], dzl, "tn", F32, f"{tag}_dwout")
    if send_own:
        side = (list(side[0]) + [_split_layer("fox_w_out", d_wout)], True)
    do, delta, dzg = _gate_bwd(dzl, s["w_out"], s["o"], s["zf"], FOX_HEADS, f"{tag}_gate_bwd")
    qkv = s["qkv"]
    dq, dk, dv, dck, dcq, side_out = _attn_bwd(qkv, 0, qkv, nb, qkv, 2 * nb, do, s["ab"], s["lse"], delta,
                                               FOX_HEADS, LANES, LANES, LANES ** -0.5, f"{tag}_attn_bwd",
                                               _dead_rows(pad), side)
    dcum = jnp.pad((dck + dcq).reshape(FOX_HEADS, -1).T, ((0, 0), (0, LANES - FOX_HEADS)))
    df, dbias = _fox_cum_bwd(dcum, s["zf"], s["bias"], pad, f"{tag}_cum_bwd")
    dproj = jnp.concatenate([dq.astype(_MXU), dk.astype(_MXU), dv.astype(_MXU), dzg, df], axis=1)
    d_wall = _matmul(s["h"], dproj, "tn", F32, f"{tag}_dwin")
    d_win = jnp.concatenate([d_wall[:, :4 * W], d_wall[:, 4 * W:4 * W + FOX_HEADS]], axis=1)
    if send_own:
        dh, win_parts = _matmul(dproj, s["w_all"], "nt", F32, f"{tag}_dh", ([_split_layer("fox_w_in", d_win)], True))
        side_out = list(side_out) + list(win_parts)
    else:
        dh = _matmul(dproj, s["w_all"], "nt", F32, f"{tag}_dh")
    return dh, (d_win, dbias[0, :FOX_HEADS], d_wout), side_out


def _mla_tables(L, pad):
    pos = (jnp.arange(L) - pad).astype(F32)
    inv = ROPE_BASE ** (-jnp.arange(0, MLA_ROPE, 2, dtype=F32) / MLA_ROPE)
    ang = pos[:, None] * inv[None, :]
    c, s = jnp.cos(ang), jnp.sin(ang)
    z32, z64 = jnp.zeros_like(s), jnp.zeros((L, LANES - MLA_ROPE), F32)
    cos = jnp.concatenate([c, c, jnp.ones_like(z64)], axis=1)
    sa = jnp.concatenate([-s, z32, z64], axis=1)
    sb = jnp.concatenate([z32, s, z64], axis=1)
    return cos, sa, sb


def _mla_weights(w_in, w_uq, w_ukv):
    a, b = MLA_Q_LORA, MLA_Q_LORA + MLA_KV_LORA
    w_zc = jnp.concatenate([w_in[:, b + MLA_ROPE:], w_in[:, :b + MLA_ROPE],
                            jnp.zeros((D_MODEL, LANES - MLA_ROPE), w_in.dtype)], axis=1)
    uq = w_uq.reshape(MLA_Q_LORA, MLA_HEADS, MLA_NOPE + MLA_ROPE)
    uq = jnp.pad(uq, ((0, 0), (0, 0), (0, MLA_DK - MLA_NOPE - MLA_ROPE))).reshape(MLA_Q_LORA, MLA_HEADS * MLA_DK)
    ukv = w_ukv.reshape(MLA_KV_LORA, MLA_HEADS, MLA_NOPE + MLA_V)
    ukv = jnp.concatenate([ukv[:, :, :MLA_NOPE].reshape(MLA_KV_LORA, -1), ukv[:, :, MLA_NOPE:].reshape(MLA_KV_LORA, -1)], axis=1)
    return w_zc, uq, ukv


def _mla_fwd(h, w, lnw, pad):
    w_in, gq, gkv, w_uq, w_ukv, w_out = w
    L = h.shape[0]
    w_zc, uq, ukv = _mla_weights(w_in, w_uq, w_ukv)
    cos, sa, sb = _mla_tables(L, pad)
    proj = _matmul(h, w_zc, "nn", F32, "mla_proj")
    cqn, ckvn = _mla_norm_fwd(proj, gq[None, :], gkv[None, :])
    qf = _matmul(cqn, uq, "nn", F32, "mla_uq")
    kv = _matmul(ckvn, ukv, "nn", _MXU, "mla_ukv")
    q_full, k_full = _mla_rope_fwd(qf, kv, proj, cos, sa, sb)
    ab = _bias_terms(jnp.zeros((MLA_HEADS, L), F32), pad)
    nb = MLA_HEADS * MLA_NOPE // LANES
    o, lse, y, _ = _attn_fwd(q_full, 0, k_full, 0, kv, nb, proj, ab, MLA_HEADS, MLA_DK, MLA_V,
                             (MLA_NOPE + MLA_ROPE) ** -0.5, "mla_attn_fwd", _dead_rows(pad))
    zl, hn = _out_ln(y, w_out, h, lnw[0], lnw[1], "mla_out_ln")
    return hn, dict(h=h, proj=proj, cqn=cqn, ckvn=ckvn, kv=kv, q_full=q_full, k_full=k_full, ab=ab,
                    dead=_dead_rows(pad),
                    o=o, lse=lse, y=y, zl=zl, w_zc=w_zc, uq=uq, ukv=ukv, w_out=w_out, gq=gq, gkv=gkv,
                    tabs=(cos, sa, sb))


def _mla_bwd(s, dzl):
    H = MLA_HEADS
    d_wout = _matmul(s["y"], dzl, "tn", F32, "mla_dwout")
    do, delta, dzg = _gate_bwd(dzl, s["w_out"], s["o"], s["proj"], H, "mla_gate_bwd")
    nb = H * MLA_NOPE // LANES
    dq_full, dk_full, dv, _, _, _ = _attn_bwd(s["q_full"], 0, s["k_full"], 0, s["kv"], nb, do, s["ab"],
                                              s["lse"], delta,
                                              H, MLA_DK, MLA_V, (MLA_NOPE + MLA_ROPE) ** -0.5, "mla_attn_bwd",
                                              s["dead"], bias_grads=False)
    cos, sa, sb = s["tabs"]
    dqf, dkn, dkr = _mla_rope_bwd(dq_full, dk_full, cos, -sa, -sb)
    d_uq = _matmul(s["cqn"], dqf, "tn", F32, "mla_duq")
    dcqn = _matmul(dqf, s["uq"], "nt", F32, "mla_dcqn")
    dkv = jnp.concatenate([dkn, dv.astype(_MXU)], axis=1)
    d_ukv = _matmul(s["ckvn"], dkv, "tn", F32, "mla_dukv")
    dckvn = _matmul(dkv, s["ukv"], "nt", F32, "mla_dckvn")
    dcq, dckv, dgq, dgkv = _mla_norm_bwd(s["proj"], s["gq"][None, :], s["gkv"][None, :], dcqn, dckvn)
    dproj = jnp.concatenate([dzg, dcq, dckv, dkr], axis=1)
    d_wzc = _matmul(s["h"], dproj, "tn", F32, "mla_dwin")
    dh = _matmul(dproj, s["w_zc"], "nt", F32, "mla_dh")
    W = D_MODEL
    d_win = jnp.concatenate([d_wzc[:, W:W + MLA_Q_LORA + MLA_KV_LORA + MLA_ROPE], d_wzc[:, :W]], axis=1)
    d_wuq = d_uq.reshape(MLA_Q_LORA, H, MLA_DK)[:, :, :MLA_NOPE + MLA_ROPE].reshape(MLA_Q_LORA, -1)
    hk = H * MLA_NOPE
    d_wukv = jnp.concatenate([d_ukv[:, :hk].reshape(MLA_KV_LORA, H, MLA_NOPE),
                              d_ukv[:, hk:].reshape(MLA_KV_LORA, H, MLA_V)], axis=2).reshape(MLA_KV_LORA, -1)
    return dh, (d_win, dgq[0], dgkv[0], d_wuq, d_wukv, d_wout)


def _ret_tables(L, pad):
    pos = (jnp.arange(L) - pad).astype(F32)
    inv = 1.0 / (ROPE_BASE ** jnp.linspace(0.0, 1.0, RET_QK // 2, dtype=F32))
    ang = pos[:, None] * inv[None, :]
    return jnp.cos(ang), jnp.sin(ang)


def _ret_layer_fwd(h, w, lnw, pad):
    w_in, gn_g, w_out = w
    L = h.shape[0]
    q1 = 2 * _RET_QKW
    v1 = q1 + RET_HEADS * RET_V
    w_qkz = jnp.concatenate([w_in[:, :q1], w_in[:, v1:]], axis=1)
    cos, sin = _ret_tables(L, pad)
    qkz = _matmul(h, w_qkz, "nn", F32, "ret_qkz")
    v = _matmul(h, w_in[:, q1:v1], "nn", _MXU, "ret_v")
    qr, kr = _ret_rope_fwd(qkz, cos, sin, pad)
    o, states = _ret_fwd(qr, kr, v)
    y = _ret_gate_fwd(o, qkz, gn_g[None, :])
    zl, hn = _out_ln(y, w_out, h, lnw[0], lnw[1], "ret_out_ln")
    return hn, dict(h=h, qkz=qkz, v=v, qr=qr, kr=kr, o=o, states=states, y=y, zl=zl, w_in=w_in, w_out=w_out,
                    gn_g=gn_g, tabs=(cos, sin))


def _ret_layer_bwd(s, dzl, pad):
    dy = _matmul(dzl, s["w_out"], "nt", F32, "ret_dy")
    d_wout = _matmul(s["y"], dzl, "tn", F32, "ret_dwout")
    do, dzg, dgn = _ret_gate_bwd(dy, s["o"], s["qkz"], s["gn_g"][None, :])
    dqr, dkr, dv = _ret_bwd(s["qr"], s["kr"], s["v"], do, s["states"])
    cos, sin = s["tabs"]
    dq, dk = _ret_rope_bwd(dqr, dkr, cos, sin, pad)
    dproj = jnp.concatenate([dq, dk, dv, dzg], axis=1)
    d_win = _matmul(s["h"], dproj, "tn", F32, "ret_dwin")
    dh = _matmul(dproj, s["w_in"], "nt", F32, "ret_dh")
    return dh, (d_win, dgn[0], d_wout)


def _local_step(x, target, meta, fox0_w, late_shards, small, ln_g, ln_b):
    n_real = x.shape[0]
    first = -(-N_META // ROW_TILE)
    pad = first * ROW_TILE - N_META
    h0 = jnp.concatenate([jnp.zeros((pad, D_MODEL), F32), meta, x], axis=0)
    lnw = [(ln_g[i][None, :], ln_b[i][None, :]) for i in range(4)]
    b_f = small["fox_b_f"]

    h1, s0, gathered = _fox_fwd(h0, (fox0_w[0], b_f[0], fox0_w[1]), lnw[0], pad, "fox0", (late_shards, False))
    w = {key: _merge_layer(key[0], g).astype(_MXU) for key, g in zip(_LATE, gathered)}
    h2, s1 = _mla_fwd(h1, (w["mla_w_in", 0], small["mla_q_norm"][0], small["mla_kv_norm"][0], w["mla_w_uq", 0],
                           w["mla_w_ukv", 0], w["mla_w_out", 0]), lnw[1], pad)
    h3, s2 = _ret_layer_fwd(h2, (w["ret_w_in", 0], small["ret_gn_g"][0], w["ret_w_out", 0]), lnw[2], pad)
    h4, s3, _ = _fox_fwd(h3, (w["fox_w_in", 1], b_f[1], w["fox_w_out", 1]), lnw[3], pad, "fox1")

    dh4, loss = _loss_head(h4, target, first)
    dz3, dg3, db3 = _ln_bwd(s3["zl"], lnw[3][0], dh4, None, "ln3_bwd")
    dm3, gf1, _ = _fox_bwd(s3, dz3, pad, "fox1")
    dz2, dg2, db2 = _ln_bwd(s2["zl"], lnw[2][0], dz3, dm3, "ln2_bwd")
    dm2, gr = _ret_layer_bwd(s2, dz2, pad)
    dz1, dg1, db1 = _ln_bwd(s1["zl"], lnw[1][0], dz2, dm2, "ln1_bwd")
    dm1, gm = _mla_bwd(s1, dz1)
    dz0, dg0, db0 = _ln_bwd(s0["zl"], lnw[0][0], dz1, dm1, "ln0_bwd")
    late = {("fox_w_in", 1): gf1[0], ("fox_w_out", 1): gf1[2], ("mla_w_in", 0): gm[0], ("mla_w_uq", 0): gm[3],
            ("mla_w_ukv", 0): gm[4], ("mla_w_out", 0): gm[5], ("ret_w_in", 0): gr[0], ("ret_w_out", 0): gr[2]}
    dm0, gf0, sent = _fox_bwd(s0, dz0, pad, "fox0", ([_split_layer(k[0], late[k]) for k in _LATE], True), True)
    parts = dict(zip(_LATE + [("fox_w_out", 0), ("fox_w_in", 0)], sent))
    grad_x, grad_meta = _input_grads(dz0, dm0, first, n_real)

    small_grads = dict(
        meta=grad_meta, fox_b_f=jnp.stack([gf0[1], gf1[1]]), mla_q_norm=gm[1][None], mla_kv_norm=gm[2][None],
        ret_gn_g=gr[1][None], ln_g=jnp.concatenate([dg0, dg1, dg2, dg3], axis=0),
        ln_b=jnp.concatenate([db0, db1, db2, db3], axis=0))
    return loss, grad_x, parts, small_grads


def _exchange(tensors, scatter, name):
    nt = len(tensors)

    def body(*refs):
        copies = _exchange_copies(refs[:nt], refs[nt:2 * nt], *refs[2 * nt:], scatter)
        for cp in copies:
            cp.start()
        for cp in copies:
            cp.wait()

    hbm = pl.BlockSpec(memory_space=pl.ANY)
    out_shape, sems = _exchange_shapes(tensors, scatter)
    return pl.pallas_call(
        body, name=name, out_shape=out_shape, in_specs=[hbm] * nt, out_specs=(hbm,) * nt, scratch_shapes=sems,
    )(*tensors)


def _adamw(parts, w, m, v, name):
    R, C = w.shape
    tr = next(t for t in range(min(R, 256), 0, -1) if R % t == 0 and (t % 8 == 0 or t == R))

    def body(p_ref, w_ref, m_ref, v_ref, g_ref, d_ref, mo_ref, vo_ref):
        g = p_ref[0]
        for d in range(1, N_DEV):
            g = g + p_ref[d]
        mn = ADAM_B1 * m_ref[...] + (1.0 - ADAM_B1) * g
        vn = ADAM_B2 * v_ref[...] + (1.0 - ADAM_B2) * (g * g)
        m_hat = mn / (1.0 - ADAM_B1 ** ADAM_STEP)
        v_hat = vn / (1.0 - ADAM_B2 ** ADAM_STEP)
        g_ref[...] = g
        d_ref[...] = -ADAM_LR * (m_hat / (jnp.sqrt(v_hat) + ADAM_EPS) + ADAM_WD * w_ref[...])
        mo_ref[...] = mn
        vo_ref[...] = vn

    blk = pl.BlockSpec((tr, C), lambda i: (i, 0))
    sd = jax.ShapeDtypeStruct((R, C), F32)
    return pl.pallas_call(
        body, name=name,
        out_shape=(sd, sd, sd, sd),
        grid=(R // tr,),
        in_specs=[pl.BlockSpec((N_DEV, tr, C), lambda i: (0, i, 0)), blk, blk, blk],
        out_specs=(blk, blk, blk, blk),
        compiler_params=_params(("parallel",)),
    )(parts, w, m, v)


_SPECS = dict(
    meta=((16, 1024), 1),
    fox_w_in=((2, 1024, 4104), 2), fox_b_f=((2, 8), None), fox_w_out=((2, 1024, 1024), 1),
    mla_w_in=((1, 1024, 1728), 2), mla_q_norm=((1, 384), None), mla_kv_norm=((1, 256), None),
    mla_w_uq=((1, 384, 1536), 2), mla_w_ukv=((1, 256, 2048), 2), mla_w_out=((1, 1024, 1024), 1),
    ret_w_in=((1, 1024, 6144), 2), ret_gn_g=((1, 2048), 1), ret_w_out=((1, 2048, 1024), 1),
    ln_g=((4, 1024), None), ln_b=((4, 1024), None),
)
_NAMES = list(_SPECS)
_BIG = ["fox_w_in", "fox_w_out", "mla_w_in", "mla_w_uq", "mla_w_ukv", "mla_w_out", "ret_w_in", "ret_w_out"]
_SMALL = [n for n in _NAMES if n not in _BIG]
_SMALL_GATHERED = ["meta", "ret_gn_g"]
_PACK_ROWS = 8


def _shard_shape(name):
    shape, ax = _SPECS[name]
    if ax is None:
        return shape
    return tuple(s // N_DEV if i == ax else s for i, s in enumerate(shape))


_EARLY = [("fox_w_in", 0), ("fox_w_out", 0)]
_LATE = [("fox_w_in", 1), ("fox_w_out", 1), ("mla_w_in", 0), ("mla_w_uq", 0), ("mla_w_ukv", 0), ("mla_w_out", 0),
         ("ret_w_in", 0), ("ret_w_out", 0)]


def _split_at(full, ax):
    shape = full.shape
    parts = full.reshape(shape[:ax] + (N_DEV, shape[ax] // N_DEV) + shape[ax + 1:])
    return jnp.moveaxis(parts, ax, 0)


def _merge_at(stacked, ax):
    full = jnp.moveaxis(stacked, 0, ax)
    return full.reshape(full.shape[:ax] + (-1,) + full.shape[ax + 2:])


def _split(name, full):
    return _split_at(full, _SPECS[name][1])


def _merge(name, stacked):
    return _merge_at(stacked, _SPECS[name][1])


def _split_layer(name, full):
    return _split_at(full, _SPECS[name][1] - 1)


def _merge_layer(name, stacked):
    return _merge_at(stacked, _SPECS[name][1] - 1)


def _pad_rows(flat):
    n = flat.shape[-1]
    unit = _PACK_ROWS * LANES
    total = -(-n // unit) * unit
    flat = jnp.pad(flat, [(0, 0)] * (flat.ndim - 1) + [(0, total - n)])
    return flat.reshape(flat.shape[:-1] + (total // LANES, LANES))


def _pack_small(tree, names, tail):
    return _pad_rows(jnp.concatenate([tree[n].reshape(-1) for n in names] + [tail]))


def _unpack_small(buf, names):
    flat, out, at = buf.reshape(-1), {}, 0
    for n in names:
        shp = _shard_shape(n)
        out[n] = flat[at:at + math.prod(shp)].reshape(shp)
        at += math.prod(shp)
    return out, flat[at:]


def kernel(x, meta, fox_w_in, fox_b_f, fox_w_out, mla_w_in, mla_q_norm, mla_kv_norm, mla_w_uq, mla_w_ukv, mla_w_out, ret_w_in, ret_gn_g, ret_w_out, ln_g, ln_b, loss_target, m_meta, m_fox_w_in, m_fox_b_f, m_fox_w_out, m_mla_w_in, m_mla_q_norm, m_mla_kv_norm, m_mla_w_uq, m_mla_w_ukv, m_mla_w_out, m_ret_w_in, m_ret_gn_g, m_ret_w_out, m_ln_g, m_ln_b, v_meta, v_fox_w_in, v_fox_b_f, v_fox_w_out, v_mla_w_in, v_mla_q_norm, v_mla_kv_norm, v_mla_w_uq, v_mla_w_ukv, v_mla_w_out, v_ret_w_in, v_ret_gn_g, v_ret_w_out, v_ln_g, v_ln_b):
    w = dict(meta=meta, fox_w_in=fox_w_in, fox_b_f=fox_b_f, fox_w_out=fox_w_out, mla_w_in=mla_w_in,
             mla_q_norm=mla_q_norm, mla_kv_norm=mla_kv_norm, mla_w_uq=mla_w_uq, mla_w_ukv=mla_w_ukv,
             mla_w_out=mla_w_out, ret_w_in=ret_w_in, ret_gn_g=ret_gn_g, ret_w_out=ret_w_out, ln_g=ln_g, ln_b=ln_b)
    m = dict(meta=m_meta, fox_w_in=m_fox_w_in, fox_b_f=m_fox_b_f, fox_w_out=m_fox_w_out, mla_w_in=m_mla_w_in,
             mla_q_norm=m_mla_q_norm, mla_kv_norm=m_mla_kv_norm, mla_w_uq=m_mla_w_uq, mla_w_ukv=m_mla_w_ukv,
             mla_w_out=m_mla_w_out, ret_w_in=m_ret_w_in, ret_gn_g=m_ret_gn_g, ret_w_out=m_ret_w_out, ln_g=m_ln_g, ln_b=m_ln_b)
    v = dict(meta=v_meta, fox_w_in=v_fox_w_in, fox_b_f=v_fox_b_f, fox_w_out=v_fox_w_out, mla_w_in=v_mla_w_in,
             mla_q_norm=v_mla_q_norm, mla_kv_norm=v_mla_kv_norm, mla_w_uq=v_mla_w_uq, mla_w_ukv=v_mla_w_ukv,
             mla_w_out=v_mla_w_out, ret_w_in=v_ret_w_in, ret_gn_g=v_ret_gn_g, ret_w_out=v_ret_w_out, ln_g=v_ln_g, ln_b=v_ln_b)

    def shard16(key):
        return w[key[0]][key[1]].astype(jnp.bfloat16)

    none = jnp.zeros((0,), F32)
    gathered = _exchange([shard16(k) for k in _EARLY] + [_pack_small(w, _SMALL_GATHERED, none)],
                         False, "weights_all_gather")
    fox0_w = [_merge_layer(k[0], g).astype(_MXU) for k, g in zip(_EARLY, gathered)]
    small = {n: w[n] for n in _SMALL}
    pieces = [_unpack_small(gathered[-1][d], _SMALL_GATHERED)[0] for d in range(N_DEV)]
    for n in _SMALL_GATHERED:
        small[n] = _merge(n, jnp.stack([p[n] for p in pieces]))

    loss, grad_x, by_key, grads = _local_step(
        x[0], loss_target[0], small["meta"], fox0_w, [shard16(k) for k in _LATE], small, ln_g, ln_b)

    def small_rows(n):
        if _SPECS[n][1] is None:
            return jnp.broadcast_to(grads[n].reshape(1, -1), (N_DEV, grads[n].size))
        return _split(n, grads[n]).reshape(N_DEV, -1)

    small_out = _pad_rows(jnp.concatenate([small_rows(n) for n in _SMALL]
                                          + [jnp.broadcast_to(loss.reshape(1, 1), (N_DEV, 1))], axis=1))
    parts = _exchange([small_out], True, "grads_all_to_all")

    def rows2d(a):
        return a.reshape(-1, a.shape[-1])

    out = {}
    for n in _BIG:
        layers = [by_key[n, i] for i in range(_SPECS[n][0][0])]
        p = layers[0] if len(layers) == 1 else jnp.stack(layers, axis=1)
        res = _adamw(p.reshape(N_DEV, -1, p.shape[-1]), rows2d(w[n]), rows2d(m[n]), rows2d(v[n]), f"adamw_{n}")
        out[n] = [r.reshape(w[n].shape) for r in res]
    zero1 = jnp.zeros((1,), F32)
    res = _adamw(parts[-1], _pack_small(w, _SMALL, zero1), _pack_small(m, _SMALL, zero1),
                 _pack_small(v, _SMALL, zero1), "adamw_small")
    small_res = [_unpack_small(r, _SMALL) for r in res]
    for n in _SMALL:
        out[n] = [s[0][n] for s in small_res]
    total_loss = small_res[0][1][0]
    return (total_loss, grad_x[None], *[out[n][0] for n in _NAMES], *[out[n][1] for n in _NAMES],
            *[out[n][2] for n in _NAMES], *[out[n][3] for n in _NAMES])
```

```python
import math

import numpy as np
import jax
import jax.numpy as jnp
from jax import lax
from jax.experimental import pallas as pl
from jax.experimental.pallas import tpu as pltpu

F32 = jnp.float32
_MXU = jnp.bfloat16

N_DEV = 8
N_META = 16
D_MODEL = 1024
ROW_TILE = 512
LANES = 128
CHUNK = 128
NEG = -1e30
V7X_VMEM_BYTES = 64 * 1024 * 1024
VMEM_LIMIT = V7X_VMEM_BYTES - 4 * 1024 * 1024

FOX_HEADS = 8
MLA_HEADS = 8
MLA_NOPE, MLA_ROPE, MLA_V = 128, 64, 128
MLA_Q_LORA, MLA_KV_LORA = 384, 256
MLA_DK = 256
RET_HEADS, RET_QK, RET_V = 4, 256, 512
ROPE_BASE = 10000.0
ALPHA = (2 * 4) ** 0.25
NORM_EPS = 1e-5

ADAM_LR, ADAM_B1, ADAM_B2, ADAM_EPS, ADAM_WD, ADAM_STEP = 0.001, 0.9, 0.999, 1e-08, 0.01, 10


def _params(sem, vmem=VMEM_LIMIT):
    return pltpu.CompilerParams(dimension_semantics=sem, vmem_limit_bytes=vmem)


def _tile(n, pref):
    if n <= pref:
        return n
    t = (pref // LANES) * LANES
    while n % t:
        t -= LANES
    return t


_MM_TILES = {"nn": (512, 4224, 2048), "nt": (512, 2048, 6144), "tn": (1024, 2048, 1536)}


def _matmul(a, b, mode, out_dtype, name, side=None):
    if mode == "nn":
        (M, K), (K2, N) = a.shape, b.shape
    elif mode == "nt":
        (M, K), (N, K2) = a.shape, b.shape
    else:
        (K, M), (K2, N) = a.shape, b.shape
    assert K == K2, (a.shape, b.shape, mode)
    tm, tn, tk = (_tile(d, p) for d, p in zip((M, N, K), _MM_TILES[mode]))
    nk = K // tk
    if mode == "nn":
        a_spec = pl.BlockSpec((tm, tk), lambda i, j, k: (i, k))
        b_spec = pl.BlockSpec((tk, tn), lambda i, j, k: (k, j))
        dn = (((1,), (0,)), ((), ()))
    elif mode == "nt":
        a_spec = pl.BlockSpec((tm, tk), lambda i, j, k: (i, k))
        b_spec = pl.BlockSpec((tn, tk), lambda i, j, k: (j, k))
        dn = (((1,), (1,)), ((), ()))
    else:
        a_spec = pl.BlockSpec((tk, tm), lambda i, j, k: (k, i))
        b_spec = pl.BlockSpec((tk, tn), lambda i, j, k: (k, j))
        dn = (((0,), (0,)), ((), ()))

    ns = len(side[0]) if side else 0
    grid = (M // tm, N // tn, nk)
    n_acc = 1 if nk > 1 else 0

    def body(*refs):
        a_ref, b_ref = refs[:2]
        o_ref = refs[2 + ns]
        k = pl.program_id(2)
        if side:
            i, j = pl.program_id(0), pl.program_id(1)
            _side_exchange(side, refs[2:2 + ns], refs[3 + ns:3 + 2 * ns], refs[3 + 2 * ns + n_acc:],
                           (i == 0) & (j == 0) & (k == 0),
                           (i == grid[0] - 1) & (j == grid[1] - 1) & (k == nk - 1))
        part = lax.dot_general(a_ref[...].astype(_MXU), b_ref[...].astype(_MXU), dn,
                               preferred_element_type=F32)
        if nk == 1:
            o_ref[...] = part.astype(out_dtype)
            return
        acc_ref = refs[3 + 2 * ns]

        @pl.when(k == 0)
        def _():
            acc_ref[...] = part

        @pl.when(k > 0)
        def _():
            acc_ref[...] += part

        @pl.when(k == nk - 1)
        def _():
            o_ref[...] = acc_ref[...].astype(out_dtype)

    hbm = pl.BlockSpec(memory_space=pl.ANY)
    side_shapes, side_sems = _exchange_shapes(*side) if side else ((), [])
    out, *side_out = pl.pallas_call(
        body, name=name,
        out_shape=(jax.ShapeDtypeStruct((M, N), out_dtype),) + side_shapes,
        grid=grid,
        in_specs=[a_spec, b_spec] + [hbm] * ns,
        out_specs=(pl.BlockSpec((tm, tn), lambda i, j, k: (i, j)),) + (hbm,) * ns,
        scratch_shapes=[pltpu.VMEM((tm, tn), F32)] * n_acc + side_sems,
        compiler_params=_params(("arbitrary",) * 3 if side else ("parallel", "parallel", "arbitrary")),
    )(a, b, *(side[0] if side else ()))
    return (out, side_out) if side else out


def _rows(w, off=0):
    return pl.BlockSpec((ROW_TILE, w), lambda i: (i + off, 0))


def _whole(shape):
    return pl.BlockSpec(shape, lambda i: (0,) * len(shape))


def _silu(z):
    return z * jax.nn.sigmoid(z)


def _dsilu(z):
    s = jax.nn.sigmoid(z)
    return s * (1.0 + z * (1.0 - s))


def _out_ln(y, w_out, h, g, b, name):
    L, D = h.shape
    K = y.shape[1]

    def body(y_ref, w_ref, h_ref, g_ref, b_ref, z_ref, o_ref):
        mo = jnp.dot(y_ref[...].astype(_MXU), w_ref[...].astype(_MXU), preferred_element_type=F32)
        z = ALPHA * h_ref[...] + mo
        mu = jnp.mean(z, axis=-1, keepdims=True)
        zc = z - mu
        var = jnp.mean(zc * zc, axis=-1, keepdims=True)
        z_ref[...] = z
        o_ref[...] = zc * lax.rsqrt(var + NORM_EPS) * g_ref[...] + b_ref[...]

    return pl.pallas_call(
        body, name=name,
        out_shape=(jax.ShapeDtypeStruct((L, D), F32), jax.ShapeDtypeStruct((L, D), F32)),
        grid=(L // ROW_TILE,),
        in_specs=[_rows(K), _whole((K, D)), _rows(D), _whole((1, D)), _whole((1, D))],
        out_specs=(_rows(D), _rows(D)),
        compiler_params=_params(("parallel",)),
    )(y, w_out, h, g, b)


def _ln_bwd(zl, g, ga, gb, name):
    L, D = zl.shape
    two = gb is not None

    def body(*refs):
        if two:
            z_ref, g_ref, ga_ref, gb_ref, dz_ref, dg_ref, db_ref = refs
            gout = ALPHA * ga_ref[...] + gb_ref[...]
        else:
            z_ref, g_ref, ga_ref, dz_ref, dg_ref, db_ref = refs
            gout = ga_ref[...]
        z = z_ref[...]
        mu = jnp.mean(z, axis=-1, keepdims=True)
        zc = z - mu
        var = jnp.mean(zc * zc, axis=-1, keepdims=True)
        rstd = lax.rsqrt(var + NORM_EPS)
        xhat = zc * rstd
        dxh = gout * g_ref[...]
        m1 = jnp.mean(dxh, axis=-1, keepdims=True)
        m2 = jnp.mean(dxh * xhat, axis=-1, keepdims=True)
        dz_ref[...] = rstd * (dxh - m1 - xhat * m2)

        @pl.when(pl.program_id(0) == 0)
        def _():
            dg_ref[...] = jnp.zeros_like(dg_ref)
            db_ref[...] = jnp.zeros_like(db_ref)

        dg_ref[...] += jnp.sum(gout * xhat, axis=0, keepdims=True)
        db_ref[...] += jnp.sum(gout, axis=0, keepdims=True)

    ins = [zl, g, ga] + ([gb] if two else [])
    return pl.pallas_call(
        body, name=name,
        out_shape=(jax.ShapeDtypeStruct((L, D), F32), jax.ShapeDtypeStruct((1, D), F32),
                   jax.ShapeDtypeStruct((1, D), F32)),
        grid=(L // ROW_TILE,),
        in_specs=[_rows(D), _whole((1, D)), _rows(D)] + ([_rows(D)] if two else []),
        out_specs=(_rows(D), _whole((1, D)), _whole((1, D))),
        compiler_params=_params(("arbitrary",)),
    )(*ins)


def _loss_head(h, target, first):
    L, D = h.shape

    def body(h_ref, t_ref, dh_ref, loss_ref):
        i = pl.program_id(0)

        @pl.when(i == 0)
        def _():
            loss_ref[...] = jnp.zeros_like(loss_ref)

        @pl.when(i < first)
        def _():
            dh_ref[...] = jnp.zeros_like(dh_ref)

        @pl.when(i >= first)
        def _():
            err = h_ref[...] - t_ref[...]
            dh_ref[...] = err * (1.0 / D)
            part = jnp.sum(jnp.sum(err * err, axis=-1, keepdims=True) * (1.0 / D), axis=0, keepdims=True)
            loss_ref[...] += 0.5 * part

    return pl.pallas_call(
        body, name="loss_head",
        out_shape=(jax.ShapeDtypeStruct((L, D), F32), jax.ShapeDtypeStruct((1, 1), F32)),
        grid=(L // ROW_TILE,),
        in_specs=[_rows(D), pl.BlockSpec((ROW_TILE, D), lambda i: (jnp.maximum(i - first, 0), 0))],
        out_specs=(_rows(D), _whole((1, 1))),
        compiler_params=_params(("arbitrary",)),
    )(h, target)


def _input_grads(dzl, dmix, first, n_real):
    L, D = dzl.shape

    def body(a_ref, b_ref, gx_ref, gm_ref):
        i = pl.program_id(0)
        val = ALPHA * a_ref[...] + b_ref[...]

        @pl.when(i == first - 1)
        def _():
            gm_ref[...] = val[ROW_TILE - N_META:, :]

        @pl.when(i >= first)
        def _():
            gx_ref[...] = val

    return pl.pallas_call(
        body, name="input_grads",
        out_shape=(jax.ShapeDtypeStruct((n_real, D), F32), jax.ShapeDtypeStruct((N_META, D), F32)),
        grid=(L // ROW_TILE,),
        in_specs=[_rows(D), _rows(D)],
        out_specs=(pl.BlockSpec((ROW_TILE, D), lambda i: (jnp.maximum(i - first, 0), 0)),
                   _whole((N_META, D))),
        compiler_params=_params(("arbitrary",)),
    )(dzl, dmix)


def _gate_bwd(dzl, w_out, o, zsrc, heads, name):
    L, W = o.shape
    D = dzl.shape[1]
    hd = W // heads

    def body(g_ref, w_ref, o_ref, z_ref, do_ref, dl_ref, dz_ref):
        z = z_ref[...]
        dy = lax.dot_general(g_ref[...].astype(_MXU), w_ref[...].astype(_MXU), (((1,), (1,)), ((), ())),
                             preferred_element_type=F32)
        o = o_ref[...]
        do = dy * _silu(z)
        do_ref[...] = do.astype(_MXU)
        dz_ref[...] = (dy * o * _dsilu(z)).astype(_MXU)
        prod = do * o
        for h in range(heads):
            dl_ref[h] = _as_row(jnp.sum(prod[:, h * hd:(h + 1) * hd], axis=-1, keepdims=True))

    return pl.pallas_call(
        body, name=name,
        out_shape=(jax.ShapeDtypeStruct((L, W), _MXU), jax.ShapeDtypeStruct((heads, L // ROW_TILE, 1, ROW_TILE), F32),
                   jax.ShapeDtypeStruct((L, W), _MXU)),
        grid=(L // ROW_TILE,),
        in_specs=[_rows(D), _whole((W, D)), _rows(W), _rows(W)],
        out_specs=(_rows(W), pl.BlockSpec((heads, None, 1, ROW_TILE), lambda i: (0, i, 0, 0)), _rows(W)),
        compiler_params=_params(("parallel",)),
    )(dzl, w_out, o, zsrc)


def _split3(x):
    hi = x.astype(_MXU)
    r1 = x - hi.astype(F32)
    mid = r1.astype(_MXU)
    lo = (r1 - mid.astype(F32)).astype(_MXU)
    return hi, mid, lo


def _tri_cumsum(x, reverse):
    T = x.shape[0]
    r = lax.broadcasted_iota(jnp.int32, (T, T), 0)
    c = lax.broadcasted_iota(jnp.int32, (T, T), 1)
    tri = jnp.where((c >= r) if reverse else (c <= r), 1.0, 0.0).astype(_MXU)
    out = jnp.zeros(x.shape, F32)
    for part in _split3(x):
        out = out + jnp.dot(tri, part, preferred_element_type=F32)
    return out


def _fox_cum(zf, bias, pad, name):
    L, Wz = zf.shape
    off = Wz // LANES - 1

    def body(f_ref, b_ref, c_ref, carry_ref):
        i = pl.program_id(0)

        @pl.when(i == 0)
        def _():
            carry_ref[...] = jnp.zeros_like(carry_ref)

        x = f_ref[...] + b_ref[...]
        logf = jnp.minimum(x, 0.0) - jnp.log(1.0 + jnp.exp(-jnp.abs(x)))
        row = i * ROW_TILE + lax.broadcasted_iota(jnp.int32, logf.shape, 0)
        logf = jnp.where(row >= pad, logf, 0.0)
        cum = _tri_cumsum(logf, False) + carry_ref[...]
        c_ref[...] = cum
        carry_ref[...] = cum[ROW_TILE - 1:, :]

    return pl.pallas_call(
        body, name=name,
        out_shape=jax.ShapeDtypeStruct((L, LANES), F32),
        grid=(L // ROW_TILE,),
        in_specs=[pl.BlockSpec((ROW_TILE, LANES), lambda i: (i, off)), _whole((1, LANES))],
        out_specs=_rows(LANES),
        scratch_shapes=[pltpu.VMEM((1, LANES), F32)],
        compiler_params=_params(("arbitrary",)),
    )(zf, bias)


def _fox_cum_bwd(dcum, zf, bias, pad, name):
    L, Wz = zf.shape
    off = Wz // LANES - 1
    n = L // ROW_TILE

    def body(d_ref, f_ref, b_ref, df_ref, db_ref, carry_ref):
        i = pl.program_id(0)

        @pl.when(i == 0)
        def _():
            carry_ref[...] = jnp.zeros_like(carry_ref)
            db_ref[...] = jnp.zeros_like(db_ref)

        rc = _tri_cumsum(d_ref[...], True) + carry_ref[...]
        carry_ref[...] = rc[:1, :]
        x = f_ref[...] + b_ref[...]
        row = (n - 1 - i) * ROW_TILE + lax.broadcasted_iota(jnp.int32, x.shape, 0)
        df = jnp.where(row >= pad, rc * jax.nn.sigmoid(-x), 0.0)
        df_ref[...] = df.astype(_MXU)
        db_ref[...] += jnp.sum(df, axis=0, keepdims=True)

    return pl.pallas_call(
        body, name=name,
        out_shape=(jax.ShapeDtypeStruct((L, LANES), _MXU), jax.ShapeDtypeStruct((1, LANES), F32)),
        grid=(n,),
        in_specs=[pl.BlockSpec((ROW_TILE, LANES), lambda i: (n - 1 - i, 0)),
                  pl.BlockSpec((ROW_TILE, LANES), lambda i: (n - 1 - i, off)), _whole((1, LANES))],
        out_specs=(pl.BlockSpec((ROW_TILE, LANES), lambda i: (n - 1 - i, 0)), _whole((1, LANES))),
        scratch_shapes=[pltpu.VMEM((1, LANES), F32)],
        compiler_params=_params(("arbitrary",)),
    )(dcum, zf, bias)


LOG2E = 1.4426950408889634
UNROLL_SIZES = (8, 4, 2)


def _bias_terms(ct, pad):
    H, L = ct.shape
    T = ROW_TILE
    n = L // T
    c2 = ct * LOG2E
    k2 = jnp.where(jnp.arange(L)[None, :] < pad, -NEG, c2)
    return dict(cq_row=c2.reshape(H, n, 1, T), ck_row=k2.reshape(H, n, 1, T), ref=c2[:, ::T].reshape(H, n, 1, 1))


def _exchange_copies(srcs, outs, send_sems, recv_sems, local_sems, scatter):
    nt = len(srcs)
    x, y, c = lax.axis_index("x"), lax.axis_index("y"), lax.axis_index("c")
    me = 4 * x + 2 * y + c
    copies = [pltpu.make_async_copy(srcs[t].at[me] if scatter else srcs[t], outs[t].at[me], local_sems.at[t])
              for t in range(nt)]
    for k in range(1, N_DEV):
        px = (x + (k >> 2)) % 2
        py = (y + ((k >> 1) & 1)) % 2
        pc = (c + (k & 1)) % 2
        peer = 4 * px + 2 * py + pc
        for t in range(nt):
            copies.append(pltpu.make_async_remote_copy(
                src_ref=srcs[t].at[peer] if scatter else srcs[t],
                dst_ref=outs[t].at[me],
                send_sem=send_sems.at[k - 1, t], recv_sem=recv_sems.at[k - 1, t],
                device_id=(px, py, pc), device_id_type=pl.DeviceIdType.MESH))
    return copies


def _exchange_shapes(tensors, scatter):
    out_shape = tuple(jax.ShapeDtypeStruct(t.shape if scatter else (N_DEV,) + t.shape, t.dtype) for t in tensors)
    nt = len(tensors)
    sems = [pltpu.SemaphoreType.DMA((N_DEV - 1, nt)), pltpu.SemaphoreType.DMA((N_DEV - 1, nt)),
            pltpu.SemaphoreType.DMA((nt,))]
    return out_shape, sems


def _side_exchange(side, srcs, outs, sems, first, last):
    if not side:
        return

    @pl.when(first)
    def _():
        for cp in _exchange_copies(srcs, outs, *sems, side[1]):
            cp.start()

    @pl.when(last)
    def _():
        for cp in _exchange_copies(srcs, outs, *sems, side[1]):
            cp.wait()


def _as_row(col):
    return jnp.transpose(jnp.broadcast_to(col, (col.shape[0], LANES)))[0:1, :]


def _as_col(row):
    return jnp.transpose(jnp.broadcast_to(row, (LANES, row.shape[1])))[:, 0:1]


def _attn_fwd(q, qoff, k, koff, v, voff, zsrc, bias, heads, dk, dv, scale, name, dead, side=None):
    L = q.shape[0]
    T = ROW_TILE
    n = L // T
    c = scale * LOG2E
    dn_qk = (((1,), (1,)), ((), ()))

    ns = len(side[0]) if side else 0

    def body(*refs):
        q_ref, k_ref, v_ref, z_ref, ck_ref, ref_ref = refs[:6]
        o_ref, lse_ref, y_ref = refs[6 + ns:9 + ns]
        s_a, s_b = refs[9 + 2 * ns:11 + 2 * ns]
        i = pl.program_id(1)
        _side_exchange(side, refs[6:6 + ns], refs[9 + ns:9 + 2 * ns], refs[11 + 2 * ns:],
                       (pl.program_id(0) == 0) & (i == 0), (pl.program_id(0) == heads - 1) & (i == n - 1))
        qb = q_ref[...]
        ref = ref_ref[...]

        def scores(j, dst):
            start = pl.multiple_of(j * T, T)
            dst[...] = lax.dot_general(qb, k_ref[pl.ds(start, T), :], dn_qk, preferred_element_type=F32)

        def update(t, vj, carry):
            m, l, acc = carry
            m_new = jnp.maximum(m, jnp.max(t, axis=1, keepdims=True))
            p = jnp.exp2(t - m_new)
            a = jnp.exp2(m - m_new)
            l = a * l + jnp.sum(p, axis=1, keepdims=True)
            acc = a * acc + jnp.dot(p.astype(_MXU), vj, preferred_element_type=F32)
            return m_new, l, acc

        def soft(j, carry, cur, diag):
            start = pl.multiple_of(j * T, T)
            t = cur[...] * c - (ck_ref[j] - ref)
            if diag:
                r = lax.broadcasted_iota(jnp.int32, (T, T), 0)
                cc = lax.broadcasted_iota(jnp.int32, (T, T), 1)
                t = jnp.where(cc <= r, t, NEG)
            return update(t, v_ref[pl.ds(start, T), :], carry)

        def head_tile(carry):
            t = lax.dot_general(qb, k_ref[dead:T, :], dn_qk, preferred_element_type=F32) * c
            return update(t - (ck_ref[0][:, dead:] - ref), v_ref[dead:T, :], carry)

        def step(j, carry, cur, nxt):
            scores(j + 1, nxt)
            return soft(j, carry, cur, False)

        first = jnp.where(i > 0, 1, 0) if dead else 0

        def run_of(j0, steps, carry):
            for u in range(0, steps, 2):
                carry = step(j0 + u + 1, step(j0 + u, carry, s_a, s_b), s_b, s_a)
            return carry

        scores(first, s_a)
        carry = (jnp.full((T, 1), -jnp.inf, F32), jnp.zeros((T, 1), F32), jnp.zeros((T, dv), F32))
        if dead:
            carry = lax.cond(i > 0, head_tile, lambda cy: cy, carry)
        done = first
        for size in UNROLL_SIZES:
            trips = (i - done) // size
            carry = lax.fori_loop(0, trips, lambda g, cy, d=done, s=size: run_of(d + s * g, s, cy), carry)
            done = done + size * trips
        odd = (i - done) == 1
        carry = lax.cond(odd, lambda cy: step(i - 1, cy, s_a, s_b), lambda cy: cy, carry)
        m, l, acc = lax.cond(odd, lambda cy: soft(i, cy, s_b, True), lambda cy: soft(i, cy, s_a, True), carry)
        o = acc / l
        o_ref[...] = o
        y_ref[...] = (o * _silu(z_ref[...])).astype(_MXU)
        lse_ref[...] = _as_row(m + jnp.log2(l))

    hbm = pl.BlockSpec(memory_space=pl.ANY)
    side_shapes, side_sems = _exchange_shapes(*side) if side else ((), [])
    o, lse, y, *side_out = pl.pallas_call(
        body, name=name,
        out_shape=(jax.ShapeDtypeStruct((L, heads * dv), F32), jax.ShapeDtypeStruct((heads, n, 1, T), F32),
                   jax.ShapeDtypeStruct((L, heads * dv), _MXU)) + side_shapes,
        grid=(heads, n),
        in_specs=[pl.BlockSpec((T, dk), lambda h, i: (i, qoff + h)),
                  pl.BlockSpec((L, dk), lambda h, i: (0, koff + h)),
                  pl.BlockSpec((L, dv), lambda h, i: (0, voff + h)),
                  pl.BlockSpec((T, dv), lambda h, i: (i, h)),
                  pl.BlockSpec((None, n, 1, T), lambda h, i: (h, 0, 0, 0)),
                  pl.BlockSpec((None, None, 1, 1), lambda h, i: (h, i, 0, 0))] + [hbm] * ns,
        out_specs=(pl.BlockSpec((T, dv), lambda h, i: (i, h)),
                   pl.BlockSpec((None, None, 1, T), lambda h, i: (h, i, 0, 0)),
                   pl.BlockSpec((T, dv), lambda h, i: (i, h))) + (hbm,) * ns,
        scratch_shapes=[pltpu.VMEM((T, T), F32), pltpu.VMEM((T, T), F32)] + side_sems,
        compiler_params=_params(("arbitrary", "arbitrary")),
    )(q, k, v, zsrc, bias["ck_row"], bias["ref"], *(side[0] if side else ()))
    return o, lse + (bias["cq_row"] - bias["ref"]), y, side_out


def _attn_bwd(q, qoff, k, koff, v, voff, do, bias, lse_row, delta_row, heads, dk, dv, scale, name, dead,
              side=None, bias_grads=True):
    L = q.shape[0]
    T = ROW_TILE
    n = L // T
    c = scale * LOG2E
    dn_nt = (((1,), (1,)), ((), ()))
    dn_tn = (((0,), (0,)), ((), ()))
    ns = len(side[0]) if side else 0

    def body(*refs):
        q_ref, k_ref, v_ref, do_ref, cq_ref, ck_ref, ref_ref, lse_ref, dl_ref = refs[:9]
        dq_hbm, dk_ref, dv_ref, dck_ref, dcq_ref = refs[9 + ns:14 + ns]
        dq_acc, sem, st_a, dp_a, st_b, dp_b = refs[14 + 2 * ns:20 + 2 * ns]
        h = pl.program_id(0)
        j = pl.program_id(1)
        _side_exchange(side, refs[9:9 + ns], refs[14 + ns:14 + 2 * ns], refs[20 + 2 * ns:],
                       (h == 0) & (j == 0), (h == heads - 1) & (j == n - 1))

        @pl.when(j == 0)
        def _():
            dq_acc[...] = jnp.zeros_like(dq_acc)
            dcq_ref[...] = jnp.zeros_like(dcq_ref)

        ref = ref_ref[...]

        def run(k0, R):
            kb = k_ref[k0:k0 + R, :]
            vb = v_ref[k0:k0 + R, :]
            bcol = _as_col(ck_ref[j][:, k0:k0 + R] - ref)

            def front(i, st_dst, dp_dst):
                start = pl.multiple_of(jnp.minimum(i, n - 1) * T, T)
                st_dst[0:R, :] = lax.dot_general(kb, q_ref[pl.ds(start, T), :], dn_nt, preferred_element_type=F32)
                dp_dst[0:R, :] = lax.dot_general(vb, do_ref[pl.ds(start, T), :], dn_nt, preferred_element_type=F32)

            def back(i, carry, st_cur, dp_cur, diag):
                dk_a, dv_a, dck_a = carry
                start = pl.multiple_of(i * T, T)
                qi = q_ref[pl.ds(start, T), :]
                doi = do_ref[pl.ds(start, T), :]
                arow = (cq_ref[i] - ref) - lse_ref[i]
                st = st_cur[0:R, :] * c + arow - bcol
                if diag:
                    r = lax.broadcasted_iota(jnp.int32, (R, T), 0) + k0
                    cc = lax.broadcasted_iota(jnp.int32, (R, T), 1)
                    st = jnp.where(r <= cc, st, NEG)
                pt = jnp.exp2(st)
                dv_a = dv_a + jnp.dot(pt.astype(_MXU), doi, preferred_element_type=F32)
                dst = pt * (dp_cur[0:R, :] - dl_ref[i])
                if bias_grads:
                    dck_a = dck_a - jnp.sum(dst, axis=1, keepdims=True)
                    dcq_ref[i] += jnp.sum(dst, axis=0, keepdims=True)
                dsb = (dst * scale).astype(_MXU)
                dk_a = dk_a + jnp.dot(dsb, qi, preferred_element_type=F32)
                dq_acc[pl.ds(start, T), :] += lax.dot_general(dsb, kb, dn_tn, preferred_element_type=F32)
                return dk_a, dv_a, dck_a

            buf_a, buf_b = (st_a, dp_a), (st_b, dp_b)

            def step(i, carry, cur, nxt, diag=False):
                front(i + 1, *nxt)
                return back(i, carry, *cur, diag)

            def run_of(i0, steps, carry):
                for u in range(0, steps, 2):
                    carry = step(i0 + u + 1, step(i0 + u, carry, buf_b, buf_a), buf_a, buf_b)
                return carry

            front(j, *buf_a)
            init = (jnp.zeros((R, dk), F32), jnp.zeros((R, dv), F32), jnp.zeros((R, 1), F32))
            carry = step(j, init, buf_a, buf_b, True)
            done = j + 1
            for size in UNROLL_SIZES:
                trips = (n - done) // size
                carry = lax.fori_loop(0, trips, lambda g, cy, d=done, s=size: run_of(d + s * g, s, cy), carry)
                done = done + size * trips
            dk_a, dv_a, dck_a = lax.cond((n - done) == 1, lambda cy: step(n - 1, cy, buf_b, buf_a),
                                         lambda cy: cy, carry)
            if k0:
                dk_ref[0:k0, :] = jnp.zeros((k0, dk), F32)
                dv_ref[0:k0, :] = jnp.zeros((k0, dv), F32)
                dck_ref[:, 0:k0] = jnp.zeros((1, k0), F32)
            dk_ref[k0:k0 + R, :] = dk_a
            dv_ref[k0:k0 + R, :] = dv_a
            dck_ref[:, k0:k0 + R] = _as_row(dck_a)

        if dead:
            pl.when(j == 0)(lambda: run(dead, T - dead))
            pl.when(j > 0)(lambda: run(0, T))
        else:
            run(0, T)

        @pl.when(j == n - 1)
        def _():
            cp = pltpu.make_async_copy(dq_acc, dq_hbm.at[:, pl.ds(pl.multiple_of(h * dk, dk), dk)], sem)
            cp.start()
            cp.wait()

    hbm = pl.BlockSpec(memory_space=pl.ANY)
    side_shapes, side_sems = _exchange_shapes(*side) if side else ((), [])
    dq, dk_, dv_, dck, dcq, *side_out = pl.pallas_call(
        body, name=name,
        out_shape=(jax.ShapeDtypeStruct((L, heads * dk), F32), jax.ShapeDtypeStruct((L, heads * dk), F32),
                   jax.ShapeDtypeStruct((L, heads * dv), F32), jax.ShapeDtypeStruct((heads, n, 1, T), F32),
                   jax.ShapeDtypeStruct((heads, n, 1, T), F32)) + side_shapes,
        grid=(heads, n),
        in_specs=[pl.BlockSpec((L, dk), lambda h, j: (0, qoff + h)),
                  pl.BlockSpec((T, dk), lambda h, j: (j, koff + h)),
                  pl.BlockSpec((T, dv), lambda h, j: (j, voff + h)),
                  pl.BlockSpec((L, dv), lambda h, j: (0, h)),
                  pl.BlockSpec((None, n, 1, T), lambda h, j: (h, 0, 0, 0)),
                  pl.BlockSpec((None, n, 1, T), lambda h, j: (h, 0, 0, 0)),
                  pl.BlockSpec((None, None, 1, 1), lambda h, j: (h, j, 0, 0)),
                  pl.BlockSpec((None, n, 1, T), lambda h, j: (h, 0, 0, 0)),
                  pl.BlockSpec((None, n, 1, T), lambda h, j: (h, 0, 0, 0))] + [hbm] * ns,
        out_specs=(hbm,
                   pl.BlockSpec((T, dk), lambda h, j: (j, h)),
                   pl.BlockSpec((T, dv), lambda h, j: (j, h)),
                   pl.BlockSpec((None, None, 1, T), lambda h, j: (h, j, 0, 0)),
                   pl.BlockSpec((None, n, 1, T), lambda h, j: (h, 0, 0, 0))) + (hbm,) * ns,
        scratch_shapes=[pltpu.VMEM((L, dk), F32), pltpu.SemaphoreType.DMA] + [pltpu.VMEM((T, T), F32)] * 4
        + side_sems,
        compiler_params=_params(("arbitrary", "arbitrary")),
    )(q, k, v, do, bias["cq_row"], bias["ck_row"], bias["ref"], lse_row, delta_row, *(side[0] if side else ()))
    return dq, dk_, dv_, dck, dcq, side_out


_CQ0, _CKV0, _KR0 = D_MODEL, D_MODEL + MLA_Q_LORA, D_MODEL + MLA_Q_LORA + MLA_KV_LORA
_MLA_PROJ = _KR0 + LANES


def _rms(x, g):
    ms = jnp.mean(x * x, axis=-1, keepdims=True)
    return x * lax.rsqrt(ms + NORM_EPS) * g


def _rms_bwd(x, g, dy):
    ms = jnp.mean(x * x, axis=-1, keepdims=True)
    r = lax.rsqrt(ms + NORM_EPS)
    xh = x * r
    dxh = dy * g
    dx = r * (dxh - xh * jnp.mean(dxh * xh, axis=-1, keepdims=True))
    return dx, jnp.sum(dy * xh, axis=0, keepdims=True)


def _mla_norm_fwd(proj, gq, gkv):
    L = proj.shape[0]

    def body(p_ref, gq_ref, gkv_ref, cq_ref, ckv_ref):
        cq_ref[...] = _rms(p_ref[:, _CQ0:_CKV0], gq_ref[...]).astype(_MXU)
        ckv_ref[...] = _rms(p_ref[:, _CKV0:_KR0], gkv_ref[...]).astype(_MXU)

    return pl.pallas_call(
        body, name="mla_norm_fwd",
        out_shape=(jax.ShapeDtypeStruct((L, MLA_Q_LORA), _MXU), jax.ShapeDtypeStruct((L, MLA_KV_LORA), _MXU)),
        grid=(L // ROW_TILE,),
        in_specs=[_rows(_MLA_PROJ), _whole((1, MLA_Q_LORA)), _whole((1, MLA_KV_LORA))],
        out_specs=(_rows(MLA_Q_LORA), _rows(MLA_KV_LORA)),
        compiler_params=_params(("parallel",)),
    )(proj, gq, gkv)


def _mla_norm_bwd(proj, gq, gkv, dcqn, dckvn):
    L = proj.shape[0]

    def body(p_ref, gq_ref, gkv_ref, dq_ref, dkv_ref, dcq_ref, dckv_ref, dgq_ref, dgkv_ref):
        @pl.when(pl.program_id(0) == 0)
        def _():
            dgq_ref[...] = jnp.zeros_like(dgq_ref)
            dgkv_ref[...] = jnp.zeros_like(dgkv_ref)

        dx, dg = _rms_bwd(p_ref[:, _CQ0:_CKV0], gq_ref[...], dq_ref[...])
        dcq_ref[...] = dx.astype(_MXU)
        dgq_ref[...] += dg
        dx, dg = _rms_bwd(p_ref[:, _CKV0:_KR0], gkv_ref[...], dkv_ref[...])
        dckv_ref[...] = dx.astype(_MXU)
        dgkv_ref[...] += dg

    return pl.pallas_call(
        body, name="mla_norm_bwd",
        out_shape=(jax.ShapeDtypeStruct((L, MLA_Q_LORA), _MXU), jax.ShapeDtypeStruct((L, MLA_KV_LORA), _MXU),
                   jax.ShapeDtypeStruct((1, MLA_Q_LORA), F32), jax.ShapeDtypeStruct((1, MLA_KV_LORA), F32)),
        grid=(L // ROW_TILE,),
        in_specs=[_rows(_MLA_PROJ), _whole((1, MLA_Q_LORA)), _whole((1, MLA_KV_LORA)),
                  _rows(MLA_Q_LORA), _rows(MLA_KV_LORA)],
        out_specs=(_rows(MLA_Q_LORA), _rows(MLA_KV_LORA), _whole((1, MLA_Q_LORA)), _whole((1, MLA_KV_LORA))),
        compiler_params=_params(("arbitrary",)),
    )(proj, gq, gkv, dcqn, dckvn)


def _rot_tile(t, cos, sa, sb):
    half = MLA_ROPE // 2
    return t * cos + pltpu.roll(t, LANES - half, 1) * sa + pltpu.roll(t, half, 1) * sb


def _mla_rope_fwd(qf, kv, proj, cos, sa, sb):
    L = qf.shape[0]
    H = MLA_HEADS

    def body(q_ref, kv_ref, p_ref, c_ref, sa_ref, sb_ref, qo_ref, ko_ref):
        cos, sa, sb = c_ref[...], sa_ref[...], sb_ref[...]
        kr = _rot_tile(p_ref[...], cos, sa, sb).astype(_MXU)
        for h in range(H):
            b = h * MLA_DK
            qo_ref[:, b:b + LANES] = q_ref[:, b:b + LANES].astype(_MXU)
            qo_ref[:, b + LANES:b + 2 * LANES] = _rot_tile(q_ref[:, b + LANES:b + 2 * LANES], cos, sa, sb).astype(_MXU)
            ko_ref[:, b:b + LANES] = kv_ref[:, h * LANES:(h + 1) * LANES]
            ko_ref[:, b + LANES:b + 2 * LANES] = kr

    W = H * MLA_DK
    return pl.pallas_call(
        body, name="mla_rope_fwd",
        out_shape=(jax.ShapeDtypeStruct((L, W), _MXU), jax.ShapeDtypeStruct((L, W), _MXU)),
        grid=(L // ROW_TILE,),
        in_specs=[_rows(W), _rows(kv.shape[1]), pl.BlockSpec((ROW_TILE, LANES), lambda i: (i, _KR0 // LANES)),
                  _rows(LANES), _rows(LANES), _rows(LANES)],
        out_specs=(_rows(W), _rows(W)),
        compiler_params=_params(("parallel",)),
    )(qf, kv, proj, cos, sa, sb)


def _mla_rope_bwd(dq_full, dk_full, cos, sa, sb):
    L = dq_full.shape[0]
    H = MLA_HEADS
    W = H * MLA_DK

    def body(dq_ref, dk_ref, c_ref, sa_ref, sb_ref, dqf_ref, dkn_ref, dkr_ref):
        cos, sa, sb = c_ref[...], sa_ref[...], sb_ref[...]
        lane = lax.broadcasted_iota(jnp.int32, (ROW_TILE, LANES), 1)
        live = lane < MLA_ROPE
        krs = jnp.zeros((ROW_TILE, LANES), F32)
        for h in range(H):
            b = h * MLA_DK
            dqf_ref[:, b:b + LANES] = dq_ref[:, b:b + LANES].astype(_MXU)
            dqr = _rot_tile(dq_ref[:, b + LANES:b + 2 * LANES], cos, sa, sb)
            dqf_ref[:, b + LANES:b + 2 * LANES] = jnp.where(live, dqr, 0.0).astype(_MXU)
            dkn_ref[:, h * LANES:(h + 1) * LANES] = dk_ref[:, b:b + LANES].astype(_MXU)
            krs = krs + dk_ref[:, b + LANES:b + 2 * LANES]
        dkr_ref[...] = jnp.where(live, _rot_tile(krs, cos, sa, sb), 0.0).astype(_MXU)

    return pl.pallas_call(
        body, name="mla_rope_bwd",
        out_shape=(jax.ShapeDtypeStruct((L, W), _MXU), jax.ShapeDtypeStruct((L, H * LANES), _MXU),
                   jax.ShapeDtypeStruct((L, LANES), _MXU)),
        grid=(L // ROW_TILE,),
        in_specs=[_rows(W), _rows(W), _rows(LANES), _rows(LANES), _rows(LANES)],
        out_specs=(_rows(W), _rows(H * LANES), _rows(LANES)),
        compiler_params=_params(("parallel",)),
    )(dq_full, dk_full, cos, sa, sb)


_RET_QKW = RET_HEADS * RET_QK


def _ret_rope_fwd(qkz, cos, sin, pad):
    L = qkz.shape[0]
    hh = RET_QK // 2
    kscale = RET_QK ** -0.5

    def body(p_ref, c_ref, s_ref, q_ref, k_ref):
        cos, sin = c_ref[...], s_ref[...]
        row = pl.program_id(0) * ROW_TILE + lax.broadcasted_iota(jnp.int32, (ROW_TILE, hh), 0)
        keep = row >= pad
        for h in range(RET_HEADS):
            b = h * RET_QK
            t1, t2 = p_ref[:, b:b + hh], p_ref[:, b + hh:b + 2 * hh]
            q_ref[:, b:b + hh] = (t1 * cos - t2 * sin).astype(_MXU)
            q_ref[:, b + hh:b + 2 * hh] = (t2 * cos + t1 * sin).astype(_MXU)
            t1, t2 = p_ref[:, _RET_QKW + b:_RET_QKW + b + hh], p_ref[:, _RET_QKW + b + hh:_RET_QKW + b + 2 * hh]
            k_ref[:, b:b + hh] = jnp.where(keep, (t1 * cos - t2 * sin) * kscale, 0.0)
            k_ref[:, b + hh:b + 2 * hh] = jnp.where(keep, (t2 * cos + t1 * sin) * kscale, 0.0)

    return pl.pallas_call(
        body, name="ret_rope_fwd",
        out_shape=(jax.ShapeDtypeStruct((L, _RET_QKW), _MXU), jax.ShapeDtypeStruct((L, _RET_QKW), F32)),
        grid=(L // ROW_TILE,),
        in_specs=[_rows(2 * _RET_QKW), _rows(hh), _rows(hh)],
        out_specs=(_rows(_RET_QKW), _rows(_RET_QKW)),
        compiler_params=_params(("parallel",)),
    )(qkz, cos, sin)


def _ret_rope_bwd(dqr, dkr, cos, sin, pad):
    L = dqr.shape[0]
    hh = RET_QK // 2
    kscale = RET_QK ** -0.5

    def body(dq_ref, dk_ref, c_ref, s_ref, oq_ref, ok_ref):
        cos, sin = c_ref[...], s_ref[...]
        row = pl.program_id(0) * ROW_TILE + lax.broadcasted_iota(jnp.int32, (ROW_TILE, hh), 0)
        keep = row >= pad
        for h in range(RET_HEADS):
            b = h * RET_QK
            d1, d2 = dq_ref[:, b:b + hh], dq_ref[:, b + hh:b + 2 * hh]
            oq_ref[:, b:b + hh] = (d1 * cos + d2 * sin).astype(_MXU)
            oq_ref[:, b + hh:b + 2 * hh] = (d2 * cos - d1 * sin).astype(_MXU)
            d1, d2 = dk_ref[:, b:b + hh], dk_ref[:, b + hh:b + 2 * hh]
            ok_ref[:, b:b + hh] = jnp.where(keep, (d1 * cos + d2 * sin) * kscale, 0.0).astype(_MXU)
            ok_ref[:, b + hh:b + 2 * hh] = jnp.where(keep, (d2 * cos - d1 * sin) * kscale, 0.0).astype(_MXU)

    return pl.pallas_call(
        body, name="ret_rope_bwd",
        out_shape=(jax.ShapeDtypeStruct((L, _RET_QKW), _MXU), jax.ShapeDtypeStruct((L, _RET_QKW), _MXU)),
        grid=(L // ROW_TILE,),
        in_specs=[_rows(_RET_QKW), _rows(_RET_QKW), _rows(hh), _rows(hh)],
        out_specs=(_rows(_RET_QKW), _rows(_RET_QKW)),
        compiler_params=_params(("parallel",)),
    )(dqr, dkr, cos, sin)


def _ret_rows(L):
    return next(r * ROW_TILE for r in (3, 2, 1) if L % (r * ROW_TILE) == 0)


def _ret_decays():
    lg = np.log1p(-np.exp2(-5.0 - np.arange(RET_HEADS, dtype=np.float32))).astype(np.float32)
    i = np.arange(CHUNK, dtype=np.float32)
    rel = i[:, None] - i[None, :]
    dm = np.where(rel[None] >= 0, np.exp(rel[None] * lg[:, None, None]), 0.0).astype(np.float32)
    qd = np.exp((i[None, :] + 1.0) * lg[:, None]).astype(np.float32)[:, :, None]
    kd = np.exp((CHUNK - 1.0 - i)[None, :] * lg[:, None]).astype(np.float32)[:, :, None]
    cd = np.broadcast_to(np.exp(CHUNK * lg).astype(np.float32)[:, None, None], (RET_HEADS, 1, RET_V))
    return jnp.asarray(dm), jnp.asarray(qd), jnp.asarray(kd), jnp.asarray(np.ascontiguousarray(cd))


def _ret_fwd(qr, kr, v):
    L = qr.shape[0]
    nc = L // CHUNK
    RB = _ret_rows(L)
    G = RB // CHUNK
    dm, qd, kd, cd = _ret_decays()
    dn_nt = (((1,), (1,)), ((), ()))
    dn_tn = (((0,), (0,)), ((), ()))

    def body(q_ref, k_ref, v_ref, dm_ref, qd_ref, kd_ref, cd_ref, o_ref, st_ref, s_ref):
        @pl.when(pl.program_id(1) == 0)
        def _():
            s_ref[...] = jnp.zeros_like(s_ref)

        state = s_ref[...]
        for g in range(G):
            rows = slice(g * CHUNK, (g + 1) * CHUNK)
            q, k, vv = q_ref[rows, :], k_ref[rows, :], v_ref[rows, :]
            sb = state.astype(_MXU)
            st_ref[g] = sb
            s = lax.dot_general(q, k.astype(_MXU), dn_nt, preferred_element_type=F32) * dm_ref[...]
            intra = jnp.dot(s.astype(_MXU), vv, preferred_element_type=F32)
            cross = jnp.dot(q, sb, preferred_element_type=F32) * qd_ref[...]
            o_ref[rows, :] = intra + cross
            kdm = (k * kd_ref[...]).astype(_MXU)
            state = cd_ref[...] * state + lax.dot_general(kdm, vv, dn_tn, preferred_element_type=F32)
        s_ref[...] = state

    return pl.pallas_call(
        body, name="ret_fwd",
        out_shape=(jax.ShapeDtypeStruct((L, RET_HEADS * RET_V), F32),
                   jax.ShapeDtypeStruct((RET_HEADS, nc, RET_QK, RET_V), _MXU)),
        grid=(RET_HEADS, nc // G),
        in_specs=[pl.BlockSpec((RB, RET_QK), lambda h, c: (c, h)),
                  pl.BlockSpec((RB, RET_QK), lambda h, c: (c, h)),
                  pl.BlockSpec((RB, RET_V), lambda h, c: (c, h)),
                  pl.BlockSpec((None, CHUNK, CHUNK), lambda h, c: (h, 0, 0)),
                  pl.BlockSpec((None, CHUNK, 1), lambda h, c: (h, 0, 0)),
                  pl.BlockSpec((None, CHUNK, 1), lambda h, c: (h, 0, 0)),
                  pl.BlockSpec((None, 1, RET_V), lambda h, c: (h, 0, 0))],
        out_specs=(pl.BlockSpec((RB, RET_V), lambda h, c: (c, h)),
                   pl.BlockSpec((None, G, RET_QK, RET_V), lambda h, c: (h, c, 0, 0))),
        scratch_shapes=[pltpu.VMEM((RET_QK, RET_V), F32)],
        compiler_params=_params(("parallel", "arbitrary")),
    )(qr, kr, v, dm, qd, kd, cd)


def _ret_bwd(qr, kr, v, do, states):
    L = qr.shape[0]
    nc = L // CHUNK
    RB = _ret_rows(L)
    G = RB // CHUNK
    nt = nc // G
    dm, qd, kd, cd = _ret_decays()
    dn_nt = (((1,), (1,)), ((), ()))
    dn_tn = (((0,), (0,)), ((), ()))

    def body(q_ref, k_ref, v_ref, do_ref, st_ref, dm_ref, qd_ref, kd_ref, cd_ref,
             dq_ref, dk_ref, dv_ref, ds_ref):
        @pl.when(pl.program_id(1) == 0)
        def _():
            ds_ref[...] = jnp.zeros_like(ds_ref)

        dstate = ds_ref[...]
        dmat = dm_ref[...]
        for g in reversed(range(G)):
            rows = slice(g * CHUNK, (g + 1) * CHUNK)
            q, k, vv, do = q_ref[rows, :], k_ref[rows, :], v_ref[rows, :], do_ref[rows, :]
            kb = k.astype(_MXU)
            dob = do.astype(_MXU)
            doq = (do * qd_ref[...]).astype(_MXU)
            dsb = dstate.astype(_MXU)
            s = (lax.dot_general(q, kb, dn_nt, preferred_element_type=F32) * dmat).astype(_MXU)
            draw = (lax.dot_general(dob, vv, dn_nt, preferred_element_type=F32) * dmat).astype(_MXU)
            kdm = (k * kd_ref[...]).astype(_MXU)
            dv = (lax.dot_general(s, dob, dn_tn, preferred_element_type=F32)
                  + jnp.dot(kdm, dsb, preferred_element_type=F32))
            dq = (jnp.dot(draw, kb, preferred_element_type=F32)
                  + lax.dot_general(doq, st_ref[g], dn_nt, preferred_element_type=F32))
            dk = (lax.dot_general(draw, q, dn_tn, preferred_element_type=F32)
                  + lax.dot_general(vv, dsb, dn_nt, preferred_element_type=F32) * kd_ref[...])
            dq_ref[rows, :] = dq
            dk_ref[rows, :] = dk
            dv_ref[rows, :] = dv.astype(_MXU)
            dstate = cd_ref[...] * dstate + lax.dot_general(q, doq, dn_tn, preferred_element_type=F32)
        ds_ref[...] = dstate

    rev = lambda h, c: (nt - 1 - c, h)
    return pl.pallas_call(
        body, name="ret_bwd",
        out_shape=(jax.ShapeDtypeStruct((L, _RET_QKW), F32), jax.ShapeDtypeStruct((L, _RET_QKW), F32),
                   jax.ShapeDtypeStruct((L, RET_HEADS * RET_V), _MXU)),
        grid=(RET_HEADS, nt),
        in_specs=[pl.BlockSpec((RB, RET_QK), rev),
                  pl.BlockSpec((RB, RET_QK), rev),
                  pl.BlockSpec((RB, RET_V), rev),
                  pl.BlockSpec((RB, RET_V), rev),
                  pl.BlockSpec((None, G, RET_QK, RET_V), lambda h, c: (h, nt - 1 - c, 0, 0)),
                  pl.BlockSpec((None, CHUNK, CHUNK), lambda h, c: (h, 0, 0)),
                  pl.BlockSpec((None, CHUNK, 1), lambda h, c: (h, 0, 0)),
                  pl.BlockSpec((None, CHUNK, 1), lambda h, c: (h, 0, 0)),
                  pl.BlockSpec((None, 1, RET_V), lambda h, c: (h, 0, 0))],
        out_specs=(pl.BlockSpec((RB, RET_QK), rev), pl.BlockSpec((RB, RET_QK), rev),
                   pl.BlockSpec((RB, RET_V), rev)),
        scratch_shapes=[pltpu.VMEM((RET_QK, RET_V), F32)],
        compiler_params=_params(("parallel", "arbitrary")),
    )(qr, kr, v, do, states, dm, qd, kd, cd)


def _gn(o, g):
    mu = jnp.mean(o, axis=-1, keepdims=True)
    oc = o - mu
    var = jnp.mean(oc * oc, axis=-1, keepdims=True)
    return oc * lax.rsqrt(var + NORM_EPS) * g


def _ret_gate_fwd(o, qkz, gn_g):
    L, W = o.shape
    zblk = qkz.shape[1] // W - 1

    def body(o_ref, z_ref, g_ref, y_ref):
        for h in range(RET_HEADS):
            sl = slice(h * RET_V, (h + 1) * RET_V)
            y_ref[:, sl] = (_gn(o_ref[:, sl], g_ref[:, sl]) * _silu(z_ref[:, sl])).astype(_MXU)

    return pl.pallas_call(
        body, name="ret_gate_fwd",
        out_shape=jax.ShapeDtypeStruct((L, W), _MXU),
        grid=(L // ROW_TILE,),
        in_specs=[_rows(W), pl.BlockSpec((ROW_TILE, W), lambda i: (i, zblk)), _whole((1, W))],
        out_specs=_rows(W),
        compiler_params=_params(("parallel",)),
    )(o, qkz, gn_g)


def _ret_gate_bwd(dzl, w_out, o, qkz, gn_g):
    L, W = o.shape
    zblk = qkz.shape[1] // W - 1

    def body(c_ref, w_ref, o_ref, z_ref, g_ref, do_ref, dz_ref, dg_ref):
        @pl.when(pl.program_id(0) == 0)
        def _():
            dg_ref[...] = jnp.zeros_like(dg_ref)

        dy_all = lax.dot_general(c_ref[...].astype(_MXU), w_ref[...].astype(_MXU), (((1,), (1,)), ((), ())),
                                 preferred_element_type=F32)
        for h in range(RET_HEADS):
            sl = slice(h * RET_V, (h + 1) * RET_V)
            z = z_ref[:, sl]
            o = o_ref[:, sl]
            g = g_ref[:, sl]
            dy = dy_all[:, sl]
            mu = jnp.mean(o, axis=-1, keepdims=True)
            oc = o - mu
            var = jnp.mean(oc * oc, axis=-1, keepdims=True)
            r = lax.rsqrt(var + NORM_EPS)
            xh = oc * r
            dn = dy * _silu(z)
            dz_ref[:, sl] = (dy * (xh * g) * _dsilu(z)).astype(_MXU)
            dg_ref[:, sl] += jnp.sum(dn * xh, axis=0, keepdims=True)
            dxh = dn * g
            do_ref[:, sl] = r * (dxh - jnp.mean(dxh, axis=-1, keepdims=True)
                                 - xh * jnp.mean(dxh * xh, axis=-1, keepdims=True))

    return pl.pallas_call(
        body, name="ret_gate_bwd",
        out_shape=(jax.ShapeDtypeStruct((L, W), F32), jax.ShapeDtypeStruct((L, W), _MXU),
                   jax.ShapeDtypeStruct((1, W), F32)),
        grid=(L // ROW_TILE,),
        in_specs=[_rows(D_MODEL), _whole((W, D_MODEL)), _rows(W), pl.BlockSpec((ROW_TILE, W), lambda i: (i, zblk)),
                  _whole((1, W))],
        out_specs=(_rows(W), _rows(W), _whole((1, W))),
        compiler_params=_params(("arbitrary",)),
    )(dzl, w_out, o, qkz, gn_g)


def _dead_rows(pad):
    return pad // LANES * LANES


def _heads_t(a):
    return a[:, :FOX_HEADS].T


def _fox_fwd(h, w, lnw, pad, tag, side=None):
    w_in, b_f, w_out = w
    L = h.shape[0]
    W = D_MODEL
    w_qkv = w_in[:, :3 * W]
    w_zf = jnp.concatenate([w_in[:, 3 * W:4 * W], jnp.pad(w_in[:, 4 * W:], ((0, 0), (0, LANES - FOX_HEADS)))], axis=1)
    bias = jnp.pad(b_f, (0, LANES - FOX_HEADS))[None, :]
    qkv = _matmul(h, w_qkv, "nn", _MXU, f"{tag}_qkv")
    zf = _matmul(h, w_zf, "nn", F32, f"{tag}_zf")
    cum = _fox_cum(zf, bias, pad, f"{tag}_cum")
    ab = _bias_terms(_heads_t(cum), pad)
    nb = W // LANES
    o, lse, y, side_out = _attn_fwd(qkv, 0, qkv, nb, qkv, 2 * nb, zf, ab, FOX_HEADS, LANES, LANES, LANES ** -0.5,
                                    f"{tag}_attn_fwd", _dead_rows(pad), side)
    zl, hn = _out_ln(y, w_out, h, lnw[0], lnw[1], f"{tag}_out_ln")
    return hn, dict(h=h, qkv=qkv, zf=zf, ab=ab, o=o, lse=lse, y=y, zl=zl,
                    w_all=jnp.concatenate([w_qkv, w_zf], axis=1), w_out=w_out, bias=bias), side_out


def _fox_bwd(s, dzl, pad, tag, side=None, send_own=False):
    W = D_MODEL
    nb = W // LANES
    d_wout = _matmul(s["y"], dzl, "tn", F32, f"{tag}_dwout")
    if send_own:
        side = (list(side[0]) + [_split_layer("fox_w_out", d_wout)], True)
    do, delta, dzg = _gate_bwd(dzl, s["w_out"], s["o"], s["zf"], FOX_HEADS, f"{tag}_gate_bwd")
    qkv = s["qkv"]
    dq, dk, dv, dck, dcq, side_out = _attn_bwd(qkv, 0, qkv, nb, qkv, 2 * nb, do, s["ab"], s["lse"], delta,
                                               FOX_HEADS, LANES, LANES, LANES ** -0.5, f"{tag}_attn_bwd",
                                               _dead_rows(pad), side)
    dcum = jnp.pad((dck + dcq).reshape(FOX_HEADS, -1).T, ((0, 0), (0, LANES - FOX_HEADS)))
    df, dbias = _fox_cum_bwd(dcum, s["zf"], s["bias"], pad, f"{tag}_cum_bwd")
    dproj = jnp.concatenate([dq.astype(_MXU), dk.astype(_MXU), dv.astype(_MXU), dzg, df], axis=1)
    d_wall = _matmul(s["h"], dproj, "tn", F32, f"{tag}_dwin")
    d_win = jnp.concatenate([d_wall[:, :4 * W], d_wall[:, 4 * W:4 * W + FOX_HEADS]], axis=1)
    if send_own:
        dh, win_parts = _matmul(dproj, s["w_all"], "nt", F32, f"{tag}_dh", ([_split_layer("fox_w_in", d_win)], True))
        side_out = list(side_out) + list(win_parts)
    else:
        dh = _matmul(dproj, s["w_all"], "nt", F32, f"{tag}_dh")
    return dh, (d_win, dbias[0, :FOX_HEADS], d_wout), side_out


def _mla_tables(L, pad):
    pos = (jnp.arange(L) - pad).astype(F32)
    inv = ROPE_BASE ** (-jnp.arange(0, MLA_ROPE, 2, dtype=F32) / MLA_ROPE)
    ang = pos[:, None] * inv[None, :]
    c, s = jnp.cos(ang), jnp.sin(ang)
    z32, z64 = jnp.zeros_like(s), jnp.zeros((L, LANES - MLA_ROPE), F32)
    cos = jnp.concatenate([c, c, jnp.ones_like(z64)], axis=1)
    sa = jnp.concatenate([-s, z32, z64], axis=1)
    sb = jnp.concatenate([z32, s, z64], axis=1)
    return cos, sa, sb


def _mla_weights(w_in, w_uq, w_ukv):
    a, b = MLA_Q_LORA, MLA_Q_LORA + MLA_KV_LORA
    w_zc = jnp.concatenate([w_in[:, b + MLA_ROPE:], w_in[:, :b + MLA_ROPE],
                            jnp.zeros((D_MODEL, LANES - MLA_ROPE), w_in.dtype)], axis=1)
    uq = w_uq.reshape(MLA_Q_LORA, MLA_HEADS, MLA_NOPE + MLA_ROPE)
    uq = jnp.pad(uq, ((0, 0), (0, 0), (0, MLA_DK - MLA_NOPE - MLA_ROPE))).reshape(MLA_Q_LORA, MLA_HEADS * MLA_DK)
    ukv = w_ukv.reshape(MLA_KV_LORA, MLA_HEADS, MLA_NOPE + MLA_V)
    ukv = jnp.concatenate([ukv[:, :, :MLA_NOPE].reshape(MLA_KV_LORA, -1), ukv[:, :, MLA_NOPE:].reshape(MLA_KV_LORA, -1)], axis=1)
    return w_zc, uq, ukv


def _mla_fwd(h, w, lnw, pad):
    w_in, gq, gkv, w_uq, w_ukv, w_out = w
    L = h.shape[0]
    w_zc, uq, ukv = _mla_weights(w_in, w_uq, w_ukv)
    cos, sa, sb = _mla_tables(L, pad)
    proj = _matmul(h, w_zc, "nn", F32, "mla_proj")
    cqn, ckvn = _mla_norm_fwd(proj, gq[None, :], gkv[None, :])
    qf = _matmul(cqn, uq, "nn", F32, "mla_uq")
    kv = _matmul(ckvn, ukv, "nn", _MXU, "mla_ukv")
    q_full, k_full = _mla_rope_fwd(qf, kv, proj, cos, sa, sb)
    ab = _bias_terms(jnp.zeros((MLA_HEADS, L), F32), pad)
    nb = MLA_HEADS * MLA_NOPE // LANES
    o, lse, y, _ = _attn_fwd(q_full, 0, k_full, 0, kv, nb, proj, ab, MLA_HEADS, MLA_DK, MLA_V,
                             (MLA_NOPE + MLA_ROPE) ** -0.5, "mla_attn_fwd", _dead_rows(pad))
    zl, hn = _out_ln(y, w_out, h, lnw[0], lnw[1], "mla_out_ln")
    return hn, dict(h=h, proj=proj, cqn=cqn, ckvn=ckvn, kv=kv, q_full=q_full, k_full=k_full, ab=ab,
                    dead=_dead_rows(pad),
                    o=o, lse=lse, y=y, zl=zl, w_zc=w_zc, uq=uq, ukv=ukv, w_out=w_out, gq=gq, gkv=gkv,
                    tabs=(cos, sa, sb))


def _mla_bwd(s, dzl):
    H = MLA_HEADS
    d_wout = _matmul(s["y"], dzl, "tn", F32, "mla_dwout")
    do, delta, dzg = _gate_bwd(dzl, s["w_out"], s["o"], s["proj"], H, "mla_gate_bwd")
    nb = H * MLA_NOPE // LANES
    dq_full, dk_full, dv, _, _, _ = _attn_bwd(s["q_full"], 0, s["k_full"], 0, s["kv"], nb, do, s["ab"],
                                              s["lse"], delta,
                                              H, MLA_DK, MLA_V, (MLA_NOPE + MLA_ROPE) ** -0.5, "mla_attn_bwd",
                                              s["dead"], bias_grads=False)
    cos, sa, sb = s["tabs"]
    dqf, dkn, dkr = _mla_rope_bwd(dq_full, dk_full, cos, -sa, -sb)
    d_uq = _matmul(s["cqn"], dqf, "tn", F32, "mla_duq")
    dcqn = _matmul(dqf, s["uq"], "nt", F32, "mla_dcqn")
    dkv = jnp.concatenate([dkn, dv.astype(_MXU)], axis=1)
    d_ukv = _matmul(s["ckvn"], dkv, "tn", F32, "mla_dukv")
    dckvn = _matmul(dkv, s["ukv"], "nt", F32, "mla_dckvn")
    dcq, dckv, dgq, dgkv = _mla_norm_bwd(s["proj"], s["gq"][None, :], s["gkv"][None, :], dcqn, dckvn)
    dproj = jnp.concatenate([dzg, dcq, dckv, dkr], axis=1)
    d_wzc = _matmul(s["h"], dproj, "tn", F32, "mla_dwin")
    dh = _matmul(dproj, s["w_zc"], "nt", F32, "mla_dh")
    W = D_MODEL
    d_win = jnp.concatenate([d_wzc[:, W:W + MLA_Q_LORA + MLA_KV_LORA + MLA_ROPE], d_wzc[:, :W]], axis=1)
    d_wuq = d_uq.reshape(MLA_Q_LORA, H, MLA_DK)[:, :, :MLA_NOPE + MLA_ROPE].reshape(MLA_Q_LORA, -1)
    hk = H * MLA_NOPE
    d_wukv = jnp.concatenate([d_ukv[:, :hk].reshape(MLA_KV_LORA, H, MLA_NOPE),
                              d_ukv[:, hk:].reshape(MLA_KV_LORA, H, MLA_V)], axis=2).reshape(MLA_KV_LORA, -1)
    return dh, (d_win, dgq[0], dgkv[0], d_wuq, d_wukv, d_wout)


def _ret_tables(L, pad):
    pos = (jnp.arange(L) - pad).astype(F32)
    inv = 1.0 / (ROPE_BASE ** jnp.linspace(0.0, 1.0, RET_QK // 2, dtype=F32))
    ang = pos[:, None] * inv[None, :]
    return jnp.cos(ang), jnp.sin(ang)


def _ret_layer_fwd(h, w, lnw, pad):
    w_in, gn_g, w_out = w
    L = h.shape[0]
    q1 = 2 * _RET_QKW
    v1 = q1 + RET_HEADS * RET_V
    w_qkz = jnp.concatenate([w_in[:, :q1], w_in[:, v1:]], axis=1)
    cos, sin = _ret_tables(L, pad)
    qkz = _matmul(h, w_qkz, "nn", F32, "ret_qkz")
    v = _matmul(h, w_in[:, q1:v1], "nn", _MXU, "ret_v")
    qr, kr = _ret_rope_fwd(qkz, cos, sin, pad)
    o, states = _ret_fwd(qr, kr, v)
    y = _ret_gate_fwd(o, qkz, gn_g[None, :])
    zl, hn = _out_ln(y, w_out, h, lnw[0], lnw[1], "ret_out_ln")
    return hn, dict(h=h, qkz=qkz, v=v, qr=qr, kr=kr, o=o, states=states, y=y, zl=zl, w_in=w_in, w_out=w_out,
                    gn_g=gn_g, tabs=(cos, sin))


def _ret_layer_bwd(s, dzl, pad):
    d_wout = _matmul(s["y"], dzl, "tn", F32, "ret_dwout")
    do, dzg, dgn = _ret_gate_bwd(dzl, s["w_out"], s["o"], s["qkz"], s["gn_g"][None, :])
    dqr, dkr, dv = _ret_bwd(s["qr"], s["kr"], s["v"], do, s["states"])
    cos, sin = s["tabs"]
    dq, dk = _ret_rope_bwd(dqr, dkr, cos, sin, pad)
    dproj = jnp.concatenate([dq, dk, dv, dzg], axis=1)
    d_win = _matmul(s["h"], dproj, "tn", F32, "ret_dwin")
    dh = _matmul(dproj, s["w_in"], "nt", F32, "ret_dh")
    return dh, (d_win, dgn[0], d_wout)


def _local_step(x, target, meta, fox0_w, late_shards, small, ln_g, ln_b):
    n_real = x.shape[0]
    first = -(-N_META // ROW_TILE)
    pad = first * ROW_TILE - N_META
    h0 = jnp.concatenate([jnp.zeros((pad, D_MODEL), F32), meta, x], axis=0)
    lnw = [(ln_g[i][None, :], ln_b[i][None, :]) for i in range(4)]
    b_f = small["fox_b_f"]

    h1, s0, gathered = _fox_fwd(h0, (fox0_w[0], b_f[0], fox0_w[1]), lnw[0], pad, "fox0", (late_shards, False))
    w = {key: _merge_layer(key[0], g).astype(_MXU) for key, g in zip(_LATE, gathered)}
    h2, s1 = _mla_fwd(h1, (w["mla_w_in", 0], small["mla_q_norm"][0], small["mla_kv_norm"][0], w["mla_w_uq", 0],
                           w["mla_w_ukv", 0], w["mla_w_out", 0]), lnw[1], pad)
    h3, s2 = _ret_layer_fwd(h2, (w["ret_w_in", 0], small["ret_gn_g"][0], w["ret_w_out", 0]), lnw[2], pad)
    h4, s3, _ = _fox_fwd(h3, (w["fox_w_in", 1], b_f[1], w["fox_w_out", 1]), lnw[3], pad, "fox1")

    dh4, loss = _loss_head(h4, target, first)
    dz3, dg3, db3 = _ln_bwd(s3["zl"], lnw[3][0], dh4, None, "ln3_bwd")
    dm3, gf1, _ = _fox_bwd(s3, dz3, pad, "fox1")
    dz2, dg2, db2 = _ln_bwd(s2["zl"], lnw[2][0], dz3, dm3, "ln2_bwd")
    dm2, gr = _ret_layer_bwd(s2, dz2, pad)
    dz1, dg1, db1 = _ln_bwd(s1["zl"], lnw[1][0], dz2, dm2, "ln1_bwd")
    dm1, gm = _mla_bwd(s1, dz1)
    dz0, dg0, db0 = _ln_bwd(s0["zl"], lnw[0][0], dz1, dm1, "ln0_bwd")
    late = {("fox_w_in", 1): gf1[0], ("fox_w_out", 1): gf1[2], ("mla_w_in", 0): gm[0], ("mla_w_uq", 0): gm[3],
            ("mla_w_ukv", 0): gm[4], ("mla_w_out", 0): gm[5], ("ret_w_in", 0): gr[0], ("ret_w_out", 0): gr[2]}
    dm0, gf0, sent = _fox_bwd(s0, dz0, pad, "fox0", ([_split_layer(k[0], late[k]) for k in _LATE], True), True)
    parts = dict(zip(_LATE + [("fox_w_out", 0), ("fox_w_in", 0)], sent))
    grad_x, grad_meta = _input_grads(dz0, dm0, first, n_real)

    small_grads = dict(
        meta=grad_meta, fox_b_f=jnp.stack([gf0[1], gf1[1]]), mla_q_norm=gm[1][None], mla_kv_norm=gm[2][None],
        ret_gn_g=gr[1][None], ln_g=jnp.concatenate([dg0, dg1, dg2, dg3], axis=0),
        ln_b=jnp.concatenate([db0, db1, db2, db3], axis=0))
    return loss, grad_x, parts, small_grads


def _exchange(tensors, scatter, name):
    nt = len(tensors)

    def body(*refs):
        copies = _exchange_copies(refs[:nt], refs[nt:2 * nt], *refs[2 * nt:], scatter)
        for cp in copies:
            cp.start()
        for cp in copies:
            cp.wait()

    hbm = pl.BlockSpec(memory_space=pl.ANY)
    out_shape, sems = _exchange_shapes(tensors, scatter)
    return pl.pallas_call(
        body, name=name, out_shape=out_shape, in_specs=[hbm] * nt, out_specs=(hbm,) * nt, scratch_shapes=sems,
    )(*tensors)


def _adamw(parts, w, m, v, name):
    R, C = w.shape
    tr = next(t for t in range(min(R, 256), 0, -1) if R % t == 0 and (t % 8 == 0 or t == R))

    def body(p_ref, w_ref, m_ref, v_ref, g_ref, d_ref, mo_ref, vo_ref):
        g = p_ref[0]
        for d in range(1, N_DEV):
            g = g + p_ref[d]
        mn = ADAM_B1 * m_ref[...] + (1.0 - ADAM_B1) * g
        vn = ADAM_B2 * v_ref[...] + (1.0 - ADAM_B2) * (g * g)
        m_hat = mn / (1.0 - ADAM_B1 ** ADAM_STEP)
        v_hat = vn / (1.0 - ADAM_B2 ** ADAM_STEP)
        g_ref[...] = g
        d_ref[...] = -ADAM_LR * (m_hat / (jnp.sqrt(v_hat) + ADAM_EPS) + ADAM_WD * w_ref[...])
        mo_ref[...] = mn
        vo_ref[...] = vn

    blk = pl.BlockSpec((tr, C), lambda i: (i, 0))
    sd = jax.ShapeDtypeStruct((R, C), F32)
    return pl.pallas_call(
        body, name=name,
        out_shape=(sd, sd, sd, sd),
        grid=(R // tr,),
        in_specs=[pl.BlockSpec((N_DEV, tr, C), lambda i: (0, i, 0)), blk, blk, blk],
        out_specs=(blk, blk, blk, blk),
        compiler_params=_params(("parallel",)),
    )(parts, w, m, v)


_SPECS = dict(
    meta=((16, 1024), 1),
    fox_w_in=((2, 1024, 4104), 2), fox_b_f=((2, 8), None), fox_w_out=((2, 1024, 1024), 1),
    mla_w_in=((1, 1024, 1728), 2), mla_q_norm=((1, 384), None), mla_kv_norm=((1, 256), None),
    mla_w_uq=((1, 384, 1536), 2), mla_w_ukv=((1, 256, 2048), 2), mla_w_out=((1, 1024, 1024), 1),
    ret_w_in=((1, 1024, 6144), 2), ret_gn_g=((1, 2048), 1), ret_w_out=((1, 2048, 1024), 1),
    ln_g=((4, 1024), None), ln_b=((4, 1024), None),
)
_NAMES = list(_SPECS)
_BIG = ["fox_w_in", "fox_w_out", "mla_w_in", "mla_w_uq", "mla_w_ukv", "mla_w_out", "ret_w_in", "ret_w_out"]
_SMALL = [n for n in _NAMES if n not in _BIG]
_SMALL_GATHERED = ["meta", "ret_gn_g"]
_PACK_ROWS = 8


def _shard_shape(name):
    shape, ax = _SPECS[name]
    if ax is None:
        return shape
    return tuple(s // N_DEV if i == ax else s for i, s in enumerate(shape))


_EARLY = [("fox_w_in", 0), ("fox_w_out", 0)]
_LATE = [("fox_w_in", 1), ("fox_w_out", 1), ("mla_w_in", 0), ("mla_w_uq", 0), ("mla_w_ukv", 0), ("mla_w_out", 0),
         ("ret_w_in", 0), ("ret_w_out", 0)]


def _split_at(full, ax):
    shape = full.shape
    parts = full.reshape(shape[:ax] + (N_DEV, shape[ax] // N_DEV) + shape[ax + 1:])
    return jnp.moveaxis(parts, ax, 0)


def _merge_at(stacked, ax):
    full = jnp.moveaxis(stacked, 0, ax)
    return full.reshape(full.shape[:ax] + (-1,) + full.shape[ax + 2:])


def _split(name, full):
    return _split_at(full, _SPECS[name][1])


def _merge(name, stacked):
    return _merge_at(stacked, _SPECS[name][1])


def _split_layer(name, full):
    return _split_at(full, _SPECS[name][1] - 1)


def _merge_layer(name, stacked):
    return _merge_at(stacked, _SPECS[name][1] - 1)


def _pad_rows(flat):
    n = flat.shape[-1]
    unit = _PACK_ROWS * LANES
    total = -(-n // unit) * unit
    flat = jnp.pad(flat, [(0, 0)] * (flat.ndim - 1) + [(0, total - n)])
    return flat.reshape(flat.shape[:-1] + (total // LANES, LANES))


def _pack_small(tree, names, tail):
    return _pad_rows(jnp.concatenate([tree[n].reshape(-1) for n in names] + [tail]))


def _unpack_small(buf, names):
    flat, out, at = buf.reshape(-1), {}, 0
    for n in names:
        shp = _shard_shape(n)
        out[n] = flat[at:at + math.prod(shp)].reshape(shp)
        at += math.prod(shp)
    return out, flat[at:]


def kernel(x, meta, fox_w_in, fox_b_f, fox_w_out, mla_w_in, mla_q_norm, mla_kv_norm, mla_w_uq, mla_w_ukv, mla_w_out, ret_w_in, ret_gn_g, ret_w_out, ln_g, ln_b, loss_target, m_meta, m_fox_w_in, m_fox_b_f, m_fox_w_out, m_mla_w_in, m_mla_q_norm, m_mla_kv_norm, m_mla_w_uq, m_mla_w_ukv, m_mla_w_out, m_ret_w_in, m_ret_gn_g, m_ret_w_out, m_ln_g, m_ln_b, v_meta, v_fox_w_in, v_fox_b_f, v_fox_w_out, v_mla_w_in, v_mla_q_norm, v_mla_kv_norm, v_mla_w_uq, v_mla_w_ukv, v_mla_w_out, v_ret_w_in, v_ret_gn_g, v_ret_w_out, v_ln_g, v_ln_b):
    w = dict(meta=meta, fox_w_in=fox_w_in, fox_b_f=fox_b_f, fox_w_out=fox_w_out, mla_w_in=mla_w_in,
             mla_q_norm=mla_q_norm, mla_kv_norm=mla_kv_norm, mla_w_uq=mla_w_uq, mla_w_ukv=mla_w_ukv,
             mla_w_out=mla_w_out, ret_w_in=ret_w_in, ret_gn_g=ret_gn_g, ret_w_out=ret_w_out, ln_g=ln_g, ln_b=ln_b)
    m = dict(meta=m_meta, fox_w_in=m_fox_w_in, fox_b_f=m_fox_b_f, fox_w_out=m_fox_w_out, mla_w_in=m_mla_w_in,
             mla_q_norm=m_mla_q_norm, mla_kv_norm=m_mla_kv_norm, mla_w_uq=m_mla_w_uq, mla_w_ukv=m_mla_w_ukv,
             mla_w_out=m_mla_w_out, ret_w_in=m_ret_w_in, ret_gn_g=m_ret_gn_g, ret_w_out=m_ret_w_out, ln_g=m_ln_g, ln_b=m_ln_b)
    v = dict(meta=v_meta, fox_w_in=v_fox_w_in, fox_b_f=v_fox_b_f, fox_w_out=v_fox_w_out, mla_w_in=v_mla_w_in,
             mla_q_norm=v_mla_q_norm, mla_kv_norm=v_mla_kv_norm, mla_w_uq=v_mla_w_uq, mla_w_ukv=v_mla_w_ukv,
             mla_w_out=v_mla_w_out, ret_w_in=v_ret_w_in, ret_gn_g=v_ret_gn_g, ret_w_out=v_ret_w_out, ln_g=v_ln_g, ln_b=v_ln_b)

    def shard16(key):
        return w[key[0]][key[1]].astype(jnp.bfloat16)

    none = jnp.zeros((0,), F32)
    gathered = _exchange([shard16(k) for k in _EARLY] + [_pack_small(w, _SMALL_GATHERED, none)],
                         False, "weights_all_gather")
    fox0_w = [_merge_layer(k[0], g).astype(_MXU) for k, g in zip(_EARLY, gathered)]
    small = {n: w[n] for n in _SMALL}
    pieces = [_unpack_small(gathered[-1][d], _SMALL_GATHERED)[0] for d in range(N_DEV)]
    for n in _SMALL_GATHERED:
        small[n] = _merge(n, jnp.stack([p[n] for p in pieces]))

    loss, grad_x, by_key, grads = _local_step(
        x[0], loss_target[0], small["meta"], fox0_w, [shard16(k) for k in _LATE], small, ln_g, ln_b)

    def small_rows(n):
        if _SPECS[n][1] is None:
            return jnp.broadcast_to(grads[n].reshape(1, -1), (N_DEV, grads[n].size))
        return _split(n, grads[n]).reshape(N_DEV, -1)

    small_out = _pad_rows(jnp.concatenate([small_rows(n) for n in _SMALL]
                                          + [jnp.broadcast_to(loss.reshape(1, 1), (N_DEV, 1))], axis=1))
    parts = _exchange([small_out], True, "grads_all_to_all")

    def rows2d(a):
        return a.reshape(-1, a.shape[-1])

    out = {}
    for n in _BIG:
        layers = [by_key[n, i] for i in range(_SPECS[n][0][0])]
        p = layers[0] if len(layers) == 1 else jnp.stack(layers, axis=1)
        res = _adamw(p.reshape(N_DEV, -1, p.shape[-1]), rows2d(w[n]), rows2d(m[n]), rows2d(v[n]), f"adamw_{n}")
        out[n] = [r.reshape(w[n].shape) for r in res]
    zero1 = jnp.zeros((1,), F32)
    res = _adamw(parts[-1], _pack_small(w, _SMALL, zero1), _pack_small(m, _SMALL, zero1),
                 _pack_small(v, _SMALL, zero1), "adamw_small")
    small_res = [_unpack_small(r, _SMALL) for r in res]
    for n in _SMALL:
        out[n] = [s[0][n] for s in small_res]
    total_loss = small_res[0][1][0]
    return (total_loss, grad_x[None], *[out[n][0] for n in _NAMES], *[out[n][1] for n in _NAMES],
            *[out[n][2] for n in _NAMES], *[out[n][3] for n in _NAMES])
```

```python
import math

import numpy as np
import jax
import jax.numpy as jnp
from jax import lax
from jax.experimental import pallas as pl
from jax.experimental.pallas import tpu as pltpu

F32 = jnp.float32
_MXU = jnp.bfloat16

N_DEV = 8
N_META = 16
D_MODEL = 1024
ROW_TILE = 512
LANES = 128
CHUNK = 128
NEG = -1e30
V7X_VMEM_BYTES = 64 * 1024 * 1024
VMEM_LIMIT = V7X_VMEM_BYTES - 4 * 1024 * 1024

FOX_HEADS = 8
MLA_HEADS = 8
MLA_NOPE, MLA_ROPE, MLA_V = 128, 64, 128
MLA_Q_LORA, MLA_KV_LORA = 384, 256
MLA_DK = 256
RET_HEADS, RET_QK, RET_V = 4, 256, 512
ROPE_BASE = 10000.0
ALPHA = (2 * 4) ** 0.25
NORM_EPS = 1e-5

ADAM_LR, ADAM_B1, ADAM_B2, ADAM_EPS, ADAM_WD, ADAM_STEP = 0.001, 0.9, 0.999, 1e-08, 0.01, 10


def _params(sem, vmem=VMEM_LIMIT):
    return pltpu.CompilerParams(dimension_semantics=sem, vmem_limit_bytes=vmem)


def _tile(n, pref):
    if n <= pref:
        return n
    t = (pref // LANES) * LANES
    while n % t:
        t -= LANES
    return t


_MM_TILES = {"nn": (512, 4224, 2048), "nt": (512, 2048, 6144), "tn": (1024, 2048, 1536)}


def _matmul(a, b, mode, out_dtype, name, side=None):
    if mode == "nn":
        (M, K), (K2, N) = a.shape, b.shape
    elif mode == "nt":
        (M, K), (N, K2) = a.shape, b.shape
    else:
        (K, M), (K2, N) = a.shape, b.shape
    assert K == K2, (a.shape, b.shape, mode)
    tm, tn, tk = (_tile(d, p) for d, p in zip((M, N, K), _MM_TILES[mode]))
    nk = K // tk
    if mode == "nn":
        a_spec = pl.BlockSpec((tm, tk), lambda i, j, k: (i, k))
        b_spec = pl.BlockSpec((tk, tn), lambda i, j, k: (k, j))
        dn = (((1,), (0,)), ((), ()))
    elif mode == "nt":
        a_spec = pl.BlockSpec((tm, tk), lambda i, j, k: (i, k))
        b_spec = pl.BlockSpec((tn, tk), lambda i, j, k: (j, k))
        dn = (((1,), (1,)), ((), ()))
    else:
        a_spec = pl.BlockSpec((tk, tm), lambda i, j, k: (k, i))
        b_spec = pl.BlockSpec((tk, tn), lambda i, j, k: (k, j))
        dn = (((0,), (0,)), ((), ()))

    ns = len(side[0]) if side else 0
    grid = (M // tm, N // tn, nk)
    n_acc = 1 if nk > 1 else 0

    def body(*refs):
        a_ref, b_ref = refs[:2]
        o_ref = refs[2 + ns]
        k = pl.program_id(2)
        if side:
            i, j = pl.program_id(0), pl.program_id(1)
            _side_exchange(side, refs[2:2 + ns], refs[3 + ns:3 + 2 * ns], refs[3 + 2 * ns + n_acc:],
                           (i == 0) & (j == 0) & (k == 0),
                           (i == grid[0] - 1) & (j == grid[1] - 1) & (k == nk - 1))
        part = lax.dot_general(a_ref[...].astype(_MXU), b_ref[...].astype(_MXU), dn,
                               preferred_element_type=F32)
        if nk == 1:
            o_ref[...] = part.astype(out_dtype)
            return
        acc_ref = refs[3 + 2 * ns]

        @pl.when(k == 0)
        def _():
            acc_ref[...] = part

        @pl.when(k > 0)
        def _():
            acc_ref[...] += part

        @pl.when(k == nk - 1)
        def _():
            o_ref[...] = acc_ref[...].astype(out_dtype)

    hbm = pl.BlockSpec(memory_space=pl.ANY)
    side_shapes, side_sems = _exchange_shapes(*side) if side else ((), [])
    out, *side_out = pl.pallas_call(
        body, name=name,
        out_shape=(jax.ShapeDtypeStruct((M, N), out_dtype),) + side_shapes,
        grid=grid,
        in_specs=[a_spec, b_spec] + [hbm] * ns,
        out_specs=(pl.BlockSpec((tm, tn), lambda i, j, k: (i, j)),) + (hbm,) * ns,
        scratch_shapes=[pltpu.VMEM((tm, tn), F32)] * n_acc + side_sems,
        compiler_params=_params(("arbitrary",) * 3 if side else ("parallel", "parallel", "arbitrary")),
    )(a, b, *(side[0] if side else ()))
    return (out, side_out) if side else out


def _rows(w, off=0):
    return pl.BlockSpec((ROW_TILE, w), lambda i: (i + off, 0))


def _whole(shape):
    return pl.BlockSpec(shape, lambda i: (0,) * len(shape))


def _silu(z):
    return z * jax.nn.sigmoid(z)


def _dsilu(z):
    s = jax.nn.sigmoid(z)
    return s * (1.0 + z * (1.0 - s))


def _out_ln(y, w_out, h, g, b, name):
    L, D = h.shape
    K = y.shape[1]

    def body(y_ref, w_ref, h_ref, g_ref, b_ref, z_ref, o_ref):
        mo = jnp.dot(y_ref[...].astype(_MXU), w_ref[...].astype(_MXU), preferred_element_type=F32)
        z = ALPHA * h_ref[...] + mo
        mu = jnp.mean(z, axis=-1, keepdims=True)
        zc = z - mu
        var = jnp.mean(zc * zc, axis=-1, keepdims=True)
        z_ref[...] = z
        o_ref[...] = zc * lax.rsqrt(var + NORM_EPS) * g_ref[...] + b_ref[...]

    return pl.pallas_call(
        body, name=name,
        out_shape=(jax.ShapeDtypeStruct((L, D), F32), jax.ShapeDtypeStruct((L, D), F32)),
        grid=(L // ROW_TILE,),
        in_specs=[_rows(K), _whole((K, D)), _rows(D), _whole((1, D)), _whole((1, D))],
        out_specs=(_rows(D), _rows(D)),
        compiler_params=_params(("parallel",)),
    )(y, w_out, h, g, b)


def _ln_bwd(zl, g, ga, gb, name):
    L, D = zl.shape
    two = gb is not None

    def body(*refs):
        if two:
            z_ref, g_ref, ga_ref, gb_ref, dz_ref, dg_ref, db_ref = refs
            gout = ALPHA * ga_ref[...] + gb_ref[...]
        else:
            z_ref, g_ref, ga_ref, dz_ref, dg_ref, db_ref = refs
            gout = ga_ref[...]
        z = z_ref[...]
        mu = jnp.mean(z, axis=-1, keepdims=True)
        zc = z - mu
        var = jnp.mean(zc * zc, axis=-1, keepdims=True)
        rstd = lax.rsqrt(var + NORM_EPS)
        xhat = zc * rstd
        dxh = gout * g_ref[...]
        m1 = jnp.mean(dxh, axis=-1, keepdims=True)
        m2 = jnp.mean(dxh * xhat, axis=-1, keepdims=True)
        dz_ref[...] = rstd * (dxh - m1 - xhat * m2)

        @pl.when(pl.program_id(0) == 0)
        def _():
            dg_ref[...] = jnp.zeros_like(dg_ref)
            db_ref[...] = jnp.zeros_like(db_ref)

        dg_ref[...] += jnp.sum(gout * xhat, axis=0, keepdims=True)
        db_ref[...] += jnp.sum(gout, axis=0, keepdims=True)

    ins = [zl, g, ga] + ([gb] if two else [])
    return pl.pallas_call(
        body, name=name,
        out_shape=(jax.ShapeDtypeStruct((L, D), F32), jax.ShapeDtypeStruct((1, D), F32),
                   jax.ShapeDtypeStruct((1, D), F32)),
        grid=(L // ROW_TILE,),
        in_specs=[_rows(D), _whole((1, D)), _rows(D)] + ([_rows(D)] if two else []),
        out_specs=(_rows(D), _whole((1, D)), _whole((1, D))),
        compiler_params=_params(("arbitrary",)),
    )(*ins)


def _loss_ln_bwd(zl, g, b, target, first):
    L, D = zl.shape

    def body(z_ref, g_ref, b_ref, t_ref, dz_ref, dg_ref, db_ref, loss_ref):
        i = pl.program_id(0)

        @pl.when(i == 0)
        def _():
            dg_ref[...] = jnp.zeros_like(dg_ref)
            db_ref[...] = jnp.zeros_like(db_ref)
            loss_ref[...] = jnp.zeros_like(loss_ref)

        @pl.when(i < first)
        def _():
            dz_ref[...] = jnp.zeros_like(dz_ref)

        @pl.when(i >= first)
        def _():
            z = z_ref[...]
            mu = jnp.mean(z, axis=-1, keepdims=True)
            zc = z - mu
            var = jnp.mean(zc * zc, axis=-1, keepdims=True)
            rstd = lax.rsqrt(var + NORM_EPS)
            xhat = zc * rstd
            err = (xhat * g_ref[...] + b_ref[...]) - t_ref[...]
            gout = err * (1.0 / D)
            part = jnp.sum(jnp.sum(err * err, axis=-1, keepdims=True) * (1.0 / D), axis=0, keepdims=True)
            loss_ref[...] += 0.5 * part
            dxh = gout * g_ref[...]
            m1 = jnp.mean(dxh, axis=-1, keepdims=True)
            m2 = jnp.mean(dxh * xhat, axis=-1, keepdims=True)
            dz_ref[...] = rstd * (dxh - m1 - xhat * m2)
            dg_ref[...] += jnp.sum(gout * xhat, axis=0, keepdims=True)
            db_ref[...] += jnp.sum(gout, axis=0, keepdims=True)

    return pl.pallas_call(
        body, name="loss_ln_bwd",
        out_shape=(jax.ShapeDtypeStruct((L, D), F32), jax.ShapeDtypeStruct((1, D), F32),
                   jax.ShapeDtypeStruct((1, D), F32), jax.ShapeDtypeStruct((1, 1), F32)),
        grid=(L // ROW_TILE,),
        in_specs=[_rows(D), _whole((1, D)), _whole((1, D)),
                  pl.BlockSpec((ROW_TILE, D), lambda i: (jnp.maximum(i - first, 0), 0))],
        out_specs=(_rows(D), _whole((1, D)), _whole((1, D)), _whole((1, 1))),
        compiler_params=_params(("arbitrary",)),
    )(zl, g, b, target)


def _input_grads(dzl, dmix, first, n_real):
    L, D = dzl.shape

    def body(a_ref, b_ref, gx_ref, gm_ref):
        i = pl.program_id(0)
        val = ALPHA * a_ref[...] + b_ref[...]

        @pl.when(i == first - 1)
        def _():
            gm_ref[...] = val[ROW_TILE - N_META:, :]

        @pl.when(i >= first)
        def _():
            gx_ref[...] = val

    return pl.pallas_call(
        body, name="input_grads",
        out_shape=(jax.ShapeDtypeStruct((n_real, D), F32), jax.ShapeDtypeStruct((N_META, D), F32)),
        grid=(L // ROW_TILE,),
        in_specs=[_rows(D), _rows(D)],
        out_specs=(pl.BlockSpec((ROW_TILE, D), lambda i: (jnp.maximum(i - first, 0), 0)),
                   _whole((N_META, D))),
        compiler_params=_params(("arbitrary",)),
    )(dzl, dmix)


def _gate_bwd(dzl, w_out, o, zsrc, heads, name):
    L, W = o.shape
    D = dzl.shape[1]
    hd = W // heads

    def body(g_ref, w_ref, o_ref, z_ref, do_ref, dl_ref, dz_ref):
        z = z_ref[...]
        dy = lax.dot_general(g_ref[...].astype(_MXU), w_ref[...].astype(_MXU), (((1,), (1,)), ((), ())),
                             preferred_element_type=F32)
        o = o_ref[...]
        do = dy * _silu(z)
        do_ref[...] = do.astype(_MXU)
        dz_ref[...] = (dy * o * _dsilu(z)).astype(_MXU)
        prod = do * o
        for h in range(heads):
            dl_ref[h] = _as_row(jnp.sum(prod[:, h * hd:(h + 1) * hd], axis=-1, keepdims=True))

    return pl.pallas_call(
        body, name=name,
        out_shape=(jax.ShapeDtypeStruct((L, W), _MXU), jax.ShapeDtypeStruct((heads, L // ROW_TILE, 1, ROW_TILE), F32),
                   jax.ShapeDtypeStruct((L, W), _MXU)),
        grid=(L // ROW_TILE,),
        in_specs=[_rows(D), _whole((W, D)), _rows(W), _rows(W)],
        out_specs=(_rows(W), pl.BlockSpec((heads, None, 1, ROW_TILE), lambda i: (0, i, 0, 0)), _rows(W)),
        compiler_params=_params(("parallel",)),
    )(dzl, w_out, o, zsrc)


def _split3(x):
    hi = x.astype(_MXU)
    r1 = x - hi.astype(F32)
    mid = r1.astype(_MXU)
    lo = (r1 - mid.astype(F32)).astype(_MXU)
    return hi, mid, lo


def _tri_cumsum(x, reverse):
    T = x.shape[0]
    r = lax.broadcasted_iota(jnp.int32, (T, T), 0)
    c = lax.broadcasted_iota(jnp.int32, (T, T), 1)
    tri = jnp.where((c >= r) if reverse else (c <= r), 1.0, 0.0).astype(_MXU)
    out = jnp.zeros(x.shape, F32)
    for part in _split3(x):
        out = out + jnp.dot(tri, part, preferred_element_type=F32)
    return out


def _fox_cum(zf, bias, pad, name):
    L, Wz = zf.shape
    off = Wz // LANES - 1

    def body(f_ref, b_ref, c_ref, carry_ref):
        i = pl.program_id(0)

        @pl.when(i == 0)
        def _():
            carry_ref[...] = jnp.zeros_like(carry_ref)

        x = f_ref[...] + b_ref[...]
        logf = jnp.minimum(x, 0.0) - jnp.log(1.0 + jnp.exp(-jnp.abs(x)))
        row = i * ROW_TILE + lax.broadcasted_iota(jnp.int32, logf.shape, 0)
        logf = jnp.where(row >= pad, logf, 0.0)
        cum = _tri_cumsum(logf, False) + carry_ref[...]
        c_ref[...] = cum
        carry_ref[...] = cum[ROW_TILE - 1:, :]

    return pl.pallas_call(
        body, name=name,
        out_shape=jax.ShapeDtypeStruct((L, LANES), F32),
        grid=(L // ROW_TILE,),
        in_specs=[pl.BlockSpec((ROW_TILE, LANES), lambda i: (i, off)), _whole((1, LANES))],
        out_specs=_rows(LANES),
        scratch_shapes=[pltpu.VMEM((1, LANES), F32)],
        compiler_params=_params(("arbitrary",)),
    )(zf, bias)


def _fox_cum_bwd(dcum, zf, bias, pad, name):
    L, Wz = zf.shape
    off = Wz // LANES - 1
    n = L // ROW_TILE

    def body(d_ref, f_ref, b_ref, df_ref, db_ref, carry_ref):
        i = pl.program_id(0)

        @pl.when(i == 0)
        def _():
            carry_ref[...] = jnp.zeros_like(carry_ref)
            db_ref[...] = jnp.zeros_like(db_ref)

        rc = _tri_cumsum(d_ref[...], True) + carry_ref[...]
        carry_ref[...] = rc[:1, :]
        x = f_ref[...] + b_ref[...]
        row = (n - 1 - i) * ROW_TILE + lax.broadcasted_iota(jnp.int32, x.shape, 0)
        df = jnp.where(row >= pad, rc * jax.nn.sigmoid(-x), 0.0)
        df_ref[...] = df.astype(_MXU)
        db_ref[...] += jnp.sum(df, axis=0, keepdims=True)

    return pl.pallas_call(
        body, name=name,
        out_shape=(jax.ShapeDtypeStruct((L, LANES), _MXU), jax.ShapeDtypeStruct((1, LANES), F32)),
        grid=(n,),
        in_specs=[pl.BlockSpec((ROW_TILE, LANES), lambda i: (n - 1 - i, 0)),
                  pl.BlockSpec((ROW_TILE, LANES), lambda i: (n - 1 - i, off)), _whole((1, LANES))],
        out_specs=(pl.BlockSpec((ROW_TILE, LANES), lambda i: (n - 1 - i, 0)), _whole((1, LANES))),
        scratch_shapes=[pltpu.VMEM((1, LANES), F32)],
        compiler_params=_params(("arbitrary",)),
    )(dcum, zf, bias)


LOG2E = 1.4426950408889634
UNROLL_SIZES = (8, 4, 2)


def _bias_terms(ct, pad):
    H, L = ct.shape
    T = ROW_TILE
    n = L // T
    c2 = ct * LOG2E
    k2 = jnp.where(jnp.arange(L)[None, :] < pad, -NEG, c2)
    return dict(cq_row=c2.reshape(H, n, 1, T), ck_row=k2.reshape(H, n, 1, T), ref=c2[:, ::T].reshape(H, n, 1, 1))


def _exchange_copies(srcs, outs, send_sems, recv_sems, local_sems, scatter):
    nt = len(srcs)
    x, y, c = lax.axis_index("x"), lax.axis_index("y"), lax.axis_index("c")
    me = 4 * x + 2 * y + c
    copies = [pltpu.make_async_copy(srcs[t].at[me] if scatter else srcs[t], outs[t].at[me], local_sems.at[t])
              for t in range(nt)]
    for k in range(1, N_DEV):
        px = (x + (k >> 2)) % 2
        py = (y + ((k >> 1) & 1)) % 2
        pc = (c + (k & 1)) % 2
        peer = 4 * px + 2 * py + pc
        for t in range(nt):
            copies.append(pltpu.make_async_remote_copy(
                src_ref=srcs[t].at[peer] if scatter else srcs[t],
                dst_ref=outs[t].at[me],
                send_sem=send_sems.at[k - 1, t], recv_sem=recv_sems.at[k - 1, t],
                device_id=(px, py, pc), device_id_type=pl.DeviceIdType.MESH))
    return copies


def _exchange_shapes(tensors, scatter):
    out_shape = tuple(jax.ShapeDtypeStruct(t.shape if scatter else (N_DEV,) + t.shape, t.dtype) for t in tensors)
    nt = len(tensors)
    sems = [pltpu.SemaphoreType.DMA((N_DEV - 1, nt)), pltpu.SemaphoreType.DMA((N_DEV - 1, nt)),
            pltpu.SemaphoreType.DMA((nt,))]
    return out_shape, sems


def _side_exchange(side, srcs, outs, sems, first, last):
    if not side:
        return

    @pl.when(first)
    def _():
        for cp in _exchange_copies(srcs, outs, *sems, side[1]):
            cp.start()

    @pl.when(last)
    def _():
        for cp in _exchange_copies(srcs, outs, *sems, side[1]):
            cp.wait()


def _as_row(col):
    return jnp.transpose(jnp.broadcast_to(col, (col.shape[0], LANES)))[0:1, :]


def _as_col(row):
    return jnp.transpose(jnp.broadcast_to(row, (LANES, row.shape[1])))[:, 0:1]


def _attn_fwd(q, qoff, k, koff, v, voff, zsrc, bias, heads, dk, dv, scale, name, dead, side=None):
    L = q.shape[0]
    T = ROW_TILE
    n = L // T
    c = scale * LOG2E
    dn_qk = (((1,), (1,)), ((), ()))

    ns = len(side[0]) if side else 0

    def body(*refs):
        q_ref, k_ref, v_ref, z_ref, ck_ref, ref_ref = refs[:6]
        o_ref, lse_ref, y_ref = refs[6 + ns:9 + ns]
        s_a, s_b = refs[9 + 2 * ns:11 + 2 * ns]
        i = pl.program_id(1)
        _side_exchange(side, refs[6:6 + ns], refs[9 + ns:9 + 2 * ns], refs[11 + 2 * ns:],
                       (pl.program_id(0) == 0) & (i == 0), (pl.program_id(0) == heads - 1) & (i == n - 1))
        qb = q_ref[...]
        ref = ref_ref[...]

        def scores(j, dst):
            start = pl.multiple_of(j * T, T)
            dst[...] = lax.dot_general(qb, k_ref[pl.ds(start, T), :], dn_qk, preferred_element_type=F32)

        def update(t, vj, carry):
            m, l, acc = carry
            m_new = jnp.maximum(m, jnp.max(t, axis=1, keepdims=True))
            p = jnp.exp2(t - m_new)
            a = jnp.exp2(m - m_new)
            l = a * l + jnp.sum(p, axis=1, keepdims=True)
            acc = a * acc + jnp.dot(p.astype(_MXU), vj, preferred_element_type=F32)
            return m_new, l, acc

        def soft(j, carry, cur, diag):
            start = pl.multiple_of(j * T, T)
            t = cur[...] * c - (ck_ref[j] - ref)
            if diag:
                r = lax.broadcasted_iota(jnp.int32, (T, T), 0)
                cc = lax.broadcasted_iota(jnp.int32, (T, T), 1)
                t = jnp.where(cc <= r, t, NEG)
            return update(t, v_ref[pl.ds(start, T), :], carry)

        def head_tile(carry):
            t = lax.dot_general(qb, k_ref[dead:T, :], dn_qk, preferred_element_type=F32) * c
            return update(t - (ck_ref[0][:, dead:] - ref), v_ref[dead:T, :], carry)

        def step(j, carry, cur, nxt):
            scores(j + 1, nxt)
            return soft(j, carry, cur, False)

        first = jnp.where(i > 0, 1, 0) if dead else 0

        def run_of(j0, steps, carry):
            for u in range(0, steps, 2):
                carry = step(j0 + u + 1, step(j0 + u, carry, s_a, s_b), s_b, s_a)
            return carry

        scores(first, s_a)
        carry = (jnp.full((T, 1), -jnp.inf, F32), jnp.zeros((T, 1), F32), jnp.zeros((T, dv), F32))
        if dead:
            carry = lax.cond(i > 0, head_tile, lambda cy: cy, carry)
        done = first
        for size in UNROLL_SIZES:
            trips = (i - done) // size
            carry = lax.fori_loop(0, trips, lambda g, cy, d=done, s=size: run_of(d + s * g, s, cy), carry)
            done = done + size * trips
        odd = (i - done) == 1
        carry = lax.cond(odd, lambda cy: step(i - 1, cy, s_a, s_b), lambda cy: cy, carry)
        m, l, acc = lax.cond(odd, lambda cy: soft(i, cy, s_b, True), lambda cy: soft(i, cy, s_a, True), carry)
        o = acc / l
        o_ref[...] = o
        y_ref[...] = (o * _silu(z_ref[...])).astype(_MXU)
        lse_ref[...] = _as_row(m + jnp.log2(l))

    hbm = pl.BlockSpec(memory_space=pl.ANY)
    side_shapes, side_sems = _exchange_shapes(*side) if side else ((), [])
    o, lse, y, *side_out = pl.pallas_call(
        body, name=name,
        out_shape=(jax.ShapeDtypeStruct((L, heads * dv), F32), jax.ShapeDtypeStruct((heads, n, 1, T), F32),
                   jax.ShapeDtypeStruct((L, heads * dv), _MXU)) + side_shapes,
        grid=(heads, n),
        in_specs=[pl.BlockSpec((T, dk), lambda h, i: (i, qoff + h)),
                  pl.BlockSpec((L, dk), lambda h, i: (0, koff + h)),
                  pl.BlockSpec((L, dv), lambda h, i: (0, voff + h)),
                  pl.BlockSpec((T, dv), lambda h, i: (i, h)),
                  pl.BlockSpec((None, n, 1, T), lambda h, i: (h, 0, 0, 0)),
                  pl.BlockSpec((None, None, 1, 1), lambda h, i: (h, i, 0, 0))] + [hbm] * ns,
        out_specs=(pl.BlockSpec((T, dv), lambda h, i: (i, h)),
                   pl.BlockSpec((None, None, 1, T), lambda h, i: (h, i, 0, 0)),
                   pl.BlockSpec((T, dv), lambda h, i: (i, h))) + (hbm,) * ns,
        scratch_shapes=[pltpu.VMEM((T, T), F32), pltpu.VMEM((T, T), F32)] + side_sems,
        compiler_params=_params(("arbitrary", "arbitrary")),
    )(q, k, v, zsrc, bias["ck_row"], bias["ref"], *(side[0] if side else ()))
    return o, lse + (bias["cq_row"] - bias["ref"]), y, side_out


def _attn_bwd(q, qoff, k, koff, v, voff, do, bias, lse_row, delta_row, heads, dk, dv, scale, name, dead,
              side=None, bias_grads=True):
    L = q.shape[0]
    T = ROW_TILE
    n = L // T
    c = scale * LOG2E
    dn_nt = (((1,), (1,)), ((), ()))
    dn_tn = (((0,), (0,)), ((), ()))
    ns = len(side[0]) if side else 0

    def body(*refs):
        q_ref, k_ref, v_ref, do_ref, cq_ref, ck_ref, ref_ref, lse_ref, dl_ref = refs[:9]
        dq_hbm, dk_ref, dv_ref, dck_ref, dcq_ref = refs[9 + ns:14 + ns]
        dq_acc, sem, st_a, dp_a, st_b, dp_b = refs[14 + 2 * ns:20 + 2 * ns]
        h = pl.program_id(0)
        j = pl.program_id(1)
        _side_exchange(side, refs[9:9 + ns], refs[14 + ns:14 + 2 * ns], refs[20 + 2 * ns:],
                       (h == 0) & (j == 0), (h == heads - 1) & (j == n - 1))

        @pl.when(j == 0)
        def _():
            dq_acc[...] = jnp.zeros_like(dq_acc)
            dcq_ref[...] = jnp.zeros_like(dcq_ref)

        ref = ref_ref[...]

        def run(k0, R):
            kb = k_ref[k0:k0 + R, :]
            vb = v_ref[k0:k0 + R, :]
            bcol = _as_col(ck_ref[j][:, k0:k0 + R] - ref)

            def front(i, st_dst, dp_dst):
                start = pl.multiple_of(jnp.minimum(i, n - 1) * T, T)
                st_dst[0:R, :] = lax.dot_general(kb, q_ref[pl.ds(start, T), :], dn_nt, preferred_element_type=F32)
                dp_dst[0:R, :] = lax.dot_general(vb, do_ref[pl.ds(start, T), :], dn_nt, preferred_element_type=F32)

            def back(i, carry, st_cur, dp_cur, diag):
                dk_a, dv_a, dck_a = carry
                start = pl.multiple_of(i * T, T)
                qi = q_ref[pl.ds(start, T), :]
                doi = do_ref[pl.ds(start, T), :]
                arow = (cq_ref[i] - ref) - lse_ref[i]
                st = st_cur[0:R, :] * c + arow - bcol
                if diag:
                    r = lax.broadcasted_iota(jnp.int32, (R, T), 0) + k0
                    cc = lax.broadcasted_iota(jnp.int32, (R, T), 1)
                    st = jnp.where(r <= cc, st, NEG)
                pt = jnp.exp2(st)
                dv_a = dv_a + jnp.dot(pt.astype(_MXU), doi, preferred_element_type=F32)
                dst = pt * (dp_cur[0:R, :] - dl_ref[i])
                if bias_grads:
                    dck_a = dck_a - jnp.sum(dst, axis=1, keepdims=True)
                    dcq_ref[i] += jnp.sum(dst, axis=0, keepdims=True)
                dsb = (dst * scale).astype(_MXU)
                dk_a = dk_a + jnp.dot(dsb, qi, preferred_element_type=F32)
                dq_acc[pl.ds(start, T), :] += lax.dot_general(dsb, kb, dn_tn, preferred_element_type=F32)
                return dk_a, dv_a, dck_a

            buf_a, buf_b = (st_a, dp_a), (st_b, dp_b)

            def step(i, carry, cur, nxt, diag=False):
                front(i + 1, *nxt)
                return back(i, carry, *cur, diag)

            def run_of(i0, steps, carry):
                for u in range(0, steps, 2):
                    carry = step(i0 + u + 1, step(i0 + u, carry, buf_b, buf_a), buf_a, buf_b)
                return carry

            front(j, *buf_a)
            init = (jnp.zeros((R, dk), F32), jnp.zeros((R, dv), F32), jnp.zeros((R, 1), F32))
            carry = step(j, init, buf_a, buf_b, True)
            done = j + 1
            for size in UNROLL_SIZES:
                trips = (n - done) // size
                carry = lax.fori_loop(0, trips, lambda g, cy, d=done, s=size: run_of(d + s * g, s, cy), carry)
                done = done + size * trips
            dk_a, dv_a, dck_a = lax.cond((n - done) == 1, lambda cy: step(n - 1, cy, buf_b, buf_a),
                                         lambda cy: cy, carry)
            if k0:
                dk_ref[0:k0, :] = jnp.zeros((k0, dk), F32)
                dv_ref[0:k0, :] = jnp.zeros((k0, dv), F32)
                dck_ref[:, 0:k0] = jnp.zeros((1, k0), F32)
            dk_ref[k0:k0 + R, :] = dk_a
            dv_ref[k0:k0 + R, :] = dv_a
            dck_ref[:, k0:k0 + R] = _as_row(dck_a)

        if dead:
            pl.when(j == 0)(lambda: run(dead, T - dead))
            pl.when(j > 0)(lambda: run(0, T))
        else:
            run(0, T)

        @pl.when(j == n - 1)
        def _():
            cp = pltpu.make_async_copy(dq_acc, dq_hbm.at[:, pl.ds(pl.multiple_of(h * dk, dk), dk)], sem)
            cp.start()
            cp.wait()

    hbm = pl.BlockSpec(memory_space=pl.ANY)
    side_shapes, side_sems = _exchange_shapes(*side) if side else ((), [])
    dq, dk_, dv_, dck, dcq, *side_out = pl.pallas_call(
        body, name=name,
        out_shape=(jax.ShapeDtypeStruct((L, heads * dk), F32), jax.ShapeDtypeStruct((L, heads * dk), F32),
                   jax.ShapeDtypeStruct((L, heads * dv), F32), jax.ShapeDtypeStruct((heads, n, 1, T), F32),
                   jax.ShapeDtypeStruct((heads, n, 1, T), F32)) + side_shapes,
        grid=(heads, n),
        in_specs=[pl.BlockSpec((L, dk), lambda h, j: (0, qoff + h)),
                  pl.BlockSpec((T, dk), lambda h, j: (j, koff + h)),
                  pl.BlockSpec((T, dv), lambda h, j: (j, voff + h)),
                  pl.BlockSpec((L, dv), lambda h, j: (0, h)),
                  pl.BlockSpec((None, n, 1, T), lambda h, j: (h, 0, 0, 0)),
                  pl.BlockSpec((None, n, 1, T), lambda h, j: (h, 0, 0, 0)),
                  pl.BlockSpec((None, None, 1, 1), lambda h, j: (h, j, 0, 0)),
                  pl.BlockSpec((None, n, 1, T), lambda h, j: (h, 0, 0, 0)),
                  pl.BlockSpec((None, n, 1, T), lambda h, j: (h, 0, 0, 0))] + [hbm] * ns,
        out_specs=(hbm,
                   pl.BlockSpec((T, dk), lambda h, j: (j, h)),
                   pl.BlockSpec((T, dv), lambda h, j: (j, h)),
                   pl.BlockSpec((None, None, 1, T), lambda h, j: (h, j, 0, 0)),
                   pl.BlockSpec((None, n, 1, T), lambda h, j: (h, 0, 0, 0))) + (hbm,) * ns,
        scratch_shapes=[pltpu.VMEM((L, dk), F32), pltpu.SemaphoreType.DMA] + [pltpu.VMEM((T, T), F32)] * 4
        + side_sems,
        compiler_params=_params(("arbitrary", "arbitrary")),
    )(q, k, v, do, bias["cq_row"], bias["ck_row"], bias["ref"], lse_row, delta_row, *(side[0] if side else ()))
    return dq, dk_, dv_, dck, dcq, side_out


_CQ0, _CKV0, _KR0 = D_MODEL, D_MODEL + MLA_Q_LORA, D_MODEL + MLA_Q_LORA + MLA_KV_LORA
_MLA_PROJ = _KR0 + LANES


def _rms(x, g):
    ms = jnp.mean(x * x, axis=-1, keepdims=True)
    return x * lax.rsqrt(ms + NORM_EPS) * g


def _rms_bwd(x, g, dy):
    ms = jnp.mean(x * x, axis=-1, keepdims=True)
    r = lax.rsqrt(ms + NORM_EPS)
    xh = x * r
    dxh = dy * g
    dx = r * (dxh - xh * jnp.mean(dxh * xh, axis=-1, keepdims=True))
    return dx, jnp.sum(dy * xh, axis=0, keepdims=True)


def _mla_norm_fwd(proj, gq, gkv):
    L = proj.shape[0]

    def body(p_ref, gq_ref, gkv_ref, cq_ref, ckv_ref):
        cq_ref[...] = _rms(p_ref[:, _CQ0:_CKV0], gq_ref[...]).astype(_MXU)
        ckv_ref[...] = _rms(p_ref[:, _CKV0:_KR0], gkv_ref[...]).astype(_MXU)

    return pl.pallas_call(
        body, name="mla_norm_fwd",
        out_shape=(jax.ShapeDtypeStruct((L, MLA_Q_LORA), _MXU), jax.ShapeDtypeStruct((L, MLA_KV_LORA), _MXU)),
        grid=(L // ROW_TILE,),
        in_specs=[_rows(_MLA_PROJ), _whole((1, MLA_Q_LORA)), _whole((1, MLA_KV_LORA))],
        out_specs=(_rows(MLA_Q_LORA), _rows(MLA_KV_LORA)),
        compiler_params=_params(("parallel",)),
    )(proj, gq, gkv)


def _mla_norm_bwd(proj, gq, gkv, dcqn, dckvn):
    L = proj.shape[0]

    def body(p_ref, gq_ref, gkv_ref, dq_ref, dkv_ref, dcq_ref, dckv_ref, dgq_ref, dgkv_ref):
        @pl.when(pl.program_id(0) == 0)
        def _():
            dgq_ref[...] = jnp.zeros_like(dgq_ref)
            dgkv_ref[...] = jnp.zeros_like(dgkv_ref)

        dx, dg = _rms_bwd(p_ref[:, _CQ0:_CKV0], gq_ref[...], dq_ref[...])
        dcq_ref[...] = dx.astype(_MXU)
        dgq_ref[...] += dg
        dx, dg = _rms_bwd(p_ref[:, _CKV0:_KR0], gkv_ref[...], dkv_ref[...])
        dckv_ref[...] = dx.astype(_MXU)
        dgkv_ref[...] += dg

    return pl.pallas_call(
        body, name="mla_norm_bwd",
        out_shape=(jax.ShapeDtypeStruct((L, MLA_Q_LORA), _MXU), jax.ShapeDtypeStruct((L, MLA_KV_LORA), _MXU),
                   jax.ShapeDtypeStruct((1, MLA_Q_LORA), F32), jax.ShapeDtypeStruct((1, MLA_KV_LORA), F32)),
        grid=(L // ROW_TILE,),
        in_specs=[_rows(_MLA_PROJ), _whole((1, MLA_Q_LORA)), _whole((1, MLA_KV_LORA)),
                  _rows(MLA_Q_LORA), _rows(MLA_KV_LORA)],
        out_specs=(_rows(MLA_Q_LORA), _rows(MLA_KV_LORA), _whole((1, MLA_Q_LORA)), _whole((1, MLA_KV_LORA))),
        compiler_params=_params(("arbitrary",)),
    )(proj, gq, gkv, dcqn, dckvn)


def _rot_tile(t, cos, sa, sb):
    half = MLA_ROPE // 2
    return t * cos + pltpu.roll(t, LANES - half, 1) * sa + pltpu.roll(t, half, 1) * sb


def _mla_rope_fwd(qf, kv, proj, cos, sa, sb):
    L = qf.shape[0]
    H = MLA_HEADS

    def body(q_ref, kv_ref, p_ref, c_ref, sa_ref, sb_ref, qo_ref, ko_ref):
        cos, sa, sb = c_ref[...], sa_ref[...], sb_ref[...]
        kr = _rot_tile(p_ref[...], cos, sa, sb).astype(_MXU)
        for h in range(H):
            b = h * MLA_DK
            qo_ref[:, b:b + LANES] = q_ref[:, b:b + LANES].astype(_MXU)
            qo_ref[:, b + LANES:b + 2 * LANES] = _rot_tile(q_ref[:, b + LANES:b + 2 * LANES], cos, sa, sb).astype(_MXU)
            ko_ref[:, b:b + LANES] = kv_ref[:, h * LANES:(h + 1) * LANES]
            ko_ref[:, b + LANES:b + 2 * LANES] = kr

    W = H * MLA_DK
    return pl.pallas_call(
        body, name="mla_rope_fwd",
        out_shape=(jax.ShapeDtypeStruct((L, W), _MXU), jax.ShapeDtypeStruct((L, W), _MXU)),
        grid=(L // ROW_TILE,),
        in_specs=[_rows(W), _rows(kv.shape[1]), pl.BlockSpec((ROW_TILE, LANES), lambda i: (i, _KR0 // LANES)),
                  _rows(LANES), _rows(LANES), _rows(LANES)],
        out_specs=(_rows(W), _rows(W)),
        compiler_params=_params(("parallel",)),
    )(qf, kv, proj, cos, sa, sb)


def _mla_rope_bwd(dq_full, dk_full, cos, sa, sb):
    L = dq_full.shape[0]
    H = MLA_HEADS
    W = H * MLA_DK

    def body(dq_ref, dk_ref, c_ref, sa_ref, sb_ref, dqf_ref, dkn_ref, dkr_ref):
        cos, sa, sb = c_ref[...], sa_ref[...], sb_ref[...]
        lane = lax.broadcasted_iota(jnp.int32, (ROW_TILE, LANES), 1)
        live = lane < MLA_ROPE
        krs = jnp.zeros((ROW_TILE, LANES), F32)
        for h in range(H):
            b = h * MLA_DK
            dqf_ref[:, b:b + LANES] = dq_ref[:, b:b + LANES].astype(_MXU)
            dqr = _rot_tile(dq_ref[:, b + LANES:b + 2 * LANES], cos, sa, sb)
            dqf_ref[:, b + LANES:b + 2 * LANES] = jnp.where(live, dqr, 0.0).astype(_MXU)
            dkn_ref[:, h * LANES:(h + 1) * LANES] = dk_ref[:, b:b + LANES].astype(_MXU)
            krs = krs + dk_ref[:, b + LANES:b + 2 * LANES]
        dkr_ref[...] = jnp.where(live, _rot_tile(krs, cos, sa, sb), 0.0).astype(_MXU)

    return pl.pallas_call(
        body, name="mla_rope_bwd",
        out_shape=(jax.ShapeDtypeStruct((L, W), _MXU), jax.ShapeDtypeStruct((L, H * LANES), _MXU),
                   jax.ShapeDtypeStruct((L, LANES), _MXU)),
        grid=(L // ROW_TILE,),
        in_specs=[_rows(W), _rows(W), _rows(LANES), _rows(LANES), _rows(LANES)],
        out_specs=(_rows(W), _rows(H * LANES), _rows(LANES)),
        compiler_params=_params(("parallel",)),
    )(dq_full, dk_full, cos, sa, sb)


_RET_QKW = RET_HEADS * RET_QK


def _ret_rope_fwd(qkz, cos, sin, pad):
    L = qkz.shape[0]
    hh = RET_QK // 2
    kscale = RET_QK ** -0.5

    def body(p_ref, c_ref, s_ref, q_ref, k_ref):
        cos, sin = c_ref[...], s_ref[...]
        row = pl.program_id(0) * ROW_TILE + lax.broadcasted_iota(jnp.int32, (ROW_TILE, hh), 0)
        keep = row >= pad
        for h in range(RET_HEADS):
            b = h * RET_QK
            t1, t2 = p_ref[:, b:b + hh], p_ref[:, b + hh:b + 2 * hh]
            q_ref[:, b:b + hh] = (t1 * cos - t2 * sin).astype(_MXU)
            q_ref[:, b + hh:b + 2 * hh] = (t2 * cos + t1 * sin).astype(_MXU)
            t1, t2 = p_ref[:, _RET_QKW + b:_RET_QKW + b + hh], p_ref[:, _RET_QKW + b + hh:_RET_QKW + b + 2 * hh]
            k_ref[:, b:b + hh] = jnp.where(keep, (t1 * cos - t2 * sin) * kscale, 0.0)
            k_ref[:, b + hh:b + 2 * hh] = jnp.where(keep, (t2 * cos + t1 * sin) * kscale, 0.0)

    return pl.pallas_call(
        body, name="ret_rope_fwd",
        out_shape=(jax.ShapeDtypeStruct((L, _RET_QKW), _MXU), jax.ShapeDtypeStruct((L, _RET_QKW), F32)),
        grid=(L // ROW_TILE,),
        in_specs=[_rows(2 * _RET_QKW), _rows(hh), _rows(hh)],
        out_specs=(_rows(_RET_QKW), _rows(_RET_QKW)),
        compiler_params=_params(("parallel",)),
    )(qkz, cos, sin)


def _ret_rope_bwd(dqr, dkr, cos, sin, pad):
    L = dqr.shape[0]
    hh = RET_QK // 2
    kscale = RET_QK ** -0.5

    def body(dq_ref, dk_ref, c_ref, s_ref, oq_ref, ok_ref):
        cos, sin = c_ref[...], s_ref[...]
        row = pl.program_id(0) * ROW_TILE + lax.broadcasted_iota(jnp.int32, (ROW_TILE, hh), 0)
        keep = row >= pad
        for h in range(RET_HEADS):
            b = h * RET_QK
            d1, d2 = dq_ref[:, b:b + hh], dq_ref[:, b + hh:b + 2 * hh]
            oq_ref[:, b:b + hh] = (d1 * cos + d2 * sin).astype(_MXU)
            oq_ref[:, b + hh:b + 2 * hh] = (d2 * cos - d1 * sin).astype(_MXU)
            d1, d2 = dk_ref[:, b:b + hh], dk_ref[:, b + hh:b + 2 * hh]
            ok_ref[:, b:b + hh] = jnp.where(keep, (d1 * cos + d2 * sin) * kscale, 0.0).astype(_MXU)
            ok_ref[:, b + hh:b + 2 * hh] = jnp.where(keep, (d2 * cos - d1 * sin) * kscale, 0.0).astype(_MXU)

    return pl.pallas_call(
        body, name="ret_rope_bwd",
        out_shape=(jax.ShapeDtypeStruct((L, _RET_QKW), _MXU), jax.ShapeDtypeStruct((L, _RET_QKW), _MXU)),
        grid=(L // ROW_TILE,),
        in_specs=[_rows(_RET_QKW), _rows(_RET_QKW), _rows(hh), _rows(hh)],
        out_specs=(_rows(_RET_QKW), _rows(_RET_QKW)),
        compiler_params=_params(("parallel",)),
    )(dqr, dkr, cos, sin)


def _ret_rows(L):
    return next(r * ROW_TILE for r in (3, 2, 1) if L % (r * ROW_TILE) == 0)


def _ret_decays():
    lg = np.log1p(-np.exp2(-5.0 - np.arange(RET_HEADS, dtype=np.float32))).astype(np.float32)
    i = np.arange(CHUNK, dtype=np.float32)
    rel = i[:, None] - i[None, :]
    dm = np.where(rel[None] >= 0, np.exp(rel[None] * lg[:, None, None]), 0.0).astype(np.float32)
    qd = np.exp((i[None, :] + 1.0) * lg[:, None]).astype(np.float32)[:, :, None]
    kd = np.exp((CHUNK - 1.0 - i)[None, :] * lg[:, None]).astype(np.float32)[:, :, None]
    cd = np.broadcast_to(np.exp(CHUNK * lg).astype(np.float32)[:, None, None], (RET_HEADS, 1, RET_V))
    return jnp.asarray(dm), jnp.asarray(qd), jnp.asarray(kd), jnp.asarray(np.ascontiguousarray(cd))


def _ret_fwd(qr, kr, v):
    L = qr.shape[0]
    nc = L // CHUNK
    RB = _ret_rows(L)
    G = RB // CHUNK
    dm, qd, kd, cd = _ret_decays()
    dn_nt = (((1,), (1,)), ((), ()))
    dn_tn = (((0,), (0,)), ((), ()))

    def body(q_ref, k_ref, v_ref, dm_ref, qd_ref, kd_ref, cd_ref, o_ref, st_ref, s_ref):
        @pl.when(pl.program_id(1) == 0)
        def _():
            s_ref[...] = jnp.zeros_like(s_ref)

        state = s_ref[...]
        for g in range(G):
            rows = slice(g * CHUNK, (g + 1) * CHUNK)
            q, k, vv = q_ref[rows, :], k_ref[rows, :], v_ref[rows, :]
            sb = state.astype(_MXU)
            st_ref[g] = sb
            s = lax.dot_general(q, k.astype(_MXU), dn_nt, preferred_element_type=F32) * dm_ref[...]
            intra = jnp.dot(s.astype(_MXU), vv, preferred_element_type=F32)
            cross = jnp.dot(q, sb, preferred_element_type=F32) * qd_ref[...]
            o_ref[rows, :] = intra + cross
            kdm = (k * kd_ref[...]).astype(_MXU)
            state = cd_ref[...] * state + lax.dot_general(kdm, vv, dn_tn, preferred_element_type=F32)
        s_ref[...] = state

    return pl.pallas_call(
        body, name="ret_fwd",
        out_shape=(jax.ShapeDtypeStruct((L, RET_HEADS * RET_V), F32),
                   jax.ShapeDtypeStruct((RET_HEADS, nc, RET_QK, RET_V), _MXU)),
        grid=(RET_HEADS, nc // G),
        in_specs=[pl.BlockSpec((RB, RET_QK), lambda h, c: (c, h)),
                  pl.BlockSpec((RB, RET_QK), lambda h, c: (c, h)),
                  pl.BlockSpec((RB, RET_V), lambda h, c: (c, h)),
                  pl.BlockSpec((None, CHUNK, CHUNK), lambda h, c: (h, 0, 0)),
                  pl.BlockSpec((None, CHUNK, 1), lambda h, c: (h, 0, 0)),
                  pl.BlockSpec((None, CHUNK, 1), lambda h, c: (h, 0, 0)),
                  pl.BlockSpec((None, 1, RET_V), lambda h, c: (h, 0, 0))],
        out_specs=(pl.BlockSpec((RB, RET_V), lambda h, c: (c, h)),
                   pl.BlockSpec((None, G, RET_QK, RET_V), lambda h, c: (h, c, 0, 0))),
        scratch_shapes=[pltpu.VMEM((RET_QK, RET_V), F32)],
        compiler_params=_params(("parallel", "arbitrary")),
    )(qr, kr, v, dm, qd, kd, cd)


def _ret_bwd(qr, kr, v, do, states):
    L = qr.shape[0]
    nc = L // CHUNK
    RB = _ret_rows(L)
    G = RB // CHUNK
    nt = nc // G
    dm, qd, kd, cd = _ret_decays()
    dn_nt = (((1,), (1,)), ((), ()))
    dn_tn = (((0,), (0,)), ((), ()))

    def body(q_ref, k_ref, v_ref, do_ref, st_ref, dm_ref, qd_ref, kd_ref, cd_ref,
             dq_ref, dk_ref, dv_ref, ds_ref):
        @pl.when(pl.program_id(1) == 0)
        def _():
            ds_ref[...] = jnp.zeros_like(ds_ref)

        dstate = ds_ref[...]
        dmat = dm_ref[...]
        for g in reversed(range(G)):
            rows = slice(g * CHUNK, (g + 1) * CHUNK)
            q, k, vv, do = q_ref[rows, :], k_ref[rows, :], v_ref[rows, :], do_ref[rows, :]
            kb = k.astype(_MXU)
            dob = do.astype(_MXU)
            doq = (do * qd_ref[...]).astype(_MXU)
            dsb = dstate.astype(_MXU)
            s = (lax.dot_general(q, kb, dn_nt, preferred_element_type=F32) * dmat).astype(_MXU)
            draw = (lax.dot_general(dob, vv, dn_nt, preferred_element_type=F32) * dmat).astype(_MXU)
            kdm = (k * kd_ref[...]).astype(_MXU)
            dv = (lax.dot_general(s, dob, dn_tn, preferred_element_type=F32)
                  + jnp.dot(kdm, dsb, preferred_element_type=F32))
            dq = (jnp.dot(draw, kb, preferred_element_type=F32)
                  + lax.dot_general(doq, st_ref[g], dn_nt, preferred_element_type=F32))
            dk = (lax.dot_general(draw, q, dn_tn, preferred_element_type=F32)
                  + lax.dot_general(vv, dsb, dn_nt, preferred_element_type=F32) * kd_ref[...])
            dq_ref[rows, :] = dq
            dk_ref[rows, :] = dk
            dv_ref[rows, :] = dv.astype(_MXU)
            dstate = cd_ref[...] * dstate + lax.dot_general(q, doq, dn_tn, preferred_element_type=F32)
        ds_ref[...] = dstate

    rev = lambda h, c: (nt - 1 - c, h)
    return pl.pallas_call(
        body, name="ret_bwd",
        out_shape=(jax.ShapeDtypeStruct((L, _RET_QKW), F32), jax.ShapeDtypeStruct((L, _RET_QKW), F32),
                   jax.ShapeDtypeStruct((L, RET_HEADS * RET_V), _MXU)),
        grid=(RET_HEADS, nt),
        in_specs=[pl.BlockSpec((RB, RET_QK), rev),
                  pl.BlockSpec((RB, RET_QK), rev),
                  pl.BlockSpec((RB, RET_V), rev),
                  pl.BlockSpec((RB, RET_V), rev),
                  pl.BlockSpec((None, G, RET_QK, RET_V), lambda h, c: (h, nt - 1 - c, 0, 0)),
                  pl.BlockSpec((None, CHUNK, CHUNK), lambda h, c: (h, 0, 0)),
                  pl.BlockSpec((None, CHUNK, 1), lambda h, c: (h, 0, 0)),
                  pl.BlockSpec((None, CHUNK, 1), lambda h, c: (h, 0, 0)),
                  pl.BlockSpec((None, 1, RET_V), lambda h, c: (h, 0, 0))],
        out_specs=(pl.BlockSpec((RB, RET_QK), rev), pl.BlockSpec((RB, RET_QK), rev),
                   pl.BlockSpec((RB, RET_V), rev)),
        scratch_shapes=[pltpu.VMEM((RET_QK, RET_V), F32)],
        compiler_params=_params(("parallel", "arbitrary")),
    )(qr, kr, v, do, states, dm, qd, kd, cd)


def _gn(o, g):
    mu = jnp.mean(o, axis=-1, keepdims=True)
    oc = o - mu
    var = jnp.mean(oc * oc, axis=-1, keepdims=True)
    return oc * lax.rsqrt(var + NORM_EPS) * g


def _ret_gate_fwd(o, qkz, gn_g):
    L, W = o.shape
    zblk = qkz.shape[1] // W - 1

    def body(o_ref, z_ref, g_ref, y_ref):
        for h in range(RET_HEADS):
            sl = slice(h * RET_V, (h + 1) * RET_V)
            y_ref[:, sl] = (_gn(o_ref[:, sl], g_ref[:, sl]) * _silu(z_ref[:, sl])).astype(_MXU)

    return pl.pallas_call(
        body, name="ret_gate_fwd",
        out_shape=jax.ShapeDtypeStruct((L, W), _MXU),
        grid=(L // ROW_TILE,),
        in_specs=[_rows(W), pl.BlockSpec((ROW_TILE, W), lambda i: (i, zblk)), _whole((1, W))],
        out_specs=_rows(W),
        compiler_params=_params(("parallel",)),
    )(o, qkz, gn_g)


def _ret_gate_bwd(dzl, w_out, o, qkz, gn_g):
    L, W = o.shape
    zblk = qkz.shape[1] // W - 1

    def body(c_ref, w_ref, o_ref, z_ref, g_ref, do_ref, dz_ref, dg_ref):
        @pl.when(pl.program_id(0) == 0)
        def _():
            dg_ref[...] = jnp.zeros_like(dg_ref)

        dy_all = lax.dot_general(c_ref[...].astype(_MXU), w_ref[...].astype(_MXU), (((1,), (1,)), ((), ())),
                                 preferred_element_type=F32)
        for h in range(RET_HEADS):
            sl = slice(h * RET_V, (h + 1) * RET_V)
            z = z_ref[:, sl]
            o = o_ref[:, sl]
            g = g_ref[:, sl]
            dy = dy_all[:, sl]
            mu = jnp.mean(o, axis=-1, keepdims=True)
            oc = o - mu
            var = jnp.mean(oc * oc, axis=-1, keepdims=True)
            r = lax.rsqrt(var + NORM_EPS)
            xh = oc * r
            dn = dy * _silu(z)
            dz_ref[:, sl] = (dy * (xh * g) * _dsilu(z)).astype(_MXU)
            dg_ref[:, sl] += jnp.sum(dn * xh, axis=0, keepdims=True)
            dxh = dn * g
            do_ref[:, sl] = r * (dxh - jnp.mean(dxh, axis=-1, keepdims=True)
                                 - xh * jnp.mean(dxh * xh, axis=-1, keepdims=True))

    return pl.pallas_call(
        body, name="ret_gate_bwd",
        out_shape=(jax.ShapeDtypeStruct((L, W), F32), jax.ShapeDtypeStruct((L, W), _MXU),
                   jax.ShapeDtypeStruct((1, W), F32)),
        grid=(L // ROW_TILE,),
        in_specs=[_rows(D_MODEL), _whole((W, D_MODEL)), _rows(W), pl.BlockSpec((ROW_TILE, W), lambda i: (i, zblk)),
                  _whole((1, W))],
        out_specs=(_rows(W), _rows(W), _whole((1, W))),
        compiler_params=_params(("arbitrary",)),
    )(dzl, w_out, o, qkz, gn_g)


def _dead_rows(pad):
    return pad // LANES * LANES


def _heads_t(a):
    return a[:, :FOX_HEADS].T


def _fox_fwd(h, w, lnw, pad, tag, side=None):
    w_in, b_f, w_out = w
    L = h.shape[0]
    W = D_MODEL
    w_qkv = w_in[:, :3 * W]
    w_zf = jnp.concatenate([w_in[:, 3 * W:4 * W], jnp.pad(w_in[:, 4 * W:], ((0, 0), (0, LANES - FOX_HEADS)))], axis=1)
    bias = jnp.pad(b_f, (0, LANES - FOX_HEADS))[None, :]
    qkv = _matmul(h, w_qkv, "nn", _MXU, f"{tag}_qkv")
    zf = _matmul(h, w_zf, "nn", F32, f"{tag}_zf")
    cum = _fox_cum(zf, bias, pad, f"{tag}_cum")
    ab = _bias_terms(_heads_t(cum), pad)
    nb = W // LANES
    o, lse, y, side_out = _attn_fwd(qkv, 0, qkv, nb, qkv, 2 * nb, zf, ab, FOX_HEADS, LANES, LANES, LANES ** -0.5,
                                    f"{tag}_attn_fwd", _dead_rows(pad), side)
    zl, hn = _out_ln(y, w_out, h, lnw[0], lnw[1], f"{tag}_out_ln")
    return hn, dict(h=h, qkv=qkv, zf=zf, ab=ab, o=o, lse=lse, y=y, zl=zl,
                    w_all=jnp.concatenate([w_qkv, w_zf], axis=1), w_out=w_out, bias=bias), side_out


def _fox_bwd(s, dzl, pad, tag, side=None, send_own=False):
    W = D_MODEL
    nb = W // LANES
    d_wout = _matmul(s["y"], dzl, "tn", F32, f"{tag}_dwout")
    if send_own:
        side = (list(side[0]) + [_split_layer("fox_w_out", d_wout)], True)
    do, delta, dzg = _gate_bwd(dzl, s["w_out"], s["o"], s["zf"], FOX_HEADS, f"{tag}_gate_bwd")
    qkv = s["qkv"]
    dq, dk, dv, dck, dcq, side_out = _attn_bwd(qkv, 0, qkv, nb, qkv, 2 * nb, do, s["ab"], s["lse"], delta,
                                               FOX_HEADS, LANES, LANES, LANES ** -0.5, f"{tag}_attn_bwd",
                                               _dead_rows(pad), side)
    dcum = jnp.pad((dck + dcq).reshape(FOX_HEADS, -1).T, ((0, 0), (0, LANES - FOX_HEADS)))
    df, dbias = _fox_cum_bwd(dcum, s["zf"], s["bias"], pad, f"{tag}_cum_bwd")
    dproj = jnp.concatenate([dq.astype(_MXU), dk.astype(_MXU), dv.astype(_MXU), dzg, df], axis=1)
    d_wall = _matmul(s["h"], dproj, "tn", F32, f"{tag}_dwin")
    d_win = jnp.concatenate([d_wall[:, :4 * W], d_wall[:, 4 * W:4 * W + FOX_HEADS]], axis=1)
    if send_own:
        dh, win_parts = _matmul(dproj, s["w_all"], "nt", F32, f"{tag}_dh", ([_split_layer("fox_w_in", d_win)], True))
        side_out = list(side_out) + list(win_parts)
    else:
        dh = _matmul(dproj, s["w_all"], "nt", F32, f"{tag}_dh")
    return dh, (d_win, dbias[0, :FOX_HEADS], d_wout), side_out


def _mla_tables(L, pad):
    pos = (jnp.arange(L) - pad).astype(F32)
    inv = ROPE_BASE ** (-jnp.arange(0, MLA_ROPE, 2, dtype=F32) / MLA_ROPE)
    ang = pos[:, None] * inv[None, :]
    c, s = jnp.cos(ang), jnp.sin(ang)
    z32, z64 = jnp.zeros_like(s), jnp.zeros((L, LANES - MLA_ROPE), F32)
    cos = jnp.concatenate([c, c, jnp.ones_like(z64)], axis=1)
    sa = jnp.concatenate([-s, z32, z64], axis=1)
    sb = jnp.concatenate([z32, s, z64], axis=1)
    return cos, sa, sb


def _mla_weights(w_in, w_uq, w_ukv):
    a, b = MLA_Q_LORA, MLA_Q_LORA + MLA_KV_LORA
    w_zc = jnp.concatenate([w_in[:, b + MLA_ROPE:], w_in[:, :b + MLA_ROPE],
                            jnp.zeros((D_MODEL, LANES - MLA_ROPE), w_in.dtype)], axis=1)
    uq = w_uq.reshape(MLA_Q_LORA, MLA_HEADS, MLA_NOPE + MLA_ROPE)
    uq = jnp.pad(uq, ((0, 0), (0, 0), (0, MLA_DK - MLA_NOPE - MLA_ROPE))).reshape(MLA_Q_LORA, MLA_HEADS * MLA_DK)
    ukv = w_ukv.reshape(MLA_KV_LORA, MLA_HEADS, MLA_NOPE + MLA_V)
    ukv = jnp.concatenate([ukv[:, :, :MLA_NOPE].reshape(MLA_KV_LORA, -1), ukv[:, :, MLA_NOPE:].reshape(MLA_KV_LORA, -1)], axis=1)
    return w_zc, uq, ukv


def _mla_fwd(h, w, lnw, pad):
    w_in, gq, gkv, w_uq, w_ukv, w_out = w
    L = h.shape[0]
    w_zc, uq, ukv = _mla_weights(w_in, w_uq, w_ukv)
    cos, sa, sb = _mla_tables(L, pad)
    proj = _matmul(h, w_zc, "nn", F32, "mla_proj")
    cqn, ckvn = _mla_norm_fwd(proj, gq[None, :], gkv[None, :])
    qf = _matmul(cqn, uq, "nn", F32, "mla_uq")
    kv = _matmul(ckvn, ukv, "nn", _MXU, "mla_ukv")
    q_full, k_full = _mla_rope_fwd(qf, kv, proj, cos, sa, sb)
    ab = _bias_terms(jnp.zeros((MLA_HEADS, L), F32), pad)
    nb = MLA_HEADS * MLA_NOPE // LANES
    o, lse, y, _ = _attn_fwd(q_full, 0, k_full, 0, kv, nb, proj, ab, MLA_HEADS, MLA_DK, MLA_V,
                             (MLA_NOPE + MLA_ROPE) ** -0.5, "mla_attn_fwd", _dead_rows(pad))
    zl, hn = _out_ln(y, w_out, h, lnw[0], lnw[1], "mla_out_ln")
    return hn, dict(h=h, proj=proj, cqn=cqn, ckvn=ckvn, kv=kv, q_full=q_full, k_full=k_full, ab=ab,
                    dead=_dead_rows(pad),
                    o=o, lse=lse, y=y, zl=zl, w_zc=w_zc, uq=uq, ukv=ukv, w_out=w_out, gq=gq, gkv=gkv,
                    tabs=(cos, sa, sb))


def _mla_bwd(s, dzl):
    H = MLA_HEADS
    d_wout = _matmul(s["y"], dzl, "tn", F32, "mla_dwout")
    do, delta, dzg = _gate_bwd(dzl, s["w_out"], s["o"], s["proj"], H, "mla_gate_bwd")
    nb = H * MLA_NOPE // LANES
    dq_full, dk_full, dv, _, _, _ = _attn_bwd(s["q_full"], 0, s["k_full"], 0, s["kv"], nb, do, s["ab"],
                                              s["lse"], delta,
                                              H, MLA_DK, MLA_V, (MLA_NOPE + MLA_ROPE) ** -0.5, "mla_attn_bwd",
                                              s["dead"], bias_grads=False)
    cos, sa, sb = s["tabs"]
    dqf, dkn, dkr = _mla_rope_bwd(dq_full, dk_full, cos, -sa, -sb)
    d_uq = _matmul(s["cqn"], dqf, "tn", F32, "mla_duq")
    dcqn = _matmul(dqf, s["uq"], "nt", F32, "mla_dcqn")
    dkv = jnp.concatenate([dkn, dv.astype(_MXU)], axis=1)
    d_ukv = _matmul(s["ckvn"], dkv, "tn", F32, "mla_dukv")
    dckvn = _matmul(dkv, s["ukv"], "nt", F32, "mla_dckvn")
    dcq, dckv, dgq, dgkv = _mla_norm_bwd(s["proj"], s["gq"][None, :], s["gkv"][None, :], dcqn, dckvn)
    dproj = jnp.concatenate([dzg, dcq, dckv, dkr], axis=1)
    d_wzc = _matmul(s["h"], dproj, "tn", F32, "mla_dwin")
    dh = _matmul(dproj, s["w_zc"], "nt", F32, "mla_dh")
    W = D_MODEL
    d_win = jnp.concatenate([d_wzc[:, W:W + MLA_Q_LORA + MLA_KV_LORA + MLA_ROPE], d_wzc[:, :W]], axis=1)
    d_wuq = d_uq.reshape(MLA_Q_LORA, H, MLA_DK)[:, :, :MLA_NOPE + MLA_ROPE].reshape(MLA_Q_LORA, -1)
    hk = H * MLA_NOPE
    d_wukv = jnp.concatenate([d_ukv[:, :hk].reshape(MLA_KV_LORA, H, MLA_NOPE),
                              d_ukv[:, hk:].reshape(MLA_KV_LORA, H, MLA_V)], axis=2).reshape(MLA_KV_LORA, -1)
    return dh, (d_win, dgq[0], dgkv[0], d_wuq, d_wukv, d_wout)


def _ret_tables(L, pad):
    pos = (jnp.arange(L) - pad).astype(F32)
    inv = 1.0 / (ROPE_BASE ** jnp.linspace(0.0, 1.0, RET_QK // 2, dtype=F32))
    ang = pos[:, None] * inv[None, :]
    return jnp.cos(ang), jnp.sin(ang)


def _ret_layer_fwd(h, w, lnw, pad):
    w_in, gn_g, w_out = w
    L = h.shape[0]
    q1 = 2 * _RET_QKW
    v1 = q1 + RET_HEADS * RET_V
    w_qkz = jnp.concatenate([w_in[:, :q1], w_in[:, v1:]], axis=1)
    cos, sin = _ret_tables(L, pad)
    qkz = _matmul(h, w_qkz, "nn", F32, "ret_qkz")
    v = _matmul(h, w_in[:, q1:v1], "nn", _MXU, "ret_v")
    qr, kr = _ret_rope_fwd(qkz, cos, sin, pad)
    o, states = _ret_fwd(qr, kr, v)
    y = _ret_gate_fwd(o, qkz, gn_g[None, :])
    zl, hn = _out_ln(y, w_out, h, lnw[0], lnw[1], "ret_out_ln")
    return hn, dict(h=h, qkz=qkz, v=v, qr=qr, kr=kr, o=o, states=states, y=y, zl=zl, w_in=w_in, w_out=w_out,
                    gn_g=gn_g, tabs=(cos, sin))


def _ret_layer_bwd(s, dzl, pad):
    d_wout = _matmul(s["y"], dzl, "tn", F32, "ret_dwout")
    do, dzg, dgn = _ret_gate_bwd(dzl, s["w_out"], s["o"], s["qkz"], s["gn_g"][None, :])
    dqr, dkr, dv = _ret_bwd(s["qr"], s["kr"], s["v"], do, s["states"])
    cos, sin = s["tabs"]
    dq, dk = _ret_rope_bwd(dqr, dkr, cos, sin, pad)
    dproj = jnp.concatenate([dq, dk, dv, dzg], axis=1)
    d_win = _matmul(s["h"], dproj, "tn", F32, "ret_dwin")
    dh = _matmul(dproj, s["w_in"], "nt", F32, "ret_dh")
    return dh, (d_win, dgn[0], d_wout)


def _local_step(x, target, meta, fox0_w, late_shards, small, ln_g, ln_b):
    n_real = x.shape[0]
    first = -(-N_META // ROW_TILE)
    pad = first * ROW_TILE - N_META
    h0 = jnp.concatenate([jnp.zeros((pad, D_MODEL), F32), meta, x], axis=0)
    lnw = [(ln_g[i][None, :], ln_b[i][None, :]) for i in range(4)]
    b_f = small["fox_b_f"]

    h1, s0, gathered = _fox_fwd(h0, (fox0_w[0], b_f[0], fox0_w[1]), lnw[0], pad, "fox0", (late_shards, False))
    w = {key: _merge_layer(key[0], g).astype(_MXU) for key, g in zip(_LATE, gathered)}
    h2, s1 = _mla_fwd(h1, (w["mla_w_in", 0], small["mla_q_norm"][0], small["mla_kv_norm"][0], w["mla_w_uq", 0],
                           w["mla_w_ukv", 0], w["mla_w_out", 0]), lnw[1], pad)
    h3, s2 = _ret_layer_fwd(h2, (w["ret_w_in", 0], small["ret_gn_g"][0], w["ret_w_out", 0]), lnw[2], pad)
    h4, s3, _ = _fox_fwd(h3, (w["fox_w_in", 1], b_f[1], w["fox_w_out", 1]), lnw[3], pad, "fox1")

    dz3, dg3, db3, loss = _loss_ln_bwd(s3["zl"], lnw[3][0], lnw[3][1], target, first)
    dm3, gf1, _ = _fox_bwd(s3, dz3, pad, "fox1")
    dz2, dg2, db2 = _ln_bwd(s2["zl"], lnw[2][0], dz3, dm3, "ln2_bwd")
    dm2, gr = _ret_layer_bwd(s2, dz2, pad)
    dz1, dg1, db1 = _ln_bwd(s1["zl"], lnw[1][0], dz2, dm2, "ln1_bwd")
    dm1, gm = _mla_bwd(s1, dz1)
    dz0, dg0, db0 = _ln_bwd(s0["zl"], lnw[0][0], dz1, dm1, "ln0_bwd")
    late = {("fox_w_in", 1): gf1[0], ("fox_w_out", 1): gf1[2], ("mla_w_in", 0): gm[0], ("mla_w_uq", 0): gm[3],
            ("mla_w_ukv", 0): gm[4], ("mla_w_out", 0): gm[5], ("ret_w_in", 0): gr[0], ("ret_w_out", 0): gr[2]}
    dm0, gf0, sent = _fox_bwd(s0, dz0, pad, "fox0", ([_split_layer(k[0], late[k]) for k in _LATE], True), True)
    parts = dict(zip(_LATE + [("fox_w_out", 0), ("fox_w_in", 0)], sent))
    grad_x, grad_meta = _input_grads(dz0, dm0, first, n_real)

    small_grads = dict(
        meta=grad_meta, fox_b_f=jnp.stack([gf0[1], gf1[1]]), mla_q_norm=gm[1][None], mla_kv_norm=gm[2][None],
        ret_gn_g=gr[1][None], ln_g=jnp.concatenate([dg0, dg1, dg2, dg3], axis=0),
        ln_b=jnp.concatenate([db0, db1, db2, db3], axis=0))
    return loss, grad_x, parts, small_grads


def _exchange(tensors, scatter, name):
    nt = len(tensors)

    def body(*refs):
        copies = _exchange_copies(refs[:nt], refs[nt:2 * nt], *refs[2 * nt:], scatter)
        for cp in copies:
            cp.start()
        for cp in copies:
            cp.wait()

    hbm = pl.BlockSpec(memory_space=pl.ANY)
    out_shape, sems = _exchange_shapes(tensors, scatter)
    return pl.pallas_call(
        body, name=name, out_shape=out_shape, in_specs=[hbm] * nt, out_specs=(hbm,) * nt, scratch_shapes=sems,
    )(*tensors)


def _adamw(parts, w, m, v, name):
    R, C = w.shape
    tr = next(t for t in range(min(R, 256), 0, -1) if R % t == 0 and (t % 8 == 0 or t == R))

    def body(p_ref, w_ref, m_ref, v_ref, g_ref, d_ref, mo_ref, vo_ref):
        g = p_ref[0]
        for d in range(1, N_DEV):
            g = g + p_ref[d]
        mn = ADAM_B1 * m_ref[...] + (1.0 - ADAM_B1) * g
        vn = ADAM_B2 * v_ref[...] + (1.0 - ADAM_B2) * (g * g)
        m_hat = mn / (1.0 - ADAM_B1 ** ADAM_STEP)
        v_hat = vn / (1.0 - ADAM_B2 ** ADAM_STEP)
        g_ref[...] = g
        d_ref[...] = -ADAM_LR * (m_hat / (jnp.sqrt(v_hat) + ADAM_EPS) + ADAM_WD * w_ref[...])
        mo_ref[...] = mn
        vo_ref[...] = vn

    blk = pl.BlockSpec((tr, C), lambda i: (i, 0))
    sd = jax.ShapeDtypeStruct((R, C), F32)
    return pl.pallas_call(
        body, name=name,
        out_shape=(sd, sd, sd, sd),
        grid=(R // tr,),
        in_specs=[pl.BlockSpec((N_DEV, tr, C), lambda i: (0, i, 0)), blk, blk, blk],
        out_specs=(blk, blk, blk, blk),
        compiler_params=_params(("parallel",)),
    )(parts, w, m, v)


_SPECS = dict(
    meta=((16, 1024), 1),
    fox_w_in=((2, 1024, 4104), 2), fox_b_f=((2, 8), None), fox_w_out=((2, 1024, 1024), 1),
    mla_w_in=((1, 1024, 1728), 2), mla_q_norm=((1, 384), None), mla_kv_norm=((1, 256), None),
    mla_w_uq=((1, 384, 1536), 2), mla_w_ukv=((1, 256, 2048), 2), mla_w_out=((1, 1024, 1024), 1),
    ret_w_in=((1, 1024, 6144), 2), ret_gn_g=((1, 2048), 1), ret_w_out=((1, 2048, 1024), 1),
    ln_g=((4, 1024), None), ln_b=((4, 1024), None),
)
_NAMES = list(_SPECS)
_BIG = ["fox_w_in", "fox_w_out", "mla_w_in", "mla_w_uq", "mla_w_ukv", "mla_w_out", "ret_w_in", "ret_w_out"]
_SMALL = [n for n in _NAMES if n not in _BIG]
_SMALL_GATHERED = ["meta", "ret_gn_g"]
_PACK_ROWS = 8


def _shard_shape(name):
    shape, ax = _SPECS[name]
    if ax is None:
        return shape
    return tuple(s // N_DEV if i == ax else s for i, s in enumerate(shape))


_EARLY = [("fox_w_in", 0), ("fox_w_out", 0)]
_LATE = [("fox_w_in", 1), ("fox_w_out", 1), ("mla_w_in", 0), ("mla_w_uq", 0), ("mla_w_ukv", 0), ("mla_w_out", 0),
         ("ret_w_in", 0), ("ret_w_out", 0)]


def _split_at(full, ax):
    shape = full.shape
    parts = full.reshape(shape[:ax] + (N_DEV, shape[ax] // N_DEV) + shape[ax + 1:])
    return jnp.moveaxis(parts, ax, 0)


def _merge_at(stacked, ax):
    full = jnp.moveaxis(stacked, 0, ax)
    return full.reshape(full.shape[:ax] + (-1,) + full.shape[ax + 2:])


def _split(name, full):
    return _split_at(full, _SPECS[name][1])


def _merge(name, stacked):
    return _merge_at(stacked, _SPECS[name][1])


def _split_layer(name, full):
    return _split_at(full, _SPECS[name][1] - 1)


def _merge_layer(name, stacked):
    return _merge_at(stacked, _SPECS[name][1] - 1)


def _pad_rows(flat):
    n = flat.shape[-1]
    unit = _PACK_ROWS * LANES
    total = -(-n // unit) * unit
    flat = jnp.pad(flat, [(0, 0)] * (flat.ndim - 1) + [(0, total - n)])
    return flat.reshape(flat.shape[:-1] + (total // LANES, LANES))


def _pack_small(tree, names, tail):
    return _pad_rows(jnp.concatenate([tree[n].reshape(-1) for n in names] + [tail]))


def _unpack_small(buf, names):
    flat, out, at = buf.reshape(-1), {}, 0
    for n in names:
        shp = _shard_shape(n)
        out[n] = flat[at:at + math.prod(shp)].reshape(shp)
        at += math.prod(shp)
    return out, flat[at:]


def kernel(x, meta, fox_w_in, fox_b_f, fox_w_out, mla_w_in, mla_q_norm, mla_kv_norm, mla_w_uq, mla_w_ukv, mla_w_out, ret_w_in, ret_gn_g, ret_w_out, ln_g, ln_b, loss_target, m_meta, m_fox_w_in, m_fox_b_f, m_fox_w_out, m_mla_w_in, m_mla_q_norm, m_mla_kv_norm, m_mla_w_uq, m_mla_w_ukv, m_mla_w_out, m_ret_w_in, m_ret_gn_g, m_ret_w_out, m_ln_g, m_ln_b, v_meta, v_fox_w_in, v_fox_b_f, v_fox_w_out, v_mla_w_in, v_mla_q_norm, v_mla_kv_norm, v_mla_w_uq, v_mla_w_ukv, v_mla_w_out, v_ret_w_in, v_ret_gn_g, v_ret_w_out, v_ln_g, v_ln_b):
    w = dict(meta=meta, fox_w_in=fox_w_in, fox_b_f=fox_b_f, fox_w_out=fox_w_out, mla_w_in=mla_w_in,
             mla_q_norm=mla_q_norm, mla_kv_norm=mla_kv_norm, mla_w_uq=mla_w_uq, mla_w_ukv=mla_w_ukv,
             mla_w_out=mla_w_out, ret_w_in=ret_w_in, ret_gn_g=ret_gn_g, ret_w_out=ret_w_out, ln_g=ln_g, ln_b=ln_b)
    m = dict(meta=m_meta, fox_w_in=m_fox_w_in, fox_b_f=m_fox_b_f, fox_w_out=m_fox_w_out, mla_w_in=m_mla_w_in,
             mla_q_norm=m_mla_q_norm, mla_kv_norm=m_mla_kv_norm, mla_w_uq=m_mla_w_uq, mla_w_ukv=m_mla_w_ukv,
             mla_w_out=m_mla_w_out, ret_w_in=m_ret_w_in, ret_gn_g=m_ret_gn_g, ret_w_out=m_ret_w_out, ln_g=m_ln_g, ln_b=m_ln_b)
    v = dict(meta=v_meta, fox_w_in=v_fox_w_in, fox_b_f=v_fox_b_f, fox_w_out=v_fox_w_out, mla_w_in=v_mla_w_in,
             mla_q_norm=v_mla_q_norm, mla_kv_norm=v_mla_kv_norm, mla_w_uq=v_mla_w_uq, mla_w_ukv=v_mla_w_ukv,
             mla_w_out=v_mla_w_out, ret_w_in=v_ret_w_in, ret_gn_g=v_ret_gn_g, ret_w_out=v_ret_w_out, ln_g=v_ln_g, ln_b=v_ln_b)

    def shard16(key):
        return w[key[0]][key[1]].astype(jnp.bfloat16)

    none = jnp.zeros((0,), F32)
    gathered = _exchange([shard16(k) for k in _EARLY] + [_pack_small(w, _SMALL_GATHERED, none)],
                         False, "weights_all_gather")
    fox0_w = [_merge_layer(k[0], g).astype(_MXU) for k, g in zip(_EARLY, gathered)]
    small = {n: w[n] for n in _SMALL}
    pieces = [_unpack_small(gathered[-1][d], _SMALL_GATHERED)[0] for d in range(N_DEV)]
    for n in _SMALL_GATHERED:
        small[n] = _merge(n, jnp.stack([p[n] for p in pieces]))

    loss, grad_x, by_key, grads = _local_step(
        x[0], loss_target[0], small["meta"], fox0_w, [shard16(k) for k in _LATE], small, ln_g, ln_b)

    def small_rows(n):
        if _SPECS[n][1] is None:
            return jnp.broadcast_to(grads[n].reshape(1, -1), (N_DEV, grads[n].size))
        return _split(n, grads[n]).reshape(N_DEV, -1)

    small_out = _pad_rows(jnp.concatenate([small_rows(n) for n in _SMALL]
                                          + [jnp.broadcast_to(loss.reshape(1, 1), (N_DEV, 1))], axis=1))
    parts = _exchange([small_out], True, "grads_all_to_all")

    def rows2d(a):
        return a.reshape(-1, a.shape[-1])

    out = {}
    for n in _BIG:
        layers = [by_key[n, i] for i in range(_SPECS[n][0][0])]
        p = layers[0] if len(layers) == 1 else jnp.stack(layers, axis=1)
        res = _adamw(p.reshape(N_DEV, -1, p.shape[-1]), rows2d(w[n]), rows2d(m[n]), rows2d(v[n]), f"adamw_{n}")
        out[n] = [r.reshape(w[n].shape) for r in res]
    zero1 = jnp.zeros((1,), F32)
    res = _adamw(parts[-1], _pack_small(w, _SMALL, zero1), _pack_small(m, _SMALL, zero1),
                 _pack_small(v, _SMALL, zero1), "adamw_small")
    small_res = [_unpack_small(r, _SMALL) for r in res]
    for n in _SMALL:
        out[n] = [s[0][n] for s in small_res]
    total_loss = small_res[0][1][0]
    return (total_loss, grad_x[None], *[out[n][0] for n in _NAMES], *[out[n][1] for n in _NAMES],
            *[out[n][2] for n in _NAMES], *[out[n][3] for n in _NAMES])
```

```python
import math

import numpy as np
import jax
import jax.numpy as jnp
from jax import lax
from jax.experimental import pallas as pl
from jax.experimental.pallas import tpu as pltpu

F32 = jnp.float32
_MXU = jnp.bfloat16

N_DEV = 8
N_META = 16
D_MODEL = 1024
ROW_TILE = 512
LANES = 128
CHUNK = 128
NEG = -1e30
V7X_VMEM_BYTES = 64 * 1024 * 1024
VMEM_LIMIT = V7X_VMEM_BYTES - 4 * 1024 * 1024

FOX_HEADS = 8
MLA_HEADS = 8
MLA_NOPE, MLA_ROPE, MLA_V = 128, 64, 128
MLA_Q_LORA, MLA_KV_LORA = 384, 256
MLA_DK = 256
RET_HEADS, RET_QK, RET_V = 4, 256, 512
ROPE_BASE = 10000.0
ALPHA = (2 * 4) ** 0.25
NORM_EPS = 1e-5

ADAM_LR, ADAM_B1, ADAM_B2, ADAM_EPS, ADAM_WD, ADAM_STEP = 0.001, 0.9, 0.999, 1e-08, 0.01, 10


def _params(sem, vmem=VMEM_LIMIT):
    return pltpu.CompilerParams(dimension_semantics=sem, vmem_limit_bytes=vmem)


def _tile(n, pref):
    if n <= pref:
        return n
    t = (pref // LANES) * LANES
    while n % t:
        t -= LANES
    return t


_MM_TILES = {"nn": (512, 4224, 2048), "nt": (512, 2048, 6144), "tn": (1024, 2048, 1536)}


def _matmul(a, b, mode, out_dtype, name, side=None):
    if mode == "nn":
        (M, K), (K2, N) = a.shape, b.shape
    elif mode == "nt":
        (M, K), (N, K2) = a.shape, b.shape
    else:
        (K, M), (K2, N) = a.shape, b.shape
    assert K == K2, (a.shape, b.shape, mode)
    tm, tn, tk = (_tile(d, p) for d, p in zip((M, N, K), _MM_TILES[mode]))
    nk = K // tk
    if mode == "nn":
        a_spec = pl.BlockSpec((tm, tk), lambda i, j, k: (i, k))
        b_spec = pl.BlockSpec((tk, tn), lambda i, j, k: (k, j))
        dn = (((1,), (0,)), ((), ()))
    elif mode == "nt":
        a_spec = pl.BlockSpec((tm, tk), lambda i, j, k: (i, k))
        b_spec = pl.BlockSpec((tn, tk), lambda i, j, k: (j, k))
        dn = (((1,), (1,)), ((), ()))
    else:
        a_spec = pl.BlockSpec((tk, tm), lambda i, j, k: (k, i))
        b_spec = pl.BlockSpec((tk, tn), lambda i, j, k: (k, j))
        dn = (((0,), (0,)), ((), ()))

    ns = len(side[0]) if side else 0
    grid = (M // tm, N // tn, nk)
    n_acc = 1 if nk > 1 else 0

    def body(*refs):
        a_ref, b_ref = refs[:2]
        o_ref = refs[2 + ns]
        k = pl.program_id(2)
        if side:
            i, j = pl.program_id(0), pl.program_id(1)
            _side_exchange(side, refs[2:2 + ns], refs[3 + ns:3 + 2 * ns], refs[3 + 2 * ns + n_acc:],
                           (i == 0) & (j == 0) & (k == 0),
                           (i == grid[0] - 1) & (j == grid[1] - 1) & (k == nk - 1))
        part = lax.dot_general(a_ref[...].astype(_MXU), b_ref[...].astype(_MXU), dn,
                               preferred_element_type=F32)
        if nk == 1:
            o_ref[...] = part.astype(out_dtype)
            return
        acc_ref = refs[3 + 2 * ns]

        @pl.when(k == 0)
        def _():
            acc_ref[...] = part

        @pl.when(k > 0)
        def _():
            acc_ref[...] += part

        @pl.when(k == nk - 1)
        def _():
            o_ref[...] = acc_ref[...].astype(out_dtype)

    hbm = pl.BlockSpec(memory_space=pl.ANY)
    side_shapes, side_sems = _exchange_shapes(*side) if side else ((), [])
    out, *side_out = pl.pallas_call(
        body, name=name,
        out_shape=(jax.ShapeDtypeStruct((M, N), out_dtype),) + side_shapes,
        grid=grid,
        in_specs=[a_spec, b_spec] + [hbm] * ns,
        out_specs=(pl.BlockSpec((tm, tn), lambda i, j, k: (i, j)),) + (hbm,) * ns,
        scratch_shapes=[pltpu.VMEM((tm, tn), F32)] * n_acc + side_sems,
        compiler_params=_params(("arbitrary",) * 3 if side else ("parallel", "parallel", "arbitrary")),
    )(a, b, *(side[0] if side else ()))
    return (out, side_out) if side else out


def _rows(w, off=0):
    return pl.BlockSpec((ROW_TILE, w), lambda i: (i + off, 0))


def _whole(shape):
    return pl.BlockSpec(shape, lambda i: (0,) * len(shape))


def _silu(z):
    return z * jax.nn.sigmoid(z)


def _dsilu(z):
    s = jax.nn.sigmoid(z)
    return s * (1.0 + z * (1.0 - s))


def _out_ln(y, w_out, h, g, b, name, last=False):
    L, D = h.shape
    K = y.shape[1]

    def body(y_ref, w_ref, h_ref, g_ref, b_ref, z_ref, *o_ref):
        mo = jnp.dot(y_ref[...].astype(_MXU), w_ref[...].astype(_MXU), preferred_element_type=F32)
        z = ALPHA * h_ref[...] + mo
        z_ref[...] = z
        if not last:
            mu = jnp.mean(z, axis=-1, keepdims=True)
            zc = z - mu
            var = jnp.mean(zc * zc, axis=-1, keepdims=True)
            o_ref[0][...] = zc * lax.rsqrt(var + NORM_EPS) * g_ref[...] + b_ref[...]

    n_out = 1 if last else 2
    res = pl.pallas_call(
        body, name=name,
        out_shape=(jax.ShapeDtypeStruct((L, D), F32),) * n_out,
        grid=(L // ROW_TILE,),
        in_specs=[_rows(K), _whole((K, D)), _rows(D), _whole((1, D)), _whole((1, D))],
        out_specs=(_rows(D),) * n_out,
        compiler_params=_params(("parallel",)),
    )(y, w_out, h, g, b)
    return (res[0], None) if last else res


def _ln_bwd(zl, g, ga, gb, name):
    L, D = zl.shape
    two = gb is not None

    def body(*refs):
        if two:
            z_ref, g_ref, ga_ref, gb_ref, dz_ref, dg_ref, db_ref = refs
            gout = ALPHA * ga_ref[...] + gb_ref[...]
        else:
            z_ref, g_ref, ga_ref, dz_ref, dg_ref, db_ref = refs
            gout = ga_ref[...]
        z = z_ref[...]
        mu = jnp.mean(z, axis=-1, keepdims=True)
        zc = z - mu
        var = jnp.mean(zc * zc, axis=-1, keepdims=True)
        rstd = lax.rsqrt(var + NORM_EPS)
        xhat = zc * rstd
        dxh = gout * g_ref[...]
        m1 = jnp.mean(dxh, axis=-1, keepdims=True)
        m2 = jnp.mean(dxh * xhat, axis=-1, keepdims=True)
        dz_ref[...] = rstd * (dxh - m1 - xhat * m2)

        @pl.when(pl.program_id(0) == 0)
        def _():
            dg_ref[...] = jnp.zeros_like(dg_ref)
            db_ref[...] = jnp.zeros_like(db_ref)

        dg_ref[...] += jnp.sum(gout * xhat, axis=0, keepdims=True)
        db_ref[...] += jnp.sum(gout, axis=0, keepdims=True)

    ins = [zl, g, ga] + ([gb] if two else [])
    return pl.pallas_call(
        body, name=name,
        out_shape=(jax.ShapeDtypeStruct((L, D), F32), jax.ShapeDtypeStruct((1, D), F32),
                   jax.ShapeDtypeStruct((1, D), F32)),
        grid=(L // ROW_TILE,),
        in_specs=[_rows(D), _whole((1, D)), _rows(D)] + ([_rows(D)] if two else []),
        out_specs=(_rows(D), _whole((1, D)), _whole((1, D))),
        compiler_params=_params(("arbitrary",)),
    )(*ins)


def _loss_ln_bwd(zl, g, b, target, first):
    L, D = zl.shape

    def body(z_ref, g_ref, b_ref, t_ref, dz_ref, dg_ref, db_ref, loss_ref):
        i = pl.program_id(0)

        @pl.when(i == 0)
        def _():
            dg_ref[...] = jnp.zeros_like(dg_ref)
            db_ref[...] = jnp.zeros_like(db_ref)
            loss_ref[...] = jnp.zeros_like(loss_ref)

        @pl.when(i < first)
        def _():
            dz_ref[...] = jnp.zeros_like(dz_ref)

        @pl.when(i >= first)
        def _():
            z = z_ref[...]
            mu = jnp.mean(z, axis=-1, keepdims=True)
            zc = z - mu
            var = jnp.mean(zc * zc, axis=-1, keepdims=True)
            rstd = lax.rsqrt(var + NORM_EPS)
            xhat = zc * rstd
            err = (xhat * g_ref[...] + b_ref[...]) - t_ref[...]
            gout = err * (1.0 / D)
            part = jnp.sum(jnp.sum(err * err, axis=-1, keepdims=True) * (1.0 / D), axis=0, keepdims=True)
            loss_ref[...] += 0.5 * part
            dxh = gout * g_ref[...]
            m1 = jnp.mean(dxh, axis=-1, keepdims=True)
            m2 = jnp.mean(dxh * xhat, axis=-1, keepdims=True)
            dz_ref[...] = rstd * (dxh - m1 - xhat * m2)
            dg_ref[...] += jnp.sum(gout * xhat, axis=0, keepdims=True)
            db_ref[...] += jnp.sum(gout, axis=0, keepdims=True)

    return pl.pallas_call(
        body, name="loss_ln_bwd",
        out_shape=(jax.ShapeDtypeStruct((L, D), F32), jax.ShapeDtypeStruct((1, D), F32),
                   jax.ShapeDtypeStruct((1, D), F32), jax.ShapeDtypeStruct((1, 1), F32)),
        grid=(L // ROW_TILE,),
        in_specs=[_rows(D), _whole((1, D)), _whole((1, D)),
                  pl.BlockSpec((ROW_TILE, D), lambda i: (jnp.maximum(i - first, 0), 0))],
        out_specs=(_rows(D), _whole((1, D)), _whole((1, D)), _whole((1, 1))),
        compiler_params=_params(("arbitrary",)),
    )(zl, g, b, target)


def _input_grads(dzl, dmix, first, n_real):
    L, D = dzl.shape

    def body(a_ref, b_ref, gx_ref, gm_ref):
        i = pl.program_id(0)
        val = ALPHA * a_ref[...] + b_ref[...]

        @pl.when(i == first - 1)
        def _():
            gm_ref[...] = val[ROW_TILE - N_META:, :]

        @pl.when(i >= first)
        def _():
            gx_ref[...] = val

    return pl.pallas_call(
        body, name="input_grads",
        out_shape=(jax.ShapeDtypeStruct((n_real, D), F32), jax.ShapeDtypeStruct((N_META, D), F32)),
        grid=(L // ROW_TILE,),
        in_specs=[_rows(D), _rows(D)],
        out_specs=(pl.BlockSpec((ROW_TILE, D), lambda i: (jnp.maximum(i - first, 0), 0)),
                   _whole((N_META, D))),
        compiler_params=_params(("arbitrary",)),
    )(dzl, dmix)


def _gate_bwd(dzl, w_out, o, zsrc, heads, name):
    L, W = o.shape
    D = dzl.shape[1]
    hd = W // heads

    def body(g_ref, w_ref, o_ref, z_ref, do_ref, dl_ref, dz_ref):
        z = z_ref[...]
        dy = lax.dot_general(g_ref[...].astype(_MXU), w_ref[...].astype(_MXU), (((1,), (1,)), ((), ())),
                             preferred_element_type=F32)
        o = o_ref[...]
        do = dy * _silu(z)
        do_ref[...] = do.astype(_MXU)
        dz_ref[...] = (dy * o * _dsilu(z)).astype(_MXU)
        prod = do * o
        for h in range(heads):
            dl_ref[h] = _as_row(jnp.sum(prod[:, h * hd:(h + 1) * hd], axis=-1, keepdims=True))

    return pl.pallas_call(
        body, name=name,
        out_shape=(jax.ShapeDtypeStruct((L, W), _MXU), jax.ShapeDtypeStruct((heads, L // ROW_TILE, 1, ROW_TILE), F32),
                   jax.ShapeDtypeStruct((L, W), _MXU)),
        grid=(L // ROW_TILE,),
        in_specs=[_rows(D), _whole((W, D)), _rows(W), _rows(W)],
        out_specs=(_rows(W), pl.BlockSpec((heads, None, 1, ROW_TILE), lambda i: (0, i, 0, 0)), _rows(W)),
        compiler_params=_params(("parallel",)),
    )(dzl, w_out, o, zsrc)


def _split3(x):
    hi = x.astype(_MXU)
    r1 = x - hi.astype(F32)
    mid = r1.astype(_MXU)
    lo = (r1 - mid.astype(F32)).astype(_MXU)
    return hi, mid, lo


def _tri_cumsum(x, reverse):
    T = x.shape[0]
    r = lax.broadcasted_iota(jnp.int32, (T, T), 0)
    c = lax.broadcasted_iota(jnp.int32, (T, T), 1)
    tri = jnp.where((c >= r) if reverse else (c <= r), 1.0, 0.0).astype(_MXU)
    out = jnp.zeros(x.shape, F32)
    for part in _split3(x):
        out = out + jnp.dot(tri, part, preferred_element_type=F32)
    return out


def _fox_cum(zf, bias, pad, name):
    L, Wz = zf.shape
    off = Wz // LANES - 1

    def body(f_ref, b_ref, c_ref, carry_ref):
        i = pl.program_id(0)

        @pl.when(i == 0)
        def _():
            carry_ref[...] = jnp.zeros_like(carry_ref)

        x = f_ref[...] + b_ref[...]
        logf = jnp.minimum(x, 0.0) - jnp.log(1.0 + jnp.exp(-jnp.abs(x)))
        row = i * ROW_TILE + lax.broadcasted_iota(jnp.int32, logf.shape, 0)
        logf = jnp.where(row >= pad, logf, 0.0)
        cum = _tri_cumsum(logf, False) + carry_ref[...]
        c_ref[...] = cum
        carry_ref[...] = cum[ROW_TILE - 1:, :]

    return pl.pallas_call(
        body, name=name,
        out_shape=jax.ShapeDtypeStruct((L, LANES), F32),
        grid=(L // ROW_TILE,),
        in_specs=[pl.BlockSpec((ROW_TILE, LANES), lambda i: (i, off)), _whole((1, LANES))],
        out_specs=_rows(LANES),
        scratch_shapes=[pltpu.VMEM((1, LANES), F32)],
        compiler_params=_params(("arbitrary",)),
    )(zf, bias)


def _fox_cum_bwd(dcum, zf, bias, pad, name):
    L, Wz = zf.shape
    off = Wz // LANES - 1
    n = L // ROW_TILE

    def body(d_ref, f_ref, b_ref, df_ref, db_ref, carry_ref):
        i = pl.program_id(0)

        @pl.when(i == 0)
        def _():
            carry_ref[...] = jnp.zeros_like(carry_ref)
            db_ref[...] = jnp.zeros_like(db_ref)

        rc = _tri_cumsum(d_ref[...], True) + carry_ref[...]
        carry_ref[...] = rc[:1, :]
        x = f_ref[...] + b_ref[...]
        row = (n - 1 - i) * ROW_TILE + lax.broadcasted_iota(jnp.int32, x.shape, 0)
        df = jnp.where(row >= pad, rc * jax.nn.sigmoid(-x), 0.0)
        df_ref[...] = df.astype(_MXU)
        db_ref[...] += jnp.sum(df, axis=0, keepdims=True)

    return pl.pallas_call(
        body, name=name,
        out_shape=(jax.ShapeDtypeStruct((L, LANES), _MXU), jax.ShapeDtypeStruct((1, LANES), F32)),
        grid=(n,),
        in_specs=[pl.BlockSpec((ROW_TILE, LANES), lambda i: (n - 1 - i, 0)),
                  pl.BlockSpec((ROW_TILE, LANES), lambda i: (n - 1 - i, off)), _whole((1, LANES))],
        out_specs=(pl.BlockSpec((ROW_TILE, LANES), lambda i: (n - 1 - i, 0)), _whole((1, LANES))),
        scratch_shapes=[pltpu.VMEM((1, LANES), F32)],
        compiler_params=_params(("arbitrary",)),
    )(dcum, zf, bias)


LOG2E = 1.4426950408889634
UNROLL_SIZES = (8, 4, 2)


def _bias_terms(ct, pad):
    H, L = ct.shape
    T = ROW_TILE
    n = L // T
    c2 = ct * LOG2E
    k2 = jnp.where(jnp.arange(L)[None, :] < pad, -NEG, c2)
    return dict(cq_row=c2.reshape(H, n, 1, T), ck_row=k2.reshape(H, n, 1, T), ref=c2[:, ::T].reshape(H, n, 1, 1))


def _exchange_copies(srcs, outs, send_sems, recv_sems, local_sems, scatter):
    nt = len(srcs)
    x, y, c = lax.axis_index("x"), lax.axis_index("y"), lax.axis_index("c")
    me = 4 * x + 2 * y + c
    copies = [pltpu.make_async_copy(srcs[t].at[me] if scatter else srcs[t], outs[t].at[me], local_sems.at[t])
              for t in range(nt)]
    for k in range(1, N_DEV):
        px = (x + (k >> 2)) % 2
        py = (y + ((k >> 1) & 1)) % 2
        pc = (c + (k & 1)) % 2
        peer = 4 * px + 2 * py + pc
        for t in range(nt):
            copies.append(pltpu.make_async_remote_copy(
                src_ref=srcs[t].at[peer] if scatter else srcs[t],
                dst_ref=outs[t].at[me],
                send_sem=send_sems.at[k - 1, t], recv_sem=recv_sems.at[k - 1, t],
                device_id=(px, py, pc), device_id_type=pl.DeviceIdType.MESH))
    return copies


def _exchange_shapes(tensors, scatter):
    out_shape = tuple(jax.ShapeDtypeStruct(t.shape if scatter else (N_DEV,) + t.shape, t.dtype) for t in tensors)
    nt = len(tensors)
    sems = [pltpu.SemaphoreType.DMA((N_DEV - 1, nt)), pltpu.SemaphoreType.DMA((N_DEV - 1, nt)),
            pltpu.SemaphoreType.DMA((nt,))]
    return out_shape, sems


def _side_exchange(side, srcs, outs, sems, first, last):
    if not side:
        return

    @pl.when(first)
    def _():
        for cp in _exchange_copies(srcs, outs, *sems, side[1]):
            cp.start()

    @pl.when(last)
    def _():
        for cp in _exchange_copies(srcs, outs, *sems, side[1]):
            cp.wait()


def _as_row(col):
    return jnp.transpose(jnp.broadcast_to(col, (col.shape[0], LANES)))[0:1, :]


def _as_col(row):
    return jnp.transpose(jnp.broadcast_to(row, (LANES, row.shape[1])))[:, 0:1]


def _attn_fwd(q, qoff, k, koff, v, voff, zsrc, bias, heads, dk, dv, scale, name, dead, side=None):
    L = q.shape[0]
    T = ROW_TILE
    n = L // T
    c = scale * LOG2E
    dn_qk = (((1,), (1,)), ((), ()))

    ns = len(side[0]) if side else 0

    def body(*refs):
        q_ref, k_ref, v_ref, z_ref, ck_ref, ref_ref = refs[:6]
        o_ref, lse_ref, y_ref = refs[6 + ns:9 + ns]
        s_a, s_b = refs[9 + 2 * ns:11 + 2 * ns]
        i = pl.program_id(1)
        _side_exchange(side, refs[6:6 + ns], refs[9 + ns:9 + 2 * ns], refs[11 + 2 * ns:],
                       (pl.program_id(0) == 0) & (i == 0), (pl.program_id(0) == heads - 1) & (i == n - 1))
        qb = q_ref[...]
        ref = ref_ref[...]

        def scores(j, dst):
            start = pl.multiple_of(j * T, T)
            dst[...] = lax.dot_general(qb, k_ref[pl.ds(start, T), :], dn_qk, preferred_element_type=F32)

        def update(t, vj, carry):
            m, l, acc = carry
            m_new = jnp.maximum(m, jnp.max(t, axis=1, keepdims=True))
            p = jnp.exp2(t - m_new)
            a = jnp.exp2(m - m_new)
            l = a * l + jnp.sum(p, axis=1, keepdims=True)
            acc = a * acc + jnp.dot(p.astype(_MXU), vj, preferred_element_type=F32)
            return m_new, l, acc

        def soft(j, carry, cur, diag):
            start = pl.multiple_of(j * T, T)
            t = cur[...] * c - (ck_ref[j] - ref)
            if diag:
                r = lax.broadcasted_iota(jnp.int32, (T, T), 0)
                cc = lax.broadcasted_iota(jnp.int32, (T, T), 1)
                t = jnp.where(cc <= r, t, NEG)
            return update(t, v_ref[pl.ds(start, T), :], carry)

        def head_tile(carry):
            t = lax.dot_general(qb, k_ref[dead:T, :], dn_qk, preferred_element_type=F32) * c
            return update(t - (ck_ref[0][:, dead:] - ref), v_ref[dead:T, :], carry)

        def step(j, carry, cur, nxt):
            scores(j + 1, nxt)
            return soft(j, carry, cur, False)

        first = jnp.where(i > 0, 1, 0) if dead else 0

        def run_of(j0, steps, carry):
            for u in range(0, steps, 2):
                carry = step(j0 + u + 1, step(j0 + u, carry, s_a, s_b), s_b, s_a)
            return carry

        scores(first, s_a)
        carry = (jnp.full((T, 1), -jnp.inf, F32), jnp.zeros((T, 1), F32), jnp.zeros((T, dv), F32))
        if dead:
            carry = lax.cond(i > 0, head_tile, lambda cy: cy, carry)
        done = first
        for size in UNROLL_SIZES:
            trips = (i - done) // size
            carry = lax.fori_loop(0, trips, lambda g, cy, d=done, s=size: run_of(d + s * g, s, cy), carry)
            done = done + size * trips
        odd = (i - done) == 1
        carry = lax.cond(odd, lambda cy: step(i - 1, cy, s_a, s_b), lambda cy: cy, carry)
        m, l, acc = lax.cond(odd, lambda cy: soft(i, cy, s_b, True), lambda cy: soft(i, cy, s_a, True), carry)
        o = acc / l
        o_ref[...] = o
        y_ref[...] = (o * _silu(z_ref[...])).astype(_MXU)
        lse_ref[...] = _as_row(m + jnp.log2(l))

    hbm = pl.BlockSpec(memory_space=pl.ANY)
    side_shapes, side_sems = _exchange_shapes(*side) if side else ((), [])
    o, lse, y, *side_out = pl.pallas_call(
        body, name=name,
        out_shape=(jax.ShapeDtypeStruct((L, heads * dv), F32), jax.ShapeDtypeStruct((heads, n, 1, T), F32),
                   jax.ShapeDtypeStruct((L, heads * dv), _MXU)) + side_shapes,
        grid=(heads, n),
        in_specs=[pl.BlockSpec((T, dk), lambda h, i: (i, qoff + h)),
                  pl.BlockSpec((L, dk), lambda h, i: (0, koff + h)),
                  pl.BlockSpec((L, dv), lambda h, i: (0, voff + h)),
                  pl.BlockSpec((T, dv), lambda h, i: (i, h)),
                  pl.BlockSpec((None, n, 1, T), lambda h, i: (h, 0, 0, 0)),
                  pl.BlockSpec((None, None, 1, 1), lambda h, i: (h, i, 0, 0))] + [hbm] * ns,
        out_specs=(pl.BlockSpec((T, dv), lambda h, i: (i, h)),
                   pl.BlockSpec((None, None, 1, T), lambda h, i: (h, i, 0, 0)),
                   pl.BlockSpec((T, dv), lambda h, i: (i, h))) + (hbm,) * ns,
        scratch_shapes=[pltpu.VMEM((T, T), F32), pltpu.VMEM((T, T), F32)] + side_sems,
        compiler_params=_params(("arbitrary", "arbitrary")),
    )(q, k, v, zsrc, bias["ck_row"], bias["ref"], *(side[0] if side else ()))
    return o, lse + (bias["cq_row"] - bias["ref"]), y, side_out


def _attn_bwd(q, qoff, k, koff, v, voff, do, bias, lse_row, delta_row, heads, dk, dv, scale, name, dead,
              side=None, bias_grads=True):
    L = q.shape[0]
    T = ROW_TILE
    n = L // T
    c = scale * LOG2E
    dn_nt = (((1,), (1,)), ((), ()))
    dn_tn = (((0,), (0,)), ((), ()))
    ns = len(side[0]) if side else 0

    def body(*refs):
        q_ref, k_ref, v_ref, do_ref, cq_ref, ck_ref, ref_ref, lse_ref, dl_ref = refs[:9]
        dq_hbm, dk_ref, dv_ref, dck_ref, dcq_ref = refs[9 + ns:14 + ns]
        dq_acc, sem, st_a, dp_a, st_b, dp_b = refs[14 + 2 * ns:20 + 2 * ns]
        h = pl.program_id(0)
        j = pl.program_id(1)
        _side_exchange(side, refs[9:9 + ns], refs[14 + ns:14 + 2 * ns], refs[20 + 2 * ns:],
                       (h == 0) & (j == 0), (h == heads - 1) & (j == n - 1))

        @pl.when(j == 0)
        def _():
            dq_acc[...] = jnp.zeros_like(dq_acc)
            dcq_ref[...] = jnp.zeros_like(dcq_ref)

        ref = ref_ref[...]

        def run(k0, R):
            kb = k_ref[k0:k0 + R, :]
            vb = v_ref[k0:k0 + R, :]
            bcol = _as_col(ck_ref[j][:, k0:k0 + R] - ref)

            def front(i, st_dst, dp_dst):
                start = pl.multiple_of(jnp.minimum(i, n - 1) * T, T)
                st_dst[0:R, :] = lax.dot_general(kb, q_ref[pl.ds(start, T), :], dn_nt, preferred_element_type=F32)
                dp_dst[0:R, :] = lax.dot_general(vb, do_ref[pl.ds(start, T), :], dn_nt, preferred_element_type=F32)

            def back(i, carry, st_cur, dp_cur, diag):
                dk_a, dv_a, dck_a = carry
                start = pl.multiple_of(i * T, T)
                qi = q_ref[pl.ds(start, T), :]
                doi = do_ref[pl.ds(start, T), :]
                arow = (cq_ref[i] - ref) - lse_ref[i]
                st = st_cur[0:R, :] * c + arow - bcol
                if diag:
                    r = lax.broadcasted_iota(jnp.int32, (R, T), 0) + k0
                    cc = lax.broadcasted_iota(jnp.int32, (R, T), 1)
                    st = jnp.where(r <= cc, st, NEG)
                pt = jnp.exp2(st)
                dv_a = dv_a + jnp.dot(pt.astype(_MXU), doi, preferred_element_type=F32)
                dst = pt * (dp_cur[0:R, :] - dl_ref[i])
                if bias_grads:
                    dck_a = dck_a - jnp.sum(dst, axis=1, keepdims=True)
                    dcq_ref[i] += jnp.sum(dst, axis=0, keepdims=True)
                dsb = (dst * scale).astype(_MXU)
                dk_a = dk_a + jnp.dot(dsb, qi, preferred_element_type=F32)
                dq_acc[pl.ds(start, T), :] += lax.dot_general(dsb, kb, dn_tn, preferred_element_type=F32)
                return dk_a, dv_a, dck_a

            buf_a, buf_b = (st_a, dp_a), (st_b, dp_b)

            def step(i, carry, cur, nxt, diag=False):
                front(i + 1, *nxt)
                return back(i, carry, *cur, diag)

            def run_of(i0, steps, carry):
                for u in range(0, steps, 2):
                    carry = step(i0 + u + 1, step(i0 + u, carry, buf_b, buf_a), buf_a, buf_b)
                return carry

            front(j, *buf_a)
            init = (jnp.zeros((R, dk), F32), jnp.zeros((R, dv), F32), jnp.zeros((R, 1), F32))
            carry = step(j, init, buf_a, buf_b, True)
            done = j + 1
            for size in UNROLL_SIZES:
                trips = (n - done) // size
                carry = lax.fori_loop(0, trips, lambda g, cy, d=done, s=size: run_of(d + s * g, s, cy), carry)
                done = done + size * trips
            dk_a, dv_a, dck_a = lax.cond((n - done) == 1, lambda cy: step(n - 1, cy, buf_b, buf_a),
                                         lambda cy: cy, carry)
            if k0:
                dk_ref[0:k0, :] = jnp.zeros((k0, dk), F32)
                dv_ref[0:k0, :] = jnp.zeros((k0, dv), F32)
                dck_ref[:, 0:k0] = jnp.zeros((1, k0), F32)
            dk_ref[k0:k0 + R, :] = dk_a
            dv_ref[k0:k0 + R, :] = dv_a
            dck_ref[:, k0:k0 + R] = _as_row(dck_a)

        if dead:
            pl.when(j == 0)(lambda: run(dead, T - dead))
            pl.when(j > 0)(lambda: run(0, T))
        else:
            run(0, T)

        @pl.when(j == n - 1)
        def _():
            cp = pltpu.make_async_copy(dq_acc, dq_hbm.at[:, pl.ds(pl.multiple_of(h * dk, dk), dk)], sem)
            cp.start()
            cp.wait()

    hbm = pl.BlockSpec(memory_space=pl.ANY)
    side_shapes, side_sems = _exchange_shapes(*side) if side else ((), [])
    dq, dk_, dv_, dck, dcq, *side_out = pl.pallas_call(
        body, name=name,
        out_shape=(jax.ShapeDtypeStruct((L, heads * dk), F32), jax.ShapeDtypeStruct((L, heads * dk), F32),
                   jax.ShapeDtypeStruct((L, heads * dv), F32), jax.ShapeDtypeStruct((heads, n, 1, T), F32),
                   jax.ShapeDtypeStruct((heads, n, 1, T), F32)) + side_shapes,
        grid=(heads, n),
        in_specs=[pl.BlockSpec((L, dk), lambda h, j: (0, qoff + h)),
                  pl.BlockSpec((T, dk), lambda h, j: (j, koff + h)),
                  pl.BlockSpec((T, dv), lambda h, j: (j, voff + h)),
                  pl.BlockSpec((L, dv), lambda h, j: (0, h)),
                  pl.BlockSpec((None, n, 1, T), lambda h, j: (h, 0, 0, 0)),
                  pl.BlockSpec((None, n, 1, T), lambda h, j: (h, 0, 0, 0)),
                  pl.BlockSpec((None, None, 1, 1), lambda h, j: (h, j, 0, 0)),
                  pl.BlockSpec((None, n, 1, T), lambda h, j: (h, 0, 0, 0)),
                  pl.BlockSpec((None, n, 1, T), lambda h, j: (h, 0, 0, 0))] + [hbm] * ns,
        out_specs=(hbm,
                   pl.BlockSpec((T, dk), lambda h, j: (j, h)),
                   pl.BlockSpec((T, dv), lambda h, j: (j, h)),
                   pl.BlockSpec((None, None, 1, T), lambda h, j: (h, j, 0, 0)),
                   pl.BlockSpec((None, n, 1, T), lambda h, j: (h, 0, 0, 0))) + (hbm,) * ns,
        scratch_shapes=[pltpu.VMEM((L, dk), F32), pltpu.SemaphoreType.DMA] + [pltpu.VMEM((T, T), F32)] * 4
        + side_sems,
        compiler_params=_params(("arbitrary", "arbitrary")),
    )(q, k, v, do, bias["cq_row"], bias["ck_row"], bias["ref"], lse_row, delta_row, *(side[0] if side else ()))
    return dq, dk_, dv_, dck, dcq, side_out


_CQ0, _CKV0, _KR0 = D_MODEL, D_MODEL + MLA_Q_LORA, D_MODEL + MLA_Q_LORA + MLA_KV_LORA
_MLA_PROJ = _KR0 + LANES


def _rms(x, g):
    ms = jnp.mean(x * x, axis=-1, keepdims=True)
    return x * lax.rsqrt(ms + NORM_EPS) * g


def _rms_bwd(x, g, dy):
    ms = jnp.mean(x * x, axis=-1, keepdims=True)
    r = lax.rsqrt(ms + NORM_EPS)
    xh = x * r
    dxh = dy * g
    dx = r * (dxh - xh * jnp.mean(dxh * xh, axis=-1, keepdims=True))
    return dx, jnp.sum(dy * xh, axis=0, keepdims=True)


def _mla_norm_fwd(proj, gq, gkv):
    L = proj.shape[0]

    def body(p_ref, gq_ref, gkv_ref, cq_ref, ckv_ref):
        cq_ref[...] = _rms(p_ref[:, _CQ0:_CKV0], gq_ref[...]).astype(_MXU)
        ckv_ref[...] = _rms(p_ref[:, _CKV0:_KR0], gkv_ref[...]).astype(_MXU)

    return pl.pallas_call(
        body, name="mla_norm_fwd",
        out_shape=(jax.ShapeDtypeStruct((L, MLA_Q_LORA), _MXU), jax.ShapeDtypeStruct((L, MLA_KV_LORA), _MXU)),
        grid=(L // ROW_TILE,),
        in_specs=[_rows(_MLA_PROJ), _whole((1, MLA_Q_LORA)), _whole((1, MLA_KV_LORA))],
        out_specs=(_rows(MLA_Q_LORA), _rows(MLA_KV_LORA)),
        compiler_params=_params(("parallel",)),
    )(proj, gq, gkv)


def _mla_norm_bwd(proj, gq, gkv, dcqn, dckvn):
    L = proj.shape[0]

    def body(p_ref, gq_ref, gkv_ref, dq_ref, dkv_ref, dcq_ref, dckv_ref, dgq_ref, dgkv_ref):
        @pl.when(pl.program_id(0) == 0)
        def _():
            dgq_ref[...] = jnp.zeros_like(dgq_ref)
            dgkv_ref[...] = jnp.zeros_like(dgkv_ref)

        dx, dg = _rms_bwd(p_ref[:, _CQ0:_CKV0], gq_ref[...], dq_ref[...])
        dcq_ref[...] = dx.astype(_MXU)
        dgq_ref[...] += dg
        dx, dg = _rms_bwd(p_ref[:, _CKV0:_KR0], gkv_ref[...], dkv_ref[...])
        dckv_ref[...] = dx.astype(_MXU)
        dgkv_ref[...] += dg

    return pl.pallas_call(
        body, name="mla_norm_bwd",
        out_shape=(jax.ShapeDtypeStruct((L, MLA_Q_LORA), _MXU), jax.ShapeDtypeStruct((L, MLA_KV_LORA), _MXU),
                   jax.ShapeDtypeStruct((1, MLA_Q_LORA), F32), jax.ShapeDtypeStruct((1, MLA_KV_LORA), F32)),
        grid=(L // ROW_TILE,),
        in_specs=[_rows(_MLA_PROJ), _whole((1, MLA_Q_LORA)), _whole((1, MLA_KV_LORA)),
                  _rows(MLA_Q_LORA), _rows(MLA_KV_LORA)],
        out_specs=(_rows(MLA_Q_LORA), _rows(MLA_KV_LORA), _whole((1, MLA_Q_LORA)), _whole((1, MLA_KV_LORA))),
        compiler_params=_params(("arbitrary",)),
    )(proj, gq, gkv, dcqn, dckvn)


def _rot_tile(t, cos, sa, sb):
    half = MLA_ROPE // 2
    return t * cos + pltpu.roll(t, LANES - half, 1) * sa + pltpu.roll(t, half, 1) * sb


def _mla_rope_fwd(qf, kv, proj, cos, sa, sb):
    L = qf.shape[0]
    H = MLA_HEADS

    def body(q_ref, kv_ref, p_ref, c_ref, sa_ref, sb_ref, qo_ref, ko_ref):
        cos, sa, sb = c_ref[...], sa_ref[...], sb_ref[...]
        kr = _rot_tile(p_ref[...], cos, sa, sb).astype(_MXU)
        for h in range(H):
            b = h * MLA_DK
            qo_ref[:, b:b + LANES] = q_ref[:, b:b + LANES].astype(_MXU)
            qo_ref[:, b + LANES:b + 2 * LANES] = _rot_tile(q_ref[:, b + LANES:b + 2 * LANES], cos, sa, sb).astype(_MXU)
            ko_ref[:, b:b + LANES] = kv_ref[:, h * LANES:(h + 1) * LANES]
            ko_ref[:, b + LANES:b + 2 * LANES] = kr

    W = H * MLA_DK
    return pl.pallas_call(
        body, name="mla_rope_fwd",
        out_shape=(jax.ShapeDtypeStruct((L, W), _MXU), jax.ShapeDtypeStruct((L, W), _MXU)),
        grid=(L // ROW_TILE,),
        in_specs=[_rows(W), _rows(kv.shape[1]), pl.BlockSpec((ROW_TILE, LANES), lambda i: (i, _KR0 // LANES)),
                  _rows(LANES), _rows(LANES), _rows(LANES)],
        out_specs=(_rows(W), _rows(W)),
        compiler_params=_params(("parallel",)),
    )(qf, kv, proj, cos, sa, sb)


def _mla_rope_bwd(dq_full, dk_full, cos, sa, sb):
    L = dq_full.shape[0]
    H = MLA_HEADS
    W = H * MLA_DK

    def body(dq_ref, dk_ref, c_ref, sa_ref, sb_ref, dqf_ref, dkn_ref, dkr_ref):
        cos, sa, sb = c_ref[...], sa_ref[...], sb_ref[...]
        lane = lax.broadcasted_iota(jnp.int32, (ROW_TILE, LANES), 1)
        live = lane < MLA_ROPE
        krs = jnp.zeros((ROW_TILE, LANES), F32)
        for h in range(H):
            b = h * MLA_DK
            dqf_ref[:, b:b + LANES] = dq_ref[:, b:b + LANES].astype(_MXU)
            dqr = _rot_tile(dq_ref[:, b + LANES:b + 2 * LANES], cos, sa, sb)
            dqf_ref[:, b + LANES:b + 2 * LANES] = jnp.where(live, dqr, 0.0).astype(_MXU)
            dkn_ref[:, h * LANES:(h + 1) * LANES] = dk_ref[:, b:b + LANES].astype(_MXU)
            krs = krs + dk_ref[:, b + LANES:b + 2 * LANES]
        dkr_ref[...] = jnp.where(live, _rot_tile(krs, cos, sa, sb), 0.0).astype(_MXU)

    return pl.pallas_call(
        body, name="mla_rope_bwd",
        out_shape=(jax.ShapeDtypeStruct((L, W), _MXU), jax.ShapeDtypeStruct((L, H * LANES), _MXU),
                   jax.ShapeDtypeStruct((L, LANES), _MXU)),
        grid=(L // ROW_TILE,),
        in_specs=[_rows(W), _rows(W), _rows(LANES), _rows(LANES), _rows(LANES)],
        out_specs=(_rows(W), _rows(H * LANES), _rows(LANES)),
        compiler_params=_params(("parallel",)),
    )(dq_full, dk_full, cos, sa, sb)


_RET_QKW = RET_HEADS * RET_QK


def _ret_rope_fwd(qkz, cos, sin, pad):
    L = qkz.shape[0]
    hh = RET_QK // 2
    kscale = RET_QK ** -0.5

    def body(p_ref, c_ref, s_ref, q_ref, k_ref):
        cos, sin = c_ref[...], s_ref[...]
        row = pl.program_id(0) * ROW_TILE + lax.broadcasted_iota(jnp.int32, (ROW_TILE, hh), 0)
        keep = row >= pad
        for h in range(RET_HEADS):
            b = h * RET_QK
            t1, t2 = p_ref[:, b:b + hh], p_ref[:, b + hh:b + 2 * hh]
            q_ref[:, b:b + hh] = (t1 * cos - t2 * sin).astype(_MXU)
            q_ref[:, b + hh:b + 2 * hh] = (t2 * cos + t1 * sin).astype(_MXU)
            t1, t2 = p_ref[:, _RET_QKW + b:_RET_QKW + b + hh], p_ref[:, _RET_QKW + b + hh:_RET_QKW + b + 2 * hh]
            k_ref[:, b:b + hh] = jnp.where(keep, (t1 * cos - t2 * sin) * kscale, 0.0)
            k_ref[:, b + hh:b + 2 * hh] = jnp.where(keep, (t2 * cos + t1 * sin) * kscale, 0.0)

    return pl.pallas_call(
        body, name="ret_rope_fwd",
        out_shape=(jax.ShapeDtypeStruct((L, _RET_QKW), _MXU), jax.ShapeDtypeStruct((L, _RET_QKW), F32)),
        grid=(L // ROW_TILE,),
        in_specs=[_rows(2 * _RET_QKW), _rows(hh), _rows(hh)],
        out_specs=(_rows(_RET_QKW), _rows(_RET_QKW)),
        compiler_params=_params(("parallel",)),
    )(qkz, cos, sin)


def _ret_rope_bwd(dqr, dkr, cos, sin, pad):
    L = dqr.shape[0]
    hh = RET_QK // 2
    kscale = RET_QK ** -0.5

    def body(dq_ref, dk_ref, c_ref, s_ref, oq_ref, ok_ref):
        cos, sin = c_ref[...], s_ref[...]
        row = pl.program_id(0) * ROW_TILE + lax.broadcasted_iota(jnp.int32, (ROW_TILE, hh), 0)
        keep = row >= pad
        for h in range(RET_HEADS):
            b = h * RET_QK
            d1, d2 = dq_ref[:, b:b + hh], dq_ref[:, b + hh:b + 2 * hh]
            oq_ref[:, b:b + hh] = (d1 * cos + d2 * sin).astype(_MXU)
            oq_ref[:, b + hh:b + 2 * hh] = (d2 * cos - d1 * sin).astype(_MXU)
            d1, d2 = dk_ref[:, b:b + hh], dk_ref[:, b + hh:b + 2 * hh]
            ok_ref[:, b:b + hh] = jnp.where(keep, (d1 * cos + d2 * sin) * kscale, 0.0).astype(_MXU)
            ok_ref[:, b + hh:b + 2 * hh] = jnp.where(keep, (d2 * cos - d1 * sin) * kscale, 0.0).astype(_MXU)

    return pl.pallas_call(
        body, name="ret_rope_bwd",
        out_shape=(jax.ShapeDtypeStruct((L, _RET_QKW), _MXU), jax.ShapeDtypeStruct((L, _RET_QKW), _MXU)),
        grid=(L // ROW_TILE,),
        in_specs=[_rows(_RET_QKW), _rows(_RET_QKW), _rows(hh), _rows(hh)],
        out_specs=(_rows(_RET_QKW), _rows(_RET_QKW)),
        compiler_params=_params(("parallel",)),
    )(dqr, dkr, cos, sin)


def _ret_rows(L):
    return next(r * ROW_TILE for r in (3, 2, 1) if L % (r * ROW_TILE) == 0)


def _ret_decays():
    lg = np.log1p(-np.exp2(-5.0 - np.arange(RET_HEADS, dtype=np.float32))).astype(np.float32)
    i = np.arange(CHUNK, dtype=np.float32)
    rel = i[:, None] - i[None, :]
    dm = np.where(rel[None] >= 0, np.exp(rel[None] * lg[:, None, None]), 0.0).astype(np.float32)
    qd = np.exp((i[None, :] + 1.0) * lg[:, None]).astype(np.float32)[:, :, None]
    kd = np.exp((CHUNK - 1.0 - i)[None, :] * lg[:, None]).astype(np.float32)[:, :, None]
    cd = np.broadcast_to(np.exp(CHUNK * lg).astype(np.float32)[:, None, None], (RET_HEADS, 1, RET_V))
    return jnp.asarray(dm), jnp.asarray(qd), jnp.asarray(kd), jnp.asarray(np.ascontiguousarray(cd))


def _ret_fwd(qr, kr, v):
    L = qr.shape[0]
    nc = L // CHUNK
    RB = _ret_rows(L)
    G = RB // CHUNK
    dm, qd, kd, cd = _ret_decays()
    dn_nt = (((1,), (1,)), ((), ()))
    dn_tn = (((0,), (0,)), ((), ()))

    def body(q_ref, k_ref, v_ref, dm_ref, qd_ref, kd_ref, cd_ref, o_ref, st_ref, s_ref):
        @pl.when(pl.program_id(1) == 0)
        def _():
            s_ref[...] = jnp.zeros_like(s_ref)

        state = s_ref[...]
        for g in range(G):
            rows = slice(g * CHUNK, (g + 1) * CHUNK)
            q, k, vv = q_ref[rows, :], k_ref[rows, :], v_ref[rows, :]
            sb = state.astype(_MXU)
            st_ref[g] = sb
            s = lax.dot_general(q, k.astype(_MXU), dn_nt, preferred_element_type=F32) * dm_ref[...]
            intra = jnp.dot(s.astype(_MXU), vv, preferred_element_type=F32)
            cross = jnp.dot(q, sb, preferred_element_type=F32) * qd_ref[...]
            o_ref[rows, :] = intra + cross
            kdm = (k * kd_ref[...]).astype(_MXU)
            state = cd_ref[...] * state + lax.dot_general(kdm, vv, dn_tn, preferred_element_type=F32)
        s_ref[...] = state

    return pl.pallas_call(
        body, name="ret_fwd",
        out_shape=(jax.ShapeDtypeStruct((L, RET_HEADS * RET_V), F32),
                   jax.ShapeDtypeStruct((RET_HEADS, nc, RET_QK, RET_V), _MXU)),
        grid=(RET_HEADS, nc // G),
        in_specs=[pl.BlockSpec((RB, RET_QK), lambda h, c: (c, h)),
                  pl.BlockSpec((RB, RET_QK), lambda h, c: (c, h)),
                  pl.BlockSpec((RB, RET_V), lambda h, c: (c, h)),
                  pl.BlockSpec((None, CHUNK, CHUNK), lambda h, c: (h, 0, 0)),
                  pl.BlockSpec((None, CHUNK, 1), lambda h, c: (h, 0, 0)),
                  pl.BlockSpec((None, CHUNK, 1), lambda h, c: (h, 0, 0)),
                  pl.BlockSpec((None, 1, RET_V), lambda h, c: (h, 0, 0))],
        out_specs=(pl.BlockSpec((RB, RET_V), lambda h, c: (c, h)),
                   pl.BlockSpec((None, G, RET_QK, RET_V), lambda h, c: (h, c, 0, 0))),
        scratch_shapes=[pltpu.VMEM((RET_QK, RET_V), F32)],
        compiler_params=_params(("parallel", "arbitrary")),
    )(qr, kr, v, dm, qd, kd, cd)


def _ret_bwd(qr, kr, v, do, states):
    L = qr.shape[0]
    nc = L // CHUNK
    RB = _ret_rows(L)
    G = RB // CHUNK
    nt = nc // G
    dm, qd, kd, cd = _ret_decays()
    dn_nt = (((1,), (1,)), ((), ()))
    dn_tn = (((0,), (0,)), ((), ()))

    def body(q_ref, k_ref, v_ref, do_ref, st_ref, dm_ref, qd_ref, kd_ref, cd_ref,
             dq_ref, dk_ref, dv_ref, ds_ref):
        @pl.when(pl.program_id(1) == 0)
        def _():
            ds_ref[...] = jnp.zeros_like(ds_ref)

        dstate = ds_ref[...]
        dmat = dm_ref[...]
        for g in reversed(range(G)):
            rows = slice(g * CHUNK, (g + 1) * CHUNK)
            q, k, vv, do = q_ref[rows, :], k_ref[rows, :], v_ref[rows, :], do_ref[rows, :]
            kb = k.astype(_MXU)
            dob = do.astype(_MXU)
            doq = (do * qd_ref[...]).astype(_MXU)
            dsb = dstate.astype(_MXU)
            s = (lax.dot_general(q, kb, dn_nt, preferred_element_type=F32) * dmat).astype(_MXU)
            draw = (lax.dot_general(dob, vv, dn_nt, preferred_element_type=F32) * dmat).astype(_MXU)
            kdm = (k * kd_ref[...]).astype(_MXU)
            dv = (lax.dot_general(s, dob, dn_tn, preferred_element_type=F32)
                  + jnp.dot(kdm, dsb, preferred_element_type=F32))
            dq = (jnp.dot(draw, kb, preferred_element_type=F32)
                  + lax.dot_general(doq, st_ref[g], dn_nt, preferred_element_type=F32))
            dk = (lax.dot_general(draw, q, dn_tn, preferred_element_type=F32)
                  + lax.dot_general(vv, dsb, dn_nt, preferred_element_type=F32) * kd_ref[...])
            dq_ref[rows, :] = dq
            dk_ref[rows, :] = dk
            dv_ref[rows, :] = dv.astype(_MXU)
            dstate = cd_ref[...] * dstate + lax.dot_general(q, doq, dn_tn, preferred_element_type=F32)
        ds_ref[...] = dstate

    rev = lambda h, c: (nt - 1 - c, h)
    return pl.pallas_call(
        body, name="ret_bwd",
        out_shape=(jax.ShapeDtypeStruct((L, _RET_QKW), F32), jax.ShapeDtypeStruct((L, _RET_QKW), F32),
                   jax.ShapeDtypeStruct((L, RET_HEADS * RET_V), _MXU)),
        grid=(RET_HEADS, nt),
        in_specs=[pl.BlockSpec((RB, RET_QK), rev),
                  pl.BlockSpec((RB, RET_QK), rev),
                  pl.BlockSpec((RB, RET_V), rev),
                  pl.BlockSpec((RB, RET_V), rev),
                  pl.BlockSpec((None, G, RET_QK, RET_V), lambda h, c: (h, nt - 1 - c, 0, 0)),
                  pl.BlockSpec((None, CHUNK, CHUNK), lambda h, c: (h, 0, 0)),
                  pl.BlockSpec((None, CHUNK, 1), lambda h, c: (h, 0, 0)),
                  pl.BlockSpec((None, CHUNK, 1), lambda h, c: (h, 0, 0)),
                  pl.BlockSpec((None, 1, RET_V), lambda h, c: (h, 0, 0))],
        out_specs=(pl.BlockSpec((RB, RET_QK), rev), pl.BlockSpec((RB, RET_QK), rev),
                   pl.BlockSpec((RB, RET_V), rev)),
        scratch_shapes=[pltpu.VMEM((RET_QK, RET_V), F32)],
        compiler_params=_params(("parallel", "arbitrary")),
    )(qr, kr, v, do, states, dm, qd, kd, cd)


def _gn(o, g):
    mu = jnp.mean(o, axis=-1, keepdims=True)
    oc = o - mu
    var = jnp.mean(oc * oc, axis=-1, keepdims=True)
    return oc * lax.rsqrt(var + NORM_EPS) * g


def _ret_gate_fwd(o, qkz, gn_g):
    L, W = o.shape
    zblk = qkz.shape[1] // W - 1

    def body(o_ref, z_ref, g_ref, y_ref):
        for h in range(RET_HEADS):
            sl = slice(h * RET_V, (h + 1) * RET_V)
            y_ref[:, sl] = (_gn(o_ref[:, sl], g_ref[:, sl]) * _silu(z_ref[:, sl])).astype(_MXU)

    return pl.pallas_call(
        body, name="ret_gate_fwd",
        out_shape=jax.ShapeDtypeStruct((L, W), _MXU),
        grid=(L // ROW_TILE,),
        in_specs=[_rows(W), pl.BlockSpec((ROW_TILE, W), lambda i: (i, zblk)), _whole((1, W))],
        out_specs=_rows(W),
        compiler_params=_params(("parallel",)),
    )(o, qkz, gn_g)


def _ret_gate_bwd(dzl, w_out, o, qkz, gn_g):
    L, W = o.shape
    zblk = qkz.shape[1] // W - 1

    def body(c_ref, w_ref, o_ref, z_ref, g_ref, do_ref, dz_ref, dg_ref):
        @pl.when(pl.program_id(0) == 0)
        def _():
            dg_ref[...] = jnp.zeros_like(dg_ref)

        dy_all = lax.dot_general(c_ref[...].astype(_MXU), w_ref[...].astype(_MXU), (((1,), (1,)), ((), ())),
                                 preferred_element_type=F32)
        for h in range(RET_HEADS):
            sl = slice(h * RET_V, (h + 1) * RET_V)
            z = z_ref[:, sl]
            o = o_ref[:, sl]
            g = g_ref[:, sl]
            dy = dy_all[:, sl]
            mu = jnp.mean(o, axis=-1, keepdims=True)
            oc = o - mu
            var = jnp.mean(oc * oc, axis=-1, keepdims=True)
            r = lax.rsqrt(var + NORM_EPS)
            xh = oc * r
            dn = dy * _silu(z)
            dz_ref[:, sl] = (dy * (xh * g) * _dsilu(z)).astype(_MXU)
            dg_ref[:, sl] += jnp.sum(dn * xh, axis=0, keepdims=True)
            dxh = dn * g
            do_ref[:, sl] = r * (dxh - jnp.mean(dxh, axis=-1, keepdims=True)
                                 - xh * jnp.mean(dxh * xh, axis=-1, keepdims=True))

    return pl.pallas_call(
        body, name="ret_gate_bwd",
        out_shape=(jax.ShapeDtypeStruct((L, W), F32), jax.ShapeDtypeStruct((L, W), _MXU),
                   jax.ShapeDtypeStruct((1, W), F32)),
        grid=(L // ROW_TILE,),
        in_specs=[_rows(D_MODEL), _whole((W, D_MODEL)), _rows(W), pl.BlockSpec((ROW_TILE, W), lambda i: (i, zblk)),
                  _whole((1, W))],
        out_specs=(_rows(W), _rows(W), _whole((1, W))),
        compiler_params=_params(("arbitrary",)),
    )(dzl, w_out, o, qkz, gn_g)


def _dead_rows(pad):
    return pad // LANES * LANES


def _heads_t(a):
    return a[:, :FOX_HEADS].T


def _fox_fwd(h, w, lnw, pad, tag, side=None, last=False):
    w_in, b_f, w_out = w
    L = h.shape[0]
    W = D_MODEL
    w_qkv = w_in[:, :3 * W]
    w_zf = jnp.concatenate([w_in[:, 3 * W:4 * W], jnp.pad(w_in[:, 4 * W:], ((0, 0), (0, LANES - FOX_HEADS)))], axis=1)
    bias = jnp.pad(b_f, (0, LANES - FOX_HEADS))[None, :]
    qkv = _matmul(h, w_qkv, "nn", _MXU, f"{tag}_qkv")
    zf = _matmul(h, w_zf, "nn", F32, f"{tag}_zf")
    cum = _fox_cum(zf, bias, pad, f"{tag}_cum")
    ab = _bias_terms(_heads_t(cum), pad)
    nb = W // LANES
    o, lse, y, side_out = _attn_fwd(qkv, 0, qkv, nb, qkv, 2 * nb, zf, ab, FOX_HEADS, LANES, LANES, LANES ** -0.5,
                                    f"{tag}_attn_fwd", _dead_rows(pad), side)
    zl, hn = _out_ln(y, w_out, h, lnw[0], lnw[1], f"{tag}_out_ln", last)
    return hn, dict(h=h, qkv=qkv, zf=zf, ab=ab, o=o, lse=lse, y=y, zl=zl,
                    w_all=jnp.concatenate([w_qkv, w_zf], axis=1), w_out=w_out, bias=bias), side_out


def _fox_bwd(s, dzl, pad, tag, side=None, send_own=False):
    W = D_MODEL
    nb = W // LANES
    d_wout = _matmul(s["y"], dzl, "tn", F32, f"{tag}_dwout")
    if send_own:
        side = (list(side[0]) + [_split_layer("fox_w_out", d_wout)], True)
    do, delta, dzg = _gate_bwd(dzl, s["w_out"], s["o"], s["zf"], FOX_HEADS, f"{tag}_gate_bwd")
    qkv = s["qkv"]
    dq, dk, dv, dck, dcq, side_out = _attn_bwd(qkv, 0, qkv, nb, qkv, 2 * nb, do, s["ab"], s["lse"], delta,
                                               FOX_HEADS, LANES, LANES, LANES ** -0.5, f"{tag}_attn_bwd",
                                               _dead_rows(pad), side)
    dcum = jnp.pad((dck + dcq).reshape(FOX_HEADS, -1).T, ((0, 0), (0, LANES - FOX_HEADS)))
    df, dbias = _fox_cum_bwd(dcum, s["zf"], s["bias"], pad, f"{tag}_cum_bwd")
    dproj = jnp.concatenate([dq.astype(_MXU), dk.astype(_MXU), dv.astype(_MXU), dzg, df], axis=1)
    d_wall = _matmul(s["h"], dproj, "tn", F32, f"{tag}_dwin")
    d_win = jnp.concatenate([d_wall[:, :4 * W], d_wall[:, 4 * W:4 * W + FOX_HEADS]], axis=1)
    if send_own:
        dh, win_parts = _matmul(dproj, s["w_all"], "nt", F32, f"{tag}_dh", ([_split_layer("fox_w_in", d_win)], True))
        side_out = list(side_out) + list(win_parts)
    else:
        dh = _matmul(dproj, s["w_all"], "nt", F32, f"{tag}_dh")
    return dh, (d_win, dbias[0, :FOX_HEADS], d_wout), side_out


def _mla_tables(L, pad):
    pos = (jnp.arange(L) - pad).astype(F32)
    inv = ROPE_BASE ** (-jnp.arange(0, MLA_ROPE, 2, dtype=F32) / MLA_ROPE)
    ang = pos[:, None] * inv[None, :]
    c, s = jnp.cos(ang), jnp.sin(ang)
    z32, z64 = jnp.zeros_like(s), jnp.zeros((L, LANES - MLA_ROPE), F32)
    cos = jnp.concatenate([c, c, jnp.ones_like(z64)], axis=1)
    sa = jnp.concatenate([-s, z32, z64], axis=1)
    sb = jnp.concatenate([z32, s, z64], axis=1)
    return cos, sa, sb


def _mla_weights(w_in, w_uq, w_ukv):
    a, b = MLA_Q_LORA, MLA_Q_LORA + MLA_KV_LORA
    w_zc = jnp.concatenate([w_in[:, b + MLA_ROPE:], w_in[:, :b + MLA_ROPE],
                            jnp.zeros((D_MODEL, LANES - MLA_ROPE), w_in.dtype)], axis=1)
    uq = w_uq.reshape(MLA_Q_LORA, MLA_HEADS, MLA_NOPE + MLA_ROPE)
    uq = jnp.pad(uq, ((0, 0), (0, 0), (0, MLA_DK - MLA_NOPE - MLA_ROPE))).reshape(MLA_Q_LORA, MLA_HEADS * MLA_DK)
    ukv = w_ukv.reshape(MLA_KV_LORA, MLA_HEADS, MLA_NOPE + MLA_V)
    ukv = jnp.concatenate([ukv[:, :, :MLA_NOPE].reshape(MLA_KV_LORA, -1), ukv[:, :, MLA_NOPE:].reshape(MLA_KV_LORA, -1)], axis=1)
    return w_zc, uq, ukv


def _mla_fwd(h, w, lnw, pad):
    w_in, gq, gkv, w_uq, w_ukv, w_out = w
    L = h.shape[0]
    w_zc, uq, ukv = _mla_weights(w_in, w_uq, w_ukv)
    cos, sa, sb = _mla_tables(L, pad)
    proj = _matmul(h, w_zc, "nn", F32, "mla_proj")
    cqn, ckvn = _mla_norm_fwd(proj, gq[None, :], gkv[None, :])
    qf = _matmul(cqn, uq, "nn", F32, "mla_uq")
    kv = _matmul(ckvn, ukv, "nn", _MXU, "mla_ukv")
    q_full, k_full = _mla_rope_fwd(qf, kv, proj, cos, sa, sb)
    ab = _bias_terms(jnp.zeros((MLA_HEADS, L), F32), pad)
    nb = MLA_HEADS * MLA_NOPE // LANES
    o, lse, y, _ = _attn_fwd(q_full, 0, k_full, 0, kv, nb, proj, ab, MLA_HEADS, MLA_DK, MLA_V,
                             (MLA_NOPE + MLA_ROPE) ** -0.5, "mla_attn_fwd", _dead_rows(pad))
    zl, hn = _out_ln(y, w_out, h, lnw[0], lnw[1], "mla_out_ln")
    return hn, dict(h=h, proj=proj, cqn=cqn, ckvn=ckvn, kv=kv, q_full=q_full, k_full=k_full, ab=ab,
                    dead=_dead_rows(pad),
                    o=o, lse=lse, y=y, zl=zl, w_zc=w_zc, uq=uq, ukv=ukv, w_out=w_out, gq=gq, gkv=gkv,
                    tabs=(cos, sa, sb))


def _mla_bwd(s, dzl):
    H = MLA_HEADS
    d_wout = _matmul(s["y"], dzl, "tn", F32, "mla_dwout")
    do, delta, dzg = _gate_bwd(dzl, s["w_out"], s["o"], s["proj"], H, "mla_gate_bwd")
    nb = H * MLA_NOPE // LANES
    dq_full, dk_full, dv, _, _, _ = _attn_bwd(s["q_full"], 0, s["k_full"], 0, s["kv"], nb, do, s["ab"],
                                              s["lse"], delta,
                                              H, MLA_DK, MLA_V, (MLA_NOPE + MLA_ROPE) ** -0.5, "mla_attn_bwd",
                                              s["dead"], bias_grads=False)
    cos, sa, sb = s["tabs"]
    dqf, dkn, dkr = _mla_rope_bwd(dq_full, dk_full, cos, -sa, -sb)
    d_uq = _matmul(s["cqn"], dqf, "tn", F32, "mla_duq")
    dcqn = _matmul(dqf, s["uq"], "nt", F32, "mla_dcqn")
    dkv = jnp.concatenate([dkn, dv.astype(_MXU)], axis=1)
    d_ukv = _matmul(s["ckvn"], dkv, "tn", F32, "mla_dukv")
    dckvn = _matmul(dkv, s["ukv"], "nt", F32, "mla_dckvn")
    dcq, dckv, dgq, dgkv = _mla_norm_bwd(s["proj"], s["gq"][None, :], s["gkv"][None, :], dcqn, dckvn)
    dproj = jnp.concatenate([dzg, dcq, dckv, dkr], axis=1)
    d_wzc = _matmul(s["h"], dproj, "tn", F32, "mla_dwin")
    dh = _matmul(dproj, s["w_zc"], "nt", F32, "mla_dh")
    W = D_MODEL
    d_win = jnp.concatenate([d_wzc[:, W:W + MLA_Q_LORA + MLA_KV_LORA + MLA_ROPE], d_wzc[:, :W]], axis=1)
    d_wuq = d_uq.reshape(MLA_Q_LORA, H, MLA_DK)[:, :, :MLA_NOPE + MLA_ROPE].reshape(MLA_Q_LORA, -1)
    hk = H * MLA_NOPE
    d_wukv = jnp.concatenate([d_ukv[:, :hk].reshape(MLA_KV_LORA, H, MLA_NOPE),
                              d_ukv[:, hk:].reshape(MLA_KV_LORA, H, MLA_V)], axis=2).reshape(MLA_KV_LORA, -1)
    return dh, (d_win, dgq[0], dgkv[0], d_wuq, d_wukv, d_wout)


def _ret_tables(L, pad):
    pos = (jnp.arange(L) - pad).astype(F32)
    inv = 1.0 / (ROPE_BASE ** jnp.linspace(0.0, 1.0, RET_QK // 2, dtype=F32))
    ang = pos[:, None] * inv[None, :]
    return jnp.cos(ang), jnp.sin(ang)


def _ret_layer_fwd(h, w, lnw, pad):
    w_in, gn_g, w_out = w
    L = h.shape[0]
    q1 = 2 * _RET_QKW
    v1 = q1 + RET_HEADS * RET_V
    w_qkz = jnp.concatenate([w_in[:, :q1], w_in[:, v1:]], axis=1)
    cos, sin = _ret_tables(L, pad)
    qkz = _matmul(h, w_qkz, "nn", F32, "ret_qkz")
    v = _matmul(h, w_in[:, q1:v1], "nn", _MXU, "ret_v")
    qr, kr = _ret_rope_fwd(qkz, cos, sin, pad)
    o, states = _ret_fwd(qr, kr, v)
    y = _ret_gate_fwd(o, qkz, gn_g[None, :])
    zl, hn = _out_ln(y, w_out, h, lnw[0], lnw[1], "ret_out_ln")
    return hn, dict(h=h, qkz=qkz, v=v, qr=qr, kr=kr, o=o, states=states, y=y, zl=zl, w_in=w_in, w_out=w_out,
                    gn_g=gn_g, tabs=(cos, sin))


def _ret_layer_bwd(s, dzl, pad):
    d_wout = _matmul(s["y"], dzl, "tn", F32, "ret_dwout")
    do, dzg, dgn = _ret_gate_bwd(dzl, s["w_out"], s["o"], s["qkz"], s["gn_g"][None, :])
    dqr, dkr, dv = _ret_bwd(s["qr"], s["kr"], s["v"], do, s["states"])
    cos, sin = s["tabs"]
    dq, dk = _ret_rope_bwd(dqr, dkr, cos, sin, pad)
    dproj = jnp.concatenate([dq, dk, dv, dzg], axis=1)
    d_win = _matmul(s["h"], dproj, "tn", F32, "ret_dwin")
    dh = _matmul(dproj, s["w_in"], "nt", F32, "ret_dh")
    return dh, (d_win, dgn[0], d_wout)


def _local_step(x, target, meta, fox0_w, late_shards, small, ln_g, ln_b):
    n_real = x.shape[0]
    first = -(-N_META // ROW_TILE)
    pad = first * ROW_TILE - N_META
    h0 = jnp.concatenate([jnp.zeros((pad, D_MODEL), F32), meta, x], axis=0)
    lnw = [(ln_g[i][None, :], ln_b[i][None, :]) for i in range(4)]
    b_f = small["fox_b_f"]

    h1, s0, gathered = _fox_fwd(h0, (fox0_w[0], b_f[0], fox0_w[1]), lnw[0], pad, "fox0", (late_shards, False))
    w = {key: _merge_layer(key[0], g).astype(_MXU) for key, g in zip(_LATE, gathered)}
    h2, s1 = _mla_fwd(h1, (w["mla_w_in", 0], small["mla_q_norm"][0], small["mla_kv_norm"][0], w["mla_w_uq", 0],
                           w["mla_w_ukv", 0], w["mla_w_out", 0]), lnw[1], pad)
    h3, s2 = _ret_layer_fwd(h2, (w["ret_w_in", 0], small["ret_gn_g"][0], w["ret_w_out", 0]), lnw[2], pad)
    _, s3, _ = _fox_fwd(h3, (w["fox_w_in", 1], b_f[1], w["fox_w_out", 1]), lnw[3], pad, "fox1", last=True)

    dz3, dg3, db3, loss = _loss_ln_bwd(s3["zl"], lnw[3][0], lnw[3][1], target, first)
    dm3, gf1, _ = _fox_bwd(s3, dz3, pad, "fox1")
    dz2, dg2, db2 = _ln_bwd(s2["zl"], lnw[2][0], dz3, dm3, "ln2_bwd")
    dm2, gr = _ret_layer_bwd(s2, dz2, pad)
    dz1, dg1, db1 = _ln_bwd(s1["zl"], lnw[1][0], dz2, dm2, "ln1_bwd")
    dm1, gm = _mla_bwd(s1, dz1)
    dz0, dg0, db0 = _ln_bwd(s0["zl"], lnw[0][0], dz1, dm1, "ln0_bwd")
    late = {("fox_w_in", 1): gf1[0], ("fox_w_out", 1): gf1[2], ("mla_w_in", 0): gm[0], ("mla_w_uq", 0): gm[3],
            ("mla_w_ukv", 0): gm[4], ("mla_w_out", 0): gm[5], ("ret_w_in", 0): gr[0], ("ret_w_out", 0): gr[2]}
    dm0, gf0, sent = _fox_bwd(s0, dz0, pad, "fox0", ([_split_layer(k[0], late[k]) for k in _LATE], True), True)
    parts = dict(zip(_LATE + [("fox_w_out", 0), ("fox_w_in", 0)], sent))
    grad_x, grad_meta = _input_grads(dz0, dm0, first, n_real)

    small_grads = dict(
        meta=grad_meta, fox_b_f=jnp.stack([gf0[1], gf1[1]]), mla_q_norm=gm[1][None], mla_kv_norm=gm[2][None],
        ret_gn_g=gr[1][None], ln_g=jnp.concatenate([dg0, dg1, dg2, dg3], axis=0),
        ln_b=jnp.concatenate([db0, db1, db2, db3], axis=0))
    return loss, grad_x, parts, small_grads


def _exchange(tensors, scatter, name):
    nt = len(tensors)

    def body(*refs):
        copies = _exchange_copies(refs[:nt], refs[nt:2 * nt], *refs[2 * nt:], scatter)
        for cp in copies:
            cp.start()
        for cp in copies:
            cp.wait()

    hbm = pl.BlockSpec(memory_space=pl.ANY)
    out_shape, sems = _exchange_shapes(tensors, scatter)
    return pl.pallas_call(
        body, name=name, out_shape=out_shape, in_specs=[hbm] * nt, out_specs=(hbm,) * nt, scratch_shapes=sems,
    )(*tensors)


def _adamw(parts, w, m, v, name):
    R, C = w.shape
    tr = next(t for t in range(min(R, 256), 0, -1) if R % t == 0 and (t % 8 == 0 or t == R))

    def body(p_ref, w_ref, m_ref, v_ref, g_ref, d_ref, mo_ref, vo_ref):
        g = p_ref[0]
        for d in range(1, N_DEV):
            g = g + p_ref[d]
        mn = ADAM_B1 * m_ref[...] + (1.0 - ADAM_B1) * g
        vn = ADAM_B2 * v_ref[...] + (1.0 - ADAM_B2) * (g * g)
        m_hat = mn / (1.0 - ADAM_B1 ** ADAM_STEP)
        v_hat = vn / (1.0 - ADAM_B2 ** ADAM_STEP)
        g_ref[...] = g
        d_ref[...] = -ADAM_LR * (m_hat / (jnp.sqrt(v_hat) + ADAM_EPS) + ADAM_WD * w_ref[...])
        mo_ref[...] = mn
        vo_ref[...] = vn

    blk = pl.BlockSpec((tr, C), lambda i: (i, 0))
    sd = jax.ShapeDtypeStruct((R, C), F32)
    return pl.pallas_call(
        body, name=name,
        out_shape=(sd, sd, sd, sd),
        grid=(R // tr,),
        in_specs=[pl.BlockSpec((N_DEV, tr, C), lambda i: (0, i, 0)), blk, blk, blk],
        out_specs=(blk, blk, blk, blk),
        compiler_params=_params(("parallel",)),
    )(parts, w, m, v)


_SPECS = dict(
    meta=((16, 1024), 1),
    fox_w_in=((2, 1024, 4104), 2), fox_b_f=((2, 8), None), fox_w_out=((2, 1024, 1024), 1),
    mla_w_in=((1, 1024, 1728), 2), mla_q_norm=((1, 384), None), mla_kv_norm=((1, 256), None),
    mla_w_uq=((1, 384, 1536), 2), mla_w_ukv=((1, 256, 2048), 2), mla_w_out=((1, 1024, 1024), 1),
    ret_w_in=((1, 1024, 6144), 2), ret_gn_g=((1, 2048), 1), ret_w_out=((1, 2048, 1024), 1),
    ln_g=((4, 1024), None), ln_b=((4, 1024), None),
)
_NAMES = list(_SPECS)
_BIG = ["fox_w_in", "fox_w_out", "mla_w_in", "mla_w_uq", "mla_w_ukv", "mla_w_out", "ret_w_in", "ret_w_out"]
_SMALL = [n for n in _NAMES if n not in _BIG]
_SMALL_GATHERED = ["meta", "ret_gn_g"]
_PACK_ROWS = 8


def _shard_shape(name):
    shape, ax = _SPECS[name]
    if ax is None:
        return shape
    return tuple(s // N_DEV if i == ax else s for i, s in enumerate(shape))


_EARLY = [("fox_w_in", 0), ("fox_w_out", 0)]
_LATE = [("fox_w_in", 1), ("fox_w_out", 1), ("mla_w_in", 0), ("mla_w_uq", 0), ("mla_w_ukv", 0), ("mla_w_out", 0),
         ("ret_w_in", 0), ("ret_w_out", 0)]


def _split_at(full, ax):
    shape = full.shape
    parts = full.reshape(shape[:ax] + (N_DEV, shape[ax] // N_DEV) + shape[ax + 1:])
    return jnp.moveaxis(parts, ax, 0)


def _merge_at(stacked, ax):
    full = jnp.moveaxis(stacked, 0, ax)
    return full.reshape(full.shape[:ax] + (-1,) + full.shape[ax + 2:])


def _split(name, full):
    return _split_at(full, _SPECS[name][1])


def _merge(name, stacked):
    return _merge_at(stacked, _SPECS[name][1])


def _split_layer(name, full):
    return _split_at(full, _SPECS[name][1] - 1)


def _merge_layer(name, stacked):
    return _merge_at(stacked, _SPECS[name][1] - 1)


def _pad_rows(flat):
    n = flat.shape[-1]
    unit = _PACK_ROWS * LANES
    total = -(-n // unit) * unit
    flat = jnp.pad(flat, [(0, 0)] * (flat.ndim - 1) + [(0, total - n)])
    return flat.reshape(flat.shape[:-1] + (total // LANES, LANES))


def _pack_small(tree, names, tail):
    return _pad_rows(jnp.concatenate([tree[n].reshape(-1) for n in names] + [tail]))


def _unpack_small(buf, names):
    flat, out, at = buf.reshape(-1), {}, 0
    for n in names:
        shp = _shard_shape(n)
        out[n] = flat[at:at + math.prod(shp)].reshape(shp)
        at += math.prod(shp)
    return out, flat[at:]


def kernel(x, meta, fox_w_in, fox_b_f, fox_w_out, mla_w_in, mla_q_norm, mla_kv_norm, mla_w_uq, mla_w_ukv, mla_w_out, ret_w_in, ret_gn_g, ret_w_out, ln_g, ln_b, loss_target, m_meta, m_fox_w_in, m_fox_b_f, m_fox_w_out, m_mla_w_in, m_mla_q_norm, m_mla_kv_norm, m_mla_w_uq, m_mla_w_ukv, m_mla_w_out, m_ret_w_in, m_ret_gn_g, m_ret_w_out, m_ln_g, m_ln_b, v_meta, v_fox_w_in, v_fox_b_f, v_fox_w_out, v_mla_w_in, v_mla_q_norm, v_mla_kv_norm, v_mla_w_uq, v_mla_w_ukv, v_mla_w_out, v_ret_w_in, v_ret_gn_g, v_ret_w_out, v_ln_g, v_ln_b):
    w = dict(meta=meta, fox_w_in=fox_w_in, fox_b_f=fox_b_f, fox_w_out=fox_w_out, mla_w_in=mla_w_in,
             mla_q_norm=mla_q_norm, mla_kv_norm=mla_kv_norm, mla_w_uq=mla_w_uq, mla_w_ukv=mla_w_ukv,
             mla_w_out=mla_w_out, ret_w_in=ret_w_in, ret_gn_g=ret_gn_g, ret_w_out=ret_w_out, ln_g=ln_g, ln_b=ln_b)
    m = dict(meta=m_meta, fox_w_in=m_fox_w_in, fox_b_f=m_fox_b_f, fox_w_out=m_fox_w_out, mla_w_in=m_mla_w_in,
             mla_q_norm=m_mla_q_norm, mla_kv_norm=m_mla_kv_norm, mla_w_uq=m_mla_w_uq, mla_w_ukv=m_mla_w_ukv,
             mla_w_out=m_mla_w_out, ret_w_in=m_ret_w_in, ret_gn_g=m_ret_gn_g, ret_w_out=m_ret_w_out, ln_g=m_ln_g, ln_b=m_ln_b)
    v = dict(meta=v_meta, fox_w_in=v_fox_w_in, fox_b_f=v_fox_b_f, fox_w_out=v_fox_w_out, mla_w_in=v_mla_w_in,
             mla_q_norm=v_mla_q_norm, mla_kv_norm=v_mla_kv_norm, mla_w_uq=v_mla_w_uq, mla_w_ukv=v_mla_w_ukv,
             mla_w_out=v_mla_w_out, ret_w_in=v_ret_w_in, ret_gn_g=v_ret_gn_g, ret_w_out=v_ret_w_out, ln_g=v_ln_g, ln_b=v_ln_b)

    def shard16(key):
        return w[key[0]][key[1]].astype(jnp.bfloat16)

    none = jnp.zeros((0,), F32)
    gathered = _exchange([shard16(k) for k in _EARLY] + [_pack_small(w, _SMALL_GATHERED, none)],
                         False, "weights_all_gather")
    fox0_w = [_merge_layer(k[0], g).astype(_MXU) for k, g in zip(_EARLY, gathered)]
    small = {n: w[n] for n in _SMALL}
    pieces = [_unpack_small(gathered[-1][d], _SMALL_GATHERED)[0] for d in range(N_DEV)]
    for n in _SMALL_GATHERED:
        small[n] = _merge(n, jnp.stack([p[n] for p in pieces]))

    loss, grad_x, by_key, grads = _local_step(
        x[0], loss_target[0], small["meta"], fox0_w, [shard16(k) for k in _LATE], small, ln_g, ln_b)

    def small_rows(n):
        if _SPECS[n][1] is None:
            return jnp.broadcast_to(grads[n].reshape(1, -1), (N_DEV, grads[n].size))
        return _split(n, grads[n]).reshape(N_DEV, -1)

    small_out = _pad_rows(jnp.concatenate([small_rows(n) for n in _SMALL]
                                          + [jnp.broadcast_to(loss.reshape(1, 1), (N_DEV, 1))], axis=1))
    parts = _exchange([small_out], True, "grads_all_to_all")

    def rows2d(a):
        return a.reshape(-1, a.shape[-1])

    out = {}
    for n in _BIG:
        layers = [by_key[n, i] for i in range(_SPECS[n][0][0])]
        p = layers[0] if len(layers) == 1 else jnp.stack(layers, axis=1)
        res = _adamw(p.reshape(N_DEV, -1, p.shape[-1]), rows2d(w[n]), rows2d(m[n]), rows2d(v[n]), f"adamw_{n}")
        out[n] = [r.reshape(w[n].shape) for r in res]
    zero1 = jnp.zeros((1,), F32)
    res = _adamw(parts[-1], _pack_small(w, _SMALL, zero1), _pack_small(m, _SMALL, zero1),
                 _pack_small(v, _SMALL, zero1), "adamw_small")
    small_res = [_unpack_small(r, _SMALL) for r in res]
    for n in _SMALL:
        out[n] = [s[0][n] for s in small_res]
    total_loss = small_res[0][1][0]
    return (total_loss, grad_x[None], *[out[n][0] for n in _NAMES], *[out[n][1] for n in _NAMES],
            *[out[n][2] for n in _NAMES], *[out[n][3] for n in _NAMES])
```
